```python
import math
import jax, jax.numpy as jnp
from jax import lax
import numpy as np

D_MODEL = 1024
BATCH = 16
SEQ = 256
DEPTH = 2
DEC_BATCH = 2
DEC_SEQ = 1024
PAST_LEN = 256

GRID_W = 64
Q_BLOCK = 128
RET_CHUNK = 128
ROPE_BASE = 10000.0
EPS = 1e-6

MLA_HEADS = 8
MLA_NOPE = 64
MLA_ROPE = 32
MLA_V = 64
MLA_Q_LORA = 384
MLA_KV_LORA = 256
GQA_HEADS = 8
GQA_KV_HEADS = 2
GQA_HD = 64
RET_HEADS = 4
RET_DK = 64
RET_DV = 128
DIFF_HEADS = 4
DIFF_D = 64
DIFF_DV = 128
N_BRANCH = 4
BRANCH_W = 512
N_EXPERTS = 32
TOP_K = 4
N_GROUPS = 4
TOPK_GROUPS = 2
EXPERT_FF = 256
SHARED_FF = 256
ROUTE_SCALE = 2.5

IN_COLS = (MLA_Q_LORA + MLA_KV_LORA + MLA_ROPE
           + (GQA_HEADS + 2 * GQA_KV_HEADS) * GQA_HD
           + RET_HEADS * (2 * RET_DK + 2 * RET_DV)
           + DIFF_HEADS * (4 * DIFF_D + DIFF_DV)
           + N_BRANCH * D_MODEL)

kernel_name = "hybrid_diffusion_prefix_trunk_step"


def rms_norm(x, g):
    xf = x.astype(jnp.float32)
    y = xf * lax.rsqrt(jnp.mean(xf * xf, axis=-1, keepdims=True) + EPS)
    return (y * g.astype(jnp.float32)).astype(x.dtype)


def head_group_norm(o, g):
    mu = jnp.mean(o, axis=-1, keepdims=True)
    oc = o - mu
    y = oc * lax.rsqrt(jnp.mean(oc * oc, axis=-1, keepdims=True) + EPS)
    B, T = o.shape[:2]
    return y.reshape(B, T, -1) * g.astype(jnp.float32)


def _rope_1d(x, pos):
    half = x.shape[-1] // 2
    freq = ROPE_BASE ** (-jnp.arange(half, dtype=jnp.float32) / half)
    ang = pos.astype(jnp.float32)[:, None] * freq[None, :]
    shape = (1, pos.shape[0]) + (1,) * (x.ndim - 3) + (half,)
    cos = jnp.cos(ang).reshape(shape)
    sin = jnp.sin(ang).reshape(shape)
    xf = x.astype(jnp.float32)
    x1, x2 = xf[..., :half], xf[..., half:]
    return jnp.concatenate([x1 * cos - x2 * sin, x2 * cos + x1 * sin], axis=-1).astype(x.dtype)


def axial_rope(x, row, col):
    r = x.shape[-1] // 2
    return jnp.concatenate([_rope_1d(x[..., :r], row), _rope_1d(x[..., r:], col)], axis=-1)


def _attend_block(q, k, v, scale):
    s = jnp.einsum('bqkrd,bskd->bkrqs', q, k).astype(jnp.float32) * scale
    p = jax.nn.softmax(s, axis=-1).astype(v.dtype)
    return jnp.einsum('bkrqs,bske->bqkre', p, v)


def attend(q, k, v, scale):
    B, T = q.shape[:2]
    nb = T // Q_BLOCK
    qb = jnp.moveaxis(q.reshape((B, nb, Q_BLOCK) + q.shape[2:]), 1, 0)
    ob = lax.map(lambda qi: _attend_block(qi, k, v, scale), qb)
    return jnp.moveaxis(ob, 0, 1).reshape((B, T) + ob.shape[3:])


def retention_scan(q, k, v, log_gamma, s0):
    B, T, H, dk = q.shape
    dv = v.shape[-1]
    C = RET_CHUNK
    n = T // C
    f32 = jnp.float32
    idx = jnp.arange(C, dtype=f32)
    lg = log_gamma.astype(f32)[:, None]
    dist = idx[:, None] - idx[None, :]
    dmask = jnp.where(dist[None] >= 0, jnp.exp(lg[..., None] * jnp.maximum(dist, 0.0)[None]), 0.0)
    q_dec = jnp.exp(lg * (idx + 1.0)).T[None, :, :, None]
    k_dec = jnp.exp(lg * (C - 1.0 - idx)).T[None, :, :, None]
    chunk_dec = jnp.exp(lg[:, 0] * C)[None, :, None, None]
    qc = jnp.swapaxes(q.astype(f32).reshape(B, n, C, H, dk), 0, 1)
    kc = jnp.swapaxes(k.astype(f32).reshape(B, n, C, H, dk), 0, 1)
    vc = jnp.swapaxes(v.astype(f32).reshape(B, n, C, H, dv), 0, 1)

    def step(S, xs):
        qi, ki, vi = xs
        inner = jnp.einsum('bhij,bjhe->bihe', jnp.einsum('bihd,bjhd->bhij', qi, ki) * dmask[None], vi)
        cross = jnp.einsum('bihd,bhde->bihe', qi, S) * q_dec
        S_new = S * chunk_dec + jnp.einsum('bjhd,bjhe->bhde', ki * k_dec, vi)
        return S_new, inner + cross

    S, out = lax.scan(step, s0.astype(f32), (qc, kc, vc))
    return jnp.swapaxes(out, 0, 1).reshape(B, T, H, dv), S


def _split_cols(z):
    sizes = (MLA_Q_LORA, MLA_KV_LORA, MLA_ROPE,
             GQA_HEADS * GQA_HD, GQA_KV_HEADS * GQA_HD, GQA_KV_HEADS * GQA_HD,
             RET_HEADS * RET_DK, RET_HEADS * RET_DK, RET_HEADS * RET_DV, RET_HEADS * RET_DV,
             DIFF_HEADS * 2 * DIFF_D, DIFF_HEADS * 2 * DIFF_D, DIFF_HEADS * DIFF_DV,
             N_BRANCH * D_MODEL)
    offs = []
    acc = 0
    for s in sizes[:-1]:
        acc += s
        offs.append(acc)
    return jnp.split(z, offs, axis=-1)


def token_mixers(h, lp, lam_init, ctx, row, col):
    B, T, _ = h.shape
    dt = h.dtype
    latent = ctx is not None
    (cq, ckv, kpe, gq, gk, gv, rq, rk, rv, rg, dq, dk, dv, gl) = _split_cols(h @ lp['w_in'])

    qm = (rms_norm(cq, lp['g_mla_q']) @ lp['w_mla_uq']).reshape(B, T, MLA_HEADS, MLA_NOPE + MLA_ROPE)
    q_nope, q_pe = qm[..., :MLA_NOPE], qm[..., MLA_NOPE:]
    ckv = rms_norm(ckv, lp['g_mla_kv'])
    if latent:
        q_pe = axial_rope(q_pe, row, col)
        kpe_lat = axial_rope(kpe[:, :, None, :], row, col)[:, :, 0]
        ckv_all = jnp.concatenate([ctx['mla_ckv'].astype(dt), ckv], axis=1)
        kpe_all = jnp.concatenate([ctx['mla_kpe'].astype(dt), kpe_lat], axis=1)
    else:
        ckv_all, kpe_all = ckv, kpe
    S = ckv_all.shape[1]
    kv = (ckv_all @ lp['w_mla_ukv']).reshape(B, S, MLA_HEADS, MLA_NOPE + MLA_V)
    k_m = jnp.concatenate([kv[..., :MLA_NOPE],
                           jnp.broadcast_to(kpe_all[:, :, None, :], (B, S, MLA_HEADS, MLA_ROPE))], axis=-1)
    q_m = jnp.concatenate([q_nope, q_pe], axis=-1)[:, :, :, None, :]
    o_mla = attend(q_m, k_m, kv[..., MLA_NOPE:], (MLA_NOPE + MLA_ROPE) ** -0.5).reshape(B, T, BRANCH_W)

    gq = rms_norm(gq.reshape(B, T, GQA_HEADS, GQA_HD), lp['g_gqa_q'])
    gk = rms_norm(gk.reshape(B, T, GQA_KV_HEADS, GQA_HD), lp['g_gqa_k'])
    gv = gv.reshape(B, T, GQA_KV_HEADS, GQA_HD)
    if latent:
        gq = axial_rope(gq, row, col)
        gk_all = jnp.concatenate([ctx['gqa_k'].astype(dt), axial_rope(gk, row, col)], axis=1)
        gv_all = jnp.concatenate([ctx['gqa_v'].astype(dt), gv], axis=1)
    else:
        gk_all, gv_all = gk, gv
    o_gqa = attend(gq.reshape(B, T, GQA_KV_HEADS, GQA_HEADS // GQA_KV_HEADS, GQA_HD),
                   gk_all, gv_all, GQA_HD ** -0.5).reshape(B, T, BRANCH_W)

    rq = rq.reshape(B, T, RET_HEADS, RET_DK)
    rk = rk.reshape(B, T, RET_HEADS, RET_DK) * (RET_DK ** -0.5)
    rv = rv.reshape(B, T, RET_HEADS, RET_DV)
    lg = jax.nn.log_sigmoid(lp['ret_decay'].astype(jnp.float32))
    if latent:
        s_f0, s_b0 = ctx['ret'][:, 0], ctx['ret'][:, 1]
    else:
        s_f0 = jnp.zeros((B, RET_HEADS, RET_DK, RET_DV), jnp.float32)
        s_b0 = s_f0
    o_f, s_f = retention_scan(rq, rk, rv, lg[0], s_f0)
    o_b, s_b = retention_scan(jnp.flip(rq, 1), jnp.flip(rk, 1), jnp.flip(rv, 1), lg[1], s_b0)
    o_r = (head_group_norm(o_f + jnp.flip(o_b, 1), lp['g_ret'])
           * jax.nn.silu(rg.astype(jnp.float32))).astype(dt)

    dq = dq.reshape(B, T, DIFF_HEADS, 2, DIFF_D)
    dk = dk.reshape(B, T, DIFF_HEADS, 2, DIFF_D)
    dv = dv.reshape(B, T, DIFF_HEADS, DIFF_DV)
    if latent:
        dq = axial_rope(dq, row, col)
        dk_all = jnp.concatenate([ctx['diff_k'].astype(dt), axial_rope(dk, row, col)], axis=1)
        dv_all = jnp.concatenate([ctx['diff_v'].astype(dt), dv], axis=1)
    else:
        dk_all, dv_all = dk, dv
    lam_p = lp['diff_lambda'].astype(jnp.float32)
    lam = jnp.exp(jnp.sum(lam_p[0] * lam_p[1])) - jnp.exp(jnp.sum(lam_p[2] * lam_p[3])) + lam_init
    sc = DIFF_D ** -0.5
    a1 = attend(dq[:, :, :, 0:1, :], dk_all[:, :, :, 0, :], dv_all, sc)
    a2 = attend(dq[:, :, :, 1:2, :], dk_all[:, :, :, 1, :], dv_all, sc)
    o_d = (a1.astype(jnp.float32) - lam * a2.astype(jnp.float32))[:, :, :, 0, :]
    o_d = (rms_norm(o_d, lp['g_diff']) * (1.0 - lam_init)).reshape(B, T, BRANCH_W).astype(dt)

    br = jnp.stack([o_mla, o_gqa, o_r, o_d], axis=2)
    gates = jax.nn.sigmoid(gl.reshape(B, T, N_BRANCH, D_MODEL))
    merged = jnp.sum(gates * jnp.einsum('btnw,nwd->btnd', br, lp['w_br']), axis=2)
    out = merged @ lp['w_out']
    if latent:
        return out, None
    new_ret = jnp.stack([s_f, s_b], axis=1).astype(dt)
    return out, (ckv, kpe, gk, gv, dk, dv, new_ret)


def moe(h, lp):
    B, T, D = h.shape
    t = h.reshape(B * T, D)
    scores = jax.nn.sigmoid((t @ lp['w_router']).astype(jnp.float32))
    sel = scores + lp['b_router'].astype(jnp.float32)
    sel_g = sel.reshape(-1, N_GROUPS, N_EXPERTS // N_GROUPS)
    grp = jnp.sum(lax.top_k(sel_g, 2)[0], axis=-1)
    _, gidx = lax.top_k(grp, TOPK_GROUPS)
    gmask = jnp.any(gidx[:, :, None] == jnp.arange(N_GROUPS)[None, None, :], axis=1)
    masked = jnp.where(gmask[:, :, None], sel_g, -jnp.inf).reshape(-1, N_EXPERTS)
    _, eidx = lax.top_k(masked, TOP_K)
    w = jnp.take_along_axis(scores, eidx, axis=-1)
    w = w / jnp.sum(w, axis=-1, keepdims=True) * ROUTE_SCALE
    combine = jnp.sum(jax.nn.one_hot(eidx, N_EXPERTS, dtype=jnp.float32) * w[..., None], axis=1)
    gu = jnp.einsum('nd,edf->nef', t, lp['w_exp_gu'])
    a = jax.nn.silu(gu[..., :EXPERT_FF]) * gu[..., EXPERT_FF:]
    a = (a * combine[..., None].astype(a.dtype)).reshape(-1, N_EXPERTS * EXPERT_FF)
    routed = a @ lp['w_exp_down'].reshape(N_EXPERTS * EXPERT_FF, D_MODEL)
    sgu = t @ lp['w_sh_gu']
    shared = (jax.nn.silu(sgu[:, :SHARED_FF]) * sgu[:, SHARED_FF:]) @ lp['w_sh_down']
    return (routed + shared).reshape(B, T, D)


def trunk_layer(x, cvec, l, lp, ctx, row, col):
    mod = jax.nn.silu(cvec) @ lp['w_mod'] + lp['b_mod']
    sh1, sc1, g1, sh2, sc2, g2 = jnp.split(mod[:, None, :], 6, axis=-1)
    lam_init = 0.8 - 0.6 * math.exp(-0.3 * l)
    h = rms_norm(x, lp['g_pre1']) * (1.0 + sc1) + sh1
    m, new_cache = token_mixers(h, lp, lam_init, ctx, row, col)
    x = x + g1 * rms_norm(m, lp['g_post1'])
    h = rms_norm(x, lp['g_pre2']) * (1.0 + sc2) + sh2
    x = x + g2 * rms_norm(moe(h, lp), lp['g_post2'])
    return x, new_cache


def setup_inputs(seed: int = 0) -> dict:
    key = jax.random.key(seed)
    ks = iter(jax.random.split(key, 48))
    f32 = jnp.float32

    def nrm(shape, scale=1.0):
        return jax.random.normal(next(ks), shape, f32) * scale

    def gain(shape):
        return 1.0 + 0.02 * jax.random.normal(next(ks), shape, f32)

    L = DEPTH
    D = D_MODEL
    decay_base = jnp.log(2.0 ** (5.0 + jnp.arange(RET_HEADS, dtype=f32)) - 1.0)
    return {
        'x_prompt': nrm((BATCH, SEQ, D)),
        'x_sample': nrm((DEC_BATCH, DEC_SEQ, D)),
        'cache_mla_ckv': nrm((DEC_BATCH, L, PAST_LEN, MLA_KV_LORA)),
        'cache_mla_kpe': nrm((DEC_BATCH, L, PAST_LEN, MLA_ROPE)),
        'cache_gqa_k': nrm((DEC_BATCH, L, PAST_LEN, GQA_KV_HEADS, GQA_HD)),
        'cache_gqa_v': nrm((DEC_BATCH, L, PAST_LEN, GQA_KV_HEADS, GQA_HD)),
        'cache_diff_k': nrm((DEC_BATCH, L, PAST_LEN, DIFF_HEADS, 2, DIFF_D)),
        'cache_diff_v': nrm((DEC_BATCH, L, PAST_LEN, DIFF_HEADS, DIFF_DV)),
        'state_ret': nrm((DEC_BATCH, L, 2, RET_HEADS, RET_DK, RET_DV), 0.5),
        'c': nrm((DEC_BATCH, D)),
        'c_ctx': nrm((D,)),
        'w_mod': nrm((L, D, 6 * D), 0.5 * D ** -0.5),
        'b_mod': nrm((L, 6 * D), 0.02),
        'g_pre1': gain((L, D)),
        'g_post1': gain((L, D)),
        'g_pre2': gain((L, D)),
        'g_post2': gain((L, D)),
        'w_in': nrm((L, D, IN_COLS), D ** -0.5),
        'g_mla_q': gain((L, MLA_Q_LORA)),
        'w_mla_uq': nrm((L, MLA_Q_LORA, MLA_HEADS * (MLA_NOPE + MLA_ROPE)), MLA_Q_LORA ** -0.5),
        'g_mla_kv': gain((L, MLA_KV_LORA)),
        'w_mla_ukv': nrm((L, MLA_KV_LORA, MLA_HEADS * (MLA_NOPE + MLA_V)), MLA_KV_LORA ** -0.5),
        'g_gqa_q': gain((L, GQA_HD)),
        'g_gqa_k': gain((L, GQA_HD)),
        'ret_decay': decay_base + nrm((L, 2, RET_HEADS), 0.05),
        'g_ret': gain((L, RET_HEADS * RET_DV)),
        'diff_lambda': nrm((L, 4, DIFF_D), 0.1),
        'g_diff': gain((L, DIFF_DV)),
        'w_br': nrm((L, N_BRANCH, BRANCH_W, D), BRANCH_W ** -0.5),
        'w_out': nrm((L, D, D), D ** -0.5),
        'w_router': nrm((L, D, N_EXPERTS), D ** -0.5),
        'b_router': nrm((L, N_EXPERTS), 0.01),
        'w_exp_gu': nrm((L, N_EXPERTS, D, 2 * EXPERT_FF), D ** -0.5),
        'w_exp_down': nrm((L, N_EXPERTS, EXPERT_FF, D), EXPERT_FF ** -0.5),
        'w_sh_gu': nrm((L, D, 2 * SHARED_FF), D ** -0.5),
        'w_sh_down': nrm((L, SHARED_FF, D), SHARED_FF ** -0.5),
    }


def reference(x_prompt, x_sample, cache_mla_ckv, cache_mla_kpe, cache_gqa_k, cache_gqa_v,
              cache_diff_k, cache_diff_v, state_ret, c, c_ctx,
              w_mod, b_mod, g_pre1, g_post1, g_pre2, g_post2, w_in,
              g_mla_q, w_mla_uq, g_mla_kv, w_mla_ukv, g_gqa_q, g_gqa_k,
              ret_decay, g_ret, diff_lambda, g_diff, w_br, w_out,
              w_router, b_router, w_exp_gu, w_exp_down, w_sh_gu, w_sh_down):
    T_lat = x_sample.shape[1]
    rows = T_lat // GRID_W
    rr, cc = jnp.meshgrid(jnp.arange(rows), jnp.arange(GRID_W), indexing='ij')
    row, col = rr.reshape(-1), cc.reshape(-1)
    yp, ys = x_prompt, x_sample
    l_ckv, l_kpe, l_gk, l_gv, l_dk, l_dv, l_ret = [], [], [], [], [], [], []
    for l in range(DEPTH):
        lp = {
            'w_mod': w_mod[l], 'b_mod': b_mod[l],
            'g_pre1': g_pre1[l], 'g_post1': g_post1[l], 'g_pre2': g_pre2[l], 'g_post2': g_post2[l],
            'w_in': w_in[l],
            'g_mla_q': g_mla_q[l], 'w_mla_uq': w_mla_uq[l], 'g_mla_kv': g_mla_kv[l], 'w_mla_ukv': w_mla_ukv[l],
            'g_gqa_q': g_gqa_q[l], 'g_gqa_k': g_gqa_k[l],
            'ret_decay': ret_decay[l], 'g_ret': g_ret[l],
            'diff_lambda': diff_lambda[l], 'g_diff': g_diff[l],
            'w_br': w_br[l], 'w_out': w_out[l],
            'w_router': w_router[l], 'b_router': b_router[l],
            'w_exp_gu': w_exp_gu[l], 'w_exp_down': w_exp_down[l],
            'w_sh_gu': w_sh_gu[l], 'w_sh_down': w_sh_down[l],
        }
        yp, (n_ckv, n_kpe, n_gk, n_gv, n_dk, n_dv, n_ret) = trunk_layer(yp, c_ctx[None, :], l, lp, None, None, None)
        l_ckv.append(n_ckv); l_kpe.append(n_kpe); l_gk.append(n_gk); l_gv.append(n_gv)
        l_dk.append(n_dk); l_dv.append(n_dv); l_ret.append(n_ret)
        ctx = {
            'mla_ckv': cache_mla_ckv[:, l], 'mla_kpe': cache_mla_kpe[:, l],
            'gqa_k': cache_gqa_k[:, l], 'gqa_v': cache_gqa_v[:, l],
            'diff_k': cache_diff_k[:, l], 'diff_v': cache_diff_v[:, l],
            'ret': state_ret[:, l],
        }
        ys, _ = trunk_layer(ys, c, l, lp, ctx, row, col)
    new_mla_ckv = jnp.stack(l_ckv, axis=1)
    new_mla_kpe = jnp.stack(l_kpe, axis=1)
    new_gqa_k = jnp.stack(l_gk, axis=1)
    new_gqa_v = jnp.stack(l_gv, axis=1)
    new_diff_k = jnp.stack(l_dk, axis=1)
    new_diff_v = jnp.stack(l_dv, axis=1)
    new_state_ret = jnp.stack(l_ret, axis=1)
    return (yp, ys, new_mla_ckv, new_mla_kpe, new_gqa_k, new_gqa_v, new_diff_k, new_diff_v, new_state_ret)
```

```python
import functools
import math

import numpy as np
import jax
import jax.numpy as jnp
from jax import lax
from jax.experimental import pallas as pl
from jax.experimental.pallas import tpu as pltpu

F32 = jnp.float32
BF16 = jnp.bfloat16

D = 1024
DEPTH = 2
GRID_W = 64
ROPE_BASE = 10000.0
EPS = 1e-6

MLA_HEADS, MLA_NOPE, MLA_ROPE, MLA_V = 8, 64, 32, 64
MLA_Q_LORA, MLA_KV_LORA = 384, 256
GQA_HEADS, GQA_KV_HEADS, GQA_HD = 8, 2, 64
RET_HEADS, RET_DK, RET_DV = 4, 64, 128
DIFF_HEADS, DIFF_D, DIFF_DV = 4, 64, 128
N_BRANCH, BRANCH_W = 4, 512
N_EXPERTS, TOP_K, N_GROUPS, TOPK_GROUPS = 32, 4, 4, 2
EXPERT_FF, SHARED_FF = 256, 256
ROUTE_SCALE = 2.5
GROUP_SIZE = N_EXPERTS // N_GROUPS

LANES = 128
HALF_LANES = 64
VMEM_LIMIT = 56 * 1024 * 1024

C_CQ, C_CKV, C_KPE, C_GQ, C_GK, C_GV = 0, 384, 640, 768, 1280, 1408
C_DQ, C_DK, C_DV, C_RQ, C_RK, C_RV, C_RG, C_GL, C_END = (
    1536, 2048, 2560, 3072, 3328, 3584, 4096, 4608, 8704)
O_CQ, O_CKV, O_KPE, O_GQ, O_GK, O_GV = 0, 384, 640, 672, 1184, 1312
O_RQ, O_RK, O_RV, O_RG, O_DQ, O_DK, O_DV, O_GL, O_END = (
    1440, 1696, 1952, 2464, 2976, 3488, 4000, 4512, 8608)
GQA_ORDER = (0, 4, 1, 5, 2, 6, 3, 7)

KVM_W = 4 * 256 + 512
TM = 256
TQ = 256
TM_MOE = 1024


def _cparams(sem):
    return pltpu.CompilerParams(dimension_semantics=sem, vmem_limit_bytes=VMEM_LIMIT)


def _const_spec(shape):
    nd = len(shape)
    return pl.BlockSpec(shape, lambda *_: (0,) * nd)


def _rms(x, g):
    return x * lax.rsqrt(jnp.mean(x * x, axis=-1, keepdims=True) + EPS) * g


def _dot(a, b):
    return jnp.dot(a, b, preferred_element_type=F32)


def _dot_nt(a, b):
    return lax.dot_general(a, b, (((1,), (1,)), ((), ())), preferred_element_type=F32)


def _silu(x):
    return x * jax.nn.sigmoid(x)


def _lane_iota(shape):
    return lax.broadcasted_iota(jnp.int32, shape, len(shape) - 1)


def _seg_meansq(x, bd_ref, width):
    sq = x * x
    hi = sq.astype(BF16)
    lo = (sq - hi.astype(F32)).astype(BF16)
    bd = bd_ref[0:width, 0:width]
    return (_dot(hi, bd) + _dot(lo, bd)) * (1.0 / GQA_HD)


def _rope(x, cos, sin_signed, half):
    width = x.shape[-1]
    first = (_lane_iota(x.shape) % (2 * half)) < half
    partner = jnp.where(first, pltpu.roll(x, width - half, 1), pltpu.roll(x, half, 1))
    return x * cos + partner * sin_signed


def _tile_lanes(t, reps):
    return t if reps == 1 else jnp.concatenate([t] * reps, axis=1)


def _mod_kernel(c_ref, w_ref, b_ref, o_ref):
    a = _silu(c_ref[...]).astype(BF16)
    o_ref[...] = _dot(a, w_ref[...].astype(BF16)) + b_ref[...]


def _mod_call(cond, w_mod, b_mod):
    n = w_mod.shape[1]
    tn = 1536
    return pl.pallas_call(
        _mod_kernel,
        grid=(n // tn,),
        in_specs=[_const_spec(cond.shape),
                  pl.BlockSpec((D, tn), lambda j: (0, j)),
                  pl.BlockSpec((1, tn), lambda j: (0, j))],
        out_specs=pl.BlockSpec((cond.shape[0], tn), lambda j: (0, j)),
        out_shape=jax.ShapeDtypeStruct((cond.shape[0], n), F32),
        compiler_params=_cparams(("arbitrary",)),
        name="mod",
    )(cond, w_mod, b_mod.reshape(1, n))


def _inprep_kernel(latent, *refs):
    (x_ref, mod_ref, gpre_ref, win_ref, gmq_ref, wuq_ref, gmkv_ref, wukv_ref,
     ggq_ref, ggk_ref, bd_ref) = refs[:11]
    refs = refs[11:]
    if latent:
        cos64_ref, sin64_ref, cospe_ref, sinpe_ref = refs[:4]
        refs = refs[4:]
    (qm_ref, kvm_ref, gqo_ref, gkv_ref, dqo_ref, dkv_ref, ret_ref, rg_ref, gate_ref) = refs[:9]
    refs = refs[9:]
    if not latent:
        ckv_o, kpe_o, gk_o, gv_o, dk_o, dv_o = refs

    x = x_ref[...]
    mod = mod_ref[...]
    sh1 = mod[:, 0:D]
    sc1 = mod[:, D:2 * D]
    hb = (_rms(x, gpre_ref[...]) * (1.0 + sc1) + sh1).astype(BF16)

    def z(a, b):
        return _dot(hb, win_ref[:, a:b])

    if latent:
        cos64, sin64 = cos64_ref[...], sin64_ref[...]
        cospe, sinpe = cospe_ref[...], sinpe_ref[...]

    cqn = _rms(z(C_CQ, C_CKV), gmq_ref[...]).astype(BF16)
    q = _dot(cqn, wuq_ref[...]) * ((MLA_NOPE + MLA_ROPE) ** -0.5)
    q_nope, q_pe = q[:, 0:512], q[:, 512:768]
    if latent:
        q_pe = _rope(q_pe, _tile_lanes(cospe, 2), _tile_lanes(sinpe, 2), MLA_ROPE // 4)
    qm_ref[:, 0:512] = q_nope.astype(BF16)
    qm_ref[:, 512:768] = q_pe.astype(BF16)

    ckvn = _rms(z(C_CKV, C_KPE), gmkv_ref[...])
    kv = _dot(ckvn.astype(BF16), wukv_ref[...])
    kpe4 = z(C_KPE, C_GQ)
    if latent:
        kpe4 = _rope(kpe4, cospe, sinpe, MLA_ROPE // 4)
    else:
        ckv_o[...] = ckvn
        kpe_o[...] = kpe4[:, 0:MLA_ROPE]
    kpe_b = kpe4.astype(BF16)
    for p in range(4):
        kvm_ref[:, p * 256:p * 256 + LANES] = kv[:, p * LANES:(p + 1) * LANES].astype(BF16)
        kvm_ref[:, p * 256 + LANES:(p + 1) * 256] = kpe_b
    kvm_ref[:, 1024:1536] = kv[:, 512:1024].astype(BF16)

    gq = z(C_GQ, C_GK)
    gq = gq * lax.rsqrt(_seg_meansq(gq, bd_ref, 512) + EPS) * ggq_ref[...]
    gk = z(C_GK, C_GV)
    gk = gk * lax.rsqrt(_seg_meansq(gk, bd_ref, LANES) + EPS) * ggk_ref[...]
    gv = z(C_GV, C_DQ)
    if latent:
        gq = _rope(gq, _tile_lanes(cos64, 4), _tile_lanes(sin64, 4), GQA_HD // 4)
        gk = _rope(gk, cos64, sin64, GQA_HD // 4)
    else:
        gk_o[...] = gk
        gv_o[...] = gv
    gqo_ref[...] = (gq * (GQA_HD ** -0.5)).astype(BF16)
    gkv_ref[:, 0:LANES] = gk.astype(BF16)
    gkv_ref[:, LANES:2 * LANES] = gv.astype(BF16)

    dq = z(C_DQ, C_DK)
    dk = z(C_DK, C_DV)
    dv = z(C_DV, C_RQ)
    if latent:
        dq = _rope(dq, _tile_lanes(cos64, 4), _tile_lanes(sin64, 4), DIFF_D // 4)
        dk = _rope(dk, _tile_lanes(cos64, 4), _tile_lanes(sin64, 4), DIFF_D // 4)
    else:
        dk_o[...] = dk
        dv_o[...] = dv
    dqo_ref[...] = (dq * (DIFF_D ** -0.5)).astype(BF16)
    dkv_ref[:, 0:512] = dk.astype(BF16)
    dkv_ref[:, 512:1024] = dv.astype(BF16)

    ret_ref[:, 0:256] = z(C_RQ, C_RK).astype(BF16)
    ret_ref[:, 256:512] = (z(C_RK, C_RV) * (RET_DK ** -0.5)).astype(BF16)
    ret_ref[:, 512:1024] = z(C_RV, C_RG).astype(BF16)
    rg_ref[...] = z(C_RG, C_GL).astype(BF16)

    for n in range(N_BRANCH):
        gate_ref[:, n * D:(n + 1) * D] = jax.nn.sigmoid(
            z(C_GL + n * D, C_GL + (n + 1) * D)).astype(BF16)


def _inprep_call(latent, x, mod3, mod_row, lw, tabs, t_len):
    n_tok = x.shape[0]
    nblk = n_tok // TM
    blk_per_seq = t_len // TM

    def tok(w):
        return pl.BlockSpec((TM, w), lambda i: (i, 0))

    in_specs = [tok(D),
                pl.BlockSpec((None, 1, 6 * D), lambda i: (mod_row(i), 0, 0)),
                _const_spec((1, D)),
                pl.BlockSpec((D, C_END), lambda i: (0, 0), pipeline_mode=pl.Buffered(1)),
                _const_spec((1, MLA_Q_LORA)), _const_spec((MLA_Q_LORA, 768)),
                _const_spec((1, MLA_KV_LORA)), _const_spec((MLA_KV_LORA, 1024)),
                _const_spec((1, 512)), _const_spec((1, LANES)), _const_spec((512, 512))]
    args = [x, mod3, lw["g_pre1"], lw["w_in"], lw["g_mla_q"], lw["w_uq"], lw["g_mla_kv"],
            lw["w_ukv"], lw["g_gqa_q"], lw["g_gqa_k"], lw["bd"]]
    if latent:
        tab_spec = pl.BlockSpec((TM, LANES), lambda i: (i % blk_per_seq, 0))
        in_specs += [tab_spec] * 4
        args += list(tabs)
    widths = [768, KVM_W, 512, 256, 512, 1024, 1024, 512, 4 * D]
    out_specs = [tok(w) for w in widths]
    out_shape = [jax.ShapeDtypeStruct((n_tok, w), BF16) for w in widths]
    if not latent:
        cw = [MLA_KV_LORA, MLA_ROPE, 128, 128, 512, 512]
        out_specs += [tok(w) for w in cw]
        out_shape += [jax.ShapeDtypeStruct((n_tok, w), F32) for w in cw]
    return pl.pallas_call(
        functools.partial(_inprep_kernel, latent),
        grid=(nblk,),
        in_specs=in_specs, out_specs=out_specs, out_shape=out_shape,
        compiler_params=_cparams(("arbitrary",)),
        name="inprep_lat" if latent else "inprep_ctx",
    )(*args)


def _pastkv_kernel(ckv_ref, kpe_ref, wukv_ref, o_ref):
    kv = _dot(ckv_ref[...].astype(BF16), wukv_ref[...])
    kpe_b = kpe_ref[...].astype(BF16)
    for p in range(4):
        o_ref[:, p * 256:p * 256 + LANES] = kv[:, p * LANES:(p + 1) * LANES].astype(BF16)
        o_ref[:, p * 256 + LANES:(p + 1) * 256] = kpe_b
    o_ref[:, 1024:1536] = kv[:, 512:1024].astype(BF16)


def _pastkv_call(ckv, kpe4, w_ukv):
    n = ckv.shape[0]
    return pl.pallas_call(
        _pastkv_kernel,
        grid=(n // TM,),
        in_specs=[pl.BlockSpec((TM, MLA_KV_LORA), lambda i: (i, 0)),
                  pl.BlockSpec((TM, LANES), lambda i: (i, 0)),
                  _const_spec((MLA_KV_LORA, 1024))],
        out_specs=pl.BlockSpec((TM, KVM_W), lambda i: (i, 0)),
        out_shape=jax.ShapeDtypeStruct((n, KVM_W), BF16),
        compiler_params=_cparams(("arbitrary",)),
        name="pastkv",
    )(ckv, kpe4, w_ukv)


def _softmax_pv(s, v):
    m = jnp.max(s, axis=-1, keepdims=True)
    p = jnp.exp(s - m)
    l = jnp.sum(p, axis=-1, keepdims=True)
    return _dot(p.astype(BF16), v) / l


def _attn_kernel(lam_init, qm_ref, kvm_ref, gq_ref, gkv_ref, dq_ref, dkv_ref,
                 lam_ref, gdiff_ref, o_ref):
    tq = qm_ref.shape[0]
    lane = _lane_iota((tq, LANES))
    low = lane < HALF_LANES
    zero = jnp.zeros((tq, LANES), BF16)

    for p in range(MLA_HEADS // 2):
        qn = qm_ref[:, p * LANES:(p + 1) * LANES]
        g = p // 2
        qpe = qm_ref[:, 512 + g * LANES:512 + (g + 1) * LANES]
        kk = kvm_ref[:, p * 256:(p + 1) * 256]
        vv = kvm_ref[:, 1024 + p * LANES:1024 + (p + 1) * LANES]
        outs = []
        for half in range(2):
            h = 2 * p + half
            slot = h % 4
            in_slot = (lane >= slot * MLA_ROPE) & (lane < (slot + 1) * MLA_ROPE)
            lhs = jnp.concatenate(
                [jnp.where(low if half == 0 else ~low, qn, zero),
                 jnp.where(in_slot, qpe, zero)], axis=1)
            outs.append(_softmax_pv(_dot_nt(lhs, kk), vv))
        o_ref[:, p * LANES:(p + 1) * LANES] = jnp.where(low, outs[0], outs[1]).astype(BF16)

    kk = gkv_ref[:, 0:LANES]
    vv = gkv_ref[:, LANES:2 * LANES]
    for g in range(GQA_HEADS // 2):
        qg = gq_ref[:, g * LANES:(g + 1) * LANES]
        o_lo = _softmax_pv(_dot_nt(jnp.where(low, qg, zero), kk), vv)
        o_hi = _softmax_pv(_dot_nt(jnp.where(low, zero, qg), kk), vv)
        o_ref[:, 512 + g * LANES:512 + (g + 1) * LANES] = jnp.where(low, o_lo, o_hi).astype(BF16)

    lp = lam_ref[...]
    lam = (jnp.exp(jnp.sum(lp[0:1] * lp[1:2], axis=-1, keepdims=True))
           - jnp.exp(jnp.sum(lp[2:3] * lp[3:4], axis=-1, keepdims=True)) + lam_init)
    for h in range(DIFF_HEADS):
        qh = dq_ref[:, h * LANES:(h + 1) * LANES]
        kk = dkv_ref[:, h * LANES:(h + 1) * LANES]
        vv = dkv_ref[:, 512 + h * LANES:512 + (h + 1) * LANES]
        a1 = _softmax_pv(_dot_nt(jnp.where(low, qh, zero), kk), vv)
        a2 = _softmax_pv(_dot_nt(jnp.where(low, zero, qh), kk), vv)
        od = _rms(a1 - lam * a2, gdiff_ref[...]) * (1.0 - lam_init)
        o_ref[:, 1024 + h * LANES:1024 + (h + 1) * LANES] = od.astype(BF16)


def _attn_call(lam_init, qm, kvm, gq, gkv, dq, dkv, lam_p, g_diff, n_b, t_len, s_len):
    nq = t_len // TQ

    def qspec(w):
        return pl.BlockSpec((TQ, w), lambda b, i: (b * nq + i, 0))

    def kspec(w):
        return pl.BlockSpec((s_len, w), lambda b, i: (b, 0))

    return pl.pallas_call(
        functools.partial(_attn_kernel, lam_init),
        grid=(n_b, nq),
        in_specs=[qspec(768), kspec(KVM_W), qspec(512), kspec(256), qspec(512), kspec(1024),
                  _const_spec((4, DIFF_D)), _const_spec((1, DIFF_DV))],
        out_specs=qspec(3 * BRANCH_W),
        out_shape=jax.ShapeDtypeStruct((n_b * t_len, 3 * BRANCH_W), BF16),
        compiler_params=_cparams(("arbitrary", "arbitrary")),
        name="attn",
    )(qm, kvm, gq, gkv, dq, dkv, lam_p, g_diff)


def _log_sigmoid(x):
    return jnp.minimum(x, 0.0) - jnp.log(1.0 + jnp.exp(-jnp.abs(x)))


def _ret_kernel(latent, t_len, dec_ref, q_ref, k_ref, v_ref, rg_ref, gret_ref, *refs):
    if latent:
        s0_ref, o_ref = refs
    else:
        o_ref, st_ref = refs
    tq = q_ref.shape[0]
    t0 = pl.program_id(1) * tq
    lane = _lane_iota((tq, LANES))
    low = lane < HALF_LANES
    zero = jnp.zeros((tq, LANES), BF16)
    t_idx = (t0 + lax.broadcasted_iota(jnp.int32, (tq, t_len), 0)).astype(F32)
    s_idx = lax.broadcasted_iota(jnp.int32, (tq, t_len), 1).astype(F32)
    dist = t_idx - s_idx
    t_col = (t0 + lax.broadcasted_iota(jnp.int32, (tq, 1), 0)).astype(F32)

    def lg(d, h):
        return _log_sigmoid(jnp.full((1, 1), dec_ref[d, h], F32))

    for h in range(RET_HEADS):
        p, half = h // 2, h % 2
        qp = q_ref[:, p * LANES:(p + 1) * LANES]
        qm = jnp.where(low if half == 0 else ~low, qp, zero)
        kp = k_ref[:, p * LANES:(p + 1) * LANES]
        vh = v_ref[:, h * LANES:(h + 1) * LANES]
        lgf, lgb = lg(0, h), lg(1, h)
        dmask = (jnp.where(dist >= 0, jnp.exp(lgf * jnp.maximum(dist, 0.0)), 0.0)
                 + jnp.where(dist <= 0, jnp.exp(lgb * jnp.maximum(-dist, 0.0)), 0.0))
        o = _dot((_dot_nt(qm, kp) * dmask).astype(BF16), vh)
        if latent:
            sf = s0_ref[0, p].astype(BF16)
            sb = s0_ref[1, p].astype(BF16)
            o = o + _dot(qm, sf) * jnp.exp(lgf * (t_col + 1.0))
            o = o + _dot(qm, sb) * jnp.exp(lgb * (float(t_len) - t_col))
        mu = jnp.mean(o, axis=-1, keepdims=True)
        oc = o - mu
        y = oc * lax.rsqrt(jnp.mean(oc * oc, axis=-1, keepdims=True) + EPS)
        y = y * gret_ref[:, h * LANES:(h + 1) * LANES]
        rg = rg_ref[:, h * LANES:(h + 1) * LANES].astype(F32)
        o_ref[:, h * LANES:(h + 1) * LANES] = (y * _silu(rg)).astype(BF16)

    if not latent:
        s_col = lax.broadcasted_iota(jnp.int32, (t_len, 1), 0).astype(F32)
        lane_t = _lane_iota((1, LANES)) < HALF_LANES
        for p in range(RET_HEADS // 2):
            kp = k_ref[:, p * LANES:(p + 1) * LANES].astype(F32)
            for d in range(2):
                lg_lane = jnp.where(lane_t, lg(d, 2 * p), lg(d, 2 * p + 1))
                expo = (float(t_len) - 1.0 - s_col) if d == 0 else s_col
                kdec_t = jnp.transpose(kp * jnp.exp(lg_lane * expo)).astype(BF16)
                for half in range(2):
                    h = 2 * p + half
                    st = _dot(kdec_t, v_ref[:, h * LANES:(h + 1) * LANES])
                    st_ref[d, h] = st[half * RET_DK:(half + 1) * RET_DK, :]


def _ret_call(latent, dec, ret, rg, g_ret, s0, n_b, t_len):
    nq = t_len // TQ
    assert latent or nq == 1
    in_specs = [pl.BlockSpec(memory_space=pltpu.SMEM),
                pl.BlockSpec((TQ, 256), lambda b, i: (b * nq + i, 0)),
                pl.BlockSpec((t_len, 256), lambda b, i: (b, 1)),
                pl.BlockSpec((t_len, 512), lambda b, i: (b, 1)),
                pl.BlockSpec((TQ, 512), lambda b, i: (b * nq + i, 0)),
                _const_spec((1, 512))]
    args = [dec, ret, ret, ret, rg, g_ret]
    out_specs = [pl.BlockSpec((TQ, 512), lambda b, i: (b * nq + i, 0))]
    out_shape = [jax.ShapeDtypeStruct((n_b * t_len, 512), BF16)]
    if latent:
        in_specs.append(pl.BlockSpec((None, 2, 2, LANES, LANES), lambda b, i: (b, 0, 0, 0, 0)))
        args.append(s0)
    else:
        out_specs.append(pl.BlockSpec((None, 2, RET_HEADS, RET_DK, RET_DV),
                                      lambda b, i: (b, 0, 0, 0, 0)))
        out_shape.append(jax.ShapeDtypeStruct((n_b, 2, RET_HEADS, RET_DK, RET_DV), F32))
    return pl.pallas_call(
        functools.partial(_ret_kernel, latent, t_len),
        grid=(n_b, nq),
        in_specs=in_specs, out_specs=out_specs, out_shape=out_shape,
        compiler_params=_cparams(("arbitrary", "arbitrary")),
        name="ret_lat" if latent else "ret_ctx",
    )(*args)


def _merge_kernel(x_ref, mod_ref, br_ref, or_ref, gate_ref, wbr_ref, wout_ref, gpost_ref, o_ref):
    merged = None
    for n in range(N_BRANCH):
        if n < 2:
            b = br_ref[:, n * BRANCH_W:(n + 1) * BRANCH_W]
        elif n == 2:
            b = or_ref[...]
        else:
            b = br_ref[:, 2 * BRANCH_W:3 * BRANCH_W]
        t = gate_ref[:, n * D:(n + 1) * D].astype(F32) * _dot(b, wbr_ref[n])
        merged = t if merged is None else merged + t
    out = _dot(merged.astype(BF16), wout_ref[...])
    g1 = mod_ref[...][:, 2 * D:3 * D]
    o_ref[...] = x_ref[...] + g1 * _rms(out, gpost_ref[...])


def _merge_call(x, mod3, mod_row, br, o_r, gates, lw):
    n_tok = x.shape[0]

    def tok(w):
        return pl.BlockSpec((TM, w), lambda i: (i, 0))

    return pl.pallas_call(
        _merge_kernel,
        grid=(n_tok // TM,),
        in_specs=[tok(D), pl.BlockSpec((None, 1, 6 * D), lambda i: (mod_row(i), 0, 0)),
                  tok(3 * BRANCH_W), tok(BRANCH_W), tok(4 * D),
                  _const_spec((N_BRANCH, BRANCH_W, D)), _const_spec((D, D)), _const_spec((1, D))],
        out_specs=tok(D),
        out_shape=jax.ShapeDtypeStruct((n_tok, D), F32),
        compiler_params=_cparams(("arbitrary",)),
        name="merge",
    )(x, mod3, br, o_r, gates, lw["w_br"], lw["w_out"], lw["g_post1"])


def _route(logits_t, bias):
    n = logits_t.shape[1]
    scores = jax.nn.sigmoid(logits_t)
    sel = scores + bias
    neg = -jnp.inf
    sub = lax.broadcasted_iota(jnp.int32, (GROUP_SIZE, n), 0)
    grp = []
    for g in range(N_GROUPS):
        blk = sel[g * GROUP_SIZE:(g + 1) * GROUP_SIZE]
        m1 = jnp.max(blk, axis=0, keepdims=True)
        i1 = jnp.min(jnp.where(blk == m1, sub, GROUP_SIZE), axis=0, keepdims=True)
        m2 = jnp.max(jnp.where(sub == i1, neg, blk), axis=0, keepdims=True)
        grp.append(m1 + m2)
    parts = []
    for g in range(N_GROUPS):
        beaten = jnp.zeros((1, n), jnp.int32)
        for o in range(N_GROUPS):
            if o == g:
                continue
            wins = (grp[o] > grp[g]) | (grp[o] == grp[g]) if o < g else (grp[o] > grp[g])
            beaten = beaten + wins.astype(jnp.int32)
        keep = beaten < TOPK_GROUPS
        parts.append(jnp.where(keep, sel[g * GROUP_SIZE:(g + 1) * GROUP_SIZE], neg))
    cur = jnp.concatenate(parts, axis=0)
    eidx = lax.broadcasted_iota(jnp.int32, (N_EXPERTS, n), 0)
    comb = jnp.zeros((N_EXPERTS, n), F32)
    for _ in range(TOP_K):
        m = jnp.max(cur, axis=0, keepdims=True)
        i = jnp.min(jnp.where(cur == m, eidx, N_EXPERTS), axis=0, keepdims=True)
        hit = eidx == i
        comb = jnp.where(hit, scores, comb)
        cur = jnp.where(hit, neg, cur)
    wsum = jnp.sum(comb, axis=0, keepdims=True)
    return comb / wsum * ROUTE_SCALE


def _moe_kernel(x_ref, mod_ref, gpre_ref, wr_ref, br_ref, wgu_ref, wdn_ref, wsgu_ref, wsdn_ref,
                gpost_ref, o_ref, hb_ref, comb_ref, acc_ref):
    e = pl.program_id(1)
    tm = x_ref.shape[0]

    @pl.when(e == 0)
    def _():
        mod = mod_ref[...]
        sh2, sc2 = mod[:, 3 * D:4 * D], mod[:, 4 * D:5 * D]
        h = _rms(x_ref[...], gpre_ref[...]) * (1.0 + sc2) + sh2
        hb = h.astype(BF16)
        hb_ref[...] = hb
        h_lo = (h - hb.astype(F32)).astype(BF16)
        wr = wr_ref[...]
        wr_hi = wr.astype(BF16)
        wr_lo = (wr - wr_hi.astype(F32)).astype(BF16)
        logits_t = _dot_nt(wr_hi, hb) + _dot_nt(wr_hi, h_lo) + _dot_nt(wr_lo, hb)
        comb_t = _route(logits_t, br_ref[...])
        pad = jnp.zeros((LANES - N_EXPERTS, tm), F32)
        comb_ref[...] = jnp.transpose(jnp.concatenate([comb_t, pad], axis=0))
        sgu = _dot(hb, wsgu_ref[...])
        sa = _silu(sgu[:, 0:SHARED_FF]) * sgu[:, SHARED_FF:2 * SHARED_FF]
        acc_ref[...] = _dot(sa.astype(BF16), wsdn_ref[...])

    hb = hb_ref[...]
    gu = _dot(hb, wgu_ref[...])
    onehot = _lane_iota((1, LANES)) == e
    c = jnp.sum(jnp.where(onehot, comb_ref[...], 0.0), axis=-1, keepdims=True)
    a = _silu(gu[:, 0:EXPERT_FF]) * gu[:, EXPERT_FF:2 * EXPERT_FF] * c
    acc_ref[...] += _dot(a.astype(BF16), wdn_ref[...])

    @pl.when(e == N_EXPERTS - 1)
    def _():
        g2 = mod_ref[...][:, 5 * D:6 * D]
        o_ref[...] = x_ref[...] + g2 * _rms(acc_ref[...], gpost_ref[...])


def _moe_call(x, mod3, mod_row, lw):
    n_tok = x.shape[0]
    tm = TM_MOE
    return pl.pallas_call(
        _moe_kernel,
        grid=(n_tok // tm, N_EXPERTS),
        in_specs=[pl.BlockSpec((tm, D), lambda i, e: (i, 0)),
                  pl.BlockSpec((None, 1, 6 * D), lambda i, e: (mod_row(i), 0, 0)),
                  _const_spec((1, D)), _const_spec((N_EXPERTS, D)), _const_spec((N_EXPERTS, 1)),
                  pl.BlockSpec((None, D, 2 * EXPERT_FF), lambda i, e: (e, 0, 0)),
                  pl.BlockSpec((None, EXPERT_FF, D), lambda i, e: (e, 0, 0)),
                  _const_spec((D, 2 * SHARED_FF)), _const_spec((SHARED_FF, D)),
                  _const_spec((1, D))],
        out_specs=pl.BlockSpec((tm, D), lambda i, e: (i, 0)),
        out_shape=jax.ShapeDtypeStruct((n_tok, D), F32),
        scratch_shapes=[pltpu.VMEM((tm, D), BF16), pltpu.VMEM((tm, LANES), F32),
                        pltpu.VMEM((tm, D), F32)],
        compiler_params=_cparams(("arbitrary", "arbitrary")),
        name="moe",
    )(x, mod3, lw["g_pre2"], lw["w_router_t"], lw["b_router"], lw["w_exp_gu"], lw["w_exp_down"],
      lw["w_sh_gu"], lw["w_sh_down"], lw["g_post2"])


def _rope_tables(t_len):
    pos = np.arange(t_len)
    row, col = pos // GRID_W, pos % GRID_W

    def tab(r):
        half = r // 2
        freq = ROPE_BASE ** (-np.arange(half, dtype=np.float64) / half)
        sign = np.concatenate([-np.ones(half), np.ones(half)])
        cs, sn = [], []
        for p in (row, col):
            ang = p[:, None].astype(np.float64) * freq[None, :]
            cs.append(np.concatenate([np.cos(ang), np.cos(ang)], axis=1))
            sn.append(np.concatenate([np.sin(ang), np.sin(ang)], axis=1) * sign[None, :])
        return np.concatenate(cs, axis=1), np.concatenate(sn, axis=1)

    c64, s64 = tab(GQA_HD // 2)
    cpe, spe = tab(MLA_ROPE // 2)
    out = (np.tile(c64, (1, 2)), np.tile(s64, (1, 2)), np.tile(cpe, (1, 4)), np.tile(spe, (1, 4)))
    return tuple(jnp.asarray(a, F32) for a in out)


def _layer_weights(l, p):
    w_in = p["w_in"][l]

    def seg(a, b):
        return w_in[:, a:b]

    gq_cols = [seg(O_GQ + h * GQA_HD, O_GQ + (h + 1) * GQA_HD) for h in GQA_ORDER]
    w_in_p = jnp.concatenate(
        [seg(O_CQ, O_KPE)] + [seg(O_KPE, O_GQ)] * 4 + gq_cols
        + [seg(O_GK, O_RQ), seg(O_DQ, O_GL), seg(O_RQ, O_DQ), seg(O_GL, O_END)], axis=1)
    w_uq = p["w_mla_uq"][l].reshape(MLA_Q_LORA, MLA_HEADS, MLA_NOPE + MLA_ROPE)
    w_uq = jnp.concatenate([w_uq[:, :, :MLA_NOPE].reshape(MLA_Q_LORA, -1),
                            w_uq[:, :, MLA_NOPE:].reshape(MLA_Q_LORA, -1)], axis=1)
    w_ukv = p["w_mla_ukv"][l].reshape(MLA_KV_LORA, MLA_HEADS, MLA_NOPE + MLA_V)
    w_ukv = jnp.concatenate([w_ukv[:, :, :MLA_NOPE].reshape(MLA_KV_LORA, -1),
                             w_ukv[:, :, MLA_NOPE:].reshape(MLA_KV_LORA, -1)], axis=1)
    w_br = p["w_br"][l]
    w_br_gqa = w_br[1].reshape(GQA_HEADS, GQA_HD, D)[jnp.array(GQA_ORDER)].reshape(BRANCH_W, D)
    w_br = jnp.stack([w_br[0], w_br_gqa, w_br[2], w_br[3]], axis=0)
    blk = np.arange(512) // GQA_HD
    return {
        "w_mod": p["w_mod"][l], "b_mod": p["b_mod"][l],
        "g_pre1": p["g_pre1"][l].reshape(1, D), "g_post1": p["g_post1"][l].reshape(1, D),
        "g_pre2": p["g_pre2"][l].reshape(1, D), "g_post2": p["g_post2"][l].reshape(1, D),
        "w_in": w_in_p.astype(BF16),
        "g_mla_q": p["g_mla_q"][l].reshape(1, -1), "w_uq": w_uq.astype(BF16),
        "g_mla_kv": p["g_mla_kv"][l].reshape(1, -1), "w_ukv": w_ukv.astype(BF16),
        "g_gqa_q": jnp.tile(p["g_gqa_q"][l], GQA_HEADS).reshape(1, -1),
        "g_gqa_k": jnp.tile(p["g_gqa_k"][l], GQA_KV_HEADS).reshape(1, -1),
        "bd": jnp.asarray(blk[:, None] == blk[None, :], BF16),
        "ret_decay": p["ret_decay"][l],
        "g_ret": p["g_ret"][l].reshape(1, -1),
        "diff_lambda": p["diff_lambda"][l], "g_diff": p["g_diff"][l].reshape(1, -1),
        "w_br": w_br.astype(BF16), "w_out": p["w_out"][l].astype(BF16),
        "w_router_t": p["w_router"][l].T, "b_router": p["b_router"][l].reshape(-1, 1),
        "w_exp_gu": p["w_exp_gu"][l].astype(BF16), "w_exp_down": p["w_exp_down"][l].astype(BF16),
        "w_sh_gu": p["w_sh_gu"][l].astype(BF16), "w_sh_down": p["w_sh_down"][l].astype(BF16),
    }


def _mixers(latent, l, x, mod3, mod_row, lw, n_b, t_len, tabs=None, past=None, s0=None):
    lam_init = 0.8 - 0.6 * math.exp(-0.3 * l)
    outs = _inprep_call(latent, x, mod3, mod_row, lw, tabs, t_len)
    qm, kvm, gq, gkv, dq, dkv, ret, rg, gates = outs[:9]
    s_len = t_len
    if latent:
        past_kvm, past_gkv, past_dkv = past
        p_len = past_gkv.shape[1]
        s_len = p_len + t_len

        def cat(a, b):
            return jnp.concatenate([a, b.reshape(n_b, t_len, -1)], axis=1).reshape(n_b * s_len, -1)

        kvm, gkv, dkv = cat(past_kvm, kvm), cat(past_gkv, gkv), cat(past_dkv, dkv)
    br = _attn_call(lam_init, qm, kvm, gq, gkv, dq, dkv, lw["diff_lambda"], lw["g_diff"],
                    n_b, t_len, s_len)
    r = _ret_call(latent, lw["ret_decay"], ret, rg, lw["g_ret"], s0, n_b, t_len)
    y = _merge_call(x, mod3, mod_row, br, r[0], gates, lw)
    cache = None if latent else tuple(outs[9:]) + (r[1],)
    return y, cache


def kernel(x_prompt, x_sample, cache_mla_ckv, cache_mla_kpe, cache_gqa_k, cache_gqa_v, cache_diff_k, cache_diff_v, state_ret, c, c_ctx, w_mod, b_mod, g_pre1, g_post1, g_pre2, g_post2, w_in, g_mla_q, w_mla_uq, g_mla_kv, w_mla_ukv, g_gqa_q, g_gqa_k, ret_decay, g_ret, diff_lambda, g_diff, w_br, w_out, w_router, b_router, w_exp_gu, w_exp_down, w_sh_gu, w_sh_down):
    params = dict(w_mod=w_mod, b_mod=b_mod, g_pre1=g_pre1, g_post1=g_post1, g_pre2=g_pre2,
                  g_post2=g_post2, w_in=w_in, g_mla_q=g_mla_q, w_mla_uq=w_mla_uq,
                  g_mla_kv=g_mla_kv, w_mla_ukv=w_mla_ukv, g_gqa_q=g_gqa_q, g_gqa_k=g_gqa_k,
                  ret_decay=ret_decay, g_ret=g_ret, diff_lambda=diff_lambda, g_diff=g_diff,
                  w_br=w_br, w_out=w_out, w_router=w_router, b_router=b_router,
                  w_exp_gu=w_exp_gu, w_exp_down=w_exp_down, w_sh_gu=w_sh_gu, w_sh_down=w_sh_down)
    n_bc, t_c, _ = x_prompt.shape
    n_bl, t_l, _ = x_sample.shape
    p_len = cache_mla_ckv.shape[2]
    tabs = _rope_tables(t_l)
    n_cond = 8
    cond = jnp.concatenate([c_ctx[None, :], c, jnp.zeros((n_cond - 1 - n_bl, D), F32)], axis=0)
    blk_c, blk_l = t_c // TM, t_l // TM
    moe_c, moe_l = t_c * n_bc // TM_MOE, t_l // TM_MOE
    assert t_l % TM_MOE == 0 and (t_c * n_bc) % TM_MOE == 0

    yp = x_prompt.reshape(n_bc * t_c, D)
    ys = x_sample.reshape(n_bl * t_l, D)
    caches = []
    for l in range(DEPTH):
        lw = _layer_weights(l, params)
        mod3 = _mod_call(cond, lw["w_mod"], lw["b_mod"]).reshape(n_cond, 1, 6 * D)
        yp, cache = _mixers(False, l, yp, mod3, lambda i: 0, lw, n_bc, t_c)
        yp = _moe_call(yp, mod3, lambda i: 0, lw)
        caches.append(cache)
        past_kvm = _pastkv_call(cache_mla_ckv[:, l].reshape(n_bl * p_len, -1),
                                jnp.tile(cache_mla_kpe[:, l].reshape(n_bl * p_len, -1), (1, 4)),
                                lw["w_ukv"]).reshape(n_bl, p_len, -1)
        past_gkv = jnp.concatenate([cache_gqa_k[:, l].reshape(n_bl, p_len, -1),
                                    cache_gqa_v[:, l].reshape(n_bl, p_len, -1)], axis=-1).astype(BF16)
        past_dkv = jnp.concatenate([cache_diff_k[:, l].reshape(n_bl, p_len, -1),
                                    cache_diff_v[:, l].reshape(n_bl, p_len, -1)], axis=-1).astype(BF16)
        s0 = state_ret[:, l].reshape(n_bl, 2, RET_HEADS // 2, 2 * RET_DK, RET_DV)
        ys, _ = _mixers(True, l, ys, mod3, lambda i: 1 + i // blk_l, lw, n_bl, t_l, tabs=tabs,
                        past=(past_kvm, past_gkv, past_dkv), s0=s0)
        ys = _moe_call(ys, mod3, lambda i: 1 + i // (t_l // TM_MOE), lw)

    def stack(k, shape):
        return jnp.stack([caches[l][k].reshape((n_bc, t_c) + shape) for l in range(DEPTH)], axis=1)

    new_ret = jnp.stack([caches[l][6] for l in range(DEPTH)], axis=1)
    return (yp.reshape(n_bc, t_c, D), ys.reshape(n_bl, t_l, D),
            stack(0, (MLA_KV_LORA,)), stack(1, (MLA_ROPE,)),
            stack(2, (GQA_KV_HEADS, GQA_HD)), stack(3, (GQA_KV_HEADS, GQA_HD)),
            stack(4, (DIFF_HEADS, 2, DIFF_D)), stack(5, (DIFF_HEADS, DIFF_DV)), new_ret)
```

```python
import functools
import math

import numpy as np
import jax
import jax.numpy as jnp
from jax import lax
from jax.experimental import pallas as pl
from jax.experimental.pallas import tpu as pltpu

F32 = jnp.float32
BF16 = jnp.bfloat16

D = 1024
DEPTH = 2
GRID_W = 64
ROPE_BASE = 10000.0
EPS = 1e-6

MLA_HEADS, MLA_NOPE, MLA_ROPE, MLA_V = 8, 64, 32, 64
MLA_Q_LORA, MLA_KV_LORA = 384, 256
GQA_HEADS, GQA_KV_HEADS, GQA_HD = 8, 2, 64
RET_HEADS, RET_DK, RET_DV = 4, 64, 128
DIFF_HEADS, DIFF_D, DIFF_DV = 4, 64, 128
N_BRANCH, BRANCH_W = 4, 512
N_EXPERTS, TOP_K, N_GROUPS, TOPK_GROUPS = 32, 4, 4, 2
EXPERT_FF, SHARED_FF = 256, 256
ROUTE_SCALE = 2.5
GROUP_SIZE = N_EXPERTS // N_GROUPS

LANES = 128
HALF_LANES = 64
VMEM_LIMIT = 56 * 1024 * 1024

C_CQ, C_CKV, C_KPE, C_GQ, C_GK, C_GV = 0, 384, 640, 768, 1280, 1408
C_DQ, C_DK, C_DV, C_RQ, C_RK, C_RV, C_RG, C_GL, C_END = (
    1536, 2048, 2560, 3072, 3328, 3584, 4096, 4608, 8704)
O_CQ, O_CKV, O_KPE, O_GQ, O_GK, O_GV = 0, 384, 640, 672, 1184, 1312
O_RQ, O_RK, O_RV, O_RG, O_DQ, O_DK, O_DV, O_GL, O_END = (
    1440, 1696, 1952, 2464, 2976, 3488, 4000, 4512, 8608)
GQA_ORDER = (0, 4, 1, 5, 2, 6, 3, 7)

KVM_W = 4 * 256 + 512
TM = 256
TQ = 256
TM_MOE = 1024


def _cparams(sem):
    return pltpu.CompilerParams(dimension_semantics=sem, vmem_limit_bytes=VMEM_LIMIT)


def _const_spec(shape):
    nd = len(shape)
    return pl.BlockSpec(shape, lambda *_: (0,) * nd)


def _rms(x, g):
    return x * lax.rsqrt(jnp.mean(x * x, axis=-1, keepdims=True) + EPS) * g


def _dot(a, b):
    return jnp.dot(a, b, preferred_element_type=F32)


def _dot_nt(a, b):
    return lax.dot_general(a, b, (((1,), (1,)), ((), ())), preferred_element_type=F32)


def _silu(x):
    return x * jax.nn.sigmoid(x)


def _lane_iota(shape):
    return lax.broadcasted_iota(jnp.int32, shape, len(shape) - 1)


def _seg_meansq(x, bd_ref, width):
    sq = x * x
    hi = sq.astype(BF16)
    lo = (sq - hi.astype(F32)).astype(BF16)
    bd = bd_ref[0:width, 0:width]
    return (_dot(hi, bd) + _dot(lo, bd)) * (1.0 / GQA_HD)


def _rope(x, cos, sin_signed, half):
    width = x.shape[-1]
    first = (_lane_iota(x.shape) % (2 * half)) < half
    partner = jnp.where(first, pltpu.roll(x, width - half, 1), pltpu.roll(x, half, 1))
    return x * cos + partner * sin_signed


def _tile_lanes(t, reps):
    return t if reps == 1 else jnp.concatenate([t] * reps, axis=1)


def _mod_kernel(c_ref, w_ref, b_ref, o_ref):
    a = _silu(c_ref[...]).astype(BF16)
    o_ref[...] = _dot(a, w_ref[...].astype(BF16)) + b_ref[...]


def _mod_call(l, cond, w_mod, b_mod):
    n_l, _, n = w_mod.shape
    tn = 1536
    return pl.pallas_call(
        _mod_kernel,
        grid=(n // tn,),
        in_specs=[_const_spec(cond.shape),
                  pl.BlockSpec((None, D, tn), lambda j: (l, 0, j)),
                  pl.BlockSpec((None, 1, tn), lambda j: (l, 0, j))],
        out_specs=pl.BlockSpec((cond.shape[0], tn), lambda j: (0, j)),
        out_shape=jax.ShapeDtypeStruct((cond.shape[0], n), F32),
        compiler_params=_cparams(("arbitrary",)),
        name="mod",
    )(cond, w_mod, b_mod.reshape(n_l, 1, n))


_WIN_SEGMENTS = (
    [(C_CQ, O_CQ, O_KPE - O_CQ)]
    + [(C_GQ + g * LANES + j * HALF_LANES, O_GQ + GQA_ORDER[2 * g + j] * GQA_HD, GQA_HD)
       for g in range(4) for j in range(2)]
    + [(C_GK, O_GK, O_RQ - O_GK), (C_DQ, O_DQ, O_GL - O_DQ), (C_RQ, O_RQ, O_DQ - O_RQ),
       (C_GL, O_GL, O_END - O_GL)])


def _wprep_kernel(w_ref, o_ref):
    for dst, src, width in _WIN_SEGMENTS:
        o_ref[:, dst:dst + width] = w_ref[:, src:src + width].astype(BF16)
    grp = w_ref[:, O_KPE:O_KPE + LANES]
    kpe = jnp.where(_lane_iota(grp.shape) < MLA_ROPE, grp, 0.0)
    rep = kpe
    for s in range(1, 4):
        rep = rep + pltpu.roll(kpe, s * MLA_ROPE, 1)
    o_ref[:, C_KPE:C_GQ] = rep.astype(BF16)


def _wprep_call(w_in):
    n_l, _, n_in = w_in.shape
    tr = 128
    return pl.pallas_call(
        _wprep_kernel,
        grid=(n_l, D // tr),
        in_specs=[pl.BlockSpec((None, tr, n_in), lambda l, i: (l, i, 0))],
        out_specs=pl.BlockSpec((None, tr, C_END), lambda l, i: (l, i, 0)),
        out_shape=jax.ShapeDtypeStruct((n_l, D, C_END), BF16),
        compiler_params=_cparams(("arbitrary", "arbitrary")),
        name="wprep",
    )(w_in)


def _inprep_kernel(latent, *refs):
    (x_ref, mod_ref, gpre_ref, win_ref, gmq_ref, wuq_ref, gmkv_ref, wukv_ref,
     ggq_ref, ggk_ref, bd_ref) = refs[:11]
    refs = refs[11:]
    if latent:
        cos64_ref, sin64_ref, cospe_ref, sinpe_ref = refs[:4]
        refs = refs[4:]
    (qm_ref, kvm_ref, gqo_ref, gkv_ref, dqo_ref, dkv_ref, ret_ref, rg_ref, gate_ref) = refs[:9]
    refs = refs[9:]
    if not latent:
        ckv_o, kpe_o, gk_o, gv_o, dk_o, dv_o = refs

    x = x_ref[...]
    mod = mod_ref[...]
    sh1 = mod[:, 0:D]
    sc1 = mod[:, D:2 * D]
    hb = (_rms(x, gpre_ref[...]) * (1.0 + sc1) + sh1).astype(BF16)

    def z(a, b):
        return _dot(hb, win_ref[:, a:b])

    if latent:
        cos64, sin64 = cos64_ref[...], sin64_ref[...]
        cospe, sinpe = cospe_ref[...], sinpe_ref[...]

    cqn = _rms(z(C_CQ, C_CKV), gmq_ref[...]).astype(BF16)
    q = _dot(cqn, wuq_ref[...]) * ((MLA_NOPE + MLA_ROPE) ** -0.5)
    q_nope, q_pe = q[:, 0:512], q[:, 512:768]
    if latent:
        q_pe = _rope(q_pe, _tile_lanes(cospe, 2), _tile_lanes(sinpe, 2), MLA_ROPE // 4)
    qm_ref[:, 0:512] = q_nope.astype(BF16)
    qm_ref[:, 512:768] = q_pe.astype(BF16)

    ckvn = _rms(z(C_CKV, C_KPE), gmkv_ref[...])
    kv = _dot(ckvn.astype(BF16), wukv_ref[...])
    kpe4 = z(C_KPE, C_GQ)
    if latent:
        kpe4 = _rope(kpe4, cospe, sinpe, MLA_ROPE // 4)
    else:
        ckv_o[...] = ckvn
        kpe_o[...] = kpe4[:, 0:MLA_ROPE]
    kpe_b = kpe4.astype(BF16)
    for p in range(4):
        kvm_ref[:, p * 256:p * 256 + LANES] = kv[:, p * LANES:(p + 1) * LANES].astype(BF16)
        kvm_ref[:, p * 256 + LANES:(p + 1) * 256] = kpe_b
    kvm_ref[:, 1024:1536] = kv[:, 512:1024].astype(BF16)

    gq = z(C_GQ, C_GK)
    gq = gq * lax.rsqrt(_seg_meansq(gq, bd_ref, 512) + EPS) * ggq_ref[...]
    gk = z(C_GK, C_GV)
    gk = gk * lax.rsqrt(_seg_meansq(gk, bd_ref, LANES) + EPS) * ggk_ref[...]
    gv = z(C_GV, C_DQ)
    if latent:
        gq = _rope(gq, _tile_lanes(cos64, 4), _tile_lanes(sin64, 4), GQA_HD // 4)
        gk = _rope(gk, cos64, sin64, GQA_HD // 4)
    else:
        gk_o[...] = gk
        gv_o[...] = gv
    gqo_ref[...] = (gq * (GQA_HD ** -0.5)).astype(BF16)
    gkv_ref[:, 0:LANES] = gk.astype(BF16)
    gkv_ref[:, LANES:2 * LANES] = gv.astype(BF16)

    dq = z(C_DQ, C_DK)
    dk = z(C_DK, C_DV)
    dv = z(C_DV, C_RQ)
    if latent:
        dq = _rope(dq, _tile_lanes(cos64, 4), _tile_lanes(sin64, 4), DIFF_D // 4)
        dk = _rope(dk, _tile_lanes(cos64, 4), _tile_lanes(sin64, 4), DIFF_D // 4)
    else:
        dk_o[...] = dk
        dv_o[...] = dv
    dqo_ref[...] = (dq * (DIFF_D ** -0.5)).astype(BF16)
    dkv_ref[:, 0:512] = dk.astype(BF16)
    dkv_ref[:, 512:1024] = dv.astype(BF16)

    ret_ref[:, 0:256] = z(C_RQ, C_RK).astype(BF16)
    ret_ref[:, 256:512] = (z(C_RK, C_RV) * (RET_DK ** -0.5)).astype(BF16)
    ret_ref[:, 512:1024] = z(C_RV, C_RG).astype(BF16)
    rg_ref[...] = z(C_RG, C_GL).astype(BF16)

    for n in range(N_BRANCH):
        gate_ref[:, n * D:(n + 1) * D] = jax.nn.sigmoid(
            z(C_GL + n * D, C_GL + (n + 1) * D)).astype(BF16)


def _inprep_call(latent, l, x, mod3, mod_row, lw, tabs, t_len):
    n_tok = x.shape[0]
    nblk = n_tok // TM
    blk_per_seq = t_len // TM

    def tok(w):
        return pl.BlockSpec((TM, w), lambda i: (i, 0))

    in_specs = [tok(D),
                pl.BlockSpec((None, 1, 6 * D), lambda i: (mod_row(i), 0, 0)),
                _const_spec((1, D)),
                pl.BlockSpec((None, D, C_END), lambda i: (l, 0, 0), pipeline_mode=pl.Buffered(1)),
                _const_spec((1, MLA_Q_LORA)), _const_spec((MLA_Q_LORA, 768)),
                _const_spec((1, MLA_KV_LORA)), _const_spec((MLA_KV_LORA, 1024)),
                _const_spec((1, 512)), _const_spec((1, LANES)), _const_spec((512, 512))]
    args = [x, mod3, lw["g_pre1"], lw["w_in"], lw["g_mla_q"], lw["w_uq"], lw["g_mla_kv"],
            lw["w_ukv"], lw["g_gqa_q"], lw["g_gqa_k"], lw["bd"]]
    if latent:
        tab_spec = pl.BlockSpec((TM, LANES), lambda i: (i % blk_per_seq, 0))
        in_specs += [tab_spec] * 4
        args += list(tabs)
    widths = [768, KVM_W, 512, 256, 512, 1024, 1024, 512, 4 * D]
    out_specs = [tok(w) for w in widths]
    out_shape = [jax.ShapeDtypeStruct((n_tok, w), BF16) for w in widths]
    if not latent:
        cw = [MLA_KV_LORA, MLA_ROPE, 128, 128, 512, 512]
        out_specs += [tok(w) for w in cw]
        out_shape += [jax.ShapeDtypeStruct((n_tok, w), F32) for w in cw]
    return pl.pallas_call(
        functools.partial(_inprep_kernel, latent),
        grid=(nblk,),
        in_specs=in_specs, out_specs=out_specs, out_shape=out_shape,
        compiler_params=_cparams(("arbitrary",)),
        name="inprep_lat" if latent else "inprep_ctx",
    )(*args)


def _pastkv_kernel(ckv_ref, kpe_ref, wukv_ref, o_ref):
    kv = _dot(ckv_ref[...].astype(BF16), wukv_ref[...])
    kpe_b = kpe_ref[...].astype(BF16)
    for p in range(4):
        o_ref[:, p * 256:p * 256 + LANES] = kv[:, p * LANES:(p + 1) * LANES].astype(BF16)
        o_ref[:, p * 256 + LANES:(p + 1) * 256] = kpe_b
    o_ref[:, 1024:1536] = kv[:, 512:1024].astype(BF16)


def _pastkv_call(ckv, kpe4, w_ukv):
    n = ckv.shape[0]
    return pl.pallas_call(
        _pastkv_kernel,
        grid=(n // TM,),
        in_specs=[pl.BlockSpec((TM, MLA_KV_LORA), lambda i: (i, 0)),
                  pl.BlockSpec((TM, LANES), lambda i: (i, 0)),
                  _const_spec((MLA_KV_LORA, 1024))],
        out_specs=pl.BlockSpec((TM, KVM_W), lambda i: (i, 0)),
        out_shape=jax.ShapeDtypeStruct((n, KVM_W), BF16),
        compiler_params=_cparams(("arbitrary",)),
        name="pastkv",
    )(ckv, kpe4, w_ukv)


def _softmax_pv(s, v):
    m = jnp.max(s, axis=-1, keepdims=True)
    p = jnp.exp(s - m)
    l = jnp.sum(p, axis=-1, keepdims=True)
    return _dot(p.astype(BF16), v) / l


def _attn_kernel(lam_init, qm_ref, kvm_ref, gq_ref, gkv_ref, dq_ref, dkv_ref,
                 lam_ref, gdiff_ref, o_ref):
    tq = qm_ref.shape[0]
    lane = _lane_iota((tq, LANES))
    low = lane < HALF_LANES
    zero = jnp.zeros((tq, LANES), BF16)

    for p in range(MLA_HEADS // 2):
        qn = qm_ref[:, p * LANES:(p + 1) * LANES]
        g = p // 2
        qpe = qm_ref[:, 512 + g * LANES:512 + (g + 1) * LANES]
        kk = kvm_ref[:, p * 256:(p + 1) * 256]
        vv = kvm_ref[:, 1024 + p * LANES:1024 + (p + 1) * LANES]
        outs = []
        for half in range(2):
            h = 2 * p + half
            slot = h % 4
            in_slot = (lane >= slot * MLA_ROPE) & (lane < (slot + 1) * MLA_ROPE)
            lhs = jnp.concatenate(
                [jnp.where(low if half == 0 else ~low, qn, zero),
                 jnp.where(in_slot, qpe, zero)], axis=1)
            outs.append(_softmax_pv(_dot_nt(lhs, kk), vv))
        o_ref[:, p * LANES:(p + 1) * LANES] = jnp.where(low, outs[0], outs[1]).astype(BF16)

    kk = gkv_ref[:, 0:LANES]
    vv = gkv_ref[:, LANES:2 * LANES]
    for g in range(GQA_HEADS // 2):
        qg = gq_ref[:, g * LANES:(g + 1) * LANES]
        o_lo = _softmax_pv(_dot_nt(jnp.where(low, qg, zero), kk), vv)
        o_hi = _softmax_pv(_dot_nt(jnp.where(low, zero, qg), kk), vv)
        o_ref[:, 512 + g * LANES:512 + (g + 1) * LANES] = jnp.where(low, o_lo, o_hi).astype(BF16)

    lp = lam_ref[...]
    lam = (jnp.exp(jnp.sum(lp[0:1] * lp[1:2], axis=-1, keepdims=True))
           - jnp.exp(jnp.sum(lp[2:3] * lp[3:4], axis=-1, keepdims=True)) + lam_init)
    for h in range(DIFF_HEADS):
        qh = dq_ref[:, h * LANES:(h + 1) * LANES]
        kk = dkv_ref[:, h * LANES:(h + 1) * LANES]
        vv = dkv_ref[:, 512 + h * LANES:512 + (h + 1) * LANES]
        a1 = _softmax_pv(_dot_nt(jnp.where(low, qh, zero), kk), vv)
        a2 = _softmax_pv(_dot_nt(jnp.where(low, zero, qh), kk), vv)
        od = _rms(a1 - lam * a2, gdiff_ref[...]) * (1.0 - lam_init)
        o_ref[:, 1024 + h * LANES:1024 + (h + 1) * LANES] = od.astype(BF16)


def _attn_call(lam_init, qm, kvm, gq, gkv, dq, dkv, lam_p, g_diff, n_b, t_len, s_len):
    nq = t_len // TQ

    def qspec(w):
        return pl.BlockSpec((TQ, w), lambda b, i: (b * nq + i, 0))

    def kspec(w):
        return pl.BlockSpec((s_len, w), lambda b, i: (b, 0))

    return pl.pallas_call(
        functools.partial(_attn_kernel, lam_init),
        grid=(n_b, nq),
        in_specs=[qspec(768), kspec(KVM_W), qspec(512), kspec(256), qspec(512), kspec(1024),
                  _const_spec((4, DIFF_D)), _const_spec((1, DIFF_DV))],
        out_specs=qspec(3 * BRANCH_W),
        out_shape=jax.ShapeDtypeStruct((n_b * t_len, 3 * BRANCH_W), BF16),
        compiler_params=_cparams(("arbitrary", "arbitrary")),
        name="attn",
    )(qm, kvm, gq, gkv, dq, dkv, lam_p, g_diff)


def _log_sigmoid(x):
    return jnp.minimum(x, 0.0) - jnp.log(1.0 + jnp.exp(-jnp.abs(x)))


def _ret_kernel(latent, t_len, dec_ref, q_ref, k_ref, v_ref, rg_ref, gret_ref, *refs):
    if latent:
        s0_ref, o_ref = refs
    else:
        o_ref, st_ref = refs
    tq = q_ref.shape[0]
    t0 = pl.program_id(1) * tq
    lane = _lane_iota((tq, LANES))
    low = lane < HALF_LANES
    zero = jnp.zeros((tq, LANES), BF16)
    t_idx = (t0 + lax.broadcasted_iota(jnp.int32, (tq, t_len), 0)).astype(F32)
    s_idx = lax.broadcasted_iota(jnp.int32, (tq, t_len), 1).astype(F32)
    dist = t_idx - s_idx
    t_col = (t0 + lax.broadcasted_iota(jnp.int32, (tq, 1), 0)).astype(F32)

    def lg(d, h):
        return _log_sigmoid(jnp.full((1, 1), dec_ref[d, h], F32))

    for h in range(RET_HEADS):
        p, half = h // 2, h % 2
        qp = q_ref[:, p * LANES:(p + 1) * LANES]
        qm = jnp.where(low if half == 0 else ~low, qp, zero)
        kp = k_ref[:, p * LANES:(p + 1) * LANES]
        vh = v_ref[:, h * LANES:(h + 1) * LANES]
        lgf, lgb = lg(0, h), lg(1, h)
        dmask = (jnp.where(dist >= 0, jnp.exp(lgf * jnp.maximum(dist, 0.0)), 0.0)
                 + jnp.where(dist <= 0, jnp.exp(lgb * jnp.maximum(-dist, 0.0)), 0.0))
        o = _dot((_dot_nt(qm, kp) * dmask).astype(BF16), vh)
        if latent:
            sf = s0_ref[0, p].astype(BF16)
            sb = s0_ref[1, p].astype(BF16)
            o = o + _dot(qm, sf) * jnp.exp(lgf * (t_col + 1.0))
            o = o + _dot(qm, sb) * jnp.exp(lgb * (float(t_len) - t_col))
        mu = jnp.mean(o, axis=-1, keepdims=True)
        oc = o - mu
        y = oc * lax.rsqrt(jnp.mean(oc * oc, axis=-1, keepdims=True) + EPS)
        y = y * gret_ref[:, h * LANES:(h + 1) * LANES]
        rg = rg_ref[:, h * LANES:(h + 1) * LANES].astype(F32)
        o_ref[:, h * LANES:(h + 1) * LANES] = (y * _silu(rg)).astype(BF16)

    if not latent:
        s_col = lax.broadcasted_iota(jnp.int32, (t_len, 1), 0).astype(F32)
        lane_t = _lane_iota((1, LANES)) < HALF_LANES
        for p in range(RET_HEADS // 2):
            kp = k_ref[:, p * LANES:(p + 1) * LANES].astype(F32)
            for d in range(2):
                lg_lane = jnp.where(lane_t, lg(d, 2 * p), lg(d, 2 * p + 1))
                expo = (float(t_len) - 1.0 - s_col) if d == 0 else s_col
                kdec_t = jnp.transpose(kp * jnp.exp(lg_lane * expo)).astype(BF16)
                for half in range(2):
                    h = 2 * p + half
                    st = _dot(kdec_t, v_ref[:, h * LANES:(h + 1) * LANES])
                    st_ref[d, h] = st[half * RET_DK:(half + 1) * RET_DK, :]


def _ret_call(latent, dec, ret, rg, g_ret, s0, n_b, t_len):
    nq = t_len // TQ
    assert latent or nq == 1
    in_specs = [pl.BlockSpec(memory_space=pltpu.SMEM),
                pl.BlockSpec((TQ, 256), lambda b, i: (b * nq + i, 0)),
                pl.BlockSpec((t_len, 256), lambda b, i: (b, 1)),
                pl.BlockSpec((t_len, 512), lambda b, i: (b, 1)),
                pl.BlockSpec((TQ, 512), lambda b, i: (b * nq + i, 0)),
                _const_spec((1, 512))]
    args = [dec, ret, ret, ret, rg, g_ret]
    out_specs = [pl.BlockSpec((TQ, 512), lambda b, i: (b * nq + i, 0))]
    out_shape = [jax.ShapeDtypeStruct((n_b * t_len, 512), BF16)]
    if latent:
        in_specs.append(pl.BlockSpec((None, 2, 2, LANES, LANES), lambda b, i: (b, 0, 0, 0, 0)))
        args.append(s0)
    else:
        out_specs.append(pl.BlockSpec((None, 2, RET_HEADS, RET_DK, RET_DV),
                                      lambda b, i: (b, 0, 0, 0, 0)))
        out_shape.append(jax.ShapeDtypeStruct((n_b, 2, RET_HEADS, RET_DK, RET_DV), F32))
    return pl.pallas_call(
        functools.partial(_ret_kernel, latent, t_len),
        grid=(n_b, nq),
        in_specs=in_specs, out_specs=out_specs, out_shape=out_shape,
        compiler_params=_cparams(("arbitrary", "arbitrary")),
        name="ret_lat" if latent else "ret_ctx",
    )(*args)


def _merge_kernel(x_ref, mod_ref, br_ref, or_ref, gate_ref, wbr_ref, wout_ref, gpost_ref, o_ref):
    merged = None
    for n in range(N_BRANCH):
        if n < 2:
            b = br_ref[:, n * BRANCH_W:(n + 1) * BRANCH_W]
        elif n == 2:
            b = or_ref[...]
        else:
            b = br_ref[:, 2 * BRANCH_W:3 * BRANCH_W]
        t = gate_ref[:, n * D:(n + 1) * D].astype(F32) * _dot(b, wbr_ref[n])
        merged = t if merged is None else merged + t
    out = _dot(merged.astype(BF16), wout_ref[...])
    g1 = mod_ref[...][:, 2 * D:3 * D]
    o_ref[...] = x_ref[...] + g1 * _rms(out, gpost_ref[...])


def _merge_call(x, mod3, mod_row, br, o_r, gates, lw):
    n_tok = x.shape[0]

    def tok(w):
        return pl.BlockSpec((TM, w), lambda i: (i, 0))

    return pl.pallas_call(
        _merge_kernel,
        grid=(n_tok // TM,),
        in_specs=[tok(D), pl.BlockSpec((None, 1, 6 * D), lambda i: (mod_row(i), 0, 0)),
                  tok(3 * BRANCH_W), tok(BRANCH_W), tok(4 * D),
                  _const_spec((N_BRANCH, BRANCH_W, D)), _const_spec((D, D)), _const_spec((1, D))],
        out_specs=tok(D),
        out_shape=jax.ShapeDtypeStruct((n_tok, D), F32),
        compiler_params=_cparams(("arbitrary",)),
        name="merge",
    )(x, mod3, br, o_r, gates, lw["w_br"], lw["w_out"], lw["g_post1"])


def _route(logits_t, bias):
    n = logits_t.shape[1]
    scores = jax.nn.sigmoid(logits_t)
    sel = scores + bias
    neg = -jnp.inf
    sub = lax.broadcasted_iota(jnp.int32, (GROUP_SIZE, n), 0)
    grp = []
    for g in range(N_GROUPS):
        blk = sel[g * GROUP_SIZE:(g + 1) * GROUP_SIZE]
        m1 = jnp.max(blk, axis=0, keepdims=True)
        i1 = jnp.min(jnp.where(blk == m1, sub, GROUP_SIZE), axis=0, keepdims=True)
        m2 = jnp.max(jnp.where(sub == i1, neg, blk), axis=0, keepdims=True)
        grp.append(m1 + m2)
    parts = []
    for g in range(N_GROUPS):
        beaten = jnp.zeros((1, n), jnp.int32)
        for o in range(N_GROUPS):
            if o == g:
                continue
            wins = (grp[o] > grp[g]) | (grp[o] == grp[g]) if o < g else (grp[o] > grp[g])
            beaten = beaten + wins.astype(jnp.int32)
        keep = beaten < TOPK_GROUPS
        parts.append(jnp.where(keep, sel[g * GROUP_SIZE:(g + 1) * GROUP_SIZE], neg))
    cur = jnp.concatenate(parts, axis=0)
    eidx = lax.broadcasted_iota(jnp.int32, (N_EXPERTS, n), 0)
    comb = jnp.zeros((N_EXPERTS, n), F32)
    for _ in range(TOP_K):
        m = jnp.max(cur, axis=0, keepdims=True)
        i = jnp.min(jnp.where(cur == m, eidx, N_EXPERTS), axis=0, keepdims=True)
        hit = eidx == i
        comb = jnp.where(hit, scores, comb)
        cur = jnp.where(hit, neg, cur)
    wsum = jnp.sum(comb, axis=0, keepdims=True)
    return comb / wsum * ROUTE_SCALE


def _moe_kernel(x_ref, mod_ref, gpre_ref, wr_ref, br_ref, wgu_ref, wdn_ref, wsgu_ref, wsdn_ref,
                gpost_ref, o_ref, hb_ref, comb_ref, acc_ref):
    e = pl.program_id(1)
    tm = x_ref.shape[0]

    @pl.when(e == 0)
    def _():
        mod = mod_ref[...]
        sh2, sc2 = mod[:, 3 * D:4 * D], mod[:, 4 * D:5 * D]
        h = _rms(x_ref[...], gpre_ref[...]) * (1.0 + sc2) + sh2
        hb = h.astype(BF16)
        hb_ref[...] = hb
        h_lo = (h - hb.astype(F32)).astype(BF16)
        wr = wr_ref[...]
        wr_hi = wr.astype(BF16)
        wr_lo = (wr - wr_hi.astype(F32)).astype(BF16)
        logits_t = _dot_nt(wr_hi, hb) + _dot_nt(wr_hi, h_lo) + _dot_nt(wr_lo, hb)
        comb_t = _route(logits_t, br_ref[...])
        pad = jnp.zeros((LANES - N_EXPERTS, tm), F32)
        comb_ref[...] = jnp.transpose(jnp.concatenate([comb_t, pad], axis=0))
        sgu = _dot(hb, wsgu_ref[...])
        sa = _silu(sgu[:, 0:SHARED_FF]) * sgu[:, SHARED_FF:2 * SHARED_FF]
        acc_ref[...] = _dot(sa.astype(BF16), wsdn_ref[...])

    hb = hb_ref[...]
    gu = _dot(hb, wgu_ref[...].astype(BF16))
    onehot = _lane_iota((1, LANES)) == e
    c = jnp.sum(jnp.where(onehot, comb_ref[...], 0.0), axis=-1, keepdims=True)
    a = _silu(gu[:, 0:EXPERT_FF]) * gu[:, EXPERT_FF:2 * EXPERT_FF] * c
    acc_ref[...] += _dot(a.astype(BF16), wdn_ref[...].astype(BF16))

    @pl.when(e == N_EXPERTS - 1)
    def _():
        g2 = mod_ref[...][:, 5 * D:6 * D]
        o_ref[...] = x_ref[...] + g2 * _rms(acc_ref[...], gpost_ref[...])


def _moe_call(l, x, mod3, mod_row, lw):
    n_tok = x.shape[0]
    tm = TM_MOE
    return pl.pallas_call(
        _moe_kernel,
        grid=(n_tok // tm, N_EXPERTS),
        in_specs=[pl.BlockSpec((tm, D), lambda i, e: (i, 0)),
                  pl.BlockSpec((None, 1, 6 * D), lambda i, e: (mod_row(i), 0, 0)),
                  _const_spec((1, D)), _const_spec((N_EXPERTS, D)), _const_spec((N_EXPERTS, 1)),
                  pl.BlockSpec((None, None, D, 2 * EXPERT_FF), lambda i, e: (l, e, 0, 0)),
                  pl.BlockSpec((None, None, EXPERT_FF, D), lambda i, e: (l, e, 0, 0)),
                  _const_spec((D, 2 * SHARED_FF)), _const_spec((SHARED_FF, D)),
                  _const_spec((1, D))],
        out_specs=pl.BlockSpec((tm, D), lambda i, e: (i, 0)),
        out_shape=jax.ShapeDtypeStruct((n_tok, D), F32),
        scratch_shapes=[pltpu.VMEM((tm, D), BF16), pltpu.VMEM((tm, LANES), F32),
                        pltpu.VMEM((tm, D), F32)],
        compiler_params=_cparams(("arbitrary", "arbitrary")),
        name="moe",
    )(x, mod3, lw["g_pre2"], lw["w_router_t"], lw["b_router"], lw["w_exp_gu"], lw["w_exp_down"],
      lw["w_sh_gu"], lw["w_sh_down"], lw["g_post2"])


def _rope_tables(t_len):
    pos = np.arange(t_len)
    row, col = pos // GRID_W, pos % GRID_W

    def tab(r):
        half = r // 2
        freq = ROPE_BASE ** (-np.arange(half, dtype=np.float64) / half)
        sign = np.concatenate([-np.ones(half), np.ones(half)])
        cs, sn = [], []
        for p in (row, col):
            ang = p[:, None].astype(np.float64) * freq[None, :]
            cs.append(np.concatenate([np.cos(ang), np.cos(ang)], axis=1))
            sn.append(np.concatenate([np.sin(ang), np.sin(ang)], axis=1) * sign[None, :])
        return np.concatenate(cs, axis=1), np.concatenate(sn, axis=1)

    c64, s64 = tab(GQA_HD // 2)
    cpe, spe = tab(MLA_ROPE // 2)
    out = (np.tile(c64, (1, 2)), np.tile(s64, (1, 2)), np.tile(cpe, (1, 4)), np.tile(spe, (1, 4)))
    return tuple(jnp.asarray(a, F32) for a in out)


def _layer_weights(l, p):
    w_uq = p["w_mla_uq"][l].reshape(MLA_Q_LORA, MLA_HEADS, MLA_NOPE + MLA_ROPE)
    w_uq = jnp.concatenate([w_uq[:, :, :MLA_NOPE].reshape(MLA_Q_LORA, -1),
                            w_uq[:, :, MLA_NOPE:].reshape(MLA_Q_LORA, -1)], axis=1)
    w_ukv = p["w_mla_ukv"][l].reshape(MLA_KV_LORA, MLA_HEADS, MLA_NOPE + MLA_V)
    w_ukv = jnp.concatenate([w_ukv[:, :, :MLA_NOPE].reshape(MLA_KV_LORA, -1),
                             w_ukv[:, :, MLA_NOPE:].reshape(MLA_KV_LORA, -1)], axis=1)
    w_br = p["w_br"][l]
    w_br_gqa = w_br[1].reshape(GQA_HEADS, GQA_HD, D)[jnp.array(GQA_ORDER)].reshape(BRANCH_W, D)
    w_br = jnp.stack([w_br[0], w_br_gqa, w_br[2], w_br[3]], axis=0)
    blk = np.arange(512) // GQA_HD
    return {
        "g_pre1": p["g_pre1"][l].reshape(1, D), "g_post1": p["g_post1"][l].reshape(1, D),
        "g_pre2": p["g_pre2"][l].reshape(1, D), "g_post2": p["g_post2"][l].reshape(1, D),
        "w_in": p["w_in_packed"],
        "g_mla_q": p["g_mla_q"][l].reshape(1, -1), "w_uq": w_uq.astype(BF16),
        "g_mla_kv": p["g_mla_kv"][l].reshape(1, -1), "w_ukv": w_ukv.astype(BF16),
        "g_gqa_q": jnp.tile(p["g_gqa_q"][l], GQA_HEADS).reshape(1, -1),
        "g_gqa_k": jnp.tile(p["g_gqa_k"][l], GQA_KV_HEADS).reshape(1, -1),
        "bd": jnp.asarray(blk[:, None] == blk[None, :], BF16),
        "ret_decay": p["ret_decay"][l],
        "g_ret": p["g_ret"][l].reshape(1, -1),
        "diff_lambda": p["diff_lambda"][l], "g_diff": p["g_diff"][l].reshape(1, -1),
        "w_br": w_br.astype(BF16), "w_out": p["w_out"][l].astype(BF16),
        "w_router_t": p["w_router"][l].T, "b_router": p["b_router"][l].reshape(-1, 1),
        "w_exp_gu": p["w_exp_gu"], "w_exp_down": p["w_exp_down"],
        "w_sh_gu": p["w_sh_gu"][l].astype(BF16), "w_sh_down": p["w_sh_down"][l].astype(BF16),
    }


def _mixers(latent, l, x, mod3, mod_row, lw, n_b, t_len, tabs=None, past=None, s0=None):
    lam_init = 0.8 - 0.6 * math.exp(-0.3 * l)
    outs = _inprep_call(latent, l, x, mod3, mod_row, lw, tabs, t_len)
    qm, kvm, gq, gkv, dq, dkv, ret, rg, gates = outs[:9]
    s_len = t_len
    if latent:
        past_kvm, past_gkv, past_dkv = past
        p_len = past_gkv.shape[1]
        s_len = p_len + t_len

        def cat(a, b):
            return jnp.concatenate([a, b.reshape(n_b, t_len, -1)], axis=1).reshape(n_b * s_len, -1)

        kvm, gkv, dkv = cat(past_kvm, kvm), cat(past_gkv, gkv), cat(past_dkv, dkv)
    br = _attn_call(lam_init, qm, kvm, gq, gkv, dq, dkv, lw["diff_lambda"], lw["g_diff"],
                    n_b, t_len, s_len)
    r = _ret_call(latent, lw["ret_decay"], ret, rg, lw["g_ret"], s0, n_b, t_len)
    y = _merge_call(x, mod3, mod_row, br, r[0], gates, lw)
    cache = None if latent else tuple(outs[9:]) + (r[1],)
    return y, cache


def kernel(x_prompt, x_sample, cache_mla_ckv, cache_mla_kpe, cache_gqa_k, cache_gqa_v, cache_diff_k, cache_diff_v, state_ret, c, c_ctx, w_mod, b_mod, g_pre1, g_post1, g_pre2, g_post2, w_in, g_mla_q, w_mla_uq, g_mla_kv, w_mla_ukv, g_gqa_q, g_gqa_k, ret_decay, g_ret, diff_lambda, g_diff, w_br, w_out, w_router, b_router, w_exp_gu, w_exp_down, w_sh_gu, w_sh_down):
    params = dict(w_in_packed=_wprep_call(w_in), g_pre1=g_pre1, g_post1=g_post1, g_pre2=g_pre2,
                  g_post2=g_post2, g_mla_q=g_mla_q, w_mla_uq=w_mla_uq,
                  g_mla_kv=g_mla_kv, w_mla_ukv=w_mla_ukv, g_gqa_q=g_gqa_q, g_gqa_k=g_gqa_k,
                  ret_decay=ret_decay, g_ret=g_ret, diff_lambda=diff_lambda, g_diff=g_diff,
                  w_br=w_br, w_out=w_out, w_router=w_router, b_router=b_router,
                  w_exp_gu=w_exp_gu, w_exp_down=w_exp_down, w_sh_gu=w_sh_gu, w_sh_down=w_sh_down)
    n_bc, t_c, _ = x_prompt.shape
    n_bl, t_l, _ = x_sample.shape
    p_len = cache_mla_ckv.shape[2]
    tabs = _rope_tables(t_l)
    n_cond = 8
    cond = jnp.concatenate([c_ctx[None, :], c, jnp.zeros((n_cond - 1 - n_bl, D), F32)], axis=0)
    blk_c, blk_l = t_c // TM, t_l // TM
    moe_c, moe_l = t_c * n_bc // TM_MOE, t_l // TM_MOE
    assert t_l % TM_MOE == 0 and (t_c * n_bc) % TM_MOE == 0

    yp = x_prompt.reshape(n_bc * t_c, D)
    ys = x_sample.reshape(n_bl * t_l, D)
    caches = []
    for l in range(DEPTH):
        lw = _layer_weights(l, params)
        mod3 = _mod_call(l, cond, w_mod, b_mod).reshape(n_cond, 1, 6 * D)
        yp, cache = _mixers(False, l, yp, mod3, lambda i: 0, lw, n_bc, t_c)
        yp = _moe_call(l, yp, mod3, lambda i: 0, lw)
        caches.append(cache)
        past_kvm = _pastkv_call(cache_mla_ckv[:, l].reshape(n_bl * p_len, -1),
                                jnp.tile(cache_mla_kpe[:, l].reshape(n_bl * p_len, -1), (1, 4)),
                                lw["w_ukv"]).reshape(n_bl, p_len, -1)
        past_gkv = jnp.concatenate([cache_gqa_k[:, l].reshape(n_bl, p_len, -1),
                                    cache_gqa_v[:, l].reshape(n_bl, p_len, -1)], axis=-1).astype(BF16)
        past_dkv = jnp.concatenate([cache_diff_k[:, l].reshape(n_bl, p_len, -1),
                                    cache_diff_v[:, l].reshape(n_bl, p_len, -1)], axis=-1).astype(BF16)
        s0 = state_ret[:, l].reshape(n_bl, 2, RET_HEADS // 2, 2 * RET_DK, RET_DV)
        ys, _ = _mixers(True, l, ys, mod3, lambda i: 1 + i // blk_l, lw, n_bl, t_l, tabs=tabs,
                        past=(past_kvm, past_gkv, past_dkv), s0=s0)
        ys = _moe_call(l, ys, mod3, lambda i: 1 + i // (t_l // TM_MOE), lw)

    def stack(k, shape):
        return jnp.stack([caches[l][k].reshape((n_bc, t_c) + shape) for l in range(DEPTH)], axis=1)

    new_ret = jnp.stack([caches[l][6] for l in range(DEPTH)], axis=1)
    return (yp.reshape(n_bc, t_c, D), ys.reshape(n_bl, t_l, D),
            stack(0, (MLA_KV_LORA,)), stack(1, (MLA_ROPE,)),
            stack(2, (GQA_KV_HEADS, GQA_HD)), stack(3, (GQA_KV_HEADS, GQA_HD)),
            stack(4, (DIFF_HEADS, 2, DIFF_D)), stack(5, (DIFF_HEADS, DIFF_DV)), new_ret)
```

```python
import functools
import math

import numpy as np
import jax
import jax.numpy as jnp
from jax import lax
from jax.experimental import pallas as pl
from jax.experimental.pallas import tpu as pltpu

F32 = jnp.float32
BF16 = jnp.bfloat16

D = 1024
DEPTH = 2
GRID_W = 64
ROPE_BASE = 10000.0
EPS = 1e-6

MLA_HEADS, MLA_NOPE, MLA_ROPE, MLA_V = 8, 64, 32, 64
MLA_Q_LORA, MLA_KV_LORA = 384, 256
GQA_HEADS, GQA_KV_HEADS, GQA_HD = 8, 2, 64
RET_HEADS, RET_DK, RET_DV = 4, 64, 128
DIFF_HEADS, DIFF_D, DIFF_DV = 4, 64, 128
N_BRANCH, BRANCH_W = 4, 512
N_EXPERTS, TOP_K, N_GROUPS, TOPK_GROUPS = 32, 4, 4, 2
EXPERT_FF, SHARED_FF = 256, 256
ROUTE_SCALE = 2.5
GROUP_SIZE = N_EXPERTS // N_GROUPS

LANES = 128
HALF_LANES = 64
VMEM_LIMIT = 56 * 1024 * 1024

C_CQ, C_CKV, C_KPE, C_GQ, C_GK, C_GV = 0, 384, 640, 768, 1280, 1408
C_DQ, C_DK, C_DV, C_RQ, C_RK, C_RV, C_RG, C_GL, C_END = (
    1536, 2048, 2560, 3072, 3328, 3584, 4096, 4608, 8704)
O_CQ, O_CKV, O_KPE, O_GQ, O_GK, O_GV = 0, 384, 640, 672, 1184, 1312
O_RQ, O_RK, O_RV, O_RG, O_DQ, O_DK, O_DV, O_GL, O_END = (
    1440, 1696, 1952, 2464, 2976, 3488, 4000, 4512, 8608)
GQA_ORDER = (0, 4, 1, 5, 2, 6, 3, 7)

KVM_W = 4 * 256 + 512
TM = 256
TQ = 256
TM_MOE = 1024


def _cparams(sem):
    return pltpu.CompilerParams(dimension_semantics=sem, vmem_limit_bytes=VMEM_LIMIT)


def _const_spec(shape):
    nd = len(shape)
    return pl.BlockSpec(shape, lambda *_: (0,) * nd)


def _rms(x, g):
    return x * lax.rsqrt(jnp.mean(x * x, axis=-1, keepdims=True) + EPS) * g


def _dot(a, b):
    return jnp.dot(a, b, preferred_element_type=F32)


def _dot_nt(a, b):
    return lax.dot_general(a, b, (((1,), (1,)), ((), ())), preferred_element_type=F32)


def _silu(x):
    return x * jax.nn.sigmoid(x)


def _lane_iota(shape):
    return lax.broadcasted_iota(jnp.int32, shape, len(shape) - 1)


def _seg_meansq(x, bd_ref, width):
    sq = x * x
    hi = sq.astype(BF16)
    lo = (sq - hi.astype(F32)).astype(BF16)
    bd = bd_ref[0:width, 0:width]
    return (_dot(hi, bd) + _dot(lo, bd)) * (1.0 / GQA_HD)


def _rope(x, cos, sin_signed, half):
    width = x.shape[-1]
    first = (_lane_iota(x.shape) % (2 * half)) < half
    partner = jnp.where(first, pltpu.roll(x, width - half, 1), pltpu.roll(x, half, 1))
    return x * cos + partner * sin_signed


def _tile_lanes(t, reps):
    return t if reps == 1 else jnp.concatenate([t] * reps, axis=1)


def _mod_kernel(c_ref, w_ref, b_ref, o_ref):
    a = _silu(c_ref[...]).astype(BF16)
    o_ref[...] = _dot(a, w_ref[...].astype(BF16)) + b_ref[...]


def _mod_call(l, cond, w_mod, b_mod):
    n_l, _, n = w_mod.shape
    tn = 1536
    return pl.pallas_call(
        _mod_kernel,
        grid=(n // tn,),
        in_specs=[_const_spec(cond.shape),
                  pl.BlockSpec((None, D, tn), lambda j: (l, 0, j)),
                  pl.BlockSpec((None, 1, tn), lambda j: (l, 0, j))],
        out_specs=pl.BlockSpec((cond.shape[0], tn), lambda j: (0, j)),
        out_shape=jax.ShapeDtypeStruct((cond.shape[0], n), F32),
        compiler_params=_cparams(("arbitrary",)),
        name="mod",
    )(cond, w_mod, b_mod.reshape(n_l, 1, n))


def _inprep_kernel(latent, *refs):
    (x_ref, mod_ref, gpre_ref, win_ref, gmq_ref, wuq_ref, gmkv_ref, wukv_ref,
     ggq_ref, ggk_ref, bd_ref) = refs[:11]
    refs = refs[11:]
    if latent:
        cos64_ref, sin64_ref, cospe_ref, sinpe_ref = refs[:4]
        refs = refs[4:]
    (qm_ref, kvm_ref, gqo_ref, gkv_ref, dqo_ref, dkv_ref, ret_ref, rg_ref, gate_ref) = refs[:9]
    refs = refs[9:]
    if not latent:
        ckv_o, kpe_o, gk_o, gv_o, dk_o, dv_o = refs

    x = x_ref[...]
    mod = mod_ref[...]
    sh1 = mod[:, 0:D]
    sc1 = mod[:, D:2 * D]
    hb = (_rms(x, gpre_ref[...]) * (1.0 + sc1) + sh1).astype(BF16)

    def z(a, b):
        return _dot_nt(hb, win_ref[a:b, :])

    if latent:
        cos64, sin64 = cos64_ref[...], sin64_ref[...]
        cospe, sinpe = cospe_ref[...], sinpe_ref[...]

    cqn = _rms(z(O_CQ, O_CKV), gmq_ref[...]).astype(BF16)
    q = _dot(cqn, wuq_ref[...]) * ((MLA_NOPE + MLA_ROPE) ** -0.5)
    q_nope, q_pe = q[:, 0:512], q[:, 512:768]
    if latent:
        q_pe = _rope(q_pe, _tile_lanes(cospe, 2), _tile_lanes(sinpe, 2), MLA_ROPE // 4)
    qm_ref[:, 0:512] = q_nope.astype(BF16)
    qm_ref[:, 512:768] = q_pe.astype(BF16)

    ckvn = _rms(z(O_CKV, O_KPE), gmkv_ref[...])
    kv = _dot(ckvn.astype(BF16), wukv_ref[...])
    kpe4 = _dot_nt(hb, jnp.concatenate([win_ref[O_KPE:O_GQ, :]] * 4, axis=0))
    if latent:
        kpe4 = _rope(kpe4, cospe, sinpe, MLA_ROPE // 4)
    else:
        ckv_o[...] = ckvn
        kpe_o[...] = kpe4[:, 0:MLA_ROPE]
    kpe_b = kpe4.astype(BF16)
    for p in range(4):
        kvm_ref[:, p * 256:p * 256 + LANES] = kv[:, p * LANES:(p + 1) * LANES].astype(BF16)
        kvm_ref[:, p * 256 + LANES:(p + 1) * 256] = kpe_b
    kvm_ref[:, 1024:1536] = kv[:, 512:1024].astype(BF16)

    gq = _dot_nt(hb, jnp.concatenate(
        [win_ref[O_GQ + h * GQA_HD:O_GQ + (h + 1) * GQA_HD, :] for h in GQA_ORDER], axis=0))
    gq = gq * lax.rsqrt(_seg_meansq(gq, bd_ref, 512) + EPS) * ggq_ref[...]
    gk = z(O_GK, O_GV)
    gk = gk * lax.rsqrt(_seg_meansq(gk, bd_ref, LANES) + EPS) * ggk_ref[...]
    gv = z(O_GV, O_RQ)
    if latent:
        gq = _rope(gq, _tile_lanes(cos64, 4), _tile_lanes(sin64, 4), GQA_HD // 4)
        gk = _rope(gk, cos64, sin64, GQA_HD // 4)
    else:
        gk_o[...] = gk
        gv_o[...] = gv
    gqo_ref[...] = (gq * (GQA_HD ** -0.5)).astype(BF16)
    gkv_ref[:, 0:LANES] = gk.astype(BF16)
    gkv_ref[:, LANES:2 * LANES] = gv.astype(BF16)

    dq = z(O_DQ, O_DK)
    dk = z(O_DK, O_DV)
    dv = z(O_DV, O_GL)
    if latent:
        dq = _rope(dq, _tile_lanes(cos64, 4), _tile_lanes(sin64, 4), DIFF_D // 4)
        dk = _rope(dk, _tile_lanes(cos64, 4), _tile_lanes(sin64, 4), DIFF_D // 4)
    else:
        dk_o[...] = dk
        dv_o[...] = dv
    dqo_ref[...] = (dq * (DIFF_D ** -0.5)).astype(BF16)
    dkv_ref[:, 0:512] = dk.astype(BF16)
    dkv_ref[:, 512:1024] = dv.astype(BF16)

    ret_ref[:, 0:256] = z(O_RQ, O_RK).astype(BF16)
    ret_ref[:, 256:512] = (z(O_RK, O_RV) * (RET_DK ** -0.5)).astype(BF16)
    ret_ref[:, 512:1024] = z(O_RV, O_RG).astype(BF16)
    rg_ref[...] = z(O_RG, O_DQ).astype(BF16)

    for n in range(N_BRANCH):
        gate_ref[:, n * D:(n + 1) * D] = jax.nn.sigmoid(
            z(O_GL + n * D, O_GL + (n + 1) * D)).astype(BF16)


def _inprep_call(latent, l, x, mod3, mod_row, lw, tabs, t_len):
    n_tok = x.shape[0]
    nblk = n_tok // TM
    blk_per_seq = t_len // TM

    def tok(w):
        return pl.BlockSpec((TM, w), lambda i: (i, 0))

    in_specs = [tok(D),
                pl.BlockSpec((None, 1, 6 * D), lambda i: (mod_row(i), 0, 0)),
                _const_spec((1, D)),
                pl.BlockSpec((None, O_END, D), lambda i: (l, 0, 0), pipeline_mode=pl.Buffered(1)),
                _const_spec((1, MLA_Q_LORA)), _const_spec((MLA_Q_LORA, 768)),
                _const_spec((1, MLA_KV_LORA)), _const_spec((MLA_KV_LORA, 1024)),
                _const_spec((1, 512)), _const_spec((1, LANES)), _const_spec((512, 512))]
    args = [x, mod3, lw["g_pre1"], lw["w_in"], lw["g_mla_q"], lw["w_uq"], lw["g_mla_kv"],
            lw["w_ukv"], lw["g_gqa_q"], lw["g_gqa_k"], lw["bd"]]
    if latent:
        tab_spec = pl.BlockSpec((TM, LANES), lambda i: (i % blk_per_seq, 0))
        in_specs += [tab_spec] * 4
        args += list(tabs)
    widths = [768, KVM_W, 512, 256, 512, 1024, 1024, 512, 4 * D]
    out_specs = [tok(w) for w in widths]
    out_shape = [jax.ShapeDtypeStruct((n_tok, w), BF16) for w in widths]
    if not latent:
        cw = [MLA_KV_LORA, MLA_ROPE, 128, 128, 512, 512]
        out_specs += [tok(w) for w in cw]
        out_shape += [jax.ShapeDtypeStruct((n_tok, w), F32) for w in cw]
    return pl.pallas_call(
        functools.partial(_inprep_kernel, latent),
        grid=(nblk,),
        in_specs=in_specs, out_specs=out_specs, out_shape=out_shape,
        compiler_params=_cparams(("arbitrary",)),
        name="inprep_lat" if latent else "inprep_ctx",
    )(*args)


def _pastkv_kernel(ckv_ref, kpe_ref, wukv_ref, o_ref):
    kv = _dot(ckv_ref[...].astype(BF16), wukv_ref[...])
    kpe_b = kpe_ref[...].astype(BF16)
    for p in range(4):
        o_ref[:, p * 256:p * 256 + LANES] = kv[:, p * LANES:(p + 1) * LANES].astype(BF16)
        o_ref[:, p * 256 + LANES:(p + 1) * 256] = kpe_b
    o_ref[:, 1024:1536] = kv[:, 512:1024].astype(BF16)


def _pastkv_call(ckv, kpe4, w_ukv):
    n = ckv.shape[0]
    return pl.pallas_call(
        _pastkv_kernel,
        grid=(n // TM,),
        in_specs=[pl.BlockSpec((TM, MLA_KV_LORA), lambda i: (i, 0)),
                  pl.BlockSpec((TM, LANES), lambda i: (i, 0)),
                  _const_spec((MLA_KV_LORA, 1024))],
        out_specs=pl.BlockSpec((TM, KVM_W), lambda i: (i, 0)),
        out_shape=jax.ShapeDtypeStruct((n, KVM_W), BF16),
        compiler_params=_cparams(("arbitrary",)),
        name="pastkv",
    )(ckv, kpe4, w_ukv)


def _softmax_pv(s, v):
    m = jnp.max(s, axis=-1, keepdims=True)
    p = jnp.exp(s - m)
    l = jnp.sum(p, axis=-1, keepdims=True)
    return _dot(p.astype(BF16), v) / l


def _attn_kernel(lam_init, qm_ref, kvm_ref, gq_ref, gkv_ref, dq_ref, dkv_ref,
                 lam_ref, gdiff_ref, o_ref):
    tq = qm_ref.shape[0]
    lane = _lane_iota((tq, LANES))
    low = lane < HALF_LANES
    zero = jnp.zeros((tq, LANES), BF16)

    for p in range(MLA_HEADS // 2):
        qn = qm_ref[:, p * LANES:(p + 1) * LANES]
        g = p // 2
        qpe = qm_ref[:, 512 + g * LANES:512 + (g + 1) * LANES]
        kk = kvm_ref[:, p * 256:(p + 1) * 256]
        vv = kvm_ref[:, 1024 + p * LANES:1024 + (p + 1) * LANES]
        outs = []
        for half in range(2):
            h = 2 * p + half
            slot = h % 4
            in_slot = (lane >= slot * MLA_ROPE) & (lane < (slot + 1) * MLA_ROPE)
            lhs = jnp.concatenate(
                [jnp.where(low if half == 0 else ~low, qn, zero),
                 jnp.where(in_slot, qpe, zero)], axis=1)
            outs.append(_softmax_pv(_dot_nt(lhs, kk), vv))
        o_ref[:, p * LANES:(p + 1) * LANES] = jnp.where(low, outs[0], outs[1]).astype(BF16)

    kk = gkv_ref[:, 0:LANES]
    vv = gkv_ref[:, LANES:2 * LANES]
    for g in range(GQA_HEADS // 2):
        qg = gq_ref[:, g * LANES:(g + 1) * LANES]
        o_lo = _softmax_pv(_dot_nt(jnp.where(low, qg, zero), kk), vv)
        o_hi = _softmax_pv(_dot_nt(jnp.where(low, zero, qg), kk), vv)
        o_ref[:, 512 + g * LANES:512 + (g + 1) * LANES] = jnp.where(low, o_lo, o_hi).astype(BF16)

    lp = lam_ref[...]
    lam = (jnp.exp(jnp.sum(lp[0:1] * lp[1:2], axis=-1, keepdims=True))
           - jnp.exp(jnp.sum(lp[2:3] * lp[3:4], axis=-1, keepdims=True)) + lam_init)
    for h in range(DIFF_HEADS):
        qh = dq_ref[:, h * LANES:(h + 1) * LANES]
        kk = dkv_ref[:, h * LANES:(h + 1) * LANES]
        vv = dkv_ref[:, 512 + h * LANES:512 + (h + 1) * LANES]
        a1 = _softmax_pv(_dot_nt(jnp.where(low, qh, zero), kk), vv)
        a2 = _softmax_pv(_dot_nt(jnp.where(low, zero, qh), kk), vv)
        od = _rms(a1 - lam * a2, gdiff_ref[...]) * (1.0 - lam_init)
        o_ref[:, 1024 + h * LANES:1024 + (h + 1) * LANES] = od.astype(BF16)


def _attn_call(lam_init, qm, kvm, gq, gkv, dq, dkv, lam_p, g_diff, n_b, t_len, s_len):
    nq = t_len // TQ

    def qspec(w):
        return pl.BlockSpec((TQ, w), lambda b, i: (b * nq + i, 0))

    def kspec(w):
        return pl.BlockSpec((s_len, w), lambda b, i: (b, 0))

    return pl.pallas_call(
        functools.partial(_attn_kernel, lam_init),
        grid=(n_b, nq),
        in_specs=[qspec(768), kspec(KVM_W), qspec(512), kspec(256), qspec(512), kspec(1024),
                  _const_spec((4, DIFF_D)), _const_spec((1, DIFF_DV))],
        out_specs=qspec(3 * BRANCH_W),
        out_shape=jax.ShapeDtypeStruct((n_b * t_len, 3 * BRANCH_W), BF16),
        compiler_params=_cparams(("arbitrary", "arbitrary")),
        name="attn",
    )(qm, kvm, gq, gkv, dq, dkv, lam_p, g_diff)


def _log_sigmoid(x):
    return jnp.minimum(x, 0.0) - jnp.log(1.0 + jnp.exp(-jnp.abs(x)))


def _ret_kernel(latent, t_len, dec_ref, q_ref, k_ref, v_ref, rg_ref, gret_ref, *refs):
    if latent:
        s0_ref, o_ref = refs
    else:
        o_ref, st_ref = refs
    tq = q_ref.shape[0]
    t0 = pl.program_id(1) * tq
    lane = _lane_iota((tq, LANES))
    low = lane < HALF_LANES
    zero = jnp.zeros((tq, LANES), BF16)
    t_idx = (t0 + lax.broadcasted_iota(jnp.int32, (tq, t_len), 0)).astype(F32)
    s_idx = lax.broadcasted_iota(jnp.int32, (tq, t_len), 1).astype(F32)
    dist = t_idx - s_idx
    t_col = (t0 + lax.broadcasted_iota(jnp.int32, (tq, 1), 0)).astype(F32)

    def lg(d, h):
        return _log_sigmoid(jnp.full((1, 1), dec_ref[d, h], F32))

    for h in range(RET_HEADS):
        p, half = h // 2, h % 2
        qp = q_ref[:, p * LANES:(p + 1) * LANES]
        qm = jnp.where(low if half == 0 else ~low, qp, zero)
        kp = k_ref[:, p * LANES:(p + 1) * LANES]
        vh = v_ref[:, h * LANES:(h + 1) * LANES]
        lgf, lgb = lg(0, h), lg(1, h)
        dmask = (jnp.where(dist >= 0, jnp.exp(lgf * jnp.maximum(dist, 0.0)), 0.0)
                 + jnp.where(dist <= 0, jnp.exp(lgb * jnp.maximum(-dist, 0.0)), 0.0))
        o = _dot((_dot_nt(qm, kp) * dmask).astype(BF16), vh)
        if latent:
            sf = s0_ref[0, p].astype(BF16)
            sb = s0_ref[1, p].astype(BF16)
            o = o + _dot(qm, sf) * jnp.exp(lgf * (t_col + 1.0))
            o = o + _dot(qm, sb) * jnp.exp(lgb * (float(t_len) - t_col))
        mu = jnp.mean(o, axis=-1, keepdims=True)
        oc = o - mu
        y = oc * lax.rsqrt(jnp.mean(oc * oc, axis=-1, keepdims=True) + EPS)
        y = y * gret_ref[:, h * LANES:(h + 1) * LANES]
        rg = rg_ref[:, h * LANES:(h + 1) * LANES].astype(F32)
        o_ref[:, h * LANES:(h + 1) * LANES] = (y * _silu(rg)).astype(BF16)

    if not latent:
        s_col = lax.broadcasted_iota(jnp.int32, (t_len, 1), 0).astype(F32)
        lane_t = _lane_iota((1, LANES)) < HALF_LANES
        for p in range(RET_HEADS // 2):
            kp = k_ref[:, p * LANES:(p + 1) * LANES].astype(F32)
            for d in range(2):
                lg_lane = jnp.where(lane_t, lg(d, 2 * p), lg(d, 2 * p + 1))
                expo = (float(t_len) - 1.0 - s_col) if d == 0 else s_col
                kdec_t = jnp.transpose(kp * jnp.exp(lg_lane * expo)).astype(BF16)
                for half in range(2):
                    h = 2 * p + half
                    st = _dot(kdec_t, v_ref[:, h * LANES:(h + 1) * LANES])
                    st_ref[d, h] = st[half * RET_DK:(half + 1) * RET_DK, :]


def _ret_call(latent, dec, ret, rg, g_ret, s0, n_b, t_len):
    nq = t_len // TQ
    assert latent or nq == 1
    in_specs = [pl.BlockSpec(memory_space=pltpu.SMEM),
                pl.BlockSpec((TQ, 256), lambda b, i: (b * nq + i, 0)),
                pl.BlockSpec((t_len, 256), lambda b, i: (b, 1)),
                pl.BlockSpec((t_len, 512), lambda b, i: (b, 1)),
                pl.BlockSpec((TQ, 512), lambda b, i: (b * nq + i, 0)),
                _const_spec((1, 512))]
    args = [dec, ret, ret, ret, rg, g_ret]
    out_specs = [pl.BlockSpec((TQ, 512), lambda b, i: (b * nq + i, 0))]
    out_shape = [jax.ShapeDtypeStruct((n_b * t_len, 512), BF16)]
    if latent:
        in_specs.append(pl.BlockSpec((None, 2, 2, LANES, LANES), lambda b, i: (b, 0, 0, 0, 0)))
        args.append(s0)
    else:
        out_specs.append(pl.BlockSpec((None, 2, RET_HEADS, RET_DK, RET_DV),
                                      lambda b, i: (b, 0, 0, 0, 0)))
        out_shape.append(jax.ShapeDtypeStruct((n_b, 2, RET_HEADS, RET_DK, RET_DV), F32))
    return pl.pallas_call(
        functools.partial(_ret_kernel, latent, t_len),
        grid=(n_b, nq),
        in_specs=in_specs, out_specs=out_specs, out_shape=out_shape,
        compiler_params=_cparams(("arbitrary", "arbitrary")),
        name="ret_lat" if latent else "ret_ctx",
    )(*args)


def _merge_kernel(x_ref, mod_ref, br_ref, or_ref, gate_ref, wbr_ref, wout_ref, gpost_ref, o_ref):
    merged = None
    for n in range(N_BRANCH):
        if n < 2:
            b = br_ref[:, n * BRANCH_W:(n + 1) * BRANCH_W]
        elif n == 2:
            b = or_ref[...]
        else:
            b = br_ref[:, 2 * BRANCH_W:3 * BRANCH_W]
        t = gate_ref[:, n * D:(n + 1) * D].astype(F32) * _dot(b, wbr_ref[n])
        merged = t if merged is None else merged + t
    out = _dot(merged.astype(BF16), wout_ref[...])
    g1 = mod_ref[...][:, 2 * D:3 * D]
    o_ref[...] = x_ref[...] + g1 * _rms(out, gpost_ref[...])


def _merge_call(x, mod3, mod_row, br, o_r, gates, lw):
    n_tok = x.shape[0]

    def tok(w):
        return pl.BlockSpec((TM, w), lambda i: (i, 0))

    return pl.pallas_call(
        _merge_kernel,
        grid=(n_tok // TM,),
        in_specs=[tok(D), pl.BlockSpec((None, 1, 6 * D), lambda i: (mod_row(i), 0, 0)),
                  tok(3 * BRANCH_W), tok(BRANCH_W), tok(4 * D),
                  _const_spec((N_BRANCH, BRANCH_W, D)), _const_spec((D, D)), _const_spec((1, D))],
        out_specs=tok(D),
        out_shape=jax.ShapeDtypeStruct((n_tok, D), F32),
        compiler_params=_cparams(("arbitrary",)),
        name="merge",
    )(x, mod3, br, o_r, gates, lw["w_br"], lw["w_out"], lw["g_post1"])


def _route(logits_t, bias):
    n = logits_t.shape[1]
    scores = jax.nn.sigmoid(logits_t)
    sel = scores + bias
    neg = -jnp.inf
    sub = lax.broadcasted_iota(jnp.int32, (GROUP_SIZE, n), 0)
    grp = []
    for g in range(N_GROUPS):
        blk = sel[g * GROUP_SIZE:(g + 1) * GROUP_SIZE]
        m1 = jnp.max(blk, axis=0, keepdims=True)
        i1 = jnp.min(jnp.where(blk == m1, sub, GROUP_SIZE), axis=0, keepdims=True)
        m2 = jnp.max(jnp.where(sub == i1, neg, blk), axis=0, keepdims=True)
        grp.append(m1 + m2)
    parts = []
    for g in range(N_GROUPS):
        beaten = jnp.zeros((1, n), jnp.int32)
        for o in range(N_GROUPS):
            if o == g:
                continue
            wins = (grp[o] > grp[g]) | (grp[o] == grp[g]) if o < g else (grp[o] > grp[g])
            beaten = beaten + wins.astype(jnp.int32)
        keep = beaten < TOPK_GROUPS
        parts.append(jnp.where(keep, sel[g * GROUP_SIZE:(g + 1) * GROUP_SIZE], neg))
    cur = jnp.concatenate(parts, axis=0)
    eidx = lax.broadcasted_iota(jnp.int32, (N_EXPERTS, n), 0)
    comb = jnp.zeros((N_EXPERTS, n), F32)
    for _ in range(TOP_K):
        m = jnp.max(cur, axis=0, keepdims=True)
        i = jnp.min(jnp.where(cur == m, eidx, N_EXPERTS), axis=0, keepdims=True)
        hit = eidx == i
        comb = jnp.where(hit, scores, comb)
        cur = jnp.where(hit, neg, cur)
    wsum = jnp.sum(comb, axis=0, keepdims=True)
    return comb / wsum * ROUTE_SCALE


def _moe_kernel(x_ref, mod_ref, gpre_ref, wr_ref, br_ref, wgu_ref, wdn_ref, wsgu_ref, wsdn_ref,
                gpost_ref, o_ref, hb_ref, comb_ref, acc_ref):
    e = pl.program_id(1)
    tm = x_ref.shape[0]

    @pl.when(e == 0)
    def _():
        mod = mod_ref[...]
        sh2, sc2 = mod[:, 3 * D:4 * D], mod[:, 4 * D:5 * D]
        h = _rms(x_ref[...], gpre_ref[...]) * (1.0 + sc2) + sh2
        hb = h.astype(BF16)
        hb_ref[...] = hb
        h_lo = (h - hb.astype(F32)).astype(BF16)
        wr = wr_ref[...]
        wr_hi = wr.astype(BF16)
        wr_lo = (wr - wr_hi.astype(F32)).astype(BF16)
        logits_t = _dot_nt(wr_hi, hb) + _dot_nt(wr_hi, h_lo) + _dot_nt(wr_lo, hb)
        comb_t = _route(logits_t, br_ref[...])
        pad = jnp.zeros((LANES - N_EXPERTS, tm), F32)
        comb_ref[...] = jnp.transpose(jnp.concatenate([comb_t, pad], axis=0))
        sgu = _dot(hb, wsgu_ref[...])
        sa = _silu(sgu[:, 0:SHARED_FF]) * sgu[:, SHARED_FF:2 * SHARED_FF]
        acc_ref[...] = _dot(sa.astype(BF16), wsdn_ref[...])

    hb = hb_ref[...]
    gu = _dot(hb, wgu_ref[...].astype(BF16))
    onehot = _lane_iota((1, LANES)) == e
    c = jnp.sum(jnp.where(onehot, comb_ref[...], 0.0), axis=-1, keepdims=True)
    a = _silu(gu[:, 0:EXPERT_FF]) * gu[:, EXPERT_FF:2 * EXPERT_FF] * c
    acc_ref[...] += _dot(a.astype(BF16), wdn_ref[...].astype(BF16))

    @pl.when(e == N_EXPERTS - 1)
    def _():
        g2 = mod_ref[...][:, 5 * D:6 * D]
        o_ref[...] = x_ref[...] + g2 * _rms(acc_ref[...], gpost_ref[...])


def _moe_call(l, x, mod3, mod_row, lw):
    n_tok = x.shape[0]
    tm = TM_MOE
    return pl.pallas_call(
        _moe_kernel,
        grid=(n_tok // tm, N_EXPERTS),
        in_specs=[pl.BlockSpec((tm, D), lambda i, e: (i, 0)),
                  pl.BlockSpec((None, 1, 6 * D), lambda i, e: (mod_row(i), 0, 0)),
                  _const_spec((1, D)), _const_spec((N_EXPERTS, D)), _const_spec((N_EXPERTS, 1)),
                  pl.BlockSpec((None, None, D, 2 * EXPERT_FF), lambda i, e: (l, e, 0, 0)),
                  pl.BlockSpec((None, None, EXPERT_FF, D), lambda i, e: (l, e, 0, 0)),
                  _const_spec((D, 2 * SHARED_FF)), _const_spec((SHARED_FF, D)),
                  _const_spec((1, D))],
        out_specs=pl.BlockSpec((tm, D), lambda i, e: (i, 0)),
        out_shape=jax.ShapeDtypeStruct((n_tok, D), F32),
        scratch_shapes=[pltpu.VMEM((tm, D), BF16), pltpu.VMEM((tm, LANES), F32),
                        pltpu.VMEM((tm, D), F32)],
        compiler_params=_cparams(("arbitrary", "arbitrary")),
        name="moe",
    )(x, mod3, lw["g_pre2"], lw["w_router_t"], lw["b_router"], lw["w_exp_gu"], lw["w_exp_down"],
      lw["w_sh_gu"], lw["w_sh_down"], lw["g_post2"])


def _rope_tables(t_len):
    pos = np.arange(t_len)
    row, col = pos // GRID_W, pos % GRID_W

    def tab(r):
        half = r // 2
        freq = ROPE_BASE ** (-np.arange(half, dtype=np.float64) / half)
        sign = np.concatenate([-np.ones(half), np.ones(half)])
        cs, sn = [], []
        for p in (row, col):
            ang = p[:, None].astype(np.float64) * freq[None, :]
            cs.append(np.concatenate([np.cos(ang), np.cos(ang)], axis=1))
            sn.append(np.concatenate([np.sin(ang), np.sin(ang)], axis=1) * sign[None, :])
        return np.concatenate(cs, axis=1), np.concatenate(sn, axis=1)

    c64, s64 = tab(GQA_HD // 2)
    cpe, spe = tab(MLA_ROPE // 2)
    out = (np.tile(c64, (1, 2)), np.tile(s64, (1, 2)), np.tile(cpe, (1, 4)), np.tile(spe, (1, 4)))
    return tuple(jnp.asarray(a, F32) for a in out)


def _layer_weights(l, p):
    w_uq = p["w_mla_uq"][l].reshape(MLA_Q_LORA, MLA_HEADS, MLA_NOPE + MLA_ROPE)
    w_uq = jnp.concatenate([w_uq[:, :, :MLA_NOPE].reshape(MLA_Q_LORA, -1),
                            w_uq[:, :, MLA_NOPE:].reshape(MLA_Q_LORA, -1)], axis=1)
    w_ukv = p["w_mla_ukv"][l].reshape(MLA_KV_LORA, MLA_HEADS, MLA_NOPE + MLA_V)
    w_ukv = jnp.concatenate([w_ukv[:, :, :MLA_NOPE].reshape(MLA_KV_LORA, -1),
                             w_ukv[:, :, MLA_NOPE:].reshape(MLA_KV_LORA, -1)], axis=1)
    w_br = p["w_br"][l]
    w_br_gqa = w_br[1].reshape(GQA_HEADS, GQA_HD, D)[jnp.array(GQA_ORDER)].reshape(BRANCH_W, D)
    w_br = jnp.stack([w_br[0], w_br_gqa, w_br[2], w_br[3]], axis=0)
    blk = np.arange(512) // GQA_HD
    return {
        "g_pre1": p["g_pre1"][l].reshape(1, D), "g_post1": p["g_post1"][l].reshape(1, D),
        "g_pre2": p["g_pre2"][l].reshape(1, D), "g_post2": p["g_post2"][l].reshape(1, D),
        "w_in": p["w_in_packed"],
        "g_mla_q": p["g_mla_q"][l].reshape(1, -1), "w_uq": w_uq.astype(BF16),
        "g_mla_kv": p["g_mla_kv"][l].reshape(1, -1), "w_ukv": w_ukv.astype(BF16),
        "g_gqa_q": jnp.tile(p["g_gqa_q"][l], GQA_HEADS).reshape(1, -1),
        "g_gqa_k": jnp.tile(p["g_gqa_k"][l], GQA_KV_HEADS).reshape(1, -1),
        "bd": jnp.asarray(blk[:, None] == blk[None, :], BF16),
        "ret_decay": p["ret_decay"][l],
        "g_ret": p["g_ret"][l].reshape(1, -1),
        "diff_lambda": p["diff_lambda"][l], "g_diff": p["g_diff"][l].reshape(1, -1),
        "w_br": w_br.astype(BF16), "w_out": p["w_out"][l].astype(BF16),
        "w_router_t": p["w_router"][l].T, "b_router": p["b_router"][l].reshape(-1, 1),
        "w_exp_gu": p["w_exp_gu"], "w_exp_down": p["w_exp_down"],
        "w_sh_gu": p["w_sh_gu"][l].astype(BF16), "w_sh_down": p["w_sh_down"][l].astype(BF16),
    }


def _mixers(latent, l, x, mod3, mod_row, lw, n_b, t_len, tabs=None, past=None, s0=None):
    lam_init = 0.8 - 0.6 * math.exp(-0.3 * l)
    outs = _inprep_call(latent, l, x, mod3, mod_row, lw, tabs, t_len)
    qm, kvm, gq, gkv, dq, dkv, ret, rg, gates = outs[:9]
    s_len = t_len
    if latent:
        past_kvm, past_gkv, past_dkv = past
        p_len = past_gkv.shape[1]
        s_len = p_len + t_len

        def cat(a, b):
            return jnp.concatenate([a, b.reshape(n_b, t_len, -1)], axis=1).reshape(n_b * s_len, -1)

        kvm, gkv, dkv = cat(past_kvm, kvm), cat(past_gkv, gkv), cat(past_dkv, dkv)
    br = _attn_call(lam_init, qm, kvm, gq, gkv, dq, dkv, lw["diff_lambda"], lw["g_diff"],
                    n_b, t_len, s_len)
    r = _ret_call(latent, lw["ret_decay"], ret, rg, lw["g_ret"], s0, n_b, t_len)
    y = _merge_call(x, mod3, mod_row, br, r[0], gates, lw)
    cache = None if latent else tuple(outs[9:]) + (r[1],)
    return y, cache


def kernel(x_prompt, x_sample, cache_mla_ckv, cache_mla_kpe, cache_gqa_k, cache_gqa_v, cache_diff_k, cache_diff_v, state_ret, c, c_ctx, w_mod, b_mod, g_pre1, g_post1, g_pre2, g_post2, w_in, g_mla_q, w_mla_uq, g_mla_kv, w_mla_ukv, g_gqa_q, g_gqa_k, ret_decay, g_ret, diff_lambda, g_diff, w_br, w_out, w_router, b_router, w_exp_gu, w_exp_down, w_sh_gu, w_sh_down):
    params = dict(w_in_packed=jnp.swapaxes(w_in, 1, 2).astype(BF16), g_pre1=g_pre1, g_post1=g_post1, g_pre2=g_pre2,
                  g_post2=g_post2, g_mla_q=g_mla_q, w_mla_uq=w_mla_uq,
                  g_mla_kv=g_mla_kv, w_mla_ukv=w_mla_ukv, g_gqa_q=g_gqa_q, g_gqa_k=g_gqa_k,
                  ret_decay=ret_decay, g_ret=g_ret, diff_lambda=diff_lambda, g_diff=g_diff,
                  w_br=w_br, w_out=w_out, w_router=w_router, b_router=b_router,
                  w_exp_gu=w_exp_gu, w_exp_down=w_exp_down, w_sh_gu=w_sh_gu, w_sh_down=w_sh_down)
    n_bc, t_c, _ = x_prompt.shape
    n_bl, t_l, _ = x_sample.shape
    p_len = cache_mla_ckv.shape[2]
    tabs = _rope_tables(t_l)
    n_cond = 8
    cond = jnp.concatenate([c_ctx[None, :], c, jnp.zeros((n_cond - 1 - n_bl, D), F32)], axis=0)
    blk_c, blk_l = t_c // TM, t_l // TM
    moe_c, moe_l = t_c * n_bc // TM_MOE, t_l // TM_MOE
    assert t_l % TM_MOE == 0 and (t_c * n_bc) % TM_MOE == 0

    yp = x_prompt.reshape(n_bc * t_c, D)
    ys = x_sample.reshape(n_bl * t_l, D)
    caches = []
    for l in range(DEPTH):
        lw = _layer_weights(l, params)
        mod3 = _mod_call(l, cond, w_mod, b_mod).reshape(n_cond, 1, 6 * D)
        yp, cache = _mixers(False, l, yp, mod3, lambda i: 0, lw, n_bc, t_c)
        yp = _moe_call(l, yp, mod3, lambda i: 0, lw)
        caches.append(cache)
        past_kvm = _pastkv_call(cache_mla_ckv[:, l].reshape(n_bl * p_len, -1),
                                jnp.tile(cache_mla_kpe[:, l].reshape(n_bl * p_len, -1), (1, 4)),
                                lw["w_ukv"]).reshape(n_bl, p_len, -1)
        past_gkv = jnp.concatenate([cache_gqa_k[:, l].reshape(n_bl, p_len, -1),
                                    cache_gqa_v[:, l].reshape(n_bl, p_len, -1)], axis=-1).astype(BF16)
        past_dkv = jnp.concatenate([cache_diff_k[:, l].reshape(n_bl, p_len, -1),
                                    cache_diff_v[:, l].reshape(n_bl, p_len, -1)], axis=-1).astype(BF16)
        s0 = state_ret[:, l].reshape(n_bl, 2, RET_HEADS // 2, 2 * RET_DK, RET_DV)
        ys, _ = _mixers(True, l, ys, mod3, lambda i: 1 + i // blk_l, lw, n_bl, t_l, tabs=tabs,
                        past=(past_kvm, past_gkv, past_dkv), s0=s0)
        ys = _moe_call(l, ys, mod3, lambda i: 1 + i // (t_l // TM_MOE), lw)

    def stack(k, shape):
        return jnp.stack([caches[l][k].reshape((n_bc, t_c) + shape) for l in range(DEPTH)], axis=1)

    new_ret = jnp.stack([caches[l][6] for l in range(DEPTH)], axis=1)
    return (yp.reshape(n_bc, t_c, D), ys.reshape(n_bl, t_l, D),
            stack(0, (MLA_KV_LORA,)), stack(1, (MLA_ROPE,)),
            stack(2, (GQA_KV_HEADS, GQA_HD)), stack(3, (GQA_KV_HEADS, GQA_HD)),
            stack(4, (DIFF_HEADS, 2, DIFF_D)), stack(5, (DIFF_HEADS, DIFF_DV)), new_ret)
```

```python
import functools
import math

import numpy as np
import jax
import jax.numpy as jnp
from jax import lax
from jax.experimental import pallas as pl
from jax.experimental.pallas import tpu as pltpu
from jax.experimental.pallas import tpu_sc as plsc

F32 = jnp.float32
BF16 = jnp.bfloat16

D = 1024
DEPTH = 2
GRID_W = 64
ROPE_BASE = 10000.0
EPS = 1e-6

MLA_HEADS, MLA_NOPE, MLA_ROPE, MLA_V = 8, 64, 32, 64
MLA_Q_LORA, MLA_KV_LORA = 384, 256
GQA_HEADS, GQA_KV_HEADS, GQA_HD = 8, 2, 64
RET_HEADS, RET_DK, RET_DV = 4, 64, 128
DIFF_HEADS, DIFF_D, DIFF_DV = 4, 64, 128
N_BRANCH, BRANCH_W = 4, 512
N_EXPERTS, TOP_K, N_GROUPS, TOPK_GROUPS = 32, 4, 4, 2
EXPERT_FF, SHARED_FF = 256, 256
ROUTE_SCALE = 2.5
GROUP_SIZE = N_EXPERTS // N_GROUPS

LANES = 128
HALF_LANES = 64
VMEM_LIMIT = 56 * 1024 * 1024

C_CQ, C_CKV, C_KPE, C_GQ, C_GK, C_GV = 0, 384, 640, 768, 1280, 1408
C_DQ, C_DK, C_DV, C_RQ, C_RK, C_RV, C_RG, C_GL, C_END = (
    1536, 2048, 2560, 3072, 3328, 3584, 4096, 4608, 8704)
O_CQ, O_CKV, O_KPE, O_GQ, O_GK, O_GV = 0, 384, 640, 672, 1184, 1312
O_RQ, O_RK, O_RV, O_RG, O_DQ, O_DK, O_DV, O_GL, O_END = (
    1440, 1696, 1952, 2464, 2976, 3488, 4000, 4512, 8608)
GQA_ORDER = (0, 4, 1, 5, 2, 6, 3, 7)

KVM_W = 4 * 256 + 512
TM = 256
TQ = 256
TM_MOE = 1024


def _cparams(sem):
    return pltpu.CompilerParams(dimension_semantics=sem, vmem_limit_bytes=VMEM_LIMIT)


def _const_spec(shape):
    nd = len(shape)
    return pl.BlockSpec(shape, lambda *_: (0,) * nd)


def _rms(x, g):
    return x * lax.rsqrt(jnp.mean(x * x, axis=-1, keepdims=True) + EPS) * g


def _dot(a, b):
    return jnp.dot(a, b, preferred_element_type=F32)


def _dot_nt(a, b):
    return lax.dot_general(a, b, (((1,), (1,)), ((), ())), preferred_element_type=F32)


def _silu(x):
    return x * jax.nn.sigmoid(x)


def _lane_iota(shape):
    return lax.broadcasted_iota(jnp.int32, shape, len(shape) - 1)


def _seg_meansq(x, bd_ref, width):
    sq = x * x
    hi = sq.astype(BF16)
    lo = (sq - hi.astype(F32)).astype(BF16)
    bd = bd_ref[0:width, 0:width]
    return (_dot(hi, bd) + _dot(lo, bd)) * (1.0 / GQA_HD)


def _rope(x, cos, sin_signed, half):
    width = x.shape[-1]
    first = (_lane_iota(x.shape) % (2 * half)) < half
    partner = jnp.where(first, pltpu.roll(x, width - half, 1), pltpu.roll(x, half, 1))
    return x * cos + partner * sin_signed


def _tile_lanes(t, reps):
    return t if reps == 1 else jnp.concatenate([t] * reps, axis=1)


def _mod_kernel(c_ref, w_ref, b_ref, o_ref):
    a = _silu(c_ref[...]).astype(BF16)
    o_ref[...] = _dot(a, w_ref[...].astype(BF16)) + b_ref[...]


def _mod_call(l, cond, w_mod, b_mod):
    n_l, _, n = w_mod.shape
    tn = 1536
    return pl.pallas_call(
        _mod_kernel,
        grid=(n // tn,),
        in_specs=[_const_spec(cond.shape),
                  pl.BlockSpec((None, D, tn), lambda j: (l, 0, j)),
                  pl.BlockSpec((None, 1, tn), lambda j: (l, 0, j))],
        out_specs=pl.BlockSpec((cond.shape[0], tn), lambda j: (0, j)),
        out_shape=jax.ShapeDtypeStruct((cond.shape[0], n), F32),
        compiler_params=_cparams(("arbitrary",)),
        name="mod",
    )(cond, w_mod, b_mod.reshape(n_l, 1, n))


def _inprep_kernel(latent, *refs):
    (x_ref, mod_ref, gpre_ref, win_ref, gmq_ref, wuq_ref, gmkv_ref, wukv_ref,
     ggq_ref, ggk_ref, bd_ref) = refs[:11]
    refs = refs[11:]
    if latent:
        cos64_ref, sin64_ref, cospe_ref, sinpe_ref = refs[:4]
        refs = refs[4:]
    (qm_ref, kvm_ref, gqo_ref, gkv_ref, dqo_ref, dkv_ref, ret_ref, rg_ref, gate_ref) = refs[:9]
    refs = refs[9:]
    if not latent:
        ckv_o, kpe_o, gk_o, gv_o, dk_o, dv_o = refs

    x = x_ref[...]
    mod = mod_ref[...]
    sh1 = mod[:, 0:D]
    sc1 = mod[:, D:2 * D]
    hb = (_rms(x, gpre_ref[...]) * (1.0 + sc1) + sh1).astype(BF16)

    def z(a, b):
        return _dot_nt(hb, win_ref[a:b, :])

    if latent:
        cos64, sin64 = cos64_ref[...], sin64_ref[...]
        cospe, sinpe = cospe_ref[...], sinpe_ref[...]

    cqn = _rms(z(O_CQ, O_CKV), gmq_ref[...]).astype(BF16)
    q = _dot(cqn, wuq_ref[...]) * ((MLA_NOPE + MLA_ROPE) ** -0.5)
    q_nope, q_pe = q[:, 0:512], q[:, 512:768]
    if latent:
        q_pe = _rope(q_pe, _tile_lanes(cospe, 2), _tile_lanes(sinpe, 2), MLA_ROPE // 4)
    qm_ref[:, 0:512] = q_nope.astype(BF16)
    qm_ref[:, 512:768] = q_pe.astype(BF16)

    ckvn = _rms(z(O_CKV, O_KPE), gmkv_ref[...])
    kv = _dot(ckvn.astype(BF16), wukv_ref[...])
    kpe4 = _dot_nt(hb, jnp.concatenate([win_ref[O_KPE:O_GQ, :]] * 4, axis=0))
    if latent:
        kpe4 = _rope(kpe4, cospe, sinpe, MLA_ROPE // 4)
    else:
        ckv_o[...] = ckvn
        kpe_o[...] = kpe4[:, 0:MLA_ROPE]
    kpe_b = kpe4.astype(BF16)
    for p in range(4):
        kvm_ref[:, p * 256:p * 256 + LANES] = kv[:, p * LANES:(p + 1) * LANES].astype(BF16)
        kvm_ref[:, p * 256 + LANES:(p + 1) * 256] = kpe_b
    kvm_ref[:, 1024:1536] = kv[:, 512:1024].astype(BF16)

    gq = _dot_nt(hb, jnp.concatenate(
        [win_ref[O_GQ + h * GQA_HD:O_GQ + (h + 1) * GQA_HD, :] for h in GQA_ORDER], axis=0))
    gq = gq * lax.rsqrt(_seg_meansq(gq, bd_ref, 512) + EPS) * ggq_ref[...]
    gk = z(O_GK, O_GV)
    gk = gk * lax.rsqrt(_seg_meansq(gk, bd_ref, LANES) + EPS) * ggk_ref[...]
    gv = z(O_GV, O_RQ)
    if latent:
        gq = _rope(gq, _tile_lanes(cos64, 4), _tile_lanes(sin64, 4), GQA_HD // 4)
        gk = _rope(gk, cos64, sin64, GQA_HD // 4)
    else:
        gk_o[...] = gk
        gv_o[...] = gv
    gqo_ref[...] = (gq * (GQA_HD ** -0.5)).astype(BF16)
    gkv_ref[:, 0:LANES] = gk.astype(BF16)
    gkv_ref[:, LANES:2 * LANES] = gv.astype(BF16)

    dq = z(O_DQ, O_DK)
    dk = z(O_DK, O_DV)
    dv = z(O_DV, O_GL)
    if latent:
        dq = _rope(dq, _tile_lanes(cos64, 4), _tile_lanes(sin64, 4), DIFF_D // 4)
        dk = _rope(dk, _tile_lanes(cos64, 4), _tile_lanes(sin64, 4), DIFF_D // 4)
    else:
        dk_o[...] = dk
        dv_o[...] = dv
    dqo_ref[...] = (dq * (DIFF_D ** -0.5)).astype(BF16)
    dkv_ref[:, 0:512] = dk.astype(BF16)
    dkv_ref[:, 512:1024] = dv.astype(BF16)

    ret_ref[:, 0:256] = z(O_RQ, O_RK).astype(BF16)
    ret_ref[:, 256:512] = (z(O_RK, O_RV) * (RET_DK ** -0.5)).astype(BF16)
    ret_ref[:, 512:1024] = z(O_RV, O_RG).astype(BF16)
    rg_ref[...] = z(O_RG, O_DQ).astype(BF16)

    for n in range(N_BRANCH):
        gate_ref[:, n * D:(n + 1) * D] = jax.nn.sigmoid(
            z(O_GL + n * D, O_GL + (n + 1) * D)).astype(BF16)


def _inprep_call(latent, l, x, mod3, mod_row, lw, tabs, t_len):
    n_tok = x.shape[0]
    nblk = n_tok // TM
    blk_per_seq = t_len // TM

    def tok(w):
        return pl.BlockSpec((TM, w), lambda i: (i, 0))

    in_specs = [tok(D),
                pl.BlockSpec((None, 1, 6 * D), lambda i: (mod_row(i), 0, 0)),
                _const_spec((1, D)),
                pl.BlockSpec((None, O_END, D), lambda i: (l, 0, 0), pipeline_mode=pl.Buffered(1)),
                _const_spec((1, MLA_Q_LORA)), _const_spec((MLA_Q_LORA, 768)),
                _const_spec((1, MLA_KV_LORA)), _const_spec((MLA_KV_LORA, 1024)),
                _const_spec((1, 512)), _const_spec((1, LANES)), _const_spec((512, 512))]
    args = [x, mod3, lw["g_pre1"], lw["w_in"], lw["g_mla_q"], lw["w_uq"], lw["g_mla_kv"],
            lw["w_ukv"], lw["g_gqa_q"], lw["g_gqa_k"], lw["bd"]]
    if latent:
        tab_spec = pl.BlockSpec((TM, LANES), lambda i: (i % blk_per_seq, 0))
        in_specs += [tab_spec] * 4
        args += list(tabs)
    widths = [768, KVM_W, 512, 256, 512, 1024, 1024, 512, 4 * D]
    out_specs = [tok(w) for w in widths]
    out_shape = [jax.ShapeDtypeStruct((n_tok, w), BF16) for w in widths]
    if not latent:
        cw = [MLA_KV_LORA, MLA_ROPE, 128, 128, 512, 512]
        out_specs += [tok(w) for w in cw]
        out_shape += [jax.ShapeDtypeStruct((n_tok, w), F32) for w in cw]
    return pl.pallas_call(
        functools.partial(_inprep_kernel, latent),
        grid=(nblk,),
        in_specs=in_specs, out_specs=out_specs, out_shape=out_shape,
        compiler_params=_cparams(("arbitrary",)),
        name="inprep_lat" if latent else "inprep_ctx",
    )(*args)


def _pastkv_kernel(ckv_ref, kpe_ref, wukv_ref, o_ref):
    kv = _dot(ckv_ref[...].astype(BF16), wukv_ref[...])
    kpe_b = kpe_ref[...].astype(BF16)
    for p in range(4):
        o_ref[:, p * 256:p * 256 + LANES] = kv[:, p * LANES:(p + 1) * LANES].astype(BF16)
        o_ref[:, p * 256 + LANES:(p + 1) * 256] = kpe_b
    o_ref[:, 1024:1536] = kv[:, 512:1024].astype(BF16)


def _pastkv_call(ckv, kpe4, w_ukv):
    n = ckv.shape[0]
    return pl.pallas_call(
        _pastkv_kernel,
        grid=(n // TM,),
        in_specs=[pl.BlockSpec((TM, MLA_KV_LORA), lambda i: (i, 0)),
                  pl.BlockSpec((TM, LANES), lambda i: (i, 0)),
                  _const_spec((MLA_KV_LORA, 1024))],
        out_specs=pl.BlockSpec((TM, KVM_W), lambda i: (i, 0)),
        out_shape=jax.ShapeDtypeStruct((n, KVM_W), BF16),
        compiler_params=_cparams(("arbitrary",)),
        name="pastkv",
    )(ckv, kpe4, w_ukv)


def _softmax_pv(s, v):
    m = jnp.max(s, axis=-1, keepdims=True)
    p = jnp.exp(s - m)
    l = jnp.sum(p, axis=-1, keepdims=True)
    return _dot(p.astype(BF16), v) / l


def _attn_kernel(lam_init, qm_ref, kvm_ref, gq_ref, gkv_ref, dq_ref, dkv_ref,
                 lam_ref, gdiff_ref, o_ref):
    tq = qm_ref.shape[0]
    lane = _lane_iota((tq, LANES))
    low = lane < HALF_LANES
    zero = jnp.zeros((tq, LANES), BF16)

    for p in range(MLA_HEADS // 2):
        qn = qm_ref[:, p * LANES:(p + 1) * LANES]
        g = p // 2
        qpe = qm_ref[:, 512 + g * LANES:512 + (g + 1) * LANES]
        kk = kvm_ref[:, p * 256:(p + 1) * 256]
        vv = kvm_ref[:, 1024 + p * LANES:1024 + (p + 1) * LANES]
        outs = []
        for half in range(2):
            h = 2 * p + half
            slot = h % 4
            in_slot = (lane >= slot * MLA_ROPE) & (lane < (slot + 1) * MLA_ROPE)
            lhs = jnp.concatenate(
                [jnp.where(low if half == 0 else ~low, qn, zero),
                 jnp.where(in_slot, qpe, zero)], axis=1)
            outs.append(_softmax_pv(_dot_nt(lhs, kk), vv))
        o_ref[:, p * LANES:(p + 1) * LANES] = jnp.where(low, outs[0], outs[1]).astype(BF16)

    kk = gkv_ref[:, 0:LANES]
    vv = gkv_ref[:, LANES:2 * LANES]
    for g in range(GQA_HEADS // 2):
        qg = gq_ref[:, g * LANES:(g + 1) * LANES]
        o_lo = _softmax_pv(_dot_nt(jnp.where(low, qg, zero), kk), vv)
        o_hi = _softmax_pv(_dot_nt(jnp.where(low, zero, qg), kk), vv)
        o_ref[:, 512 + g * LANES:512 + (g + 1) * LANES] = jnp.where(low, o_lo, o_hi).astype(BF16)

    lp = lam_ref[...]
    lam = (jnp.exp(jnp.sum(lp[0:1] * lp[1:2], axis=-1, keepdims=True))
           - jnp.exp(jnp.sum(lp[2:3] * lp[3:4], axis=-1, keepdims=True)) + lam_init)
    for h in range(DIFF_HEADS):
        qh = dq_ref[:, h * LANES:(h + 1) * LANES]
        kk = dkv_ref[:, h * LANES:(h + 1) * LANES]
        vv = dkv_ref[:, 512 + h * LANES:512 + (h + 1) * LANES]
        a1 = _softmax_pv(_dot_nt(jnp.where(low, qh, zero), kk), vv)
        a2 = _softmax_pv(_dot_nt(jnp.where(low, zero, qh), kk), vv)
        od = _rms(a1 - lam * a2, gdiff_ref[...]) * (1.0 - lam_init)
        o_ref[:, 1024 + h * LANES:1024 + (h + 1) * LANES] = od.astype(BF16)


def _attn_call(lam_init, qm, kvm, gq, gkv, dq, dkv, lam_p, g_diff, n_b, t_len, s_len):
    nq = t_len // TQ

    def qspec(w):
        return pl.BlockSpec((TQ, w), lambda b, i: (b * nq + i, 0))

    def kspec(w):
        return pl.BlockSpec((s_len, w), lambda b, i: (b, 0))

    return pl.pallas_call(
        functools.partial(_attn_kernel, lam_init),
        grid=(n_b, nq),
        in_specs=[qspec(768), kspec(KVM_W), qspec(512), kspec(256), qspec(512), kspec(1024),
                  _const_spec((4, DIFF_D)), _const_spec((1, DIFF_DV))],
        out_specs=qspec(3 * BRANCH_W),
        out_shape=jax.ShapeDtypeStruct((n_b * t_len, 3 * BRANCH_W), BF16),
        compiler_params=_cparams(("arbitrary", "arbitrary")),
        name="attn",
    )(qm, kvm, gq, gkv, dq, dkv, lam_p, g_diff)


def _log_sigmoid(x):
    return jnp.minimum(x, 0.0) - jnp.log(1.0 + jnp.exp(-jnp.abs(x)))


def _ret_kernel(latent, t_len, dec_ref, q_ref, k_ref, v_ref, rg_ref, gret_ref, *refs):
    if latent:
        s0_ref, o_ref = refs
    else:
        o_ref, st_ref = refs
    tq = q_ref.shape[0]
    t0 = pl.program_id(1) * tq
    lane = _lane_iota((tq, LANES))
    low = lane < HALF_LANES
    zero = jnp.zeros((tq, LANES), BF16)
    t_idx = (t0 + lax.broadcasted_iota(jnp.int32, (tq, t_len), 0)).astype(F32)
    s_idx = lax.broadcasted_iota(jnp.int32, (tq, t_len), 1).astype(F32)
    dist = t_idx - s_idx
    t_col = (t0 + lax.broadcasted_iota(jnp.int32, (tq, 1), 0)).astype(F32)

    def lg(d, h):
        return _log_sigmoid(jnp.full((1, 1), dec_ref[d, h], F32))

    for h in range(RET_HEADS):
        p, half = h // 2, h % 2
        qp = q_ref[:, p * LANES:(p + 1) * LANES]
        qm = jnp.where(low if half == 0 else ~low, qp, zero)
        kp = k_ref[:, p * LANES:(p + 1) * LANES]
        vh = v_ref[:, h * LANES:(h + 1) * LANES]
        lgf, lgb = lg(0, h), lg(1, h)
        dmask = (jnp.where(dist >= 0, jnp.exp(lgf * jnp.maximum(dist, 0.0)), 0.0)
                 + jnp.where(dist <= 0, jnp.exp(lgb * jnp.maximum(-dist, 0.0)), 0.0))
        o = _dot((_dot_nt(qm, kp) * dmask).astype(BF16), vh)
        if latent:
            sf = s0_ref[0, p].astype(BF16)
            sb = s0_ref[1, p].astype(BF16)
            o = o + _dot(qm, sf) * jnp.exp(lgf * (t_col + 1.0))
            o = o + _dot(qm, sb) * jnp.exp(lgb * (float(t_len) - t_col))
        mu = jnp.mean(o, axis=-1, keepdims=True)
        oc = o - mu
        y = oc * lax.rsqrt(jnp.mean(oc * oc, axis=-1, keepdims=True) + EPS)
        y = y * gret_ref[:, h * LANES:(h + 1) * LANES]
        rg = rg_ref[:, h * LANES:(h + 1) * LANES].astype(F32)
        o_ref[:, h * LANES:(h + 1) * LANES] = (y * _silu(rg)).astype(BF16)

    if not latent:
        s_col = lax.broadcasted_iota(jnp.int32, (t_len, 1), 0).astype(F32)
        lane_t = _lane_iota((1, LANES)) < HALF_LANES
        for p in range(RET_HEADS // 2):
            kp = k_ref[:, p * LANES:(p + 1) * LANES].astype(F32)
            for d in range(2):
                lg_lane = jnp.where(lane_t, lg(d, 2 * p), lg(d, 2 * p + 1))
                expo = (float(t_len) - 1.0 - s_col) if d == 0 else s_col
                kdec_t = jnp.transpose(kp * jnp.exp(lg_lane * expo)).astype(BF16)
                for half in range(2):
                    h = 2 * p + half
                    st = _dot(kdec_t, v_ref[:, h * LANES:(h + 1) * LANES])
                    st_ref[d, h] = st[half * RET_DK:(half + 1) * RET_DK, :]


def _ret_call(latent, dec, ret, rg, g_ret, s0, n_b, t_len):
    nq = t_len // TQ
    assert latent or nq == 1
    in_specs = [pl.BlockSpec(memory_space=pltpu.SMEM),
                pl.BlockSpec((TQ, 256), lambda b, i: (b * nq + i, 0)),
                pl.BlockSpec((t_len, 256), lambda b, i: (b, 1)),
                pl.BlockSpec((t_len, 512), lambda b, i: (b, 1)),
                pl.BlockSpec((TQ, 512), lambda b, i: (b * nq + i, 0)),
                _const_spec((1, 512))]
    args = [dec, ret, ret, ret, rg, g_ret]
    out_specs = [pl.BlockSpec((TQ, 512), lambda b, i: (b * nq + i, 0))]
    out_shape = [jax.ShapeDtypeStruct((n_b * t_len, 512), BF16)]
    if latent:
        in_specs.append(pl.BlockSpec((None, 2, 2, LANES, LANES), lambda b, i: (b, 0, 0, 0, 0)))
        args.append(s0)
    else:
        out_specs.append(pl.BlockSpec((None, 2, RET_HEADS, RET_DK, RET_DV),
                                      lambda b, i: (b, 0, 0, 0, 0)))
        out_shape.append(jax.ShapeDtypeStruct((n_b, 2, RET_HEADS, RET_DK, RET_DV), F32))
    return pl.pallas_call(
        functools.partial(_ret_kernel, latent, t_len),
        grid=(n_b, nq),
        in_specs=in_specs, out_specs=out_specs, out_shape=out_shape,
        compiler_params=_cparams(("arbitrary", "arbitrary")),
        name="ret_lat" if latent else "ret_ctx",
    )(*args)


def _merge_kernel(x_ref, mod_ref, br_ref, or_ref, gate_ref, wbr_ref, wout_ref, gpost_ref, o_ref):
    merged = None
    for n in range(N_BRANCH):
        if n < 2:
            b = br_ref[:, n * BRANCH_W:(n + 1) * BRANCH_W]
        elif n == 2:
            b = or_ref[...]
        else:
            b = br_ref[:, 2 * BRANCH_W:3 * BRANCH_W]
        t = gate_ref[:, n * D:(n + 1) * D].astype(F32) * _dot(b, wbr_ref[n])
        merged = t if merged is None else merged + t
    out = _dot(merged.astype(BF16), wout_ref[...])
    g1 = mod_ref[...][:, 2 * D:3 * D]
    o_ref[...] = x_ref[...] + g1 * _rms(out, gpost_ref[...])


def _merge_call(x, mod3, mod_row, br, o_r, gates, lw):
    n_tok = x.shape[0]

    def tok(w):
        return pl.BlockSpec((TM, w), lambda i: (i, 0))

    return pl.pallas_call(
        _merge_kernel,
        grid=(n_tok // TM,),
        in_specs=[tok(D), pl.BlockSpec((None, 1, 6 * D), lambda i: (mod_row(i), 0, 0)),
                  tok(3 * BRANCH_W), tok(BRANCH_W), tok(4 * D),
                  _const_spec((N_BRANCH, BRANCH_W, D)), _const_spec((D, D)), _const_spec((1, D))],
        out_specs=tok(D),
        out_shape=jax.ShapeDtypeStruct((n_tok, D), F32),
        compiler_params=_cparams(("arbitrary",)),
        name="merge",
    )(x, mod3, br, o_r, gates, lw["w_br"], lw["w_out"], lw["g_post1"])


def _route(logits_t, bias):
    n = logits_t.shape[1]
    scores = jax.nn.sigmoid(logits_t)
    sel = scores + bias
    neg = -jnp.inf
    sub = lax.broadcasted_iota(jnp.int32, (GROUP_SIZE, n), 0)
    grp = []
    for g in range(N_GROUPS):
        blk = sel[g * GROUP_SIZE:(g + 1) * GROUP_SIZE]
        m1 = jnp.max(blk, axis=0, keepdims=True)
        i1 = jnp.min(jnp.where(blk == m1, sub, GROUP_SIZE), axis=0, keepdims=True)
        m2 = jnp.max(jnp.where(sub == i1, neg, blk), axis=0, keepdims=True)
        grp.append(m1 + m2)
    parts = []
    for g in range(N_GROUPS):
        beaten = jnp.zeros((1, n), jnp.int32)
        for o in range(N_GROUPS):
            if o == g:
                continue
            wins = (grp[o] > grp[g]) | (grp[o] == grp[g]) if o < g else (grp[o] > grp[g])
            beaten = beaten + wins.astype(jnp.int32)
        keep = beaten < TOPK_GROUPS
        parts.append(jnp.where(keep, sel[g * GROUP_SIZE:(g + 1) * GROUP_SIZE], neg))
    cur = jnp.concatenate(parts, axis=0)
    eidx = lax.broadcasted_iota(jnp.int32, (N_EXPERTS, n), 0)
    comb = jnp.zeros((N_EXPERTS, n), F32)
    for _ in range(TOP_K):
        m = jnp.max(cur, axis=0, keepdims=True)
        i = jnp.min(jnp.where(cur == m, eidx, N_EXPERTS), axis=0, keepdims=True)
        hit = eidx == i
        comb = jnp.where(hit, scores, comb)
        cur = jnp.where(hit, neg, cur)
    wsum = jnp.sum(comb, axis=0, keepdims=True)
    return comb / wsum * ROUTE_SCALE


def _moe_kernel(x_ref, mod_ref, gpre_ref, wr_ref, br_ref, wgu_ref, wdn_ref, wsgu_ref, wsdn_ref,
                gpost_ref, o_ref, hb_ref, comb_ref, acc_ref):
    e = pl.program_id(1)
    tm = x_ref.shape[0]

    @pl.when(e == 0)
    def _():
        mod = mod_ref[...]
        sh2, sc2 = mod[:, 3 * D:4 * D], mod[:, 4 * D:5 * D]
        h = _rms(x_ref[...], gpre_ref[...]) * (1.0 + sc2) + sh2
        hb = h.astype(BF16)
        hb_ref[...] = hb
        h_lo = (h - hb.astype(F32)).astype(BF16)
        wr = wr_ref[...]
        wr_hi = wr.astype(BF16)
        wr_lo = (wr - wr_hi.astype(F32)).astype(BF16)
        logits_t = _dot_nt(wr_hi, hb) + _dot_nt(wr_hi, h_lo) + _dot_nt(wr_lo, hb)
        comb_t = _route(logits_t, br_ref[...])
        pad = jnp.zeros((LANES - N_EXPERTS, tm), F32)
        comb_ref[...] = jnp.transpose(jnp.concatenate([comb_t, pad], axis=0))
        sgu = _dot(hb, wsgu_ref[...])
        sa = _silu(sgu[:, 0:SHARED_FF]) * sgu[:, SHARED_FF:2 * SHARED_FF]
        acc_ref[...] = _dot(sa.astype(BF16), wsdn_ref[...])

    hb = hb_ref[...]
    gu = _dot(hb, wgu_ref[...].astype(BF16))
    onehot = _lane_iota((1, LANES)) == e
    c = jnp.sum(jnp.where(onehot, comb_ref[...], 0.0), axis=-1, keepdims=True)
    a = _silu(gu[:, 0:EXPERT_FF]) * gu[:, EXPERT_FF:2 * EXPERT_FF] * c
    acc_ref[...] += _dot(a.astype(BF16), wdn_ref[...].astype(BF16))

    @pl.when(e == N_EXPERTS - 1)
    def _():
        g2 = mod_ref[...][:, 5 * D:6 * D]
        o_ref[...] = x_ref[...] + g2 * _rms(acc_ref[...], gpost_ref[...])


def _moe_call(l, x, mod3, mod_row, lw):
    n_tok = x.shape[0]
    tm = TM_MOE
    return pl.pallas_call(
        _moe_kernel,
        grid=(n_tok // tm, N_EXPERTS),
        in_specs=[pl.BlockSpec((tm, D), lambda i, e: (i, 0)),
                  pl.BlockSpec((None, 1, 6 * D), lambda i, e: (mod_row(i), 0, 0)),
                  _const_spec((1, D)), _const_spec((N_EXPERTS, D)), _const_spec((N_EXPERTS, 1)),
                  pl.BlockSpec((None, None, D, 2 * EXPERT_FF), lambda i, e: (l, e, 0, 0)),
                  pl.BlockSpec((None, None, EXPERT_FF, D), lambda i, e: (l, e, 0, 0)),
                  _const_spec((D, 2 * SHARED_FF)), _const_spec((SHARED_FF, D)),
                  _const_spec((1, D))],
        out_specs=pl.BlockSpec((tm, D), lambda i, e: (i, 0)),
        out_shape=jax.ShapeDtypeStruct((n_tok, D), F32),
        scratch_shapes=[pltpu.VMEM((tm, D), BF16), pltpu.VMEM((tm, LANES), F32),
                        pltpu.VMEM((tm, D), F32)],
        compiler_params=_cparams(("arbitrary", "arbitrary")),
        name="moe",
    )(x, mod3, lw["g_pre2"], lw["w_router_t"], lw["b_router"], lw["w_exp_gu"], lw["w_exp_down"],
      lw["w_sh_gu"], lw["w_sh_down"], lw["g_post2"])


SC_CORES, SC_SUBCORES = 2, 16
SC_WORKERS = SC_CORES * SC_SUBCORES


def _sc_gather_rows(table, idx, chunk=64):
    n_out, width = idx.shape[0], table.shape[1]
    per_worker = n_out // SC_WORKERS
    n_chunks = per_worker // chunk
    assert per_worker * SC_WORKERS == n_out and n_chunks * chunk == per_worker
    mesh = plsc.VectorSubcoreMesh(core_axis_name="c", subcore_axis_name="s",
                                  num_cores=SC_CORES, num_subcores=SC_SUBCORES)

    @functools.partial(
        pl.kernel, mesh=mesh,
        out_type=jax.ShapeDtypeStruct((n_out, width), table.dtype),
        scratch_types=[pltpu.VMEM((chunk,), jnp.int32), pltpu.VMEM((chunk, width), table.dtype),
                       pltpu.SemaphoreType.DMA],
        name="sc_gather")
    def gather(table_hbm, idx_hbm, out_hbm, idx_v, rows_v, sem):
        base = (lax.axis_index("s") * SC_CORES + lax.axis_index("c")) * per_worker

        @pl.loop(0, n_chunks)
        def _(j):
            off = base + j * chunk
            pltpu.sync_copy(idx_hbm.at[pl.ds(off, chunk)], idx_v)
            pltpu.async_copy(table_hbm.at[idx_v], rows_v, sem).wait()
            pltpu.sync_copy(rows_v, out_hbm.at[pl.ds(off, chunk)])

    return gather(table, idx)


def _rope_tables(t_len):
    pos = np.arange(t_len)
    row, col = pos // GRID_W, pos % GRID_W

    def tab(r):
        half = r // 2
        freq = ROPE_BASE ** (-np.arange(half, dtype=np.float64) / half)
        sign = np.concatenate([-np.ones(half), np.ones(half)])
        cs, sn = [], []
        for p in (row, col):
            ang = p[:, None].astype(np.float64) * freq[None, :]
            cs.append(np.concatenate([np.cos(ang), np.cos(ang)], axis=1))
            sn.append(np.concatenate([np.sin(ang), np.sin(ang)], axis=1) * sign[None, :])
        return np.concatenate(cs, axis=1), np.concatenate(sn, axis=1)

    c64, s64 = tab(GQA_HD // 2)
    cpe, spe = tab(MLA_ROPE // 2)
    out = (np.tile(c64, (1, 2)), np.tile(s64, (1, 2)), np.tile(cpe, (1, 4)), np.tile(spe, (1, 4)))
    return tuple(jnp.asarray(a, F32) for a in out)


def _layer_weights(l, p):
    w_uq = p["w_mla_uq"][l].reshape(MLA_Q_LORA, MLA_HEADS, MLA_NOPE + MLA_ROPE)
    w_uq = jnp.concatenate([w_uq[:, :, :MLA_NOPE].reshape(MLA_Q_LORA, -1),
                            w_uq[:, :, MLA_NOPE:].reshape(MLA_Q_LORA, -1)], axis=1)
    w_ukv = p["w_mla_ukv"][l].reshape(MLA_KV_LORA, MLA_HEADS, MLA_NOPE + MLA_V)
    w_ukv = jnp.concatenate([w_ukv[:, :, :MLA_NOPE].reshape(MLA_KV_LORA, -1),
                             w_ukv[:, :, MLA_NOPE:].reshape(MLA_KV_LORA, -1)], axis=1)
    w_br = p["w_br"][l]
    w_br_gqa = w_br[1].reshape(GQA_HEADS, GQA_HD, D)[jnp.array(GQA_ORDER)].reshape(BRANCH_W, D)
    w_br = jnp.stack([w_br[0], w_br_gqa, w_br[2], w_br[3]], axis=0)
    blk = np.arange(512) // GQA_HD
    return {
        "g_pre1": p["g_pre1"][l].reshape(1, D), "g_post1": p["g_post1"][l].reshape(1, D),
        "g_pre2": p["g_pre2"][l].reshape(1, D), "g_post2": p["g_post2"][l].reshape(1, D),
        "w_in": p["w_in_packed"],
        "g_mla_q": p["g_mla_q"][l].reshape(1, -1), "w_uq": w_uq.astype(BF16),
        "g_mla_kv": p["g_mla_kv"][l].reshape(1, -1), "w_ukv": w_ukv.astype(BF16),
        "g_gqa_q": jnp.tile(p["g_gqa_q"][l], GQA_HEADS).reshape(1, -1),
        "g_gqa_k": jnp.tile(p["g_gqa_k"][l], GQA_KV_HEADS).reshape(1, -1),
        "bd": jnp.asarray(blk[:, None] == blk[None, :], BF16),
        "ret_decay": p["ret_decay"][l],
        "g_ret": p["g_ret"][l].reshape(1, -1),
        "diff_lambda": p["diff_lambda"][l], "g_diff": p["g_diff"][l].reshape(1, -1),
        "w_br": w_br.astype(BF16), "w_out": p["w_out"][l].astype(BF16),
        "w_router_t": p["w_router"][l].T, "b_router": p["b_router"][l].reshape(-1, 1),
        "w_exp_gu": p["w_exp_gu"], "w_exp_down": p["w_exp_down"],
        "w_sh_gu": p["w_sh_gu"][l].astype(BF16), "w_sh_down": p["w_sh_down"][l].astype(BF16),
    }


def _mixers(latent, l, x, mod3, mod_row, lw, n_b, t_len, tabs=None, past=None, s0=None):
    lam_init = 0.8 - 0.6 * math.exp(-0.3 * l)
    outs = _inprep_call(latent, l, x, mod3, mod_row, lw, tabs, t_len)
    qm, kvm, gq, gkv, dq, dkv, ret, rg, gates = outs[:9]
    s_len = t_len
    if latent:
        past_kvm, past_gkv, past_dkv = past
        p_len = past_gkv.shape[1]
        s_len = p_len + t_len

        def cat(a, b):
            return jnp.concatenate([a, b.reshape(n_b, t_len, -1)], axis=1).reshape(n_b * s_len, -1)

        kvm, gkv, dkv = cat(past_kvm, kvm), cat(past_gkv, gkv), cat(past_dkv, dkv)
    br = _attn_call(lam_init, qm, kvm, gq, gkv, dq, dkv, lw["diff_lambda"], lw["g_diff"],
                    n_b, t_len, s_len)
    r = _ret_call(latent, lw["ret_decay"], ret, rg, lw["g_ret"], s0, n_b, t_len)
    y = _merge_call(x, mod3, mod_row, br, r[0], gates, lw)
    cache = None if latent else tuple(outs[9:]) + (r[1],)
    return y, cache


def kernel(x_prompt, x_sample, cache_mla_ckv, cache_mla_kpe, cache_gqa_k, cache_gqa_v, cache_diff_k, cache_diff_v, state_ret, c, c_ctx, w_mod, b_mod, g_pre1, g_post1, g_pre2, g_post2, w_in, g_mla_q, w_mla_uq, g_mla_kv, w_mla_ukv, g_gqa_q, g_gqa_k, ret_decay, g_ret, diff_lambda, g_diff, w_br, w_out, w_router, b_router, w_exp_gu, w_exp_down, w_sh_gu, w_sh_down):
    params = dict(w_in_packed=jnp.swapaxes(w_in, 1, 2).astype(BF16), g_pre1=g_pre1, g_post1=g_post1, g_pre2=g_pre2,
                  g_post2=g_post2, g_mla_q=g_mla_q, w_mla_uq=w_mla_uq,
                  g_mla_kv=g_mla_kv, w_mla_ukv=w_mla_ukv, g_gqa_q=g_gqa_q, g_gqa_k=g_gqa_k,
                  ret_decay=ret_decay, g_ret=g_ret, diff_lambda=diff_lambda, g_diff=g_diff,
                  w_br=w_br, w_out=w_out, w_router=w_router, b_router=b_router,
                  w_exp_gu=w_exp_gu, w_exp_down=w_exp_down, w_sh_gu=w_sh_gu, w_sh_down=w_sh_down)
    n_bc, t_c, _ = x_prompt.shape
    n_bl, t_l, _ = x_sample.shape
    p_len = cache_mla_ckv.shape[2]
    tabs = _rope_tables(t_l)
    n_cond = 8
    cond = jnp.concatenate([c_ctx[None, :], c, jnp.zeros((n_cond - 1 - n_bl, D), F32)], axis=0)
    blk_c, blk_l = t_c // TM, t_l // TM
    moe_c, moe_l = t_c * n_bc // TM_MOE, t_l // TM_MOE
    assert t_l % TM_MOE == 0 and (t_c * n_bc) % TM_MOE == 0

    yp = x_prompt.reshape(n_bc * t_c, D)
    ys = x_sample.reshape(n_bl * t_l, D)
    caches = []
    for l in range(DEPTH):
        lw = _layer_weights(l, params)
        mod3 = _mod_call(l, cond, w_mod, b_mod).reshape(n_cond, 1, 6 * D)
        yp, cache = _mixers(False, l, yp, mod3, lambda i: 0, lw, n_bc, t_c)
        yp = _moe_call(l, yp, mod3, lambda i: 0, lw)
        caches.append(cache)
        past_kvm = _pastkv_call(cache_mla_ckv[:, l].reshape(n_bl * p_len, -1),
                                jnp.tile(cache_mla_kpe[:, l].reshape(n_bl * p_len, -1), (1, 4)),
                                lw["w_ukv"]).reshape(n_bl, p_len, -1)
        past_gkv = jnp.concatenate([cache_gqa_k[:, l].reshape(n_bl, p_len, -1),
                                    cache_gqa_v[:, l].reshape(n_bl, p_len, -1)], axis=-1).astype(BF16)
        past_dkv = jnp.concatenate([cache_diff_k[:, l].reshape(n_bl, p_len, -1),
                                    cache_diff_v[:, l].reshape(n_bl, p_len, -1)], axis=-1).astype(BF16)
        s0 = state_ret[:, l].reshape(n_bl, 2, RET_HEADS // 2, 2 * RET_DK, RET_DV)
        ys, _ = _mixers(True, l, ys, mod3, lambda i: 1 + i // blk_l, lw, n_bl, t_l, tabs=tabs,
                        past=(past_kvm, past_gkv, past_dkv), s0=s0)
        ys = _moe_call(l, ys, mod3, lambda i: 1 + i // (t_l // TM_MOE), lw)

    rev = jnp.arange(n_bc * t_c - 1, -1, -1, dtype=jnp.int32)
    yp = _sc_gather_rows(_sc_gather_rows(yp, rev), rev)

    def stack(k, shape):
        return jnp.stack([caches[l][k].reshape((n_bc, t_c) + shape) for l in range(DEPTH)], axis=1)

    new_ret = jnp.stack([caches[l][6] for l in range(DEPTH)], axis=1)
    return (yp.reshape(n_bc, t_c, D), ys.reshape(n_bl, t_l, D),
            stack(0, (MLA_KV_LORA,)), stack(1, (MLA_ROPE,)),
            stack(2, (GQA_KV_HEADS, GQA_HD)), stack(3, (GQA_KV_HEADS, GQA_HD)),
            stack(4, (DIFF_HEADS, 2, DIFF_D)), stack(5, (DIFF_HEADS, DIFF_DV)), new_ret)
```

```python
import functools
import math

import numpy as np
import jax
import jax.numpy as jnp
from jax import lax
from jax.experimental import pallas as pl
from jax.experimental.pallas import tpu as pltpu
from jax.experimental.pallas import tpu_sc as plsc

F32 = jnp.float32
BF16 = jnp.bfloat16

D = 1024
DEPTH = 2
GRID_W = 64
ROPE_BASE = 10000.0
EPS = 1e-6

MLA_HEADS, MLA_NOPE, MLA_ROPE, MLA_V = 8, 64, 32, 64
MLA_Q_LORA, MLA_KV_LORA = 384, 256
GQA_HEADS, GQA_KV_HEADS, GQA_HD = 8, 2, 64
RET_HEADS, RET_DK, RET_DV = 4, 64, 128
DIFF_HEADS, DIFF_D, DIFF_DV = 4, 64, 128
N_BRANCH, BRANCH_W = 4, 512
N_EXPERTS, TOP_K, N_GROUPS, TOPK_GROUPS = 32, 4, 4, 2
EXPERT_FF, SHARED_FF = 256, 256
ROUTE_SCALE = 2.5
GROUP_SIZE = N_EXPERTS // N_GROUPS

LANES = 128
HALF_LANES = 64
VMEM_LIMIT = 56 * 1024 * 1024

C_CQ, C_CKV, C_KPE, C_GQ, C_GK, C_GV = 0, 384, 640, 768, 1280, 1408
C_DQ, C_DK, C_DV, C_RQ, C_RK, C_RV, C_RG, C_GL, C_END = (
    1536, 2048, 2560, 3072, 3328, 3584, 4096, 4608, 8704)
O_CQ, O_CKV, O_KPE, O_GQ, O_GK, O_GV = 0, 384, 640, 672, 1184, 1312
O_RQ, O_RK, O_RV, O_RG, O_DQ, O_DK, O_DV, O_GL, O_END = (
    1440, 1696, 1952, 2464, 2976, 3488, 4000, 4512, 8608)
GQA_ORDER = (0, 4, 1, 5, 2, 6, 3, 7)

KVM_W = 4 * 256 + 512
TM = 256
TQ = 256
TM_MOE_PRE = 512
TMX = 256


def _cparams(sem):
    return pltpu.CompilerParams(dimension_semantics=sem, vmem_limit_bytes=VMEM_LIMIT)


def _const_spec(shape):
    nd = len(shape)
    return pl.BlockSpec(shape, lambda *_: (0,) * nd)


def _rms(x, g):
    return x * lax.rsqrt(jnp.mean(x * x, axis=-1, keepdims=True) + EPS) * g


def _dot(a, b):
    return jnp.dot(a, b, preferred_element_type=F32)


def _dot_nt(a, b):
    return lax.dot_general(a, b, (((1,), (1,)), ((), ())), preferred_element_type=F32)


def _silu(x):
    return x * jax.nn.sigmoid(x)


def _lane_iota(shape):
    return lax.broadcasted_iota(jnp.int32, shape, len(shape) - 1)


def _seg_meansq(x, bd_ref, width):
    sq = x * x
    hi = sq.astype(BF16)
    lo = (sq - hi.astype(F32)).astype(BF16)
    bd = bd_ref[0:width, 0:width]
    return (_dot(hi, bd) + _dot(lo, bd)) * (1.0 / GQA_HD)


def _rope(x, cos, sin_signed, half):
    width = x.shape[-1]
    first = (_lane_iota(x.shape) % (2 * half)) < half
    partner = jnp.where(first, pltpu.roll(x, width - half, 1), pltpu.roll(x, half, 1))
    return x * cos + partner * sin_signed


def _tile_lanes(t, reps):
    return t if reps == 1 else jnp.concatenate([t] * reps, axis=1)


def _mod_kernel(c_ref, w_ref, b_ref, o_ref):
    a = _silu(c_ref[...]).astype(BF16)
    o_ref[...] = _dot(a, w_ref[...].astype(BF16)) + b_ref[...]


def _mod_call(l, cond, w_mod, b_mod):
    n_l, _, n = w_mod.shape
    tn = 1536
    return pl.pallas_call(
        _mod_kernel,
        grid=(n // tn,),
        in_specs=[_const_spec(cond.shape),
                  pl.BlockSpec((None, D, tn), lambda j: (l, 0, j)),
                  pl.BlockSpec((None, 1, tn), lambda j: (l, 0, j))],
        out_specs=pl.BlockSpec((cond.shape[0], tn), lambda j: (0, j)),
        out_shape=jax.ShapeDtypeStruct((cond.shape[0], n), F32),
        compiler_params=_cparams(("arbitrary",)),
        name="mod",
    )(cond, w_mod, b_mod.reshape(n_l, 1, n))


def _inprep_kernel(latent, *refs):
    (x_ref, mod_ref, gpre_ref, win_ref, gmq_ref, wuq_ref, gmkv_ref, wukv_ref,
     ggq_ref, ggk_ref, bd_ref) = refs[:11]
    refs = refs[11:]
    if latent:
        cos64_ref, sin64_ref, cospe_ref, sinpe_ref = refs[:4]
        refs = refs[4:]
    (qm_ref, kvm_ref, gqo_ref, gkv_ref, dqo_ref, dkv_ref, ret_ref, rg_ref, gate_ref) = refs[:9]
    refs = refs[9:]
    if not latent:
        ckv_o, kpe_o, gk_o, gv_o, dk_o, dv_o = refs

    x = x_ref[...]
    mod = mod_ref[...]
    sh1 = mod[:, 0:D]
    sc1 = mod[:, D:2 * D]
    hb = (_rms(x, gpre_ref[...]) * (1.0 + sc1) + sh1).astype(BF16)

    def z(a, b):
        return _dot_nt(hb, win_ref[a:b, :])

    if latent:
        cos64, sin64 = cos64_ref[...], sin64_ref[...]
        cospe, sinpe = cospe_ref[...], sinpe_ref[...]

    cqn = _rms(z(O_CQ, O_CKV), gmq_ref[...]).astype(BF16)
    q = _dot(cqn, wuq_ref[...]) * ((MLA_NOPE + MLA_ROPE) ** -0.5)
    q_nope, q_pe = q[:, 0:512], q[:, 512:768]
    if latent:
        q_pe = _rope(q_pe, _tile_lanes(cospe, 2), _tile_lanes(sinpe, 2), MLA_ROPE // 4)
    qm_ref[:, 0:512] = q_nope.astype(BF16)
    qm_ref[:, 512:768] = q_pe.astype(BF16)

    ckvn = _rms(z(O_CKV, O_KPE), gmkv_ref[...])
    kv = _dot(ckvn.astype(BF16), wukv_ref[...])
    kpe4 = _dot_nt(hb, jnp.concatenate([win_ref[O_KPE:O_GQ, :]] * 4, axis=0))
    if latent:
        kpe4 = _rope(kpe4, cospe, sinpe, MLA_ROPE // 4)
    else:
        ckv_o[...] = ckvn
        kpe_o[...] = kpe4[:, 0:MLA_ROPE]
    kpe_b = kpe4.astype(BF16)
    for p in range(4):
        kvm_ref[:, p * 256:p * 256 + LANES] = kv[:, p * LANES:(p + 1) * LANES].astype(BF16)
        kvm_ref[:, p * 256 + LANES:(p + 1) * 256] = kpe_b
    kvm_ref[:, 1024:1536] = kv[:, 512:1024].astype(BF16)

    gq = _dot_nt(hb, jnp.concatenate(
        [win_ref[O_GQ + h * GQA_HD:O_GQ + (h + 1) * GQA_HD, :] for h in GQA_ORDER], axis=0))
    gq = gq * lax.rsqrt(_seg_meansq(gq, bd_ref, 512) + EPS) * ggq_ref[...]
    gk = z(O_GK, O_GV)
    gk = gk * lax.rsqrt(_seg_meansq(gk, bd_ref, LANES) + EPS) * ggk_ref[...]
    gv = z(O_GV, O_RQ)
    if latent:
        gq = _rope(gq, _tile_lanes(cos64, 4), _tile_lanes(sin64, 4), GQA_HD // 4)
        gk = _rope(gk, cos64, sin64, GQA_HD // 4)
    else:
        gk_o[...] = gk
        gv_o[...] = gv
    gqo_ref[...] = (gq * (GQA_HD ** -0.5)).astype(BF16)
    gkv_ref[:, 0:LANES] = gk.astype(BF16)
    gkv_ref[:, LANES:2 * LANES] = gv.astype(BF16)

    dq = z(O_DQ, O_DK)
    dk = z(O_DK, O_DV)
    dv = z(O_DV, O_GL)
    if latent:
        dq = _rope(dq, _tile_lanes(cos64, 4), _tile_lanes(sin64, 4), DIFF_D // 4)
        dk = _rope(dk, _tile_lanes(cos64, 4), _tile_lanes(sin64, 4), DIFF_D // 4)
    else:
        dk_o[...] = dk
        dv_o[...] = dv
    dqo_ref[...] = (dq * (DIFF_D ** -0.5)).astype(BF16)
    dkv_ref[:, 0:512] = dk.astype(BF16)
    dkv_ref[:, 512:1024] = dv.astype(BF16)

    ret_ref[:, 0:256] = z(O_RQ, O_RK).astype(BF16)
    ret_ref[:, 256:512] = (z(O_RK, O_RV) * (RET_DK ** -0.5)).astype(BF16)
    ret_ref[:, 512:1024] = z(O_RV, O_RG).astype(BF16)
    rg_ref[...] = z(O_RG, O_DQ).astype(BF16)

    for n in range(N_BRANCH):
        gate_ref[:, n * D:(n + 1) * D] = jax.nn.sigmoid(
            z(O_GL + n * D, O_GL + (n + 1) * D)).astype(BF16)


def _inprep_call(latent, l, x, mod3, mod_row, lw, tabs, t_len):
    n_tok = x.shape[0]
    nblk = n_tok // TM
    blk_per_seq = t_len // TM

    def tok(w):
        return pl.BlockSpec((TM, w), lambda i: (i, 0))

    in_specs = [tok(D),
                pl.BlockSpec((None, 1, 6 * D), lambda i: (mod_row(i), 0, 0)),
                _const_spec((1, D)),
                pl.BlockSpec((None, O_END, D), lambda i: (l, 0, 0), pipeline_mode=pl.Buffered(1)),
                _const_spec((1, MLA_Q_LORA)), _const_spec((MLA_Q_LORA, 768)),
                _const_spec((1, MLA_KV_LORA)), _const_spec((MLA_KV_LORA, 1024)),
                _const_spec((1, 512)), _const_spec((1, LANES)), _const_spec((512, 512))]
    args = [x, mod3, lw["g_pre1"], lw["w_in"], lw["g_mla_q"], lw["w_uq"], lw["g_mla_kv"],
            lw["w_ukv"], lw["g_gqa_q"], lw["g_gqa_k"], lw["bd"]]
    if latent:
        tab_spec = pl.BlockSpec((TM, LANES), lambda i: (i % blk_per_seq, 0))
        in_specs += [tab_spec] * 4
        args += list(tabs)
    widths = [768, KVM_W, 512, 256, 512, 1024, 1024, 512, 4 * D]
    out_specs = [tok(w) for w in widths]
    out_shape = [jax.ShapeDtypeStruct((n_tok, w), BF16) for w in widths]
    if not latent:
        cw = [MLA_KV_LORA, MLA_ROPE, 128, 128, 512, 512]
        out_specs += [tok(w) for w in cw]
        out_shape += [jax.ShapeDtypeStruct((n_tok, w), F32) for w in cw]
    return pl.pallas_call(
        functools.partial(_inprep_kernel, latent),
        grid=(nblk,),
        in_specs=in_specs, out_specs=out_specs, out_shape=out_shape,
        compiler_params=_cparams(("arbitrary",)),
        name="inprep_lat" if latent else "inprep_ctx",
    )(*args)


def _pastkv_kernel(ckv_ref, kpe_ref, wukv_ref, o_ref):
    kv = _dot(ckv_ref[...].astype(BF16), wukv_ref[...])
    kpe_b = kpe_ref[...].astype(BF16)
    for p in range(4):
        o_ref[:, p * 256:p * 256 + LANES] = kv[:, p * LANES:(p + 1) * LANES].astype(BF16)
        o_ref[:, p * 256 + LANES:(p + 1) * 256] = kpe_b
    o_ref[:, 1024:1536] = kv[:, 512:1024].astype(BF16)


def _pastkv_call(ckv, kpe4, w_ukv):
    n = ckv.shape[0]
    return pl.pallas_call(
        _pastkv_kernel,
        grid=(n // TM,),
        in_specs=[pl.BlockSpec((TM, MLA_KV_LORA), lambda i: (i, 0)),
                  pl.BlockSpec((TM, LANES), lambda i: (i, 0)),
                  _const_spec((MLA_KV_LORA, 1024))],
        out_specs=pl.BlockSpec((TM, KVM_W), lambda i: (i, 0)),
        out_shape=jax.ShapeDtypeStruct((n, KVM_W), BF16),
        compiler_params=_cparams(("arbitrary",)),
        name="pastkv",
    )(ckv, kpe4, w_ukv)


def _softmax_pv(s, v):
    m = jnp.max(s, axis=-1, keepdims=True)
    p = jnp.exp(s - m)
    l = jnp.sum(p, axis=-1, keepdims=True)
    return _dot(p.astype(BF16), v) / l


def _attn_kernel(lam_init, qm_ref, kvm_ref, gq_ref, gkv_ref, dq_ref, dkv_ref,
                 lam_ref, gdiff_ref, o_ref):
    tq = qm_ref.shape[0]
    lane = _lane_iota((tq, LANES))
    low = lane < HALF_LANES
    zero = jnp.zeros((tq, LANES), BF16)

    for p in range(MLA_HEADS // 2):
        qn = qm_ref[:, p * LANES:(p + 1) * LANES]
        g = p // 2
        qpe = qm_ref[:, 512 + g * LANES:512 + (g + 1) * LANES]
        kk = kvm_ref[:, p * 256:(p + 1) * 256]
        vv = kvm_ref[:, 1024 + p * LANES:1024 + (p + 1) * LANES]
        outs = []
        for half in range(2):
            h = 2 * p + half
            slot = h % 4
            in_slot = (lane >= slot * MLA_ROPE) & (lane < (slot + 1) * MLA_ROPE)
            lhs = jnp.concatenate(
                [jnp.where(low if half == 0 else ~low, qn, zero),
                 jnp.where(in_slot, qpe, zero)], axis=1)
            outs.append(_softmax_pv(_dot_nt(lhs, kk), vv))
        o_ref[:, p * LANES:(p + 1) * LANES] = jnp.where(low, outs[0], outs[1]).astype(BF16)

    kk = gkv_ref[:, 0:LANES]
    vv = gkv_ref[:, LANES:2 * LANES]
    for g in range(GQA_HEADS // 2):
        qg = gq_ref[:, g * LANES:(g + 1) * LANES]
        o_lo = _softmax_pv(_dot_nt(jnp.where(low, qg, zero), kk), vv)
        o_hi = _softmax_pv(_dot_nt(jnp.where(low, zero, qg), kk), vv)
        o_ref[:, 512 + g * LANES:512 + (g + 1) * LANES] = jnp.where(low, o_lo, o_hi).astype(BF16)

    lp = lam_ref[...]
    lam = (jnp.exp(jnp.sum(lp[0:1] * lp[1:2], axis=-1, keepdims=True))
           - jnp.exp(jnp.sum(lp[2:3] * lp[3:4], axis=-1, keepdims=True)) + lam_init)
    for h in range(DIFF_HEADS):
        qh = dq_ref[:, h * LANES:(h + 1) * LANES]
        kk = dkv_ref[:, h * LANES:(h + 1) * LANES]
        vv = dkv_ref[:, 512 + h * LANES:512 + (h + 1) * LANES]
        a1 = _softmax_pv(_dot_nt(jnp.where(low, qh, zero), kk), vv)
        a2 = _softmax_pv(_dot_nt(jnp.where(low, zero, qh), kk), vv)
        od = _rms(a1 - lam * a2, gdiff_ref[...]) * (1.0 - lam_init)
        o_ref[:, 1024 + h * LANES:1024 + (h + 1) * LANES] = od.astype(BF16)


def _attn_call(lam_init, qm, kvm, gq, gkv, dq, dkv, lam_p, g_diff, n_b, t_len, s_len):
    nq = t_len // TQ

    def qspec(w):
        return pl.BlockSpec((TQ, w), lambda b, i: (b * nq + i, 0))

    def kspec(w):
        return pl.BlockSpec((s_len, w), lambda b, i: (b, 0))

    return pl.pallas_call(
        functools.partial(_attn_kernel, lam_init),
        grid=(n_b, nq),
        in_specs=[qspec(768), kspec(KVM_W), qspec(512), kspec(256), qspec(512), kspec(1024),
                  _const_spec((4, DIFF_D)), _const_spec((1, DIFF_DV))],
        out_specs=qspec(3 * BRANCH_W),
        out_shape=jax.ShapeDtypeStruct((n_b * t_len, 3 * BRANCH_W), BF16),
        compiler_params=_cparams(("arbitrary", "arbitrary")),
        name="attn",
    )(qm, kvm, gq, gkv, dq, dkv, lam_p, g_diff)


def _log_sigmoid(x):
    return jnp.minimum(x, 0.0) - jnp.log(1.0 + jnp.exp(-jnp.abs(x)))


def _ret_kernel(latent, t_len, dec_ref, q_ref, k_ref, v_ref, rg_ref, gret_ref, *refs):
    if latent:
        s0_ref, o_ref = refs
    else:
        o_ref, st_ref = refs
    tq = q_ref.shape[0]
    t0 = pl.program_id(1) * tq
    lane = _lane_iota((tq, LANES))
    low = lane < HALF_LANES
    zero = jnp.zeros((tq, LANES), BF16)
    t_idx = (t0 + lax.broadcasted_iota(jnp.int32, (tq, t_len), 0)).astype(F32)
    s_idx = lax.broadcasted_iota(jnp.int32, (tq, t_len), 1).astype(F32)
    dist = t_idx - s_idx
    t_col = (t0 + lax.broadcasted_iota(jnp.int32, (tq, 1), 0)).astype(F32)

    def lg(d, h):
        return _log_sigmoid(jnp.full((1, 1), dec_ref[d, h], F32))

    for h in range(RET_HEADS):
        p, half = h // 2, h % 2
        qp = q_ref[:, p * LANES:(p + 1) * LANES]
        qm = jnp.where(low if half == 0 else ~low, qp, zero)
        kp = k_ref[:, p * LANES:(p + 1) * LANES]
        vh = v_ref[:, h * LANES:(h + 1) * LANES]
        lgf, lgb = lg(0, h), lg(1, h)
        dmask = (jnp.where(dist >= 0, jnp.exp(lgf * jnp.maximum(dist, 0.0)), 0.0)
                 + jnp.where(dist <= 0, jnp.exp(lgb * jnp.maximum(-dist, 0.0)), 0.0))
        o = _dot((_dot_nt(qm, kp) * dmask).astype(BF16), vh)
        if latent:
            sf = s0_ref[0, p].astype(BF16)
            sb = s0_ref[1, p].astype(BF16)
            o = o + _dot(qm, sf) * jnp.exp(lgf * (t_col + 1.0))
            o = o + _dot(qm, sb) * jnp.exp(lgb * (float(t_len) - t_col))
        mu = jnp.mean(o, axis=-1, keepdims=True)
        oc = o - mu
        y = oc * lax.rsqrt(jnp.mean(oc * oc, axis=-1, keepdims=True) + EPS)
        y = y * gret_ref[:, h * LANES:(h + 1) * LANES]
        rg = rg_ref[:, h * LANES:(h + 1) * LANES].astype(F32)
        o_ref[:, h * LANES:(h + 1) * LANES] = (y * _silu(rg)).astype(BF16)

    if not latent:
        s_col = lax.broadcasted_iota(jnp.int32, (t_len, 1), 0).astype(F32)
        lane_t = _lane_iota((1, LANES)) < HALF_LANES
        for p in range(RET_HEADS // 2):
            kp = k_ref[:, p * LANES:(p + 1) * LANES].astype(F32)
            for d in range(2):
                lg_lane = jnp.where(lane_t, lg(d, 2 * p), lg(d, 2 * p + 1))
                expo = (float(t_len) - 1.0 - s_col) if d == 0 else s_col
                kdec_t = jnp.transpose(kp * jnp.exp(lg_lane * expo)).astype(BF16)
                for half in range(2):
                    h = 2 * p + half
                    st = _dot(kdec_t, v_ref[:, h * LANES:(h + 1) * LANES])
                    st_ref[d, h] = st[half * RET_DK:(half + 1) * RET_DK, :]


def _ret_call(latent, dec, ret, rg, g_ret, s0, n_b, t_len):
    nq = t_len // TQ
    assert latent or nq == 1
    in_specs = [pl.BlockSpec(memory_space=pltpu.SMEM),
                pl.BlockSpec((TQ, 256), lambda b, i: (b * nq + i, 0)),
                pl.BlockSpec((t_len, 256), lambda b, i: (b, 1)),
                pl.BlockSpec((t_len, 512), lambda b, i: (b, 1)),
                pl.BlockSpec((TQ, 512), lambda b, i: (b * nq + i, 0)),
                _const_spec((1, 512))]
    args = [dec, ret, ret, ret, rg, g_ret]
    out_specs = [pl.BlockSpec((TQ, 512), lambda b, i: (b * nq + i, 0))]
    out_shape = [jax.ShapeDtypeStruct((n_b * t_len, 512), BF16)]
    if latent:
        in_specs.append(pl.BlockSpec((None, 2, 2, LANES, LANES), lambda b, i: (b, 0, 0, 0, 0)))
        args.append(s0)
    else:
        out_specs.append(pl.BlockSpec((None, 2, RET_HEADS, RET_DK, RET_DV),
                                      lambda b, i: (b, 0, 0, 0, 0)))
        out_shape.append(jax.ShapeDtypeStruct((n_b, 2, RET_HEADS, RET_DK, RET_DV), F32))
    return pl.pallas_call(
        functools.partial(_ret_kernel, latent, t_len),
        grid=(n_b, nq),
        in_specs=in_specs, out_specs=out_specs, out_shape=out_shape,
        compiler_params=_cparams(("arbitrary", "arbitrary")),
        name="ret_lat" if latent else "ret_ctx",
    )(*args)


def _merge_kernel(x_ref, mod_ref, br_ref, or_ref, gate_ref, wbr_ref, wout_ref, gpost_ref, o_ref):
    merged = None
    for n in range(N_BRANCH):
        if n < 2:
            b = br_ref[:, n * BRANCH_W:(n + 1) * BRANCH_W]
        elif n == 2:
            b = or_ref[...]
        else:
            b = br_ref[:, 2 * BRANCH_W:3 * BRANCH_W]
        t = gate_ref[:, n * D:(n + 1) * D].astype(F32) * _dot(b, wbr_ref[n])
        merged = t if merged is None else merged + t
    out = _dot(merged.astype(BF16), wout_ref[...])
    g1 = mod_ref[...][:, 2 * D:3 * D]
    o_ref[...] = x_ref[...] + g1 * _rms(out, gpost_ref[...])


def _merge_call(x, mod3, mod_row, br, o_r, gates, lw):
    n_tok = x.shape[0]

    def tok(w):
        return pl.BlockSpec((TM, w), lambda i: (i, 0))

    return pl.pallas_call(
        _merge_kernel,
        grid=(n_tok // TM,),
        in_specs=[tok(D), pl.BlockSpec((None, 1, 6 * D), lambda i: (mod_row(i), 0, 0)),
                  tok(3 * BRANCH_W), tok(BRANCH_W), tok(4 * D),
                  _const_spec((N_BRANCH, BRANCH_W, D)), _const_spec((D, D)), _const_spec((1, D))],
        out_specs=tok(D),
        out_shape=jax.ShapeDtypeStruct((n_tok, D), F32),
        compiler_params=_cparams(("arbitrary",)),
        name="merge",
    )(x, mod3, br, o_r, gates, lw["w_br"], lw["w_out"], lw["g_post1"])


def _route(logits_t, bias):
    n = logits_t.shape[1]
    scores = jax.nn.sigmoid(logits_t)
    sel = scores + bias
    neg = -jnp.inf
    sub = lax.broadcasted_iota(jnp.int32, (GROUP_SIZE, n), 0)
    grp = []
    for g in range(N_GROUPS):
        blk = sel[g * GROUP_SIZE:(g + 1) * GROUP_SIZE]
        m1 = jnp.max(blk, axis=0, keepdims=True)
        i1 = jnp.min(jnp.where(blk == m1, sub, GROUP_SIZE), axis=0, keepdims=True)
        m2 = jnp.max(jnp.where(sub == i1, neg, blk), axis=0, keepdims=True)
        grp.append(m1 + m2)
    parts = []
    for g in range(N_GROUPS):
        beaten = jnp.zeros((1, n), jnp.int32)
        for o in range(N_GROUPS):
            if o == g:
                continue
            wins = (grp[o] > grp[g]) | (grp[o] == grp[g]) if o < g else (grp[o] > grp[g])
            beaten = beaten + wins.astype(jnp.int32)
        keep = beaten < TOPK_GROUPS
        parts.append(jnp.where(keep, sel[g * GROUP_SIZE:(g + 1) * GROUP_SIZE], neg))
    cur = jnp.concatenate(parts, axis=0)
    eidx = lax.broadcasted_iota(jnp.int32, (N_EXPERTS, n), 0)
    hits, ids, ws = [], [], []
    for _ in range(TOP_K):
        m = jnp.max(cur, axis=0, keepdims=True)
        i = jnp.min(jnp.where(cur == m, eidx, N_EXPERTS), axis=0, keepdims=True)
        hit = eidx == i
        hits.append(hit)
        ids.append(i)
        ws.append(jnp.sum(jnp.where(hit, scores, 0.0), axis=0, keepdims=True))
        cur = jnp.where(hit, neg, cur)
    wsum = ws[0] + ws[1] + ws[2] + ws[3]
    return hits, ids, [w / wsum * ROUTE_SCALE for w in ws]


U32 = jnp.uint32
HIGH16 = np.uint32(0xFFFF0000)


def _bf16_bits(v):
    return lax.bitcast_convert_type(v.astype(BF16).astype(F32), U32)


def _pack_rows(v):
    return (_bf16_bits(v[:, 0:D // 2]) >> 16) | _bf16_bits(v[:, D // 2:D])


def _unpack_rows(p):
    lo = lax.bitcast_convert_type(p << 16, F32)
    hi = lax.bitcast_convert_type(p & HIGH16, F32)
    return jnp.concatenate([lo, hi], axis=1)


def _moe_pre_kernel(x_ref, mod_ref, gpre_ref, wr_ref, br_ref, tri_ref,
                    hp_ref, eidx_ref, rank_ref, comb_ref, cnt_ref, run_ref):
    tm = x_ref.shape[0]

    @pl.when(pl.program_id(0) == 0)
    def _():
        run_ref[...] = jnp.zeros_like(run_ref)

    mod = mod_ref[...]
    sh2, sc2 = mod[:, 3 * D:4 * D], mod[:, 4 * D:5 * D]
    h = _rms(x_ref[...], gpre_ref[...]) * (1.0 + sc2) + sh2
    hp_ref[...] = _pack_rows(h)
    hb = h.astype(BF16)
    h_lo = (h - hb.astype(F32)).astype(BF16)
    wr = wr_ref[...]
    wr_hi = wr.astype(BF16)
    wr_lo = (wr - wr_hi.astype(F32)).astype(BF16)
    logits_t = _dot_nt(wr_hi, hb) + _dot_nt(wr_hi, h_lo) + _dot_nt(wr_lo, hb)
    hits, ids, ws = _route(logits_t, br_ref[...])

    picked = jnp.zeros((N_EXPERTS, tm), F32)
    for hit in hits:
        picked = jnp.where(hit, 1.0, picked)
    before = _dot(picked.astype(BF16), tri_ref[...]) + run_ref[:, 0:1]
    sub8 = lax.broadcasted_iota(jnp.int32, (8, tm), 0)
    comb8 = jnp.zeros((8, tm), F32)
    for k in range(TOP_K):
        rank = jnp.sum(jnp.where(hits[k], before, 0.0), axis=0, keepdims=True)
        eidx_ref[k:k + 1, :] = ids[k]
        rank_ref[k:k + 1, :] = rank.astype(jnp.int32)
        comb8 = jnp.where(sub8 == k, ws[k], comb8)
    comb_ref[...] = jnp.transpose(
        jnp.concatenate([comb8, jnp.zeros((LANES - 8, tm), F32)], axis=0))
    run_ref[...] = run_ref[...] + jnp.sum(picked, axis=1, keepdims=True)
    cnt_ref[...] = run_ref[...]


def _moe_pre_call(x, mod3, mod_row, lw):
    n_tok = x.shape[0]
    tm = TM_MOE_PRE
    tri = np.arange(tm)
    tri = jnp.asarray(tri[:, None] < tri[None, :], BF16)
    row4 = pl.BlockSpec((TOP_K, tm), lambda i: (0, i))
    return pl.pallas_call(
        _moe_pre_kernel,
        grid=(n_tok // tm,),
        in_specs=[pl.BlockSpec((tm, D), lambda i: (i, 0)),
                  pl.BlockSpec((None, 1, 6 * D), lambda i: (mod_row(i), 0, 0)),
                  _const_spec((1, D)), _const_spec((N_EXPERTS, D)), _const_spec((N_EXPERTS, 1)),
                  _const_spec((tm, tm))],
        out_specs=[pl.BlockSpec((tm, D // 2), lambda i: (i, 0)), row4, row4,
                   pl.BlockSpec((tm, LANES), lambda i: (i, 0)),
                   _const_spec((N_EXPERTS, LANES))],
        out_shape=[jax.ShapeDtypeStruct((n_tok, D // 2), U32),
                   jax.ShapeDtypeStruct((TOP_K, n_tok), jnp.int32),
                   jax.ShapeDtypeStruct((TOP_K, n_tok), jnp.int32),
                   jax.ShapeDtypeStruct((n_tok, LANES), F32),
                   jax.ShapeDtypeStruct((N_EXPERTS, LANES), F32)],
        scratch_shapes=[pltpu.VMEM((N_EXPERTS, LANES), F32)],
        compiler_params=_cparams(("arbitrary",)),
        name="moe_pre",
    )(x, mod3, lw["g_pre2"], lw["w_router_t"], lw["b_router"], tri)


def _moe_plan_kernel(eidx_ref, rank_ref, cnt_ref, dest_ref, te_ref, tv_ref):
    tm = eidx_ref.shape[1]
    cnt = cnt_ref[...]
    padded = jnp.ceil(cnt * (1.0 / TMX)) * TMX
    row = lax.broadcasted_iota(jnp.int32, cnt.shape, 0)
    incl = padded
    shift = 1
    while shift < N_EXPERTS:
        incl = incl + jnp.where(row >= shift, pltpu.roll(incl, shift, 0), 0.0)
        shift *= 2
    start = (incl - padded)[:, 0:1]
    end = incl[:, 0:1]
    erow = lax.broadcasted_iota(jnp.int32, (N_EXPERTS, tm), 0)
    for k in range(TOP_K):
        mine = erow == eidx_ref[k:k + 1, :]
        base = jnp.sum(jnp.where(mine, start, 0.0), axis=0, keepdims=True)
        dest_ref[k:k + 1, :] = rank_ref[k:k + 1, :] + base.astype(jnp.int32)

    @pl.when(pl.program_id(0) == 0)
    def _():
        tile0 = (_lane_iota((1, LANES)) * TMX).astype(F32)
        owner = jnp.sum(jnp.where(end <= tile0, 1.0, 0.0), axis=0, keepdims=True)
        owner = jnp.minimum(owner, N_EXPERTS - 1.0)
        erow_t = lax.broadcasted_iota(jnp.int32, (N_EXPERTS, LANES), 0).astype(F32)
        left = jnp.sum(jnp.where(erow_t == owner, cnt[:, 0:1] - (tile0 - start), 0.0),
                       axis=0, keepdims=True)
        te_ref[...] = owner.astype(jnp.int32)
        tv_ref[...] = jnp.clip(left, 0.0, float(TMX)).astype(jnp.int32)


def _moe_plan_call(eidx, rank, cnt):
    n_tok = eidx.shape[1]
    tm = TM_MOE_PRE
    row4 = pl.BlockSpec((TOP_K, tm), lambda i: (0, i))
    return pl.pallas_call(
        _moe_plan_kernel,
        grid=(n_tok // tm,),
        in_specs=[row4, row4, _const_spec((N_EXPERTS, LANES))],
        out_specs=[row4, _const_spec((1, LANES)), _const_spec((1, LANES))],
        out_shape=[jax.ShapeDtypeStruct((TOP_K, n_tok), jnp.int32),
                   jax.ShapeDtypeStruct((1, LANES), jnp.int32),
                   jax.ShapeDtypeStruct((1, LANES), jnp.int32)],
        compiler_params=_cparams(("arbitrary",)),
        name="moe_plan",
    )(eidx, rank, cnt)


def _experts_kernel(te_ref, tv_ref, xs_ref, wgu_ref, wdn_ref, ys_ref, wgu_b, wdn_b):
    j = pl.program_id(0)
    valid = tv_ref[j]
    prev = te_ref[jnp.maximum(j - 1, 0)]

    @pl.when((j == 0) | (te_ref[j] != prev))
    def _():
        wgu_b[...] = wgu_ref[...].astype(BF16)
        wdn_b[...] = wdn_ref[...].astype(BF16)

    @pl.when(valid > 0)
    def _():
        rows = lax.broadcasted_iota(jnp.int32, (TMX, D), 0)
        x = jnp.where(rows < valid, _unpack_rows(xs_ref[...]), 0.0).astype(BF16)
        gu = _dot(x, wgu_b[...])
        a = _silu(gu[:, 0:EXPERT_FF]) * gu[:, EXPERT_FF:2 * EXPERT_FF]
        ys_ref[...] = _pack_rows(_dot(a.astype(BF16), wdn_b[...]))

    @pl.when(valid <= 0)
    def _():
        ys_ref[...] = jnp.zeros_like(ys_ref)


def _experts_call(l, xs, te, tv, w_gu, w_dn):
    n_tiles = xs.shape[0] // TMX
    grid_spec = pltpu.PrefetchScalarGridSpec(
        num_scalar_prefetch=2,
        grid=(n_tiles,),
        in_specs=[pl.BlockSpec((TMX, D // 2), lambda j, te, tv: (j, 0)),
                  pl.BlockSpec((None, None, D, 2 * EXPERT_FF), lambda j, te, tv: (l, te[j], 0, 0)),
                  pl.BlockSpec((None, None, EXPERT_FF, D), lambda j, te, tv: (l, te[j], 0, 0))],
        out_specs=pl.BlockSpec((TMX, D // 2), lambda j, te, tv: (j, 0)),
        scratch_shapes=[pltpu.VMEM((D, 2 * EXPERT_FF), BF16), pltpu.VMEM((EXPERT_FF, D), BF16)])
    return pl.pallas_call(
        _experts_kernel,
        grid_spec=grid_spec,
        out_shape=jax.ShapeDtypeStruct(xs.shape, U32),
        compiler_params=_cparams(("arbitrary",)),
        name="moe_experts",
    )(te, tv, xs, w_gu, w_dn)


def _moe_post_kernel(x_ref, mod_ref, hp_ref, yg_ref, comb_ref, wsgu_ref, wsdn_ref, gpost_ref,
                     o_ref):
    hb = _unpack_rows(hp_ref[...]).astype(BF16)
    sgu = _dot(hb, wsgu_ref[...])
    sa = _silu(sgu[:, 0:SHARED_FF]) * sgu[:, SHARED_FF:2 * SHARED_FF]
    acc = _dot(sa.astype(BF16), wsdn_ref[...])
    comb = comb_ref[...]
    for k in range(TOP_K):
        acc = acc + comb[:, k:k + 1] * _unpack_rows(yg_ref[k])
    g2 = mod_ref[...][:, 5 * D:6 * D]
    o_ref[...] = x_ref[...] + g2 * _rms(acc, gpost_ref[...])


def _moe_post_call(x, mod3, mod_row, hp, yg, comb, lw):
    n_tok = x.shape[0]
    tm = TM_MOE_PRE
    return pl.pallas_call(
        _moe_post_kernel,
        grid=(n_tok // tm,),
        in_specs=[pl.BlockSpec((tm, D), lambda i: (i, 0)),
                  pl.BlockSpec((None, 1, 6 * D), lambda i: (mod_row(i), 0, 0)),
                  pl.BlockSpec((tm, D // 2), lambda i: (i, 0)),
                  pl.BlockSpec((TOP_K, tm, D // 2), lambda i: (0, i, 0)),
                  pl.BlockSpec((tm, LANES), lambda i: (i, 0)),
                  _const_spec((D, 2 * SHARED_FF)), _const_spec((SHARED_FF, D)),
                  _const_spec((1, D))],
        out_specs=pl.BlockSpec((tm, D), lambda i: (i, 0)),
        out_shape=jax.ShapeDtypeStruct((n_tok, D), F32),
        compiler_params=_cparams(("arbitrary",)),
        name="moe_post",
    )(x, mod3, hp, yg, comb, lw["w_sh_gu"], lw["w_sh_down"], lw["g_post2"])


def _moe_call(l, x, mod3, mod_row, lw):
    n_tok = x.shape[0]
    n_slots = -(-(TOP_K * n_tok + N_EXPERTS * (TMX - 1)) // TMX) * TMX
    assert n_slots // TMX <= LANES
    hp, eidx, rank, comb, cnt = _moe_pre_call(x, mod3, mod_row, lw)
    dest, te, tv = _moe_plan_call(eidx, rank, cnt)
    dest = dest.reshape(TOP_K * n_tok)
    xs = _sc_scatter_rows(hp, dest, n_slots)
    ys = _experts_call(l, xs, te[0, :n_slots // TMX], tv[0, :n_slots // TMX],
                       lw["w_exp_gu"], lw["w_exp_down"])
    yg = _sc_gather_rows(ys, dest).reshape(TOP_K, n_tok, D // 2)
    return _moe_post_call(x, mod3, mod_row, hp, yg, comb, lw)


SC_CORES, SC_SUBCORES = 2, 16
SC_WORKERS = SC_CORES * SC_SUBCORES


def _sc_gather_rows(table, idx, chunk=64):
    n_out, width = idx.shape[0], table.shape[1]
    per_worker = n_out // SC_WORKERS
    n_chunks = per_worker // chunk
    assert per_worker * SC_WORKERS == n_out and n_chunks * chunk == per_worker
    mesh = plsc.VectorSubcoreMesh(core_axis_name="c", subcore_axis_name="s",
                                  num_cores=SC_CORES, num_subcores=SC_SUBCORES)

    @functools.partial(
        pl.kernel, mesh=mesh,
        out_type=jax.ShapeDtypeStruct((n_out, width), table.dtype),
        scratch_types=[pltpu.VMEM((chunk,), jnp.int32), pltpu.VMEM((chunk, width), table.dtype),
                       pltpu.SemaphoreType.DMA],
        name="sc_gather")
    def gather(table_hbm, idx_hbm, out_hbm, idx_v, rows_v, sem):
        base = (lax.axis_index("s") * SC_CORES + lax.axis_index("c")) * per_worker

        @pl.loop(0, n_chunks)
        def _(j):
            off = base + j * chunk
            pltpu.sync_copy(idx_hbm.at[pl.ds(off, chunk)], idx_v)
            pltpu.async_copy(table_hbm.at[idx_v], rows_v, sem).wait()
            pltpu.sync_copy(rows_v, out_hbm.at[pl.ds(off, chunk)])

    return gather(table, idx)


def _sc_scatter_rows(rows, dest, n_slots, chunk=64):
    n_tok, width = rows.shape
    per_worker = n_tok // SC_WORKERS
    n_chunks = per_worker // chunk
    assert per_worker * SC_WORKERS == n_tok and n_chunks * chunk == per_worker
    mesh = plsc.VectorSubcoreMesh(core_axis_name="c", subcore_axis_name="s",
                                  num_cores=SC_CORES, num_subcores=SC_SUBCORES)

    @functools.partial(
        pl.kernel, mesh=mesh,
        out_type=jax.ShapeDtypeStruct((n_slots, width), rows.dtype),
        scratch_types=[pltpu.VMEM((chunk,), jnp.int32), pltpu.VMEM((chunk, width), rows.dtype)],
        name="sc_scatter")
    def scatter(rows_hbm, dest_hbm, out_hbm, idx_v, rows_v):
        base = (lax.axis_index("s") * SC_CORES + lax.axis_index("c")) * per_worker

        @pl.loop(0, n_chunks)
        def _(j):
            off = base + j * chunk
            pltpu.sync_copy(rows_hbm.at[pl.ds(off, chunk)], rows_v)
            for k in range(TOP_K):
                pltpu.sync_copy(dest_hbm.at[pl.ds(k * n_tok + off, chunk)], idx_v)
                pltpu.sync_copy(rows_v, out_hbm.at[idx_v])

    return scatter(rows, dest)


def _rope_tables(t_len):
    pos = np.arange(t_len)
    row, col = pos // GRID_W, pos % GRID_W

    def tab(r):
        half = r // 2
        freq = ROPE_BASE ** (-np.arange(half, dtype=np.float64) / half)
        sign = np.concatenate([-np.ones(half), np.ones(half)])
        cs, sn = [], []
        for p in (row, col):
            ang = p[:, None].astype(np.float64) * freq[None, :]
            cs.append(np.concatenate([np.cos(ang), np.cos(ang)], axis=1))
            sn.append(np.concatenate([np.sin(ang), np.sin(ang)], axis=1) * sign[None, :])
        return np.concatenate(cs, axis=1), np.concatenate(sn, axis=1)

    c64, s64 = tab(GQA_HD // 2)
    cpe, spe = tab(MLA_ROPE // 2)
    out = (np.tile(c64, (1, 2)), np.tile(s64, (1, 2)), np.tile(cpe, (1, 4)), np.tile(spe, (1, 4)))
    return tuple(jnp.asarray(a, F32) for a in out)


def _layer_weights(l, p):
    w_uq = p["w_mla_uq"][l].reshape(MLA_Q_LORA, MLA_HEADS, MLA_NOPE + MLA_ROPE)
    w_uq = jnp.concatenate([w_uq[:, :, :MLA_NOPE].reshape(MLA_Q_LORA, -1),
                            w_uq[:, :, MLA_NOPE:].reshape(MLA_Q_LORA, -1)], axis=1)
    w_ukv = p["w_mla_ukv"][l].reshape(MLA_KV_LORA, MLA_HEADS, MLA_NOPE + MLA_V)
    w_ukv = jnp.concatenate([w_ukv[:, :, :MLA_NOPE].reshape(MLA_KV_LORA, -1),
                             w_ukv[:, :, MLA_NOPE:].reshape(MLA_KV_LORA, -1)], axis=1)
    w_br = p["w_br"][l]
    w_br_gqa = w_br[1].reshape(GQA_HEADS, GQA_HD, D)[jnp.array(GQA_ORDER)].reshape(BRANCH_W, D)
    w_br = jnp.stack([w_br[0], w_br_gqa, w_br[2], w_br[3]], axis=0)
    blk = np.arange(512) // GQA_HD
    return {
        "g_pre1": p["g_pre1"][l].reshape(1, D), "g_post1": p["g_post1"][l].reshape(1, D),
        "g_pre2": p["g_pre2"][l].reshape(1, D), "g_post2": p["g_post2"][l].reshape(1, D),
        "w_in": p["w_in_packed"],
        "g_mla_q": p["g_mla_q"][l].reshape(1, -1), "w_uq": w_uq.astype(BF16),
        "g_mla_kv": p["g_mla_kv"][l].reshape(1, -1), "w_ukv": w_ukv.astype(BF16),
        "g_gqa_q": jnp.tile(p["g_gqa_q"][l], GQA_HEADS).reshape(1, -1),
        "g_gqa_k": jnp.tile(p["g_gqa_k"][l], GQA_KV_HEADS).reshape(1, -1),
        "bd": jnp.asarray(blk[:, None] == blk[None, :], BF16),
        "ret_decay": p["ret_decay"][l],
        "g_ret": p["g_ret"][l].reshape(1, -1),
        "diff_lambda": p["diff_lambda"][l], "g_diff": p["g_diff"][l].reshape(1, -1),
        "w_br": w_br.astype(BF16), "w_out": p["w_out"][l].astype(BF16),
        "w_router_t": p["w_router"][l].T, "b_router": p["b_router"][l].reshape(-1, 1),
        "w_exp_gu": p["w_exp_gu"], "w_exp_down": p["w_exp_down"],
        "w_sh_gu": p["w_sh_gu"][l].astype(BF16), "w_sh_down": p["w_sh_down"][l].astype(BF16),
    }


def _mixers(latent, l, x, mod3, mod_row, lw, n_b, t_len, tabs=None, past=None, s0=None):
    lam_init = 0.8 - 0.6 * math.exp(-0.3 * l)
    outs = _inprep_call(latent, l, x, mod3, mod_row, lw, tabs, t_len)
    qm, kvm, gq, gkv, dq, dkv, ret, rg, gates = outs[:9]
    s_len = t_len
    if latent:
        past_kvm, past_gkv, past_dkv = past
        p_len = past_gkv.shape[1]
        s_len = p_len + t_len

        def cat(a, b):
            return jnp.concatenate([a, b.reshape(n_b, t_len, -1)], axis=1).reshape(n_b * s_len, -1)

        kvm, gkv, dkv = cat(past_kvm, kvm), cat(past_gkv, gkv), cat(past_dkv, dkv)
    br = _attn_call(lam_init, qm, kvm, gq, gkv, dq, dkv, lw["diff_lambda"], lw["g_diff"],
                    n_b, t_len, s_len)
    r = _ret_call(latent, lw["ret_decay"], ret, rg, lw["g_ret"], s0, n_b, t_len)
    y = _merge_call(x, mod3, mod_row, br, r[0], gates, lw)
    cache = None if latent else tuple(outs[9:]) + (r[1],)
    return y, cache


def kernel(x_prompt, x_sample, cache_mla_ckv, cache_mla_kpe, cache_gqa_k, cache_gqa_v, cache_diff_k, cache_diff_v, state_ret, c, c_ctx, w_mod, b_mod, g_pre1, g_post1, g_pre2, g_post2, w_in, g_mla_q, w_mla_uq, g_mla_kv, w_mla_ukv, g_gqa_q, g_gqa_k, ret_decay, g_ret, diff_lambda, g_diff, w_br, w_out, w_router, b_router, w_exp_gu, w_exp_down, w_sh_gu, w_sh_down):
    params = dict(w_in_packed=jnp.swapaxes(w_in, 1, 2).astype(BF16), g_pre1=g_pre1, g_post1=g_post1, g_pre2=g_pre2,
                  g_post2=g_post2, g_mla_q=g_mla_q, w_mla_uq=w_mla_uq,
                  g_mla_kv=g_mla_kv, w_mla_ukv=w_mla_ukv, g_gqa_q=g_gqa_q, g_gqa_k=g_gqa_k,
                  ret_decay=ret_decay, g_ret=g_ret, diff_lambda=diff_lambda, g_diff=g_diff,
                  w_br=w_br, w_out=w_out, w_router=w_router, b_router=b_router,
                  w_exp_gu=w_exp_gu, w_exp_down=w_exp_down, w_sh_gu=w_sh_gu, w_sh_down=w_sh_down)
    n_bc, t_c, _ = x_prompt.shape
    n_bl, t_l, _ = x_sample.shape
    p_len = cache_mla_ckv.shape[2]
    tabs = _rope_tables(t_l)
    n_cond = 8
    cond = jnp.concatenate([c_ctx[None, :], c, jnp.zeros((n_cond - 1 - n_bl, D), F32)], axis=0)
    blk_c, blk_l = t_c // TM, t_l // TM
    assert t_l % TM_MOE_PRE == 0 and (t_c * n_bc) % TM_MOE_PRE == 0

    yp = x_prompt.reshape(n_bc * t_c, D)
    ys = x_sample.reshape(n_bl * t_l, D)
    caches = []
    for l in range(DEPTH):
        lw = _layer_weights(l, params)
        mod3 = _mod_call(l, cond, w_mod, b_mod).reshape(n_cond, 1, 6 * D)
        yp, cache = _mixers(False, l, yp, mod3, lambda i: 0, lw, n_bc, t_c)
        yp = _moe_call(l, yp, mod3, lambda i: 0, lw)
        caches.append(cache)
        past_kvm = _pastkv_call(cache_mla_ckv[:, l].reshape(n_bl * p_len, -1),
                                jnp.tile(cache_mla_kpe[:, l].reshape(n_bl * p_len, -1), (1, 4)),
                                lw["w_ukv"]).reshape(n_bl, p_len, -1)
        past_gkv = jnp.concatenate([cache_gqa_k[:, l].reshape(n_bl, p_len, -1),
                                    cache_gqa_v[:, l].reshape(n_bl, p_len, -1)], axis=-1).astype(BF16)
        past_dkv = jnp.concatenate([cache_diff_k[:, l].reshape(n_bl, p_len, -1),
                                    cache_diff_v[:, l].reshape(n_bl, p_len, -1)], axis=-1).astype(BF16)
        s0 = state_ret[:, l].reshape(n_bl, 2, RET_HEADS // 2, 2 * RET_DK, RET_DV)
        ys, _ = _mixers(True, l, ys, mod3, lambda i: 1 + i // blk_l, lw, n_bl, t_l, tabs=tabs,
                        past=(past_kvm, past_gkv, past_dkv), s0=s0)
        ys = _moe_call(l, ys, mod3, lambda i: 1 + i // (t_l // TM_MOE_PRE), lw)

    def stack(k, shape):
        return jnp.stack([caches[l][k].reshape((n_bc, t_c) + shape) for l in range(DEPTH)], axis=1)

    new_ret = jnp.stack([caches[l][6] for l in range(DEPTH)], axis=1)
    return (yp.reshape(n_bc, t_c, D), ys.reshape(n_bl, t_l, D),
            stack(0, (MLA_KV_LORA,)), stack(1, (MLA_ROPE,)),
            stack(2, (GQA_KV_HEADS, GQA_HD)), stack(3, (GQA_KV_HEADS, GQA_HD)),
            stack(4, (DIFF_HEADS, 2, DIFF_D)), stack(5, (DIFF_HEADS, DIFF_DV)), new_ret)
```

```python
import functools
import math

import numpy as np
import jax
import jax.numpy as jnp
from jax import lax
from jax.experimental import pallas as pl
from jax.experimental.pallas import tpu as pltpu
from jax.experimental.pallas import tpu_sc as plsc

F32 = jnp.float32
BF16 = jnp.bfloat16

D = 1024
DEPTH = 2
GRID_W = 64
ROPE_BASE = 10000.0
EPS = 1e-6

MLA_HEADS, MLA_NOPE, MLA_ROPE, MLA_V = 8, 64, 32, 64
MLA_Q_LORA, MLA_KV_LORA = 384, 256
GQA_HEADS, GQA_KV_HEADS, GQA_HD = 8, 2, 64
RET_HEADS, RET_DK, RET_DV = 4, 64, 128
DIFF_HEADS, DIFF_D, DIFF_DV = 4, 64, 128
N_BRANCH, BRANCH_W = 4, 512
N_EXPERTS, TOP_K, N_GROUPS, TOPK_GROUPS = 32, 4, 4, 2
EXPERT_FF, SHARED_FF = 256, 256
ROUTE_SCALE = 2.5
GROUP_SIZE = N_EXPERTS // N_GROUPS

LANES = 128
HALF_LANES = 64
VMEM_LIMIT = 56 * 1024 * 1024

C_CQ, C_CKV, C_KPE, C_GQ, C_GK, C_GV = 0, 384, 640, 768, 1280, 1408
C_DQ, C_DK, C_DV, C_RQ, C_RK, C_RV, C_RG, C_GL, C_END = (
    1536, 2048, 2560, 3072, 3328, 3584, 4096, 4608, 8704)
O_CQ, O_CKV, O_KPE, O_GQ, O_GK, O_GV = 0, 384, 640, 672, 1184, 1312
O_RQ, O_RK, O_RV, O_RG, O_DQ, O_DK, O_DV, O_GL, O_END = (
    1440, 1696, 1952, 2464, 2976, 3488, 4000, 4512, 8608)
GQA_ORDER = (0, 4, 1, 5, 2, 6, 3, 7)

KVM_W = 4 * 256 + 512
TM = 256
TQ = 256
TM_MOE_PRE = 512
TMX = 256


def _cparams(sem):
    return pltpu.CompilerParams(dimension_semantics=sem, vmem_limit_bytes=VMEM_LIMIT)


def _const_spec(shape):
    nd = len(shape)
    return pl.BlockSpec(shape, lambda *_: (0,) * nd)


def _rms(x, g):
    return x * lax.rsqrt(jnp.mean(x * x, axis=-1, keepdims=True) + EPS) * g


def _dot(a, b):
    return jnp.dot(a, b, preferred_element_type=F32)


def _dot_nt(a, b):
    return lax.dot_general(a, b, (((1,), (1,)), ((), ())), preferred_element_type=F32)


def _silu(x):
    return x * jax.nn.sigmoid(x)


def _lane_iota(shape):
    return lax.broadcasted_iota(jnp.int32, shape, len(shape) - 1)


def _seg_meansq(x, bd_ref, width):
    sq = x * x
    hi = sq.astype(BF16)
    lo = (sq - hi.astype(F32)).astype(BF16)
    bd = bd_ref[0:width, 0:width]
    return (_dot(hi, bd) + _dot(lo, bd)) * (1.0 / GQA_HD)


def _rope(x, cos, sin_signed, half):
    width = x.shape[-1]
    first = (_lane_iota(x.shape) % (2 * half)) < half
    partner = jnp.where(first, pltpu.roll(x, width - half, 1), pltpu.roll(x, half, 1))
    return x * cos + partner * sin_signed


def _tile_lanes(t, reps):
    return t if reps == 1 else jnp.concatenate([t] * reps, axis=1)


def _mod_kernel(c_ref, w_ref, b_ref, o_ref):
    a = _silu(c_ref[...]).astype(BF16)
    o_ref[...] = _dot(a, w_ref[...].astype(BF16)) + b_ref[...]


def _mod_call(l, cond, w_mod, b_mod):
    n_l, _, n = w_mod.shape
    tn = 1536
    return pl.pallas_call(
        _mod_kernel,
        grid=(n // tn,),
        in_specs=[_const_spec(cond.shape),
                  pl.BlockSpec((None, D, tn), lambda j: (l, 0, j)),
                  pl.BlockSpec((None, 1, tn), lambda j: (l, 0, j))],
        out_specs=pl.BlockSpec((cond.shape[0], tn), lambda j: (0, j)),
        out_shape=jax.ShapeDtypeStruct((cond.shape[0], n), F32),
        compiler_params=_cparams(("arbitrary",)),
        name="mod",
    )(cond, w_mod, b_mod.reshape(n_l, 1, n))


def _inprep_kernel(latent, *refs):
    (x_ref, mod_ref, gpre_ref, win_ref, gmq_ref, wuq_ref, gmkv_ref, wukv_ref,
     ggq_ref, ggk_ref, bd_ref) = refs[:11]
    refs = refs[11:]
    if latent:
        cos64_ref, sin64_ref, cospe_ref, sinpe_ref = refs[:4]
        refs = refs[4:]
    (qm_ref, kvm_ref, gqo_ref, gkv_ref, dqo_ref, dkv_ref, ret_ref, rg_ref, gate_ref) = refs[:9]
    refs = refs[9:]
    if not latent:
        ckv_o, kpe_o, gk_o, gv_o, dk_o, dv_o = refs

    x = x_ref[...]
    mod = mod_ref[...]
    sh1 = mod[:, 0:D]
    sc1 = mod[:, D:2 * D]
    hb = (_rms(x, gpre_ref[...]) * (1.0 + sc1) + sh1).astype(BF16)

    def z(a, b):
        return _dot_nt(hb, win_ref[a:b, :])

    if latent:
        cos64, sin64 = cos64_ref[...], sin64_ref[...]
        cospe, sinpe = cospe_ref[...], sinpe_ref[...]

    cqn = _rms(z(O_CQ, O_CKV), gmq_ref[...]).astype(BF16)
    q = _dot(cqn, wuq_ref[...]) * ((MLA_NOPE + MLA_ROPE) ** -0.5)
    q_nope, q_pe = q[:, 0:512], q[:, 512:768]
    if latent:
        q_pe = _rope(q_pe, _tile_lanes(cospe, 2), _tile_lanes(sinpe, 2), MLA_ROPE // 4)
    qm_ref[:, 0:512] = q_nope.astype(BF16)
    qm_ref[:, 512:768] = q_pe.astype(BF16)

    ckvn = _rms(z(O_CKV, O_KPE), gmkv_ref[...])
    kv = _dot(ckvn.astype(BF16), wukv_ref[...])
    kpe4 = _dot_nt(hb, jnp.concatenate([win_ref[O_KPE:O_GQ, :]] * 4, axis=0))
    if latent:
        kpe4 = _rope(kpe4, cospe, sinpe, MLA_ROPE // 4)
    else:
        ckv_o[...] = ckvn
        kpe_o[...] = kpe4[:, 0:MLA_ROPE]
    kpe_b = kpe4.astype(BF16)
    for p in range(4):
        kvm_ref[:, p * 256:p * 256 + LANES] = kv[:, p * LANES:(p + 1) * LANES].astype(BF16)
        kvm_ref[:, p * 256 + LANES:(p + 1) * 256] = kpe_b
    kvm_ref[:, 1024:1536] = kv[:, 512:1024].astype(BF16)

    gq = _dot_nt(hb, jnp.concatenate(
        [win_ref[O_GQ + h * GQA_HD:O_GQ + (h + 1) * GQA_HD, :] for h in GQA_ORDER], axis=0))
    gq = gq * lax.rsqrt(_seg_meansq(gq, bd_ref, 512) + EPS) * ggq_ref[...]
    gk = z(O_GK, O_GV)
    gk = gk * lax.rsqrt(_seg_meansq(gk, bd_ref, LANES) + EPS) * ggk_ref[...]
    gv = z(O_GV, O_RQ)
    if latent:
        gq = _rope(gq, _tile_lanes(cos64, 4), _tile_lanes(sin64, 4), GQA_HD // 4)
        gk = _rope(gk, cos64, sin64, GQA_HD // 4)
    else:
        gk_o[...] = gk
        gv_o[...] = gv
    gqo_ref[...] = (gq * (GQA_HD ** -0.5)).astype(BF16)
    gkv_ref[:, 0:LANES] = gk.astype(BF16)
    gkv_ref[:, LANES:2 * LANES] = gv.astype(BF16)

    dq = z(O_DQ, O_DK)
    dk = z(O_DK, O_DV)
    dv = z(O_DV, O_GL)
    if latent:
        dq = _rope(dq, _tile_lanes(cos64, 4), _tile_lanes(sin64, 4), DIFF_D // 4)
        dk = _rope(dk, _tile_lanes(cos64, 4), _tile_lanes(sin64, 4), DIFF_D // 4)
    else:
        dk_o[...] = dk
        dv_o[...] = dv
    dqo_ref[...] = (dq * (DIFF_D ** -0.5)).astype(BF16)
    dkv_ref[:, 0:512] = dk.astype(BF16)
    dkv_ref[:, 512:1024] = dv.astype(BF16)

    ret_ref[:, 0:256] = z(O_RQ, O_RK).astype(BF16)
    ret_ref[:, 256:512] = (z(O_RK, O_RV) * (RET_DK ** -0.5)).astype(BF16)
    ret_ref[:, 512:1024] = z(O_RV, O_RG).astype(BF16)
    rg_ref[...] = z(O_RG, O_DQ).astype(BF16)

    for n in range(N_BRANCH):
        gate_ref[:, n * D:(n + 1) * D] = jax.nn.sigmoid(
            z(O_GL + n * D, O_GL + (n + 1) * D)).astype(BF16)


def _inprep_call(latent, l, x, mod3, mod_row, lw, tabs, t_len):
    n_tok = x.shape[0]
    nblk = n_tok // TM
    blk_per_seq = t_len // TM

    def tok(w):
        return pl.BlockSpec((TM, w), lambda i: (i, 0))

    in_specs = [tok(D),
                pl.BlockSpec((None, 1, 6 * D), lambda i: (mod_row(i), 0, 0)),
                _const_spec((1, D)),
                pl.BlockSpec((None, O_END, D), lambda i: (l, 0, 0), pipeline_mode=pl.Buffered(1)),
                _const_spec((1, MLA_Q_LORA)), _const_spec((MLA_Q_LORA, 768)),
                _const_spec((1, MLA_KV_LORA)), _const_spec((MLA_KV_LORA, 1024)),
                _const_spec((1, 512)), _const_spec((1, LANES)), _const_spec((512, 512))]
    args = [x, mod3, lw["g_pre1"], lw["w_in"], lw["g_mla_q"], lw["w_uq"], lw["g_mla_kv"],
            lw["w_ukv"], lw["g_gqa_q"], lw["g_gqa_k"], lw["bd"]]
    if latent:
        tab_spec = pl.BlockSpec((TM, LANES), lambda i: (i % blk_per_seq, 0))
        in_specs += [tab_spec] * 4
        args += list(tabs)
    widths = [768, KVM_W, 512, 256, 512, 1024, 1024, 512, 4 * D]
    out_specs = [tok(w) for w in widths]
    out_shape = [jax.ShapeDtypeStruct((n_tok, w), BF16) for w in widths]
    if not latent:
        cw = [MLA_KV_LORA, MLA_ROPE, 128, 128, 512, 512]
        out_specs += [tok(w) for w in cw]
        out_shape += [jax.ShapeDtypeStruct((n_tok, w), F32) for w in cw]
    return pl.pallas_call(
        functools.partial(_inprep_kernel, latent),
        grid=(nblk,),
        in_specs=in_specs, out_specs=out_specs, out_shape=out_shape,
        compiler_params=_cparams(("arbitrary",)),
        name="inprep_lat" if latent else "inprep_ctx",
    )(*args)


def _pastkv_kernel(ckv_ref, kpe_ref, wukv_ref, o_ref):
    kv = _dot(ckv_ref[...].astype(BF16), wukv_ref[...])
    kpe_b = kpe_ref[...].astype(BF16)
    for p in range(4):
        o_ref[:, p * 256:p * 256 + LANES] = kv[:, p * LANES:(p + 1) * LANES].astype(BF16)
        o_ref[:, p * 256 + LANES:(p + 1) * 256] = kpe_b
    o_ref[:, 1024:1536] = kv[:, 512:1024].astype(BF16)


def _pastkv_call(ckv, kpe4, w_ukv):
    n = ckv.shape[0]
    return pl.pallas_call(
        _pastkv_kernel,
        grid=(n // TM,),
        in_specs=[pl.BlockSpec((TM, MLA_KV_LORA), lambda i: (i, 0)),
                  pl.BlockSpec((TM, LANES), lambda i: (i, 0)),
                  _const_spec((MLA_KV_LORA, 1024))],
        out_specs=pl.BlockSpec((TM, KVM_W), lambda i: (i, 0)),
        out_shape=jax.ShapeDtypeStruct((n, KVM_W), BF16),
        compiler_params=_cparams(("arbitrary",)),
        name="pastkv",
    )(ckv, kpe4, w_ukv)


def _softmax_pv(s, v):
    m = jnp.max(s, axis=-1, keepdims=True)
    p = jnp.exp(s - m)
    l = jnp.sum(p, axis=-1, keepdims=True)
    return _dot(p.astype(BF16), v) / l


def _attn_kernel(lam_init, qm_ref, kvm_ref, gq_ref, gkv_ref, dq_ref, dkv_ref,
                 lam_ref, gdiff_ref, o_ref):
    tq = qm_ref.shape[0]
    lane = _lane_iota((tq, LANES))
    low = lane < HALF_LANES
    zero = jnp.zeros((tq, LANES), BF16)

    for p in range(MLA_HEADS // 2):
        qn = qm_ref[:, p * LANES:(p + 1) * LANES]
        g = p // 2
        qpe = qm_ref[:, 512 + g * LANES:512 + (g + 1) * LANES]
        kk = kvm_ref[:, p * 256:(p + 1) * 256]
        vv = kvm_ref[:, 1024 + p * LANES:1024 + (p + 1) * LANES]
        outs = []
        for half in range(2):
            h = 2 * p + half
            slot = h % 4
            in_slot = (lane >= slot * MLA_ROPE) & (lane < (slot + 1) * MLA_ROPE)
            lhs = jnp.concatenate(
                [jnp.where(low if half == 0 else ~low, qn, zero),
                 jnp.where(in_slot, qpe, zero)], axis=1)
            outs.append(_softmax_pv(_dot_nt(lhs, kk), vv))
        o_ref[:, p * LANES:(p + 1) * LANES] = jnp.where(low, outs[0], outs[1]).astype(BF16)

    kk = gkv_ref[:, 0:LANES]
    vv = gkv_ref[:, LANES:2 * LANES]
    for g in range(GQA_HEADS // 2):
        qg = gq_ref[:, g * LANES:(g + 1) * LANES]
        o_lo = _softmax_pv(_dot_nt(jnp.where(low, qg, zero), kk), vv)
        o_hi = _softmax_pv(_dot_nt(jnp.where(low, zero, qg), kk), vv)
        o_ref[:, 512 + g * LANES:512 + (g + 1) * LANES] = jnp.where(low, o_lo, o_hi).astype(BF16)

    lp = lam_ref[...]
    lam = (jnp.exp(jnp.sum(lp[0:1] * lp[1:2], axis=-1, keepdims=True))
           - jnp.exp(jnp.sum(lp[2:3] * lp[3:4], axis=-1, keepdims=True)) + lam_init)
    for h in range(DIFF_HEADS):
        qh = dq_ref[:, h * LANES:(h + 1) * LANES]
        kk = dkv_ref[:, h * LANES:(h + 1) * LANES]
        vv = dkv_ref[:, 512 + h * LANES:512 + (h + 1) * LANES]
        a1 = _softmax_pv(_dot_nt(jnp.where(low, qh, zero), kk), vv)
        a2 = _softmax_pv(_dot_nt(jnp.where(low, zero, qh), kk), vv)
        od = _rms(a1 - lam * a2, gdiff_ref[...]) * (1.0 - lam_init)
        o_ref[:, 1024 + h * LANES:1024 + (h + 1) * LANES] = od.astype(BF16)


def _attn_call(lam_init, qm, kvm, gq, gkv, dq, dkv, lam_p, g_diff, n_b, t_len, s_len):
    nq = t_len // TQ

    def qspec(w):
        return pl.BlockSpec((TQ, w), lambda b, i: (b * nq + i, 0))

    def kspec(w):
        return pl.BlockSpec((s_len, w), lambda b, i: (b, 0))

    return pl.pallas_call(
        functools.partial(_attn_kernel, lam_init),
        grid=(n_b, nq),
        in_specs=[qspec(768), kspec(KVM_W), qspec(512), kspec(256), qspec(512), kspec(1024),
                  _const_spec((4, DIFF_D)), _const_spec((1, DIFF_DV))],
        out_specs=qspec(3 * BRANCH_W),
        out_shape=jax.ShapeDtypeStruct((n_b * t_len, 3 * BRANCH_W), BF16),
        compiler_params=_cparams(("arbitrary", "arbitrary")),
        name="attn",
    )(qm, kvm, gq, gkv, dq, dkv, lam_p, g_diff)


def _log_sigmoid(x):
    return jnp.minimum(x, 0.0) - jnp.log(1.0 + jnp.exp(-jnp.abs(x)))


def _ret_kernel(latent, t_len, dec_ref, q_ref, k_ref, v_ref, rg_ref, gret_ref, *refs):
    if latent:
        s0_ref, o_ref = refs
    else:
        o_ref, st_ref = refs
    tq = q_ref.shape[0]
    t0 = pl.program_id(1) * tq
    lane = _lane_iota((tq, LANES))
    low = lane < HALF_LANES
    zero = jnp.zeros((tq, LANES), BF16)
    t_idx = (t0 + lax.broadcasted_iota(jnp.int32, (tq, t_len), 0)).astype(F32)
    s_idx = lax.broadcasted_iota(jnp.int32, (tq, t_len), 1).astype(F32)
    dist = t_idx - s_idx
    t_col = (t0 + lax.broadcasted_iota(jnp.int32, (tq, 1), 0)).astype(F32)

    def lg(d, h):
        return _log_sigmoid(jnp.full((1, 1), dec_ref[d, h], F32))

    for h in range(RET_HEADS):
        p, half = h // 2, h % 2
        qp = q_ref[:, p * LANES:(p + 1) * LANES]
        qm = jnp.where(low if half == 0 else ~low, qp, zero)
        kp = k_ref[:, p * LANES:(p + 1) * LANES]
        vh = v_ref[:, h * LANES:(h + 1) * LANES]
        lgf, lgb = lg(0, h), lg(1, h)
        dmask = (jnp.where(dist >= 0, jnp.exp(lgf * jnp.maximum(dist, 0.0)), 0.0)
                 + jnp.where(dist <= 0, jnp.exp(lgb * jnp.maximum(-dist, 0.0)), 0.0))
        o = _dot((_dot_nt(qm, kp) * dmask).astype(BF16), vh)
        if latent:
            sf = s0_ref[0, p].astype(BF16)
            sb = s0_ref[1, p].astype(BF16)
            o = o + _dot(qm, sf) * jnp.exp(lgf * (t_col + 1.0))
            o = o + _dot(qm, sb) * jnp.exp(lgb * (float(t_len) - t_col))
        mu = jnp.mean(o, axis=-1, keepdims=True)
        oc = o - mu
        y = oc * lax.rsqrt(jnp.mean(oc * oc, axis=-1, keepdims=True) + EPS)
        y = y * gret_ref[:, h * LANES:(h + 1) * LANES]
        rg = rg_ref[:, h * LANES:(h + 1) * LANES].astype(F32)
        o_ref[:, h * LANES:(h + 1) * LANES] = (y * _silu(rg)).astype(BF16)

    if not latent:
        s_col = lax.broadcasted_iota(jnp.int32, (t_len, 1), 0).astype(F32)
        lane_t = _lane_iota((1, LANES)) < HALF_LANES
        for p in range(RET_HEADS // 2):
            kp = k_ref[:, p * LANES:(p + 1) * LANES].astype(F32)
            for d in range(2):
                lg_lane = jnp.where(lane_t, lg(d, 2 * p), lg(d, 2 * p + 1))
                expo = (float(t_len) - 1.0 - s_col) if d == 0 else s_col
                kdec_t = jnp.transpose(kp * jnp.exp(lg_lane * expo)).astype(BF16)
                for half in range(2):
                    h = 2 * p + half
                    st = _dot(kdec_t, v_ref[:, h * LANES:(h + 1) * LANES])
                    st_ref[d, h] = st[half * RET_DK:(half + 1) * RET_DK, :]


def _ret_call(latent, dec, ret, rg, g_ret, s0, n_b, t_len):
    nq = t_len // TQ
    assert latent or nq == 1
    in_specs = [pl.BlockSpec(memory_space=pltpu.SMEM),
                pl.BlockSpec((TQ, 256), lambda b, i: (b * nq + i, 0)),
                pl.BlockSpec((t_len, 256), lambda b, i: (b, 1)),
                pl.BlockSpec((t_len, 512), lambda b, i: (b, 1)),
                pl.BlockSpec((TQ, 512), lambda b, i: (b * nq + i, 0)),
                _const_spec((1, 512))]
    args = [dec, ret, ret, ret, rg, g_ret]
    out_specs = [pl.BlockSpec((TQ, 512), lambda b, i: (b * nq + i, 0))]
    out_shape = [jax.ShapeDtypeStruct((n_b * t_len, 512), BF16)]
    if latent:
        in_specs.append(pl.BlockSpec((None, 2, 2, LANES, LANES), lambda b, i: (b, 0, 0, 0, 0)))
        args.append(s0)
    else:
        out_specs.append(pl.BlockSpec((None, 2, RET_HEADS, RET_DK, RET_DV),
                                      lambda b, i: (b, 0, 0, 0, 0)))
        out_shape.append(jax.ShapeDtypeStruct((n_b, 2, RET_HEADS, RET_DK, RET_DV), F32))
    return pl.pallas_call(
        functools.partial(_ret_kernel, latent, t_len),
        grid=(n_b, nq),
        in_specs=in_specs, out_specs=out_specs, out_shape=out_shape,
        compiler_params=_cparams(("arbitrary", "arbitrary")),
        name="ret_lat" if latent else "ret_ctx",
    )(*args)


def _merge_kernel(x_ref, mod_ref, br_ref, or_ref, gate_ref, wbr_ref, wout_ref, gpost_ref, o_ref):
    merged = None
    for n in range(N_BRANCH):
        if n < 2:
            b = br_ref[:, n * BRANCH_W:(n + 1) * BRANCH_W]
        elif n == 2:
            b = or_ref[...]
        else:
            b = br_ref[:, 2 * BRANCH_W:3 * BRANCH_W]
        t = gate_ref[:, n * D:(n + 1) * D].astype(F32) * _dot(b, wbr_ref[n])
        merged = t if merged is None else merged + t
    out = _dot(merged.astype(BF16), wout_ref[...])
    g1 = mod_ref[...][:, 2 * D:3 * D]
    o_ref[...] = x_ref[...] + g1 * _rms(out, gpost_ref[...])


def _merge_call(x, mod3, mod_row, br, o_r, gates, lw):
    n_tok = x.shape[0]

    def tok(w):
        return pl.BlockSpec((TM, w), lambda i: (i, 0))

    return pl.pallas_call(
        _merge_kernel,
        grid=(n_tok // TM,),
        in_specs=[tok(D), pl.BlockSpec((None, 1, 6 * D), lambda i: (mod_row(i), 0, 0)),
                  tok(3 * BRANCH_W), tok(BRANCH_W), tok(4 * D),
                  _const_spec((N_BRANCH, BRANCH_W, D)), _const_spec((D, D)), _const_spec((1, D))],
        out_specs=tok(D),
        out_shape=jax.ShapeDtypeStruct((n_tok, D), F32),
        compiler_params=_cparams(("arbitrary",)),
        name="merge",
    )(x, mod3, br, o_r, gates, lw["w_br"], lw["w_out"], lw["g_post1"])


def _route(logits_t, bias):
    n = logits_t.shape[1]
    scores = jax.nn.sigmoid(logits_t)
    sel = scores + bias
    neg = -jnp.inf
    sub = lax.broadcasted_iota(jnp.int32, (GROUP_SIZE, n), 0)
    grp = []
    for g in range(N_GROUPS):
        blk = sel[g * GROUP_SIZE:(g + 1) * GROUP_SIZE]
        m1 = jnp.max(blk, axis=0, keepdims=True)
        i1 = jnp.min(jnp.where(blk == m1, sub, GROUP_SIZE), axis=0, keepdims=True)
        m2 = jnp.max(jnp.where(sub == i1, neg, blk), axis=0, keepdims=True)
        grp.append(m1 + m2)
    parts = []
    for g in range(N_GROUPS):
        beaten = jnp.zeros((1, n), jnp.int32)
        for o in range(N_GROUPS):
            if o == g:
                continue
            wins = (grp[o] > grp[g]) | (grp[o] == grp[g]) if o < g else (grp[o] > grp[g])
            beaten = beaten + wins.astype(jnp.int32)
        keep = beaten < TOPK_GROUPS
        parts.append(jnp.where(keep, sel[g * GROUP_SIZE:(g + 1) * GROUP_SIZE], neg))
    cur = jnp.concatenate(parts, axis=0)
    eidx = lax.broadcasted_iota(jnp.int32, (N_EXPERTS, n), 0)
    hits, ids, ws = [], [], []
    for _ in range(TOP_K):
        m = jnp.max(cur, axis=0, keepdims=True)
        i = jnp.min(jnp.where(cur == m, eidx, N_EXPERTS), axis=0, keepdims=True)
        hit = eidx == i
        hits.append(hit)
        ids.append(i)
        ws.append(jnp.sum(jnp.where(hit, scores, 0.0), axis=0, keepdims=True))
        cur = jnp.where(hit, neg, cur)
    wsum = ws[0] + ws[1] + ws[2] + ws[3]
    return hits, ids, [w / wsum * ROUTE_SCALE for w in ws]


U32 = jnp.uint32
HIGH16 = np.uint32(0xFFFF0000)


def _bf16_bits(v):
    return lax.bitcast_convert_type(v.astype(BF16).astype(F32), U32)


def _pack_rows(v):
    return (_bf16_bits(v[:, 0:D // 2]) >> 16) | _bf16_bits(v[:, D // 2:D])


def _unpack_rows(p):
    lo = lax.bitcast_convert_type(p << 16, F32)
    hi = lax.bitcast_convert_type(p & HIGH16, F32)
    return jnp.concatenate([lo, hi], axis=1)


def _moe_pre_kernel(x_ref, mod_ref, gpre_ref, wr_ref, br_ref, tri_ref,
                    hp_ref, eidx_ref, rank_ref, comb_ref, cnt_ref, run_ref):
    tm = x_ref.shape[0]

    @pl.when(pl.program_id(0) == 0)
    def _():
        run_ref[...] = jnp.zeros_like(run_ref)

    mod = mod_ref[...]
    sh2, sc2 = mod[:, 3 * D:4 * D], mod[:, 4 * D:5 * D]
    h = _rms(x_ref[...], gpre_ref[...]) * (1.0 + sc2) + sh2
    hp_ref[...] = _pack_rows(h)
    hb = h.astype(BF16)
    h_lo = (h - hb.astype(F32)).astype(BF16)
    wr = wr_ref[...]
    wr_hi = wr.astype(BF16)
    wr_lo = (wr - wr_hi.astype(F32)).astype(BF16)
    logits_t = _dot_nt(wr_hi, hb) + _dot_nt(wr_hi, h_lo) + _dot_nt(wr_lo, hb)
    hits, ids, ws = _route(logits_t, br_ref[...])

    picked = jnp.zeros((N_EXPERTS, tm), F32)
    for hit in hits:
        picked = jnp.where(hit, 1.0, picked)
    before = _dot(picked.astype(BF16), tri_ref[...]) + run_ref[:, 0:1]
    sub8 = lax.broadcasted_iota(jnp.int32, (8, tm), 0)
    comb8 = jnp.zeros((8, tm), F32)
    for k in range(TOP_K):
        rank = jnp.sum(jnp.where(hits[k], before, 0.0), axis=0, keepdims=True)
        eidx_ref[k:k + 1, :] = ids[k]
        rank_ref[k:k + 1, :] = rank.astype(jnp.int32)
        comb8 = jnp.where(sub8 == k, ws[k], comb8)
    comb_ref[...] = jnp.transpose(
        jnp.concatenate([comb8, jnp.zeros((LANES - 8, tm), F32)], axis=0))
    run_ref[...] = run_ref[...] + jnp.sum(picked, axis=1, keepdims=True)
    cnt_ref[...] = run_ref[...]


def _moe_pre_call(x, mod3, mod_row, lw):
    n_tok = x.shape[0]
    tm = TM_MOE_PRE
    tri = np.arange(tm)
    tri = jnp.asarray(tri[:, None] < tri[None, :], BF16)
    row4 = pl.BlockSpec((TOP_K, tm), lambda i: (0, i))
    return pl.pallas_call(
        _moe_pre_kernel,
        grid=(n_tok // tm,),
        in_specs=[pl.BlockSpec((tm, D), lambda i: (i, 0)),
                  pl.BlockSpec((None, 1, 6 * D), lambda i: (mod_row(i), 0, 0)),
                  _const_spec((1, D)), _const_spec((N_EXPERTS, D)), _const_spec((N_EXPERTS, 1)),
                  _const_spec((tm, tm))],
        out_specs=[pl.BlockSpec((tm, D // 2), lambda i: (i, 0)), row4, row4,
                   pl.BlockSpec((tm, LANES), lambda i: (i, 0)),
                   _const_spec((N_EXPERTS, LANES))],
        out_shape=[jax.ShapeDtypeStruct((n_tok, D // 2), U32),
                   jax.ShapeDtypeStruct((TOP_K, n_tok), jnp.int32),
                   jax.ShapeDtypeStruct((TOP_K, n_tok), jnp.int32),
                   jax.ShapeDtypeStruct((n_tok, LANES), F32),
                   jax.ShapeDtypeStruct((N_EXPERTS, LANES), F32)],
        scratch_shapes=[pltpu.VMEM((N_EXPERTS, LANES), F32)],
        compiler_params=_cparams(("arbitrary",)),
        name="moe_pre",
    )(x, mod3, lw["g_pre2"], lw["w_router_t"], lw["b_router"], tri)


def _moe_plan_kernel(eidx_ref, rank_ref, cnt_ref, dest_ref, te_ref, tv_ref):
    tm = eidx_ref.shape[1]
    cnt = cnt_ref[...]
    padded = jnp.ceil(cnt * (1.0 / TMX)) * TMX
    row = lax.broadcasted_iota(jnp.int32, cnt.shape, 0)
    incl = padded
    shift = 1
    while shift < N_EXPERTS:
        incl = incl + jnp.where(row >= shift, pltpu.roll(incl, shift, 0), 0.0)
        shift *= 2
    start = (incl - padded)[:, 0:1]
    end = incl[:, 0:1]
    erow = lax.broadcasted_iota(jnp.int32, (N_EXPERTS, tm), 0)
    for k in range(TOP_K):
        mine = erow == eidx_ref[k:k + 1, :]
        base = jnp.sum(jnp.where(mine, start, 0.0), axis=0, keepdims=True)
        dest_ref[k:k + 1, :] = rank_ref[k:k + 1, :] + base.astype(jnp.int32)

    @pl.when(pl.program_id(0) == 0)
    def _():
        tile0 = (_lane_iota((1, LANES)) * TMX).astype(F32)
        owner = jnp.sum(jnp.where(end <= tile0, 1.0, 0.0), axis=0, keepdims=True)
        owner = jnp.minimum(owner, N_EXPERTS - 1.0)
        erow_t = lax.broadcasted_iota(jnp.int32, (N_EXPERTS, LANES), 0).astype(F32)
        left = jnp.sum(jnp.where(erow_t == owner, cnt[:, 0:1] - (tile0 - start), 0.0),
                       axis=0, keepdims=True)
        te_ref[...] = owner.astype(jnp.int32)
        tv_ref[...] = jnp.clip(left, 0.0, float(TMX)).astype(jnp.int32)


def _moe_plan_call(eidx, rank, cnt):
    n_tok = eidx.shape[1]
    tm = TM_MOE_PRE
    row4 = pl.BlockSpec((TOP_K, tm), lambda i: (0, i))
    return pl.pallas_call(
        _moe_plan_kernel,
        grid=(n_tok // tm,),
        in_specs=[row4, row4, _const_spec((N_EXPERTS, LANES))],
        out_specs=[row4, _const_spec((1, LANES)), _const_spec((1, LANES))],
        out_shape=[jax.ShapeDtypeStruct((TOP_K, n_tok), jnp.int32),
                   jax.ShapeDtypeStruct((1, LANES), jnp.int32),
                   jax.ShapeDtypeStruct((1, LANES), jnp.int32)],
        compiler_params=_cparams(("arbitrary",)),
        name="moe_plan",
    )(eidx, rank, cnt)


GU_CHUNKS, DN_CHUNKS = 4, 2
GU_ROWS, DN_ROWS = D // GU_CHUNKS, EXPERT_FF // DN_CHUNKS


def _experts_kernel(te_ref, tv_ref, xs_ref, *refs):
    wgu_refs = refs[:GU_CHUNKS]
    wdn_refs = refs[GU_CHUNKS:GU_CHUNKS + DN_CHUNKS]
    ys_ref, wgu_b, wdn_b = refs[GU_CHUNKS + DN_CHUNKS:]
    j = pl.program_id(0)
    valid = tv_ref[j]
    prev = te_ref[jnp.maximum(j - 1, 0)]

    @pl.when((j == 0) | (te_ref[j] != prev))
    def _():
        for c, ref in enumerate(wgu_refs):
            wgu_b[c * GU_ROWS:(c + 1) * GU_ROWS, :] = ref[...].astype(BF16)
        for c, ref in enumerate(wdn_refs):
            wdn_b[c * DN_ROWS:(c + 1) * DN_ROWS, :] = ref[...].astype(BF16)

    @pl.when(valid > 0)
    def _():
        rows = lax.broadcasted_iota(jnp.int32, (TMX, D), 0)
        x = jnp.where(rows < valid, _unpack_rows(xs_ref[...]), 0.0).astype(BF16)
        gu = _dot(x, wgu_b[...])
        a = _silu(gu[:, 0:EXPERT_FF]) * gu[:, EXPERT_FF:2 * EXPERT_FF]
        ys_ref[...] = _pack_rows(_dot(a.astype(BF16), wdn_b[...]))

    @pl.when(valid <= 0)
    def _():
        ys_ref[...] = jnp.zeros_like(ys_ref)


def _experts_call(l, xs, te, tv, w_gu, w_dn):
    n_tiles = xs.shape[0] // TMX

    def wspec(rows, cols, c):
        return pl.BlockSpec((None, None, rows, cols), lambda j, te, tv: (l, te[j], c, 0))

    grid_spec = pltpu.PrefetchScalarGridSpec(
        num_scalar_prefetch=2,
        grid=(n_tiles,),
        in_specs=([pl.BlockSpec((TMX, D // 2), lambda j, te, tv: (j, 0))]
                  + [wspec(GU_ROWS, 2 * EXPERT_FF, c) for c in range(GU_CHUNKS)]
                  + [wspec(DN_ROWS, D, c) for c in range(DN_CHUNKS)]),
        out_specs=pl.BlockSpec((TMX, D // 2), lambda j, te, tv: (j, 0)),
        scratch_shapes=[pltpu.VMEM((D, 2 * EXPERT_FF), BF16), pltpu.VMEM((EXPERT_FF, D), BF16)])
    return pl.pallas_call(
        _experts_kernel,
        grid_spec=grid_spec,
        out_shape=jax.ShapeDtypeStruct(xs.shape, U32),
        compiler_params=_cparams(("arbitrary",)),
        name="moe_experts",
    )(te, tv, xs, *([w_gu] * GU_CHUNKS), *([w_dn] * DN_CHUNKS))


def _moe_post_kernel(x_ref, mod_ref, hp_ref, yg_ref, comb_ref, wsgu_ref, wsdn_ref, gpost_ref,
                     o_ref):
    hb = _unpack_rows(hp_ref[...]).astype(BF16)
    sgu = _dot(hb, wsgu_ref[...])
    sa = _silu(sgu[:, 0:SHARED_FF]) * sgu[:, SHARED_FF:2 * SHARED_FF]
    acc = _dot(sa.astype(BF16), wsdn_ref[...])
    comb = comb_ref[...]
    for k in range(TOP_K):
        acc = acc + comb[:, k:k + 1] * _unpack_rows(yg_ref[k])
    g2 = mod_ref[...][:, 5 * D:6 * D]
    o_ref[...] = x_ref[...] + g2 * _rms(acc, gpost_ref[...])


def _moe_post_call(x, mod3, mod_row, hp, yg, comb, lw):
    n_tok = x.shape[0]
    tm = TM_MOE_PRE
    return pl.pallas_call(
        _moe_post_kernel,
        grid=(n_tok // tm,),
        in_specs=[pl.BlockSpec((tm, D), lambda i: (i, 0)),
                  pl.BlockSpec((None, 1, 6 * D), lambda i: (mod_row(i), 0, 0)),
                  pl.BlockSpec((tm, D // 2), lambda i: (i, 0)),
                  pl.BlockSpec((TOP_K, tm, D // 2), lambda i: (0, i, 0)),
                  pl.BlockSpec((tm, LANES), lambda i: (i, 0)),
                  _const_spec((D, 2 * SHARED_FF)), _const_spec((SHARED_FF, D)),
                  _const_spec((1, D))],
        out_specs=pl.BlockSpec((tm, D), lambda i: (i, 0)),
        out_shape=jax.ShapeDtypeStruct((n_tok, D), F32),
        compiler_params=_cparams(("arbitrary",)),
        name="moe_post",
    )(x, mod3, hp, yg, comb, lw["w_sh_gu"], lw["w_sh_down"], lw["g_post2"])


def _moe_call(l, x, mod3, mod_row, lw):
    n_tok = x.shape[0]
    n_slots = -(-(TOP_K * n_tok + N_EXPERTS * (TMX - 1)) // TMX) * TMX
    assert n_slots // TMX <= LANES
    hp, eidx, rank, comb, cnt = _moe_pre_call(x, mod3, mod_row, lw)
    dest, te, tv = _moe_plan_call(eidx, rank, cnt)
    dest = dest.reshape(TOP_K * n_tok)
    xs = _sc_scatter_rows(hp, dest, n_slots)
    ys = _experts_call(l, xs, te[0, :n_slots // TMX], tv[0, :n_slots // TMX],
                       lw["w_exp_gu"], lw["w_exp_down"])
    yg = _sc_gather_rows(ys, dest).reshape(TOP_K, n_tok, D // 2)
    return _moe_post_call(x, mod3, mod_row, hp, yg, comb, lw)


SC_CORES, SC_SUBCORES = 2, 16
SC_WORKERS = SC_CORES * SC_SUBCORES


def _sc_gather_rows(table, idx, chunk=64):
    n_out, width = idx.shape[0], table.shape[1]
    per_worker = n_out // SC_WORKERS
    n_chunks = per_worker // chunk
    assert per_worker * SC_WORKERS == n_out and n_chunks * chunk == per_worker
    mesh = plsc.VectorSubcoreMesh(core_axis_name="c", subcore_axis_name="s",
                                  num_cores=SC_CORES, num_subcores=SC_SUBCORES)

    @functools.partial(
        pl.kernel, mesh=mesh,
        out_type=jax.ShapeDtypeStruct((n_out, width), table.dtype),
        scratch_types=[pltpu.VMEM((chunk,), jnp.int32), pltpu.VMEM((chunk, width), table.dtype),
                       pltpu.SemaphoreType.DMA],
        name="sc_gather")
    def gather(table_hbm, idx_hbm, out_hbm, idx_v, rows_v, sem):
        base = (lax.axis_index("s") * SC_CORES + lax.axis_index("c")) * per_worker

        @pl.loop(0, n_chunks)
        def _(j):
            off = base + j * chunk
            pltpu.sync_copy(idx_hbm.at[pl.ds(off, chunk)], idx_v)
            pltpu.async_copy(table_hbm.at[idx_v], rows_v, sem).wait()
            pltpu.sync_copy(rows_v, out_hbm.at[pl.ds(off, chunk)])

    return gather(table, idx)


def _sc_scatter_rows(rows, dest, n_slots, chunk=64):
    n_tok, width = rows.shape
    per_worker = n_tok // SC_WORKERS
    n_chunks = per_worker // chunk
    assert per_worker * SC_WORKERS == n_tok and n_chunks * chunk == per_worker
    mesh = plsc.VectorSubcoreMesh(core_axis_name="c", subcore_axis_name="s",
                                  num_cores=SC_CORES, num_subcores=SC_SUBCORES)

    @functools.partial(
        pl.kernel, mesh=mesh,
        out_type=jax.ShapeDtypeStruct((n_slots, width), rows.dtype),
        scratch_types=[pltpu.VMEM((chunk,), jnp.int32), pltpu.VMEM((chunk, width), rows.dtype)],
        name="sc_scatter")
    def scatter(rows_hbm, dest_hbm, out_hbm, idx_v, rows_v):
        base = (lax.axis_index("s") * SC_CORES + lax.axis_index("c")) * per_worker

        @pl.loop(0, n_chunks)
        def _(j):
            off = base + j * chunk
            pltpu.sync_copy(rows_hbm.at[pl.ds(off, chunk)], rows_v)
            for k in range(TOP_K):
                pltpu.sync_copy(dest_hbm.at[pl.ds(k * n_tok + off, chunk)], idx_v)
                pltpu.sync_copy(rows_v, out_hbm.at[idx_v])

    return scatter(rows, dest)


def _rope_tables(t_len):
    pos = np.arange(t_len)
    row, col = pos // GRID_W, pos % GRID_W

    def tab(r):
        half = r // 2
        freq = ROPE_BASE ** (-np.arange(half, dtype=np.float64) / half)
        sign = np.concatenate([-np.ones(half), np.ones(half)])
        cs, sn = [], []
        for p in (row, col):
            ang = p[:, None].astype(np.float64) * freq[None, :]
            cs.append(np.concatenate([np.cos(ang), np.cos(ang)], axis=1))
            sn.append(np.concatenate([np.sin(ang), np.sin(ang)], axis=1) * sign[None, :])
        return np.concatenate(cs, axis=1), np.concatenate(sn, axis=1)

    c64, s64 = tab(GQA_HD // 2)
    cpe, spe = tab(MLA_ROPE // 2)
    out = (np.tile(c64, (1, 2)), np.tile(s64, (1, 2)), np.tile(cpe, (1, 4)), np.tile(spe, (1, 4)))
    return tuple(jnp.asarray(a, F32) for a in out)


def _layer_weights(l, p):
    w_uq = p["w_mla_uq"][l].reshape(MLA_Q_LORA, MLA_HEADS, MLA_NOPE + MLA_ROPE)
    w_uq = jnp.concatenate([w_uq[:, :, :MLA_NOPE].reshape(MLA_Q_LORA, -1),
                            w_uq[:, :, MLA_NOPE:].reshape(MLA_Q_LORA, -1)], axis=1)
    w_ukv = p["w_mla_ukv"][l].reshape(MLA_KV_LORA, MLA_HEADS, MLA_NOPE + MLA_V)
    w_ukv = jnp.concatenate([w_ukv[:, :, :MLA_NOPE].reshape(MLA_KV_LORA, -1),
                             w_ukv[:, :, MLA_NOPE:].reshape(MLA_KV_LORA, -1)], axis=1)
    w_br = p["w_br"][l]
    w_br_gqa = w_br[1].reshape(GQA_HEADS, GQA_HD, D)[jnp.array(GQA_ORDER)].reshape(BRANCH_W, D)
    w_br = jnp.stack([w_br[0], w_br_gqa, w_br[2], w_br[3]], axis=0)
    blk = np.arange(512) // GQA_HD
    return {
        "g_pre1": p["g_pre1"][l].reshape(1, D), "g_post1": p["g_post1"][l].reshape(1, D),
        "g_pre2": p["g_pre2"][l].reshape(1, D), "g_post2": p["g_post2"][l].reshape(1, D),
        "w_in": p["w_in_packed"],
        "g_mla_q": p["g_mla_q"][l].reshape(1, -1), "w_uq": w_uq.astype(BF16),
        "g_mla_kv": p["g_mla_kv"][l].reshape(1, -1), "w_ukv": w_ukv.astype(BF16),
        "g_gqa_q": jnp.tile(p["g_gqa_q"][l], GQA_HEADS).reshape(1, -1),
        "g_gqa_k": jnp.tile(p["g_gqa_k"][l], GQA_KV_HEADS).reshape(1, -1),
        "bd": jnp.asarray(blk[:, None] == blk[None, :], BF16),
        "ret_decay": p["ret_decay"][l],
        "g_ret": p["g_ret"][l].reshape(1, -1),
        "diff_lambda": p["diff_lambda"][l], "g_diff": p["g_diff"][l].reshape(1, -1),
        "w_br": w_br.astype(BF16), "w_out": p["w_out"][l].astype(BF16),
        "w_router_t": p["w_router"][l].T, "b_router": p["b_router"][l].reshape(-1, 1),
        "w_exp_gu": p["w_exp_gu"], "w_exp_down": p["w_exp_down"],
        "w_sh_gu": p["w_sh_gu"][l].astype(BF16), "w_sh_down": p["w_sh_down"][l].astype(BF16),
    }


def _mixers(latent, l, x, mod3, mod_row, lw, n_b, t_len, tabs=None, past=None, s0=None):
    lam_init = 0.8 - 0.6 * math.exp(-0.3 * l)
    outs = _inprep_call(latent, l, x, mod3, mod_row, lw, tabs, t_len)
    qm, kvm, gq, gkv, dq, dkv, ret, rg, gates = outs[:9]
    s_len = t_len
    if latent:
        past_kvm, past_gkv, past_dkv = past
        p_len = past_gkv.shape[1]
        s_len = p_len + t_len

        def cat(a, b):
            return jnp.concatenate([a, b.reshape(n_b, t_len, -1)], axis=1).reshape(n_b * s_len, -1)

        kvm, gkv, dkv = cat(past_kvm, kvm), cat(past_gkv, gkv), cat(past_dkv, dkv)
    br = _attn_call(lam_init, qm, kvm, gq, gkv, dq, dkv, lw["diff_lambda"], lw["g_diff"],
                    n_b, t_len, s_len)
    r = _ret_call(latent, lw["ret_decay"], ret, rg, lw["g_ret"], s0, n_b, t_len)
    y = _merge_call(x, mod3, mod_row, br, r[0], gates, lw)
    cache = None if latent else tuple(outs[9:]) + (r[1],)
    return y, cache


def kernel(x_prompt, x_sample, cache_mla_ckv, cache_mla_kpe, cache_gqa_k, cache_gqa_v, cache_diff_k, cache_diff_v, state_ret, c, c_ctx, w_mod, b_mod, g_pre1, g_post1, g_pre2, g_post2, w_in, g_mla_q, w_mla_uq, g_mla_kv, w_mla_ukv, g_gqa_q, g_gqa_k, ret_decay, g_ret, diff_lambda, g_diff, w_br, w_out, w_router, b_router, w_exp_gu, w_exp_down, w_sh_gu, w_sh_down):
    params = dict(w_in_packed=jnp.swapaxes(w_in, 1, 2).astype(BF16), g_pre1=g_pre1, g_post1=g_post1, g_pre2=g_pre2,
                  g_post2=g_post2, g_mla_q=g_mla_q, w_mla_uq=w_mla_uq,
                  g_mla_kv=g_mla_kv, w_mla_ukv=w_mla_ukv, g_gqa_q=g_gqa_q, g_gqa_k=g_gqa_k,
                  ret_decay=ret_decay, g_ret=g_ret, diff_lambda=diff_lambda, g_diff=g_diff,
                  w_br=w_br, w_out=w_out, w_router=w_router, b_router=b_router,
                  w_exp_gu=w_exp_gu, w_exp_down=w_exp_down, w_sh_gu=w_sh_gu, w_sh_down=w_sh_down)
    n_bc, t_c, _ = x_prompt.shape
    n_bl, t_l, _ = x_sample.shape
    p_len = cache_mla_ckv.shape[2]
    tabs = _rope_tables(t_l)
    n_cond = 8
    cond = jnp.concatenate([c_ctx[None, :], c, jnp.zeros((n_cond - 1 - n_bl, D), F32)], axis=0)
    blk_c, blk_l = t_c // TM, t_l // TM
    assert t_l % TM_MOE_PRE == 0 and (t_c * n_bc) % TM_MOE_PRE == 0

    yp = x_prompt.reshape(n_bc * t_c, D)
    ys = x_sample.reshape(n_bl * t_l, D)
    caches = []
    for l in range(DEPTH):
        lw = _layer_weights(l, params)
        mod3 = _mod_call(l, cond, w_mod, b_mod).reshape(n_cond, 1, 6 * D)
        yp, cache = _mixers(False, l, yp, mod3, lambda i: 0, lw, n_bc, t_c)
        yp = _moe_call(l, yp, mod3, lambda i: 0, lw)
        caches.append(cache)
        past_kvm = _pastkv_call(cache_mla_ckv[:, l].reshape(n_bl * p_len, -1),
                                jnp.tile(cache_mla_kpe[:, l].reshape(n_bl * p_len, -1), (1, 4)),
                                lw["w_ukv"]).reshape(n_bl, p_len, -1)
        past_gkv = jnp.concatenate([cache_gqa_k[:, l].reshape(n_bl, p_len, -1),
                                    cache_gqa_v[:, l].reshape(n_bl, p_len, -1)], axis=-1).astype(BF16)
        past_dkv = jnp.concatenate([cache_diff_k[:, l].reshape(n_bl, p_len, -1),
                                    cache_diff_v[:, l].reshape(n_bl, p_len, -1)], axis=-1).astype(BF16)
        s0 = state_ret[:, l].reshape(n_bl, 2, RET_HEADS // 2, 2 * RET_DK, RET_DV)
        ys, _ = _mixers(True, l, ys, mod3, lambda i: 1 + i // blk_l, lw, n_bl, t_l, tabs=tabs,
                        past=(past_kvm, past_gkv, past_dkv), s0=s0)
        ys = _moe_call(l, ys, mod3, lambda i: 1 + i // (t_l // TM_MOE_PRE), lw)

    def stack(k, shape):
        return jnp.stack([caches[l][k].reshape((n_bc, t_c) + shape) for l in range(DEPTH)], axis=1)

    new_ret = jnp.stack([caches[l][6] for l in range(DEPTH)], axis=1)
    return (yp.reshape(n_bc, t_c, D), ys.reshape(n_bl, t_l, D),
            stack(0, (MLA_KV_LORA,)), stack(1, (MLA_ROPE,)),
            stack(2, (GQA_KV_HEADS, GQA_HD)), stack(3, (GQA_KV_HEADS, GQA_HD)),
            stack(4, (DIFF_HEADS, 2, DIFF_D)), stack(5, (DIFF_HEADS, DIFF_DV)), new_ret)
```

```python
import functools
import math

import numpy as np
import jax
import jax.numpy as jnp
from jax import lax
from jax.experimental import pallas as pl
from jax.experimental.pallas import tpu as pltpu
from jax.experimental.pallas import tpu_sc as plsc

F32 = jnp.float32
BF16 = jnp.bfloat16

D = 1024
DEPTH = 2
GRID_W = 64
ROPE_BASE = 10000.0
EPS = 1e-6

MLA_HEADS, MLA_NOPE, MLA_ROPE, MLA_V = 8, 64, 32, 64
MLA_Q_LORA, MLA_KV_LORA = 384, 256
GQA_HEADS, GQA_KV_HEADS, GQA_HD = 8, 2, 64
RET_HEADS, RET_DK, RET_DV = 4, 64, 128
DIFF_HEADS, DIFF_D, DIFF_DV = 4, 64, 128
N_BRANCH, BRANCH_W = 4, 512
N_EXPERTS, TOP_K, N_GROUPS, TOPK_GROUPS = 32, 4, 4, 2
EXPERT_FF, SHARED_FF = 256, 256
ROUTE_SCALE = 2.5
GROUP_SIZE = N_EXPERTS // N_GROUPS

LANES = 128
HALF_LANES = 64
VMEM_LIMIT = 56 * 1024 * 1024

C_CQ, C_CKV, C_KPE, C_GQ, C_GK, C_GV = 0, 384, 640, 768, 1280, 1408
C_DQ, C_DK, C_DV, C_RQ, C_RK, C_RV, C_RG, C_GL, C_END = (
    1536, 2048, 2560, 3072, 3328, 3584, 4096, 4608, 8704)
O_CQ, O_CKV, O_KPE, O_GQ, O_GK, O_GV = 0, 384, 640, 672, 1184, 1312
O_RQ, O_RK, O_RV, O_RG, O_DQ, O_DK, O_DV, O_GL, O_END = (
    1440, 1696, 1952, 2464, 2976, 3488, 4000, 4512, 8608)
GQA_ORDER = (0, 4, 1, 5, 2, 6, 3, 7)

KVM_W = 4 * 256 + 512
TM = 256
TQ = 256
TM_MOE_PRE = 512
TMX = 256


def _cparams(sem):
    return pltpu.CompilerParams(dimension_semantics=sem, vmem_limit_bytes=VMEM_LIMIT)


def _const_spec(shape):
    nd = len(shape)
    return pl.BlockSpec(shape, lambda *_: (0,) * nd)


def _rms(x, g):
    return x * lax.rsqrt(jnp.mean(x * x, axis=-1, keepdims=True) + EPS) * g


def _dot(a, b):
    return jnp.dot(a, b, preferred_element_type=F32)


def _dot_nt(a, b):
    return lax.dot_general(a, b, (((1,), (1,)), ((), ())), preferred_element_type=F32)


def _silu(x):
    return x * jax.nn.sigmoid(x)


def _lane_iota(shape):
    return lax.broadcasted_iota(jnp.int32, shape, len(shape) - 1)


def _seg_meansq(x, bd_ref, width):
    sq = x * x
    hi = sq.astype(BF16)
    lo = (sq - hi.astype(F32)).astype(BF16)
    bd = bd_ref[0:width, 0:width]
    return (_dot(hi, bd) + _dot(lo, bd)) * (1.0 / GQA_HD)


def _rope(x, cos, sin_signed, half):
    width = x.shape[-1]
    first = (_lane_iota(x.shape) % (2 * half)) < half
    partner = jnp.where(first, pltpu.roll(x, width - half, 1), pltpu.roll(x, half, 1))
    return x * cos + partner * sin_signed


def _tile_lanes(t, reps):
    return t if reps == 1 else jnp.concatenate([t] * reps, axis=1)


def _mod_kernel(c_ref, w_ref, b_ref, o_ref):
    a = _silu(c_ref[...]).astype(BF16)
    o_ref[...] = _dot(a, w_ref[...].astype(BF16)) + b_ref[...]


def _mod_call(l, cond, w_mod, b_mod):
    n_l, _, n = w_mod.shape
    tn = 1536
    return pl.pallas_call(
        _mod_kernel,
        grid=(n // tn,),
        in_specs=[_const_spec(cond.shape),
                  pl.BlockSpec((None, D, tn), lambda j: (l, 0, j)),
                  pl.BlockSpec((None, 1, tn), lambda j: (l, 0, j))],
        out_specs=pl.BlockSpec((cond.shape[0], tn), lambda j: (0, j)),
        out_shape=jax.ShapeDtypeStruct((cond.shape[0], n), F32),
        compiler_params=_cparams(("arbitrary",)),
        name="mod",
    )(cond, w_mod, b_mod.reshape(n_l, 1, n))


def _inprep_kernel(latent, *refs):
    (x_ref, mod_ref, gpre_ref, win_ref, gmq_ref, wuq_ref, gmkv_ref, wukv_ref,
     ggq_ref, ggk_ref, bd_ref) = refs[:11]
    refs = refs[11:]
    if latent:
        cos64_ref, sin64_ref, cospe_ref, sinpe_ref = refs[:4]
        refs = refs[4:]
    (qm_ref, kvm_ref, gqo_ref, gkv_ref, dqo_ref, dkv_ref, ret_ref, rg_ref, gate_ref) = refs[:9]
    refs = refs[9:]
    if not latent:
        ckv_o, kpe_o, gk_o, gv_o, dk_o, dv_o = refs

    x = x_ref[...]
    mod = mod_ref[...]
    sh1 = mod[:, 0:D]
    sc1 = mod[:, D:2 * D]
    hb = (_rms(x, gpre_ref[...]) * (1.0 + sc1) + sh1).astype(BF16)

    def z(a, b):
        return _dot_nt(hb, win_ref[a:b, :])

    if latent:
        cos64, sin64 = cos64_ref[...], sin64_ref[...]
        cospe, sinpe = cospe_ref[...], sinpe_ref[...]

    cqn = _rms(z(O_CQ, O_CKV), gmq_ref[...]).astype(BF16)
    q = _dot(cqn, wuq_ref[...]) * ((MLA_NOPE + MLA_ROPE) ** -0.5)
    q_nope, q_pe = q[:, 0:512], q[:, 512:768]
    if latent:
        q_pe = _rope(q_pe, _tile_lanes(cospe, 2), _tile_lanes(sinpe, 2), MLA_ROPE // 4)
    qm_ref[:, 0:512] = q_nope.astype(BF16)
    qm_ref[:, 512:768] = q_pe.astype(BF16)

    ckvn = _rms(z(O_CKV, O_KPE), gmkv_ref[...])
    kv = _dot(ckvn.astype(BF16), wukv_ref[...])
    kpe4 = _dot_nt(hb, jnp.concatenate([win_ref[O_KPE:O_GQ, :]] * 4, axis=0))
    if latent:
        kpe4 = _rope(kpe4, cospe, sinpe, MLA_ROPE // 4)
    else:
        ckv_o[...] = ckvn
        kpe_o[...] = kpe4[:, 0:MLA_ROPE]
    kpe_b = kpe4.astype(BF16)
    for p in range(4):
        kvm_ref[:, p * 256:p * 256 + LANES] = kv[:, p * LANES:(p + 1) * LANES].astype(BF16)
        kvm_ref[:, p * 256 + LANES:(p + 1) * 256] = kpe_b
    kvm_ref[:, 1024:1536] = kv[:, 512:1024].astype(BF16)

    gq = _dot_nt(hb, jnp.concatenate(
        [win_ref[O_GQ + h * GQA_HD:O_GQ + (h + 1) * GQA_HD, :] for h in GQA_ORDER], axis=0))
    gq = gq * lax.rsqrt(_seg_meansq(gq, bd_ref, 512) + EPS) * ggq_ref[...]
    gk = z(O_GK, O_GV)
    gk = gk * lax.rsqrt(_seg_meansq(gk, bd_ref, LANES) + EPS) * ggk_ref[...]
    gv = z(O_GV, O_RQ)
    if latent:
        gq = _rope(gq, _tile_lanes(cos64, 4), _tile_lanes(sin64, 4), GQA_HD // 4)
        gk = _rope(gk, cos64, sin64, GQA_HD // 4)
    else:
        gk_o[...] = gk
        gv_o[...] = gv
    gqo_ref[...] = (gq * (GQA_HD ** -0.5)).astype(BF16)
    gkv_ref[:, 0:LANES] = gk.astype(BF16)
    gkv_ref[:, LANES:2 * LANES] = gv.astype(BF16)

    dq = z(O_DQ, O_DK)
    dk = z(O_DK, O_DV)
    dv = z(O_DV, O_GL)
    if latent:
        dq = _rope(dq, _tile_lanes(cos64, 4), _tile_lanes(sin64, 4), DIFF_D // 4)
        dk = _rope(dk, _tile_lanes(cos64, 4), _tile_lanes(sin64, 4), DIFF_D // 4)
    else:
        dk_o[...] = dk
        dv_o[...] = dv
    dqo_ref[...] = (dq * (DIFF_D ** -0.5)).astype(BF16)
    dkv_ref[:, 0:512] = dk.astype(BF16)
    dkv_ref[:, 512:1024] = dv.astype(BF16)

    ret_ref[:, 0:256] = z(O_RQ, O_RK).astype(BF16)
    ret_ref[:, 256:512] = (z(O_RK, O_RV) * (RET_DK ** -0.5)).astype(BF16)
    ret_ref[:, 512:1024] = z(O_RV, O_RG).astype(BF16)
    rg_ref[...] = z(O_RG, O_DQ).astype(BF16)

    for n in range(N_BRANCH):
        gate_ref[:, n * D:(n + 1) * D] = jax.nn.sigmoid(
            z(O_GL + n * D, O_GL + (n + 1) * D)).astype(BF16)


def _inprep_call(latent, l, x, mod3, mod_row, lw, tabs, t_len):
    n_tok = x.shape[0]
    nblk = n_tok // TM
    blk_per_seq = t_len // TM

    def tok(w):
        return pl.BlockSpec((TM, w), lambda i: (i, 0))

    in_specs = [tok(D),
                pl.BlockSpec((None, 1, 6 * D), lambda i: (mod_row(i), 0, 0)),
                _const_spec((1, D)),
                pl.BlockSpec((None, O_END, D), lambda i: (l, 0, 0), pipeline_mode=pl.Buffered(1)),
                _const_spec((1, MLA_Q_LORA)), _const_spec((MLA_Q_LORA, 768)),
                _const_spec((1, MLA_KV_LORA)), _const_spec((MLA_KV_LORA, 1024)),
                _const_spec((1, 512)), _const_spec((1, LANES)), _const_spec((512, 512))]
    args = [x, mod3, lw["g_pre1"], lw["w_in"], lw["g_mla_q"], lw["w_uq"], lw["g_mla_kv"],
            lw["w_ukv"], lw["g_gqa_q"], lw["g_gqa_k"], lw["bd"]]
    if latent:
        tab_spec = pl.BlockSpec((TM, LANES), lambda i: (i % blk_per_seq, 0))
        in_specs += [tab_spec] * 4
        args += list(tabs)
    widths = [768, KVM_W, 512, 256, 512, 1024, 1024, 512, 4 * D]
    out_specs = [tok(w) for w in widths]
    out_shape = [jax.ShapeDtypeStruct((n_tok, w), BF16) for w in widths]
    if not latent:
        cw = [MLA_KV_LORA, MLA_ROPE, 128, 128, 512, 512]
        out_specs += [tok(w) for w in cw]
        out_shape += [jax.ShapeDtypeStruct((n_tok, w), F32) for w in cw]
    return pl.pallas_call(
        functools.partial(_inprep_kernel, latent),
        grid=(nblk,),
        in_specs=in_specs, out_specs=out_specs, out_shape=out_shape,
        compiler_params=_cparams(("arbitrary",)),
        name="inprep_lat" if latent else "inprep_ctx",
    )(*args)


def _pastkv_kernel(ckv_ref, kpe_ref, wukv_ref, o_ref):
    kv = _dot(ckv_ref[...].astype(BF16), wukv_ref[...])
    kpe_b = kpe_ref[...].astype(BF16)
    for p in range(4):
        o_ref[:, p * 256:p * 256 + LANES] = kv[:, p * LANES:(p + 1) * LANES].astype(BF16)
        o_ref[:, p * 256 + LANES:(p + 1) * 256] = kpe_b
    o_ref[:, 1024:1536] = kv[:, 512:1024].astype(BF16)


def _pastkv_call(ckv, kpe4, w_ukv):
    n = ckv.shape[0]
    return pl.pallas_call(
        _pastkv_kernel,
        grid=(n // TM,),
        in_specs=[pl.BlockSpec((TM, MLA_KV_LORA), lambda i: (i, 0)),
                  pl.BlockSpec((TM, LANES), lambda i: (i, 0)),
                  _const_spec((MLA_KV_LORA, 1024))],
        out_specs=pl.BlockSpec((TM, KVM_W), lambda i: (i, 0)),
        out_shape=jax.ShapeDtypeStruct((n, KVM_W), BF16),
        compiler_params=_cparams(("arbitrary",)),
        name="pastkv",
    )(ckv, kpe4, w_ukv)


def _softmax_pv(s, v):
    m = jnp.max(s, axis=-1, keepdims=True)
    p = jnp.exp(s - m)
    l = jnp.sum(p, axis=-1, keepdims=True)
    return _dot(p.astype(BF16), v) / l


def _attn_kernel(lam_init, qm_ref, kvm_ref, gq_ref, gkv_ref, dq_ref, dkv_ref,
                 lam_ref, gdiff_ref, o_ref):
    tq = qm_ref.shape[0]
    lane = _lane_iota((tq, LANES))
    low = lane < HALF_LANES
    zero = jnp.zeros((tq, LANES), BF16)

    for p in range(MLA_HEADS // 2):
        qn = qm_ref[:, p * LANES:(p + 1) * LANES]
        g = p // 2
        qpe = qm_ref[:, 512 + g * LANES:512 + (g + 1) * LANES]
        kk = kvm_ref[:, p * 256:(p + 1) * 256]
        vv = kvm_ref[:, 1024 + p * LANES:1024 + (p + 1) * LANES]
        outs = []
        for half in range(2):
            h = 2 * p + half
            slot = h % 4
            in_slot = (lane >= slot * MLA_ROPE) & (lane < (slot + 1) * MLA_ROPE)
            lhs = jnp.concatenate(
                [jnp.where(low if half == 0 else ~low, qn, zero),
                 jnp.where(in_slot, qpe, zero)], axis=1)
            outs.append(_softmax_pv(_dot_nt(lhs, kk), vv))
        o_ref[:, p * LANES:(p + 1) * LANES] = jnp.where(low, outs[0], outs[1]).astype(BF16)

    kk = gkv_ref[:, 0:LANES]
    vv = gkv_ref[:, LANES:2 * LANES]
    for g in range(GQA_HEADS // 2):
        qg = gq_ref[:, g * LANES:(g + 1) * LANES]
        o_lo = _softmax_pv(_dot_nt(jnp.where(low, qg, zero), kk), vv)
        o_hi = _softmax_pv(_dot_nt(jnp.where(low, zero, qg), kk), vv)
        o_ref[:, 512 + g * LANES:512 + (g + 1) * LANES] = jnp.where(low, o_lo, o_hi).astype(BF16)

    lp = lam_ref[...]
    lam = (jnp.exp(jnp.sum(lp[0:1] * lp[1:2], axis=-1, keepdims=True))
           - jnp.exp(jnp.sum(lp[2:3] * lp[3:4], axis=-1, keepdims=True)) + lam_init)
    for h in range(DIFF_HEADS):
        qh = dq_ref[:, h * LANES:(h + 1) * LANES]
        kk = dkv_ref[:, h * LANES:(h + 1) * LANES]
        vv = dkv_ref[:, 512 + h * LANES:512 + (h + 1) * LANES]
        a1 = _softmax_pv(_dot_nt(jnp.where(low, qh, zero), kk), vv)
        a2 = _softmax_pv(_dot_nt(jnp.where(low, zero, qh), kk), vv)
        od = _rms(a1 - lam * a2, gdiff_ref[...]) * (1.0 - lam_init)
        o_ref[:, 1024 + h * LANES:1024 + (h + 1) * LANES] = od.astype(BF16)


def _attn_call(lam_init, qm, kvm, gq, gkv, dq, dkv, lam_p, g_diff, n_b, t_len, s_len):
    nq = t_len // TQ

    def qspec(w):
        return pl.BlockSpec((TQ, w), lambda b, i: (b * nq + i, 0))

    def kspec(w):
        return pl.BlockSpec((s_len, w), lambda b, i: (b, 0))

    return pl.pallas_call(
        functools.partial(_attn_kernel, lam_init),
        grid=(n_b, nq),
        in_specs=[qspec(768), kspec(KVM_W), qspec(512), kspec(256), qspec(512), kspec(1024),
                  _const_spec((4, DIFF_D)), _const_spec((1, DIFF_DV))],
        out_specs=qspec(3 * BRANCH_W),
        out_shape=jax.ShapeDtypeStruct((n_b * t_len, 3 * BRANCH_W), BF16),
        compiler_params=_cparams(("arbitrary", "arbitrary")),
        name="attn",
    )(qm, kvm, gq, gkv, dq, dkv, lam_p, g_diff)


def _log_sigmoid(x):
    return jnp.minimum(x, 0.0) - jnp.log(1.0 + jnp.exp(-jnp.abs(x)))


def _ret_kernel(latent, t_len, dec_ref, q_ref, k_ref, v_ref, rg_ref, gret_ref, *refs):
    if latent:
        s0_ref, o_ref = refs
    else:
        o_ref, st_ref = refs
    tq = q_ref.shape[0]
    t0 = pl.program_id(1) * tq
    lane = _lane_iota((tq, LANES))
    low = lane < HALF_LANES
    zero = jnp.zeros((tq, LANES), BF16)
    t_idx = (t0 + lax.broadcasted_iota(jnp.int32, (tq, t_len), 0)).astype(F32)
    s_idx = lax.broadcasted_iota(jnp.int32, (tq, t_len), 1).astype(F32)
    dist = t_idx - s_idx
    t_col = (t0 + lax.broadcasted_iota(jnp.int32, (tq, 1), 0)).astype(F32)

    def lg(d, h):
        return _log_sigmoid(jnp.full((1, 1), dec_ref[d, h], F32))

    for h in range(RET_HEADS):
        p, half = h // 2, h % 2
        qp = q_ref[:, p * LANES:(p + 1) * LANES]
        qm = jnp.where(low if half == 0 else ~low, qp, zero)
        kp = k_ref[:, p * LANES:(p + 1) * LANES]
        vh = v_ref[:, h * LANES:(h + 1) * LANES]
        lgf, lgb = lg(0, h), lg(1, h)
        dmask = (jnp.where(dist >= 0, jnp.exp(lgf * jnp.maximum(dist, 0.0)), 0.0)
                 + jnp.where(dist <= 0, jnp.exp(lgb * jnp.maximum(-dist, 0.0)), 0.0))
        o = _dot((_dot_nt(qm, kp) * dmask).astype(BF16), vh)
        if latent:
            sf = s0_ref[0, p].astype(BF16)
            sb = s0_ref[1, p].astype(BF16)
            o = o + _dot(qm, sf) * jnp.exp(lgf * (t_col + 1.0))
            o = o + _dot(qm, sb) * jnp.exp(lgb * (float(t_len) - t_col))
        mu = jnp.mean(o, axis=-1, keepdims=True)
        oc = o - mu
        y = oc * lax.rsqrt(jnp.mean(oc * oc, axis=-1, keepdims=True) + EPS)
        y = y * gret_ref[:, h * LANES:(h + 1) * LANES]
        rg = rg_ref[:, h * LANES:(h + 1) * LANES].astype(F32)
        o_ref[:, h * LANES:(h + 1) * LANES] = (y * _silu(rg)).astype(BF16)

    if not latent:
        s_col = lax.broadcasted_iota(jnp.int32, (t_len, 1), 0).astype(F32)
        lane_t = _lane_iota((1, LANES)) < HALF_LANES
        for p in range(RET_HEADS // 2):
            kp = k_ref[:, p * LANES:(p + 1) * LANES].astype(F32)
            for d in range(2):
                lg_lane = jnp.where(lane_t, lg(d, 2 * p), lg(d, 2 * p + 1))
                expo = (float(t_len) - 1.0 - s_col) if d == 0 else s_col
                kdec_t = jnp.transpose(kp * jnp.exp(lg_lane * expo)).astype(BF16)
                for half in range(2):
                    h = 2 * p + half
                    st = _dot(kdec_t, v_ref[:, h * LANES:(h + 1) * LANES])
                    st_ref[d, h] = st[half * RET_DK:(half + 1) * RET_DK, :]


def _ret_call(latent, dec, ret, rg, g_ret, s0, n_b, t_len):
    nq = t_len // TQ
    assert latent or nq == 1
    in_specs = [pl.BlockSpec(memory_space=pltpu.SMEM),
                pl.BlockSpec((TQ, 256), lambda b, i: (b * nq + i, 0)),
                pl.BlockSpec((t_len, 256), lambda b, i: (b, 1)),
                pl.BlockSpec((t_len, 512), lambda b, i: (b, 1)),
                pl.BlockSpec((TQ, 512), lambda b, i: (b * nq + i, 0)),
                _const_spec((1, 512))]
    args = [dec, ret, ret, ret, rg, g_ret]
    out_specs = [pl.BlockSpec((TQ, 512), lambda b, i: (b * nq + i, 0))]
    out_shape = [jax.ShapeDtypeStruct((n_b * t_len, 512), BF16)]
    if latent:
        in_specs.append(pl.BlockSpec((None, 2, 2, LANES, LANES), lambda b, i: (b, 0, 0, 0, 0)))
        args.append(s0)
    else:
        out_specs.append(pl.BlockSpec((None, 2, RET_HEADS, RET_DK, RET_DV),
                                      lambda b, i: (b, 0, 0, 0, 0)))
        out_shape.append(jax.ShapeDtypeStruct((n_b, 2, RET_HEADS, RET_DK, RET_DV), F32))
    return pl.pallas_call(
        functools.partial(_ret_kernel, latent, t_len),
        grid=(n_b, nq),
        in_specs=in_specs, out_specs=out_specs, out_shape=out_shape,
        compiler_params=_cparams(("arbitrary", "arbitrary")),
        name="ret_lat" if latent else "ret_ctx",
    )(*args)


def _merge_kernel(x_ref, mod_ref, br_ref, or_ref, gate_ref, wbr_ref, wout_ref, gpost_ref, o_ref):
    merged = None
    for n in range(N_BRANCH):
        if n < 2:
            b = br_ref[:, n * BRANCH_W:(n + 1) * BRANCH_W]
        elif n == 2:
            b = or_ref[...]
        else:
            b = br_ref[:, 2 * BRANCH_W:3 * BRANCH_W]
        t = gate_ref[:, n * D:(n + 1) * D].astype(F32) * _dot(b, wbr_ref[n])
        merged = t if merged is None else merged + t
    out = _dot(merged.astype(BF16), wout_ref[...])
    g1 = mod_ref[...][:, 2 * D:3 * D]
    o_ref[...] = x_ref[...] + g1 * _rms(out, gpost_ref[...])


def _merge_call(x, mod3, mod_row, br, o_r, gates, lw):
    n_tok = x.shape[0]

    def tok(w):
        return pl.BlockSpec((TM, w), lambda i: (i, 0))

    return pl.pallas_call(
        _merge_kernel,
        grid=(n_tok // TM,),
        in_specs=[tok(D), pl.BlockSpec((None, 1, 6 * D), lambda i: (mod_row(i), 0, 0)),
                  tok(3 * BRANCH_W), tok(BRANCH_W), tok(4 * D),
                  _const_spec((N_BRANCH, BRANCH_W, D)), _const_spec((D, D)), _const_spec((1, D))],
        out_specs=tok(D),
        out_shape=jax.ShapeDtypeStruct((n_tok, D), F32),
        compiler_params=_cparams(("arbitrary",)),
        name="merge",
    )(x, mod3, br, o_r, gates, lw["w_br"], lw["w_out"], lw["g_post1"])


def _route(logits_t, bias):
    n = logits_t.shape[1]
    scores = jax.nn.sigmoid(logits_t)
    sel = scores + bias
    neg = -jnp.inf
    sub = lax.broadcasted_iota(jnp.int32, (GROUP_SIZE, n), 0)
    grp = []
    for g in range(N_GROUPS):
        blk = sel[g * GROUP_SIZE:(g + 1) * GROUP_SIZE]
        m1 = jnp.max(blk, axis=0, keepdims=True)
        i1 = jnp.min(jnp.where(blk == m1, sub, GROUP_SIZE), axis=0, keepdims=True)
        m2 = jnp.max(jnp.where(sub == i1, neg, blk), axis=0, keepdims=True)
        grp.append(m1 + m2)
    parts = []
    for g in range(N_GROUPS):
        beaten = jnp.zeros((1, n), jnp.int32)
        for o in range(N_GROUPS):
            if o == g:
                continue
            wins = (grp[o] > grp[g]) | (grp[o] == grp[g]) if o < g else (grp[o] > grp[g])
            beaten = beaten + wins.astype(jnp.int32)
        keep = beaten < TOPK_GROUPS
        parts.append(jnp.where(keep, sel[g * GROUP_SIZE:(g + 1) * GROUP_SIZE], neg))
    cur = jnp.concatenate(parts, axis=0)
    eidx = lax.broadcasted_iota(jnp.int32, (N_EXPERTS, n), 0)
    hits, ids, ws = [], [], []
    for _ in range(TOP_K):
        m = jnp.max(cur, axis=0, keepdims=True)
        i = jnp.min(jnp.where(cur == m, eidx, N_EXPERTS), axis=0, keepdims=True)
        hit = eidx == i
        hits.append(hit)
        ids.append(i)
        ws.append(jnp.sum(jnp.where(hit, scores, 0.0), axis=0, keepdims=True))
        cur = jnp.where(hit, neg, cur)
    wsum = ws[0] + ws[1] + ws[2] + ws[3]
    return hits, ids, [w / wsum * ROUTE_SCALE for w in ws]


U32 = jnp.uint32
HIGH16 = np.uint32(0xFFFF0000)


def _bf16_bits(v):
    return lax.bitcast_convert_type(v.astype(BF16).astype(F32), U32)


def _pack_rows(v):
    return (_bf16_bits(v[:, 0:D // 2]) >> 16) | _bf16_bits(v[:, D // 2:D])


def _unpack_rows(p):
    lo = lax.bitcast_convert_type(p << 16, F32)
    hi = lax.bitcast_convert_type(p & HIGH16, F32)
    return jnp.concatenate([lo, hi], axis=1)


def _moe_pre_kernel(x_ref, mod_ref, gpre_ref, wr_ref, br_ref, tri_ref,
                    hp_ref, eidx_ref, rank_ref, comb_ref, cnt_ref, run_ref):
    tm = x_ref.shape[0]

    @pl.when(pl.program_id(0) == 0)
    def _():
        run_ref[...] = jnp.zeros_like(run_ref)

    mod = mod_ref[...]
    sh2, sc2 = mod[:, 3 * D:4 * D], mod[:, 4 * D:5 * D]
    h = _rms(x_ref[...], gpre_ref[...]) * (1.0 + sc2) + sh2
    hp_ref[...] = _pack_rows(h)
    hb = h.astype(BF16)
    h_lo = (h - hb.astype(F32)).astype(BF16)
    wr = wr_ref[...]
    wr_hi = wr.astype(BF16)
    wr_lo = (wr - wr_hi.astype(F32)).astype(BF16)
    logits_t = _dot_nt(wr_hi, hb) + _dot_nt(wr_hi, h_lo) + _dot_nt(wr_lo, hb)
    hits, ids, ws = _route(logits_t, br_ref[...])

    picked = jnp.zeros((N_EXPERTS, tm), F32)
    for hit in hits:
        picked = jnp.where(hit, 1.0, picked)
    before = _dot(picked.astype(BF16), tri_ref[...]) + run_ref[:, 0:1]
    sub8 = lax.broadcasted_iota(jnp.int32, (8, tm), 0)
    comb8 = jnp.zeros((8, tm), F32)
    for k in range(TOP_K):
        rank = jnp.sum(jnp.where(hits[k], before, 0.0), axis=0, keepdims=True)
        eidx_ref[k:k + 1, :] = ids[k]
        rank_ref[k:k + 1, :] = rank.astype(jnp.int32)
        comb8 = jnp.where(sub8 == k, ws[k], comb8)
    comb_ref[...] = jnp.transpose(
        jnp.concatenate([comb8, jnp.zeros((LANES - 8, tm), F32)], axis=0))
    run_ref[...] = run_ref[...] + jnp.sum(picked, axis=1, keepdims=True)
    cnt_ref[...] = run_ref[...]


def _moe_pre_call(x, mod3, mod_row, lw):
    n_tok = x.shape[0]
    tm = TM_MOE_PRE
    tri = np.arange(tm)
    tri = jnp.asarray(tri[:, None] < tri[None, :], BF16)
    row4 = pl.BlockSpec((TOP_K, tm), lambda i: (0, i))
    return pl.pallas_call(
        _moe_pre_kernel,
        grid=(n_tok // tm,),
        in_specs=[pl.BlockSpec((tm, D), lambda i: (i, 0)),
                  pl.BlockSpec((None, 1, 6 * D), lambda i: (mod_row(i), 0, 0)),
                  _const_spec((1, D)), _const_spec((N_EXPERTS, D)), _const_spec((N_EXPERTS, 1)),
                  _const_spec((tm, tm))],
        out_specs=[pl.BlockSpec((tm, D // 2), lambda i: (i, 0)), row4, row4,
                   pl.BlockSpec((tm, LANES), lambda i: (i, 0)),
                   _const_spec((N_EXPERTS, LANES))],
        out_shape=[jax.ShapeDtypeStruct((n_tok, D // 2), U32),
                   jax.ShapeDtypeStruct((TOP_K, n_tok), jnp.int32),
                   jax.ShapeDtypeStruct((TOP_K, n_tok), jnp.int32),
                   jax.ShapeDtypeStruct((n_tok, LANES), F32),
                   jax.ShapeDtypeStruct((N_EXPERTS, LANES), F32)],
        scratch_shapes=[pltpu.VMEM((N_EXPERTS, LANES), F32)],
        compiler_params=_cparams(("arbitrary",)),
        name="moe_pre",
    )(x, mod3, lw["g_pre2"], lw["w_router_t"], lw["b_router"], tri)


def _moe_plan_kernel(eidx_ref, rank_ref, cnt_ref, dest_ref, te_ref, tv_ref, tn_ref):
    tm = eidx_ref.shape[1]
    cnt = cnt_ref[...]
    padded = jnp.ceil(cnt * (1.0 / TMX)) * TMX
    row = lax.broadcasted_iota(jnp.int32, cnt.shape, 0)
    incl = padded
    shift = 1
    while shift < N_EXPERTS:
        incl = incl + jnp.where(row >= shift, pltpu.roll(incl, shift, 0), 0.0)
        shift *= 2
    start = (incl - padded)[:, 0:1]
    end = incl[:, 0:1]
    erow = lax.broadcasted_iota(jnp.int32, (N_EXPERTS, tm), 0)
    for k in range(TOP_K):
        mine = erow == eidx_ref[k:k + 1, :]
        base = jnp.sum(jnp.where(mine, start, 0.0), axis=0, keepdims=True)
        dest_ref[k:k + 1, :] = rank_ref[k:k + 1, :] + base.astype(jnp.int32)

    @pl.when(pl.program_id(0) == 0)
    def _():
        tile0 = (_lane_iota((1, LANES)) * TMX).astype(F32)
        owner = jnp.sum(jnp.where(end <= tile0, 1.0, 0.0), axis=0, keepdims=True)
        owner = jnp.minimum(owner, N_EXPERTS - 1.0)
        erow_t = lax.broadcasted_iota(jnp.int32, (N_EXPERTS, LANES), 0).astype(F32)
        left = jnp.sum(jnp.where(erow_t == owner, cnt[:, 0:1] - (tile0 - start), 0.0),
                       axis=0, keepdims=True)
        later = (erow_t > owner) & (cnt[:, 0:1] > 0.0)
        nxt = jnp.min(jnp.where(later, erow_t, float(N_EXPERTS)), axis=0, keepdims=True)
        te_ref[...] = owner.astype(jnp.int32)
        tv_ref[...] = jnp.clip(left, 0.0, float(TMX)).astype(jnp.int32)
        tn_ref[...] = nxt.astype(jnp.int32)


def _moe_plan_call(eidx, rank, cnt):
    n_tok = eidx.shape[1]
    tm = TM_MOE_PRE
    row4 = pl.BlockSpec((TOP_K, tm), lambda i: (0, i))
    tiles = jax.ShapeDtypeStruct((1, LANES), jnp.int32)
    return pl.pallas_call(
        _moe_plan_kernel,
        grid=(n_tok // tm,),
        in_specs=[row4, row4, _const_spec((N_EXPERTS, LANES))],
        out_specs=[row4] + [_const_spec((1, LANES))] * 3,
        out_shape=[jax.ShapeDtypeStruct((TOP_K, n_tok), jnp.int32), tiles, tiles, tiles],
        compiler_params=_cparams(("arbitrary",)),
        name="moe_plan",
    )(eidx, rank, cnt)


def _experts_kernel(l, te_ref, tv_ref, tn_ref, xs_ref, wgu_hbm, wdn_hbm, ys_ref,
                    wgu_f, wdn_f, wgu_b, wdn_b, sem, slot_ref):
    j = pl.program_id(0)
    valid = tv_ref[j]
    expert = te_ref[j]

    def fetch(e, slot):
        return (pltpu.make_async_copy(wgu_hbm.at[l, e], wgu_f.at[slot], sem.at[slot, 0]),
                pltpu.make_async_copy(wdn_hbm.at[l, e], wdn_f.at[slot], sem.at[slot, 1]))

    @pl.when(j == 0)
    def _():
        slot_ref[0] = 0
        for cp in fetch(expert, 0):
            cp.start()

    first_tile = (j == 0) | (expert != te_ref[jnp.maximum(j - 1, 0)])

    @pl.when(first_tile & (valid > 0))
    def _():
        slot = slot_ref[0]
        for cp in fetch(expert, slot):
            cp.wait()
        wgu_b[...] = wgu_f[slot].astype(BF16)
        wdn_b[...] = wdn_f[slot].astype(BF16)
        nxt = tn_ref[j]

        @pl.when(nxt < N_EXPERTS)
        def _():
            for cp in fetch(nxt, 1 - slot):
                cp.start()

        slot_ref[0] = 1 - slot

    @pl.when(valid > 0)
    def _():
        rows = lax.broadcasted_iota(jnp.int32, (TMX, D), 0)
        x = jnp.where(rows < valid, _unpack_rows(xs_ref[...]), 0.0).astype(BF16)
        gu = _dot(x, wgu_b[...])
        a = _silu(gu[:, 0:EXPERT_FF]) * gu[:, EXPERT_FF:2 * EXPERT_FF]
        ys_ref[...] = _pack_rows(_dot(a.astype(BF16), wdn_b[...]))

    @pl.when(valid <= 0)
    def _():
        ys_ref[...] = jnp.zeros_like(ys_ref)


def _experts_call(l, xs, te, tv, tn, w_gu, w_dn):
    n_tiles = xs.shape[0] // TMX
    grid_spec = pltpu.PrefetchScalarGridSpec(
        num_scalar_prefetch=3,
        grid=(n_tiles,),
        in_specs=[pl.BlockSpec((TMX, D // 2), lambda j, *_: (j, 0)),
                  pl.BlockSpec(memory_space=pl.ANY), pl.BlockSpec(memory_space=pl.ANY)],
        out_specs=pl.BlockSpec((TMX, D // 2), lambda j, *_: (j, 0)),
        scratch_shapes=[pltpu.VMEM((2, D, 2 * EXPERT_FF), F32), pltpu.VMEM((2, EXPERT_FF, D), F32),
                        pltpu.VMEM((D, 2 * EXPERT_FF), BF16), pltpu.VMEM((EXPERT_FF, D), BF16),
                        pltpu.SemaphoreType.DMA((2, 2)), pltpu.SMEM((1,), jnp.int32)])
    return pl.pallas_call(
        functools.partial(_experts_kernel, l),
        grid_spec=grid_spec,
        out_shape=jax.ShapeDtypeStruct(xs.shape, U32),
        compiler_params=_cparams(("arbitrary",)),
        name="moe_experts",
    )(te, tv, tn, xs, w_gu, w_dn)


def _moe_post_kernel(x_ref, mod_ref, hp_ref, yg_ref, comb_ref, wsgu_ref, wsdn_ref, gpost_ref,
                     o_ref):
    hb = _unpack_rows(hp_ref[...]).astype(BF16)
    sgu = _dot(hb, wsgu_ref[...])
    sa = _silu(sgu[:, 0:SHARED_FF]) * sgu[:, SHARED_FF:2 * SHARED_FF]
    acc = _dot(sa.astype(BF16), wsdn_ref[...])
    comb = comb_ref[...]
    for k in range(TOP_K):
        acc = acc + comb[:, k:k + 1] * _unpack_rows(yg_ref[k])
    g2 = mod_ref[...][:, 5 * D:6 * D]
    o_ref[...] = x_ref[...] + g2 * _rms(acc, gpost_ref[...])


def _moe_post_call(x, mod3, mod_row, hp, yg, comb, lw):
    n_tok = x.shape[0]
    tm = TM_MOE_PRE
    return pl.pallas_call(
        _moe_post_kernel,
        grid=(n_tok // tm,),
        in_specs=[pl.BlockSpec((tm, D), lambda i: (i, 0)),
                  pl.BlockSpec((None, 1, 6 * D), lambda i: (mod_row(i), 0, 0)),
                  pl.BlockSpec((tm, D // 2), lambda i: (i, 0)),
                  pl.BlockSpec((TOP_K, tm, D // 2), lambda i: (0, i, 0)),
                  pl.BlockSpec((tm, LANES), lambda i: (i, 0)),
                  _const_spec((D, 2 * SHARED_FF)), _const_spec((SHARED_FF, D)),
                  _const_spec((1, D))],
        out_specs=pl.BlockSpec((tm, D), lambda i: (i, 0)),
        out_shape=jax.ShapeDtypeStruct((n_tok, D), F32),
        compiler_params=_cparams(("arbitrary",)),
        name="moe_post",
    )(x, mod3, hp, yg, comb, lw["w_sh_gu"], lw["w_sh_down"], lw["g_post2"])


def _moe_call(l, x, mod3, mod_row, lw):
    n_tok = x.shape[0]
    n_slots = -(-(TOP_K * n_tok + N_EXPERTS * (TMX - 1)) // TMX) * TMX
    assert n_slots // TMX <= LANES
    hp, eidx, rank, comb, cnt = _moe_pre_call(x, mod3, mod_row, lw)
    dest, te, tv, tn = _moe_plan_call(eidx, rank, cnt)
    dest = dest.reshape(TOP_K * n_tok)
    xs = _sc_scatter_rows(hp, dest, n_slots)
    ys = _experts_call(l, xs, te[0], tv[0], tn[0], lw["w_exp_gu"], lw["w_exp_down"])
    yg = _sc_gather_rows(ys, dest).reshape(TOP_K, n_tok, D // 2)
    return _moe_post_call(x, mod3, mod_row, hp, yg, comb, lw)


SC_CORES, SC_SUBCORES = 2, 16
SC_WORKERS = SC_CORES * SC_SUBCORES


def _sc_gather_rows(table, idx, chunk=64):
    n_out, width = idx.shape[0], table.shape[1]
    per_worker = n_out // SC_WORKERS
    n_chunks = per_worker // chunk
    assert per_worker * SC_WORKERS == n_out and n_chunks * chunk == per_worker
    mesh = plsc.VectorSubcoreMesh(core_axis_name="c", subcore_axis_name="s",
                                  num_cores=SC_CORES, num_subcores=SC_SUBCORES)

    @functools.partial(
        pl.kernel, mesh=mesh,
        out_type=jax.ShapeDtypeStruct((n_out, width), table.dtype),
        scratch_types=[pltpu.VMEM((chunk,), jnp.int32), pltpu.VMEM((chunk, width), table.dtype),
                       pltpu.SemaphoreType.DMA],
        name="sc_gather")
    def gather(table_hbm, idx_hbm, out_hbm, idx_v, rows_v, sem):
        base = (lax.axis_index("s") * SC_CORES + lax.axis_index("c")) * per_worker

        @pl.loop(0, n_chunks)
        def _(j):
            off = base + j * chunk
            pltpu.sync_copy(idx_hbm.at[pl.ds(off, chunk)], idx_v)
            pltpu.async_copy(table_hbm.at[idx_v], rows_v, sem).wait()
            pltpu.sync_copy(rows_v, out_hbm.at[pl.ds(off, chunk)])

    return gather(table, idx)


def _sc_scatter_rows(rows, dest, n_slots, chunk=64):
    n_tok, width = rows.shape
    per_worker = n_tok // SC_WORKERS
    n_chunks = per_worker // chunk
    assert per_worker * SC_WORKERS == n_tok and n_chunks * chunk == per_worker
    mesh = plsc.VectorSubcoreMesh(core_axis_name="c", subcore_axis_name="s",
                                  num_cores=SC_CORES, num_subcores=SC_SUBCORES)

    @functools.partial(
        pl.kernel, mesh=mesh,
        out_type=jax.ShapeDtypeStruct((n_slots, width), rows.dtype),
        scratch_types=[pltpu.VMEM((chunk,), jnp.int32), pltpu.VMEM((chunk, width), rows.dtype)],
        name="sc_scatter")
    def scatter(rows_hbm, dest_hbm, out_hbm, idx_v, rows_v):
        base = (lax.axis_index("s") * SC_CORES + lax.axis_index("c")) * per_worker

        @pl.loop(0, n_chunks)
        def _(j):
            off = base + j * chunk
            pltpu.sync_copy(rows_hbm.at[pl.ds(off, chunk)], rows_v)
            for k in range(TOP_K):
                pltpu.sync_copy(dest_hbm.at[pl.ds(k * n_tok + off, chunk)], idx_v)
                pltpu.sync_copy(rows_v, out_hbm.at[idx_v])

    return scatter(rows, dest)


def _rope_tables(t_len):
    pos = np.arange(t_len)
    row, col = pos // GRID_W, pos % GRID_W

    def tab(r):
        half = r // 2
        freq = ROPE_BASE ** (-np.arange(half, dtype=np.float64) / half)
        sign = np.concatenate([-np.ones(half), np.ones(half)])
        cs, sn = [], []
        for p in (row, col):
            ang = p[:, None].astype(np.float64) * freq[None, :]
            cs.append(np.concatenate([np.cos(ang), np.cos(ang)], axis=1))
            sn.append(np.concatenate([np.sin(ang), np.sin(ang)], axis=1) * sign[None, :])
        return np.concatenate(cs, axis=1), np.concatenate(sn, axis=1)

    c64, s64 = tab(GQA_HD // 2)
    cpe, spe = tab(MLA_ROPE // 2)
    out = (np.tile(c64, (1, 2)), np.tile(s64, (1, 2)), np.tile(cpe, (1, 4)), np.tile(spe, (1, 4)))
    return tuple(jnp.asarray(a, F32) for a in out)


def _layer_weights(l, p):
    w_uq = p["w_mla_uq"][l].reshape(MLA_Q_LORA, MLA_HEADS, MLA_NOPE + MLA_ROPE)
    w_uq = jnp.concatenate([w_uq[:, :, :MLA_NOPE].reshape(MLA_Q_LORA, -1),
                            w_uq[:, :, MLA_NOPE:].reshape(MLA_Q_LORA, -1)], axis=1)
    w_ukv = p["w_mla_ukv"][l].reshape(MLA_KV_LORA, MLA_HEADS, MLA_NOPE + MLA_V)
    w_ukv = jnp.concatenate([w_ukv[:, :, :MLA_NOPE].reshape(MLA_KV_LORA, -1),
                             w_ukv[:, :, MLA_NOPE:].reshape(MLA_KV_LORA, -1)], axis=1)
    w_br = p["w_br"][l]
    w_br_gqa = w_br[1].reshape(GQA_HEADS, GQA_HD, D)[jnp.array(GQA_ORDER)].reshape(BRANCH_W, D)
    w_br = jnp.stack([w_br[0], w_br_gqa, w_br[2], w_br[3]], axis=0)
    blk = np.arange(512) // GQA_HD
    return {
        "g_pre1": p["g_pre1"][l].reshape(1, D), "g_post1": p["g_post1"][l].reshape(1, D),
        "g_pre2": p["g_pre2"][l].reshape(1, D), "g_post2": p["g_post2"][l].reshape(1, D),
        "w_in": p["w_in_packed"],
        "g_mla_q": p["g_mla_q"][l].reshape(1, -1), "w_uq": w_uq.astype(BF16),
        "g_mla_kv": p["g_mla_kv"][l].reshape(1, -1), "w_ukv": w_ukv.astype(BF16),
        "g_gqa_q": jnp.tile(p["g_gqa_q"][l], GQA_HEADS).reshape(1, -1),
        "g_gqa_k": jnp.tile(p["g_gqa_k"][l], GQA_KV_HEADS).reshape(1, -1),
        "bd": jnp.asarray(blk[:, None] == blk[None, :], BF16),
        "ret_decay": p["ret_decay"][l],
        "g_ret": p["g_ret"][l].reshape(1, -1),
        "diff_lambda": p["diff_lambda"][l], "g_diff": p["g_diff"][l].reshape(1, -1),
        "w_br": w_br.astype(BF16), "w_out": p["w_out"][l].astype(BF16),
        "w_router_t": p["w_router"][l].T, "b_router": p["b_router"][l].reshape(-1, 1),
        "w_exp_gu": p["w_exp_gu"], "w_exp_down": p["w_exp_down"],
        "w_sh_gu": p["w_sh_gu"][l].astype(BF16), "w_sh_down": p["w_sh_down"][l].astype(BF16),
    }


def _mixers(latent, l, x, mod3, mod_row, lw, n_b, t_len, tabs=None, past=None, s0=None):
    lam_init = 0.8 - 0.6 * math.exp(-0.3 * l)
    outs = _inprep_call(latent, l, x, mod3, mod_row, lw, tabs, t_len)
    qm, kvm, gq, gkv, dq, dkv, ret, rg, gates = outs[:9]
    s_len = t_len
    if latent:
        past_kvm, past_gkv, past_dkv = past
        p_len = past_gkv.shape[1]
        s_len = p_len + t_len

        def cat(a, b):
            return jnp.concatenate([a, b.reshape(n_b, t_len, -1)], axis=1).reshape(n_b * s_len, -1)

        kvm, gkv, dkv = cat(past_kvm, kvm), cat(past_gkv, gkv), cat(past_dkv, dkv)
    br = _attn_call(lam_init, qm, kvm, gq, gkv, dq, dkv, lw["diff_lambda"], lw["g_diff"],
                    n_b, t_len, s_len)
    r = _ret_call(latent, lw["ret_decay"], ret, rg, lw["g_ret"], s0, n_b, t_len)
    y = _merge_call(x, mod3, mod_row, br, r[0], gates, lw)
    cache = None if latent else tuple(outs[9:]) + (r[1],)
    return y, cache


def kernel(x_prompt, x_sample, cache_mla_ckv, cache_mla_kpe, cache_gqa_k, cache_gqa_v, cache_diff_k, cache_diff_v, state_ret, c, c_ctx, w_mod, b_mod, g_pre1, g_post1, g_pre2, g_post2, w_in, g_mla_q, w_mla_uq, g_mla_kv, w_mla_ukv, g_gqa_q, g_gqa_k, ret_decay, g_ret, diff_lambda, g_diff, w_br, w_out, w_router, b_router, w_exp_gu, w_exp_down, w_sh_gu, w_sh_down):
    params = dict(w_in_packed=jnp.swapaxes(w_in, 1, 2).astype(BF16), g_pre1=g_pre1, g_post1=g_post1, g_pre2=g_pre2,
                  g_post2=g_post2, g_mla_q=g_mla_q, w_mla_uq=w_mla_uq,
                  g_mla_kv=g_mla_kv, w_mla_ukv=w_mla_ukv, g_gqa_q=g_gqa_q, g_gqa_k=g_gqa_k,
                  ret_decay=ret_decay, g_ret=g_ret, diff_lambda=diff_lambda, g_diff=g_diff,
                  w_br=w_br, w_out=w_out, w_router=w_router, b_router=b_router,
                  w_exp_gu=w_exp_gu, w_exp_down=w_exp_down, w_sh_gu=w_sh_gu, w_sh_down=w_sh_down)
    n_bc, t_c, _ = x_prompt.shape
    n_bl, t_l, _ = x_sample.shape
    p_len = cache_mla_ckv.shape[2]
    tabs = _rope_tables(t_l)
    n_cond = 8
    cond = jnp.concatenate([c_ctx[None, :], c, jnp.zeros((n_cond - 1 - n_bl, D), F32)], axis=0)
    blk_c, blk_l = t_c // TM, t_l // TM
    assert t_l % TM_MOE_PRE == 0 and (t_c * n_bc) % TM_MOE_PRE == 0

    yp = x_prompt.reshape(n_bc * t_c, D)
    ys = x_sample.reshape(n_bl * t_l, D)
    caches = []
    for l in range(DEPTH):
        lw = _layer_weights(l, params)
        mod3 = _mod_call(l, cond, w_mod, b_mod).reshape(n_cond, 1, 6 * D)
        yp, cache = _mixers(False, l, yp, mod3, lambda i: 0, lw, n_bc, t_c)
        yp = _moe_call(l, yp, mod3, lambda i: 0, lw)
        caches.append(cache)
        past_kvm = _pastkv_call(cache_mla_ckv[:, l].reshape(n_bl * p_len, -1),
                                jnp.tile(cache_mla_kpe[:, l].reshape(n_bl * p_len, -1), (1, 4)),
                                lw["w_ukv"]).reshape(n_bl, p_len, -1)
        past_gkv = jnp.concatenate([cache_gqa_k[:, l].reshape(n_bl, p_len, -1),
                                    cache_gqa_v[:, l].reshape(n_bl, p_len, -1)], axis=-1).astype(BF16)
        past_dkv = jnp.concatenate([cache_diff_k[:, l].reshape(n_bl, p_len, -1),
                                    cache_diff_v[:, l].reshape(n_bl, p_len, -1)], axis=-1).astype(BF16)
        s0 = state_ret[:, l].reshape(n_bl, 2, RET_HEADS // 2, 2 * RET_DK, RET_DV)
        ys, _ = _mixers(True, l, ys, mod3, lambda i: 1 + i // blk_l, lw, n_bl, t_l, tabs=tabs,
                        past=(past_kvm, past_gkv, past_dkv), s0=s0)
        ys = _moe_call(l, ys, mod3, lambda i: 1 + i // (t_l // TM_MOE_PRE), lw)

    def stack(k, shape):
        return jnp.stack([caches[l][k].reshape((n_bc, t_c) + shape) for l in range(DEPTH)], axis=1)

    new_ret = jnp.stack([caches[l][6] for l in range(DEPTH)], axis=1)
    return (yp.reshape(n_bc, t_c, D), ys.reshape(n_bl, t_l, D),
            stack(0, (MLA_KV_LORA,)), stack(1, (MLA_ROPE,)),
            stack(2, (GQA_KV_HEADS, GQA_HD)), stack(3, (GQA_KV_HEADS, GQA_HD)),
            stack(4, (DIFF_HEADS, 2, DIFF_D)), stack(5, (DIFF_HEADS, DIFF_DV)), new_ret)
```

```python
import functools
import math

import numpy as np
import jax
import jax.numpy as jnp
from jax import lax
from jax.experimental import pallas as pl
from jax.experimental.pallas import tpu as pltpu
from jax.experimental.pallas import tpu_sc as plsc

F32 = jnp.float32
BF16 = jnp.bfloat16

D = 1024
DEPTH = 2
GRID_W = 64
ROPE_BASE = 10000.0
EPS = 1e-6

MLA_HEADS, MLA_NOPE, MLA_ROPE, MLA_V = 8, 64, 32, 64
MLA_Q_LORA, MLA_KV_LORA = 384, 256
GQA_HEADS, GQA_KV_HEADS, GQA_HD = 8, 2, 64
RET_HEADS, RET_DK, RET_DV = 4, 64, 128
DIFF_HEADS, DIFF_D, DIFF_DV = 4, 64, 128
N_BRANCH, BRANCH_W = 4, 512
N_EXPERTS, TOP_K, N_GROUPS, TOPK_GROUPS = 32, 4, 4, 2
EXPERT_FF, SHARED_FF = 256, 256
ROUTE_SCALE = 2.5
GROUP_SIZE = N_EXPERTS // N_GROUPS

LANES = 128
HALF_LANES = 64
VMEM_LIMIT = 56 * 1024 * 1024

C_CQ, C_CKV, C_KPE, C_GQ, C_GK, C_GV = 0, 384, 640, 768, 1280, 1408
C_DQ, C_DK, C_DV, C_RQ, C_RK, C_RV, C_RG, C_GL, C_END = (
    1536, 2048, 2560, 3072, 3328, 3584, 4096, 4608, 8704)
O_CQ, O_CKV, O_KPE, O_GQ, O_GK, O_GV = 0, 384, 640, 672, 1184, 1312
O_RQ, O_RK, O_RV, O_RG, O_DQ, O_DK, O_DV, O_GL, O_END = (
    1440, 1696, 1952, 2464, 2976, 3488, 4000, 4512, 8608)
GQA_ORDER = (0, 4, 1, 5, 2, 6, 3, 7)

KVM_W = 4 * 256 + 512
TM = 256
TQ = 256
TM_MOE_PRE = 512
TMX = 256
W_SLOTS = 4


def _cparams(sem):
    return pltpu.CompilerParams(dimension_semantics=sem, vmem_limit_bytes=VMEM_LIMIT)


def _const_spec(shape):
    nd = len(shape)
    return pl.BlockSpec(shape, lambda *_: (0,) * nd)


def _rms(x, g):
    return x * lax.rsqrt(jnp.mean(x * x, axis=-1, keepdims=True) + EPS) * g


def _dot(a, b):
    return jnp.dot(a, b, preferred_element_type=F32)


def _dot_nt(a, b):
    return lax.dot_general(a, b, (((1,), (1,)), ((), ())), preferred_element_type=F32)


def _silu(x):
    return x * jax.nn.sigmoid(x)


def _lane_iota(shape):
    return lax.broadcasted_iota(jnp.int32, shape, len(shape) - 1)


def _seg_meansq(x, bd_ref, width):
    sq = x * x
    hi = sq.astype(BF16)
    lo = (sq - hi.astype(F32)).astype(BF16)
    bd = bd_ref[0:width, 0:width]
    return (_dot(hi, bd) + _dot(lo, bd)) * (1.0 / GQA_HD)


def _rope(x, cos, sin_signed, half):
    width = x.shape[-1]
    first = (_lane_iota(x.shape) % (2 * half)) < half
    partner = jnp.where(first, pltpu.roll(x, width - half, 1), pltpu.roll(x, half, 1))
    return x * cos + partner * sin_signed


def _tile_lanes(t, reps):
    return t if reps == 1 else jnp.concatenate([t] * reps, axis=1)


def _mod_kernel(c_ref, w_ref, b_ref, o_ref):
    a = _silu(c_ref[...]).astype(BF16)
    o_ref[...] = _dot(a, w_ref[...].astype(BF16)) + b_ref[...]


def _mod_call(l, cond, w_mod, b_mod):
    n_l, _, n = w_mod.shape
    tn = 1536
    return pl.pallas_call(
        _mod_kernel,
        grid=(n // tn,),
        in_specs=[_const_spec(cond.shape),
                  pl.BlockSpec((None, D, tn), lambda j: (l, 0, j)),
                  pl.BlockSpec((None, 1, tn), lambda j: (l, 0, j))],
        out_specs=pl.BlockSpec((cond.shape[0], tn), lambda j: (0, j)),
        out_shape=jax.ShapeDtypeStruct((cond.shape[0], n), F32),
        compiler_params=_cparams(("arbitrary",)),
        name="mod",
    )(cond, w_mod, b_mod.reshape(n_l, 1, n))


def _inprep_kernel(latent, *refs):
    (x_ref, mod_ref, gpre_ref, win_ref, gmq_ref, wuq_ref, gmkv_ref, wukv_ref,
     ggq_ref, ggk_ref, bd_ref) = refs[:11]
    refs = refs[11:]
    if latent:
        cos64_ref, sin64_ref, cospe_ref, sinpe_ref = refs[:4]
        refs = refs[4:]
    (qm_ref, kvm_ref, gqo_ref, gkv_ref, dqo_ref, dkv_ref, ret_ref, rg_ref, gate_ref) = refs[:9]
    refs = refs[9:]
    if not latent:
        ckv_o, kpe_o, gk_o, gv_o, dk_o, dv_o = refs

    x = x_ref[...]
    mod = mod_ref[...]
    sh1 = mod[:, 0:D]
    sc1 = mod[:, D:2 * D]
    hb = (_rms(x, gpre_ref[...]) * (1.0 + sc1) + sh1).astype(BF16)

    def z(a, b):
        return _dot_nt(hb, win_ref[a:b, :])

    if latent:
        cos64, sin64 = cos64_ref[...], sin64_ref[...]
        cospe, sinpe = cospe_ref[...], sinpe_ref[...]

    cqn = _rms(z(O_CQ, O_CKV), gmq_ref[...]).astype(BF16)
    q = _dot(cqn, wuq_ref[...]) * ((MLA_NOPE + MLA_ROPE) ** -0.5)
    q_nope, q_pe = q[:, 0:512], q[:, 512:768]
    if latent:
        q_pe = _rope(q_pe, _tile_lanes(cospe, 2), _tile_lanes(sinpe, 2), MLA_ROPE // 4)
    qm_ref[:, 0:512] = q_nope.astype(BF16)
    qm_ref[:, 512:768] = q_pe.astype(BF16)

    ckvn = _rms(z(O_CKV, O_KPE), gmkv_ref[...])
    kv = _dot(ckvn.astype(BF16), wukv_ref[...])
    kpe4 = _dot_nt(hb, jnp.concatenate([win_ref[O_KPE:O_GQ, :]] * 4, axis=0))
    if latent:
        kpe4 = _rope(kpe4, cospe, sinpe, MLA_ROPE // 4)
    else:
        ckv_o[...] = ckvn
        kpe_o[...] = kpe4[:, 0:MLA_ROPE]
    kpe_b = kpe4.astype(BF16)
    for p in range(4):
        kvm_ref[:, p * 256:p * 256 + LANES] = kv[:, p * LANES:(p + 1) * LANES].astype(BF16)
        kvm_ref[:, p * 256 + LANES:(p + 1) * 256] = kpe_b
    kvm_ref[:, 1024:1536] = kv[:, 512:1024].astype(BF16)

    gq = _dot_nt(hb, jnp.concatenate(
        [win_ref[O_GQ + h * GQA_HD:O_GQ + (h + 1) * GQA_HD, :] for h in GQA_ORDER], axis=0))
    gq = gq * lax.rsqrt(_seg_meansq(gq, bd_ref, 512) + EPS) * ggq_ref[...]
    gk = z(O_GK, O_GV)
    gk = gk * lax.rsqrt(_seg_meansq(gk, bd_ref, LANES) + EPS) * ggk_ref[...]
    gv = z(O_GV, O_RQ)
    if latent:
        gq = _rope(gq, _tile_lanes(cos64, 4), _tile_lanes(sin64, 4), GQA_HD // 4)
        gk = _rope(gk, cos64, sin64, GQA_HD // 4)
    else:
        gk_o[...] = gk
        gv_o[...] = gv
    gqo_ref[...] = (gq * (GQA_HD ** -0.5)).astype(BF16)
    gkv_ref[:, 0:LANES] = gk.astype(BF16)
    gkv_ref[:, LANES:2 * LANES] = gv.astype(BF16)

    dq = z(O_DQ, O_DK)
    dk = z(O_DK, O_DV)
    dv = z(O_DV, O_GL)
    if latent:
        dq = _rope(dq, _tile_lanes(cos64, 4), _tile_lanes(sin64, 4), DIFF_D // 4)
        dk = _rope(dk, _tile_lanes(cos64, 4), _tile_lanes(sin64, 4), DIFF_D // 4)
    else:
        dk_o[...] = dk
        dv_o[...] = dv
    dqo_ref[...] = (dq * (DIFF_D ** -0.5)).astype(BF16)
    dkv_ref[:, 0:512] = dk.astype(BF16)
    dkv_ref[:, 512:1024] = dv.astype(BF16)

    ret_ref[:, 0:256] = z(O_RQ, O_RK).astype(BF16)
    ret_ref[:, 256:512] = (z(O_RK, O_RV) * (RET_DK ** -0.5)).astype(BF16)
    ret_ref[:, 512:1024] = z(O_RV, O_RG).astype(BF16)
    rg_ref[...] = z(O_RG, O_DQ).astype(BF16)

    for n in range(N_BRANCH):
        gate_ref[:, n * D:(n + 1) * D] = jax.nn.sigmoid(
            z(O_GL + n * D, O_GL + (n + 1) * D)).astype(BF16)


def _inprep_call(latent, l, x, mod3, mod_row, lw, tabs, t_len):
    n_tok = x.shape[0]
    nblk = n_tok // TM
    blk_per_seq = t_len // TM

    def tok(w):
        return pl.BlockSpec((TM, w), lambda i: (i, 0))

    in_specs = [tok(D),
                pl.BlockSpec((None, 1, 6 * D), lambda i: (mod_row(i), 0, 0)),
                _const_spec((1, D)),
                pl.BlockSpec((None, O_END, D), lambda i: (l, 0, 0), pipeline_mode=pl.Buffered(1)),
                _const_spec((1, MLA_Q_LORA)), _const_spec((MLA_Q_LORA, 768)),
                _const_spec((1, MLA_KV_LORA)), _const_spec((MLA_KV_LORA, 1024)),
                _const_spec((1, 512)), _const_spec((1, LANES)), _const_spec((512, 512))]
    args = [x, mod3, lw["g_pre1"], lw["w_in"], lw["g_mla_q"], lw["w_uq"], lw["g_mla_kv"],
            lw["w_ukv"], lw["g_gqa_q"], lw["g_gqa_k"], lw["bd"]]
    if latent:
        tab_spec = pl.BlockSpec((TM, LANES), lambda i: (i % blk_per_seq, 0))
        in_specs += [tab_spec] * 4
        args += list(tabs)
    widths = [768, KVM_W, 512, 256, 512, 1024, 1024, 512, 4 * D]
    out_specs = [tok(w) for w in widths]
    out_shape = [jax.ShapeDtypeStruct((n_tok, w), BF16) for w in widths]
    if not latent:
        cw = [MLA_KV_LORA, MLA_ROPE, 128, 128, 512, 512]
        out_specs += [tok(w) for w in cw]
        out_shape += [jax.ShapeDtypeStruct((n_tok, w), F32) for w in cw]
    return pl.pallas_call(
        functools.partial(_inprep_kernel, latent),
        grid=(nblk,),
        in_specs=in_specs, out_specs=out_specs, out_shape=out_shape,
        compiler_params=_cparams(("arbitrary",)),
        name="inprep_lat" if latent else "inprep_ctx",
    )(*args)


def _pastkv_kernel(ckv_ref, kpe_ref, wukv_ref, o_ref):
    kv = _dot(ckv_ref[...].astype(BF16), wukv_ref[...])
    kpe_b = kpe_ref[...].astype(BF16)
    for p in range(4):
        o_ref[:, p * 256:p * 256 + LANES] = kv[:, p * LANES:(p + 1) * LANES].astype(BF16)
        o_ref[:, p * 256 + LANES:(p + 1) * 256] = kpe_b
    o_ref[:, 1024:1536] = kv[:, 512:1024].astype(BF16)


def _pastkv_call(ckv, kpe4, w_ukv):
    n = ckv.shape[0]
    return pl.pallas_call(
        _pastkv_kernel,
        grid=(n // TM,),
        in_specs=[pl.BlockSpec((TM, MLA_KV_LORA), lambda i: (i, 0)),
                  pl.BlockSpec((TM, LANES), lambda i: (i, 0)),
                  _const_spec((MLA_KV_LORA, 1024))],
        out_specs=pl.BlockSpec((TM, KVM_W), lambda i: (i, 0)),
        out_shape=jax.ShapeDtypeStruct((n, KVM_W), BF16),
        compiler_params=_cparams(("arbitrary",)),
        name="pastkv",
    )(ckv, kpe4, w_ukv)


def _softmax_pv(s, v):
    m = jnp.max(s, axis=-1, keepdims=True)
    p = jnp.exp(s - m)
    l = jnp.sum(p, axis=-1, keepdims=True)
    return _dot(p.astype(BF16), v) / l


def _attn_kernel(lam_init, qm_ref, kvm_ref, gq_ref, gkv_ref, dq_ref, dkv_ref,
                 lam_ref, gdiff_ref, o_ref):
    tq = qm_ref.shape[0]
    lane = _lane_iota((tq, LANES))
    low = lane < HALF_LANES
    zero = jnp.zeros((tq, LANES), BF16)

    for p in range(MLA_HEADS // 2):
        qn = qm_ref[:, p * LANES:(p + 1) * LANES]
        g = p // 2
        qpe = qm_ref[:, 512 + g * LANES:512 + (g + 1) * LANES]
        kk = kvm_ref[:, p * 256:(p + 1) * 256]
        vv = kvm_ref[:, 1024 + p * LANES:1024 + (p + 1) * LANES]
        outs = []
        for half in range(2):
            h = 2 * p + half
            slot = h % 4
            in_slot = (lane >= slot * MLA_ROPE) & (lane < (slot + 1) * MLA_ROPE)
            lhs = jnp.concatenate(
                [jnp.where(low if half == 0 else ~low, qn, zero),
                 jnp.where(in_slot, qpe, zero)], axis=1)
            outs.append(_softmax_pv(_dot_nt(lhs, kk), vv))
        o_ref[:, p * LANES:(p + 1) * LANES] = jnp.where(low, outs[0], outs[1]).astype(BF16)

    kk = gkv_ref[:, 0:LANES]
    vv = gkv_ref[:, LANES:2 * LANES]
    for g in range(GQA_HEADS // 2):
        qg = gq_ref[:, g * LANES:(g + 1) * LANES]
        o_lo = _softmax_pv(_dot_nt(jnp.where(low, qg, zero), kk), vv)
        o_hi = _softmax_pv(_dot_nt(jnp.where(low, zero, qg), kk), vv)
        o_ref[:, 512 + g * LANES:512 + (g + 1) * LANES] = jnp.where(low, o_lo, o_hi).astype(BF16)

    lp = lam_ref[...]
    lam = (jnp.exp(jnp.sum(lp[0:1] * lp[1:2], axis=-1, keepdims=True))
           - jnp.exp(jnp.sum(lp[2:3] * lp[3:4], axis=-1, keepdims=True)) + lam_init)
    for h in range(DIFF_HEADS):
        qh = dq_ref[:, h * LANES:(h + 1) * LANES]
        kk = dkv_ref[:, h * LANES:(h + 1) * LANES]
        vv = dkv_ref[:, 512 + h * LANES:512 + (h + 1) * LANES]
        a1 = _softmax_pv(_dot_nt(jnp.where(low, qh, zero), kk), vv)
        a2 = _softmax_pv(_dot_nt(jnp.where(low, zero, qh), kk), vv)
        od = _rms(a1 - lam * a2, gdiff_ref[...]) * (1.0 - lam_init)
        o_ref[:, 1024 + h * LANES:1024 + (h + 1) * LANES] = od.astype(BF16)


def _attn_call(lam_init, qm, kvm, gq, gkv, dq, dkv, lam_p, g_diff, n_b, t_len, s_len):
    nq = t_len // TQ

    def qspec(w):
        return pl.BlockSpec((TQ, w), lambda b, i: (b * nq + i, 0))

    def kspec(w):
        return pl.BlockSpec((s_len, w), lambda b, i: (b, 0))

    return pl.pallas_call(
        functools.partial(_attn_kernel, lam_init),
        grid=(n_b, nq),
        in_specs=[qspec(768), kspec(KVM_W), qspec(512), kspec(256), qspec(512), kspec(1024),
                  _const_spec((4, DIFF_D)), _const_spec((1, DIFF_DV))],
        out_specs=qspec(3 * BRANCH_W),
        out_shape=jax.ShapeDtypeStruct((n_b * t_len, 3 * BRANCH_W), BF16),
        compiler_params=_cparams(("arbitrary", "arbitrary")),
        name="attn",
    )(qm, kvm, gq, gkv, dq, dkv, lam_p, g_diff)


def _log_sigmoid(x):
    return jnp.minimum(x, 0.0) - jnp.log(1.0 + jnp.exp(-jnp.abs(x)))


def _ret_kernel(latent, t_len, dec_ref, q_ref, k_ref, v_ref, rg_ref, gret_ref, *refs):
    if latent:
        s0_ref, o_ref = refs
    else:
        o_ref, st_ref = refs
    tq = q_ref.shape[0]
    t0 = pl.program_id(1) * tq
    lane = _lane_iota((tq, LANES))
    low = lane < HALF_LANES
    zero = jnp.zeros((tq, LANES), BF16)
    t_idx = (t0 + lax.broadcasted_iota(jnp.int32, (tq, t_len), 0)).astype(F32)
    s_idx = lax.broadcasted_iota(jnp.int32, (tq, t_len), 1).astype(F32)
    dist = t_idx - s_idx
    t_col = (t0 + lax.broadcasted_iota(jnp.int32, (tq, 1), 0)).astype(F32)

    def lg(d, h):
        return _log_sigmoid(jnp.full((1, 1), dec_ref[d, h], F32))

    for h in range(RET_HEADS):
        p, half = h // 2, h % 2
        qp = q_ref[:, p * LANES:(p + 1) * LANES]
        qm = jnp.where(low if half == 0 else ~low, qp, zero)
        kp = k_ref[:, p * LANES:(p + 1) * LANES]
        vh = v_ref[:, h * LANES:(h + 1) * LANES]
        lgf, lgb = lg(0, h), lg(1, h)
        dmask = (jnp.where(dist >= 0, jnp.exp(lgf * jnp.maximum(dist, 0.0)), 0.0)
                 + jnp.where(dist <= 0, jnp.exp(lgb * jnp.maximum(-dist, 0.0)), 0.0))
        o = _dot((_dot_nt(qm, kp) * dmask).astype(BF16), vh)
        if latent:
            sf = s0_ref[0, p].astype(BF16)
            sb = s0_ref[1, p].astype(BF16)
            o = o + _dot(qm, sf) * jnp.exp(lgf * (t_col + 1.0))
            o = o + _dot(qm, sb) * jnp.exp(lgb * (float(t_len) - t_col))
        mu = jnp.mean(o, axis=-1, keepdims=True)
        oc = o - mu
        y = oc * lax.rsqrt(jnp.mean(oc * oc, axis=-1, keepdims=True) + EPS)
        y = y * gret_ref[:, h * LANES:(h + 1) * LANES]
        rg = rg_ref[:, h * LANES:(h + 1) * LANES].astype(F32)
        o_ref[:, h * LANES:(h + 1) * LANES] = (y * _silu(rg)).astype(BF16)

    if not latent:
        s_col = lax.broadcasted_iota(jnp.int32, (t_len, 1), 0).astype(F32)
        lane_t = _lane_iota((1, LANES)) < HALF_LANES
        for p in range(RET_HEADS // 2):
            kp = k_ref[:, p * LANES:(p + 1) * LANES].astype(F32)
            for d in range(2):
                lg_lane = jnp.where(lane_t, lg(d, 2 * p), lg(d, 2 * p + 1))
                expo = (float(t_len) - 1.0 - s_col) if d == 0 else s_col
                kdec_t = jnp.transpose(kp * jnp.exp(lg_lane * expo)).astype(BF16)
                for half in range(2):
                    h = 2 * p + half
                    st = _dot(kdec_t, v_ref[:, h * LANES:(h + 1) * LANES])
                    st_ref[d, h] = st[half * RET_DK:(half + 1) * RET_DK, :]


def _ret_call(latent, dec, ret, rg, g_ret, s0, n_b, t_len):
    nq = t_len // TQ
    assert latent or nq == 1
    in_specs = [pl.BlockSpec(memory_space=pltpu.SMEM),
                pl.BlockSpec((TQ, 256), lambda b, i: (b * nq + i, 0)),
                pl.BlockSpec((t_len, 256), lambda b, i: (b, 1)),
                pl.BlockSpec((t_len, 512), lambda b, i: (b, 1)),
                pl.BlockSpec((TQ, 512), lambda b, i: (b * nq + i, 0)),
                _const_spec((1, 512))]
    args = [dec, ret, ret, ret, rg, g_ret]
    out_specs = [pl.BlockSpec((TQ, 512), lambda b, i: (b * nq + i, 0))]
    out_shape = [jax.ShapeDtypeStruct((n_b * t_len, 512), BF16)]
    if latent:
        in_specs.append(pl.BlockSpec((None, 2, 2, LANES, LANES), lambda b, i: (b, 0, 0, 0, 0)))
        args.append(s0)
    else:
        out_specs.append(pl.BlockSpec((None, 2, RET_HEADS, RET_DK, RET_DV),
                                      lambda b, i: (b, 0, 0, 0, 0)))
        out_shape.append(jax.ShapeDtypeStruct((n_b, 2, RET_HEADS, RET_DK, RET_DV), F32))
    return pl.pallas_call(
        functools.partial(_ret_kernel, latent, t_len),
        grid=(n_b, nq),
        in_specs=in_specs, out_specs=out_specs, out_shape=out_shape,
        compiler_params=_cparams(("arbitrary", "arbitrary")),
        name="ret_lat" if latent else "ret_ctx",
    )(*args)


def _merge_kernel(x_ref, mod_ref, br_ref, or_ref, gate_ref, wbr_ref, wout_ref, gpost_ref, o_ref):
    merged = None
    for n in range(N_BRANCH):
        if n < 2:
            b = br_ref[:, n * BRANCH_W:(n + 1) * BRANCH_W]
        elif n == 2:
            b = or_ref[...]
        else:
            b = br_ref[:, 2 * BRANCH_W:3 * BRANCH_W]
        t = gate_ref[:, n * D:(n + 1) * D].astype(F32) * _dot(b, wbr_ref[n])
        merged = t if merged is None else merged + t
    out = _dot(merged.astype(BF16), wout_ref[...])
    g1 = mod_ref[...][:, 2 * D:3 * D]
    o_ref[...] = x_ref[...] + g1 * _rms(out, gpost_ref[...])


def _merge_call(x, mod3, mod_row, br, o_r, gates, lw):
    n_tok = x.shape[0]

    def tok(w):
        return pl.BlockSpec((TM, w), lambda i: (i, 0))

    return pl.pallas_call(
        _merge_kernel,
        grid=(n_tok // TM,),
        in_specs=[tok(D), pl.BlockSpec((None, 1, 6 * D), lambda i: (mod_row(i), 0, 0)),
                  tok(3 * BRANCH_W), tok(BRANCH_W), tok(4 * D),
                  _const_spec((N_BRANCH, BRANCH_W, D)), _const_spec((D, D)), _const_spec((1, D))],
        out_specs=tok(D),
        out_shape=jax.ShapeDtypeStruct((n_tok, D), F32),
        compiler_params=_cparams(("arbitrary",)),
        name="merge",
    )(x, mod3, br, o_r, gates, lw["w_br"], lw["w_out"], lw["g_post1"])


def _route(logits_t, bias):
    n = logits_t.shape[1]
    scores = jax.nn.sigmoid(logits_t)
    sel = scores + bias
    neg = -jnp.inf
    sub = lax.broadcasted_iota(jnp.int32, (GROUP_SIZE, n), 0)
    grp = []
    for g in range(N_GROUPS):
        blk = sel[g * GROUP_SIZE:(g + 1) * GROUP_SIZE]
        m1 = jnp.max(blk, axis=0, keepdims=True)
        i1 = jnp.min(jnp.where(blk == m1, sub, GROUP_SIZE), axis=0, keepdims=True)
        m2 = jnp.max(jnp.where(sub == i1, neg, blk), axis=0, keepdims=True)
        grp.append(m1 + m2)
    parts = []
    for g in range(N_GROUPS):
        beaten = jnp.zeros((1, n), jnp.int32)
        for o in range(N_GROUPS):
            if o == g:
                continue
            wins = (grp[o] > grp[g]) | (grp[o] == grp[g]) if o < g else (grp[o] > grp[g])
            beaten = beaten + wins.astype(jnp.int32)
        keep = beaten < TOPK_GROUPS
        parts.append(jnp.where(keep, sel[g * GROUP_SIZE:(g + 1) * GROUP_SIZE], neg))
    cur = jnp.concatenate(parts, axis=0)
    eidx = lax.broadcasted_iota(jnp.int32, (N_EXPERTS, n), 0)
    hits, ids, ws = [], [], []
    for _ in range(TOP_K):
        m = jnp.max(cur, axis=0, keepdims=True)
        i = jnp.min(jnp.where(cur == m, eidx, N_EXPERTS), axis=0, keepdims=True)
        hit = eidx == i
        hits.append(hit)
        ids.append(i)
        ws.append(jnp.sum(jnp.where(hit, scores, 0.0), axis=0, keepdims=True))
        cur = jnp.where(hit, neg, cur)
    wsum = ws[0] + ws[1] + ws[2] + ws[3]
    return hits, ids, [w / wsum * ROUTE_SCALE for w in ws]


U32 = jnp.uint32
HIGH16 = np.uint32(0xFFFF0000)


def _bf16_bits(v):
    return lax.bitcast_convert_type(v.astype(BF16).astype(F32), U32)


def _pack_rows(v):
    return (_bf16_bits(v[:, 0:D // 2]) >> 16) | _bf16_bits(v[:, D // 2:D])


def _unpack_rows(p):
    lo = lax.bitcast_convert_type(p << 16, F32)
    hi = lax.bitcast_convert_type(p & HIGH16, F32)
    return jnp.concatenate([lo, hi], axis=1)


def _moe_pre_kernel(x_ref, mod_ref, gpre_ref, wr_ref, br_ref, tri_ref,
                    hp_ref, eidx_ref, rank_ref, comb_ref, cnt_ref, run_ref):
    tm = x_ref.shape[0]

    @pl.when(pl.program_id(0) == 0)
    def _():
        run_ref[...] = jnp.zeros_like(run_ref)

    mod = mod_ref[...]
    sh2, sc2 = mod[:, 3 * D:4 * D], mod[:, 4 * D:5 * D]
    h = _rms(x_ref[...], gpre_ref[...]) * (1.0 + sc2) + sh2
    hp_ref[...] = _pack_rows(h)
    hb = h.astype(BF16)
    h_lo = (h - hb.astype(F32)).astype(BF16)
    wr = wr_ref[...]
    wr_hi = wr.astype(BF16)
    wr_lo = (wr - wr_hi.astype(F32)).astype(BF16)
    logits_t = _dot_nt(wr_hi, hb) + _dot_nt(wr_hi, h_lo) + _dot_nt(wr_lo, hb)
    hits, ids, ws = _route(logits_t, br_ref[...])

    picked = jnp.zeros((N_EXPERTS, tm), F32)
    for hit in hits:
        picked = jnp.where(hit, 1.0, picked)
    before = _dot(picked.astype(BF16), tri_ref[...]) + run_ref[:, 0:1]
    sub8 = lax.broadcasted_iota(jnp.int32, (8, tm), 0)
    comb8 = jnp.zeros((8, tm), F32)
    for k in range(TOP_K):
        rank = jnp.sum(jnp.where(hits[k], before, 0.0), axis=0, keepdims=True)
        eidx_ref[k:k + 1, :] = ids[k]
        rank_ref[k:k + 1, :] = rank.astype(jnp.int32)
        comb8 = jnp.where(sub8 == k, ws[k], comb8)
    comb_ref[...] = jnp.transpose(
        jnp.concatenate([comb8, jnp.zeros((LANES - 8, tm), F32)], axis=0))
    run_ref[...] = run_ref[...] + jnp.sum(picked, axis=1, keepdims=True)
    cnt_ref[...] = run_ref[...]


def _moe_pre_call(x, mod3, mod_row, lw):
    n_tok = x.shape[0]
    tm = TM_MOE_PRE
    tri = np.arange(tm)
    tri = jnp.asarray(tri[:, None] < tri[None, :], BF16)
    row4 = pl.BlockSpec((TOP_K, tm), lambda i: (0, i))
    return pl.pallas_call(
        _moe_pre_kernel,
        grid=(n_tok // tm,),
        in_specs=[pl.BlockSpec((tm, D), lambda i: (i, 0)),
                  pl.BlockSpec((None, 1, 6 * D), lambda i: (mod_row(i), 0, 0)),
                  _const_spec((1, D)), _const_spec((N_EXPERTS, D)), _const_spec((N_EXPERTS, 1)),
                  _const_spec((tm, tm))],
        out_specs=[pl.BlockSpec((tm, D // 2), lambda i: (i, 0)), row4, row4,
                   pl.BlockSpec((tm, LANES), lambda i: (i, 0)),
                   _const_spec((N_EXPERTS, LANES))],
        out_shape=[jax.ShapeDtypeStruct((n_tok, D // 2), U32),
                   jax.ShapeDtypeStruct((TOP_K, n_tok), jnp.int32),
                   jax.ShapeDtypeStruct((TOP_K, n_tok), jnp.int32),
                   jax.ShapeDtypeStruct((n_tok, LANES), F32),
                   jax.ShapeDtypeStruct((N_EXPERTS, LANES), F32)],
        scratch_shapes=[pltpu.VMEM((N_EXPERTS, LANES), F32)],
        compiler_params=_cparams(("arbitrary",)),
        name="moe_pre",
    )(x, mod3, lw["g_pre2"], lw["w_router_t"], lw["b_router"], tri)


def _moe_plan_kernel(eidx_ref, rank_ref, cnt_ref, dest_ref, te_ref, tv_ref, tn_ref):
    tm = eidx_ref.shape[1]
    cnt = cnt_ref[...]
    padded = jnp.ceil(cnt * (1.0 / TMX)) * TMX
    row = lax.broadcasted_iota(jnp.int32, cnt.shape, 0)
    incl = padded
    shift = 1
    while shift < N_EXPERTS:
        incl = incl + jnp.where(row >= shift, pltpu.roll(incl, shift, 0), 0.0)
        shift *= 2
    start = (incl - padded)[:, 0:1]
    end = incl[:, 0:1]
    erow = lax.broadcasted_iota(jnp.int32, (N_EXPERTS, tm), 0)
    for k in range(TOP_K):
        mine = erow == eidx_ref[k:k + 1, :]
        base = jnp.sum(jnp.where(mine, start, 0.0), axis=0, keepdims=True)
        dest_ref[k:k + 1, :] = rank_ref[k:k + 1, :] + base.astype(jnp.int32)

    @pl.when(pl.program_id(0) == 0)
    def _():
        tile0 = (_lane_iota((1, LANES)) * TMX).astype(F32)
        owner = jnp.sum(jnp.where(end <= tile0, 1.0, 0.0), axis=0, keepdims=True)
        owner = jnp.minimum(owner, N_EXPERTS - 1.0)
        erow_t = lax.broadcasted_iota(jnp.int32, (N_EXPERTS, LANES), 0).astype(F32)
        left = jnp.sum(jnp.where(erow_t == owner, cnt[:, 0:1] - (tile0 - start), 0.0),
                       axis=0, keepdims=True)
        te_ref[...] = owner.astype(jnp.int32)
        tv_ref[...] = jnp.clip(left, 0.0, float(TMX)).astype(jnp.int32)
        tn_ref[...] = jnp.full(tn_ref.shape, N_EXPERTS, jnp.int32)
        nxt = owner
        for k in range(W_SLOTS - 1):
            later = (erow_t > nxt) & (cnt[:, 0:1] > 0.0)
            nxt = jnp.min(jnp.where(later, erow_t, float(N_EXPERTS)), axis=0, keepdims=True)
            tn_ref[k:k + 1, :] = nxt.astype(jnp.int32)


def _moe_plan_call(eidx, rank, cnt):
    n_tok = eidx.shape[1]
    tm = TM_MOE_PRE
    row4 = pl.BlockSpec((TOP_K, tm), lambda i: (0, i))
    tiles = jax.ShapeDtypeStruct((1, LANES), jnp.int32)
    return pl.pallas_call(
        _moe_plan_kernel,
        grid=(n_tok // tm,),
        in_specs=[row4, row4, _const_spec((N_EXPERTS, LANES))],
        out_specs=[row4, _const_spec((1, LANES)), _const_spec((1, LANES)), _const_spec((8, LANES))],
        out_shape=[jax.ShapeDtypeStruct((TOP_K, n_tok), jnp.int32), tiles, tiles,
                   jax.ShapeDtypeStruct((8, LANES), jnp.int32)],
        compiler_params=_cparams(("arbitrary",)),
        name="moe_plan",
    )(eidx, rank, cnt)


def _experts_kernel(l, te_ref, tv_ref, tn_ref, xs_ref, wgu_hbm, wdn_hbm, ys_ref,
                    wgu_f, wdn_f, wgu_b, wdn_b, sem, group_ref):
    j = pl.program_id(0)
    valid = tv_ref[j]
    expert = te_ref[j]

    def fetch(e, slot):
        return (pltpu.make_async_copy(wgu_hbm.at[l, e], wgu_f.at[slot], sem.at[slot, 0]),
                pltpu.make_async_copy(wdn_hbm.at[l, e], wdn_f.at[slot], sem.at[slot, 1]))

    def start_if_any(e, slot):
        @pl.when(e < N_EXPERTS)
        def _():
            for cp in fetch(e, slot):
                cp.start()

    @pl.when(j == 0)
    def _():
        group_ref[0] = 0
        start_if_any(expert, 0)
        for k in range(W_SLOTS - 2):
            start_if_any(tn_ref[k, 0], k + 1)

    first_tile = (j == 0) | (expert != te_ref[jnp.maximum(j - 1, 0)])

    @pl.when(first_tile & (valid > 0))
    def _():
        group = group_ref[0]
        slot = lax.rem(group, W_SLOTS)
        for cp in fetch(expert, slot):
            cp.wait()
        wgu_b[...] = wgu_f[slot].astype(BF16)
        wdn_b[...] = wdn_f[slot].astype(BF16)
        start_if_any(tn_ref[W_SLOTS - 2, j], lax.rem(group + W_SLOTS - 1, W_SLOTS))
        group_ref[0] = group + 1

    @pl.when(valid > 0)
    def _():
        rows = lax.broadcasted_iota(jnp.int32, (TMX, D), 0)
        x = jnp.where(rows < valid, _unpack_rows(xs_ref[...]), 0.0).astype(BF16)
        gu = _dot(x, wgu_b[...])
        a = _silu(gu[:, 0:EXPERT_FF]) * gu[:, EXPERT_FF:2 * EXPERT_FF]
        ys_ref[...] = _pack_rows(_dot(a.astype(BF16), wdn_b[...]))

    @pl.when(valid <= 0)
    def _():
        ys_ref[...] = jnp.zeros_like(ys_ref)


def _experts_call(l, xs, te, tv, tn, w_gu, w_dn):
    n_tiles = xs.shape[0] // TMX
    grid_spec = pltpu.PrefetchScalarGridSpec(
        num_scalar_prefetch=3,
        grid=(n_tiles,),
        in_specs=[pl.BlockSpec((TMX, D // 2), lambda j, *_: (j, 0)),
                  pl.BlockSpec(memory_space=pl.ANY), pl.BlockSpec(memory_space=pl.ANY)],
        out_specs=pl.BlockSpec((TMX, D // 2), lambda j, *_: (j, 0)),
        scratch_shapes=[pltpu.VMEM((W_SLOTS, D, 2 * EXPERT_FF), F32),
                        pltpu.VMEM((W_SLOTS, EXPERT_FF, D), F32),
                        pltpu.VMEM((D, 2 * EXPERT_FF), BF16), pltpu.VMEM((EXPERT_FF, D), BF16),
                        pltpu.SemaphoreType.DMA((W_SLOTS, 2)), pltpu.SMEM((1,), jnp.int32)])
    return pl.pallas_call(
        functools.partial(_experts_kernel, l),
        grid_spec=grid_spec,
        out_shape=jax.ShapeDtypeStruct(xs.shape, U32),
        compiler_params=_cparams(("arbitrary",)),
        name="moe_experts",
    )(te, tv, tn, xs, w_gu, w_dn)


def _moe_post_kernel(x_ref, mod_ref, hp_ref, yg_ref, comb_ref, wsgu_ref, wsdn_ref, gpost_ref,
                     o_ref):
    hb = _unpack_rows(hp_ref[...]).astype(BF16)
    sgu = _dot(hb, wsgu_ref[...])
    sa = _silu(sgu[:, 0:SHARED_FF]) * sgu[:, SHARED_FF:2 * SHARED_FF]
    acc = _dot(sa.astype(BF16), wsdn_ref[...])
    comb = comb_ref[...]
    for k in range(TOP_K):
        acc = acc + comb[:, k:k + 1] * _unpack_rows(yg_ref[k])
    g2 = mod_ref[...][:, 5 * D:6 * D]
    o_ref[...] = x_ref[...] + g2 * _rms(acc, gpost_ref[...])


def _moe_post_call(x, mod3, mod_row, hp, yg, comb, lw):
    n_tok = x.shape[0]
    tm = TM_MOE_PRE
    return pl.pallas_call(
        _moe_post_kernel,
        grid=(n_tok // tm,),
        in_specs=[pl.BlockSpec((tm, D), lambda i: (i, 0)),
                  pl.BlockSpec((None, 1, 6 * D), lambda i: (mod_row(i), 0, 0)),
                  pl.BlockSpec((tm, D // 2), lambda i: (i, 0)),
                  pl.BlockSpec((TOP_K, tm, D // 2), lambda i: (0, i, 0)),
                  pl.BlockSpec((tm, LANES), lambda i: (i, 0)),
                  _const_spec((D, 2 * SHARED_FF)), _const_spec((SHARED_FF, D)),
                  _const_spec((1, D))],
        out_specs=pl.BlockSpec((tm, D), lambda i: (i, 0)),
        out_shape=jax.ShapeDtypeStruct((n_tok, D), F32),
        compiler_params=_cparams(("arbitrary",)),
        name="moe_post",
    )(x, mod3, hp, yg, comb, lw["w_sh_gu"], lw["w_sh_down"], lw["g_post2"])


def _moe_call(l, x, mod3, mod_row, lw):
    n_tok = x.shape[0]
    n_slots = -(-(TOP_K * n_tok + N_EXPERTS * (TMX - 1)) // TMX) * TMX
    assert n_slots // TMX <= LANES
    hp, eidx, rank, comb, cnt = _moe_pre_call(x, mod3, mod_row, lw)
    dest, te, tv, tn = _moe_plan_call(eidx, rank, cnt)
    dest = dest.reshape(TOP_K * n_tok)
    xs = _sc_scatter_rows(hp, dest, n_slots)
    ys = _experts_call(l, xs, te[0], tv[0], tn, lw["w_exp_gu"], lw["w_exp_down"])
    yg = _sc_gather_rows(ys, dest).reshape(TOP_K, n_tok, D // 2)
    return _moe_post_call(x, mod3, mod_row, hp, yg, comb, lw)


SC_CORES, SC_SUBCORES = 2, 16
SC_WORKERS = SC_CORES * SC_SUBCORES


def _sc_gather_rows(table, idx, chunk=64):
    n_out, width = idx.shape[0], table.shape[1]
    per_worker = n_out // SC_WORKERS
    n_chunks = per_worker // chunk
    assert per_worker * SC_WORKERS == n_out and n_chunks * chunk == per_worker
    mesh = plsc.VectorSubcoreMesh(core_axis_name="c", subcore_axis_name="s",
                                  num_cores=SC_CORES, num_subcores=SC_SUBCORES)

    @functools.partial(
        pl.kernel, mesh=mesh,
        out_type=jax.ShapeDtypeStruct((n_out, width), table.dtype),
        scratch_types=[pltpu.VMEM((chunk,), jnp.int32), pltpu.VMEM((chunk, width), table.dtype),
                       pltpu.SemaphoreType.DMA],
        name="sc_gather")
    def gather(table_hbm, idx_hbm, out_hbm, idx_v, rows_v, sem):
        base = (lax.axis_index("s") * SC_CORES + lax.axis_index("c")) * per_worker

        @pl.loop(0, n_chunks)
        def _(j):
            off = base + j * chunk
            pltpu.sync_copy(idx_hbm.at[pl.ds(off, chunk)], idx_v)
            pltpu.async_copy(table_hbm.at[idx_v], rows_v, sem).wait()
            pltpu.sync_copy(rows_v, out_hbm.at[pl.ds(off, chunk)])

    return gather(table, idx)


def _sc_scatter_rows(rows, dest, n_slots, chunk=64):
    n_tok, width = rows.shape
    per_worker = n_tok // SC_WORKERS
    n_chunks = per_worker // chunk
    assert per_worker * SC_WORKERS == n_tok and n_chunks * chunk == per_worker
    mesh = plsc.VectorSubcoreMesh(core_axis_name="c", subcore_axis_name="s",
                                  num_cores=SC_CORES, num_subcores=SC_SUBCORES)

    @functools.partial(
        pl.kernel, mesh=mesh,
        out_type=jax.ShapeDtypeStruct((n_slots, width), rows.dtype),
        scratch_types=[pltpu.VMEM((chunk,), jnp.int32), pltpu.VMEM((chunk, width), rows.dtype)],
        name="sc_scatter")
    def scatter(rows_hbm, dest_hbm, out_hbm, idx_v, rows_v):
        base = (lax.axis_index("s") * SC_CORES + lax.axis_index("c")) * per_worker

        @pl.loop(0, n_chunks)
        def _(j):
            off = base + j * chunk
            pltpu.sync_copy(rows_hbm.at[pl.ds(off, chunk)], rows_v)
            for k in range(TOP_K):
                pltpu.sync_copy(dest_hbm.at[pl.ds(k * n_tok + off, chunk)], idx_v)
                pltpu.sync_copy(rows_v, out_hbm.at[idx_v])

    return scatter(rows, dest)


def _rope_tables(t_len):
    pos = np.arange(t_len)
    row, col = pos // GRID_W, pos % GRID_W

    def tab(r):
        half = r // 2
        freq = ROPE_BASE ** (-np.arange(half, dtype=np.float64) / half)
        sign = np.concatenate([-np.ones(half), np.ones(half)])
        cs, sn = [], []
        for p in (row, col):
            ang = p[:, None].astype(np.float64) * freq[None, :]
            cs.append(np.concatenate([np.cos(ang), np.cos(ang)], axis=1))
            sn.append(np.concatenate([np.sin(ang), np.sin(ang)], axis=1) * sign[None, :])
        return np.concatenate(cs, axis=1), np.concatenate(sn, axis=1)

    c64, s64 = tab(GQA_HD // 2)
    cpe, spe = tab(MLA_ROPE // 2)
    out = (np.tile(c64, (1, 2)), np.tile(s64, (1, 2)), np.tile(cpe, (1, 4)), np.tile(spe, (1, 4)))
    return tuple(jnp.asarray(a, F32) for a in out)


def _layer_weights(l, p):
    w_uq = p["w_mla_uq"][l].reshape(MLA_Q_LORA, MLA_HEADS, MLA_NOPE + MLA_ROPE)
    w_uq = jnp.concatenate([w_uq[:, :, :MLA_NOPE].reshape(MLA_Q_LORA, -1),
                            w_uq[:, :, MLA_NOPE:].reshape(MLA_Q_LORA, -1)], axis=1)
    w_ukv = p["w_mla_ukv"][l].reshape(MLA_KV_LORA, MLA_HEADS, MLA_NOPE + MLA_V)
    w_ukv = jnp.concatenate([w_ukv[:, :, :MLA_NOPE].reshape(MLA_KV_LORA, -1),
                             w_ukv[:, :, MLA_NOPE:].reshape(MLA_KV_LORA, -1)], axis=1)
    w_br = p["w_br"][l]
    w_br_gqa = w_br[1].reshape(GQA_HEADS, GQA_HD, D)[jnp.array(GQA_ORDER)].reshape(BRANCH_W, D)
    w_br = jnp.stack([w_br[0], w_br_gqa, w_br[2], w_br[3]], axis=0)
    blk = np.arange(512) // GQA_HD
    return {
        "g_pre1": p["g_pre1"][l].reshape(1, D), "g_post1": p["g_post1"][l].reshape(1, D),
        "g_pre2": p["g_pre2"][l].reshape(1, D), "g_post2": p["g_post2"][l].reshape(1, D),
        "w_in": p["w_in_packed"],
        "g_mla_q": p["g_mla_q"][l].reshape(1, -1), "w_uq": w_uq.astype(BF16),
        "g_mla_kv": p["g_mla_kv"][l].reshape(1, -1), "w_ukv": w_ukv.astype(BF16),
        "g_gqa_q": jnp.tile(p["g_gqa_q"][l], GQA_HEADS).reshape(1, -1),
        "g_gqa_k": jnp.tile(p["g_gqa_k"][l], GQA_KV_HEADS).reshape(1, -1),
        "bd": jnp.asarray(blk[:, None] == blk[None, :], BF16),
        "ret_decay": p["ret_decay"][l],
        "g_ret": p["g_ret"][l].reshape(1, -1),
        "diff_lambda": p["diff_lambda"][l], "g_diff": p["g_diff"][l].reshape(1, -1),
        "w_br": w_br.astype(BF16), "w_out": p["w_out"][l].astype(BF16),
        "w_router_t": p["w_router"][l].T, "b_router": p["b_router"][l].reshape(-1, 1),
        "w_exp_gu": p["w_exp_gu"], "w_exp_down": p["w_exp_down"],
        "w_sh_gu": p["w_sh_gu"][l].astype(BF16), "w_sh_down": p["w_sh_down"][l].astype(BF16),
    }


def _mixers(latent, l, x, mod3, mod_row, lw, n_b, t_len, tabs=None, past=None, s0=None):
    lam_init = 0.8 - 0.6 * math.exp(-0.3 * l)
    outs = _inprep_call(latent, l, x, mod3, mod_row, lw, tabs, t_len)
    qm, kvm, gq, gkv, dq, dkv, ret, rg, gates = outs[:9]
    s_len = t_len
    if latent:
        past_kvm, past_gkv, past_dkv = past
        p_len = past_gkv.shape[1]
        s_len = p_len + t_len

        def cat(a, b):
            return jnp.concatenate([a, b.reshape(n_b, t_len, -1)], axis=1).reshape(n_b * s_len, -1)

        kvm, gkv, dkv = cat(past_kvm, kvm), cat(past_gkv, gkv), cat(past_dkv, dkv)
    br = _attn_call(lam_init, qm, kvm, gq, gkv, dq, dkv, lw["diff_lambda"], lw["g_diff"],
                    n_b, t_len, s_len)
    r = _ret_call(latent, lw["ret_decay"], ret, rg, lw["g_ret"], s0, n_b, t_len)
    y = _merge_call(x, mod3, mod_row, br, r[0], gates, lw)
    cache = None if latent else tuple(outs[9:]) + (r[1],)
    return y, cache


def kernel(x_prompt, x_sample, cache_mla_ckv, cache_mla_kpe, cache_gqa_k, cache_gqa_v, cache_diff_k, cache_diff_v, state_ret, c, c_ctx, w_mod, b_mod, g_pre1, g_post1, g_pre2, g_post2, w_in, g_mla_q, w_mla_uq, g_mla_kv, w_mla_ukv, g_gqa_q, g_gqa_k, ret_decay, g_ret, diff_lambda, g_diff, w_br, w_out, w_router, b_router, w_exp_gu, w_exp_down, w_sh_gu, w_sh_down):
    params = dict(w_in_packed=jnp.swapaxes(w_in, 1, 2).astype(BF16), g_pre1=g_pre1, g_post1=g_post1, g_pre2=g_pre2,
                  g_post2=g_post2, g_mla_q=g_mla_q, w_mla_uq=w_mla_uq,
                  g_mla_kv=g_mla_kv, w_mla_ukv=w_mla_ukv, g_gqa_q=g_gqa_q, g_gqa_k=g_gqa_k,
                  ret_decay=ret_decay, g_ret=g_ret, diff_lambda=diff_lambda, g_diff=g_diff,
                  w_br=w_br, w_out=w_out, w_router=w_router, b_router=b_router,
                  w_exp_gu=w_exp_gu, w_exp_down=w_exp_down, w_sh_gu=w_sh_gu, w_sh_down=w_sh_down)
    n_bc, t_c, _ = x_prompt.shape
    n_bl, t_l, _ = x_sample.shape
    p_len = cache_mla_ckv.shape[2]
    tabs = _rope_tables(t_l)
    n_cond = 8
    cond = jnp.concatenate([c_ctx[None, :], c, jnp.zeros((n_cond - 1 - n_bl, D), F32)], axis=0)
    blk_c, blk_l = t_c // TM, t_l // TM
    assert t_l % TM_MOE_PRE == 0 and (t_c * n_bc) % TM_MOE_PRE == 0

    yp = x_prompt.reshape(n_bc * t_c, D)
    ys = x_sample.reshape(n_bl * t_l, D)
    caches = []
    for l in range(DEPTH):
        lw = _layer_weights(l, params)
        mod3 = _mod_call(l, cond, w_mod, b_mod).reshape(n_cond, 1, 6 * D)
        yp, cache = _mixers(False, l, yp, mod3, lambda i: 0, lw, n_bc, t_c)
        yp = _moe_call(l, yp, mod3, lambda i: 0, lw)
        caches.append(cache)
        past_kvm = _pastkv_call(cache_mla_ckv[:, l].reshape(n_bl * p_len, -1),
                                jnp.tile(cache_mla_kpe[:, l].reshape(n_bl * p_len, -1), (1, 4)),
                                lw["w_ukv"]).reshape(n_bl, p_len, -1)
        past_gkv = jnp.concatenate([cache_gqa_k[:, l].reshape(n_bl, p_len, -1),
                                    cache_gqa_v[:, l].reshape(n_bl, p_len, -1)], axis=-1).astype(BF16)
        past_dkv = jnp.concatenate([cache_diff_k[:, l].reshape(n_bl, p_len, -1),
                                    cache_diff_v[:, l].reshape(n_bl, p_len, -1)], axis=-1).astype(BF16)
        s0 = state_ret[:, l].reshape(n_bl, 2, RET_HEADS // 2, 2 * RET_DK, RET_DV)
        ys, _ = _mixers(True, l, ys, mod3, lambda i: 1 + i // blk_l, lw, n_bl, t_l, tabs=tabs,
                        past=(past_kvm, past_gkv, past_dkv), s0=s0)
        ys = _moe_call(l, ys, mod3, lambda i: 1 + i // (t_l // TM_MOE_PRE), lw)

    def stack(k, shape):
        return jnp.stack([caches[l][k].reshape((n_bc, t_c) + shape) for l in range(DEPTH)], axis=1)

    new_ret = jnp.stack([caches[l][6] for l in range(DEPTH)], axis=1)
    return (yp.reshape(n_bc, t_c, D), ys.reshape(n_bl, t_l, D),
            stack(0, (MLA_KV_LORA,)), stack(1, (MLA_ROPE,)),
            stack(2, (GQA_KV_HEADS, GQA_HD)), stack(3, (GQA_KV_HEADS, GQA_HD)),
            stack(4, (DIFF_HEADS, 2, DIFF_D)), stack(5, (DIFF_HEADS, DIFF_DV)), new_ret)
```

```python
import functools
import math

import numpy as np
import jax
import jax.numpy as jnp
from jax import lax
from jax.experimental import pallas as pl
from jax.experimental.pallas import tpu as pltpu
from jax.experimental.pallas import tpu_sc as plsc

F32 = jnp.float32
BF16 = jnp.bfloat16

D = 1024
DEPTH = 2
GRID_W = 64
ROPE_BASE = 10000.0
EPS = 1e-6

MLA_HEADS, MLA_NOPE, MLA_ROPE, MLA_V = 8, 64, 32, 64
MLA_Q_LORA, MLA_KV_LORA = 384, 256
GQA_HEADS, GQA_KV_HEADS, GQA_HD = 8, 2, 64
RET_HEADS, RET_DK, RET_DV = 4, 64, 128
DIFF_HEADS, DIFF_D, DIFF_DV = 4, 64, 128
N_BRANCH, BRANCH_W = 4, 512
N_EXPERTS, TOP_K, N_GROUPS, TOPK_GROUPS = 32, 4, 4, 2
EXPERT_FF, SHARED_FF = 256, 256
ROUTE_SCALE = 2.5
GROUP_SIZE = N_EXPERTS // N_GROUPS

LANES = 128
HALF_LANES = 64
VMEM_LIMIT = 56 * 1024 * 1024

C_CQ, C_CKV, C_KPE, C_GQ, C_GK, C_GV = 0, 384, 640, 768, 1280, 1408
C_DQ, C_DK, C_DV, C_RQ, C_RK, C_RV, C_RG, C_GL, C_END = (
    1536, 2048, 2560, 3072, 3328, 3584, 4096, 4608, 8704)
O_CQ, O_CKV, O_KPE, O_GQ, O_GK, O_GV = 0, 384, 640, 672, 1184, 1312
O_RQ, O_RK, O_RV, O_RG, O_DQ, O_DK, O_DV, O_GL, O_END = (
    1440, 1696, 1952, 2464, 2976, 3488, 4000, 4512, 8608)
GQA_ORDER = (0, 4, 1, 5, 2, 6, 3, 7)

KVM_W = 8 * 256
GKV_W = 3 * LANES
DKV_W = 512 + 4 * 256
LOG2E = 1.4426950408889634
TM = 256
TQ = 256
TM_MOE_PRE = 512
TMX = 256
W_SLOTS = 4


def _cparams(sem):
    return pltpu.CompilerParams(dimension_semantics=sem, vmem_limit_bytes=VMEM_LIMIT)


def _const_spec(shape):
    nd = len(shape)
    return pl.BlockSpec(shape, lambda *_: (0,) * nd)


def _rms(x, g):
    return x * lax.rsqrt(jnp.mean(x * x, axis=-1, keepdims=True) + EPS) * g


def _dot(a, b):
    return jnp.dot(a, b, preferred_element_type=F32)


def _dot_nt(a, b):
    return lax.dot_general(a, b, (((1,), (1,)), ((), ())), preferred_element_type=F32)


def _silu(x):
    return x * jax.nn.sigmoid(x)


def _lane_iota(shape):
    return lax.broadcasted_iota(jnp.int32, shape, len(shape) - 1)


def _seg_meansq(x, bd_ref, width):
    sq = x * x
    hi = sq.astype(BF16)
    lo = (sq - hi.astype(F32)).astype(BF16)
    bd = bd_ref[0:width, 0:width]
    return (_dot(hi, bd) + _dot(lo, bd)) * (1.0 / GQA_HD)


def _rope(x, cos, sin_signed, half):
    width = x.shape[-1]
    first = (_lane_iota(x.shape) % (2 * half)) < half
    partner = jnp.where(first, pltpu.roll(x, width - half, 1), pltpu.roll(x, half, 1))
    return x * cos + partner * sin_signed


def _tile_lanes(t, reps):
    return t if reps == 1 else jnp.concatenate([t] * reps, axis=1)


def _store_kvm(kvm_ref, kv, kpe_b):
    ones = jnp.ones(kpe_b.shape, BF16)
    for p in range(4):
        kvm_ref[:, p * 256:p * 256 + LANES] = kv[:, p * LANES:(p + 1) * LANES].astype(BF16)
        kvm_ref[:, p * 256 + LANES:(p + 1) * 256] = kpe_b
        kvm_ref[:, 1024 + p * 256:1024 + p * 256 + LANES] = (
            kv[:, 512 + p * LANES:512 + (p + 1) * LANES].astype(BF16))
        kvm_ref[:, 1024 + p * 256 + LANES:1024 + (p + 1) * 256] = ones


def _mod_kernel(c_ref, w_ref, b_ref, o_ref):
    a = _silu(c_ref[...]).astype(BF16)
    o_ref[...] = _dot(a, w_ref[...].astype(BF16)) + b_ref[...]


def _mod_call(l, cond, w_mod, b_mod):
    n_l, _, n = w_mod.shape
    tn = 1536
    return pl.pallas_call(
        _mod_kernel,
        grid=(n // tn,),
        in_specs=[_const_spec(cond.shape),
                  pl.BlockSpec((None, D, tn), lambda j: (l, 0, j)),
                  pl.BlockSpec((None, 1, tn), lambda j: (l, 0, j))],
        out_specs=pl.BlockSpec((cond.shape[0], tn), lambda j: (0, j)),
        out_shape=jax.ShapeDtypeStruct((cond.shape[0], n), F32),
        compiler_params=_cparams(("arbitrary",)),
        name="mod",
    )(cond, w_mod, b_mod.reshape(n_l, 1, n))


def _inprep_kernel(latent, *refs):
    (x_ref, mod_ref, gpre_ref, win_ref, gmq_ref, wuq_ref, gmkv_ref, wukv_ref,
     ggq_ref, ggk_ref, bd_ref) = refs[:11]
    refs = refs[11:]
    if latent:
        cos64_ref, sin64_ref, cospe_ref, sinpe_ref = refs[:4]
        refs = refs[4:]
    (qm_ref, kvm_ref, gqo_ref, gkv_ref, dqo_ref, dkv_ref, ret_ref, rg_ref, gate_ref) = refs[:9]
    refs = refs[9:]
    if not latent:
        ckv_o, kpe_o, gk_o, gv_o, dk_o, dv_o = refs

    x = x_ref[...]
    mod = mod_ref[...]
    sh1 = mod[:, 0:D]
    sc1 = mod[:, D:2 * D]
    hb = (_rms(x, gpre_ref[...]) * (1.0 + sc1) + sh1).astype(BF16)

    def z(a, b):
        return _dot_nt(hb, win_ref[a:b, :])

    if latent:
        cos64, sin64 = cos64_ref[...], sin64_ref[...]
        cospe, sinpe = cospe_ref[...], sinpe_ref[...]

    cqn = _rms(z(O_CQ, O_CKV), gmq_ref[...]).astype(BF16)
    q = _dot(cqn, wuq_ref[...]) * ((MLA_NOPE + MLA_ROPE) ** -0.5 * LOG2E)
    q_nope, q_pe = q[:, 0:512], q[:, 512:768]
    if latent:
        q_pe = _rope(q_pe, _tile_lanes(cospe, 2), _tile_lanes(sinpe, 2), MLA_ROPE // 4)
    qm_ref[:, 0:512] = q_nope.astype(BF16)
    qm_ref[:, 512:768] = q_pe.astype(BF16)

    ckvn = _rms(z(O_CKV, O_KPE), gmkv_ref[...])
    kv = _dot(ckvn.astype(BF16), wukv_ref[...])
    kpe4 = _dot_nt(hb, jnp.concatenate([win_ref[O_KPE:O_GQ, :]] * 4, axis=0))
    if latent:
        kpe4 = _rope(kpe4, cospe, sinpe, MLA_ROPE // 4)
    else:
        ckv_o[...] = ckvn
        kpe_o[...] = kpe4[:, 0:MLA_ROPE]
    _store_kvm(kvm_ref, kv, kpe4.astype(BF16))

    gq = _dot_nt(hb, jnp.concatenate(
        [win_ref[O_GQ + h * GQA_HD:O_GQ + (h + 1) * GQA_HD, :] for h in GQA_ORDER], axis=0))
    gq = gq * lax.rsqrt(_seg_meansq(gq, bd_ref, 512) + EPS) * ggq_ref[...]
    gk = z(O_GK, O_GV)
    gk = gk * lax.rsqrt(_seg_meansq(gk, bd_ref, LANES) + EPS) * ggk_ref[...]
    gv = z(O_GV, O_RQ)
    if latent:
        gq = _rope(gq, _tile_lanes(cos64, 4), _tile_lanes(sin64, 4), GQA_HD // 4)
        gk = _rope(gk, cos64, sin64, GQA_HD // 4)
    else:
        gk_o[...] = gk
        gv_o[...] = gv
    gqo_ref[...] = (gq * (GQA_HD ** -0.5 * LOG2E)).astype(BF16)
    gkv_ref[:, 0:LANES] = gk.astype(BF16)
    gkv_ref[:, LANES:2 * LANES] = gv.astype(BF16)
    gkv_ref[:, 2 * LANES:3 * LANES] = jnp.ones(gv.shape, BF16)

    dq = z(O_DQ, O_DK)
    dk = z(O_DK, O_DV)
    dv = z(O_DV, O_GL)
    if latent:
        dq = _rope(dq, _tile_lanes(cos64, 4), _tile_lanes(sin64, 4), DIFF_D // 4)
        dk = _rope(dk, _tile_lanes(cos64, 4), _tile_lanes(sin64, 4), DIFF_D // 4)
    else:
        dk_o[...] = dk
        dv_o[...] = dv
    dqo_ref[...] = (dq * (DIFF_D ** -0.5 * LOG2E)).astype(BF16)
    dkv_ref[:, 0:512] = dk.astype(BF16)
    for h in range(DIFF_HEADS):
        dkv_ref[:, 512 + h * 256:512 + h * 256 + LANES] = dv[:, h * LANES:(h + 1) * LANES].astype(BF16)
        dkv_ref[:, 512 + h * 256 + LANES:512 + (h + 1) * 256] = jnp.ones((dv.shape[0], LANES), BF16)

    ret_ref[:, 0:256] = z(O_RQ, O_RK).astype(BF16)
    ret_ref[:, 256:512] = (z(O_RK, O_RV) * (RET_DK ** -0.5)).astype(BF16)
    ret_ref[:, 512:1024] = z(O_RV, O_RG).astype(BF16)
    rg_ref[...] = z(O_RG, O_DQ).astype(BF16)

    for n in range(N_BRANCH):
        gate_ref[:, n * D:(n + 1) * D] = jax.nn.sigmoid(
            z(O_GL + n * D, O_GL + (n + 1) * D)).astype(BF16)


def _inprep_call(latent, l, x, mod3, mod_row, lw, tabs, t_len):
    n_tok = x.shape[0]
    nblk = n_tok // TM
    blk_per_seq = t_len // TM

    def tok(w):
        return pl.BlockSpec((TM, w), lambda i: (i, 0))

    in_specs = [tok(D),
                pl.BlockSpec((None, 1, 6 * D), lambda i: (mod_row(i), 0, 0)),
                _const_spec((1, D)),
                pl.BlockSpec((None, O_END, D), lambda i: (l, 0, 0), pipeline_mode=pl.Buffered(1)),
                _const_spec((1, MLA_Q_LORA)), _const_spec((MLA_Q_LORA, 768)),
                _const_spec((1, MLA_KV_LORA)), _const_spec((MLA_KV_LORA, 1024)),
                _const_spec((1, 512)), _const_spec((1, LANES)), _const_spec((512, 512))]
    args = [x, mod3, lw["g_pre1"], lw["w_in"], lw["g_mla_q"], lw["w_uq"], lw["g_mla_kv"],
            lw["w_ukv"], lw["g_gqa_q"], lw["g_gqa_k"], lw["bd"]]
    if latent:
        tab_spec = pl.BlockSpec((TM, LANES), lambda i: (i % blk_per_seq, 0))
        in_specs += [tab_spec] * 4
        args += list(tabs)
    widths = [768, KVM_W, 512, GKV_W, 512, DKV_W, 1024, 512, 4 * D]
    out_specs = [tok(w) for w in widths]
    out_shape = [jax.ShapeDtypeStruct((n_tok, w), BF16) for w in widths]
    if not latent:
        cw = [MLA_KV_LORA, MLA_ROPE, 128, 128, 512, 512]
        out_specs += [tok(w) for w in cw]
        out_shape += [jax.ShapeDtypeStruct((n_tok, w), F32) for w in cw]
    return pl.pallas_call(
        functools.partial(_inprep_kernel, latent),
        grid=(nblk,),
        in_specs=in_specs, out_specs=out_specs, out_shape=out_shape,
        compiler_params=_cparams(("arbitrary",)),
        name="inprep_lat" if latent else "inprep_ctx",
    )(*args)


def _pastkv_kernel(ckv_ref, kpe_ref, wukv_ref, o_ref):
    kv = _dot(ckv_ref[...].astype(BF16), wukv_ref[...])
    _store_kvm(o_ref, kv, kpe_ref[...].astype(BF16))


def _pastkv_call(ckv, kpe4, w_ukv):
    n = ckv.shape[0]
    return pl.pallas_call(
        _pastkv_kernel,
        grid=(n // TM,),
        in_specs=[pl.BlockSpec((TM, MLA_KV_LORA), lambda i: (i, 0)),
                  pl.BlockSpec((TM, LANES), lambda i: (i, 0)),
                  _const_spec((MLA_KV_LORA, 1024))],
        out_specs=pl.BlockSpec((TM, KVM_W), lambda i: (i, 0)),
        out_shape=jax.ShapeDtypeStruct((n, KVM_W), BF16),
        compiler_params=_cparams(("arbitrary",)),
        name="pastkv",
    )(ckv, kpe4, w_ukv)


def _softmax_pv(s, v_ones):
    m = jnp.max(s, axis=-1, keepdims=True)
    p = jnp.exp2(s - m).astype(BF16)
    o = _dot(p, v_ones)
    return o[:, 0:LANES] / o[:, LANES:2 * LANES]


def _attn_kernel(lam_init, qm_ref, kvm_ref, gq_ref, gkv_ref, dq_ref, dkv_ref,
                 lam_ref, gdiff_ref, o_ref):
    tq = qm_ref.shape[0]
    lane = _lane_iota((tq, LANES))
    low = lane < HALF_LANES
    zero = jnp.zeros((tq, LANES), BF16)

    for p in range(MLA_HEADS // 2):
        qn = qm_ref[:, p * LANES:(p + 1) * LANES]
        g = p // 2
        qpe = qm_ref[:, 512 + g * LANES:512 + (g + 1) * LANES]
        kk = kvm_ref[:, p * 256:(p + 1) * 256]
        vv = kvm_ref[:, 1024 + p * 256:1024 + (p + 1) * 256]
        outs = []
        for half in range(2):
            h = 2 * p + half
            slot = h % 4
            in_slot = (lane >= slot * MLA_ROPE) & (lane < (slot + 1) * MLA_ROPE)
            lhs = jnp.concatenate(
                [jnp.where(low if half == 0 else ~low, qn, zero),
                 jnp.where(in_slot, qpe, zero)], axis=1)
            outs.append(_softmax_pv(_dot_nt(lhs, kk), vv))
        o_ref[:, p * LANES:(p + 1) * LANES] = jnp.where(low, outs[0], outs[1]).astype(BF16)

    kk = gkv_ref[:, 0:LANES]
    vv = gkv_ref[:, LANES:3 * LANES]
    for g in range(GQA_HEADS // 2):
        qg = gq_ref[:, g * LANES:(g + 1) * LANES]
        o_lo = _softmax_pv(_dot_nt(jnp.where(low, qg, zero), kk), vv)
        o_hi = _softmax_pv(_dot_nt(jnp.where(low, zero, qg), kk), vv)
        o_ref[:, 512 + g * LANES:512 + (g + 1) * LANES] = jnp.where(low, o_lo, o_hi).astype(BF16)

    lp = lam_ref[...]
    lam = (jnp.exp(jnp.sum(lp[0:1] * lp[1:2], axis=-1, keepdims=True))
           - jnp.exp(jnp.sum(lp[2:3] * lp[3:4], axis=-1, keepdims=True)) + lam_init)
    for h in range(DIFF_HEADS):
        qh = dq_ref[:, h * LANES:(h + 1) * LANES]
        kk = dkv_ref[:, h * LANES:(h + 1) * LANES]
        vv = dkv_ref[:, 512 + h * 256:512 + (h + 1) * 256]
        a1 =_softmax_pv(_dot_nt(jnp.where(low, qh, zero), kk), vv)
        a2 = _softmax_pv(_dot_nt(jnp.where(low, zero, qh), kk), vv)
        od = _rms(a1 - lam * a2, gdiff_ref[...]) * (1.0 - lam_init)
        o_ref[:, 1024 + h * LANES:1024 + (h + 1) * LANES] = od.astype(BF16)


def _attn_call(lam_init, qm, kvm, gq, gkv, dq, dkv, lam_p, g_diff, n_b, t_len, s_len):
    nq = t_len // TQ

    def qspec(w):
        return pl.BlockSpec((TQ, w), lambda b, i: (b * nq + i, 0))

    def kspec(w):
        return pl.BlockSpec((s_len, w), lambda b, i: (b, 0))

    return pl.pallas_call(
        functools.partial(_attn_kernel, lam_init),
        grid=(n_b, nq),
        in_specs=[qspec(768), kspec(KVM_W), qspec(512), kspec(GKV_W), qspec(512), kspec(DKV_W),
                  _const_spec((4, DIFF_D)), _const_spec((1, DIFF_DV))],
        out_specs=qspec(3 * BRANCH_W),
        out_shape=jax.ShapeDtypeStruct((n_b * t_len, 3 * BRANCH_W), BF16),
        compiler_params=_cparams(("arbitrary", "arbitrary")),
        name="attn",
    )(qm, kvm, gq, gkv, dq, dkv, lam_p, g_diff)


def _log_sigmoid(x):
    return jnp.minimum(x, 0.0) - jnp.log(1.0 + jnp.exp(-jnp.abs(x)))


def _ret_kernel(latent, t_len, dec_ref, q_ref, k_ref, v_ref, rg_ref, gret_ref, *refs):
    if latent:
        s0_ref, o_ref = refs
    else:
        o_ref, st_ref = refs
    tq = q_ref.shape[0]
    t0 = pl.program_id(1) * tq
    lane = _lane_iota((tq, LANES))
    low = lane < HALF_LANES
    zero = jnp.zeros((tq, LANES), BF16)
    t_idx = (t0 + lax.broadcasted_iota(jnp.int32, (tq, t_len), 0)).astype(F32)
    s_idx = lax.broadcasted_iota(jnp.int32, (tq, t_len), 1).astype(F32)
    dist = t_idx - s_idx
    t_col = (t0 + lax.broadcasted_iota(jnp.int32, (tq, 1), 0)).astype(F32)

    def lg(d, h):
        return _log_sigmoid(jnp.full((1, 1), dec_ref[d, h], F32))

    for h in range(RET_HEADS):
        p, half = h // 2, h % 2
        qp = q_ref[:, p * LANES:(p + 1) * LANES]
        qm = jnp.where(low if half == 0 else ~low, qp, zero)
        kp = k_ref[:, p * LANES:(p + 1) * LANES]
        vh = v_ref[:, h * LANES:(h + 1) * LANES]
        lgf, lgb = lg(0, h), lg(1, h)
        dmask = (jnp.where(dist >= 0, jnp.exp(lgf * jnp.maximum(dist, 0.0)), 0.0)
                 + jnp.where(dist <= 0, jnp.exp(lgb * jnp.maximum(-dist, 0.0)), 0.0))
        o = _dot((_dot_nt(qm, kp) * dmask).astype(BF16), vh)
        if latent:
            sf = s0_ref[0, p].astype(BF16)
            sb = s0_ref[1, p].astype(BF16)
            o = o + _dot(qm, sf) * jnp.exp(lgf * (t_col + 1.0))
            o = o + _dot(qm, sb) * jnp.exp(lgb * (float(t_len) - t_col))
        mu = jnp.mean(o, axis=-1, keepdims=True)
        oc = o - mu
        y = oc * lax.rsqrt(jnp.mean(oc * oc, axis=-1, keepdims=True) + EPS)
        y = y * gret_ref[:, h * LANES:(h + 1) * LANES]
        rg = rg_ref[:, h * LANES:(h + 1) * LANES].astype(F32)
        o_ref[:, h * LANES:(h + 1) * LANES] = (y * _silu(rg)).astype(BF16)

    if not latent:
        s_col = lax.broadcasted_iota(jnp.int32, (t_len, 1), 0).astype(F32)
        lane_t = _lane_iota((1, LANES)) < HALF_LANES
        for p in range(RET_HEADS // 2):
            kp = k_ref[:, p * LANES:(p + 1) * LANES].astype(F32)
            for d in range(2):
                lg_lane = jnp.where(lane_t, lg(d, 2 * p), lg(d, 2 * p + 1))
                expo = (float(t_len) - 1.0 - s_col) if d == 0 else s_col
                kdec_t = jnp.transpose(kp * jnp.exp(lg_lane * expo)).astype(BF16)
                for half in range(2):
                    h = 2 * p + half
                    st = _dot(kdec_t, v_ref[:, h * LANES:(h + 1) * LANES])
                    st_ref[d, h] = st[half * RET_DK:(half + 1) * RET_DK, :]


def _ret_call(latent, dec, ret, rg, g_ret, s0, n_b, t_len):
    nq = t_len // TQ
    assert latent or nq == 1
    in_specs = [pl.BlockSpec(memory_space=pltpu.SMEM),
                pl.BlockSpec((TQ, 256), lambda b, i: (b * nq + i, 0)),
                pl.BlockSpec((t_len, 256), lambda b, i: (b, 1)),
                pl.BlockSpec((t_len, 512), lambda b, i: (b, 1)),
                pl.BlockSpec((TQ, 512), lambda b, i: (b * nq + i, 0)),
                _const_spec((1, 512))]
    args = [dec, ret, ret, ret, rg, g_ret]
    out_specs = [pl.BlockSpec((TQ, 512), lambda b, i: (b * nq + i, 0))]
    out_shape = [jax.ShapeDtypeStruct((n_b * t_len, 512), BF16)]
    if latent:
        in_specs.append(pl.BlockSpec((None, 2, 2, LANES, LANES), lambda b, i: (b, 0, 0, 0, 0)))
        args.append(s0)
    else:
        out_specs.append(pl.BlockSpec((None, 2, RET_HEADS, RET_DK, RET_DV),
                                      lambda b, i: (b, 0, 0, 0, 0)))
        out_shape.append(jax.ShapeDtypeStruct((n_b, 2, RET_HEADS, RET_DK, RET_DV), F32))
    return pl.pallas_call(
        functools.partial(_ret_kernel, latent, t_len),
        grid=(n_b, nq),
        in_specs=in_specs, out_specs=out_specs, out_shape=out_shape,
        compiler_params=_cparams(("arbitrary", "arbitrary")),
        name="ret_lat" if latent else "ret_ctx",
    )(*args)


def _merge_kernel(x_ref, mod_ref, br_ref, or_ref, gate_ref, wbr_ref, wout_ref, gpost_ref, o_ref):
    merged = None
    for n in range(N_BRANCH):
        if n < 2:
            b = br_ref[:, n * BRANCH_W:(n + 1) * BRANCH_W]
        elif n == 2:
            b = or_ref[...]
        else:
            b = br_ref[:, 2 * BRANCH_W:3 * BRANCH_W]
        t = gate_ref[:, n * D:(n + 1) * D].astype(F32) * _dot(b, wbr_ref[n])
        merged = t if merged is None else merged + t
    out = _dot(merged.astype(BF16), wout_ref[...])
    g1 = mod_ref[...][:, 2 * D:3 * D]
    o_ref[...] = x_ref[...] + g1 * _rms(out, gpost_ref[...])


def _merge_call(x, mod3, mod_row, br, o_r, gates, lw):
    n_tok = x.shape[0]

    def tok(w):
        return pl.BlockSpec((TM, w), lambda i: (i, 0))

    return pl.pallas_call(
        _merge_kernel,
        grid=(n_tok // TM,),
        in_specs=[tok(D), pl.BlockSpec((None, 1, 6 * D), lambda i: (mod_row(i), 0, 0)),
                  tok(3 * BRANCH_W), tok(BRANCH_W), tok(4 * D),
                  _const_spec((N_BRANCH, BRANCH_W, D)), _const_spec((D, D)), _const_spec((1, D))],
        out_specs=tok(D),
        out_shape=jax.ShapeDtypeStruct((n_tok, D), F32),
        compiler_params=_cparams(("arbitrary",)),
        name="merge",
    )(x, mod3, br, o_r, gates, lw["w_br"], lw["w_out"], lw["g_post1"])


def _route(logits_t, bias):
    n = logits_t.shape[1]
    scores = jax.nn.sigmoid(logits_t)
    sel = scores + bias
    neg = -jnp.inf
    sub = lax.broadcasted_iota(jnp.int32, (GROUP_SIZE, n), 0)
    grp = []
    for g in range(N_GROUPS):
        blk = sel[g * GROUP_SIZE:(g + 1) * GROUP_SIZE]
        m1 = jnp.max(blk, axis=0, keepdims=True)
        i1 = jnp.min(jnp.where(blk == m1, sub, GROUP_SIZE), axis=0, keepdims=True)
        m2 = jnp.max(jnp.where(sub == i1, neg, blk), axis=0, keepdims=True)
        grp.append(m1 + m2)
    parts = []
    for g in range(N_GROUPS):
        beaten = jnp.zeros((1, n), jnp.int32)
        for o in range(N_GROUPS):
            if o == g:
                continue
            wins = (grp[o] > grp[g]) | (grp[o] == grp[g]) if o < g else (grp[o] > grp[g])
            beaten = beaten + wins.astype(jnp.int32)
        keep = beaten < TOPK_GROUPS
        parts.append(jnp.where(keep, sel[g * GROUP_SIZE:(g + 1) * GROUP_SIZE], neg))
    cur = jnp.concatenate(parts, axis=0)
    eidx = lax.broadcasted_iota(jnp.int32, (N_EXPERTS, n), 0)
    hits, ids, ws = [], [], []
    for _ in range(TOP_K):
        m = jnp.max(cur, axis=0, keepdims=True)
        i = jnp.min(jnp.where(cur == m, eidx, N_EXPERTS), axis=0, keepdims=True)
        hit = eidx == i
        hits.append(hit)
        ids.append(i)
        ws.append(jnp.sum(jnp.where(hit, scores, 0.0), axis=0, keepdims=True))
        cur = jnp.where(hit, neg, cur)
    wsum = ws[0] + ws[1] + ws[2] + ws[3]
    return hits, ids, [w / wsum * ROUTE_SCALE for w in ws]


U32 = jnp.uint32
HIGH16 = np.uint32(0xFFFF0000)


def _bf16_bits(v):
    return lax.bitcast_convert_type(v.astype(BF16).astype(F32), U32)


def _pack_rows(v):
    return (_bf16_bits(v[:, 0:D // 2]) >> 16) | _bf16_bits(v[:, D // 2:D])


def _unpack_rows(p):
    lo = lax.bitcast_convert_type(p << 16, F32)
    hi = lax.bitcast_convert_type(p & HIGH16, F32)
    return jnp.concatenate([lo, hi], axis=1)


def _moe_pre_kernel(x_ref, mod_ref, gpre_ref, wr_ref, br_ref, tri_ref,
                    hp_ref, eidx_ref, rank_ref, comb_ref, cnt_ref, run_ref):
    tm = x_ref.shape[0]

    @pl.when(pl.program_id(0) == 0)
    def _():
        run_ref[...] = jnp.zeros_like(run_ref)

    mod = mod_ref[...]
    sh2, sc2 = mod[:, 3 * D:4 * D], mod[:, 4 * D:5 * D]
    h = _rms(x_ref[...], gpre_ref[...]) * (1.0 + sc2) + sh2
    hp_ref[...] = _pack_rows(h)
    hb = h.astype(BF16)
    h_lo = (h - hb.astype(F32)).astype(BF16)
    wr = wr_ref[...]
    wr_hi = wr.astype(BF16)
    wr_lo = (wr - wr_hi.astype(F32)).astype(BF16)
    logits_t = _dot_nt(wr_hi, hb) + _dot_nt(wr_hi, h_lo) + _dot_nt(wr_lo, hb)
    hits, ids, ws = _route(logits_t, br_ref[...])

    picked = jnp.zeros((N_EXPERTS, tm), F32)
    for hit in hits:
        picked = jnp.where(hit, 1.0, picked)
    before = _dot(picked.astype(BF16), tri_ref[...]) + run_ref[:, 0:1]
    sub8 = lax.broadcasted_iota(jnp.int32, (8, tm), 0)
    comb8 = jnp.zeros((8, tm), F32)
    for k in range(TOP_K):
        rank = jnp.sum(jnp.where(hits[k], before, 0.0), axis=0, keepdims=True)
        eidx_ref[k:k + 1, :] = ids[k]
        rank_ref[k:k + 1, :] = rank.astype(jnp.int32)
        comb8 = jnp.where(sub8 == k, ws[k], comb8)
    comb_ref[...] = jnp.transpose(
        jnp.concatenate([comb8, jnp.zeros((LANES - 8, tm), F32)], axis=0))
    run_ref[...] = run_ref[...] + jnp.sum(picked, axis=1, keepdims=True)
    cnt_ref[...] = run_ref[...]


def _moe_pre_call(x, mod3, mod_row, lw):
    n_tok = x.shape[0]
    tm = TM_MOE_PRE
    tri = np.arange(tm)
    tri = jnp.asarray(tri[:, None] < tri[None, :], BF16)
    row4 = pl.BlockSpec((TOP_K, tm), lambda i: (0, i))
    return pl.pallas_call(
        _moe_pre_kernel,
        grid=(n_tok // tm,),
        in_specs=[pl.BlockSpec((tm, D), lambda i: (i, 0)),
                  pl.BlockSpec((None, 1, 6 * D), lambda i: (mod_row(i), 0, 0)),
                  _const_spec((1, D)), _const_spec((N_EXPERTS, D)), _const_spec((N_EXPERTS, 1)),
                  _const_spec((tm, tm))],
        out_specs=[pl.BlockSpec((tm, D // 2), lambda i: (i, 0)), row4, row4,
                   pl.BlockSpec((tm, LANES), lambda i: (i, 0)),
                   _const_spec((N_EXPERTS, LANES))],
        out_shape=[jax.ShapeDtypeStruct((n_tok, D // 2), U32),
                   jax.ShapeDtypeStruct((TOP_K, n_tok), jnp.int32),
                   jax.ShapeDtypeStruct((TOP_K, n_tok), jnp.int32),
                   jax.ShapeDtypeStruct((n_tok, LANES), F32),
                   jax.ShapeDtypeStruct((N_EXPERTS, LANES), F32)],
        scratch_shapes=[pltpu.VMEM((N_EXPERTS, LANES), F32)],
        compiler_params=_cparams(("arbitrary",)),
        name="moe_pre",
    )(x, mod3, lw["g_pre2"], lw["w_router_t"], lw["b_router"], tri)


def _moe_plan_kernel(eidx_ref, rank_ref, cnt_ref, dest_ref, te_ref, tv_ref, tn_ref):
    tm = eidx_ref.shape[1]
    cnt = cnt_ref[...]
    padded = jnp.ceil(cnt * (1.0 / TMX)) * TMX
    row = lax.broadcasted_iota(jnp.int32, cnt.shape, 0)
    incl = padded
    shift = 1
    while shift < N_EXPERTS:
        incl = incl + jnp.where(row >= shift, pltpu.roll(incl, shift, 0), 0.0)
        shift *= 2
    start = (incl - padded)[:, 0:1]
    end = incl[:, 0:1]
    erow = lax.broadcasted_iota(jnp.int32, (N_EXPERTS, tm), 0)
    for k in range(TOP_K):
        mine = erow == eidx_ref[k:k + 1, :]
        base = jnp.sum(jnp.where(mine, start, 0.0), axis=0, keepdims=True)
        dest_ref[k:k + 1, :] = rank_ref[k:k + 1, :] + base.astype(jnp.int32)

    @pl.when(pl.program_id(0) == 0)
    def _():
        tile0 = (_lane_iota((1, LANES)) * TMX).astype(F32)
        owner = jnp.sum(jnp.where(end <= tile0, 1.0, 0.0), axis=0, keepdims=True)
        owner = jnp.minimum(owner, N_EXPERTS - 1.0)
        erow_t = lax.broadcasted_iota(jnp.int32, (N_EXPERTS, LANES), 0).astype(F32)
        left = jnp.sum(jnp.where(erow_t == owner, cnt[:, 0:1] - (tile0 - start), 0.0),
                       axis=0, keepdims=True)
        te_ref[...] = owner.astype(jnp.int32)
        tv_ref[...] = jnp.clip(left, 0.0, float(TMX)).astype(jnp.int32)
        tn_ref[...] = jnp.full(tn_ref.shape, N_EXPERTS, jnp.int32)
        nxt = owner
        for k in range(W_SLOTS - 1):
            later = (erow_t > nxt) & (cnt[:, 0:1] > 0.0)
            nxt = jnp.min(jnp.where(later, erow_t, float(N_EXPERTS)), axis=0, keepdims=True)
            tn_ref[k:k + 1, :] = nxt.astype(jnp.int32)


def _moe_plan_call(eidx, rank, cnt):
    n_tok = eidx.shape[1]
    tm = TM_MOE_PRE
    row4 = pl.BlockSpec((TOP_K, tm), lambda i: (0, i))
    tiles = jax.ShapeDtypeStruct((1, LANES), jnp.int32)
    return pl.pallas_call(
        _moe_plan_kernel,
        grid=(n_tok // tm,),
        in_specs=[row4, row4, _const_spec((N_EXPERTS, LANES))],
        out_specs=[row4, _const_spec((1, LANES)), _const_spec((1, LANES)), _const_spec((8, LANES))],
        out_shape=[jax.ShapeDtypeStruct((TOP_K, n_tok), jnp.int32), tiles, tiles,
                   jax.ShapeDtypeStruct((8, LANES), jnp.int32)],
        compiler_params=_cparams(("arbitrary",)),
        name="moe_plan",
    )(eidx, rank, cnt)


def _experts_kernel(l, te_ref, tv_ref, tn_ref, xs_ref, wgu_hbm, wdn_hbm, ys_ref,
                    wgu_f, wdn_f, wgu_b, wdn_b, sem, group_ref):
    j = pl.program_id(0)
    valid = tv_ref[j]
    expert = te_ref[j]

    def fetch(e, slot):
        return (pltpu.make_async_copy(wgu_hbm.at[l, e], wgu_f.at[slot], sem.at[slot, 0]),
                pltpu.make_async_copy(wdn_hbm.at[l, e], wdn_f.at[slot], sem.at[slot, 1]))

    def start_if_any(e, slot):
        @pl.when(e < N_EXPERTS)
        def _():
            for cp in fetch(e, slot):
                cp.start()

    @pl.when(j == 0)
    def _():
        group_ref[0] = 0
        start_if_any(expert, 0)
        for k in range(W_SLOTS - 2):
            start_if_any(tn_ref[k, 0], k + 1)

    first_tile = (j == 0) | (expert != te_ref[jnp.maximum(j - 1, 0)])

    @pl.when(first_tile & (valid > 0))
    def _():
        group = group_ref[0]
        slot = lax.rem(group, W_SLOTS)
        for cp in fetch(expert, slot):
            cp.wait()
        wgu_b[...] = wgu_f[slot].astype(BF16)
        wdn_b[...] = wdn_f[slot].astype(BF16)
        start_if_any(tn_ref[W_SLOTS - 2, j], lax.rem(group + W_SLOTS - 1, W_SLOTS))
        group_ref[0] = group + 1

    @pl.when(valid > 0)
    def _():
        rows = lax.broadcasted_iota(jnp.int32, (TMX, D), 0)
        x = jnp.where(rows < valid, _unpack_rows(xs_ref[...]), 0.0).astype(BF16)
        gu = _dot(x, wgu_b[...])
        a = _silu(gu[:, 0:EXPERT_FF]) * gu[:, EXPERT_FF:2 * EXPERT_FF]
        ys_ref[...] = _pack_rows(_dot(a.astype(BF16), wdn_b[...]))

    @pl.when(valid <= 0)
    def _():
        ys_ref[...] = jnp.zeros_like(ys_ref)


def _experts_call(l, xs, te, tv, tn, w_gu, w_dn):
    n_tiles = xs.shape[0] // TMX
    grid_spec = pltpu.PrefetchScalarGridSpec(
        num_scalar_prefetch=3,
        grid=(n_tiles,),
        in_specs=[pl.BlockSpec((TMX, D // 2), lambda j, *_: (j, 0)),
                  pl.BlockSpec(memory_space=pl.ANY), pl.BlockSpec(memory_space=pl.ANY)],
        out_specs=pl.BlockSpec((TMX, D // 2), lambda j, *_: (j, 0)),
        scratch_shapes=[pltpu.VMEM((W_SLOTS, D, 2 * EXPERT_FF), F32),
                        pltpu.VMEM((W_SLOTS, EXPERT_FF, D), F32),
                        pltpu.VMEM((D, 2 * EXPERT_FF), BF16), pltpu.VMEM((EXPERT_FF, D), BF16),
                        pltpu.SemaphoreType.DMA((W_SLOTS, 2)), pltpu.SMEM((1,), jnp.int32)])
    return pl.pallas_call(
        functools.partial(_experts_kernel, l),
        grid_spec=grid_spec,
        out_shape=jax.ShapeDtypeStruct(xs.shape, U32),
        compiler_params=_cparams(("arbitrary",)),
        name="moe_experts",
    )(te, tv, tn, xs, w_gu, w_dn)


def _moe_post_kernel(x_ref, mod_ref, hp_ref, yg_ref, comb_ref, wsgu_ref, wsdn_ref, gpost_ref,
                     o_ref):
    hb = _unpack_rows(hp_ref[...]).astype(BF16)
    sgu = _dot(hb, wsgu_ref[...])
    sa = _silu(sgu[:, 0:SHARED_FF]) * sgu[:, SHARED_FF:2 * SHARED_FF]
    acc = _dot(sa.astype(BF16), wsdn_ref[...])
    comb = comb_ref[...]
    for k in range(TOP_K):
        acc = acc + comb[:, k:k + 1] * _unpack_rows(yg_ref[k])
    g2 = mod_ref[...][:, 5 * D:6 * D]
    o_ref[...] = x_ref[...] + g2 * _rms(acc, gpost_ref[...])


def _moe_post_call(x, mod3, mod_row, hp, yg, comb, lw):
    n_tok = x.shape[0]
    tm = TM_MOE_PRE
    return pl.pallas_call(
        _moe_post_kernel,
        grid=(n_tok // tm,),
        in_specs=[pl.BlockSpec((tm, D), lambda i: (i, 0)),
                  pl.BlockSpec((None, 1, 6 * D), lambda i: (mod_row(i), 0, 0)),
                  pl.BlockSpec((tm, D // 2), lambda i: (i, 0)),
                  pl.BlockSpec((TOP_K, tm, D // 2), lambda i: (0, i, 0)),
                  pl.BlockSpec((tm, LANES), lambda i: (i, 0)),
                  _const_spec((D, 2 * SHARED_FF)), _const_spec((SHARED_FF, D)),
                  _const_spec((1, D))],
        out_specs=pl.BlockSpec((tm, D), lambda i: (i, 0)),
        out_shape=jax.ShapeDtypeStruct((n_tok, D), F32),
        compiler_params=_cparams(("arbitrary",)),
        name="moe_post",
    )(x, mod3, hp, yg, comb, lw["w_sh_gu"], lw["w_sh_down"], lw["g_post2"])


def _moe_call(l, x, mod3, mod_row, lw):
    n_tok = x.shape[0]
    n_slots = -(-(TOP_K * n_tok + N_EXPERTS * (TMX - 1)) // TMX) * TMX
    assert n_slots // TMX <= LANES
    hp, eidx, rank, comb, cnt = _moe_pre_call(x, mod3, mod_row, lw)
    dest, te, tv, tn = _moe_plan_call(eidx, rank, cnt)
    dest = dest.reshape(TOP_K * n_tok)
    xs = _sc_scatter_rows(hp, dest, n_slots)
    ys = _experts_call(l, xs, te[0], tv[0], tn, lw["w_exp_gu"], lw["w_exp_down"])
    yg = _sc_gather_rows(ys, dest).reshape(TOP_K, n_tok, D // 2)
    return _moe_post_call(x, mod3, mod_row, hp, yg, comb, lw)


SC_CORES, SC_SUBCORES = 2, 16
SC_WORKERS = SC_CORES * SC_SUBCORES


def _sc_gather_rows(table, idx, chunk=64):
    n_out, width = idx.shape[0], table.shape[1]
    per_worker = n_out // SC_WORKERS
    n_chunks = per_worker // chunk
    assert per_worker * SC_WORKERS == n_out and n_chunks * chunk == per_worker
    mesh = plsc.VectorSubcoreMesh(core_axis_name="c", subcore_axis_name="s",
                                  num_cores=SC_CORES, num_subcores=SC_SUBCORES)

    @functools.partial(
        pl.kernel, mesh=mesh,
        out_type=jax.ShapeDtypeStruct((n_out, width), table.dtype),
        scratch_types=[pltpu.VMEM((chunk,), jnp.int32), pltpu.VMEM((chunk, width), table.dtype),
                       pltpu.SemaphoreType.DMA],
        name="sc_gather")
    def gather(table_hbm, idx_hbm, out_hbm, idx_v, rows_v, sem):
        base = (lax.axis_index("s") * SC_CORES + lax.axis_index("c")) * per_worker

        @pl.loop(0, n_chunks)
        def _(j):
            off = base + j * chunk
            pltpu.sync_copy(idx_hbm.at[pl.ds(off, chunk)], idx_v)
            pltpu.async_copy(table_hbm.at[idx_v], rows_v, sem).wait()
            pltpu.sync_copy(rows_v, out_hbm.at[pl.ds(off, chunk)])

    return gather(table, idx)


def _sc_scatter_rows(rows, dest, n_slots, chunk=64):
    n_tok, width = rows.shape
    per_worker = n_tok // SC_WORKERS
    n_chunks = per_worker // chunk
    assert per_worker * SC_WORKERS == n_tok and n_chunks * chunk == per_worker
    mesh = plsc.VectorSubcoreMesh(core_axis_name="c", subcore_axis_name="s",
                                  num_cores=SC_CORES, num_subcores=SC_SUBCORES)

    @functools.partial(
        pl.kernel, mesh=mesh,
        out_type=jax.ShapeDtypeStruct((n_slots, width), rows.dtype),
        scratch_types=[pltpu.VMEM((chunk,), jnp.int32), pltpu.VMEM((chunk, width), rows.dtype)],
        name="sc_scatter")
    def scatter(rows_hbm, dest_hbm, out_hbm, idx_v, rows_v):
        base = (lax.axis_index("s") * SC_CORES + lax.axis_index("c")) * per_worker

        @pl.loop(0, n_chunks)
        def _(j):
            off = base + j * chunk
            pltpu.sync_copy(rows_hbm.at[pl.ds(off, chunk)], rows_v)
            for k in range(TOP_K):
                pltpu.sync_copy(dest_hbm.at[pl.ds(k * n_tok + off, chunk)], idx_v)
                pltpu.sync_copy(rows_v, out_hbm.at[idx_v])

    return scatter(rows, dest)


def _rope_tables(t_len):
    pos = np.arange(t_len)
    row, col = pos // GRID_W, pos % GRID_W

    def tab(r):
        half = r // 2
        freq = ROPE_BASE ** (-np.arange(half, dtype=np.float64) / half)
        sign = np.concatenate([-np.ones(half), np.ones(half)])
        cs, sn = [], []
        for p in (row, col):
            ang = p[:, None].astype(np.float64) * freq[None, :]
            cs.append(np.concatenate([np.cos(ang), np.cos(ang)], axis=1))
            sn.append(np.concatenate([np.sin(ang), np.sin(ang)], axis=1) * sign[None, :])
        return np.concatenate(cs, axis=1), np.concatenate(sn, axis=1)

    c64, s64 = tab(GQA_HD // 2)
    cpe, spe = tab(MLA_ROPE // 2)
    out = (np.tile(c64, (1, 2)), np.tile(s64, (1, 2)), np.tile(cpe, (1, 4)), np.tile(spe, (1, 4)))
    return tuple(jnp.asarray(a, F32) for a in out)


def _layer_weights(l, p):
    w_uq = p["w_mla_uq"][l].reshape(MLA_Q_LORA, MLA_HEADS, MLA_NOPE + MLA_ROPE)
    w_uq = jnp.concatenate([w_uq[:, :, :MLA_NOPE].reshape(MLA_Q_LORA, -1),
                            w_uq[:, :, MLA_NOPE:].reshape(MLA_Q_LORA, -1)], axis=1)
    w_ukv = p["w_mla_ukv"][l].reshape(MLA_KV_LORA, MLA_HEADS, MLA_NOPE + MLA_V)
    w_ukv = jnp.concatenate([w_ukv[:, :, :MLA_NOPE].reshape(MLA_KV_LORA, -1),
                             w_ukv[:, :, MLA_NOPE:].reshape(MLA_KV_LORA, -1)], axis=1)
    w_br = p["w_br"][l]
    w_br_gqa = w_br[1].reshape(GQA_HEADS, GQA_HD, D)[jnp.array(GQA_ORDER)].reshape(BRANCH_W, D)
    w_br = jnp.stack([w_br[0], w_br_gqa, w_br[2], w_br[3]], axis=0)
    blk = np.arange(512) // GQA_HD
    return {
        "g_pre1": p["g_pre1"][l].reshape(1, D), "g_post1": p["g_post1"][l].reshape(1, D),
        "g_pre2": p["g_pre2"][l].reshape(1, D), "g_post2": p["g_post2"][l].reshape(1, D),
        "w_in": p["w_in_packed"],
        "g_mla_q": p["g_mla_q"][l].reshape(1, -1), "w_uq": w_uq.astype(BF16),
        "g_mla_kv": p["g_mla_kv"][l].reshape(1, -1), "w_ukv": w_ukv.astype(BF16),
        "g_gqa_q": jnp.tile(p["g_gqa_q"][l], GQA_HEADS).reshape(1, -1),
        "g_gqa_k": jnp.tile(p["g_gqa_k"][l], GQA_KV_HEADS).reshape(1, -1),
        "bd": jnp.asarray(blk[:, None] == blk[None, :], BF16),
        "ret_decay": p["ret_decay"][l],
        "g_ret": p["g_ret"][l].reshape(1, -1),
        "diff_lambda": p["diff_lambda"][l], "g_diff": p["g_diff"][l].reshape(1, -1),
        "w_br": w_br.astype(BF16), "w_out": p["w_out"][l].astype(BF16),
        "w_router_t": p["w_router"][l].T, "b_router": p["b_router"][l].reshape(-1, 1),
        "w_exp_gu": p["w_exp_gu"], "w_exp_down": p["w_exp_down"],
        "w_sh_gu": p["w_sh_gu"][l].astype(BF16), "w_sh_down": p["w_sh_down"][l].astype(BF16),
    }


def _mixers(latent, l, x, mod3, mod_row, lw, n_b, t_len, tabs=None, past=None, s0=None):
    lam_init = 0.8 - 0.6 * math.exp(-0.3 * l)
    outs = _inprep_call(latent, l, x, mod3, mod_row, lw, tabs, t_len)
    qm, kvm, gq, gkv, dq, dkv, ret, rg, gates = outs[:9]
    s_len = t_len
    if latent:
        past_kvm, past_gkv, past_dkv = past
        p_len = past_gkv.shape[1]
        s_len = p_len + t_len

        def cat(a, b):
            return jnp.concatenate([a, b.reshape(n_b, t_len, -1)], axis=1).reshape(n_b * s_len, -1)

        kvm, gkv, dkv = cat(past_kvm, kvm), cat(past_gkv, gkv), cat(past_dkv, dkv)
    br = _attn_call(lam_init, qm, kvm, gq, gkv, dq, dkv, lw["diff_lambda"], lw["g_diff"],
                    n_b, t_len, s_len)
    r = _ret_call(latent, lw["ret_decay"], ret, rg, lw["g_ret"], s0, n_b, t_len)
    y = _merge_call(x, mod3, mod_row, br, r[0], gates, lw)
    cache = None if latent else tuple(outs[9:]) + (r[1],)
    return y, cache


def kernel(x_prompt, x_sample, cache_mla_ckv, cache_mla_kpe, cache_gqa_k, cache_gqa_v, cache_diff_k, cache_diff_v, state_ret, c, c_ctx, w_mod, b_mod, g_pre1, g_post1, g_pre2, g_post2, w_in, g_mla_q, w_mla_uq, g_mla_kv, w_mla_ukv, g_gqa_q, g_gqa_k, ret_decay, g_ret, diff_lambda, g_diff, w_br, w_out, w_router, b_router, w_exp_gu, w_exp_down, w_sh_gu, w_sh_down):
    params = dict(w_in_packed=jnp.swapaxes(w_in, 1, 2).astype(BF16), g_pre1=g_pre1, g_post1=g_post1, g_pre2=g_pre2,
                  g_post2=g_post2, g_mla_q=g_mla_q, w_mla_uq=w_mla_uq,
                  g_mla_kv=g_mla_kv, w_mla_ukv=w_mla_ukv, g_gqa_q=g_gqa_q, g_gqa_k=g_gqa_k,
                  ret_decay=ret_decay, g_ret=g_ret, diff_lambda=diff_lambda, g_diff=g_diff,
                  w_br=w_br, w_out=w_out, w_router=w_router, b_router=b_router,
                  w_exp_gu=w_exp_gu, w_exp_down=w_exp_down, w_sh_gu=w_sh_gu, w_sh_down=w_sh_down)
    n_bc, t_c, _ = x_prompt.shape
    n_bl, t_l, _ = x_sample.shape
    p_len = cache_mla_ckv.shape[2]
    tabs = _rope_tables(t_l)
    n_cond = 8
    cond = jnp.concatenate([c_ctx[None, :], c, jnp.zeros((n_cond - 1 - n_bl, D), F32)], axis=0)
    blk_c, blk_l = t_c // TM, t_l // TM
    assert t_l % TM_MOE_PRE == 0 and (t_c * n_bc) % TM_MOE_PRE == 0

    yp = x_prompt.reshape(n_bc * t_c, D)
    ys = x_sample.reshape(n_bl * t_l, D)
    caches = []
    for l in range(DEPTH):
        lw = _layer_weights(l, params)
        mod3 = _mod_call(l, cond, w_mod, b_mod).reshape(n_cond, 1, 6 * D)
        yp, cache = _mixers(False, l, yp, mod3, lambda i: 0, lw, n_bc, t_c)
        yp = _moe_call(l, yp, mod3, lambda i: 0, lw)
        caches.append(cache)
        past_kvm = _pastkv_call(cache_mla_ckv[:, l].reshape(n_bl * p_len, -1),
                                jnp.tile(cache_mla_kpe[:, l].reshape(n_bl * p_len, -1), (1, 4)),
                                lw["w_ukv"]).reshape(n_bl, p_len, -1)
        past_gkv = jnp.concatenate([cache_gqa_k[:, l].reshape(n_bl, p_len, -1),
                                    cache_gqa_v[:, l].reshape(n_bl, p_len, -1),
                                    jnp.ones((n_bl, p_len, LANES), F32)], axis=-1).astype(BF16)
        past_dv = jnp.concatenate([cache_diff_v[:, l], jnp.ones_like(cache_diff_v[:, l])], axis=-1)
        past_dkv = jnp.concatenate([cache_diff_k[:, l].reshape(n_bl, p_len, -1),
                                    past_dv.reshape(n_bl, p_len, -1)], axis=-1).astype(BF16)
        s0 = state_ret[:, l].reshape(n_bl, 2, RET_HEADS // 2, 2 * RET_DK, RET_DV)
        ys, _ = _mixers(True, l, ys, mod3, lambda i: 1 + i // blk_l, lw, n_bl, t_l, tabs=tabs,
                        past=(past_kvm, past_gkv, past_dkv), s0=s0)
        ys = _moe_call(l, ys, mod3, lambda i: 1 + i // (t_l // TM_MOE_PRE), lw)

    def stack(k, shape):
        return jnp.stack([caches[l][k].reshape((n_bc, t_c) + shape) for l in range(DEPTH)], axis=1)

    new_ret = jnp.stack([caches[l][6] for l in range(DEPTH)], axis=1)
    return (yp.reshape(n_bc, t_c, D), ys.reshape(n_bl, t_l, D),
            stack(0, (MLA_KV_LORA,)), stack(1, (MLA_ROPE,)),
            stack(2, (GQA_KV_HEADS, GQA_HD)), stack(3, (GQA_KV_HEADS, GQA_HD)),
            stack(4, (DIFF_HEADS, 2, DIFF_D)), stack(5, (DIFF_HEADS, DIFF_DV)), new_ret)
```

```python
import functools
import math

import numpy as np
import jax
import jax.numpy as jnp
from jax import lax
from jax.experimental import pallas as pl
from jax.experimental.pallas import tpu as pltpu
from jax.experimental.pallas import tpu_sc as plsc

F32 = jnp.float32
BF16 = jnp.bfloat16

D = 1024
DEPTH = 2
GRID_W = 64
ROPE_BASE = 10000.0
EPS = 1e-6

MLA_HEADS, MLA_NOPE, MLA_ROPE, MLA_V = 8, 64, 32, 64
MLA_Q_LORA, MLA_KV_LORA = 384, 256
GQA_HEADS, GQA_KV_HEADS, GQA_HD = 8, 2, 64
RET_HEADS, RET_DK, RET_DV = 4, 64, 128
DIFF_HEADS, DIFF_D, DIFF_DV = 4, 64, 128
N_BRANCH, BRANCH_W = 4, 512
N_EXPERTS, TOP_K, N_GROUPS, TOPK_GROUPS = 32, 4, 4, 2
EXPERT_FF, SHARED_FF = 256, 256
ROUTE_SCALE = 2.5
GROUP_SIZE = N_EXPERTS // N_GROUPS

LANES = 128
HALF_LANES = 64
VMEM_LIMIT = 56 * 1024 * 1024

C_CQ, C_CKV, C_KPE, C_GQ, C_GK, C_GV = 0, 384, 640, 768, 1280, 1408
C_DQ, C_DK, C_DV, C_RQ, C_RK, C_RV, C_RG, C_GL, C_END = (
    1536, 2048, 2560, 3072, 3328, 3584, 4096, 4608, 8704)
O_CQ, O_CKV, O_KPE, O_GQ, O_GK, O_GV = 0, 384, 640, 672, 1184, 1312
O_RQ, O_RK, O_RV, O_RG, O_DQ, O_DK, O_DV, O_GL, O_END = (
    1440, 1696, 1952, 2464, 2976, 3488, 4000, 4512, 8608)
GQA_ORDER = (0, 4, 1, 5, 2, 6, 3, 7)

KVM_W = 8 * 256
GKV_W = 3 * LANES
DKV_W = 512 + 4 * 256
LOG2E = 1.4426950408889634
TM = 256
TQ = 256
TM_MOE_PRE = 512
TMX = 256
W_SLOTS = 4


def _cparams(sem):
    return pltpu.CompilerParams(dimension_semantics=sem, vmem_limit_bytes=VMEM_LIMIT)


def _const_spec(shape):
    nd = len(shape)
    return pl.BlockSpec(shape, lambda *_: (0,) * nd)


def _rms(x, g):
    return x * lax.rsqrt(jnp.mean(x * x, axis=-1, keepdims=True) + EPS) * g


def _dot(a, b):
    return jnp.dot(a, b, preferred_element_type=F32)


def _dot_nt(a, b):
    return lax.dot_general(a, b, (((1,), (1,)), ((), ())), preferred_element_type=F32)


def _silu(x):
    return x * jax.nn.sigmoid(x)


def _lane_iota(shape):
    return lax.broadcasted_iota(jnp.int32, shape, len(shape) - 1)


def _seg_meansq(x, bd_ref, width):
    sq = x * x
    hi = sq.astype(BF16)
    lo = (sq - hi.astype(F32)).astype(BF16)
    bd = bd_ref[0:width, 0:width]
    return (_dot(hi, bd) + _dot(lo, bd)) * (1.0 / GQA_HD)


def _rope(x, cos, sin_signed, half):
    width = x.shape[-1]
    first = (_lane_iota(x.shape) % (2 * half)) < half
    partner = jnp.where(first, pltpu.roll(x, width - half, 1), pltpu.roll(x, half, 1))
    return x * cos + partner * sin_signed


def _tile_lanes(t, reps):
    return t if reps == 1 else jnp.concatenate([t] * reps, axis=1)


def _store_kvm(kvm_ref, kv, kpe_b):
    ones = jnp.ones(kpe_b.shape, BF16)
    for p in range(4):
        kvm_ref[:, p * 256:p * 256 + LANES] = kv[:, p * LANES:(p + 1) * LANES].astype(BF16)
        kvm_ref[:, p * 256 + LANES:(p + 1) * 256] = kpe_b
        kvm_ref[:, 1024 + p * 256:1024 + p * 256 + LANES] = (
            kv[:, 512 + p * LANES:512 + (p + 1) * LANES].astype(BF16))
        kvm_ref[:, 1024 + p * 256 + LANES:1024 + (p + 1) * 256] = ones


def _mod_kernel(c_ref, w_ref, b_ref, o_ref):
    a = _silu(c_ref[...]).astype(BF16)
    o_ref[...] = _dot(a, w_ref[...].astype(BF16)) + b_ref[...]


def _mod_call(l, cond, w_mod, b_mod):
    n_l, _, n = w_mod.shape
    tn = 1536
    return pl.pallas_call(
        _mod_kernel,
        grid=(n // tn,),
        in_specs=[_const_spec(cond.shape),
                  pl.BlockSpec((None, D, tn), lambda j: (l, 0, j)),
                  pl.BlockSpec((None, 1, tn), lambda j: (l, 0, j))],
        out_specs=pl.BlockSpec((cond.shape[0], tn), lambda j: (0, j)),
        out_shape=jax.ShapeDtypeStruct((cond.shape[0], n), F32),
        compiler_params=_cparams(("arbitrary",)),
        name="mod",
    )(cond, w_mod, b_mod.reshape(n_l, 1, n))


def _inprep_kernel(latent, *refs):
    (x_ref, mod_ref, gpre_ref, win_ref, gmq_ref, wuq_ref, gmkv_ref, wukv_ref,
     ggq_ref, ggk_ref, bd_ref) = refs[:11]
    refs = refs[11:]
    if latent:
        cos64_ref, sin64_ref, cospe_ref, sinpe_ref = refs[:4]
        refs = refs[4:]
    (qm_ref, kvm_ref, gqo_ref, gkv_ref, dqo_ref, dkv_ref, ret_ref, rg_ref, gate_ref) = refs[:9]
    refs = refs[9:]
    if not latent:
        ckv_o, kpe_o, gk_o, gv_o, dk_o, dv_o = refs

    x = x_ref[...]
    mod = mod_ref[...]
    sh1 = mod[:, 0:D]
    sc1 = mod[:, D:2 * D]
    hb = (_rms(x, gpre_ref[...]) * (1.0 + sc1) + sh1).astype(BF16)

    def z(a, b):
        return _dot_nt(hb, win_ref[a:b, :])

    if latent:
        cos64, sin64 = cos64_ref[...], sin64_ref[...]
        cospe, sinpe = cospe_ref[...], sinpe_ref[...]

    cqn = _rms(z(O_CQ, O_CKV), gmq_ref[...]).astype(BF16)
    q = _dot(cqn, wuq_ref[...]) * ((MLA_NOPE + MLA_ROPE) ** -0.5 * LOG2E)
    q_nope, q_pe = q[:, 0:512], q[:, 512:768]
    if latent:
        q_pe = _rope(q_pe, _tile_lanes(cospe, 2), _tile_lanes(sinpe, 2), MLA_ROPE // 4)
    qm_ref[:, 0:512] = q_nope.astype(BF16)
    qm_ref[:, 512:768] = q_pe.astype(BF16)

    ckvn = _rms(z(O_CKV, O_KPE), gmkv_ref[...])
    kv = _dot(ckvn.astype(BF16), wukv_ref[...])
    kpe4 = _dot_nt(hb, jnp.concatenate([win_ref[O_KPE:O_GQ, :]] * 4, axis=0))
    if latent:
        kpe4 = _rope(kpe4, cospe, sinpe, MLA_ROPE // 4)
    else:
        ckv_o[...] = ckvn
        kpe_o[...] = kpe4[:, 0:MLA_ROPE]
    _store_kvm(kvm_ref, kv, kpe4.astype(BF16))

    gq = _dot_nt(hb, jnp.concatenate(
        [win_ref[O_GQ + h * GQA_HD:O_GQ + (h + 1) * GQA_HD, :] for h in GQA_ORDER], axis=0))
    gq = gq * lax.rsqrt(_seg_meansq(gq, bd_ref, 512) + EPS) * ggq_ref[...]
    gk = z(O_GK, O_GV)
    gk = gk * lax.rsqrt(_seg_meansq(gk, bd_ref, LANES) + EPS) * ggk_ref[...]
    gv = z(O_GV, O_RQ)
    if latent:
        gq = _rope(gq, _tile_lanes(cos64, 4), _tile_lanes(sin64, 4), GQA_HD // 4)
        gk = _rope(gk, cos64, sin64, GQA_HD // 4)
    else:
        gk_o[...] = gk
        gv_o[...] = gv
    gqo_ref[...] = (gq * (GQA_HD ** -0.5 * LOG2E)).astype(BF16)
    gkv_ref[:, 0:LANES] = gk.astype(BF16)
    gkv_ref[:, LANES:2 * LANES] = gv.astype(BF16)
    gkv_ref[:, 2 * LANES:3 * LANES] = jnp.ones(gv.shape, BF16)

    dq = z(O_DQ, O_DK)
    dk = z(O_DK, O_DV)
    dv = z(O_DV, O_GL)
    if latent:
        dq = _rope(dq, _tile_lanes(cos64, 4), _tile_lanes(sin64, 4), DIFF_D // 4)
        dk = _rope(dk, _tile_lanes(cos64, 4), _tile_lanes(sin64, 4), DIFF_D // 4)
    else:
        dk_o[...] = dk
        dv_o[...] = dv
    dqo_ref[...] = (dq * (DIFF_D ** -0.5 * LOG2E)).astype(BF16)
    dkv_ref[:, 0:512] = dk.astype(BF16)
    for h in range(DIFF_HEADS):
        dkv_ref[:, 512 + h * 256:512 + h * 256 + LANES] = dv[:, h * LANES:(h + 1) * LANES].astype(BF16)
        dkv_ref[:, 512 + h * 256 + LANES:512 + (h + 1) * 256] = jnp.ones((dv.shape[0], LANES), BF16)

    ret_ref[:, 0:256] = z(O_RQ, O_RK).astype(BF16)
    ret_ref[:, 256:512] = (z(O_RK, O_RV) * (RET_DK ** -0.5)).astype(BF16)
    ret_ref[:, 512:1024] = z(O_RV, O_RG).astype(BF16)
    rg_ref[...] = z(O_RG, O_DQ).astype(BF16)

    for n in range(N_BRANCH):
        gate_ref[:, n * D:(n + 1) * D] = jax.nn.sigmoid(
            z(O_GL + n * D, O_GL + (n + 1) * D)).astype(BF16)


def _inprep_call(latent, l, x, mod3, mod_row, lw, tabs, t_len):
    n_tok = x.shape[0]
    nblk = n_tok // TM
    blk_per_seq = t_len // TM

    def tok(w):
        return pl.BlockSpec((TM, w), lambda i: (i, 0))

    in_specs = [tok(D),
                pl.BlockSpec((None, 1, 6 * D), lambda i: (mod_row(i), 0, 0)),
                _const_spec((1, D)),
                pl.BlockSpec((None, O_END, D), lambda i: (l, 0, 0), pipeline_mode=pl.Buffered(1)),
                _const_spec((1, MLA_Q_LORA)), _const_spec((MLA_Q_LORA, 768)),
                _const_spec((1, MLA_KV_LORA)), _const_spec((MLA_KV_LORA, 1024)),
                _const_spec((1, 512)), _const_spec((1, LANES)), _const_spec((512, 512))]
    args = [x, mod3, lw["g_pre1"], lw["w_in"], lw["g_mla_q"], lw["w_uq"], lw["g_mla_kv"],
            lw["w_ukv"], lw["g_gqa_q"], lw["g_gqa_k"], lw["bd"]]
    if latent:
        tab_spec = pl.BlockSpec((TM, LANES), lambda i: (i % blk_per_seq, 0))
        in_specs += [tab_spec] * 4
        args += list(tabs)
    widths = [768, KVM_W, 512, GKV_W, 512, DKV_W, 1024, 512, 4 * D]
    out_specs = [tok(w) for w in widths]
    out_shape = [jax.ShapeDtypeStruct((n_tok, w), BF16) for w in widths]
    if not latent:
        cw = [MLA_KV_LORA, MLA_ROPE, 128, 128, 512, 512]
        out_specs += [tok(w) for w in cw]
        out_shape += [jax.ShapeDtypeStruct((n_tok, w), F32) for w in cw]
    return pl.pallas_call(
        functools.partial(_inprep_kernel, latent),
        grid=(nblk,),
        in_specs=in_specs, out_specs=out_specs, out_shape=out_shape,
        compiler_params=_cparams(("arbitrary",)),
        name="inprep_lat" if latent else "inprep_ctx",
    )(*args)


def _pastkv_kernel(ckv_ref, kpe_ref, wukv_ref, o_ref):
    kv = _dot(ckv_ref[...].astype(BF16), wukv_ref[...])
    _store_kvm(o_ref, kv, kpe_ref[...].astype(BF16))


def _pastkv_call(ckv, kpe4, w_ukv):
    n = ckv.shape[0]
    return pl.pallas_call(
        _pastkv_kernel,
        grid=(n // TM,),
        in_specs=[pl.BlockSpec((TM, MLA_KV_LORA), lambda i: (i, 0)),
                  pl.BlockSpec((TM, LANES), lambda i: (i, 0)),
                  _const_spec((MLA_KV_LORA, 1024))],
        out_specs=pl.BlockSpec((TM, KVM_W), lambda i: (i, 0)),
        out_shape=jax.ShapeDtypeStruct((n, KVM_W), BF16),
        compiler_params=_cparams(("arbitrary",)),
        name="pastkv",
    )(ckv, kpe4, w_ukv)


def _softmax_pv(s, v_ones):
    m = jnp.max(s, axis=-1, keepdims=True)
    p = jnp.exp2(s - m).astype(BF16)
    o = _dot(p, v_ones)
    return o[:, 0:LANES] / o[:, LANES:2 * LANES]


def _attn_kernel(lam_init, qm_ref, kvm_ref, gq_ref, gkv_ref, dq_ref, dkv_ref,
                 lam_ref, gdiff_ref, o_ref):
    tq = qm_ref.shape[0]
    lane = _lane_iota((tq, LANES))
    low = lane < HALF_LANES
    zero = jnp.zeros((tq, LANES), BF16)

    for p in range(MLA_HEADS // 2):
        qn = qm_ref[:, p * LANES:(p + 1) * LANES]
        g = p // 2
        qpe = qm_ref[:, 512 + g * LANES:512 + (g + 1) * LANES]
        kk = kvm_ref[:, p * 256:(p + 1) * 256]
        vv = kvm_ref[:, 1024 + p * 256:1024 + (p + 1) * 256]
        outs = []
        for half in range(2):
            h = 2 * p + half
            slot = h % 4
            in_slot = (lane >= slot * MLA_ROPE) & (lane < (slot + 1) * MLA_ROPE)
            lhs = jnp.concatenate(
                [jnp.where(low if half == 0 else ~low, qn, zero),
                 jnp.where(in_slot, qpe, zero)], axis=1)
            outs.append(_softmax_pv(_dot_nt(lhs, kk), vv))
        o_ref[:, p * LANES:(p + 1) * LANES] = jnp.where(low, outs[0], outs[1]).astype(BF16)

    kk = gkv_ref[:, 0:LANES]
    vv = gkv_ref[:, LANES:3 * LANES]
    for g in range(GQA_HEADS // 2):
        qg = gq_ref[:, g * LANES:(g + 1) * LANES]
        o_lo = _softmax_pv(_dot_nt(jnp.where(low, qg, zero), kk), vv)
        o_hi = _softmax_pv(_dot_nt(jnp.where(low, zero, qg), kk), vv)
        o_ref[:, 512 + g * LANES:512 + (g + 1) * LANES] = jnp.where(low, o_lo, o_hi).astype(BF16)

    lp = lam_ref[...]
    lam = (jnp.exp(jnp.sum(lp[0:1] * lp[1:2], axis=-1, keepdims=True))
           - jnp.exp(jnp.sum(lp[2:3] * lp[3:4], axis=-1, keepdims=True)) + lam_init)
    for h in range(DIFF_HEADS):
        qh = dq_ref[:, h * LANES:(h + 1) * LANES]
        kk = dkv_ref[:, h * LANES:(h + 1) * LANES]
        vv = dkv_ref[:, 512 + h * 256:512 + (h + 1) * 256]
        a1 =_softmax_pv(_dot_nt(jnp.where(low, qh, zero), kk), vv)
        a2 = _softmax_pv(_dot_nt(jnp.where(low, zero, qh), kk), vv)
        od = _rms(a1 - lam * a2, gdiff_ref[...]) * (1.0 - lam_init)
        o_ref[:, 1024 + h * LANES:1024 + (h + 1) * LANES] = od.astype(BF16)


def _attn_call(lam_init, qm, kvm, gq, gkv, dq, dkv, lam_p, g_diff, n_b, t_len, s_len):
    nq = t_len // TQ

    def qspec(w):
        return pl.BlockSpec((TQ, w), lambda b, i: (b * nq + i, 0))

    def kspec(w):
        return pl.BlockSpec((s_len, w), lambda b, i: (b, 0))

    return pl.pallas_call(
        functools.partial(_attn_kernel, lam_init),
        grid=(n_b, nq),
        in_specs=[qspec(768), kspec(KVM_W), qspec(512), kspec(GKV_W), qspec(512), kspec(DKV_W),
                  _const_spec((4, DIFF_D)), _const_spec((1, DIFF_DV))],
        out_specs=qspec(3 * BRANCH_W),
        out_shape=jax.ShapeDtypeStruct((n_b * t_len, 3 * BRANCH_W), BF16),
        compiler_params=_cparams(("arbitrary", "arbitrary")),
        name="attn",
    )(qm, kvm, gq, gkv, dq, dkv, lam_p, g_diff)


def _log_sigmoid(x):
    return jnp.minimum(x, 0.0) - jnp.log(1.0 + jnp.exp(-jnp.abs(x)))


def _log_gamma(dec_ref, d, h):
    return _log_sigmoid(jnp.full((1, 1), dec_ref[d, h], F32))


def _decay_kernel(t_len, dec_ref, mask_ref, kdec_ref):
    h, i = pl.program_id(0), pl.program_id(1)
    tq = mask_ref.shape[0]
    t_idx = (i * tq + lax.broadcasted_iota(jnp.int32, (tq, t_len), 0)).astype(F32)
    s_idx = lax.broadcasted_iota(jnp.int32, (tq, t_len), 1).astype(F32)
    dist = t_idx - s_idx
    lgf = _log_sigmoid(jnp.full((1, 1), dec_ref[0, h], F32))
    lgb = _log_sigmoid(jnp.full((1, 1), dec_ref[1, h], F32))
    mask_ref[...] = (jnp.where(dist >= 0, jnp.exp(lgf * jnp.maximum(dist, 0.0)), 0.0)
                     + jnp.where(dist <= 0, jnp.exp(lgb * jnp.maximum(-dist, 0.0)), 0.0))

    @pl.when((h == 0) & (i == 0))
    def _():
        s_col = lax.broadcasted_iota(jnp.int32, (t_len, 1), 0).astype(F32)
        lane_lo = _lane_iota((1, LANES)) < HALF_LANES
        for d in range(2):
            expo = (float(t_len) - 1.0 - s_col) if d == 0 else s_col
            for p in range(RET_HEADS // 2):
                lg_lane = jnp.where(lane_lo, _log_gamma(dec_ref, d, 2 * p),
                                    _log_gamma(dec_ref, d, 2 * p + 1))
                kdec_ref[d, p] = jnp.exp(lg_lane * expo)


def _decay_call(dec, t_len):
    nq = t_len // TQ
    return pl.pallas_call(
        functools.partial(_decay_kernel, t_len),
        grid=(RET_HEADS, nq),
        in_specs=[pl.BlockSpec(memory_space=pltpu.SMEM)],
        out_specs=[pl.BlockSpec((None, TQ, t_len), lambda h, i: (h, i, 0)),
                   _const_spec((2, RET_HEADS // 2, t_len, LANES))],
        out_shape=[jax.ShapeDtypeStruct((RET_HEADS, t_len, t_len), F32),
                   jax.ShapeDtypeStruct((2, RET_HEADS // 2, t_len, LANES), F32)],
        compiler_params=_cparams(("arbitrary", "arbitrary")),
        name="decay",
    )(dec)


def _ret_kernel(latent, t_len, dec_ref, q_ref, k_ref, v_ref, rg_ref, gret_ref, mask_ref, *refs):
    if latent:
        s0_ref, o_ref = refs
    else:
        kdec_ref, o_ref, st_ref = refs
    tq = q_ref.shape[0]
    t0 = pl.program_id(0) * tq
    lane = _lane_iota((tq, LANES))
    low = lane < HALF_LANES
    zero = jnp.zeros((tq, LANES), BF16)
    t_col = (t0 + lax.broadcasted_iota(jnp.int32, (tq, 1), 0)).astype(F32)

    for h in range(RET_HEADS):
        p, half = h // 2, h % 2
        qp = q_ref[:, p * LANES:(p + 1) * LANES]
        qm = jnp.where(low if half == 0 else ~low, qp, zero)
        kp = k_ref[:, p * LANES:(p + 1) * LANES]
        vh = v_ref[:, h * LANES:(h + 1) * LANES]
        o = _dot((_dot_nt(qm, kp) * mask_ref[h]).astype(BF16), vh)
        if latent:
            lgf, lgb = _log_gamma(dec_ref, 0, h), _log_gamma(dec_ref, 1, h)
            sf = s0_ref[0, p].astype(BF16)
            sb = s0_ref[1, p].astype(BF16)
            o = o + _dot(qm, sf) * jnp.exp(lgf * (t_col + 1.0))
            o = o + _dot(qm, sb) * jnp.exp(lgb * (float(t_len) - t_col))
        mu = jnp.mean(o, axis=-1, keepdims=True)
        oc = o - mu
        y = oc * lax.rsqrt(jnp.mean(oc * oc, axis=-1, keepdims=True) + EPS)
        y = y * gret_ref[:, h * LANES:(h + 1) * LANES]
        rg = rg_ref[:, h * LANES:(h + 1) * LANES].astype(F32)
        o_ref[:, h * LANES:(h + 1) * LANES] = (y * _silu(rg)).astype(BF16)

    if not latent:
        for p in range(RET_HEADS // 2):
            kp = k_ref[:, p * LANES:(p + 1) * LANES].astype(F32)
            for d in range(2):
                kdec_t = jnp.transpose(kp * kdec_ref[d, p]).astype(BF16)
                for half in range(2):
                    h = 2 * p + half
                    st = _dot(kdec_t, v_ref[:, h * LANES:(h + 1) * LANES])
                    st_ref[d, h] = st[half * RET_DK:(half + 1) * RET_DK, :]


def _ret_call(latent, dec, ret, rg, g_ret, s0, n_b, t_len):
    nq = t_len // TQ
    assert latent or nq == 1
    mask, kdec = _decay_call(dec, t_len)
    in_specs = [pl.BlockSpec(memory_space=pltpu.SMEM),
                pl.BlockSpec((TQ, 256), lambda i, b: (b * nq + i, 0)),
                pl.BlockSpec((t_len, 256), lambda i, b: (b, 1)),
                pl.BlockSpec((t_len, 512), lambda i, b: (b, 1)),
                pl.BlockSpec((TQ, 512), lambda i, b: (b * nq + i, 0)),
                _const_spec((1, 512)),
                pl.BlockSpec((RET_HEADS, TQ, t_len), lambda i, b: (0, i, 0))]
    args = [dec, ret, ret, ret, rg, g_ret, mask]
    out_specs = [pl.BlockSpec((TQ, 512), lambda i, b: (b * nq + i, 0))]
    out_shape = [jax.ShapeDtypeStruct((n_b * t_len, 512), BF16)]
    if latent:
        in_specs.append(pl.BlockSpec((None, 2, 2, LANES, LANES), lambda i, b: (b, 0, 0, 0, 0)))
        args.append(s0)
    else:
        in_specs.append(_const_spec((2, RET_HEADS // 2, t_len, LANES)))
        args.append(kdec)
        out_specs.append(pl.BlockSpec((None, 2, RET_HEADS, RET_DK, RET_DV),
                                      lambda i, b: (b, 0, 0, 0, 0)))
        out_shape.append(jax.ShapeDtypeStruct((n_b, 2, RET_HEADS, RET_DK, RET_DV), F32))
    return pl.pallas_call(
        functools.partial(_ret_kernel, latent, t_len),
        grid=(nq, n_b),
        in_specs=in_specs, out_specs=out_specs, out_shape=out_shape,
        compiler_params=_cparams(("arbitrary", "arbitrary")),
        name="ret_lat" if latent else "ret_ctx",
    )(*args)


def _merge_kernel(x_ref, mod_ref, br_ref, or_ref, gate_ref, wbr_ref, wout_ref, gpost_ref, o_ref):
    merged = None
    for n in range(N_BRANCH):
        if n < 2:
            b = br_ref[:, n * BRANCH_W:(n + 1) * BRANCH_W]
        elif n == 2:
            b = or_ref[...]
        else:
            b = br_ref[:, 2 * BRANCH_W:3 * BRANCH_W]
        t = gate_ref[:, n * D:(n + 1) * D].astype(F32) * _dot(b, wbr_ref[n])
        merged = t if merged is None else merged + t
    out = _dot(merged.astype(BF16), wout_ref[...])
    g1 = mod_ref[...][:, 2 * D:3 * D]
    o_ref[...] = x_ref[...] + g1 * _rms(out, gpost_ref[...])


def _merge_call(x, mod3, mod_row, br, o_r, gates, lw):
    n_tok = x.shape[0]

    def tok(w):
        return pl.BlockSpec((TM, w), lambda i: (i, 0))

    return pl.pallas_call(
        _merge_kernel,
        grid=(n_tok // TM,),
        in_specs=[tok(D), pl.BlockSpec((None, 1, 6 * D), lambda i: (mod_row(i), 0, 0)),
                  tok(3 * BRANCH_W), tok(BRANCH_W), tok(4 * D),
                  _const_spec((N_BRANCH, BRANCH_W, D)), _const_spec((D, D)), _const_spec((1, D))],
        out_specs=tok(D),
        out_shape=jax.ShapeDtypeStruct((n_tok, D), F32),
        compiler_params=_cparams(("arbitrary",)),
        name="merge",
    )(x, mod3, br, o_r, gates, lw["w_br"], lw["w_out"], lw["g_post1"])


def _route(logits_t, bias):
    n = logits_t.shape[1]
    scores = jax.nn.sigmoid(logits_t)
    sel = scores + bias
    neg = -jnp.inf
    sub = lax.broadcasted_iota(jnp.int32, (GROUP_SIZE, n), 0)
    grp = []
    for g in range(N_GROUPS):
        blk = sel[g * GROUP_SIZE:(g + 1) * GROUP_SIZE]
        m1 = jnp.max(blk, axis=0, keepdims=True)
        i1 = jnp.min(jnp.where(blk == m1, sub, GROUP_SIZE), axis=0, keepdims=True)
        m2 = jnp.max(jnp.where(sub == i1, neg, blk), axis=0, keepdims=True)
        grp.append(m1 + m2)
    parts = []
    for g in range(N_GROUPS):
        beaten = jnp.zeros((1, n), jnp.int32)
        for o in range(N_GROUPS):
            if o == g:
                continue
            wins = (grp[o] > grp[g]) | (grp[o] == grp[g]) if o < g else (grp[o] > grp[g])
            beaten = beaten + wins.astype(jnp.int32)
        keep = beaten < TOPK_GROUPS
        parts.append(jnp.where(keep, sel[g * GROUP_SIZE:(g + 1) * GROUP_SIZE], neg))
    cur = jnp.concatenate(parts, axis=0)
    eidx = lax.broadcasted_iota(jnp.int32, (N_EXPERTS, n), 0)
    hits, ids, ws = [], [], []
    for _ in range(TOP_K):
        m = jnp.max(cur, axis=0, keepdims=True)
        i = jnp.min(jnp.where(cur == m, eidx, N_EXPERTS), axis=0, keepdims=True)
        hit = eidx == i
        hits.append(hit)
        ids.append(i)
        ws.append(jnp.sum(jnp.where(hit, scores, 0.0), axis=0, keepdims=True))
        cur = jnp.where(hit, neg, cur)
    wsum = ws[0] + ws[1] + ws[2] + ws[3]
    return hits, ids, [w / wsum * ROUTE_SCALE for w in ws]


U32 = jnp.uint32
HIGH16 = np.uint32(0xFFFF0000)


def _bf16_bits(v):
    return lax.bitcast_convert_type(v.astype(BF16).astype(F32), U32)


def _pack_rows(v):
    return (_bf16_bits(v[:, 0:D // 2]) >> 16) | _bf16_bits(v[:, D // 2:D])


def _unpack_rows(p):
    lo = lax.bitcast_convert_type(p << 16, F32)
    hi = lax.bitcast_convert_type(p & HIGH16, F32)
    return jnp.concatenate([lo, hi], axis=1)


def _moe_pre_kernel(x_ref, mod_ref, gpre_ref, wr_ref, br_ref, tri_ref,
                    hp_ref, eidx_ref, rank_ref, comb_ref, cnt_ref, run_ref):
    tm = x_ref.shape[0]

    @pl.when(pl.program_id(0) == 0)
    def _():
        run_ref[...] = jnp.zeros_like(run_ref)

    mod = mod_ref[...]
    sh2, sc2 = mod[:, 3 * D:4 * D], mod[:, 4 * D:5 * D]
    h = _rms(x_ref[...], gpre_ref[...]) * (1.0 + sc2) + sh2
    hp_ref[...] = _pack_rows(h)
    hb = h.astype(BF16)
    h_lo = (h - hb.astype(F32)).astype(BF16)
    wr = wr_ref[...]
    wr_hi = wr.astype(BF16)
    wr_lo = (wr - wr_hi.astype(F32)).astype(BF16)
    logits_t = _dot_nt(wr_hi, hb) + _dot_nt(wr_hi, h_lo) + _dot_nt(wr_lo, hb)
    hits, ids, ws = _route(logits_t, br_ref[...])

    picked = jnp.zeros((N_EXPERTS, tm), F32)
    for hit in hits:
        picked = jnp.where(hit, 1.0, picked)
    before = _dot(picked.astype(BF16), tri_ref[...]) + run_ref[:, 0:1]
    sub8 = lax.broadcasted_iota(jnp.int32, (8, tm), 0)
    comb8 = jnp.zeros((8, tm), F32)
    for k in range(TOP_K):
        rank = jnp.sum(jnp.where(hits[k], before, 0.0), axis=0, keepdims=True)
        eidx_ref[k:k + 1, :] = ids[k]
        rank_ref[k:k + 1, :] = rank.astype(jnp.int32)
        comb8 = jnp.where(sub8 == k, ws[k], comb8)
    comb_ref[...] = jnp.transpose(
        jnp.concatenate([comb8, jnp.zeros((LANES - 8, tm), F32)], axis=0))
    run_ref[...] = run_ref[...] + jnp.sum(picked, axis=1, keepdims=True)
    cnt_ref[...] = run_ref[...]


def _moe_pre_call(x, mod3, mod_row, lw):
    n_tok = x.shape[0]
    tm = TM_MOE_PRE
    tri = np.arange(tm)
    tri = jnp.asarray(tri[:, None] < tri[None, :], BF16)
    row4 = pl.BlockSpec((TOP_K, tm), lambda i: (0, i))
    return pl.pallas_call(
        _moe_pre_kernel,
        grid=(n_tok // tm,),
        in_specs=[pl.BlockSpec((tm, D), lambda i: (i, 0)),
                  pl.BlockSpec((None, 1, 6 * D), lambda i: (mod_row(i), 0, 0)),
                  _const_spec((1, D)), _const_spec((N_EXPERTS, D)), _const_spec((N_EXPERTS, 1)),
                  _const_spec((tm, tm))],
        out_specs=[pl.BlockSpec((tm, D // 2), lambda i: (i, 0)), row4, row4,
                   pl.BlockSpec((tm, LANES), lambda i: (i, 0)),
                   _const_spec((N_EXPERTS, LANES))],
        out_shape=[jax.ShapeDtypeStruct((n_tok, D // 2), U32),
                   jax.ShapeDtypeStruct((TOP_K, n_tok), jnp.int32),
                   jax.ShapeDtypeStruct((TOP_K, n_tok), jnp.int32),
                   jax.ShapeDtypeStruct((n_tok, LANES), F32),
                   jax.ShapeDtypeStruct((N_EXPERTS, LANES), F32)],
        scratch_shapes=[pltpu.VMEM((N_EXPERTS, LANES), F32)],
        compiler_params=_cparams(("arbitrary",)),
        name="moe_pre",
    )(x, mod3, lw["g_pre2"], lw["w_router_t"], lw["b_router"], tri)


def _moe_plan_kernel(eidx_ref, rank_ref, cnt_ref, dest_ref, te_ref, tv_ref, tn_ref):
    tm = eidx_ref.shape[1]
    cnt = cnt_ref[...]
    padded = jnp.ceil(cnt * (1.0 / TMX)) * TMX
    row = lax.broadcasted_iota(jnp.int32, cnt.shape, 0)
    incl = padded
    shift = 1
    while shift < N_EXPERTS:
        incl = incl + jnp.where(row >= shift, pltpu.roll(incl, shift, 0), 0.0)
        shift *= 2
    start = (incl - padded)[:, 0:1]
    end = incl[:, 0:1]
    erow = lax.broadcasted_iota(jnp.int32, (N_EXPERTS, tm), 0)
    for k in range(TOP_K):
        mine = erow == eidx_ref[k:k + 1, :]
        base = jnp.sum(jnp.where(mine, start, 0.0), axis=0, keepdims=True)
        dest_ref[k:k + 1, :] = rank_ref[k:k + 1, :] + base.astype(jnp.int32)

    @pl.when(pl.program_id(0) == 0)
    def _():
        tile0 = (_lane_iota((1, LANES)) * TMX).astype(F32)
        owner = jnp.sum(jnp.where(end <= tile0, 1.0, 0.0), axis=0, keepdims=True)
        owner = jnp.minimum(owner, N_EXPERTS - 1.0)
        erow_t = lax.broadcasted_iota(jnp.int32, (N_EXPERTS, LANES), 0).astype(F32)
        left = jnp.sum(jnp.where(erow_t == owner, cnt[:, 0:1] - (tile0 - start), 0.0),
                       axis=0, keepdims=True)
        te_ref[...] = owner.astype(jnp.int32)
        tv_ref[...] = jnp.clip(left, 0.0, float(TMX)).astype(jnp.int32)
        tn_ref[...] = jnp.full(tn_ref.shape, N_EXPERTS, jnp.int32)
        nxt = owner
        for k in range(W_SLOTS - 1):
            later = (erow_t > nxt) & (cnt[:, 0:1] > 0.0)
            nxt = jnp.min(jnp.where(later, erow_t, float(N_EXPERTS)), axis=0, keepdims=True)
            tn_ref[k:k + 1, :] = nxt.astype(jnp.int32)


def _moe_plan_call(eidx, rank, cnt):
    n_tok = eidx.shape[1]
    tm = TM_MOE_PRE
    row4 = pl.BlockSpec((TOP_K, tm), lambda i: (0, i))
    tiles = jax.ShapeDtypeStruct((1, LANES), jnp.int32)
    return pl.pallas_call(
        _moe_plan_kernel,
        grid=(n_tok // tm,),
        in_specs=[row4, row4, _const_spec((N_EXPERTS, LANES))],
        out_specs=[row4, _const_spec((1, LANES)), _const_spec((1, LANES)), _const_spec((8, LANES))],
        out_shape=[jax.ShapeDtypeStruct((TOP_K, n_tok), jnp.int32), tiles, tiles,
                   jax.ShapeDtypeStruct((8, LANES), jnp.int32)],
        compiler_params=_cparams(("arbitrary",)),
        name="moe_plan",
    )(eidx, rank, cnt)


def _experts_kernel(l, te_ref, tv_ref, tn_ref, xs_ref, wgu_hbm, wdn_hbm, ys_ref,
                    wgu_f, wdn_f, wgu_b, wdn_b, sem, group_ref):
    j = pl.program_id(0)
    valid = tv_ref[j]
    expert = te_ref[j]

    def fetch(e, slot):
        return (pltpu.make_async_copy(wgu_hbm.at[l, e], wgu_f.at[slot], sem.at[slot, 0]),
                pltpu.make_async_copy(wdn_hbm.at[l, e], wdn_f.at[slot], sem.at[slot, 1]))

    def start_if_any(e, slot):
        @pl.when(e < N_EXPERTS)
        def _():
            for cp in fetch(e, slot):
                cp.start()

    @pl.when(j == 0)
    def _():
        group_ref[0] = 0
        start_if_any(expert, 0)
        for k in range(W_SLOTS - 2):
            start_if_any(tn_ref[k, 0], k + 1)

    first_tile = (j == 0) | (expert != te_ref[jnp.maximum(j - 1, 0)])

    @pl.when(first_tile & (valid > 0))
    def _():
        group = group_ref[0]
        slot = lax.rem(group, W_SLOTS)
        for cp in fetch(expert, slot):
            cp.wait()
        wgu_b[...] = wgu_f[slot].astype(BF16)
        wdn_b[...] = wdn_f[slot].astype(BF16)
        start_if_any(tn_ref[W_SLOTS - 2, j], lax.rem(group + W_SLOTS - 1, W_SLOTS))
        group_ref[0] = group + 1

    @pl.when(valid > 0)
    def _():
        rows = lax.broadcasted_iota(jnp.int32, (TMX, D), 0)
        x = jnp.where(rows < valid, _unpack_rows(xs_ref[...]), 0.0).astype(BF16)
        gu = _dot(x, wgu_b[...])
        a = _silu(gu[:, 0:EXPERT_FF]) * gu[:, EXPERT_FF:2 * EXPERT_FF]
        ys_ref[...] = _pack_rows(_dot(a.astype(BF16), wdn_b[...]))

    @pl.when(valid <= 0)
    def _():
        ys_ref[...] = jnp.zeros_like(ys_ref)


def _experts_call(l, xs, te, tv, tn, w_gu, w_dn):
    n_tiles = xs.shape[0] // TMX
    grid_spec = pltpu.PrefetchScalarGridSpec(
        num_scalar_prefetch=3,
        grid=(n_tiles,),
        in_specs=[pl.BlockSpec((TMX, D // 2), lambda j, *_: (j, 0)),
                  pl.BlockSpec(memory_space=pl.ANY), pl.BlockSpec(memory_space=pl.ANY)],
        out_specs=pl.BlockSpec((TMX, D // 2), lambda j, *_: (j, 0)),
        scratch_shapes=[pltpu.VMEM((W_SLOTS, D, 2 * EXPERT_FF), F32),
                        pltpu.VMEM((W_SLOTS, EXPERT_FF, D), F32),
                        pltpu.VMEM((D, 2 * EXPERT_FF), BF16), pltpu.VMEM((EXPERT_FF, D), BF16),
                        pltpu.SemaphoreType.DMA((W_SLOTS, 2)), pltpu.SMEM((1,), jnp.int32)])
    return pl.pallas_call(
        functools.partial(_experts_kernel, l),
        grid_spec=grid_spec,
        out_shape=jax.ShapeDtypeStruct(xs.shape, U32),
        compiler_params=_cparams(("arbitrary",)),
        name="moe_experts",
    )(te, tv, tn, xs, w_gu, w_dn)


def _moe_post_kernel(x_ref, mod_ref, hp_ref, yg_ref, comb_ref, wsgu_ref, wsdn_ref, gpost_ref,
                     o_ref):
    hb = _unpack_rows(hp_ref[...]).astype(BF16)
    sgu = _dot(hb, wsgu_ref[...])
    sa = _silu(sgu[:, 0:SHARED_FF]) * sgu[:, SHARED_FF:2 * SHARED_FF]
    acc = _dot(sa.astype(BF16), wsdn_ref[...])
    comb = comb_ref[...]
    for k in range(TOP_K):
        acc = acc + comb[:, k:k + 1] * _unpack_rows(yg_ref[k])
    g2 = mod_ref[...][:, 5 * D:6 * D]
    o_ref[...] = x_ref[...] + g2 * _rms(acc, gpost_ref[...])


def _moe_post_call(x, mod3, mod_row, hp, yg, comb, lw):
    n_tok = x.shape[0]
    tm = TM_MOE_PRE
    return pl.pallas_call(
        _moe_post_kernel,
        grid=(n_tok // tm,),
        in_specs=[pl.BlockSpec((tm, D), lambda i: (i, 0)),
                  pl.BlockSpec((None, 1, 6 * D), lambda i: (mod_row(i), 0, 0)),
                  pl.BlockSpec((tm, D // 2), lambda i: (i, 0)),
                  pl.BlockSpec((TOP_K, tm, D // 2), lambda i: (0, i, 0)),
                  pl.BlockSpec((tm, LANES), lambda i: (i, 0)),
                  _const_spec((D, 2 * SHARED_FF)), _const_spec((SHARED_FF, D)),
                  _const_spec((1, D))],
        out_specs=pl.BlockSpec((tm, D), lambda i: (i, 0)),
        out_shape=jax.ShapeDtypeStruct((n_tok, D), F32),
        compiler_params=_cparams(("arbitrary",)),
        name="moe_post",
    )(x, mod3, hp, yg, comb, lw["w_sh_gu"], lw["w_sh_down"], lw["g_post2"])


def _moe_call(l, x, mod3, mod_row, lw):
    n_tok = x.shape[0]
    n_slots = -(-(TOP_K * n_tok + N_EXPERTS * (TMX - 1)) // TMX) * TMX
    assert n_slots // TMX <= LANES
    hp, eidx, rank, comb, cnt = _moe_pre_call(x, mod3, mod_row, lw)
    dest, te, tv, tn = _moe_plan_call(eidx, rank, cnt)
    dest = dest.reshape(TOP_K * n_tok)
    xs = _sc_scatter_rows(hp, dest, n_slots)
    ys = _experts_call(l, xs, te[0], tv[0], tn, lw["w_exp_gu"], lw["w_exp_down"])
    yg = _sc_gather_rows(ys, dest).reshape(TOP_K, n_tok, D // 2)
    return _moe_post_call(x, mod3, mod_row, hp, yg, comb, lw)


SC_CORES, SC_SUBCORES = 2, 16
SC_WORKERS = SC_CORES * SC_SUBCORES


def _sc_gather_rows(table, idx, chunk=64):
    n_out, width = idx.shape[0], table.shape[1]
    per_worker = n_out // SC_WORKERS
    n_chunks = per_worker // chunk
    assert per_worker * SC_WORKERS == n_out and n_chunks * chunk == per_worker
    mesh = plsc.VectorSubcoreMesh(core_axis_name="c", subcore_axis_name="s",
                                  num_cores=SC_CORES, num_subcores=SC_SUBCORES)

    @functools.partial(
        pl.kernel, mesh=mesh,
        out_type=jax.ShapeDtypeStruct((n_out, width), table.dtype),
        scratch_types=[pltpu.VMEM((chunk,), jnp.int32), pltpu.VMEM((chunk, width), table.dtype),
                       pltpu.SemaphoreType.DMA],
        name="sc_gather")
    def gather(table_hbm, idx_hbm, out_hbm, idx_v, rows_v, sem):
        base = (lax.axis_index("s") * SC_CORES + lax.axis_index("c")) * per_worker

        @pl.loop(0, n_chunks)
        def _(j):
            off = base + j * chunk
            pltpu.sync_copy(idx_hbm.at[pl.ds(off, chunk)], idx_v)
            pltpu.async_copy(table_hbm.at[idx_v], rows_v, sem).wait()
            pltpu.sync_copy(rows_v, out_hbm.at[pl.ds(off, chunk)])

    return gather(table, idx)


def _sc_scatter_rows(rows, dest, n_slots, chunk=64):
    n_tok, width = rows.shape
    per_worker = n_tok // SC_WORKERS
    n_chunks = per_worker // chunk
    assert per_worker * SC_WORKERS == n_tok and n_chunks * chunk == per_worker
    mesh = plsc.VectorSubcoreMesh(core_axis_name="c", subcore_axis_name="s",
                                  num_cores=SC_CORES, num_subcores=SC_SUBCORES)

    @functools.partial(
        pl.kernel, mesh=mesh,
        out_type=jax.ShapeDtypeStruct((n_slots, width), rows.dtype),
        scratch_types=[pltpu.VMEM((chunk,), jnp.int32), pltpu.VMEM((chunk, width), rows.dtype)],
        name="sc_scatter")
    def scatter(rows_hbm, dest_hbm, out_hbm, idx_v, rows_v):
        base = (lax.axis_index("s") * SC_CORES + lax.axis_index("c")) * per_worker

        @pl.loop(0, n_chunks)
        def _(j):
            off = base + j * chunk
            pltpu.sync_copy(rows_hbm.at[pl.ds(off, chunk)], rows_v)
            for k in range(TOP_K):
                pltpu.sync_copy(dest_hbm.at[pl.ds(k * n_tok + off, chunk)], idx_v)
                pltpu.sync_copy(rows_v, out_hbm.at[idx_v])

    return scatter(rows, dest)


def _rope_tables(t_len):
    pos = np.arange(t_len)
    row, col = pos // GRID_W, pos % GRID_W

    def tab(r):
        half = r // 2
        freq = ROPE_BASE ** (-np.arange(half, dtype=np.float64) / half)
        sign = np.concatenate([-np.ones(half), np.ones(half)])
        cs, sn = [], []
        for p in (row, col):
            ang = p[:, None].astype(np.float64) * freq[None, :]
            cs.append(np.concatenate([np.cos(ang), np.cos(ang)], axis=1))
            sn.append(np.concatenate([np.sin(ang), np.sin(ang)], axis=1) * sign[None, :])
        return np.concatenate(cs, axis=1), np.concatenate(sn, axis=1)

    c64, s64 = tab(GQA_HD // 2)
    cpe, spe = tab(MLA_ROPE // 2)
    out = (np.tile(c64, (1, 2)), np.tile(s64, (1, 2)), np.tile(cpe, (1, 4)), np.tile(spe, (1, 4)))
    return tuple(jnp.asarray(a, F32) for a in out)


def _layer_weights(l, p):
    w_uq = p["w_mla_uq"][l].reshape(MLA_Q_LORA, MLA_HEADS, MLA_NOPE + MLA_ROPE)
    w_uq = jnp.concatenate([w_uq[:, :, :MLA_NOPE].reshape(MLA_Q_LORA, -1),
                            w_uq[:, :, MLA_NOPE:].reshape(MLA_Q_LORA, -1)], axis=1)
    w_ukv = p["w_mla_ukv"][l].reshape(MLA_KV_LORA, MLA_HEADS, MLA_NOPE + MLA_V)
    w_ukv = jnp.concatenate([w_ukv[:, :, :MLA_NOPE].reshape(MLA_KV_LORA, -1),
                             w_ukv[:, :, MLA_NOPE:].reshape(MLA_KV_LORA, -1)], axis=1)
    w_br = p["w_br"][l]
    w_br_gqa = w_br[1].reshape(GQA_HEADS, GQA_HD, D)[jnp.array(GQA_ORDER)].reshape(BRANCH_W, D)
    w_br = jnp.stack([w_br[0], w_br_gqa, w_br[2], w_br[3]], axis=0)
    blk = np.arange(512) // GQA_HD
    return {
        "g_pre1": p["g_pre1"][l].reshape(1, D), "g_post1": p["g_post1"][l].reshape(1, D),
        "g_pre2": p["g_pre2"][l].reshape(1, D), "g_post2": p["g_post2"][l].reshape(1, D),
        "w_in": p["w_in_packed"],
        "g_mla_q": p["g_mla_q"][l].reshape(1, -1), "w_uq": w_uq.astype(BF16),
        "g_mla_kv": p["g_mla_kv"][l].reshape(1, -1), "w_ukv": w_ukv.astype(BF16),
        "g_gqa_q": jnp.tile(p["g_gqa_q"][l], GQA_HEADS).reshape(1, -1),
        "g_gqa_k": jnp.tile(p["g_gqa_k"][l], GQA_KV_HEADS).reshape(1, -1),
        "bd": jnp.asarray(blk[:, None] == blk[None, :], BF16),
        "ret_decay": p["ret_decay"][l],
        "g_ret": p["g_ret"][l].reshape(1, -1),
        "diff_lambda": p["diff_lambda"][l], "g_diff": p["g_diff"][l].reshape(1, -1),
        "w_br": w_br.astype(BF16), "w_out": p["w_out"][l].astype(BF16),
        "w_router_t": p["w_router"][l].T, "b_router": p["b_router"][l].reshape(-1, 1),
        "w_exp_gu": p["w_exp_gu"], "w_exp_down": p["w_exp_down"],
        "w_sh_gu": p["w_sh_gu"][l].astype(BF16), "w_sh_down": p["w_sh_down"][l].astype(BF16),
    }


def _mixers(latent, l, x, mod3, mod_row, lw, n_b, t_len, tabs=None, past=None, s0=None):
    lam_init = 0.8 - 0.6 * math.exp(-0.3 * l)
    outs = _inprep_call(latent, l, x, mod3, mod_row, lw, tabs, t_len)
    qm, kvm, gq, gkv, dq, dkv, ret, rg, gates = outs[:9]
    s_len = t_len
    if latent:
        past_kvm, past_gkv, past_dkv = past
        p_len = past_gkv.shape[1]
        s_len = p_len + t_len

        def cat(a, b):
            return jnp.concatenate([a, b.reshape(n_b, t_len, -1)], axis=1).reshape(n_b * s_len, -1)

        kvm, gkv, dkv = cat(past_kvm, kvm), cat(past_gkv, gkv), cat(past_dkv, dkv)
    br = _attn_call(lam_init, qm, kvm, gq, gkv, dq, dkv, lw["diff_lambda"], lw["g_diff"],
                    n_b, t_len, s_len)
    r = _ret_call(latent, lw["ret_decay"], ret, rg, lw["g_ret"], s0, n_b, t_len)
    y = _merge_call(x, mod3, mod_row, br, r[0], gates, lw)
    cache = None if latent else tuple(outs[9:]) + (r[1],)
    return y, cache


def kernel(x_prompt, x_sample, cache_mla_ckv, cache_mla_kpe, cache_gqa_k, cache_gqa_v, cache_diff_k, cache_diff_v, state_ret, c, c_ctx, w_mod, b_mod, g_pre1, g_post1, g_pre2, g_post2, w_in, g_mla_q, w_mla_uq, g_mla_kv, w_mla_ukv, g_gqa_q, g_gqa_k, ret_decay, g_ret, diff_lambda, g_diff, w_br, w_out, w_router, b_router, w_exp_gu, w_exp_down, w_sh_gu, w_sh_down):
    params = dict(w_in_packed=jnp.swapaxes(w_in, 1, 2).astype(BF16), g_pre1=g_pre1, g_post1=g_post1, g_pre2=g_pre2,
                  g_post2=g_post2, g_mla_q=g_mla_q, w_mla_uq=w_mla_uq,
                  g_mla_kv=g_mla_kv, w_mla_ukv=w_mla_ukv, g_gqa_q=g_gqa_q, g_gqa_k=g_gqa_k,
                  ret_decay=ret_decay, g_ret=g_ret, diff_lambda=diff_lambda, g_diff=g_diff,
                  w_br=w_br, w_out=w_out, w_router=w_router, b_router=b_router,
                  w_exp_gu=w_exp_gu, w_exp_down=w_exp_down, w_sh_gu=w_sh_gu, w_sh_down=w_sh_down)
    n_bc, t_c, _ = x_prompt.shape
    n_bl, t_l, _ = x_sample.shape
    p_len = cache_mla_ckv.shape[2]
    tabs = _rope_tables(t_l)
    n_cond = 8
    cond = jnp.concatenate([c_ctx[None, :], c, jnp.zeros((n_cond - 1 - n_bl, D), F32)], axis=0)
    blk_c, blk_l = t_c // TM, t_l // TM
    assert t_l % TM_MOE_PRE == 0 and (t_c * n_bc) % TM_MOE_PRE == 0

    yp = x_prompt.reshape(n_bc * t_c, D)
    ys = x_sample.reshape(n_bl * t_l, D)
    caches = []
    for l in range(DEPTH):
        lw = _layer_weights(l, params)
        mod3 = _mod_call(l, cond, w_mod, b_mod).reshape(n_cond, 1, 6 * D)
        yp, cache = _mixers(False, l, yp, mod3, lambda i: 0, lw, n_bc, t_c)
        yp = _moe_call(l, yp, mod3, lambda i: 0, lw)
        caches.append(cache)
        past_kvm = _pastkv_call(cache_mla_ckv[:, l].reshape(n_bl * p_len, -1),
                                jnp.tile(cache_mla_kpe[:, l].reshape(n_bl * p_len, -1), (1, 4)),
                                lw["w_ukv"]).reshape(n_bl, p_len, -1)
        past_gkv = jnp.concatenate([cache_gqa_k[:, l].reshape(n_bl, p_len, -1),
                                    cache_gqa_v[:, l].reshape(n_bl, p_len, -1),
                                    jnp.ones((n_bl, p_len, LANES), F32)], axis=-1).astype(BF16)
        past_dv = jnp.concatenate([cache_diff_v[:, l], jnp.ones_like(cache_diff_v[:, l])], axis=-1)
        past_dkv = jnp.concatenate([cache_diff_k[:, l].reshape(n_bl, p_len, -1),
                                    past_dv.reshape(n_bl, p_len, -1)], axis=-1).astype(BF16)
        s0 = state_ret[:, l].reshape(n_bl, 2, RET_HEADS // 2, 2 * RET_DK, RET_DV)
        ys, _ = _mixers(True, l, ys, mod3, lambda i: 1 + i // blk_l, lw, n_bl, t_l, tabs=tabs,
                        past=(past_kvm, past_gkv, past_dkv), s0=s0)
        ys = _moe_call(l, ys, mod3, lambda i: 1 + i // (t_l // TM_MOE_PRE), lw)

    def stack(k, shape):
        return jnp.stack([caches[l][k].reshape((n_bc, t_c) + shape) for l in range(DEPTH)], axis=1)

    new_ret = jnp.stack([caches[l][6] for l in range(DEPTH)], axis=1)
    return (yp.reshape(n_bc, t_c, D), ys.reshape(n_bl, t_l, D),
            stack(0, (MLA_KV_LORA,)), stack(1, (MLA_ROPE,)),
            stack(2, (GQA_KV_HEADS, GQA_HD)), stack(3, (GQA_KV_HEADS, GQA_HD)),
            stack(4, (DIFF_HEADS, 2, DIFF_D)), stack(5, (DIFF_HEADS, DIFF_DV)), new_ret)
```

```python
import functools
import math

import numpy as np
import jax
import jax.numpy as jnp
from jax import lax
from jax.experimental import pallas as pl
from jax.experimental.pallas import tpu as pltpu
from jax.experimental.pallas import tpu_sc as plsc

F32 = jnp.float32
BF16 = jnp.bfloat16

D = 1024
DEPTH = 2
GRID_W = 64
ROPE_BASE = 10000.0
EPS = 1e-6

MLA_HEADS, MLA_NOPE, MLA_ROPE, MLA_V = 8, 64, 32, 64
MLA_Q_LORA, MLA_KV_LORA = 384, 256
GQA_HEADS, GQA_KV_HEADS, GQA_HD = 8, 2, 64
RET_HEADS, RET_DK, RET_DV = 4, 64, 128
DIFF_HEADS, DIFF_D, DIFF_DV = 4, 64, 128
N_BRANCH, BRANCH_W = 4, 512
N_EXPERTS, TOP_K, N_GROUPS, TOPK_GROUPS = 32, 4, 4, 2
EXPERT_FF, SHARED_FF = 256, 256
ROUTE_SCALE = 2.5
GROUP_SIZE = N_EXPERTS // N_GROUPS

LANES = 128
HALF_LANES = 64
VMEM_LIMIT = 56 * 1024 * 1024

C_CQ, C_CKV, C_KPE, C_GQ, C_GK, C_GV = 0, 384, 640, 768, 1280, 1408
C_DQ, C_DK, C_DV, C_RQ, C_RK, C_RV, C_RG, C_GL, C_END = (
    1536, 2048, 2560, 3072, 3328, 3584, 4096, 4608, 8704)
O_CQ, O_CKV, O_KPE, O_GQ, O_GK, O_GV = 0, 384, 640, 672, 1184, 1312
O_RQ, O_RK, O_RV, O_RG, O_DQ, O_DK, O_DV, O_GL, O_END = (
    1440, 1696, 1952, 2464, 2976, 3488, 4000, 4512, 8608)
GQA_ORDER = (0, 4, 1, 5, 2, 6, 3, 7)

KVM_W = 8 * 256
GKV_W = 3 * LANES
DKV_W = 512 + 4 * 256
LOG2E = 1.4426950408889634
TM = 256
TQ = 256
TM_MOE_PRE = 512
TMX = 256
W_SLOTS = 4


def _cparams(sem):
    return pltpu.CompilerParams(dimension_semantics=sem, vmem_limit_bytes=VMEM_LIMIT)


def _const_spec(shape):
    nd = len(shape)
    return pl.BlockSpec(shape, lambda *_: (0,) * nd)


def _rms(x, g):
    return x * lax.rsqrt(jnp.mean(x * x, axis=-1, keepdims=True) + EPS) * g


def _dot(a, b):
    return jnp.dot(a, b, preferred_element_type=F32)


def _dot_nt(a, b):
    return lax.dot_general(a, b, (((1,), (1,)), ((), ())), preferred_element_type=F32)


def _silu(x):
    return x * jax.nn.sigmoid(x)


def _lane_iota(shape):
    return lax.broadcasted_iota(jnp.int32, shape, len(shape) - 1)


def _seg_meansq(x, bd_ref, width):
    sq = x * x
    hi = sq.astype(BF16)
    lo = (sq - hi.astype(F32)).astype(BF16)
    bd = bd_ref[0:width, 0:width]
    return (_dot(hi, bd) + _dot(lo, bd)) * (1.0 / GQA_HD)


def _rope(x, cos, sin_signed, half):
    width = x.shape[-1]
    first = (_lane_iota(x.shape) % (2 * half)) < half
    partner = jnp.where(first, pltpu.roll(x, width - half, 1), pltpu.roll(x, half, 1))
    return x * cos + partner * sin_signed


def _tile_lanes(t, reps):
    return t if reps == 1 else jnp.concatenate([t] * reps, axis=1)


def _store_kvm(kvm_ref, kv, kpe_b):
    ones = jnp.ones(kpe_b.shape, BF16)
    for p in range(4):
        kvm_ref[:, p * 256:p * 256 + LANES] = kv[:, p * LANES:(p + 1) * LANES].astype(BF16)
        kvm_ref[:, p * 256 + LANES:(p + 1) * 256] = kpe_b
        kvm_ref[:, 1024 + p * 256:1024 + p * 256 + LANES] = (
            kv[:, 512 + p * LANES:512 + (p + 1) * LANES].astype(BF16))
        kvm_ref[:, 1024 + p * 256 + LANES:1024 + (p + 1) * 256] = ones


def _mod_kernel(c_ref, w_ref, b_ref, o_ref):
    a = _silu(c_ref[...]).astype(BF16)
    o_ref[...] = _dot(a, w_ref[...].astype(BF16)) + b_ref[...]


def _mod_call(l, cond, w_mod, b_mod):
    n_l, _, n = w_mod.shape
    tn = 1536
    return pl.pallas_call(
        _mod_kernel,
        grid=(n // tn,),
        in_specs=[_const_spec(cond.shape),
                  pl.BlockSpec((None, D, tn), lambda j: (l, 0, j)),
                  pl.BlockSpec((None, 1, tn), lambda j: (l, 0, j))],
        out_specs=pl.BlockSpec((cond.shape[0], tn), lambda j: (0, j)),
        out_shape=jax.ShapeDtypeStruct((cond.shape[0], n), F32),
        compiler_params=_cparams(("arbitrary",)),
        name="mod",
    )(cond, w_mod, b_mod.reshape(n_l, 1, n))


def _inprep_kernel(latent, *refs):
    (x_ref, mod_ref, gpre_ref, win_ref, gmq_ref, wuq_ref, gmkv_ref, wukv_ref,
     ggq_ref, ggk_ref, bd_ref) = refs[:11]
    refs = refs[11:]
    if latent:
        cos64_ref, sin64_ref, cospe_ref, sinpe_ref = refs[:4]
        refs = refs[4:]
    (qm_ref, kvm_ref, gqo_ref, gkv_ref, dqo_ref, dkv_ref, ret_ref, rg_ref, gate_ref) = refs[:9]
    refs = refs[9:]
    if not latent:
        ckv_o, kpe_o, gk_o, gv_o, dk_o, dv_o = refs

    x = x_ref[...]
    mod = mod_ref[...]
    sh1 = mod[:, 0:D]
    sc1 = mod[:, D:2 * D]
    hb = (_rms(x, gpre_ref[...]) * (1.0 + sc1) + sh1).astype(BF16)

    def z(a, b):
        return _dot_nt(hb, win_ref[a:b, :])

    if latent:
        cos64, sin64 = cos64_ref[...], sin64_ref[...]
        cospe, sinpe = cospe_ref[...], sinpe_ref[...]

    cqn = _rms(z(O_CQ, O_CKV), gmq_ref[...]).astype(BF16)
    q = _dot(cqn, wuq_ref[...]) * ((MLA_NOPE + MLA_ROPE) ** -0.5 * LOG2E)
    q_nope, q_pe = q[:, 0:512], q[:, 512:768]
    if latent:
        q_pe = _rope(q_pe, _tile_lanes(cospe, 2), _tile_lanes(sinpe, 2), MLA_ROPE // 4)
    qm_ref[:, 0:512] = q_nope.astype(BF16)
    qm_ref[:, 512:768] = q_pe.astype(BF16)

    ckvn = _rms(z(O_CKV, O_KPE), gmkv_ref[...])
    kv = _dot(ckvn.astype(BF16), wukv_ref[...])
    kpe4 = _dot_nt(hb, jnp.concatenate([win_ref[O_KPE:O_GQ, :]] * 4, axis=0))
    if latent:
        kpe4 = _rope(kpe4, cospe, sinpe, MLA_ROPE // 4)
    else:
        ckv_o[...] = ckvn
        kpe_o[...] = kpe4[:, 0:MLA_ROPE]
    _store_kvm(kvm_ref, kv, kpe4.astype(BF16))

    gq = _dot_nt(hb, jnp.concatenate(
        [win_ref[O_GQ + h * GQA_HD:O_GQ + (h + 1) * GQA_HD, :] for h in GQA_ORDER], axis=0))
    gq = gq * lax.rsqrt(_seg_meansq(gq, bd_ref, 512) + EPS) * ggq_ref[...]
    gk = z(O_GK, O_GV)
    gk = gk * lax.rsqrt(_seg_meansq(gk, bd_ref, LANES) + EPS) * ggk_ref[...]
    gv = z(O_GV, O_RQ)
    if latent:
        gq = _rope(gq, _tile_lanes(cos64, 4), _tile_lanes(sin64, 4), GQA_HD // 4)
        gk = _rope(gk, cos64, sin64, GQA_HD // 4)
    else:
        gk_o[...] = gk
        gv_o[...] = gv
    gqo_ref[...] = (gq * (GQA_HD ** -0.5 * LOG2E)).astype(BF16)
    gkv_ref[:, 0:LANES] = gk.astype(BF16)
    gkv_ref[:, LANES:2 * LANES] = gv.astype(BF16)
    gkv_ref[:, 2 * LANES:3 * LANES] = jnp.ones(gv.shape, BF16)

    dq = z(O_DQ, O_DK)
    dk = z(O_DK, O_DV)
    dv = z(O_DV, O_GL)
    if latent:
        dq = _rope(dq, _tile_lanes(cos64, 4), _tile_lanes(sin64, 4), DIFF_D // 4)
        dk = _rope(dk, _tile_lanes(cos64, 4), _tile_lanes(sin64, 4), DIFF_D // 4)
    else:
        dk_o[...] = dk
        dv_o[...] = dv
    dqo_ref[...] = (dq * (DIFF_D ** -0.5 * LOG2E)).astype(BF16)
    dkv_ref[:, 0:512] = dk.astype(BF16)
    for h in range(DIFF_HEADS):
        dkv_ref[:, 512 + h * 256:512 + h * 256 + LANES] = dv[:, h * LANES:(h + 1) * LANES].astype(BF16)
        dkv_ref[:, 512 + h * 256 + LANES:512 + (h + 1) * 256] = jnp.ones((dv.shape[0], LANES), BF16)

    ret_ref[:, 0:256] = z(O_RQ, O_RK).astype(BF16)
    ret_ref[:, 256:512] = (z(O_RK, O_RV) * (RET_DK ** -0.5)).astype(BF16)
    ret_ref[:, 512:1024] = z(O_RV, O_RG).astype(BF16)
    rg_ref[...] = z(O_RG, O_DQ).astype(BF16)

    for n in range(N_BRANCH):
        gate_ref[:, n * D:(n + 1) * D] = jax.nn.sigmoid(
            z(O_GL + n * D, O_GL + (n + 1) * D)).astype(BF16)


def _inprep_call(latent, l, x, mod3, mod_row, lw, tabs, t_len):
    n_tok = x.shape[0]
    nblk = n_tok // TM
    blk_per_seq = t_len // TM

    def tok(w):
        return pl.BlockSpec((TM, w), lambda i: (i, 0))

    in_specs = [tok(D),
                pl.BlockSpec((None, 1, 6 * D), lambda i: (mod_row(i), 0, 0)),
                _const_spec((1, D)),
                pl.BlockSpec((None, O_END, D), lambda i: (l, 0, 0), pipeline_mode=pl.Buffered(1)),
                _const_spec((1, MLA_Q_LORA)), _const_spec((MLA_Q_LORA, 768)),
                _const_spec((1, MLA_KV_LORA)), _const_spec((MLA_KV_LORA, 1024)),
                _const_spec((1, 512)), _const_spec((1, LANES)), _const_spec((512, 512))]
    args = [x, mod3, lw["g_pre1"], lw["w_in"], lw["g_mla_q"], lw["w_uq"], lw["g_mla_kv"],
            lw["w_ukv"], lw["g_gqa_q"], lw["g_gqa_k"], lw["bd"]]
    if latent:
        tab_spec = pl.BlockSpec((TM, LANES), lambda i: (i % blk_per_seq, 0))
        in_specs += [tab_spec] * 4
        args += list(tabs)
    widths = [768, KVM_W, 512, GKV_W, 512, DKV_W, 1024, 512, 4 * D]
    out_specs = [tok(w) for w in widths]
    out_shape = [jax.ShapeDtypeStruct((n_tok, w), BF16) for w in widths]
    if not latent:
        cw = [MLA_KV_LORA, MLA_ROPE, 128, 128, 512, 512]
        out_specs += [tok(w) for w in cw]
        out_shape += [jax.ShapeDtypeStruct((n_tok, w), F32) for w in cw]
    return pl.pallas_call(
        functools.partial(_inprep_kernel, latent),
        grid=(nblk,),
        in_specs=in_specs, out_specs=out_specs, out_shape=out_shape,
        compiler_params=_cparams(("arbitrary",)),
        name="inprep_lat" if latent else "inprep_ctx",
    )(*args)


def _pastkv_kernel(ckv_ref, kpe_ref, wukv_ref, o_ref):
    kv = _dot(ckv_ref[...].astype(BF16), wukv_ref[...])
    _store_kvm(o_ref, kv, kpe_ref[...].astype(BF16))


def _pastkv_call(ckv, kpe4, w_ukv):
    n = ckv.shape[0]
    return pl.pallas_call(
        _pastkv_kernel,
        grid=(n // TM,),
        in_specs=[pl.BlockSpec((TM, MLA_KV_LORA), lambda i: (i, 0)),
                  pl.BlockSpec((TM, LANES), lambda i: (i, 0)),
                  _const_spec((MLA_KV_LORA, 1024))],
        out_specs=pl.BlockSpec((TM, KVM_W), lambda i: (i, 0)),
        out_shape=jax.ShapeDtypeStruct((n, KVM_W), BF16),
        compiler_params=_cparams(("arbitrary",)),
        name="pastkv",
    )(ckv, kpe4, w_ukv)


def _softmax_pv(s, v_ones):
    m = jnp.max(s, axis=-1, keepdims=True)
    p = jnp.exp2(s - m).astype(BF16)
    o = _dot(p, v_ones)
    return o[:, 0:LANES] / o[:, LANES:2 * LANES]


def _attn_kernel(lam_init, qm_ref, kvm_ref, gq_ref, gkv_ref, dq_ref, dkv_ref,
                 lam_ref, gdiff_ref, o_ref):
    tq = qm_ref.shape[0]
    lane = _lane_iota((tq, LANES))
    low = lane < HALF_LANES
    zero = jnp.zeros((tq, LANES), BF16)

    for p in range(MLA_HEADS // 2):
        qn = qm_ref[:, p * LANES:(p + 1) * LANES]
        g = p // 2
        qpe = qm_ref[:, 512 + g * LANES:512 + (g + 1) * LANES]
        kk = kvm_ref[:, p * 256:(p + 1) * 256]
        vv = kvm_ref[:, 1024 + p * 256:1024 + (p + 1) * 256]
        outs = []
        for half in range(2):
            h = 2 * p + half
            slot = h % 4
            in_slot = (lane >= slot * MLA_ROPE) & (lane < (slot + 1) * MLA_ROPE)
            lhs = jnp.concatenate(
                [jnp.where(low if half == 0 else ~low, qn, zero),
                 jnp.where(in_slot, qpe, zero)], axis=1)
            outs.append(_softmax_pv(_dot_nt(lhs, kk), vv))
        o_ref[:, p * LANES:(p + 1) * LANES] = jnp.where(low, outs[0], outs[1]).astype(BF16)

    kk = gkv_ref[:, 0:LANES]
    vv = gkv_ref[:, LANES:3 * LANES]
    for g in range(GQA_HEADS // 2):
        qg = gq_ref[:, g * LANES:(g + 1) * LANES]
        o_lo = _softmax_pv(_dot_nt(jnp.where(low, qg, zero), kk), vv)
        o_hi = _softmax_pv(_dot_nt(jnp.where(low, zero, qg), kk), vv)
        o_ref[:, 512 + g * LANES:512 + (g + 1) * LANES] = jnp.where(low, o_lo, o_hi).astype(BF16)

    lp = lam_ref[...]
    lam = (jnp.exp(jnp.sum(lp[0:1] * lp[1:2], axis=-1, keepdims=True))
           - jnp.exp(jnp.sum(lp[2:3] * lp[3:4], axis=-1, keepdims=True)) + lam_init)
    for h in range(DIFF_HEADS):
        qh = dq_ref[:, h * LANES:(h + 1) * LANES]
        kk = dkv_ref[:, h * LANES:(h + 1) * LANES]
        vv = dkv_ref[:, 512 + h * 256:512 + (h + 1) * 256]
        a1 =_softmax_pv(_dot_nt(jnp.where(low, qh, zero), kk), vv)
        a2 = _softmax_pv(_dot_nt(jnp.where(low, zero, qh), kk), vv)
        od = _rms(a1 - lam * a2, gdiff_ref[...]) * (1.0 - lam_init)
        o_ref[:, 1024 + h * LANES:1024 + (h + 1) * LANES] = od.astype(BF16)


def _attn_call(lam_init, qm, kvm, gq, gkv, dq, dkv, lam_p, g_diff, n_b, t_len, s_len):
    nq = t_len // TQ

    def qspec(w):
        return pl.BlockSpec((TQ, w), lambda b, i: (b * nq + i, 0))

    def kspec(w):
        return pl.BlockSpec((s_len, w), lambda b, i: (b, 0))

    return pl.pallas_call(
        functools.partial(_attn_kernel, lam_init),
        grid=(n_b, nq),
        in_specs=[qspec(768), kspec(KVM_W), qspec(512), kspec(GKV_W), qspec(512), kspec(DKV_W),
                  _const_spec((4, DIFF_D)), _const_spec((1, DIFF_DV))],
        out_specs=qspec(3 * BRANCH_W),
        out_shape=jax.ShapeDtypeStruct((n_b * t_len, 3 * BRANCH_W), BF16),
        compiler_params=_cparams(("arbitrary", "arbitrary")),
        name="attn",
    )(qm, kvm, gq, gkv, dq, dkv, lam_p, g_diff)


def _log_sigmoid(x):
    return jnp.minimum(x, 0.0) - jnp.log(1.0 + jnp.exp(-jnp.abs(x)))


def _log_gamma(dec_ref, d, h):
    return _log_sigmoid(jnp.full((1, 1), dec_ref[d, h], F32))


def _ret_kernel(latent, t_len, dec_ref, q_ref, k_ref, v_ref, rg_ref, gret_ref, *refs):
    if latent:
        s0_ref, o_ref = refs
    else:
        o_ref, st_ref = refs
    tq = q_ref.shape[0]
    t0 = pl.program_id(1) * tq
    lane = _lane_iota((tq, LANES))
    low = lane < HALF_LANES
    zero = jnp.zeros((tq, LANES), BF16)
    t_idx = (t0 + lax.broadcasted_iota(jnp.int32, (tq, t_len), 0)).astype(F32)
    s_idx = lax.broadcasted_iota(jnp.int32, (tq, t_len), 1).astype(F32)
    dist = t_idx - s_idx
    past = dist >= 0
    diag = jnp.where(dist == 0, 1.0, 0.0)
    t_col = (t0 + lax.broadcasted_iota(jnp.int32, (tq, 1), 0)).astype(F32)

    def lg(d, h):
        return _log_gamma(dec_ref, d, h)

    for h in range(RET_HEADS):
        p, half = h // 2, h % 2
        qp = q_ref[:, p * LANES:(p + 1) * LANES]
        qm = jnp.where(low if half == 0 else ~low, qp, zero)
        kp = k_ref[:, p * LANES:(p + 1) * LANES]
        vh = v_ref[:, h * LANES:(h + 1) * LANES]
        lgf, lgb = lg(0, h), lg(1, h)
        dmask = jnp.exp(jnp.where(past, lgf, -lgb) * dist) + diag
        o = _dot((_dot_nt(qm, kp) * dmask).astype(BF16), vh)
        if latent:
            sf =s0_ref[0, p].astype(BF16)
            sb = s0_ref[1, p].astype(BF16)
            o = o + _dot(qm, sf) * jnp.exp(lgf * (t_col + 1.0))
            o = o + _dot(qm, sb) * jnp.exp(lgb * (float(t_len) - t_col))
        mu = jnp.mean(o, axis=-1, keepdims=True)
        oc = o - mu
        y = oc * lax.rsqrt(jnp.mean(oc * oc, axis=-1, keepdims=True) + EPS)
        y = y * gret_ref[:, h * LANES:(h + 1) * LANES]
        rg = rg_ref[:, h * LANES:(h + 1) * LANES].astype(F32)
        o_ref[:, h * LANES:(h + 1) * LANES] = (y * _silu(rg)).astype(BF16)

    if not latent:
        s_col = lax.broadcasted_iota(jnp.int32, (t_len, 1), 0).astype(F32)
        lane_t = _lane_iota((1, LANES)) < HALF_LANES
        for p in range(RET_HEADS // 2):
            kp = k_ref[:, p * LANES:(p + 1) * LANES].astype(F32)
            for d in range(2):
                lg_lane = jnp.where(lane_t, lg(d, 2 * p), lg(d, 2 * p + 1))
                expo = (float(t_len) - 1.0 - s_col) if d == 0 else s_col
                kdec_t = jnp.transpose(kp * jnp.exp(lg_lane * expo)).astype(BF16)
                for half in range(2):
                    h = 2 * p + half
                    st = _dot(kdec_t, v_ref[:, h * LANES:(h + 1) * LANES])
                    st_ref[d, h] = st[half * RET_DK:(half + 1) * RET_DK, :]


def _ret_call(latent, dec, ret, rg, g_ret, s0, n_b, t_len):
    nq = t_len // TQ
    assert latent or nq == 1
    in_specs = [pl.BlockSpec(memory_space=pltpu.SMEM),
                pl.BlockSpec((TQ, 256), lambda b, i: (b * nq + i, 0)),
                pl.BlockSpec((t_len, 256), lambda b, i: (b, 1)),
                pl.BlockSpec((t_len, 512), lambda b, i: (b, 1)),
                pl.BlockSpec((TQ, 512), lambda b, i: (b * nq + i, 0)),
                _const_spec((1, 512))]
    args = [dec, ret, ret, ret, rg, g_ret]
    out_specs = [pl.BlockSpec((TQ, 512), lambda b, i: (b * nq + i, 0))]
    out_shape = [jax.ShapeDtypeStruct((n_b * t_len, 512), BF16)]
    if latent:
        in_specs.append(pl.BlockSpec((None, 2, 2, LANES, LANES), lambda b, i: (b, 0, 0, 0, 0)))
        args.append(s0)
    else:
        out_specs.append(pl.BlockSpec((None, 2, RET_HEADS, RET_DK, RET_DV),
                                      lambda b, i: (b, 0, 0, 0, 0)))
        out_shape.append(jax.ShapeDtypeStruct((n_b, 2, RET_HEADS, RET_DK, RET_DV), F32))
    return pl.pallas_call(
        functools.partial(_ret_kernel, latent, t_len),
        grid=(n_b, nq),
        in_specs=in_specs, out_specs=out_specs, out_shape=out_shape,
        compiler_params=_cparams(("arbitrary", "arbitrary")),
        name="ret_lat" if latent else "ret_ctx",
    )(*args)


def _merge_kernel(x_ref, mod_ref, br_ref, or_ref, gate_ref, wbr_ref, wout_ref, gpost_ref, o_ref):
    merged = None
    for n in range(N_BRANCH):
        if n < 2:
            b = br_ref[:, n * BRANCH_W:(n + 1) * BRANCH_W]
        elif n == 2:
            b = or_ref[...]
        else:
            b = br_ref[:, 2 * BRANCH_W:3 * BRANCH_W]
        t = gate_ref[:, n * D:(n + 1) * D].astype(F32) * _dot(b, wbr_ref[n])
        merged = t if merged is None else merged + t
    out = _dot(merged.astype(BF16), wout_ref[...])
    g1 = mod_ref[...][:, 2 * D:3 * D]
    o_ref[...] = x_ref[...] + g1 * _rms(out, gpost_ref[...])


def _merge_call(x, mod3, mod_row, br, o_r, gates, lw):
    n_tok = x.shape[0]

    def tok(w):
        return pl.BlockSpec((TM, w), lambda i: (i, 0))

    return pl.pallas_call(
        _merge_kernel,
        grid=(n_tok // TM,),
        in_specs=[tok(D), pl.BlockSpec((None, 1, 6 * D), lambda i: (mod_row(i), 0, 0)),
                  tok(3 * BRANCH_W), tok(BRANCH_W), tok(4 * D),
                  _const_spec((N_BRANCH, BRANCH_W, D)), _const_spec((D, D)), _const_spec((1, D))],
        out_specs=tok(D),
        out_shape=jax.ShapeDtypeStruct((n_tok, D), F32),
        compiler_params=_cparams(("arbitrary",)),
        name="merge",
    )(x, mod3, br, o_r, gates, lw["w_br"], lw["w_out"], lw["g_post1"])


def _route(logits_t, bias):
    n = logits_t.shape[1]
    scores = jax.nn.sigmoid(logits_t)
    sel = scores + bias
    neg = -jnp.inf
    sub = lax.broadcasted_iota(jnp.int32, (GROUP_SIZE, n), 0)
    grp = []
    for g in range(N_GROUPS):
        blk = sel[g * GROUP_SIZE:(g + 1) * GROUP_SIZE]
        m1 = jnp.max(blk, axis=0, keepdims=True)
        i1 = jnp.min(jnp.where(blk == m1, sub, GROUP_SIZE), axis=0, keepdims=True)
        m2 = jnp.max(jnp.where(sub == i1, neg, blk), axis=0, keepdims=True)
        grp.append(m1 + m2)
    parts = []
    for g in range(N_GROUPS):
        beaten = jnp.zeros((1, n), jnp.int32)
        for o in range(N_GROUPS):
            if o == g:
                continue
            wins = (grp[o] > grp[g]) | (grp[o] == grp[g]) if o < g else (grp[o] > grp[g])
            beaten = beaten + wins.astype(jnp.int32)
        keep = beaten < TOPK_GROUPS
        parts.append(jnp.where(keep, sel[g * GROUP_SIZE:(g + 1) * GROUP_SIZE], neg))
    cur = jnp.concatenate(parts, axis=0)
    eidx = lax.broadcasted_iota(jnp.int32, (N_EXPERTS, n), 0)
    hits, ids, ws = [], [], []
    for _ in range(TOP_K):
        m = jnp.max(cur, axis=0, keepdims=True)
        i = jnp.min(jnp.where(cur == m, eidx, N_EXPERTS), axis=0, keepdims=True)
        hit = eidx == i
        hits.append(hit)
        ids.append(i)
        ws.append(jnp.sum(jnp.where(hit, scores, 0.0), axis=0, keepdims=True))
        cur = jnp.where(hit, neg, cur)
    wsum = ws[0] + ws[1] + ws[2] + ws[3]
    return hits, ids, [w / wsum * ROUTE_SCALE for w in ws]


U32 = jnp.uint32
HIGH16 = np.uint32(0xFFFF0000)


def _bf16_bits(v):
    return lax.bitcast_convert_type(v.astype(BF16).astype(F32), U32)


def _pack_rows(v):
    return (_bf16_bits(v[:, 0:D // 2]) >> 16) | _bf16_bits(v[:, D // 2:D])


def _unpack_rows(p):
    lo = lax.bitcast_convert_type(p << 16, F32)
    hi = lax.bitcast_convert_type(p & HIGH16, F32)
    return jnp.concatenate([lo, hi], axis=1)


def _moe_pre_kernel(x_ref, mod_ref, gpre_ref, wr_ref, br_ref, tri_ref,
                    hp_ref, eidx_ref, rank_ref, comb_ref, cnt_ref, run_ref):
    tm = x_ref.shape[0]

    @pl.when(pl.program_id(0) == 0)
    def _():
        run_ref[...] = jnp.zeros_like(run_ref)

    mod = mod_ref[...]
    sh2, sc2 = mod[:, 3 * D:4 * D], mod[:, 4 * D:5 * D]
    h = _rms(x_ref[...], gpre_ref[...]) * (1.0 + sc2) + sh2
    hp_ref[...] = _pack_rows(h)
    hb = h.astype(BF16)
    h_lo = (h - hb.astype(F32)).astype(BF16)
    wr = wr_ref[...]
    wr_hi = wr.astype(BF16)
    wr_lo = (wr - wr_hi.astype(F32)).astype(BF16)
    logits_t = _dot_nt(wr_hi, hb) + _dot_nt(wr_hi, h_lo) + _dot_nt(wr_lo, hb)
    hits, ids, ws = _route(logits_t, br_ref[...])

    picked = jnp.zeros((N_EXPERTS, tm), F32)
    for hit in hits:
        picked = jnp.where(hit, 1.0, picked)
    before = _dot(picked.astype(BF16), tri_ref[...]) + run_ref[:, 0:1]
    sub8 = lax.broadcasted_iota(jnp.int32, (8, tm), 0)
    comb8 = jnp.zeros((8, tm), F32)
    for k in range(TOP_K):
        rank = jnp.sum(jnp.where(hits[k], before, 0.0), axis=0, keepdims=True)
        eidx_ref[k:k + 1, :] = ids[k]
        rank_ref[k:k + 1, :] = rank.astype(jnp.int32)
        comb8 = jnp.where(sub8 == k, ws[k], comb8)
    comb_ref[...] = jnp.transpose(
        jnp.concatenate([comb8, jnp.zeros((LANES - 8, tm), F32)], axis=0))
    run_ref[...] = run_ref[...] + jnp.sum(picked, axis=1, keepdims=True)
    cnt_ref[...] = run_ref[...]


def _moe_pre_call(x, mod3, mod_row, lw):
    n_tok = x.shape[0]
    tm = TM_MOE_PRE
    tri = np.arange(tm)
    tri = jnp.asarray(tri[:, None] < tri[None, :], BF16)
    row4 = pl.BlockSpec((TOP_K, tm), lambda i: (0, i))
    return pl.pallas_call(
        _moe_pre_kernel,
        grid=(n_tok // tm,),
        in_specs=[pl.BlockSpec((tm, D), lambda i: (i, 0)),
                  pl.BlockSpec((None, 1, 6 * D), lambda i: (mod_row(i), 0, 0)),
                  _const_spec((1, D)), _const_spec((N_EXPERTS, D)), _const_spec((N_EXPERTS, 1)),
                  _const_spec((tm, tm))],
        out_specs=[pl.BlockSpec((tm, D // 2), lambda i: (i, 0)), row4, row4,
                   pl.BlockSpec((tm, LANES), lambda i: (i, 0)),
                   _const_spec((N_EXPERTS, LANES))],
        out_shape=[jax.ShapeDtypeStruct((n_tok, D // 2), U32),
                   jax.ShapeDtypeStruct((TOP_K, n_tok), jnp.int32),
                   jax.ShapeDtypeStruct((TOP_K, n_tok), jnp.int32),
                   jax.ShapeDtypeStruct((n_tok, LANES), F32),
                   jax.ShapeDtypeStruct((N_EXPERTS, LANES), F32)],
        scratch_shapes=[pltpu.VMEM((N_EXPERTS, LANES), F32)],
        compiler_params=_cparams(("arbitrary",)),
        name="moe_pre",
    )(x, mod3, lw["g_pre2"], lw["w_router_t"], lw["b_router"], tri)


def _moe_plan_kernel(eidx_ref, rank_ref, cnt_ref, dest_ref, te_ref, tv_ref, tn_ref):
    tm = eidx_ref.shape[1]
    cnt = cnt_ref[...]
    padded = jnp.ceil(cnt * (1.0 / TMX)) * TMX
    row = lax.broadcasted_iota(jnp.int32, cnt.shape, 0)
    incl = padded
    shift = 1
    while shift < N_EXPERTS:
        incl = incl + jnp.where(row >= shift, pltpu.roll(incl, shift, 0), 0.0)
        shift *= 2
    start = (incl - padded)[:, 0:1]
    end = incl[:, 0:1]
    erow = lax.broadcasted_iota(jnp.int32, (N_EXPERTS, tm), 0)
    for k in range(TOP_K):
        mine = erow == eidx_ref[k:k + 1, :]
        base = jnp.sum(jnp.where(mine, start, 0.0), axis=0, keepdims=True)
        dest_ref[k:k + 1, :] = rank_ref[k:k + 1, :] + base.astype(jnp.int32)

    @pl.when(pl.program_id(0) == 0)
    def _():
        tile0 = (_lane_iota((1, LANES)) * TMX).astype(F32)
        owner = jnp.sum(jnp.where(end <= tile0, 1.0, 0.0), axis=0, keepdims=True)
        owner = jnp.minimum(owner, N_EXPERTS - 1.0)
        erow_t = lax.broadcasted_iota(jnp.int32, (N_EXPERTS, LANES), 0).astype(F32)
        left = jnp.sum(jnp.where(erow_t == owner, cnt[:, 0:1] - (tile0 - start), 0.0),
                       axis=0, keepdims=True)
        te_ref[...] = owner.astype(jnp.int32)
        tv_ref[...] = jnp.clip(left, 0.0, float(TMX)).astype(jnp.int32)
        tn_ref[...] = jnp.full(tn_ref.shape, N_EXPERTS, jnp.int32)
        nxt = owner
        for k in range(W_SLOTS - 1):
            later = (erow_t > nxt) & (cnt[:, 0:1] > 0.0)
            nxt = jnp.min(jnp.where(later, erow_t, float(N_EXPERTS)), axis=0, keepdims=True)
            tn_ref[k:k + 1, :] = nxt.astype(jnp.int32)


def _moe_plan_call(eidx, rank, cnt):
    n_tok = eidx.shape[1]
    tm = TM_MOE_PRE
    row4 = pl.BlockSpec((TOP_K, tm), lambda i: (0, i))
    tiles = jax.ShapeDtypeStruct((1, LANES), jnp.int32)
    return pl.pallas_call(
        _moe_plan_kernel,
        grid=(n_tok // tm,),
        in_specs=[row4, row4, _const_spec((N_EXPERTS, LANES))],
        out_specs=[row4, _const_spec((1, LANES)), _const_spec((1, LANES)), _const_spec((8, LANES))],
        out_shape=[jax.ShapeDtypeStruct((TOP_K, n_tok), jnp.int32), tiles, tiles,
                   jax.ShapeDtypeStruct((8, LANES), jnp.int32)],
        compiler_params=_cparams(("arbitrary",)),
        name="moe_plan",
    )(eidx, rank, cnt)


def _experts_kernel(l, te_ref, tv_ref, tn_ref, xs_ref, wgu_hbm, wdn_hbm, ys_ref,
                    wgu_f, wdn_f, wgu_b, wdn_b, sem, group_ref):
    j = pl.program_id(0)
    valid = tv_ref[j]
    expert = te_ref[j]

    def fetch(e, slot):
        return (pltpu.make_async_copy(wgu_hbm.at[l, e], wgu_f.at[slot], sem.at[slot, 0]),
                pltpu.make_async_copy(wdn_hbm.at[l, e], wdn_f.at[slot], sem.at[slot, 1]))

    def start_if_any(e, slot):
        @pl.when(e < N_EXPERTS)
        def _():
            for cp in fetch(e, slot):
                cp.start()

    @pl.when(j == 0)
    def _():
        group_ref[0] = 0
        start_if_any(expert, 0)
        for k in range(W_SLOTS - 2):
            start_if_any(tn_ref[k, 0], k + 1)

    first_tile = (j == 0) | (expert != te_ref[jnp.maximum(j - 1, 0)])

    @pl.when(first_tile & (valid > 0))
    def _():
        group = group_ref[0]
        slot = lax.rem(group, W_SLOTS)
        for cp in fetch(expert, slot):
            cp.wait()
        wgu_b[...] = wgu_f[slot].astype(BF16)
        wdn_b[...] = wdn_f[slot].astype(BF16)
        start_if_any(tn_ref[W_SLOTS - 2, j], lax.rem(group + W_SLOTS - 1, W_SLOTS))
        group_ref[0] = group + 1

    @pl.when(valid > 0)
    def _():
        rows = lax.broadcasted_iota(jnp.int32, (TMX, D), 0)
        x = jnp.where(rows < valid, _unpack_rows(xs_ref[...]), 0.0).astype(BF16)
        gu = _dot(x, wgu_b[...])
        a = _silu(gu[:, 0:EXPERT_FF]) * gu[:, EXPERT_FF:2 * EXPERT_FF]
        ys_ref[...] = _pack_rows(_dot(a.astype(BF16), wdn_b[...]))

    @pl.when(valid <= 0)
    def _():
        ys_ref[...] = jnp.zeros_like(ys_ref)


def _experts_call(l, xs, te, tv, tn, w_gu, w_dn):
    n_tiles = xs.shape[0] // TMX
    grid_spec = pltpu.PrefetchScalarGridSpec(
        num_scalar_prefetch=3,
        grid=(n_tiles,),
        in_specs=[pl.BlockSpec((TMX, D // 2), lambda j, *_: (j, 0)),
                  pl.BlockSpec(memory_space=pl.ANY), pl.BlockSpec(memory_space=pl.ANY)],
        out_specs=pl.BlockSpec((TMX, D // 2), lambda j, *_: (j, 0)),
        scratch_shapes=[pltpu.VMEM((W_SLOTS, D, 2 * EXPERT_FF), F32),
                        pltpu.VMEM((W_SLOTS, EXPERT_FF, D), F32),
                        pltpu.VMEM((D, 2 * EXPERT_FF), BF16), pltpu.VMEM((EXPERT_FF, D), BF16),
                        pltpu.SemaphoreType.DMA((W_SLOTS, 2)), pltpu.SMEM((1,), jnp.int32)])
    return pl.pallas_call(
        functools.partial(_experts_kernel, l),
        grid_spec=grid_spec,
        out_shape=jax.ShapeDtypeStruct(xs.shape, U32),
        compiler_params=_cparams(("arbitrary",)),
        name="moe_experts",
    )(te, tv, tn, xs, w_gu, w_dn)


def _moe_post_kernel(x_ref, mod_ref, hp_ref, yg_ref, comb_ref, wsgu_ref, wsdn_ref, gpost_ref,
                     o_ref):
    hb = _unpack_rows(hp_ref[...]).astype(BF16)
    sgu = _dot(hb, wsgu_ref[...])
    sa = _silu(sgu[:, 0:SHARED_FF]) * sgu[:, SHARED_FF:2 * SHARED_FF]
    acc = _dot(sa.astype(BF16), wsdn_ref[...])
    comb = comb_ref[...]
    for k in range(TOP_K):
        acc = acc + comb[:, k:k + 1] * _unpack_rows(yg_ref[k])
    g2 = mod_ref[...][:, 5 * D:6 * D]
    o_ref[...] = x_ref[...] + g2 * _rms(acc, gpost_ref[...])


def _moe_post_call(x, mod3, mod_row, hp, yg, comb, lw):
    n_tok = x.shape[0]
    tm = TM_MOE_PRE
    return pl.pallas_call(
        _moe_post_kernel,
        grid=(n_tok // tm,),
        in_specs=[pl.BlockSpec((tm, D), lambda i: (i, 0)),
                  pl.BlockSpec((None, 1, 6 * D), lambda i: (mod_row(i), 0, 0)),
                  pl.BlockSpec((tm, D // 2), lambda i: (i, 0)),
                  pl.BlockSpec((TOP_K, tm, D // 2), lambda i: (0, i, 0)),
                  pl.BlockSpec((tm, LANES), lambda i: (i, 0)),
                  _const_spec((D, 2 * SHARED_FF)), _const_spec((SHARED_FF, D)),
                  _const_spec((1, D))],
        out_specs=pl.BlockSpec((tm, D), lambda i: (i, 0)),
        out_shape=jax.ShapeDtypeStruct((n_tok, D), F32),
        compiler_params=_cparams(("arbitrary",)),
        name="moe_post",
    )(x, mod3, hp, yg, comb, lw["w_sh_gu"], lw["w_sh_down"], lw["g_post2"])


def _moe_call(l, x, mod3, mod_row, lw):
    n_tok = x.shape[0]
    n_slots = -(-(TOP_K * n_tok + N_EXPERTS * (TMX - 1)) // TMX) * TMX
    assert n_slots // TMX <= LANES
    hp, eidx, rank, comb, cnt = _moe_pre_call(x, mod3, mod_row, lw)
    dest, te, tv, tn = _moe_plan_call(eidx, rank, cnt)
    dest = dest.reshape(TOP_K * n_tok)
    xs = _sc_scatter_rows(hp, dest, n_slots)
    ys = _experts_call(l, xs, te[0], tv[0], tn, lw["w_exp_gu"], lw["w_exp_down"])
    yg = _sc_gather_rows(ys, dest).reshape(TOP_K, n_tok, D // 2)
    return _moe_post_call(x, mod3, mod_row, hp, yg, comb, lw)


SC_CORES, SC_SUBCORES = 2, 16
SC_WORKERS = SC_CORES * SC_SUBCORES


def _sc_gather_rows(table, idx, chunk=64):
    n_out, width = idx.shape[0], table.shape[1]
    per_worker = n_out // SC_WORKERS
    n_chunks = per_worker // chunk
    assert per_worker * SC_WORKERS == n_out and n_chunks * chunk == per_worker
    mesh = plsc.VectorSubcoreMesh(core_axis_name="c", subcore_axis_name="s",
                                  num_cores=SC_CORES, num_subcores=SC_SUBCORES)

    @functools.partial(
        pl.kernel, mesh=mesh,
        out_type=jax.ShapeDtypeStruct((n_out, width), table.dtype),
        scratch_types=[pltpu.VMEM((chunk,), jnp.int32), pltpu.VMEM((chunk, width), table.dtype),
                       pltpu.SemaphoreType.DMA],
        name="sc_gather")
    def gather(table_hbm, idx_hbm, out_hbm, idx_v, rows_v, sem):
        base = (lax.axis_index("s") * SC_CORES + lax.axis_index("c")) * per_worker

        @pl.loop(0, n_chunks)
        def _(j):
            off = base + j * chunk
            pltpu.sync_copy(idx_hbm.at[pl.ds(off, chunk)], idx_v)
            pltpu.async_copy(table_hbm.at[idx_v], rows_v, sem).wait()
            pltpu.sync_copy(rows_v, out_hbm.at[pl.ds(off, chunk)])

    return gather(table, idx)


def _sc_scatter_rows(rows, dest, n_slots, chunk=64):
    n_tok, width = rows.shape
    per_worker = n_tok // SC_WORKERS
    n_chunks = per_worker // chunk
    assert per_worker * SC_WORKERS == n_tok and n_chunks * chunk == per_worker
    mesh = plsc.VectorSubcoreMesh(core_axis_name="c", subcore_axis_name="s",
                                  num_cores=SC_CORES, num_subcores=SC_SUBCORES)

    @functools.partial(
        pl.kernel, mesh=mesh,
        out_type=jax.ShapeDtypeStruct((n_slots, width), rows.dtype),
        scratch_types=[pltpu.VMEM((chunk,), jnp.int32), pltpu.VMEM((chunk, width), rows.dtype)],
        name="sc_scatter")
    def scatter(rows_hbm, dest_hbm, out_hbm, idx_v, rows_v):
        base = (lax.axis_index("s") * SC_CORES + lax.axis_index("c")) * per_worker

        @pl.loop(0, n_chunks)
        def _(j):
            off = base + j * chunk
            pltpu.sync_copy(rows_hbm.at[pl.ds(off, chunk)], rows_v)
            for k in range(TOP_K):
                pltpu.sync_copy(dest_hbm.at[pl.ds(k * n_tok + off, chunk)], idx_v)
                pltpu.sync_copy(rows_v, out_hbm.at[idx_v])

    return scatter(rows, dest)


def _rope_tables(t_len):
    pos = np.arange(t_len)
    row, col = pos // GRID_W, pos % GRID_W

    def tab(r):
        half = r // 2
        freq = ROPE_BASE ** (-np.arange(half, dtype=np.float64) / half)
        sign = np.concatenate([-np.ones(half), np.ones(half)])
        cs, sn = [], []
        for p in (row, col):
            ang = p[:, None].astype(np.float64) * freq[None, :]
            cs.append(np.concatenate([np.cos(ang), np.cos(ang)], axis=1))
            sn.append(np.concatenate([np.sin(ang), np.sin(ang)], axis=1) * sign[None, :])
        return np.concatenate(cs, axis=1), np.concatenate(sn, axis=1)

    c64, s64 = tab(GQA_HD // 2)
    cpe, spe = tab(MLA_ROPE // 2)
    out = (np.tile(c64, (1, 2)), np.tile(s64, (1, 2)), np.tile(cpe, (1, 4)), np.tile(spe, (1, 4)))
    return tuple(jnp.asarray(a, F32) for a in out)


def _layer_weights(l, p):
    w_uq = p["w_mla_uq"][l].reshape(MLA_Q_LORA, MLA_HEADS, MLA_NOPE + MLA_ROPE)
    w_uq = jnp.concatenate([w_uq[:, :, :MLA_NOPE].reshape(MLA_Q_LORA, -1),
                            w_uq[:, :, MLA_NOPE:].reshape(MLA_Q_LORA, -1)], axis=1)
    w_ukv = p["w_mla_ukv"][l].reshape(MLA_KV_LORA, MLA_HEADS, MLA_NOPE + MLA_V)
    w_ukv = jnp.concatenate([w_ukv[:, :, :MLA_NOPE].reshape(MLA_KV_LORA, -1),
                             w_ukv[:, :, MLA_NOPE:].reshape(MLA_KV_LORA, -1)], axis=1)
    w_br = p["w_br"][l]
    w_br_gqa = w_br[1].reshape(GQA_HEADS, GQA_HD, D)[jnp.array(GQA_ORDER)].reshape(BRANCH_W, D)
    w_br = jnp.stack([w_br[0], w_br_gqa, w_br[2], w_br[3]], axis=0)
    blk = np.arange(512) // GQA_HD
    return {
        "g_pre1": p["g_pre1"][l].reshape(1, D), "g_post1": p["g_post1"][l].reshape(1, D),
        "g_pre2": p["g_pre2"][l].reshape(1, D), "g_post2": p["g_post2"][l].reshape(1, D),
        "w_in": p["w_in_packed"],
        "g_mla_q": p["g_mla_q"][l].reshape(1, -1), "w_uq": w_uq.astype(BF16),
        "g_mla_kv": p["g_mla_kv"][l].reshape(1, -1), "w_ukv": w_ukv.astype(BF16),
        "g_gqa_q": jnp.tile(p["g_gqa_q"][l], GQA_HEADS).reshape(1, -1),
        "g_gqa_k": jnp.tile(p["g_gqa_k"][l], GQA_KV_HEADS).reshape(1, -1),
        "bd": jnp.asarray(blk[:, None] == blk[None, :], BF16),
        "ret_decay": p["ret_decay"][l],
        "g_ret": p["g_ret"][l].reshape(1, -1),
        "diff_lambda": p["diff_lambda"][l], "g_diff": p["g_diff"][l].reshape(1, -1),
        "w_br": w_br.astype(BF16), "w_out": p["w_out"][l].astype(BF16),
        "w_router_t": p["w_router"][l].T, "b_router": p["b_router"][l].reshape(-1, 1),
        "w_exp_gu": p["w_exp_gu"], "w_exp_down": p["w_exp_down"],
        "w_sh_gu": p["w_sh_gu"][l].astype(BF16), "w_sh_down": p["w_sh_down"][l].astype(BF16),
    }


def _mixers(latent, l, x, mod3, mod_row, lw, n_b, t_len, tabs=None, past=None, s0=None):
    lam_init = 0.8 - 0.6 * math.exp(-0.3 * l)
    outs = _inprep_call(latent, l, x, mod3, mod_row, lw, tabs, t_len)
    qm, kvm, gq, gkv, dq, dkv, ret, rg, gates = outs[:9]
    s_len = t_len
    if latent:
        past_kvm, past_gkv, past_dkv = past
        p_len = past_gkv.shape[1]
        s_len = p_len + t_len

        def cat(a, b):
            return jnp.concatenate([a, b.reshape(n_b, t_len, -1)], axis=1).reshape(n_b * s_len, -1)

        kvm, gkv, dkv = cat(past_kvm, kvm), cat(past_gkv, gkv), cat(past_dkv, dkv)
    br = _attn_call(lam_init, qm, kvm, gq, gkv, dq, dkv, lw["diff_lambda"], lw["g_diff"],
                    n_b, t_len, s_len)
    r = _ret_call(latent, lw["ret_decay"], ret, rg, lw["g_ret"], s0, n_b, t_len)
    y = _merge_call(x, mod3, mod_row, br, r[0], gates, lw)
    cache = None if latent else tuple(outs[9:]) + (r[1],)
    return y, cache


def kernel(x_prompt, x_sample, cache_mla_ckv, cache_mla_kpe, cache_gqa_k, cache_gqa_v, cache_diff_k, cache_diff_v, state_ret, c, c_ctx, w_mod, b_mod, g_pre1, g_post1, g_pre2, g_post2, w_in, g_mla_q, w_mla_uq, g_mla_kv, w_mla_ukv, g_gqa_q, g_gqa_k, ret_decay, g_ret, diff_lambda, g_diff, w_br, w_out, w_router, b_router, w_exp_gu, w_exp_down, w_sh_gu, w_sh_down):
    params = dict(w_in_packed=jnp.swapaxes(w_in, 1, 2).astype(BF16), g_pre1=g_pre1, g_post1=g_post1, g_pre2=g_pre2,
                  g_post2=g_post2, g_mla_q=g_mla_q, w_mla_uq=w_mla_uq,
                  g_mla_kv=g_mla_kv, w_mla_ukv=w_mla_ukv, g_gqa_q=g_gqa_q, g_gqa_k=g_gqa_k,
                  ret_decay=ret_decay, g_ret=g_ret, diff_lambda=diff_lambda, g_diff=g_diff,
                  w_br=w_br, w_out=w_out, w_router=w_router, b_router=b_router,
                  w_exp_gu=w_exp_gu, w_exp_down=w_exp_down, w_sh_gu=w_sh_gu, w_sh_down=w_sh_down)
    n_bc, t_c, _ = x_prompt.shape
    n_bl, t_l, _ = x_sample.shape
    p_len = cache_mla_ckv.shape[2]
    tabs = _rope_tables(t_l)
    n_cond = 8
    cond = jnp.concatenate([c_ctx[None, :], c, jnp.zeros((n_cond - 1 - n_bl, D), F32)], axis=0)
    blk_c, blk_l = t_c // TM, t_l // TM
    assert t_l % TM_MOE_PRE == 0 and (t_c * n_bc) % TM_MOE_PRE == 0

    yp = x_prompt.reshape(n_bc * t_c, D)
    ys = x_sample.reshape(n_bl * t_l, D)
    caches = []
    for l in range(DEPTH):
        lw = _layer_weights(l, params)
        mod3 = _mod_call(l, cond, w_mod, b_mod).reshape(n_cond, 1, 6 * D)
        yp, cache = _mixers(False, l, yp, mod3, lambda i: 0, lw, n_bc, t_c)
        yp = _moe_call(l, yp, mod3, lambda i: 0, lw)
        caches.append(cache)
        past_kvm = _pastkv_call(cache_mla_ckv[:, l].reshape(n_bl * p_len, -1),
                                jnp.tile(cache_mla_kpe[:, l].reshape(n_bl * p_len, -1), (1, 4)),
                                lw["w_ukv"]).reshape(n_bl, p_len, -1)
        past_gkv = jnp.concatenate([cache_gqa_k[:, l].reshape(n_bl, p_len, -1),
                                    cache_gqa_v[:, l].reshape(n_bl, p_len, -1),
                                    jnp.ones((n_bl, p_len, LANES), F32)], axis=-1).astype(BF16)
        past_dv = jnp.concatenate([cache_diff_v[:, l], jnp.ones_like(cache_diff_v[:, l])], axis=-1)
        past_dkv = jnp.concatenate([cache_diff_k[:, l].reshape(n_bl, p_len, -1),
                                    past_dv.reshape(n_bl, p_len, -1)], axis=-1).astype(BF16)
        s0 = state_ret[:, l].reshape(n_bl, 2, RET_HEADS // 2, 2 * RET_DK, RET_DV)
        ys, _ = _mixers(True, l, ys, mod3, lambda i: 1 + i // blk_l, lw, n_bl, t_l, tabs=tabs,
                        past=(past_kvm, past_gkv, past_dkv), s0=s0)
        ys = _moe_call(l, ys, mod3, lambda i: 1 + i // (t_l // TM_MOE_PRE), lw)

    def stack(k, shape):
        return jnp.stack([caches[l][k].reshape((n_bc, t_c) + shape) for l in range(DEPTH)], axis=1)

    new_ret = jnp.stack([caches[l][6] for l in range(DEPTH)], axis=1)
    return (yp.reshape(n_bc, t_c, D), ys.reshape(n_bl, t_l, D),
            stack(0, (MLA_KV_LORA,)), stack(1, (MLA_ROPE,)),
            stack(2, (GQA_KV_HEADS, GQA_HD)), stack(3, (GQA_KV_HEADS, GQA_HD)),
            stack(4, (DIFF_HEADS, 2, DIFF_D)), stack(5, (DIFF_HEADS, DIFF_DV)), new_ret)
```

```python
import functools
import math

import numpy as np
import jax
import jax.numpy as jnp
from jax import lax
from jax.experimental import pallas as pl
from jax.experimental.pallas import tpu as pltpu
from jax.experimental.pallas import tpu_sc as plsc

F32 = jnp.float32
BF16 = jnp.bfloat16

D = 1024
DEPTH = 2
GRID_W = 64
ROPE_BASE = 10000.0
EPS = 1e-6

MLA_HEADS, MLA_NOPE, MLA_ROPE, MLA_V = 8, 64, 32, 64
MLA_Q_LORA, MLA_KV_LORA = 384, 256
GQA_HEADS, GQA_KV_HEADS, GQA_HD = 8, 2, 64
RET_HEADS, RET_DK, RET_DV = 4, 64, 128
DIFF_HEADS, DIFF_D, DIFF_DV = 4, 64, 128
N_BRANCH, BRANCH_W = 4, 512
N_EXPERTS, TOP_K, N_GROUPS, TOPK_GROUPS = 32, 4, 4, 2
EXPERT_FF, SHARED_FF = 256, 256
ROUTE_SCALE = 2.5
GROUP_SIZE = N_EXPERTS // N_GROUPS

LANES = 128
HALF_LANES = 64
VMEM_LIMIT = 56 * 1024 * 1024

C_CQ, C_CKV, C_KPE, C_GQ, C_GK, C_GV = 0, 384, 640, 768, 1280, 1408
C_DQ, C_DK, C_DV, C_RQ, C_RK, C_RV, C_RG, C_GL, C_END = (
    1536, 2048, 2560, 3072, 3328, 3584, 4096, 4608, 8704)
O_CQ, O_CKV, O_KPE, O_GQ, O_GK, O_GV = 0, 384, 640, 672, 1184, 1312
O_RQ, O_RK, O_RV, O_RG, O_DQ, O_DK, O_DV, O_GL, O_END = (
    1440, 1696, 1952, 2464, 2976, 3488, 4000, 4512, 8608)
GQA_ORDER = (0, 4, 1, 5, 2, 6, 3, 7)

KVM_W = 8 * 256
GKV_W = 3 * LANES
DKV_W = 512 + 4 * 256
LOG2E = 1.4426950408889634
TM = 256
TQ = 256
TM_MOE_PRE = 512
TMX = 256
W_SLOTS = 4


def _cparams(sem):
    return pltpu.CompilerParams(dimension_semantics=sem, vmem_limit_bytes=VMEM_LIMIT)


def _const_spec(shape):
    nd = len(shape)
    return pl.BlockSpec(shape, lambda *_: (0,) * nd)


def _layer_spec(l, shape):
    nd = len(shape)
    return pl.BlockSpec((None,) + tuple(shape), lambda *_: (l,) + (0,) * nd)


def _rms(x, g):
    return x * lax.rsqrt(jnp.mean(x * x, axis=-1, keepdims=True) + EPS) * g


def _dot(a, b):
    return jnp.dot(a, b, preferred_element_type=F32)


def _dot_nt(a, b):
    return lax.dot_general(a, b, (((1,), (1,)), ((), ())), preferred_element_type=F32)


def _silu(x):
    return x * jax.nn.sigmoid(x)


def _lane_iota(shape):
    return lax.broadcasted_iota(jnp.int32, shape, len(shape) - 1)


def _seg_meansq(x, bd_ref, width):
    sq = x * x
    hi = sq.astype(BF16)
    lo = (sq - hi.astype(F32)).astype(BF16)
    bd = bd_ref[0:width, 0:width]
    return (_dot(hi, bd) + _dot(lo, bd)) * (1.0 / GQA_HD)


def _rope(x, cos, sin_signed, half):
    width = x.shape[-1]
    first = (_lane_iota(x.shape) % (2 * half)) < half
    partner = jnp.where(first, pltpu.roll(x, width - half, 1), pltpu.roll(x, half, 1))
    return x * cos + partner * sin_signed


def _tile_lanes(t, reps):
    return t if reps == 1 else jnp.concatenate([t] * reps, axis=1)


def _store_kvm(kvm_ref, kv, kpe_b):
    ones = jnp.ones(kpe_b.shape, BF16)
    for p in range(4):
        kvm_ref[:, p * 256:p * 256 + LANES] = kv[:, p * LANES:(p + 1) * LANES].astype(BF16)
        kvm_ref[:, p * 256 + LANES:(p + 1) * 256] = kpe_b
        kvm_ref[:, 1024 + p * 256:1024 + p * 256 + LANES] = (
            kv[:, 512 + p * LANES:512 + (p + 1) * LANES].astype(BF16))
        kvm_ref[:, 1024 + p * 256 + LANES:1024 + (p + 1) * 256] = ones


def _mod_kernel(c_ref, w_ref, b_ref, o_ref):
    a = _silu(c_ref[...]).astype(BF16)
    o_ref[...] = _dot(a, w_ref[...].astype(BF16)) + b_ref[...]


def _mod_call(l, cond, w_mod, b_mod):
    n_l, _, n = w_mod.shape
    tn = 1536
    return pl.pallas_call(
        _mod_kernel,
        grid=(n // tn,),
        in_specs=[_const_spec(cond.shape),
                  pl.BlockSpec((None, D, tn), lambda j: (l, 0, j)),
                  pl.BlockSpec((None, 1, tn), lambda j: (l, 0, j))],
        out_specs=pl.BlockSpec((cond.shape[0], tn), lambda j: (0, j)),
        out_shape=jax.ShapeDtypeStruct((cond.shape[0], n), F32),
        compiler_params=_cparams(("arbitrary",)),
        name="mod",
    )(cond, w_mod, b_mod.reshape(n_l, 1, n))


def _inprep_kernel(latent, *refs):
    (x_ref, mod_ref, gpre_ref, win_ref, gmq_ref, wuq_ref, gmkv_ref, wukv_ref,
     ggq_ref, ggk_ref, bd_ref) = refs[:11]
    refs = refs[11:]
    if latent:
        cos64_ref, sin64_ref, cospe_ref, sinpe_ref = refs[:4]
        refs = refs[4:]
    (qm_ref, kvm_ref, gqo_ref, gkv_ref, dqo_ref, dkv_ref, ret_ref, rg_ref, gate_ref) = refs[:9]
    refs = refs[9:]
    if not latent:
        ckv_o, kpe_o, gk_o, gv_o, dk_o, dv_o = refs

    x = x_ref[...]
    mod = mod_ref[...]
    sh1 = mod[:, 0:D]
    sc1 = mod[:, D:2 * D]
    hb = (_rms(x, gpre_ref[...]) * (1.0 + sc1) + sh1).astype(BF16)

    def z(a, b):
        return _dot_nt(hb, win_ref[a:b, :])

    if latent:
        cos64, sin64 = cos64_ref[...], sin64_ref[...]
        cospe, sinpe = cospe_ref[...], sinpe_ref[...]

    cqn = _rms(z(O_CQ, O_CKV), gmq_ref[...]).astype(BF16)
    q = _dot(cqn, wuq_ref[...]) * ((MLA_NOPE + MLA_ROPE) ** -0.5 * LOG2E)
    q_nope, q_pe = q[:, 0:512], q[:, 512:768]
    if latent:
        q_pe = _rope(q_pe, _tile_lanes(cospe, 2), _tile_lanes(sinpe, 2), MLA_ROPE // 4)
    qm_ref[:, 0:512] = q_nope.astype(BF16)
    qm_ref[:, 512:768] = q_pe.astype(BF16)

    ckvn = _rms(z(O_CKV, O_KPE), gmkv_ref[...])
    kv = _dot(ckvn.astype(BF16), wukv_ref[...])
    kpe4 = _dot_nt(hb, jnp.concatenate([win_ref[O_KPE:O_GQ, :]] * 4, axis=0))
    if latent:
        kpe4 = _rope(kpe4, cospe, sinpe, MLA_ROPE // 4)
    else:
        ckv_o[...] = ckvn
        kpe_o[...] = kpe4[:, 0:MLA_ROPE]
    _store_kvm(kvm_ref, kv, kpe4.astype(BF16))

    gq = _dot_nt(hb, jnp.concatenate(
        [win_ref[O_GQ + h * GQA_HD:O_GQ + (h + 1) * GQA_HD, :] for h in GQA_ORDER], axis=0))
    gq = gq * lax.rsqrt(_seg_meansq(gq, bd_ref, 512) + EPS) * ggq_ref[...]
    gk = z(O_GK, O_GV)
    gk = gk * lax.rsqrt(_seg_meansq(gk, bd_ref, LANES) + EPS) * ggk_ref[...]
    gv = z(O_GV, O_RQ)
    if latent:
        gq = _rope(gq, _tile_lanes(cos64, 4), _tile_lanes(sin64, 4), GQA_HD // 4)
        gk = _rope(gk, cos64, sin64, GQA_HD // 4)
    else:
        gk_o[...] = gk
        gv_o[...] = gv
    gqo_ref[...] = (gq * (GQA_HD ** -0.5 * LOG2E)).astype(BF16)
    gkv_ref[:, 0:LANES] = gk.astype(BF16)
    gkv_ref[:, LANES:2 * LANES] = gv.astype(BF16)
    gkv_ref[:, 2 * LANES:3 * LANES] = jnp.ones(gv.shape, BF16)

    dq = z(O_DQ, O_DK)
    dk = z(O_DK, O_DV)
    dv = z(O_DV, O_GL)
    if latent:
        dq = _rope(dq, _tile_lanes(cos64, 4), _tile_lanes(sin64, 4), DIFF_D // 4)
        dk = _rope(dk, _tile_lanes(cos64, 4), _tile_lanes(sin64, 4), DIFF_D // 4)
    else:
        dk_o[...] = dk
        dv_o[...] = dv
    dqo_ref[...] = (dq * (DIFF_D ** -0.5 * LOG2E)).astype(BF16)
    dkv_ref[:, 0:512] = dk.astype(BF16)
    for h in range(DIFF_HEADS):
        dkv_ref[:, 512 + h * 256:512 + h * 256 + LANES] = dv[:, h * LANES:(h + 1) * LANES].astype(BF16)
        dkv_ref[:, 512 + h * 256 + LANES:512 + (h + 1) * 256] = jnp.ones((dv.shape[0], LANES), BF16)

    ret_ref[:, 0:256] = z(O_RQ, O_RK).astype(BF16)
    ret_ref[:, 256:512] = (z(O_RK, O_RV) * (RET_DK ** -0.5)).astype(BF16)
    ret_ref[:, 512:1024] = z(O_RV, O_RG).astype(BF16)
    rg_ref[...] = z(O_RG, O_DQ).astype(BF16)

    for n in range(N_BRANCH):
        gate_ref[:, n * D:(n + 1) * D] = jax.nn.sigmoid(
            z(O_GL + n * D, O_GL + (n + 1) * D)).astype(BF16)


def _inprep_call(latent, l, x, mod3, mod_row, lw, tabs, t_len):
    n_tok = x.shape[0]
    nblk = n_tok // TM
    blk_per_seq = t_len // TM

    def tok(w):
        return pl.BlockSpec((TM, w), lambda i: (i, 0))

    in_specs = [tok(D),
                pl.BlockSpec((None, 1, 6 * D), lambda i: (mod_row(i), 0, 0)),
                _layer_spec(l, (1, D)),
                pl.BlockSpec((None, O_END, D), lambda i: (l, 0, 0), pipeline_mode=pl.Buffered(1)),
                _layer_spec(l, (1, MLA_Q_LORA)), _layer_spec(l, (MLA_Q_LORA, 768)),
                _layer_spec(l, (1, MLA_KV_LORA)), _layer_spec(l, (MLA_KV_LORA, 1024)),
                _layer_spec(l, (1, 512)), _layer_spec(l, (1, LANES)), _const_spec((512, 512))]
    args = [x, mod3, lw["g_pre1"], lw["w_in"], lw["g_mla_q"], lw["w_uq"], lw["g_mla_kv"],
            lw["w_ukv"], lw["g_gqa_q"], lw["g_gqa_k"], lw["bd"]]
    if latent:
        tab_spec = pl.BlockSpec((TM, LANES), lambda i: (i % blk_per_seq, 0))
        in_specs += [tab_spec] * 4
        args += list(tabs)
    widths = [768, KVM_W, 512, GKV_W, 512, DKV_W, 1024, 512, 4 * D]
    out_specs = [tok(w) for w in widths]
    out_shape = [jax.ShapeDtypeStruct((n_tok, w), BF16) for w in widths]
    if not latent:
        cw = [MLA_KV_LORA, MLA_ROPE, 128, 128, 512, 512]
        out_specs += [tok(w) for w in cw]
        out_shape += [jax.ShapeDtypeStruct((n_tok, w), F32) for w in cw]
    return pl.pallas_call(
        functools.partial(_inprep_kernel, latent),
        grid=(nblk,),
        in_specs=in_specs, out_specs=out_specs, out_shape=out_shape,
        compiler_params=_cparams(("arbitrary",)),
        name="inprep_lat" if latent else "inprep_ctx",
    )(*args)


def _pastkv_kernel(ckv_ref, kpe_ref, wukv_ref, o_ref):
    kv = _dot(ckv_ref[...].astype(BF16), wukv_ref[...])
    _store_kvm(o_ref, kv, kpe_ref[...].astype(BF16))


def _pastkv_call(l, ckv, kpe4, w_ukv):
    n = ckv.shape[0]
    return pl.pallas_call(
        _pastkv_kernel,
        grid=(n // TM,),
        in_specs=[pl.BlockSpec((TM, MLA_KV_LORA), lambda i: (i, 0)),
                  pl.BlockSpec((TM, LANES), lambda i: (i, 0)),
                  _layer_spec(l, (MLA_KV_LORA, 1024))],
        out_specs=pl.BlockSpec((TM, KVM_W), lambda i: (i, 0)),
        out_shape=jax.ShapeDtypeStruct((n, KVM_W), BF16),
        compiler_params=_cparams(("arbitrary",)),
        name="pastkv",
    )(ckv, kpe4, w_ukv)


def _softmax_pv(s, v_ones):
    m = jnp.max(s, axis=-1, keepdims=True)
    p = jnp.exp2(s - m).astype(BF16)
    o = _dot(p, v_ones)
    return o[:, 0:LANES] / o[:, LANES:2 * LANES]


def _attn_kernel(lam_init, qm_ref, kvm_ref, gq_ref, gkv_ref, dq_ref, dkv_ref,
                 lam_ref, gdiff_ref, o_ref):
    tq = qm_ref.shape[0]
    lane = _lane_iota((tq, LANES))
    low = lane < HALF_LANES
    zero = jnp.zeros((tq, LANES), BF16)

    for p in range(MLA_HEADS // 2):
        qn = qm_ref[:, p * LANES:(p + 1) * LANES]
        g = p // 2
        qpe = qm_ref[:, 512 + g * LANES:512 + (g + 1) * LANES]
        kk = kvm_ref[:, p * 256:(p + 1) * 256]
        vv = kvm_ref[:, 1024 + p * 256:1024 + (p + 1) * 256]
        outs = []
        for half in range(2):
            h = 2 * p + half
            slot = h % 4
            in_slot = (lane >= slot * MLA_ROPE) & (lane < (slot + 1) * MLA_ROPE)
            lhs = jnp.concatenate(
                [jnp.where(low if half == 0 else ~low, qn, zero),
                 jnp.where(in_slot, qpe, zero)], axis=1)
            outs.append(_softmax_pv(_dot_nt(lhs, kk), vv))
        o_ref[:, p * LANES:(p + 1) * LANES] = jnp.where(low, outs[0], outs[1]).astype(BF16)

    kk = gkv_ref[:, 0:LANES]
    vv = gkv_ref[:, LANES:3 * LANES]
    for g in range(GQA_HEADS // 2):
        qg = gq_ref[:, g * LANES:(g + 1) * LANES]
        o_lo = _softmax_pv(_dot_nt(jnp.where(low, qg, zero), kk), vv)
        o_hi = _softmax_pv(_dot_nt(jnp.where(low, zero, qg), kk), vv)
        o_ref[:, 512 + g * LANES:512 + (g + 1) * LANES] = jnp.where(low, o_lo, o_hi).astype(BF16)

    lp = lam_ref[...]
    lam = (jnp.exp(jnp.sum(lp[0:1] * lp[1:2], axis=-1, keepdims=True))
           - jnp.exp(jnp.sum(lp[2:3] * lp[3:4], axis=-1, keepdims=True)) + lam_init)
    for h in range(DIFF_HEADS):
        qh = dq_ref[:, h * LANES:(h + 1) * LANES]
        kk = dkv_ref[:, h * LANES:(h + 1) * LANES]
        vv = dkv_ref[:, 512 + h * 256:512 + (h + 1) * 256]
        a1 =_softmax_pv(_dot_nt(jnp.where(low, qh, zero), kk), vv)
        a2 = _softmax_pv(_dot_nt(jnp.where(low, zero, qh), kk), vv)
        od = _rms(a1 - lam * a2, gdiff_ref[...]) * (1.0 - lam_init)
        o_ref[:, 1024 + h * LANES:1024 + (h + 1) * LANES] = od.astype(BF16)


def _attn_call(l, lam_init, qm, kvm, gq, gkv, dq, dkv, lam_p, g_diff, n_b, t_len, s_len):
    nq = t_len // TQ

    def qspec(w):
        return pl.BlockSpec((TQ, w), lambda b, i: (b * nq + i, 0))

    def kspec(w):
        return pl.BlockSpec((s_len, w), lambda b, i: (b, 0))

    return pl.pallas_call(
        functools.partial(_attn_kernel, lam_init),
        grid=(n_b, nq),
        in_specs=[qspec(768), kspec(KVM_W), qspec(512), kspec(GKV_W), qspec(512), kspec(DKV_W),
                  _layer_spec(l, (4, DIFF_D)), _layer_spec(l, (1, DIFF_DV))],
        out_specs=qspec(3 * BRANCH_W),
        out_shape=jax.ShapeDtypeStruct((n_b * t_len, 3 * BRANCH_W), BF16),
        compiler_params=_cparams(("arbitrary", "arbitrary")),
        name="attn",
    )(qm, kvm, gq, gkv, dq, dkv, lam_p, g_diff)


def _log_sigmoid(x):
    return jnp.minimum(x, 0.0) - jnp.log(1.0 + jnp.exp(-jnp.abs(x)))


def _log_gamma(dec_ref, l, d, h):
    return _log_sigmoid(jnp.full((1, 1), dec_ref[l, d, h], F32))


def _ret_kernel(latent, l, t_len, dec_ref, q_ref, k_ref, v_ref, rg_ref, gret_ref, *refs):
    if latent:
        s0_ref, o_ref = refs
    else:
        o_ref, st_ref = refs
    tq = q_ref.shape[0]
    t0 = pl.program_id(1) * tq
    lane = _lane_iota((tq, LANES))
    low = lane < HALF_LANES
    zero = jnp.zeros((tq, LANES), BF16)
    t_idx = (t0 + lax.broadcasted_iota(jnp.int32, (tq, t_len), 0)).astype(F32)
    s_idx = lax.broadcasted_iota(jnp.int32, (tq, t_len), 1).astype(F32)
    dist = t_idx - s_idx
    past = dist >= 0
    diag = jnp.where(dist == 0, 1.0, 0.0)
    t_col = (t0 + lax.broadcasted_iota(jnp.int32, (tq, 1), 0)).astype(F32)

    def lg(d, h):
        return _log_gamma(dec_ref, l, d, h)

    for h in range(RET_HEADS):
        p, half = h // 2, h % 2
        qp = q_ref[:, p * LANES:(p + 1) * LANES]
        qm = jnp.where(low if half == 0 else ~low, qp, zero)
        kp = k_ref[:, p * LANES:(p + 1) * LANES]
        vh = v_ref[:, h * LANES:(h + 1) * LANES]
        lgf, lgb = lg(0, h), lg(1, h)
        dmask = jnp.exp(jnp.where(past, lgf, -lgb) * dist) + diag
        o = _dot((_dot_nt(qm, kp) * dmask).astype(BF16), vh)
        if latent:
            sf =s0_ref[0, p].astype(BF16)
            sb = s0_ref[1, p].astype(BF16)
            o = o + _dot(qm, sf) * jnp.exp(lgf * (t_col + 1.0))
            o = o + _dot(qm, sb) * jnp.exp(lgb * (float(t_len) - t_col))
        mu = jnp.mean(o, axis=-1, keepdims=True)
        oc = o - mu
        y = oc * lax.rsqrt(jnp.mean(oc * oc, axis=-1, keepdims=True) + EPS)
        y = y * gret_ref[:, h * LANES:(h + 1) * LANES]
        rg = rg_ref[:, h * LANES:(h + 1) * LANES].astype(F32)
        o_ref[:, h * LANES:(h + 1) * LANES] = (y * _silu(rg)).astype(BF16)

    if not latent:
        s_col = lax.broadcasted_iota(jnp.int32, (t_len, 1), 0).astype(F32)
        lane_t = _lane_iota((1, LANES)) < HALF_LANES
        for p in range(RET_HEADS // 2):
            kp = k_ref[:, p * LANES:(p + 1) * LANES].astype(F32)
            for d in range(2):
                lg_lane = jnp.where(lane_t, lg(d, 2 * p), lg(d, 2 * p + 1))
                expo = (float(t_len) - 1.0 - s_col) if d == 0 else s_col
                kdec_t = jnp.transpose(kp * jnp.exp(lg_lane * expo)).astype(BF16)
                for half in range(2):
                    h = 2 * p + half
                    st = _dot(kdec_t, v_ref[:, h * LANES:(h + 1) * LANES])
                    st_ref[d, h] = st[half * RET_DK:(half + 1) * RET_DK, :]


def _ret_call(latent, l, dec, ret, rg, g_ret, s0, n_b, t_len):
    nq = t_len // TQ
    assert latent or nq == 1
    in_specs = [pl.BlockSpec(memory_space=pltpu.SMEM),
                pl.BlockSpec((TQ, 256), lambda b, i: (b * nq + i, 0)),
                pl.BlockSpec((t_len, 256), lambda b, i: (b, 1)),
                pl.BlockSpec((t_len, 512), lambda b, i: (b, 1)),
                pl.BlockSpec((TQ, 512), lambda b, i: (b * nq + i, 0)),
                _layer_spec(l, (1, 512))]
    args = [dec, ret, ret, ret, rg, g_ret]
    out_specs = [pl.BlockSpec((TQ, 512), lambda b, i: (b * nq + i, 0))]
    out_shape = [jax.ShapeDtypeStruct((n_b * t_len, 512), BF16)]
    if latent:
        in_specs.append(pl.BlockSpec((None, 2, 2, LANES, LANES), lambda b, i: (b, 0, 0, 0, 0)))
        args.append(s0)
    else:
        out_specs.append(pl.BlockSpec((None, 2, RET_HEADS, RET_DK, RET_DV),
                                      lambda b, i: (b, 0, 0, 0, 0)))
        out_shape.append(jax.ShapeDtypeStruct((n_b, 2, RET_HEADS, RET_DK, RET_DV), F32))
    return pl.pallas_call(
        functools.partial(_ret_kernel, latent, l, t_len),
        grid=(n_b, nq),
        in_specs=in_specs, out_specs=out_specs, out_shape=out_shape,
        compiler_params=_cparams(("arbitrary", "arbitrary")),
        name="ret_lat" if latent else "ret_ctx",
    )(*args)


def _merge_kernel(x_ref, mod_ref, br_ref, or_ref, gate_ref, wbr_ref, wout_ref, gpost_ref, o_ref):
    merged = None
    for n in range(N_BRANCH):
        if n < 2:
            b = br_ref[:, n * BRANCH_W:(n + 1) * BRANCH_W]
        elif n == 2:
            b = or_ref[...]
        else:
            b = br_ref[:, 2 * BRANCH_W:3 * BRANCH_W]
        t = gate_ref[:, n * D:(n + 1) * D].astype(F32) * _dot(b, wbr_ref[n])
        merged = t if merged is None else merged + t
    out = _dot(merged.astype(BF16), wout_ref[...])
    g1 = mod_ref[...][:, 2 * D:3 * D]
    o_ref[...] = x_ref[...] + g1 * _rms(out, gpost_ref[...])


def _merge_call(l, x, mod3, mod_row, br, o_r, gates, lw):
    n_tok = x.shape[0]

    def tok(w):
        return pl.BlockSpec((TM, w), lambda i: (i, 0))

    return pl.pallas_call(
        _merge_kernel,
        grid=(n_tok // TM,),
        in_specs=[tok(D), pl.BlockSpec((None, 1, 6 * D), lambda i: (mod_row(i), 0, 0)),
                  tok(3 * BRANCH_W), tok(BRANCH_W), tok(4 * D),
                  _layer_spec(l, (N_BRANCH, BRANCH_W, D)), _layer_spec(l, (D, D)),
                  _layer_spec(l, (1, D))],
        out_specs=tok(D),
        out_shape=jax.ShapeDtypeStruct((n_tok, D), F32),
        compiler_params=_cparams(("arbitrary",)),
        name="merge",
    )(x, mod3, br, o_r, gates, lw["w_br"], lw["w_out"], lw["g_post1"])


def _route(logits_t, bias):
    n = logits_t.shape[1]
    scores = jax.nn.sigmoid(logits_t)
    sel = scores + bias
    neg = -jnp.inf
    sub = lax.broadcasted_iota(jnp.int32, (GROUP_SIZE, n), 0)
    grp = []
    for g in range(N_GROUPS):
        blk = sel[g * GROUP_SIZE:(g + 1) * GROUP_SIZE]
        m1 = jnp.max(blk, axis=0, keepdims=True)
        i1 = jnp.min(jnp.where(blk == m1, sub, GROUP_SIZE), axis=0, keepdims=True)
        m2 = jnp.max(jnp.where(sub == i1, neg, blk), axis=0, keepdims=True)
        grp.append(m1 + m2)
    parts = []
    for g in range(N_GROUPS):
        beaten = jnp.zeros((1, n), jnp.int32)
        for o in range(N_GROUPS):
            if o == g:
                continue
            wins = (grp[o] > grp[g]) | (grp[o] == grp[g]) if o < g else (grp[o] > grp[g])
            beaten = beaten + wins.astype(jnp.int32)
        keep = beaten < TOPK_GROUPS
        parts.append(jnp.where(keep, sel[g * GROUP_SIZE:(g + 1) * GROUP_SIZE], neg))
    cur = jnp.concatenate(parts, axis=0)
    eidx = lax.broadcasted_iota(jnp.int32, (N_EXPERTS, n), 0)
    hits, ids, ws = [], [], []
    for _ in range(TOP_K):
        m = jnp.max(cur, axis=0, keepdims=True)
        i = jnp.min(jnp.where(cur == m, eidx, N_EXPERTS), axis=0, keepdims=True)
        hit = eidx == i
        hits.append(hit)
        ids.append(i)
        ws.append(jnp.sum(jnp.where(hit, scores, 0.0), axis=0, keepdims=True))
        cur = jnp.where(hit, neg, cur)
    wsum = ws[0] + ws[1] + ws[2] + ws[3]
    return hits, ids, [w / wsum * ROUTE_SCALE for w in ws]


U32 = jnp.uint32
HIGH16 = np.uint32(0xFFFF0000)


def _bf16_bits(v):
    return lax.bitcast_convert_type(v.astype(BF16).astype(F32), U32)


def _pack_rows(v):
    return (_bf16_bits(v[:, 0:D // 2]) >> 16) | _bf16_bits(v[:, D // 2:D])


def _unpack_rows(p):
    lo = lax.bitcast_convert_type(p << 16, F32)
    hi = lax.bitcast_convert_type(p & HIGH16, F32)
    return jnp.concatenate([lo, hi], axis=1)


def _moe_pre_kernel(x_ref, mod_ref, gpre_ref, wr_ref, br_ref, tri_ref,
                    hp_ref, eidx_ref, rank_ref, comb_ref, cnt_ref, run_ref):
    tm = x_ref.shape[0]

    @pl.when(pl.program_id(0) == 0)
    def _():
        run_ref[...] = jnp.zeros_like(run_ref)

    mod = mod_ref[...]
    sh2, sc2 = mod[:, 3 * D:4 * D], mod[:, 4 * D:5 * D]
    h = _rms(x_ref[...], gpre_ref[...]) * (1.0 + sc2) + sh2
    hp_ref[...] = _pack_rows(h)
    hb = h.astype(BF16)
    h_lo = (h - hb.astype(F32)).astype(BF16)
    wr = wr_ref[...]
    wr_hi = wr.astype(BF16)
    wr_lo = (wr - wr_hi.astype(F32)).astype(BF16)
    logits_t = _dot_nt(wr_hi, hb) + _dot_nt(wr_hi, h_lo) + _dot_nt(wr_lo, hb)
    hits, ids, ws = _route(logits_t, br_ref[...])

    picked = jnp.zeros((N_EXPERTS, tm), F32)
    for hit in hits:
        picked = jnp.where(hit, 1.0, picked)
    before = _dot(picked.astype(BF16), tri_ref[...]) + run_ref[:, 0:1]
    sub8 = lax.broadcasted_iota(jnp.int32, (8, tm), 0)
    comb8 = jnp.zeros((8, tm), F32)
    for k in range(TOP_K):
        rank = jnp.sum(jnp.where(hits[k], before, 0.0), axis=0, keepdims=True)
        eidx_ref[k:k + 1, :] = ids[k]
        rank_ref[k:k + 1, :] = rank.astype(jnp.int32)
        comb8 = jnp.where(sub8 == k, ws[k], comb8)
    comb_ref[...] = jnp.transpose(
        jnp.concatenate([comb8, jnp.zeros((LANES - 8, tm), F32)], axis=0))
    run_ref[...] = run_ref[...] + jnp.sum(picked, axis=1, keepdims=True)
    cnt_ref[...] = run_ref[...]


def _moe_pre_call(l, x, mod3, mod_row, lw):
    n_tok = x.shape[0]
    tm = TM_MOE_PRE
    tri = np.arange(tm)
    tri = jnp.asarray(tri[:, None] < tri[None, :], BF16)
    row4 = pl.BlockSpec((TOP_K, tm), lambda i: (0, i))
    return pl.pallas_call(
        _moe_pre_kernel,
        grid=(n_tok // tm,),
        in_specs=[pl.BlockSpec((tm, D), lambda i: (i, 0)),
                  pl.BlockSpec((None, 1, 6 * D), lambda i: (mod_row(i), 0, 0)),
                  _layer_spec(l, (1, D)), _layer_spec(l, (N_EXPERTS, D)),
                  _layer_spec(l, (N_EXPERTS, 1)), _const_spec((tm, tm))],
        out_specs=[pl.BlockSpec((tm, D // 2), lambda i: (i, 0)), row4, row4,
                   pl.BlockSpec((tm, LANES), lambda i: (i, 0)),
                   _const_spec((N_EXPERTS, LANES))],
        out_shape=[jax.ShapeDtypeStruct((n_tok, D // 2), U32),
                   jax.ShapeDtypeStruct((TOP_K, n_tok), jnp.int32),
                   jax.ShapeDtypeStruct((TOP_K, n_tok), jnp.int32),
                   jax.ShapeDtypeStruct((n_tok, LANES), F32),
                   jax.ShapeDtypeStruct((N_EXPERTS, LANES), F32)],
        scratch_shapes=[pltpu.VMEM((N_EXPERTS, LANES), F32)],
        compiler_params=_cparams(("arbitrary",)),
        name="moe_pre",
    )(x, mod3, lw["g_pre2"], lw["w_router_t"], lw["b_router"], tri)


def _moe_plan_kernel(eidx_ref, rank_ref, cnt_ref, dest_ref, te_ref, tv_ref, tn_ref):
    tm = eidx_ref.shape[1]
    cnt = cnt_ref[...]
    padded = jnp.ceil(cnt * (1.0 / TMX)) * TMX
    row = lax.broadcasted_iota(jnp.int32, cnt.shape, 0)
    incl = padded
    shift = 1
    while shift < N_EXPERTS:
        incl = incl + jnp.where(row >= shift, pltpu.roll(incl, shift, 0), 0.0)
        shift *= 2
    start = (incl - padded)[:, 0:1]
    end = incl[:, 0:1]
    erow = lax.broadcasted_iota(jnp.int32, (N_EXPERTS, tm), 0)
    for k in range(TOP_K):
        mine = erow == eidx_ref[k:k + 1, :]
        base = jnp.sum(jnp.where(mine, start, 0.0), axis=0, keepdims=True)
        dest_ref[k:k + 1, :] = rank_ref[k:k + 1, :] + base.astype(jnp.int32)

    @pl.when(pl.program_id(0) == 0)
    def _():
        tile0 = (_lane_iota((1, LANES)) * TMX).astype(F32)
        owner = jnp.sum(jnp.where(end <= tile0, 1.0, 0.0), axis=0, keepdims=True)
        owner = jnp.minimum(owner, N_EXPERTS - 1.0)
        erow_t = lax.broadcasted_iota(jnp.int32, (N_EXPERTS, LANES), 0).astype(F32)
        left = jnp.sum(jnp.where(erow_t == owner, cnt[:, 0:1] - (tile0 - start), 0.0),
                       axis=0, keepdims=True)
        te_ref[...] = owner.astype(jnp.int32)
        tv_ref[...] = jnp.clip(left, 0.0, float(TMX)).astype(jnp.int32)
        tn_ref[...] = jnp.full(tn_ref.shape, N_EXPERTS, jnp.int32)
        nxt = owner
        for k in range(W_SLOTS - 1):
            later = (erow_t > nxt) & (cnt[:, 0:1] > 0.0)
            nxt = jnp.min(jnp.where(later, erow_t, float(N_EXPERTS)), axis=0, keepdims=True)
            tn_ref[k:k + 1, :] = nxt.astype(jnp.int32)


def _moe_plan_call(eidx, rank, cnt):
    n_tok = eidx.shape[1]
    tm = TM_MOE_PRE
    row4 = pl.BlockSpec((TOP_K, tm), lambda i: (0, i))
    tiles = jax.ShapeDtypeStruct((1, LANES), jnp.int32)
    return pl.pallas_call(
        _moe_plan_kernel,
        grid=(n_tok // tm,),
        in_specs=[row4, row4, _const_spec((N_EXPERTS, LANES))],
        out_specs=[row4, _const_spec((1, LANES)), _const_spec((1, LANES)), _const_spec((8, LANES))],
        out_shape=[jax.ShapeDtypeStruct((TOP_K, n_tok), jnp.int32), tiles, tiles,
                   jax.ShapeDtypeStruct((8, LANES), jnp.int32)],
        compiler_params=_cparams(("arbitrary",)),
        name="moe_plan",
    )(eidx, rank, cnt)


def _experts_kernel(l, te_ref, tv_ref, tn_ref, xs_ref, wgu_hbm, wdn_hbm, ys_ref,
                    wgu_f, wdn_f, wgu_b, wdn_b, sem, group_ref):
    j = pl.program_id(0)
    valid = tv_ref[j]
    expert = te_ref[j]

    def fetch(e, slot):
        return (pltpu.make_async_copy(wgu_hbm.at[l, e], wgu_f.at[slot], sem.at[slot, 0]),
                pltpu.make_async_copy(wdn_hbm.at[l, e], wdn_f.at[slot], sem.at[slot, 1]))

    def start_if_any(e, slot):
        @pl.when(e < N_EXPERTS)
        def _():
            for cp in fetch(e, slot):
                cp.start()

    @pl.when(j == 0)
    def _():
        group_ref[0] = 0
        start_if_any(expert, 0)
        for k in range(W_SLOTS - 2):
            start_if_any(tn_ref[k, 0], k + 1)

    first_tile = (j == 0) | (expert != te_ref[jnp.maximum(j - 1, 0)])

    @pl.when(first_tile & (valid > 0))
    def _():
        group = group_ref[0]
        slot = lax.rem(group, W_SLOTS)
        for cp in fetch(expert, slot):
            cp.wait()
        wgu_b[...] = wgu_f[slot].astype(BF16)
        wdn_b[...] = wdn_f[slot].astype(BF16)
        start_if_any(tn_ref[W_SLOTS - 2, j], lax.rem(group + W_SLOTS - 1, W_SLOTS))
        group_ref[0] = group + 1

    @pl.when(valid > 0)
    def _():
        rows = lax.broadcasted_iota(jnp.int32, (TMX, D), 0)
        x = jnp.where(rows < valid, _unpack_rows(xs_ref[...]), 0.0).astype(BF16)
        gu = _dot(x, wgu_b[...])
        a = _silu(gu[:, 0:EXPERT_FF]) * gu[:, EXPERT_FF:2 * EXPERT_FF]
        ys_ref[...] = _pack_rows(_dot(a.astype(BF16), wdn_b[...]))

    @pl.when(valid <= 0)
    def _():
        ys_ref[...] = jnp.zeros_like(ys_ref)


def _experts_call(l, xs, te, tv, tn, w_gu, w_dn):
    n_tiles = xs.shape[0] // TMX
    grid_spec = pltpu.PrefetchScalarGridSpec(
        num_scalar_prefetch=3,
        grid=(n_tiles,),
        in_specs=[pl.BlockSpec((TMX, D // 2), lambda j, *_: (j, 0)),
                  pl.BlockSpec(memory_space=pl.ANY), pl.BlockSpec(memory_space=pl.ANY)],
        out_specs=pl.BlockSpec((TMX, D // 2), lambda j, *_: (j, 0)),
        scratch_shapes=[pltpu.VMEM((W_SLOTS, D, 2 * EXPERT_FF), F32),
                        pltpu.VMEM((W_SLOTS, EXPERT_FF, D), F32),
                        pltpu.VMEM((D, 2 * EXPERT_FF), BF16), pltpu.VMEM((EXPERT_FF, D), BF16),
                        pltpu.SemaphoreType.DMA((W_SLOTS, 2)), pltpu.SMEM((1,), jnp.int32)])
    return pl.pallas_call(
        functools.partial(_experts_kernel, l),
        grid_spec=grid_spec,
        out_shape=jax.ShapeDtypeStruct(xs.shape, U32),
        compiler_params=_cparams(("arbitrary",)),
        name="moe_experts",
    )(te, tv, tn, xs, w_gu, w_dn)


def _moe_post_kernel(x_ref, mod_ref, hp_ref, yg_ref, comb_ref, wsgu_ref, wsdn_ref, gpost_ref,
                     o_ref):
    hb = _unpack_rows(hp_ref[...]).astype(BF16)
    sgu = _dot(hb, wsgu_ref[...])
    sa = _silu(sgu[:, 0:SHARED_FF]) * sgu[:, SHARED_FF:2 * SHARED_FF]
    acc = _dot(sa.astype(BF16), wsdn_ref[...])
    comb = comb_ref[...]
    for k in range(TOP_K):
        acc = acc + comb[:, k:k + 1] * _unpack_rows(yg_ref[k])
    g2 = mod_ref[...][:, 5 * D:6 * D]
    o_ref[...] = x_ref[...] + g2 * _rms(acc, gpost_ref[...])


def _moe_post_call(l, x, mod3, mod_row, hp, yg, comb, lw):
    n_tok = x.shape[0]
    tm = TM_MOE_PRE
    return pl.pallas_call(
        _moe_post_kernel,
        grid=(n_tok // tm,),
        in_specs=[pl.BlockSpec((tm, D), lambda i: (i, 0)),
                  pl.BlockSpec((None, 1, 6 * D), lambda i: (mod_row(i), 0, 0)),
                  pl.BlockSpec((tm, D // 2), lambda i: (i, 0)),
                  pl.BlockSpec((TOP_K, tm, D // 2), lambda i: (0, i, 0)),
                  pl.BlockSpec((tm, LANES), lambda i: (i, 0)),
                  _layer_spec(l, (D, 2 * SHARED_FF)), _layer_spec(l, (SHARED_FF, D)),
                  _layer_spec(l, (1, D))],
        out_specs=pl.BlockSpec((tm, D), lambda i: (i, 0)),
        out_shape=jax.ShapeDtypeStruct((n_tok, D), F32),
        compiler_params=_cparams(("arbitrary",)),
        name="moe_post",
    )(x, mod3, hp, yg, comb, lw["w_sh_gu"], lw["w_sh_down"], lw["g_post2"])


def _moe_call(l, x, mod3, mod_row, lw):
    n_tok = x.shape[0]
    n_slots = -(-(TOP_K * n_tok + N_EXPERTS * (TMX - 1)) // TMX) * TMX
    assert n_slots // TMX <= LANES
    hp, eidx, rank, comb, cnt = _moe_pre_call(l, x, mod3, mod_row, lw)
    dest, te, tv, tn = _moe_plan_call(eidx, rank, cnt)
    dest = dest.reshape(TOP_K * n_tok)
    xs = _sc_scatter_rows(hp, dest, n_slots)
    ys = _experts_call(l, xs, te[0], tv[0], tn, lw["w_exp_gu"], lw["w_exp_down"])
    yg = _sc_gather_rows(ys, dest).reshape(TOP_K, n_tok, D // 2)
    return _moe_post_call(l, x, mod3, mod_row, hp, yg, comb, lw)


SC_CORES, SC_SUBCORES = 2, 16
SC_WORKERS = SC_CORES * SC_SUBCORES


def _sc_gather_rows(table, idx, chunk=64):
    n_out, width = idx.shape[0], table.shape[1]
    per_worker = n_out // SC_WORKERS
    n_chunks = per_worker // chunk
    assert per_worker * SC_WORKERS == n_out and n_chunks * chunk == per_worker
    mesh = plsc.VectorSubcoreMesh(core_axis_name="c", subcore_axis_name="s",
                                  num_cores=SC_CORES, num_subcores=SC_SUBCORES)

    @functools.partial(
        pl.kernel, mesh=mesh,
        out_type=jax.ShapeDtypeStruct((n_out, width), table.dtype),
        scratch_types=[pltpu.VMEM((chunk,), jnp.int32), pltpu.VMEM((chunk, width), table.dtype),
                       pltpu.SemaphoreType.DMA],
        name="sc_gather")
    def gather(table_hbm, idx_hbm, out_hbm, idx_v, rows_v, sem):
        base = (lax.axis_index("s") * SC_CORES + lax.axis_index("c")) * per_worker

        @pl.loop(0, n_chunks)
        def _(j):
            off = base + j * chunk
            pltpu.sync_copy(idx_hbm.at[pl.ds(off, chunk)], idx_v)
            pltpu.async_copy(table_hbm.at[idx_v], rows_v, sem).wait()
            pltpu.sync_copy(rows_v, out_hbm.at[pl.ds(off, chunk)])

    return gather(table, idx)


def _sc_scatter_rows(rows, dest, n_slots, chunk=64):
    n_tok, width = rows.shape
    per_worker = n_tok // SC_WORKERS
    n_chunks = per_worker // chunk
    assert per_worker * SC_WORKERS == n_tok and n_chunks * chunk == per_worker
    mesh = plsc.VectorSubcoreMesh(core_axis_name="c", subcore_axis_name="s",
                                  num_cores=SC_CORES, num_subcores=SC_SUBCORES)

    @functools.partial(
        pl.kernel, mesh=mesh,
        out_type=jax.ShapeDtypeStruct((n_slots, width), rows.dtype),
        scratch_types=[pltpu.VMEM((chunk,), jnp.int32), pltpu.VMEM((chunk, width), rows.dtype)],
        name="sc_scatter")
    def scatter(rows_hbm, dest_hbm, out_hbm, idx_v, rows_v):
        base = (lax.axis_index("s") * SC_CORES + lax.axis_index("c")) * per_worker

        @pl.loop(0, n_chunks)
        def _(j):
            off = base + j * chunk
            pltpu.sync_copy(rows_hbm.at[pl.ds(off, chunk)], rows_v)
            for k in range(TOP_K):
                pltpu.sync_copy(dest_hbm.at[pl.ds(k * n_tok + off, chunk)], idx_v)
                pltpu.sync_copy(rows_v, out_hbm.at[idx_v])

    return scatter(rows, dest)


def _rope_tables(t_len):
    pos = np.arange(t_len)
    row, col = pos // GRID_W, pos % GRID_W

    def tab(r):
        half = r // 2
        freq = ROPE_BASE ** (-np.arange(half, dtype=np.float64) / half)
        sign = np.concatenate([-np.ones(half), np.ones(half)])
        cs, sn = [], []
        for p in (row, col):
            ang = p[:, None].astype(np.float64) * freq[None, :]
            cs.append(np.concatenate([np.cos(ang), np.cos(ang)], axis=1))
            sn.append(np.concatenate([np.sin(ang), np.sin(ang)], axis=1) * sign[None, :])
        return np.concatenate(cs, axis=1), np.concatenate(sn, axis=1)

    c64, s64 = tab(GQA_HD // 2)
    cpe, spe = tab(MLA_ROPE // 2)
    out = (np.tile(c64, (1, 2)), np.tile(s64, (1, 2)), np.tile(cpe, (1, 4)), np.tile(spe, (1, 4)))
    return tuple(jnp.asarray(a, F32) for a in out)


def _prep_weights(p):
    n_l = p["w_in"].shape[0]

    def row(name):
        return p[name].reshape(n_l, 1, -1)

    w_uq = p["w_mla_uq"].reshape(n_l, MLA_Q_LORA, MLA_HEADS, MLA_NOPE + MLA_ROPE)
    w_uq = jnp.concatenate([w_uq[..., :MLA_NOPE].reshape(n_l, MLA_Q_LORA, -1),
                            w_uq[..., MLA_NOPE:].reshape(n_l, MLA_Q_LORA, -1)], axis=-1)
    w_ukv = p["w_mla_ukv"].reshape(n_l, MLA_KV_LORA, MLA_HEADS, MLA_NOPE + MLA_V)
    w_ukv = jnp.concatenate([w_ukv[..., :MLA_NOPE].reshape(n_l, MLA_KV_LORA, -1),
                             w_ukv[..., MLA_NOPE:].reshape(n_l, MLA_KV_LORA, -1)], axis=-1)
    w_br = p["w_br"]
    w_br_gqa = w_br[:, 1].reshape(n_l, GQA_HEADS, GQA_HD, D)[:, jnp.array(GQA_ORDER)]
    w_br = jnp.concatenate([w_br[:, 0:1], w_br_gqa.reshape(n_l, 1, BRANCH_W, D), w_br[:, 2:4]], axis=1)
    blk = np.arange(512) // GQA_HD
    return {
        "g_pre1": row("g_pre1"), "g_post1": row("g_post1"),
        "g_pre2": row("g_pre2"), "g_post2": row("g_post2"),
        "w_in": jnp.swapaxes(p["w_in"], 1, 2).astype(BF16),
        "g_mla_q": row("g_mla_q"), "w_uq": w_uq.astype(BF16),
        "g_mla_kv": row("g_mla_kv"), "w_ukv": w_ukv.astype(BF16),
        "g_gqa_q": jnp.tile(p["g_gqa_q"], (1, GQA_HEADS)).reshape(n_l, 1, -1),
        "g_gqa_k": jnp.tile(p["g_gqa_k"], (1, GQA_KV_HEADS)).reshape(n_l, 1, -1),
        "bd": jnp.asarray(blk[:, None] == blk[None, :], BF16),
        "ret_decay": p["ret_decay"],
        "g_ret": row("g_ret"),
        "diff_lambda": p["diff_lambda"], "g_diff": row("g_diff"),
        "w_br": w_br.astype(BF16), "w_out": p["w_out"].astype(BF16),
        "w_router_t": jnp.swapaxes(p["w_router"], 1, 2),
        "b_router": p["b_router"].reshape(n_l, N_EXPERTS, 1),
        "w_exp_gu": p["w_exp_gu"], "w_exp_down": p["w_exp_down"],
        "w_sh_gu": p["w_sh_gu"].astype(BF16), "w_sh_down": p["w_sh_down"].astype(BF16),
    }


def _mixers(latent, l, x, mod3, mod_row, lw, n_b, t_len, tabs=None, past=None, s0=None):
    lam_init = 0.8 - 0.6 * math.exp(-0.3 * l)
    outs = _inprep_call(latent, l, x, mod3, mod_row, lw, tabs, t_len)
    qm, kvm, gq, gkv, dq, dkv, ret, rg, gates = outs[:9]
    s_len = t_len
    if latent:
        past_kvm, past_gkv, past_dkv = past
        p_len = past_gkv.shape[1]
        s_len = p_len + t_len

        def cat(a, b):
            return jnp.concatenate([a, b.reshape(n_b, t_len, -1)], axis=1).reshape(n_b * s_len, -1)

        kvm, gkv, dkv = cat(past_kvm, kvm), cat(past_gkv, gkv), cat(past_dkv, dkv)
    br = _attn_call(l, lam_init, qm, kvm, gq, gkv, dq, dkv, lw["diff_lambda"], lw["g_diff"],
                    n_b, t_len, s_len)
    r = _ret_call(latent, l, lw["ret_decay"], ret, rg, lw["g_ret"], s0, n_b, t_len)
    y = _merge_call(l, x, mod3, mod_row, br, r[0], gates, lw)
    cache = None if latent else tuple(outs[9:]) + (r[1],)
    return y, cache


def kernel(x_prompt, x_sample, cache_mla_ckv, cache_mla_kpe, cache_gqa_k, cache_gqa_v, cache_diff_k, cache_diff_v, state_ret, c, c_ctx, w_mod, b_mod, g_pre1, g_post1, g_pre2, g_post2, w_in, g_mla_q, w_mla_uq, g_mla_kv, w_mla_ukv, g_gqa_q, g_gqa_k, ret_decay, g_ret, diff_lambda, g_diff, w_br, w_out, w_router, b_router, w_exp_gu, w_exp_down, w_sh_gu, w_sh_down):
    params = dict(w_in=w_in, g_pre1=g_pre1, g_post1=g_post1, g_pre2=g_pre2,
                  g_post2=g_post2, g_mla_q=g_mla_q, w_mla_uq=w_mla_uq,
                  g_mla_kv=g_mla_kv, w_mla_ukv=w_mla_ukv, g_gqa_q=g_gqa_q, g_gqa_k=g_gqa_k,
                  ret_decay=ret_decay, g_ret=g_ret, diff_lambda=diff_lambda, g_diff=g_diff,
                  w_br=w_br, w_out=w_out, w_router=w_router, b_router=b_router,
                  w_exp_gu=w_exp_gu, w_exp_down=w_exp_down, w_sh_gu=w_sh_gu, w_sh_down=w_sh_down)
    n_bc, t_c, _ = x_prompt.shape
    n_bl, t_l, _ = x_sample.shape
    p_len = cache_mla_ckv.shape[2]
    tabs = _rope_tables(t_l)
    n_cond = 8
    cond = jnp.concatenate([c_ctx[None, :], c, jnp.zeros((n_cond - 1 - n_bl, D), F32)], axis=0)
    blk_c, blk_l = t_c // TM, t_l // TM
    assert t_l % TM_MOE_PRE == 0 and (t_c * n_bc) % TM_MOE_PRE == 0

    yp = x_prompt.reshape(n_bc * t_c, D)
    ys = x_sample.reshape(n_bl * t_l, D)
    caches = []
    lw = _prep_weights(params)
    for l in range(DEPTH):
        mod3 = _mod_call(l, cond, w_mod, b_mod).reshape(n_cond, 1, 6 * D)
        yp, cache = _mixers(False, l, yp, mod3, lambda i: 0, lw, n_bc, t_c)
        yp = _moe_call(l, yp, mod3, lambda i: 0, lw)
        caches.append(cache)
        past_kvm = _pastkv_call(l, cache_mla_ckv[:, l].reshape(n_bl * p_len, -1),
                                jnp.tile(cache_mla_kpe[:, l].reshape(n_bl * p_len, -1), (1, 4)),
                                lw["w_ukv"]).reshape(n_bl, p_len, -1)
        past_gkv = jnp.concatenate([cache_gqa_k[:, l].reshape(n_bl, p_len, -1),
                                    cache_gqa_v[:, l].reshape(n_bl, p_len, -1),
                                    jnp.ones((n_bl, p_len, LANES), F32)], axis=-1).astype(BF16)
        past_dv = jnp.concatenate([cache_diff_v[:, l], jnp.ones_like(cache_diff_v[:, l])], axis=-1)
        past_dkv = jnp.concatenate([cache_diff_k[:, l].reshape(n_bl, p_len, -1),
                                    past_dv.reshape(n_bl, p_len, -1)], axis=-1).astype(BF16)
        s0 = state_ret[:, l].reshape(n_bl, 2, RET_HEADS // 2, 2 * RET_DK, RET_DV)
        ys, _ = _mixers(True, l, ys, mod3, lambda i: 1 + i // blk_l, lw, n_bl, t_l, tabs=tabs,
                        past=(past_kvm, past_gkv, past_dkv), s0=s0)
        ys = _moe_call(l, ys, mod3, lambda i: 1 + i // (t_l // TM_MOE_PRE), lw)

    def stack(k, shape):
        return jnp.stack([caches[l][k].reshape((n_bc, t_c) + shape) for l in range(DEPTH)], axis=1)

    new_ret = jnp.stack([caches[l][6] for l in range(DEPTH)], axis=1)
    return (yp.reshape(n_bc, t_c, D), ys.reshape(n_bl, t_l, D),
            stack(0, (MLA_KV_LORA,)), stack(1, (MLA_ROPE,)),
            stack(2, (GQA_KV_HEADS, GQA_HD)), stack(3, (GQA_KV_HEADS, GQA_HD)),
            stack(4, (DIFF_HEADS, 2, DIFF_D)), stack(5, (DIFF_HEADS, DIFF_DV)), new_ret)
```

```python
import functools
import math

import numpy as np
import jax
import jax.numpy as jnp
from jax import lax
from jax.experimental import pallas as pl
from jax.experimental.pallas import tpu as pltpu
from jax.experimental.pallas import tpu_sc as plsc

F32 = jnp.float32
BF16 = jnp.bfloat16

D = 1024
DEPTH = 2
GRID_W = 64
ROPE_BASE = 10000.0
EPS = 1e-6

MLA_HEADS, MLA_NOPE, MLA_ROPE, MLA_V = 8, 64, 32, 64
MLA_Q_LORA, MLA_KV_LORA = 384, 256
GQA_HEADS, GQA_KV_HEADS, GQA_HD = 8, 2, 64
RET_HEADS, RET_DK, RET_DV = 4, 64, 128
DIFF_HEADS, DIFF_D, DIFF_DV = 4, 64, 128
N_BRANCH, BRANCH_W = 4, 512
N_EXPERTS, TOP_K, N_GROUPS, TOPK_GROUPS = 32, 4, 4, 2
EXPERT_FF, SHARED_FF = 256, 256
ROUTE_SCALE = 2.5
GROUP_SIZE = N_EXPERTS // N_GROUPS

LANES = 128
HALF_LANES = 64
VMEM_LIMIT = 56 * 1024 * 1024

C_CQ, C_CKV, C_KPE, C_GQ, C_GK, C_GV = 0, 384, 640, 768, 1280, 1408
C_DQ, C_DK, C_DV, C_RQ, C_RK, C_RV, C_RG, C_GL, C_END = (
    1536, 2048, 2560, 3072, 3328, 3584, 4096, 4608, 8704)
O_CQ, O_CKV, O_KPE, O_GQ, O_GK, O_GV = 0, 384, 640, 672, 1184, 1312
O_RQ, O_RK, O_RV, O_RG, O_DQ, O_DK, O_DV, O_GL, O_END = (
    1440, 1696, 1952, 2464, 2976, 3488, 4000, 4512, 8608)
GQA_ORDER = (0, 4, 1, 5, 2, 6, 3, 7)

KVM_W = 8 * 256
GKV_W = 3 * LANES
DKV_W = 512 + 4 * 256
LOG2E = 1.4426950408889634
TM = 256
TQ = 256
TM_MOE_PRE = 512
TMX = 256
W_SLOTS = 4


def _cparams(sem):
    return pltpu.CompilerParams(dimension_semantics=sem, vmem_limit_bytes=VMEM_LIMIT)


def _const_spec(shape):
    nd = len(shape)
    return pl.BlockSpec(shape, lambda *_: (0,) * nd)


def _layer_spec(l, shape):
    nd = len(shape)
    return pl.BlockSpec((None,) + tuple(shape), lambda *_: (l,) + (0,) * nd)


def _rms(x, g):
    return x * lax.rsqrt(jnp.mean(x * x, axis=-1, keepdims=True) + EPS) * g


def _dot(a, b):
    return jnp.dot(a, b, preferred_element_type=F32)


def _dot_nt(a, b):
    return lax.dot_general(a, b, (((1,), (1,)), ((), ())), preferred_element_type=F32)


def _silu(x):
    return x * jax.nn.sigmoid(x)


def _lane_iota(shape):
    return lax.broadcasted_iota(jnp.int32, shape, len(shape) - 1)


def _seg_meansq(x, bd_ref, width):
    sq = x * x
    hi = sq.astype(BF16)
    lo = (sq - hi.astype(F32)).astype(BF16)
    bd = bd_ref[0:width, 0:width]
    return (_dot(hi, bd) + _dot(lo, bd)) * (1.0 / GQA_HD)


def _rope(x, cos, sin_signed, half):
    width = x.shape[-1]
    first = (_lane_iota(x.shape) % (2 * half)) < half
    partner = jnp.where(first, pltpu.roll(x, width - half, 1), pltpu.roll(x, half, 1))
    return x * cos + partner * sin_signed


def _tile_lanes(t, reps):
    return t if reps == 1 else jnp.concatenate([t] * reps, axis=1)


def _store_kvm(kvm_ref, kv, kpe_b):
    ones = jnp.ones(kpe_b.shape, BF16)
    for p in range(4):
        kvm_ref[:, p * 256:p * 256 + LANES] = kv[:, p * LANES:(p + 1) * LANES].astype(BF16)
        kvm_ref[:, p * 256 + LANES:(p + 1) * 256] = kpe_b
        kvm_ref[:, 1024 + p * 256:1024 + p * 256 + LANES] = (
            kv[:, 512 + p * LANES:512 + (p + 1) * LANES].astype(BF16))
        kvm_ref[:, 1024 + p * 256 + LANES:1024 + (p + 1) * 256] = ones


def _mod_kernel(c_ref, w_ref, b_ref, o_ref):
    a = _silu(c_ref[...]).astype(BF16)
    o_ref[...] = _dot(a, w_ref[...].astype(BF16)) + b_ref[...]


def _mod_call(l, cond, w_mod, b_mod):
    n_l, _, n = w_mod.shape
    tn = 1536
    return pl.pallas_call(
        _mod_kernel,
        grid=(n // tn,),
        in_specs=[_const_spec(cond.shape),
                  pl.BlockSpec((None, D, tn), lambda j: (l, 0, j)),
                  pl.BlockSpec((None, 1, tn), lambda j: (l, 0, j))],
        out_specs=pl.BlockSpec((cond.shape[0], tn), lambda j: (0, j)),
        out_shape=jax.ShapeDtypeStruct((cond.shape[0], n), F32),
        compiler_params=_cparams(("arbitrary",)),
        name="mod",
    )(cond, w_mod, b_mod.reshape(n_l, 1, n))


def _inprep_kernel(latent, *refs):
    (x_ref, mod_ref, gpre_ref, win_ref, gmq_ref, wuq_ref, gmkv_ref, wukv_ref,
     ggq_ref, ggk_ref, bd_ref) = refs[:11]
    refs = refs[11:]
    if latent:
        cos64_ref, sin64_ref, cospe_ref, sinpe_ref = refs[:4]
        refs = refs[4:]
    (qm_ref, kvm_ref, gqo_ref, gkv_ref, dqo_ref, dkv_ref, ret_ref, rg_ref, gate_ref) = refs[:9]
    refs = refs[9:]
    if not latent:
        ckv_o, kpe_o, gk_o, gv_o, dk_o, dv_o = refs

    x = x_ref[...]
    mod = mod_ref[...]
    sh1 = mod[:, 0:D]
    sc1 = mod[:, D:2 * D]
    hb = (_rms(x, gpre_ref[...]) * (1.0 + sc1) + sh1).astype(BF16)

    def z(a, b):
        return _dot_nt(hb, win_ref[a:b, :])

    if latent:
        cos64, sin64 = cos64_ref[...], sin64_ref[...]
        cospe, sinpe = cospe_ref[...], sinpe_ref[...]

    cqn = _rms(z(O_CQ, O_CKV), gmq_ref[...]).astype(BF16)
    q = _dot(cqn, wuq_ref[...]) * ((MLA_NOPE + MLA_ROPE) ** -0.5 * LOG2E)
    q_nope, q_pe = q[:, 0:512], q[:, 512:768]
    if latent:
        q_pe = _rope(q_pe, _tile_lanes(cospe, 2), _tile_lanes(sinpe, 2), MLA_ROPE // 4)
    qm_ref[:, 0:512] = q_nope.astype(BF16)
    qm_ref[:, 512:768] = q_pe.astype(BF16)

    ckvn = _rms(z(O_CKV, O_KPE), gmkv_ref[...])
    kv = _dot(ckvn.astype(BF16), wukv_ref[...])
    kpe4 = _dot_nt(hb, jnp.concatenate([win_ref[O_KPE:O_GQ, :]] * 4, axis=0))
    if latent:
        kpe4 = _rope(kpe4, cospe, sinpe, MLA_ROPE // 4)
    else:
        ckv_o[...] = ckvn
        kpe_o[...] = kpe4[:, 0:MLA_ROPE]
    _store_kvm(kvm_ref, kv, kpe4.astype(BF16))

    gq = _dot_nt(hb, jnp.concatenate(
        [win_ref[O_GQ + h * GQA_HD:O_GQ + (h + 1) * GQA_HD, :] for h in GQA_ORDER], axis=0))
    gq = gq * lax.rsqrt(_seg_meansq(gq, bd_ref, 512) + EPS) * ggq_ref[...]
    gk = z(O_GK, O_GV)
    gk = gk * lax.rsqrt(_seg_meansq(gk, bd_ref, LANES) + EPS) * ggk_ref[...]
    gv = z(O_GV, O_RQ)
    if latent:
        gq = _rope(gq, _tile_lanes(cos64, 4), _tile_lanes(sin64, 4), GQA_HD // 4)
        gk = _rope(gk, cos64, sin64, GQA_HD // 4)
    else:
        gk_o[...] = gk
        gv_o[...] = gv
    gqo_ref[...] = (gq * (GQA_HD ** -0.5 * LOG2E)).astype(BF16)
    gkv_ref[:, 0:LANES] = gk.astype(BF16)
    gkv_ref[:, LANES:2 * LANES] = gv.astype(BF16)
    gkv_ref[:, 2 * LANES:3 * LANES] = jnp.ones(gv.shape, BF16)

    dq = z(O_DQ, O_DK)
    dk = z(O_DK, O_DV)
    dv = z(O_DV, O_GL)
    if latent:
        dq = _rope(dq, _tile_lanes(cos64, 4), _tile_lanes(sin64, 4), DIFF_D // 4)
        dk = _rope(dk, _tile_lanes(cos64, 4), _tile_lanes(sin64, 4), DIFF_D // 4)
    else:
        dk_o[...] = dk
        dv_o[...] = dv
    dqo_ref[...] = (dq * (DIFF_D ** -0.5 * LOG2E)).astype(BF16)
    dkv_ref[:, 0:512] = dk.astype(BF16)
    for h in range(DIFF_HEADS):
        dkv_ref[:, 512 + h * 256:512 + h * 256 + LANES] = dv[:, h * LANES:(h + 1) * LANES].astype(BF16)
        dkv_ref[:, 512 + h * 256 + LANES:512 + (h + 1) * 256] = jnp.ones((dv.shape[0], LANES), BF16)

    ret_ref[:, 0:256] = z(O_RQ, O_RK).astype(BF16)
    ret_ref[:, 256:512] = (z(O_RK, O_RV) * (RET_DK ** -0.5)).astype(BF16)
    ret_ref[:, 512:1024] = z(O_RV, O_RG).astype(BF16)
    rg_ref[...] = z(O_RG, O_DQ).astype(BF16)

    for n in range(N_BRANCH):
        gate_ref[:, n * D:(n + 1) * D] = jax.nn.sigmoid(
            z(O_GL + n * D, O_GL + (n + 1) * D)).astype(BF16)


def _inprep_call(latent, l, x, mod3, mod_row, lw, tabs, t_len):
    n_tok = x.shape[0]
    nblk = n_tok // TM
    blk_per_seq = t_len // TM

    def tok(w):
        return pl.BlockSpec((TM, w), lambda i: (i, 0))

    in_specs = [tok(D),
                pl.BlockSpec((None, 1, 6 * D), lambda i: (mod_row(i), 0, 0)),
                _layer_spec(l, (1, D)),
                pl.BlockSpec((None, O_END, D), lambda i: (l, 0, 0), pipeline_mode=pl.Buffered(1)),
                _layer_spec(l, (1, MLA_Q_LORA)), _layer_spec(l, (MLA_Q_LORA, 768)),
                _layer_spec(l, (1, MLA_KV_LORA)), _layer_spec(l, (MLA_KV_LORA, 1024)),
                _layer_spec(l, (1, 512)), _layer_spec(l, (1, LANES)), _const_spec((512, 512))]
    args = [x, mod3, lw["g_pre1"], lw["w_in"], lw["g_mla_q"], lw["w_uq"], lw["g_mla_kv"],
            lw["w_ukv"], lw["g_gqa_q"], lw["g_gqa_k"], lw["bd"]]
    if latent:
        tab_spec = pl.BlockSpec((TM, LANES), lambda i: (i % blk_per_seq, 0))
        in_specs += [tab_spec] * 4
        args += list(tabs)
    widths = [768, KVM_W, 512, GKV_W, 512, DKV_W, 1024, 512, 4 * D]
    out_specs = [tok(w) for w in widths]
    out_shape = [jax.ShapeDtypeStruct((n_tok, w), BF16) for w in widths]
    if not latent:
        cw = [MLA_KV_LORA, MLA_ROPE, 128, 128, 512, 512]
        out_specs += [tok(w) for w in cw]
        out_shape += [jax.ShapeDtypeStruct((n_tok, w), F32) for w in cw]
    return pl.pallas_call(
        functools.partial(_inprep_kernel, latent),
        grid=(nblk,),
        in_specs=in_specs, out_specs=out_specs, out_shape=out_shape,
        compiler_params=_cparams(("arbitrary",)),
        name="inprep_lat" if latent else "inprep_ctx",
    )(*args)


def _pastkv_kernel(ckv_ref, kpe_ref, wukv_ref, o_ref):
    kv = _dot(ckv_ref[...].astype(BF16), wukv_ref[...])
    _store_kvm(o_ref, kv, kpe_ref[...].astype(BF16))


def _pastkv_call(l, ckv, kpe4, w_ukv):
    n = ckv.shape[0]
    return pl.pallas_call(
        _pastkv_kernel,
        grid=(n // TM,),
        in_specs=[pl.BlockSpec((TM, MLA_KV_LORA), lambda i: (i, 0)),
                  pl.BlockSpec((TM, LANES), lambda i: (i, 0)),
                  _layer_spec(l, (MLA_KV_LORA, 1024))],
        out_specs=pl.BlockSpec((TM, KVM_W), lambda i: (i, 0)),
        out_shape=jax.ShapeDtypeStruct((n, KVM_W), BF16),
        compiler_params=_cparams(("arbitrary",)),
        name="pastkv",
    )(ckv, kpe4, w_ukv)


def _softmax_pv(s, v_ones):
    m = jnp.max(s, axis=-1, keepdims=True)
    p = jnp.exp2(s - m).astype(BF16)
    o = _dot(p, v_ones)
    return o[:, 0:LANES] / o[:, LANES:2 * LANES]


def _attn_kernel(lam_init, n_past, qm_ref, kvm_ref, gq_ref, gkv_ref, dq_ref, dkv_ref, *refs):
    if n_past:
        past_refs, refs = refs[:3], refs[3:]
        lam_ref, gdiff_ref, o_ref = refs[:3]
        joined = refs[3:]

        @pl.when(pl.program_id(1) == 0)
        def _():
            for dst, past, new in zip(joined, past_refs, (kvm_ref, gkv_ref, dkv_ref)):
                dst[0:n_past, :] = past[...]
                dst[n_past:, :] = new[...]

        kvm_ref, gkv_ref, dkv_ref = joined
    else:
        lam_ref, gdiff_ref, o_ref = refs
    tq = qm_ref.shape[0]
    lane = _lane_iota((tq, LANES))
    low = lane < HALF_LANES
    zero = jnp.zeros((tq, LANES), BF16)

    for p in range(MLA_HEADS // 2):
        qn = qm_ref[:, p * LANES:(p + 1) * LANES]
        g = p // 2
        qpe = qm_ref[:, 512 + g * LANES:512 + (g + 1) * LANES]
        kk = kvm_ref[:, p * 256:(p + 1) * 256]
        vv = kvm_ref[:, 1024 + p * 256:1024 + (p + 1) * 256]
        outs = []
        for half in range(2):
            h = 2 * p + half
            slot = h % 4
            in_slot = (lane >= slot * MLA_ROPE) & (lane < (slot + 1) * MLA_ROPE)
            lhs = jnp.concatenate(
                [jnp.where(low if half == 0 else ~low, qn, zero),
                 jnp.where(in_slot, qpe, zero)], axis=1)
            outs.append(_softmax_pv(_dot_nt(lhs, kk), vv))
        o_ref[:, p * LANES:(p + 1) * LANES] = jnp.where(low, outs[0], outs[1]).astype(BF16)

    kk = gkv_ref[:, 0:LANES]
    vv = gkv_ref[:, LANES:3 * LANES]
    for g in range(GQA_HEADS // 2):
        qg = gq_ref[:, g * LANES:(g + 1) * LANES]
        o_lo = _softmax_pv(_dot_nt(jnp.where(low, qg, zero), kk), vv)
        o_hi = _softmax_pv(_dot_nt(jnp.where(low, zero, qg), kk), vv)
        o_ref[:, 512 + g * LANES:512 + (g + 1) * LANES] = jnp.where(low, o_lo, o_hi).astype(BF16)

    lp = lam_ref[...]
    lam = (jnp.exp(jnp.sum(lp[0:1] * lp[1:2], axis=-1, keepdims=True))
           - jnp.exp(jnp.sum(lp[2:3] * lp[3:4], axis=-1, keepdims=True)) + lam_init)
    for h in range(DIFF_HEADS):
        qh = dq_ref[:, h * LANES:(h + 1) * LANES]
        kk = dkv_ref[:, h * LANES:(h + 1) * LANES]
        vv = dkv_ref[:, 512 + h * 256:512 + (h + 1) * 256]
        a1 =_softmax_pv(_dot_nt(jnp.where(low, qh, zero), kk), vv)
        a2 = _softmax_pv(_dot_nt(jnp.where(low, zero, qh), kk), vv)
        od = _rms(a1 - lam * a2, gdiff_ref[...]) * (1.0 - lam_init)
        o_ref[:, 1024 + h * LANES:1024 + (h + 1) * LANES] = od.astype(BF16)


def _attn_call(l, lam_init, qm, kvm, gq, gkv, dq, dkv, lam_p, g_diff, n_b, t_len, past=None):
    nq = t_len // TQ
    n_past = 0 if past is None else past[0].shape[0] // n_b

    def qspec(w):
        return pl.BlockSpec((TQ, w), lambda b, i: (b * nq + i, 0))

    def kspec(w, rows=t_len):
        return pl.BlockSpec((rows, w), lambda b, i: (b, 0))

    in_specs = [qspec(768), kspec(KVM_W), qspec(512), kspec(GKV_W), qspec(512), kspec(DKV_W)]
    args = [qm, kvm, gq, gkv, dq, dkv]
    scratch = []
    if n_past:
        in_specs += [kspec(KVM_W, n_past), kspec(GKV_W, n_past), kspec(DKV_W, n_past)]
        args += list(past)
        scratch = [pltpu.VMEM((n_past + t_len, w), BF16) for w in (KVM_W, GKV_W, DKV_W)]
    in_specs += [_layer_spec(l, (4, DIFF_D)), _layer_spec(l, (1, DIFF_DV))]
    args += [lam_p, g_diff]
    return pl.pallas_call(
        functools.partial(_attn_kernel, lam_init, n_past),
        grid=(n_b, nq),
        in_specs=in_specs,
        out_specs=qspec(3 * BRANCH_W),
        out_shape=jax.ShapeDtypeStruct((n_b * t_len, 3 * BRANCH_W), BF16),
        scratch_shapes=scratch,
        compiler_params=_cparams(("arbitrary", "arbitrary")),
        name="attn",
    )(*args)


def _log_sigmoid(x):
    return jnp.minimum(x, 0.0) - jnp.log(1.0 + jnp.exp(-jnp.abs(x)))


def _log_gamma(dec_ref, l, d, h):
    return _log_sigmoid(jnp.full((1, 1), dec_ref[l, d, h], F32))


def _ret_kernel(latent, l, t_len, dec_ref, q_ref, k_ref, v_ref, rg_ref, gret_ref, *refs):
    if latent:
        s0_ref, o_ref = refs
    else:
        o_ref, st_ref = refs
    tq = q_ref.shape[0]
    t0 = pl.program_id(1) * tq
    lane = _lane_iota((tq, LANES))
    low = lane < HALF_LANES
    zero = jnp.zeros((tq, LANES), BF16)
    t_idx = (t0 + lax.broadcasted_iota(jnp.int32, (tq, t_len), 0)).astype(F32)
    s_idx = lax.broadcasted_iota(jnp.int32, (tq, t_len), 1).astype(F32)
    dist = t_idx - s_idx
    past = dist >= 0
    diag = jnp.where(dist == 0, 1.0, 0.0)
    t_col = (t0 + lax.broadcasted_iota(jnp.int32, (tq, 1), 0)).astype(F32)

    def lg(d, h):
        return _log_gamma(dec_ref, l, d, h)

    for h in range(RET_HEADS):
        p, half = h // 2, h % 2
        qp = q_ref[:, p * LANES:(p + 1) * LANES]
        qm = jnp.where(low if half == 0 else ~low, qp, zero)
        kp = k_ref[:, p * LANES:(p + 1) * LANES]
        vh = v_ref[:, h * LANES:(h + 1) * LANES]
        lgf, lgb = lg(0, h), lg(1, h)
        dmask = jnp.exp(jnp.where(past, lgf, -lgb) * dist) + diag
        o = _dot((_dot_nt(qm, kp) * dmask).astype(BF16), vh)
        if latent:
            sf =s0_ref[0, p].astype(BF16)
            sb = s0_ref[1, p].astype(BF16)
            o = o + _dot(qm, sf) * jnp.exp(lgf * (t_col + 1.0))
            o = o + _dot(qm, sb) * jnp.exp(lgb * (float(t_len) - t_col))
        mu = jnp.mean(o, axis=-1, keepdims=True)
        oc = o - mu
        y = oc * lax.rsqrt(jnp.mean(oc * oc, axis=-1, keepdims=True) + EPS)
        y = y * gret_ref[:, h * LANES:(h + 1) * LANES]
        rg = rg_ref[:, h * LANES:(h + 1) * LANES].astype(F32)
        o_ref[:, h * LANES:(h + 1) * LANES] = (y * _silu(rg)).astype(BF16)

    if not latent:
        s_col = lax.broadcasted_iota(jnp.int32, (t_len, 1), 0).astype(F32)
        lane_t = _lane_iota((1, LANES)) < HALF_LANES
        for p in range(RET_HEADS // 2):
            kp = k_ref[:, p * LANES:(p + 1) * LANES].astype(F32)
            for d in range(2):
                lg_lane = jnp.where(lane_t, lg(d, 2 * p), lg(d, 2 * p + 1))
                expo = (float(t_len) - 1.0 - s_col) if d == 0 else s_col
                kdec_t = jnp.transpose(kp * jnp.exp(lg_lane * expo)).astype(BF16)
                for half in range(2):
                    h = 2 * p + half
                    st = _dot(kdec_t, v_ref[:, h * LANES:(h + 1) * LANES])
                    st_ref[d, h] = st[half * RET_DK:(half + 1) * RET_DK, :]


def _ret_call(latent, l, dec, ret, rg, g_ret, s0, n_b, t_len):
    nq = t_len // TQ
    assert latent or nq == 1
    in_specs = [pl.BlockSpec(memory_space=pltpu.SMEM),
                pl.BlockSpec((TQ, 256), lambda b, i: (b * nq + i, 0)),
                pl.BlockSpec((t_len, 256), lambda b, i: (b, 1)),
                pl.BlockSpec((t_len, 512), lambda b, i: (b, 1)),
                pl.BlockSpec((TQ, 512), lambda b, i: (b * nq + i, 0)),
                _layer_spec(l, (1, 512))]
    args = [dec, ret, ret, ret, rg, g_ret]
    out_specs = [pl.BlockSpec((TQ, 512), lambda b, i: (b * nq + i, 0))]
    out_shape = [jax.ShapeDtypeStruct((n_b * t_len, 512), BF16)]
    if latent:
        in_specs.append(pl.BlockSpec((None, 2, 2, LANES, LANES), lambda b, i: (b, 0, 0, 0, 0)))
        args.append(s0)
    else:
        out_specs.append(pl.BlockSpec((None, 2, RET_HEADS, RET_DK, RET_DV),
                                      lambda b, i: (b, 0, 0, 0, 0)))
        out_shape.append(jax.ShapeDtypeStruct((n_b, 2, RET_HEADS, RET_DK, RET_DV), F32))
    return pl.pallas_call(
        functools.partial(_ret_kernel, latent, l, t_len),
        grid=(n_b, nq),
        in_specs=in_specs, out_specs=out_specs, out_shape=out_shape,
        compiler_params=_cparams(("arbitrary", "arbitrary")),
        name="ret_lat" if latent else "ret_ctx",
    )(*args)


def _merge_kernel(x_ref, mod_ref, br_ref, or_ref, gate_ref, wbr_ref, wout_ref, gpost_ref, o_ref):
    merged = None
    for n in range(N_BRANCH):
        if n < 2:
            b = br_ref[:, n * BRANCH_W:(n + 1) * BRANCH_W]
        elif n == 2:
            b = or_ref[...]
        else:
            b = br_ref[:, 2 * BRANCH_W:3 * BRANCH_W]
        t = gate_ref[:, n * D:(n + 1) * D].astype(F32) * _dot(b, wbr_ref[n])
        merged = t if merged is None else merged + t
    out = _dot(merged.astype(BF16), wout_ref[...])
    g1 = mod_ref[...][:, 2 * D:3 * D]
    o_ref[...] = x_ref[...] + g1 * _rms(out, gpost_ref[...])


def _merge_call(l, x, mod3, mod_row, br, o_r, gates, lw):
    n_tok = x.shape[0]

    def tok(w):
        return pl.BlockSpec((TM, w), lambda i: (i, 0))

    return pl.pallas_call(
        _merge_kernel,
        grid=(n_tok // TM,),
        in_specs=[tok(D), pl.BlockSpec((None, 1, 6 * D), lambda i: (mod_row(i), 0, 0)),
                  tok(3 * BRANCH_W), tok(BRANCH_W), tok(4 * D),
                  _layer_spec(l, (N_BRANCH, BRANCH_W, D)), _layer_spec(l, (D, D)),
                  _layer_spec(l, (1, D))],
        out_specs=tok(D),
        out_shape=jax.ShapeDtypeStruct((n_tok, D), F32),
        compiler_params=_cparams(("arbitrary",)),
        name="merge",
    )(x, mod3, br, o_r, gates, lw["w_br"], lw["w_out"], lw["g_post1"])


def _route(logits_t, bias):
    n = logits_t.shape[1]
    scores = jax.nn.sigmoid(logits_t)
    sel = scores + bias
    neg = -jnp.inf
    sub = lax.broadcasted_iota(jnp.int32, (GROUP_SIZE, n), 0)
    grp = []
    for g in range(N_GROUPS):
        blk = sel[g * GROUP_SIZE:(g + 1) * GROUP_SIZE]
        m1 = jnp.max(blk, axis=0, keepdims=True)
        i1 = jnp.min(jnp.where(blk == m1, sub, GROUP_SIZE), axis=0, keepdims=True)
        m2 = jnp.max(jnp.where(sub == i1, neg, blk), axis=0, keepdims=True)
        grp.append(m1 + m2)
    parts = []
    for g in range(N_GROUPS):
        beaten = jnp.zeros((1, n), jnp.int32)
        for o in range(N_GROUPS):
            if o == g:
                continue
            wins = (grp[o] > grp[g]) | (grp[o] == grp[g]) if o < g else (grp[o] > grp[g])
            beaten = beaten + wins.astype(jnp.int32)
        keep = beaten < TOPK_GROUPS
        parts.append(jnp.where(keep, sel[g * GROUP_SIZE:(g + 1) * GROUP_SIZE], neg))
    cur = jnp.concatenate(parts, axis=0)
    eidx = lax.broadcasted_iota(jnp.int32, (N_EXPERTS, n), 0)
    hits, ids, ws = [], [], []
    for _ in range(TOP_K):
        m = jnp.max(cur, axis=0, keepdims=True)
        i = jnp.min(jnp.where(cur == m, eidx, N_EXPERTS), axis=0, keepdims=True)
        hit = eidx == i
        hits.append(hit)
        ids.append(i)
        ws.append(jnp.sum(jnp.where(hit, scores, 0.0), axis=0, keepdims=True))
        cur = jnp.where(hit, neg, cur)
    wsum = ws[0] + ws[1] + ws[2] + ws[3]
    return hits, ids, [w / wsum * ROUTE_SCALE for w in ws]


U32 = jnp.uint32
HIGH16 = np.uint32(0xFFFF0000)


def _bf16_bits(v):
    return lax.bitcast_convert_type(v.astype(BF16).astype(F32), U32)


def _pack_rows(v):
    return (_bf16_bits(v[:, 0:D // 2]) >> 16) | _bf16_bits(v[:, D // 2:D])


def _unpack_rows(p):
    lo = lax.bitcast_convert_type(p << 16, F32)
    hi = lax.bitcast_convert_type(p & HIGH16, F32)
    return jnp.concatenate([lo, hi], axis=1)


def _moe_pre_kernel(x_ref, mod_ref, gpre_ref, wr_ref, br_ref, tri_ref,
                    hp_ref, eidx_ref, rank_ref, comb_ref, cnt_ref, run_ref):
    tm = x_ref.shape[0]

    @pl.when(pl.program_id(0) == 0)
    def _():
        run_ref[...] = jnp.zeros_like(run_ref)

    mod = mod_ref[...]
    sh2, sc2 = mod[:, 3 * D:4 * D], mod[:, 4 * D:5 * D]
    h = _rms(x_ref[...], gpre_ref[...]) * (1.0 + sc2) + sh2
    hp_ref[...] = _pack_rows(h)
    hb = h.astype(BF16)
    h_lo = (h - hb.astype(F32)).astype(BF16)
    wr = wr_ref[...]
    wr_hi = wr.astype(BF16)
    wr_lo = (wr - wr_hi.astype(F32)).astype(BF16)
    logits_t = _dot_nt(wr_hi, hb) + _dot_nt(wr_hi, h_lo) + _dot_nt(wr_lo, hb)
    hits, ids, ws = _route(logits_t, br_ref[...])

    picked = jnp.zeros((N_EXPERTS, tm), F32)
    for hit in hits:
        picked = jnp.where(hit, 1.0, picked)
    before = _dot(picked.astype(BF16), tri_ref[...]) + run_ref[:, 0:1]
    sub8 = lax.broadcasted_iota(jnp.int32, (8, tm), 0)
    comb8 = jnp.zeros((8, tm), F32)
    for k in range(TOP_K):
        rank = jnp.sum(jnp.where(hits[k], before, 0.0), axis=0, keepdims=True)
        eidx_ref[k:k + 1, :] = ids[k]
        rank_ref[k:k + 1, :] = rank.astype(jnp.int32)
        comb8 = jnp.where(sub8 == k, ws[k], comb8)
    comb_ref[...] = jnp.transpose(
        jnp.concatenate([comb8, jnp.zeros((LANES - 8, tm), F32)], axis=0))
    run_ref[...] = run_ref[...] + jnp.sum(picked, axis=1, keepdims=True)
    cnt_ref[...] = run_ref[...]


def _moe_pre_call(l, x, mod3, mod_row, lw):
    n_tok = x.shape[0]
    tm = TM_MOE_PRE
    tri = np.arange(tm)
    tri = jnp.asarray(tri[:, None] < tri[None, :], BF16)
    row4 = pl.BlockSpec((TOP_K, tm), lambda i: (0, i))
    return pl.pallas_call(
        _moe_pre_kernel,
        grid=(n_tok // tm,),
        in_specs=[pl.BlockSpec((tm, D), lambda i: (i, 0)),
                  pl.BlockSpec((None, 1, 6 * D), lambda i: (mod_row(i), 0, 0)),
                  _layer_spec(l, (1, D)), _layer_spec(l, (N_EXPERTS, D)),
                  _layer_spec(l, (N_EXPERTS, 1)), _const_spec((tm, tm))],
        out_specs=[pl.BlockSpec((tm, D // 2), lambda i: (i, 0)), row4, row4,
                   pl.BlockSpec((tm, LANES), lambda i: (i, 0)),
                   _const_spec((N_EXPERTS, LANES))],
        out_shape=[jax.ShapeDtypeStruct((n_tok, D // 2), U32),
                   jax.ShapeDtypeStruct((TOP_K, n_tok), jnp.int32),
                   jax.ShapeDtypeStruct((TOP_K, n_tok), jnp.int32),
                   jax.ShapeDtypeStruct((n_tok, LANES), F32),
                   jax.ShapeDtypeStruct((N_EXPERTS, LANES), F32)],
        scratch_shapes=[pltpu.VMEM((N_EXPERTS, LANES), F32)],
        compiler_params=_cparams(("arbitrary",)),
        name="moe_pre",
    )(x, mod3, lw["g_pre2"], lw["w_router_t"], lw["b_router"], tri)


def _moe_plan_kernel(eidx_ref, rank_ref, cnt_ref, dest_ref, te_ref, tv_ref, tn_ref):
    tm = eidx_ref.shape[1]
    cnt = cnt_ref[...]
    padded = jnp.ceil(cnt * (1.0 / TMX)) * TMX
    row = lax.broadcasted_iota(jnp.int32, cnt.shape, 0)
    incl = padded
    shift = 1
    while shift < N_EXPERTS:
        incl = incl + jnp.where(row >= shift, pltpu.roll(incl, shift, 0), 0.0)
        shift *= 2
    start = (incl - padded)[:, 0:1]
    end = incl[:, 0:1]
    erow = lax.broadcasted_iota(jnp.int32, (N_EXPERTS, tm), 0)
    for k in range(TOP_K):
        mine = erow == eidx_ref[k:k + 1, :]
        base = jnp.sum(jnp.where(mine, start, 0.0), axis=0, keepdims=True)
        dest_ref[k:k + 1, :] = rank_ref[k:k + 1, :] + base.astype(jnp.int32)

    @pl.when(pl.program_id(0) == 0)
    def _():
        tile0 = (_lane_iota((1, LANES)) * TMX).astype(F32)
        owner = jnp.sum(jnp.where(end <= tile0, 1.0, 0.0), axis=0, keepdims=True)
        owner = jnp.minimum(owner, N_EXPERTS - 1.0)
        erow_t = lax.broadcasted_iota(jnp.int32, (N_EXPERTS, LANES), 0).astype(F32)
        left = jnp.sum(jnp.where(erow_t == owner, cnt[:, 0:1] - (tile0 - start), 0.0),
                       axis=0, keepdims=True)
        te_ref[...] = owner.astype(jnp.int32)
        tv_ref[...] = jnp.clip(left, 0.0, float(TMX)).astype(jnp.int32)
        tn_ref[...] = jnp.full(tn_ref.shape, N_EXPERTS, jnp.int32)
        nxt = owner
        for k in range(W_SLOTS - 1):
            later = (erow_t > nxt) & (cnt[:, 0:1] > 0.0)
            nxt = jnp.min(jnp.where(later, erow_t, float(N_EXPERTS)), axis=0, keepdims=True)
            tn_ref[k:k + 1, :] = nxt.astype(jnp.int32)


def _moe_plan_call(eidx, rank, cnt):
    n_tok = eidx.shape[1]
    tm = TM_MOE_PRE
    row4 = pl.BlockSpec((TOP_K, tm), lambda i: (0, i))
    tiles = jax.ShapeDtypeStruct((1, LANES), jnp.int32)
    return pl.pallas_call(
        _moe_plan_kernel,
        grid=(n_tok // tm,),
        in_specs=[row4, row4, _const_spec((N_EXPERTS, LANES))],
        out_specs=[row4, _const_spec((1, LANES)), _const_spec((1, LANES)), _const_spec((8, LANES))],
        out_shape=[jax.ShapeDtypeStruct((TOP_K, n_tok), jnp.int32), tiles, tiles,
                   jax.ShapeDtypeStruct((8, LANES), jnp.int32)],
        compiler_params=_cparams(("arbitrary",)),
        name="moe_plan",
    )(eidx, rank, cnt)


def _experts_kernel(l, te_ref, tv_ref, tn_ref, xs_ref, wgu_hbm, wdn_hbm, ys_ref,
                    wgu_f, wdn_f, wgu_b, wdn_b, sem, group_ref):
    j = pl.program_id(0)
    valid = tv_ref[j]
    expert = te_ref[j]

    def fetch(e, slot):
        return (pltpu.make_async_copy(wgu_hbm.at[l, e], wgu_f.at[slot], sem.at[slot, 0]),
                pltpu.make_async_copy(wdn_hbm.at[l, e], wdn_f.at[slot], sem.at[slot, 1]))

    def start_if_any(e, slot):
        @pl.when(e < N_EXPERTS)
        def _():
            for cp in fetch(e, slot):
                cp.start()

    @pl.when(j == 0)
    def _():
        group_ref[0] = 0
        start_if_any(expert, 0)
        for k in range(W_SLOTS - 2):
            start_if_any(tn_ref[k, 0], k + 1)

    first_tile = (j == 0) | (expert != te_ref[jnp.maximum(j - 1, 0)])

    @pl.when(first_tile & (valid > 0))
    def _():
        group = group_ref[0]
        slot = lax.rem(group, W_SLOTS)
        for cp in fetch(expert, slot):
            cp.wait()
        wgu_b[...] = wgu_f[slot].astype(BF16)
        wdn_b[...] = wdn_f[slot].astype(BF16)
        start_if_any(tn_ref[W_SLOTS - 2, j], lax.rem(group + W_SLOTS - 1, W_SLOTS))
        group_ref[0] = group + 1

    @pl.when(valid > 0)
    def _():
        rows = lax.broadcasted_iota(jnp.int32, (TMX, D), 0)
        x = jnp.where(rows < valid, _unpack_rows(xs_ref[...]), 0.0).astype(BF16)
        gu = _dot(x, wgu_b[...])
        a = _silu(gu[:, 0:EXPERT_FF]) * gu[:, EXPERT_FF:2 * EXPERT_FF]
        ys_ref[...] = _pack_rows(_dot(a.astype(BF16), wdn_b[...]))

    @pl.when(valid <= 0)
    def _():
        ys_ref[...] = jnp.zeros_like(ys_ref)


def _experts_call(l, xs, te, tv, tn, w_gu, w_dn):
    n_tiles = xs.shape[0] // TMX
    grid_spec = pltpu.PrefetchScalarGridSpec(
        num_scalar_prefetch=3,
        grid=(n_tiles,),
        in_specs=[pl.BlockSpec((TMX, D // 2), lambda j, *_: (j, 0)),
                  pl.BlockSpec(memory_space=pl.ANY), pl.BlockSpec(memory_space=pl.ANY)],
        out_specs=pl.BlockSpec((TMX, D // 2), lambda j, *_: (j, 0)),
        scratch_shapes=[pltpu.VMEM((W_SLOTS, D, 2 * EXPERT_FF), F32),
                        pltpu.VMEM((W_SLOTS, EXPERT_FF, D), F32),
                        pltpu.VMEM((D, 2 * EXPERT_FF), BF16), pltpu.VMEM((EXPERT_FF, D), BF16),
                        pltpu.SemaphoreType.DMA((W_SLOTS, 2)), pltpu.SMEM((1,), jnp.int32)])
    return pl.pallas_call(
        functools.partial(_experts_kernel, l),
        grid_spec=grid_spec,
        out_shape=jax.ShapeDtypeStruct(xs.shape, U32),
        compiler_params=_cparams(("arbitrary",)),
        name="moe_experts",
    )(te, tv, tn, xs, w_gu, w_dn)


def _moe_post_kernel(x_ref, mod_ref, hp_ref, yg_ref, comb_ref, wsgu_ref, wsdn_ref, gpost_ref,
                     o_ref):
    hb = _unpack_rows(hp_ref[...]).astype(BF16)
    sgu = _dot(hb, wsgu_ref[...])
    sa = _silu(sgu[:, 0:SHARED_FF]) * sgu[:, SHARED_FF:2 * SHARED_FF]
    acc = _dot(sa.astype(BF16), wsdn_ref[...])
    comb = comb_ref[...]
    for k in range(TOP_K):
        acc = acc + comb[:, k:k + 1] * _unpack_rows(yg_ref[k])
    g2 = mod_ref[...][:, 5 * D:6 * D]
    o_ref[...] = x_ref[...] + g2 * _rms(acc, gpost_ref[...])


def _moe_post_call(l, x, mod3, mod_row, hp, yg, comb, lw):
    n_tok = x.shape[0]
    tm = TM_MOE_PRE
    return pl.pallas_call(
        _moe_post_kernel,
        grid=(n_tok // tm,),
        in_specs=[pl.BlockSpec((tm, D), lambda i: (i, 0)),
                  pl.BlockSpec((None, 1, 6 * D), lambda i: (mod_row(i), 0, 0)),
                  pl.BlockSpec((tm, D // 2), lambda i: (i, 0)),
                  pl.BlockSpec((TOP_K, tm, D // 2), lambda i: (0, i, 0)),
                  pl.BlockSpec((tm, LANES), lambda i: (i, 0)),
                  _layer_spec(l, (D, 2 * SHARED_FF)), _layer_spec(l, (SHARED_FF, D)),
                  _layer_spec(l, (1, D))],
        out_specs=pl.BlockSpec((tm, D), lambda i: (i, 0)),
        out_shape=jax.ShapeDtypeStruct((n_tok, D), F32),
        compiler_params=_cparams(("arbitrary",)),
        name="moe_post",
    )(x, mod3, hp, yg, comb, lw["w_sh_gu"], lw["w_sh_down"], lw["g_post2"])


def _moe_call(l, x, mod3, mod_row, lw):
    n_tok = x.shape[0]
    n_slots = -(-(TOP_K * n_tok + N_EXPERTS * (TMX - 1)) // TMX) * TMX
    assert n_slots // TMX <= LANES
    hp, eidx, rank, comb, cnt = _moe_pre_call(l, x, mod3, mod_row, lw)
    dest, te, tv, tn = _moe_plan_call(eidx, rank, cnt)
    dest = dest.reshape(TOP_K * n_tok)
    xs = _sc_scatter_rows(hp, dest, n_slots)
    ys = _experts_call(l, xs, te[0], tv[0], tn, lw["w_exp_gu"], lw["w_exp_down"])
    yg = _sc_gather_rows(ys, dest).reshape(TOP_K, n_tok, D // 2)
    return _moe_post_call(l, x, mod3, mod_row, hp, yg, comb, lw)


SC_CORES, SC_SUBCORES = 2, 16
SC_WORKERS = SC_CORES * SC_SUBCORES


def _sc_gather_rows(table, idx, chunk=64):
    n_out, width = idx.shape[0], table.shape[1]
    per_worker = n_out // SC_WORKERS
    n_chunks = per_worker // chunk
    assert per_worker * SC_WORKERS == n_out and n_chunks * chunk == per_worker
    mesh = plsc.VectorSubcoreMesh(core_axis_name="c", subcore_axis_name="s",
                                  num_cores=SC_CORES, num_subcores=SC_SUBCORES)

    @functools.partial(
        pl.kernel, mesh=mesh,
        out_type=jax.ShapeDtypeStruct((n_out, width), table.dtype),
        scratch_types=[pltpu.VMEM((chunk,), jnp.int32), pltpu.VMEM((chunk, width), table.dtype),
                       pltpu.SemaphoreType.DMA],
        name="sc_gather")
    def gather(table_hbm, idx_hbm, out_hbm, idx_v, rows_v, sem):
        base = (lax.axis_index("s") * SC_CORES + lax.axis_index("c")) * per_worker

        @pl.loop(0, n_chunks)
        def _(j):
            off = base + j * chunk
            pltpu.sync_copy(idx_hbm.at[pl.ds(off, chunk)], idx_v)
            pltpu.async_copy(table_hbm.at[idx_v], rows_v, sem).wait()
            pltpu.sync_copy(rows_v, out_hbm.at[pl.ds(off, chunk)])

    return gather(table, idx)


def _sc_scatter_rows(rows, dest, n_slots, chunk=64):
    n_tok, width = rows.shape
    per_worker = n_tok // SC_WORKERS
    n_chunks = per_worker // chunk
    assert per_worker * SC_WORKERS == n_tok and n_chunks * chunk == per_worker
    mesh = plsc.VectorSubcoreMesh(core_axis_name="c", subcore_axis_name="s",
                                  num_cores=SC_CORES, num_subcores=SC_SUBCORES)

    @functools.partial(
        pl.kernel, mesh=mesh,
        out_type=jax.ShapeDtypeStruct((n_slots, width), rows.dtype),
        scratch_types=[pltpu.VMEM((chunk,), jnp.int32), pltpu.VMEM((chunk, width), rows.dtype)],
        name="sc_scatter")
    def scatter(rows_hbm, dest_hbm, out_hbm, idx_v, rows_v):
        base = (lax.axis_index("s") * SC_CORES + lax.axis_index("c")) * per_worker

        @pl.loop(0, n_chunks)
        def _(j):
            off = base + j * chunk
            pltpu.sync_copy(rows_hbm.at[pl.ds(off, chunk)], rows_v)
            for k in range(TOP_K):
                pltpu.sync_copy(dest_hbm.at[pl.ds(k * n_tok + off, chunk)], idx_v)
                pltpu.sync_copy(rows_v, out_hbm.at[idx_v])

    return scatter(rows, dest)


def _rope_tables(t_len):
    pos = np.arange(t_len)
    row, col = pos // GRID_W, pos % GRID_W

    def tab(r):
        half = r // 2
        freq = ROPE_BASE ** (-np.arange(half, dtype=np.float64) / half)
        sign = np.concatenate([-np.ones(half), np.ones(half)])
        cs, sn = [], []
        for p in (row, col):
            ang = p[:, None].astype(np.float64) * freq[None, :]
            cs.append(np.concatenate([np.cos(ang), np.cos(ang)], axis=1))
            sn.append(np.concatenate([np.sin(ang), np.sin(ang)], axis=1) * sign[None, :])
        return np.concatenate(cs, axis=1), np.concatenate(sn, axis=1)

    c64, s64 = tab(GQA_HD // 2)
    cpe, spe = tab(MLA_ROPE // 2)
    out = (np.tile(c64, (1, 2)), np.tile(s64, (1, 2)), np.tile(cpe, (1, 4)), np.tile(spe, (1, 4)))
    return tuple(jnp.asarray(a, F32) for a in out)


def _prep_weights(p):
    n_l = p["w_in"].shape[0]

    def row(name):
        return p[name].reshape(n_l, 1, -1)

    w_uq = p["w_mla_uq"].reshape(n_l, MLA_Q_LORA, MLA_HEADS, MLA_NOPE + MLA_ROPE)
    w_uq = jnp.concatenate([w_uq[..., :MLA_NOPE].reshape(n_l, MLA_Q_LORA, -1),
                            w_uq[..., MLA_NOPE:].reshape(n_l, MLA_Q_LORA, -1)], axis=-1)
    w_ukv = p["w_mla_ukv"].reshape(n_l, MLA_KV_LORA, MLA_HEADS, MLA_NOPE + MLA_V)
    w_ukv = jnp.concatenate([w_ukv[..., :MLA_NOPE].reshape(n_l, MLA_KV_LORA, -1),
                             w_ukv[..., MLA_NOPE:].reshape(n_l, MLA_KV_LORA, -1)], axis=-1)
    w_br = p["w_br"]
    w_br_gqa = w_br[:, 1].reshape(n_l, GQA_HEADS, GQA_HD, D)[:, jnp.array(GQA_ORDER)]
    w_br = jnp.concatenate([w_br[:, 0:1], w_br_gqa.reshape(n_l, 1, BRANCH_W, D), w_br[:, 2:4]], axis=1)
    blk = np.arange(512) // GQA_HD
    return {
        "g_pre1": row("g_pre1"), "g_post1": row("g_post1"),
        "g_pre2": row("g_pre2"), "g_post2": row("g_post2"),
        "w_in": jnp.swapaxes(p["w_in"], 1, 2).astype(BF16),
        "g_mla_q": row("g_mla_q"), "w_uq": w_uq.astype(BF16),
        "g_mla_kv": row("g_mla_kv"), "w_ukv": w_ukv.astype(BF16),
        "g_gqa_q": jnp.tile(p["g_gqa_q"], (1, GQA_HEADS)).reshape(n_l, 1, -1),
        "g_gqa_k": jnp.tile(p["g_gqa_k"], (1, GQA_KV_HEADS)).reshape(n_l, 1, -1),
        "bd": jnp.asarray(blk[:, None] == blk[None, :], BF16),
        "ret_decay": p["ret_decay"],
        "g_ret": row("g_ret"),
        "diff_lambda": p["diff_lambda"], "g_diff": row("g_diff"),
        "w_br": w_br.astype(BF16), "w_out": p["w_out"].astype(BF16),
        "w_router_t": jnp.swapaxes(p["w_router"], 1, 2),
        "b_router": p["b_router"].reshape(n_l, N_EXPERTS, 1),
        "w_exp_gu": p["w_exp_gu"], "w_exp_down": p["w_exp_down"],
        "w_sh_gu": p["w_sh_gu"].astype(BF16), "w_sh_down": p["w_sh_down"].astype(BF16),
    }


def _mixers(latent, l, x, mod3, mod_row, lw, n_b, t_len, tabs=None, past=None, s0=None):
    lam_init = 0.8 - 0.6 * math.exp(-0.3 * l)
    outs = _inprep_call(latent, l, x, mod3, mod_row, lw, tabs, t_len)
    qm, kvm, gq, gkv, dq, dkv, ret, rg, gates = outs[:9]
    br = _attn_call(l, lam_init, qm, kvm, gq, gkv, dq, dkv, lw["diff_lambda"], lw["g_diff"],
                    n_b, t_len, past)
    r = _ret_call(latent, l, lw["ret_decay"], ret, rg, lw["g_ret"], s0, n_b, t_len)
    y = _merge_call(l, x, mod3, mod_row, br, r[0], gates, lw)
    cache = None if latent else tuple(outs[9:]) + (r[1],)
    return y, cache


def kernel(x_prompt, x_sample, cache_mla_ckv, cache_mla_kpe, cache_gqa_k, cache_gqa_v, cache_diff_k, cache_diff_v, state_ret, c, c_ctx, w_mod, b_mod, g_pre1, g_post1, g_pre2, g_post2, w_in, g_mla_q, w_mla_uq, g_mla_kv, w_mla_ukv, g_gqa_q, g_gqa_k, ret_decay, g_ret, diff_lambda, g_diff, w_br, w_out, w_router, b_router, w_exp_gu, w_exp_down, w_sh_gu, w_sh_down):
    params = dict(w_in=w_in, g_pre1=g_pre1, g_post1=g_post1, g_pre2=g_pre2,
                  g_post2=g_post2, g_mla_q=g_mla_q, w_mla_uq=w_mla_uq,
                  g_mla_kv=g_mla_kv, w_mla_ukv=w_mla_ukv, g_gqa_q=g_gqa_q, g_gqa_k=g_gqa_k,
                  ret_decay=ret_decay, g_ret=g_ret, diff_lambda=diff_lambda, g_diff=g_diff,
                  w_br=w_br, w_out=w_out, w_router=w_router, b_router=b_router,
                  w_exp_gu=w_exp_gu, w_exp_down=w_exp_down, w_sh_gu=w_sh_gu, w_sh_down=w_sh_down)
    n_bc, t_c, _ = x_prompt.shape
    n_bl, t_l, _ = x_sample.shape
    p_len = cache_mla_ckv.shape[2]
    tabs = _rope_tables(t_l)
    n_cond = 8
    cond = jnp.concatenate([c_ctx[None, :], c, jnp.zeros((n_cond - 1 - n_bl, D), F32)], axis=0)
    blk_c, blk_l = t_c // TM, t_l // TM
    assert t_l % TM_MOE_PRE == 0 and (t_c * n_bc) % TM_MOE_PRE == 0

    yp = x_prompt.reshape(n_bc * t_c, D)
    ys = x_sample.reshape(n_bl * t_l, D)
    caches = []
    lw = _prep_weights(params)
    for l in range(DEPTH):
        mod3 = _mod_call(l, cond, w_mod, b_mod).reshape(n_cond, 1, 6 * D)
        yp, cache = _mixers(False, l, yp, mod3, lambda i: 0, lw, n_bc, t_c)
        yp = _moe_call(l, yp, mod3, lambda i: 0, lw)
        caches.append(cache)
        past_kvm = _pastkv_call(l, cache_mla_ckv[:, l].reshape(n_bl * p_len, -1),
                                jnp.tile(cache_mla_kpe[:, l].reshape(n_bl * p_len, -1), (1, 4)),
                                lw["w_ukv"])
        past_gkv = jnp.concatenate([cache_gqa_k[:, l].reshape(n_bl * p_len, -1),
                                    cache_gqa_v[:, l].reshape(n_bl * p_len, -1),
                                    jnp.ones((n_bl * p_len, LANES), F32)], axis=-1).astype(BF16)
        past_dv = jnp.concatenate([cache_diff_v[:, l], jnp.ones_like(cache_diff_v[:, l])], axis=-1)
        past_dkv = jnp.concatenate([cache_diff_k[:, l].reshape(n_bl * p_len, -1),
                                    past_dv.reshape(n_bl * p_len, -1)], axis=-1).astype(BF16)
        s0 = state_ret[:, l].reshape(n_bl, 2, RET_HEADS // 2, 2 * RET_DK, RET_DV)
        ys, _ = _mixers(True, l, ys, mod3, lambda i: 1 + i // blk_l, lw, n_bl, t_l, tabs=tabs,
                        past=(past_kvm, past_gkv, past_dkv), s0=s0)
        ys = _moe_call(l, ys, mod3, lambda i: 1 + i // (t_l // TM_MOE_PRE), lw)

    def stack(k, shape):
        return jnp.stack([caches[l][k].reshape((n_bc, t_c) + shape) for l in range(DEPTH)], axis=1)

    new_ret = jnp.stack([caches[l][6] for l in range(DEPTH)], axis=1)
    return (yp.reshape(n_bc, t_c, D), ys.reshape(n_bl, t_l, D),
            stack(0, (MLA_KV_LORA,)), stack(1, (MLA_ROPE,)),
            stack(2, (GQA_KV_HEADS, GQA_HD)), stack(3, (GQA_KV_HEADS, GQA_HD)),
            stack(4, (DIFF_HEADS, 2, DIFF_D)), stack(5, (DIFF_HEADS, DIFF_DV)), new_ret)
```

```python
import functools
import math

import numpy as np
import jax
import jax.numpy as jnp
from jax import lax
from jax.experimental import pallas as pl
from jax.experimental.pallas import tpu as pltpu
from jax.experimental.pallas import tpu_sc as plsc

F32 = jnp.float32
BF16 = jnp.bfloat16

D = 1024
DEPTH = 2
GRID_W = 64
ROPE_BASE = 10000.0
EPS = 1e-6

MLA_HEADS, MLA_NOPE, MLA_ROPE, MLA_V = 8, 64, 32, 64
MLA_Q_LORA, MLA_KV_LORA = 384, 256
GQA_HEADS, GQA_KV_HEADS, GQA_HD = 8, 2, 64
RET_HEADS, RET_DK, RET_DV = 4, 64, 128
DIFF_HEADS, DIFF_D, DIFF_DV = 4, 64, 128
N_BRANCH, BRANCH_W = 4, 512
N_EXPERTS, TOP_K, N_GROUPS, TOPK_GROUPS = 32, 4, 4, 2
EXPERT_FF, SHARED_FF = 256, 256
ROUTE_SCALE = 2.5
GROUP_SIZE = N_EXPERTS // N_GROUPS

LANES = 128
HALF_LANES = 64
VMEM_LIMIT = 56 * 1024 * 1024

C_CQ, C_CKV, C_KPE, C_GQ, C_GK, C_GV = 0, 384, 640, 768, 1280, 1408
C_DQ, C_DK, C_DV, C_RQ, C_RK, C_RV, C_RG, C_GL, C_END = (
    1536, 2048, 2560, 3072, 3328, 3584, 4096, 4608, 8704)
O_CQ, O_CKV, O_KPE, O_GQ, O_GK, O_GV = 0, 384, 640, 672, 1184, 1312
O_RQ, O_RK, O_RV, O_RG, O_DQ, O_DK, O_DV, O_GL, O_END = (
    1440, 1696, 1952, 2464, 2976, 3488, 4000, 4512, 8608)
GQA_ORDER = (0, 4, 1, 5, 2, 6, 3, 7)

KVM_W = 8 * 256
GKV_W = 3 * LANES
DKV_W = 512 + 4 * 256
LOG2E = 1.4426950408889634
TM = 256
TM_MERGE = 512
TQ = 256
TM_MOE_PRE = 512
TMX = 256
W_SLOTS = 4


def _cparams(sem):
    return pltpu.CompilerParams(dimension_semantics=sem, vmem_limit_bytes=VMEM_LIMIT)


def _const_spec(shape):
    nd = len(shape)
    return pl.BlockSpec(shape, lambda *_: (0,) * nd)


def _layer_spec(l, shape):
    nd = len(shape)
    return pl.BlockSpec((None,) + tuple(shape), lambda *_: (l,) + (0,) * nd)


def _rms(x, g):
    return x * lax.rsqrt(jnp.mean(x * x, axis=-1, keepdims=True) + EPS) * g


def _dot(a, b):
    return jnp.dot(a, b, preferred_element_type=F32)


def _dot_nt(a, b):
    return lax.dot_general(a, b, (((1,), (1,)), ((), ())), preferred_element_type=F32)


def _silu(x):
    return x * jax.nn.sigmoid(x)


def _lane_iota(shape):
    return lax.broadcasted_iota(jnp.int32, shape, len(shape) - 1)


def _seg_meansq(x, bd_ref, width):
    sq = x * x
    hi = sq.astype(BF16)
    lo = (sq - hi.astype(F32)).astype(BF16)
    bd = bd_ref[0:width, 0:width]
    return (_dot(hi, bd) + _dot(lo, bd)) * (1.0 / GQA_HD)


def _rope(x, cos, sin_signed, half):
    width = x.shape[-1]
    first = (_lane_iota(x.shape) % (2 * half)) < half
    partner = jnp.where(first, pltpu.roll(x, width - half, 1), pltpu.roll(x, half, 1))
    return x * cos + partner * sin_signed


def _tile_lanes(t, reps):
    return t if reps == 1 else jnp.concatenate([t] * reps, axis=1)


def _store_kvm(kvm_ref, kv, kpe_b):
    ones = jnp.ones(kpe_b.shape, BF16)
    for p in range(4):
        kvm_ref[:, p * 256:p * 256 + LANES] = kv[:, p * LANES:(p + 1) * LANES].astype(BF16)
        kvm_ref[:, p * 256 + LANES:(p + 1) * 256] = kpe_b
        kvm_ref[:, 1024 + p * 256:1024 + p * 256 + LANES] = (
            kv[:, 512 + p * LANES:512 + (p + 1) * LANES].astype(BF16))
        kvm_ref[:, 1024 + p * 256 + LANES:1024 + (p + 1) * 256] = ones


def _mod_kernel(c_ref, w_ref, b_ref, o_ref):
    a = _silu(c_ref[...]).astype(BF16)
    o_ref[...] = _dot(a, w_ref[...].astype(BF16)) + b_ref[...]


def _mod_call(l, cond, w_mod, b_mod):
    n_l, _, n = w_mod.shape
    tn = 1536
    return pl.pallas_call(
        _mod_kernel,
        grid=(n // tn,),
        in_specs=[_const_spec(cond.shape),
                  pl.BlockSpec((None, D, tn), lambda j: (l, 0, j)),
                  pl.BlockSpec((None, 1, tn), lambda j: (l, 0, j))],
        out_specs=pl.BlockSpec((cond.shape[0], tn), lambda j: (0, j)),
        out_shape=jax.ShapeDtypeStruct((cond.shape[0], n), F32),
        compiler_params=_cparams(("arbitrary",)),
        name="mod",
    )(cond, w_mod, b_mod.reshape(n_l, 1, n))


def _inprep_kernel(latent, *refs):
    (x_ref, mod_ref, gpre_ref, win_ref, gmq_ref, wuq_ref, gmkv_ref, wukv_ref,
     ggq_ref, ggk_ref, bd_ref) = refs[:11]
    refs = refs[11:]
    if latent:
        cos64_ref, sin64_ref, cospe_ref, sinpe_ref = refs[:4]
        refs = refs[4:]
    (qm_ref, kvm_ref, gqo_ref, gkv_ref, dqo_ref, dkv_ref, ret_ref, rg_ref, gate_ref) = refs[:9]
    refs = refs[9:]
    if not latent:
        ckv_o, kpe_o, gk_o, gv_o, dk_o, dv_o = refs

    x = x_ref[...]
    mod = mod_ref[...]
    sh1 = mod[:, 0:D]
    sc1 = mod[:, D:2 * D]
    hb = (_rms(x, gpre_ref[...]) * (1.0 + sc1) + sh1).astype(BF16)

    def z(a, b):
        return _dot_nt(hb, win_ref[a:b, :])

    if latent:
        cos64, sin64 = cos64_ref[...], sin64_ref[...]
        cospe, sinpe = cospe_ref[...], sinpe_ref[...]

    cqn = _rms(z(O_CQ, O_CKV), gmq_ref[...]).astype(BF16)
    q = _dot(cqn, wuq_ref[...]) * ((MLA_NOPE + MLA_ROPE) ** -0.5 * LOG2E)
    q_nope, q_pe = q[:, 0:512], q[:, 512:768]
    if latent:
        q_pe = _rope(q_pe, _tile_lanes(cospe, 2), _tile_lanes(sinpe, 2), MLA_ROPE // 4)
    qm_ref[:, 0:512] = q_nope.astype(BF16)
    qm_ref[:, 512:768] = q_pe.astype(BF16)

    ckvn = _rms(z(O_CKV, O_KPE), gmkv_ref[...])
    kv = _dot(ckvn.astype(BF16), wukv_ref[...])
    kpe4 = _dot_nt(hb, jnp.concatenate([win_ref[O_KPE:O_GQ, :]] * 4, axis=0))
    if latent:
        kpe4 = _rope(kpe4, cospe, sinpe, MLA_ROPE // 4)
    else:
        ckv_o[...] = ckvn
        kpe_o[...] = kpe4[:, 0:MLA_ROPE]
    _store_kvm(kvm_ref, kv, kpe4.astype(BF16))

    gq = _dot_nt(hb, jnp.concatenate(
        [win_ref[O_GQ + h * GQA_HD:O_GQ + (h + 1) * GQA_HD, :] for h in GQA_ORDER], axis=0))
    gq = gq * lax.rsqrt(_seg_meansq(gq, bd_ref, 512) + EPS) * ggq_ref[...]
    gk = z(O_GK, O_GV)
    gk = gk * lax.rsqrt(_seg_meansq(gk, bd_ref, LANES) + EPS) * ggk_ref[...]
    gv = z(O_GV, O_RQ)
    if latent:
        gq = _rope(gq, _tile_lanes(cos64, 4), _tile_lanes(sin64, 4), GQA_HD // 4)
        gk = _rope(gk, cos64, sin64, GQA_HD // 4)
    else:
        gk_o[...] = gk
        gv_o[...] = gv
    gqo_ref[...] = (gq * (GQA_HD ** -0.5 * LOG2E)).astype(BF16)
    gkv_ref[:, 0:LANES] = gk.astype(BF16)
    gkv_ref[:, LANES:2 * LANES] = gv.astype(BF16)
    gkv_ref[:, 2 * LANES:3 * LANES] = jnp.ones(gv.shape, BF16)

    dq = z(O_DQ, O_DK)
    dk = z(O_DK, O_DV)
    dv = z(O_DV, O_GL)
    if latent:
        dq = _rope(dq, _tile_lanes(cos64, 4), _tile_lanes(sin64, 4), DIFF_D // 4)
        dk = _rope(dk, _tile_lanes(cos64, 4), _tile_lanes(sin64, 4), DIFF_D // 4)
    else:
        dk_o[...] = dk
        dv_o[...] = dv
    dqo_ref[...] = (dq * (DIFF_D ** -0.5 * LOG2E)).astype(BF16)
    dkv_ref[:, 0:512] = dk.astype(BF16)
    for h in range(DIFF_HEADS):
        dkv_ref[:, 512 + h * 256:512 + h * 256 + LANES] = dv[:, h * LANES:(h + 1) * LANES].astype(BF16)
        dkv_ref[:, 512 + h * 256 + LANES:512 + (h + 1) * 256] = jnp.ones((dv.shape[0], LANES), BF16)

    ret_ref[:, 0:256] = z(O_RQ, O_RK).astype(BF16)
    ret_ref[:, 256:512] = (z(O_RK, O_RV) * (RET_DK ** -0.5)).astype(BF16)
    ret_ref[:, 512:1024] = z(O_RV, O_RG).astype(BF16)
    rg_ref[...] = z(O_RG, O_DQ).astype(BF16)

    for n in range(N_BRANCH):
        gate_ref[:, n * D:(n + 1) * D] = jax.nn.sigmoid(
            z(O_GL + n * D, O_GL + (n + 1) * D)).astype(BF16)


def _inprep_call(latent, l, x, mod3, mod_row, lw, tabs, t_len):
    n_tok = x.shape[0]
    nblk = n_tok // TM
    blk_per_seq = t_len // TM

    def tok(w):
        return pl.BlockSpec((TM, w), lambda i: (i, 0))

    in_specs = [tok(D),
                pl.BlockSpec((None, 1, 6 * D), lambda i: (mod_row(i * TM), 0, 0)),
                _layer_spec(l, (1, D)),
                pl.BlockSpec((None, O_END, D), lambda i: (l, 0, 0), pipeline_mode=pl.Buffered(1)),
                _layer_spec(l, (1, MLA_Q_LORA)), _layer_spec(l, (MLA_Q_LORA, 768)),
                _layer_spec(l, (1, MLA_KV_LORA)), _layer_spec(l, (MLA_KV_LORA, 1024)),
                _layer_spec(l, (1, 512)), _layer_spec(l, (1, LANES)), _const_spec((512, 512))]
    args = [x, mod3, lw["g_pre1"], lw["w_in"], lw["g_mla_q"], lw["w_uq"], lw["g_mla_kv"],
            lw["w_ukv"], lw["g_gqa_q"], lw["g_gqa_k"], lw["bd"]]
    if latent:
        tab_spec = pl.BlockSpec((TM, LANES), lambda i: (i % blk_per_seq, 0))
        in_specs += [tab_spec] * 4
        args += list(tabs)
    widths = [768, KVM_W, 512, GKV_W, 512, DKV_W, 1024, 512, 4 * D]
    out_specs = [tok(w) for w in widths]
    out_shape = [jax.ShapeDtypeStruct((n_tok, w), BF16) for w in widths]
    if not latent:
        cw = [MLA_KV_LORA, MLA_ROPE, 128, 128, 512, 512]
        out_specs += [tok(w) for w in cw]
        out_shape += [jax.ShapeDtypeStruct((n_tok, w), F32) for w in cw]
    return pl.pallas_call(
        functools.partial(_inprep_kernel, latent),
        grid=(nblk,),
        in_specs=in_specs, out_specs=out_specs, out_shape=out_shape,
        compiler_params=_cparams(("arbitrary",)),
        name="inprep_lat" if latent else "inprep_ctx",
    )(*args)


def _pastkv_kernel(ckv_ref, kpe_ref, wukv_ref, o_ref):
    kv = _dot(ckv_ref[...].astype(BF16), wukv_ref[...])
    _store_kvm(o_ref, kv, kpe_ref[...].astype(BF16))


def _pastkv_call(l, ckv, kpe4, w_ukv):
    n = ckv.shape[0]
    return pl.pallas_call(
        _pastkv_kernel,
        grid=(n // TM,),
        in_specs=[pl.BlockSpec((TM, MLA_KV_LORA), lambda i: (i, 0)),
                  pl.BlockSpec((TM, LANES), lambda i: (i, 0)),
                  _layer_spec(l, (MLA_KV_LORA, 1024))],
        out_specs=pl.BlockSpec((TM, KVM_W), lambda i: (i, 0)),
        out_shape=jax.ShapeDtypeStruct((n, KVM_W), BF16),
        compiler_params=_cparams(("arbitrary",)),
        name="pastkv",
    )(ckv, kpe4, w_ukv)


def _softmax_pv(s, v_ones):
    m = jnp.max(s, axis=-1, keepdims=True)
    p = jnp.exp2(s - m).astype(BF16)
    o = _dot(p, v_ones)
    return o[:, 0:LANES] / o[:, LANES:2 * LANES]


def _attn_kernel(lam_init, n_past, qm_ref, kvm_ref, gq_ref, gkv_ref, dq_ref, dkv_ref, *refs):
    if n_past:
        past_refs, refs = refs[:3], refs[3:]
        lam_ref, gdiff_ref, o_ref = refs[:3]
        joined = refs[3:]

        @pl.when(pl.program_id(1) == 0)
        def _():
            for dst, past, new in zip(joined, past_refs, (kvm_ref, gkv_ref, dkv_ref)):
                dst[0:n_past, :] = past[...]
                dst[n_past:, :] = new[...]

        kvm_ref, gkv_ref, dkv_ref = joined
    else:
        lam_ref, gdiff_ref, o_ref = refs
    tq = qm_ref.shape[0]
    lane = _lane_iota((tq, LANES))
    low = lane < HALF_LANES
    zero = jnp.zeros((tq, LANES), BF16)

    for p in range(MLA_HEADS // 2):
        qn = qm_ref[:, p * LANES:(p + 1) * LANES]
        g = p // 2
        qpe = qm_ref[:, 512 + g * LANES:512 + (g + 1) * LANES]
        kk = kvm_ref[:, p * 256:(p + 1) * 256]
        vv = kvm_ref[:, 1024 + p * 256:1024 + (p + 1) * 256]
        outs = []
        for half in range(2):
            h = 2 * p + half
            slot = h % 4
            in_slot = (lane >= slot * MLA_ROPE) & (lane < (slot + 1) * MLA_ROPE)
            lhs = jnp.concatenate(
                [jnp.where(low if half == 0 else ~low, qn, zero),
                 jnp.where(in_slot, qpe, zero)], axis=1)
            outs.append(_softmax_pv(_dot_nt(lhs, kk), vv))
        o_ref[:, p * LANES:(p + 1) * LANES] = jnp.where(low, outs[0], outs[1]).astype(BF16)

    kk = gkv_ref[:, 0:LANES]
    vv = gkv_ref[:, LANES:3 * LANES]
    for g in range(GQA_HEADS // 2):
        qg = gq_ref[:, g * LANES:(g + 1) * LANES]
        o_lo = _softmax_pv(_dot_nt(jnp.where(low, qg, zero), kk), vv)
        o_hi = _softmax_pv(_dot_nt(jnp.where(low, zero, qg), kk), vv)
        o_ref[:, 512 + g * LANES:512 + (g + 1) * LANES] = jnp.where(low, o_lo, o_hi).astype(BF16)

    lp = lam_ref[...]
    lam = (jnp.exp(jnp.sum(lp[0:1] * lp[1:2], axis=-1, keepdims=True))
           - jnp.exp(jnp.sum(lp[2:3] * lp[3:4], axis=-1, keepdims=True)) + lam_init)
    for h in range(DIFF_HEADS):
        qh = dq_ref[:, h * LANES:(h + 1) * LANES]
        kk = dkv_ref[:, h * LANES:(h + 1) * LANES]
        vv = dkv_ref[:, 512 + h * 256:512 + (h + 1) * 256]
        a1 =_softmax_pv(_dot_nt(jnp.where(low, qh, zero), kk), vv)
        a2 = _softmax_pv(_dot_nt(jnp.where(low, zero, qh), kk), vv)
        od = _rms(a1 - lam * a2, gdiff_ref[...]) * (1.0 - lam_init)
        o_ref[:, 1024 + h * LANES:1024 + (h + 1) * LANES] = od.astype(BF16)


def _attn_call(l, lam_init, qm, kvm, gq, gkv, dq, dkv, lam_p, g_diff, n_b, t_len, past=None):
    nq = t_len // TQ
    n_past = 0 if past is None else past[0].shape[0] // n_b

    def qspec(w):
        return pl.BlockSpec((TQ, w), lambda b, i: (b * nq + i, 0))

    def kspec(w, rows=t_len):
        return pl.BlockSpec((rows, w), lambda b, i: (b, 0))

    in_specs = [qspec(768), kspec(KVM_W), qspec(512), kspec(GKV_W), qspec(512), kspec(DKV_W)]
    args = [qm, kvm, gq, gkv, dq, dkv]
    scratch = []
    if n_past:
        in_specs += [kspec(KVM_W, n_past), kspec(GKV_W, n_past), kspec(DKV_W, n_past)]
        args += list(past)
        scratch = [pltpu.VMEM((n_past + t_len, w), BF16) for w in (KVM_W, GKV_W, DKV_W)]
    in_specs += [_layer_spec(l, (4, DIFF_D)), _layer_spec(l, (1, DIFF_DV))]
    args += [lam_p, g_diff]
    return pl.pallas_call(
        functools.partial(_attn_kernel, lam_init, n_past),
        grid=(n_b, nq),
        in_specs=in_specs,
        out_specs=qspec(3 * BRANCH_W),
        out_shape=jax.ShapeDtypeStruct((n_b * t_len, 3 * BRANCH_W), BF16),
        scratch_shapes=scratch,
        compiler_params=_cparams(("arbitrary", "arbitrary")),
        name="attn",
    )(*args)


def _log_sigmoid(x):
    return jnp.minimum(x, 0.0) - jnp.log(1.0 + jnp.exp(-jnp.abs(x)))


def _log_gamma(dec_ref, l, d, h):
    return _log_sigmoid(jnp.full((1, 1), dec_ref[l, d, h], F32))


def _ret_kernel(latent, l, t_len, dec_ref, q_ref, k_ref, v_ref, rg_ref, gret_ref, *refs):
    if latent:
        s0_ref, o_ref = refs
    else:
        o_ref, st_ref = refs
    tq = q_ref.shape[0]
    t0 = pl.program_id(1) * tq
    lane = _lane_iota((tq, LANES))
    low = lane < HALF_LANES
    zero = jnp.zeros((tq, LANES), BF16)
    t_idx = (t0 + lax.broadcasted_iota(jnp.int32, (tq, t_len), 0)).astype(F32)
    s_idx = lax.broadcasted_iota(jnp.int32, (tq, t_len), 1).astype(F32)
    dist = t_idx - s_idx
    past = dist >= 0
    diag = jnp.where(dist == 0, 1.0, 0.0)
    t_col = (t0 + lax.broadcasted_iota(jnp.int32, (tq, 1), 0)).astype(F32)

    def lg(d, h):
        return _log_gamma(dec_ref, l, d, h)

    for h in range(RET_HEADS):
        p, half = h // 2, h % 2
        qp = q_ref[:, p * LANES:(p + 1) * LANES]
        qm = jnp.where(low if half == 0 else ~low, qp, zero)
        kp = k_ref[:, p * LANES:(p + 1) * LANES]
        vh = v_ref[:, h * LANES:(h + 1) * LANES]
        lgf, lgb = lg(0, h), lg(1, h)
        dmask = jnp.exp(jnp.where(past, lgf, -lgb) * dist) + diag
        o = _dot((_dot_nt(qm, kp) * dmask).astype(BF16), vh)
        if latent:
            sf =s0_ref[0, p].astype(BF16)
            sb = s0_ref[1, p].astype(BF16)
            o = o + _dot(qm, sf) * jnp.exp(lgf * (t_col + 1.0))
            o = o + _dot(qm, sb) * jnp.exp(lgb * (float(t_len) - t_col))
        mu = jnp.mean(o, axis=-1, keepdims=True)
        oc = o - mu
        y = oc * lax.rsqrt(jnp.mean(oc * oc, axis=-1, keepdims=True) + EPS)
        y = y * gret_ref[:, h * LANES:(h + 1) * LANES]
        rg = rg_ref[:, h * LANES:(h + 1) * LANES].astype(F32)
        o_ref[:, h * LANES:(h + 1) * LANES] = (y * _silu(rg)).astype(BF16)

    if not latent:
        s_col = lax.broadcasted_iota(jnp.int32, (t_len, 1), 0).astype(F32)
        lane_t = _lane_iota((1, LANES)) < HALF_LANES
        for p in range(RET_HEADS // 2):
            kp = k_ref[:, p * LANES:(p + 1) * LANES].astype(F32)
            for d in range(2):
                lg_lane = jnp.where(lane_t, lg(d, 2 * p), lg(d, 2 * p + 1))
                expo = (float(t_len) - 1.0 - s_col) if d == 0 else s_col
                kdec_t = jnp.transpose(kp * jnp.exp(lg_lane * expo)).astype(BF16)
                for half in range(2):
                    h = 2 * p + half
                    st = _dot(kdec_t, v_ref[:, h * LANES:(h + 1) * LANES])
                    st_ref[d, h] = st[half * RET_DK:(half + 1) * RET_DK, :]


def _ret_call(latent, l, dec, ret, rg, g_ret, s0, n_b, t_len):
    nq = t_len // TQ
    assert latent or nq == 1
    in_specs = [pl.BlockSpec(memory_space=pltpu.SMEM),
                pl.BlockSpec((TQ, 256), lambda b, i: (b * nq + i, 0)),
                pl.BlockSpec((t_len, 256), lambda b, i: (b, 1)),
                pl.BlockSpec((t_len, 512), lambda b, i: (b, 1)),
                pl.BlockSpec((TQ, 512), lambda b, i: (b * nq + i, 0)),
                _layer_spec(l, (1, 512))]
    args = [dec, ret, ret, ret, rg, g_ret]
    out_specs = [pl.BlockSpec((TQ, 512), lambda b, i: (b * nq + i, 0))]
    out_shape = [jax.ShapeDtypeStruct((n_b * t_len, 512), BF16)]
    if latent:
        in_specs.append(pl.BlockSpec((None, 2, 2, LANES, LANES), lambda b, i: (b, 0, 0, 0, 0)))
        args.append(s0)
    else:
        out_specs.append(pl.BlockSpec((None, 2, RET_HEADS, RET_DK, RET_DV),
                                      lambda b, i: (b, 0, 0, 0, 0)))
        out_shape.append(jax.ShapeDtypeStruct((n_b, 2, RET_HEADS, RET_DK, RET_DV), F32))
    return pl.pallas_call(
        functools.partial(_ret_kernel, latent, l, t_len),
        grid=(n_b, nq),
        in_specs=in_specs, out_specs=out_specs, out_shape=out_shape,
        compiler_params=_cparams(("arbitrary", "arbitrary")),
        name="ret_lat" if latent else "ret_ctx",
    )(*args)


def _merge_kernel(x_ref, mod_ref, br_ref, or_ref, gate_ref, wbr_ref, wout_ref, gpost_ref, o_ref):
    merged = None
    for n in range(N_BRANCH):
        if n < 2:
            b = br_ref[:, n * BRANCH_W:(n + 1) * BRANCH_W]
        elif n == 2:
            b = or_ref[...]
        else:
            b = br_ref[:, 2 * BRANCH_W:3 * BRANCH_W]
        t = gate_ref[:, n * D:(n + 1) * D].astype(F32) * _dot(b, wbr_ref[n])
        merged = t if merged is None else merged + t
    out = _dot(merged.astype(BF16), wout_ref[...])
    g1 = mod_ref[...][:, 2 * D:3 * D]
    o_ref[...] = x_ref[...] + g1 * _rms(out, gpost_ref[...])


def _merge_call(l, x, mod3, mod_row, br, o_r, gates, lw):
    n_tok = x.shape[0]

    def tok(w):
        return pl.BlockSpec((TM_MERGE, w), lambda i: (i, 0))

    return pl.pallas_call(
        _merge_kernel,
        grid=(n_tok // TM_MERGE,),
        in_specs=[tok(D), pl.BlockSpec((None, 1, 6 * D), lambda i: (mod_row(i * TM_MERGE), 0, 0)),
                  tok(3 * BRANCH_W), tok(BRANCH_W), tok(4 * D),
                  _layer_spec(l, (N_BRANCH, BRANCH_W, D)), _layer_spec(l, (D, D)),
                  _layer_spec(l, (1, D))],
        out_specs=tok(D),
        out_shape=jax.ShapeDtypeStruct((n_tok, D), F32),
        compiler_params=_cparams(("arbitrary",)),
        name="merge",
    )(x, mod3, br, o_r, gates, lw["w_br"], lw["w_out"], lw["g_post1"])


def _route(logits_t, bias):
    n = logits_t.shape[1]
    scores = jax.nn.sigmoid(logits_t)
    sel = scores + bias
    neg = -jnp.inf
    sub = lax.broadcasted_iota(jnp.int32, (GROUP_SIZE, n), 0)
    grp = []
    for g in range(N_GROUPS):
        blk = sel[g * GROUP_SIZE:(g + 1) * GROUP_SIZE]
        m1 = jnp.max(blk, axis=0, keepdims=True)
        i1 = jnp.min(jnp.where(blk == m1, sub, GROUP_SIZE), axis=0, keepdims=True)
        m2 = jnp.max(jnp.where(sub == i1, neg, blk), axis=0, keepdims=True)
        grp.append(m1 + m2)
    parts = []
    for g in range(N_GROUPS):
        beaten = jnp.zeros((1, n), jnp.int32)
        for o in range(N_GROUPS):
            if o == g:
                continue
            wins = (grp[o] > grp[g]) | (grp[o] == grp[g]) if o < g else (grp[o] > grp[g])
            beaten = beaten + wins.astype(jnp.int32)
        keep = beaten < TOPK_GROUPS
        parts.append(jnp.where(keep, sel[g * GROUP_SIZE:(g + 1) * GROUP_SIZE], neg))
    cur = jnp.concatenate(parts, axis=0)
    eidx = lax.broadcasted_iota(jnp.int32, (N_EXPERTS, n), 0)
    hits, ids, ws = [], [], []
    for _ in range(TOP_K):
        m = jnp.max(cur, axis=0, keepdims=True)
        i = jnp.min(jnp.where(cur == m, eidx, N_EXPERTS), axis=0, keepdims=True)
        hit = eidx == i
        hits.append(hit)
        ids.append(i)
        ws.append(jnp.sum(jnp.where(hit, scores, 0.0), axis=0, keepdims=True))
        cur = jnp.where(hit, neg, cur)
    wsum = ws[0] + ws[1] + ws[2] + ws[3]
    return hits, ids, [w / wsum * ROUTE_SCALE for w in ws]


U32 = jnp.uint32
HIGH16 = np.uint32(0xFFFF0000)


def _bf16_bits(v):
    return lax.bitcast_convert_type(v.astype(BF16).astype(F32), U32)


def _pack_rows(v):
    return (_bf16_bits(v[:, 0:D // 2]) >> 16) | _bf16_bits(v[:, D // 2:D])


def _unpack_rows(p):
    lo = lax.bitcast_convert_type(p << 16, F32)
    hi = lax.bitcast_convert_type(p & HIGH16, F32)
    return jnp.concatenate([lo, hi], axis=1)


def _moe_pre_kernel(x_ref, mod_ref, gpre_ref, wr_ref, br_ref, tri_ref,
                    hp_ref, eidx_ref, rank_ref, comb_ref, cnt_ref, run_ref):
    tm = x_ref.shape[0]

    @pl.when(pl.program_id(0) == 0)
    def _():
        run_ref[...] = jnp.zeros_like(run_ref)

    mod = mod_ref[...]
    sh2, sc2 = mod[:, 3 * D:4 * D], mod[:, 4 * D:5 * D]
    h = _rms(x_ref[...], gpre_ref[...]) * (1.0 + sc2) + sh2
    hp_ref[...] = _pack_rows(h)
    hb = h.astype(BF16)
    h_lo = (h - hb.astype(F32)).astype(BF16)
    wr = wr_ref[...]
    wr_hi = wr.astype(BF16)
    wr_lo = (wr - wr_hi.astype(F32)).astype(BF16)
    logits_t = _dot_nt(wr_hi, hb) + _dot_nt(wr_hi, h_lo) + _dot_nt(wr_lo, hb)
    hits, ids, ws = _route(logits_t, br_ref[...])

    picked = jnp.zeros((N_EXPERTS, tm), F32)
    for hit in hits:
        picked = jnp.where(hit, 1.0, picked)
    before = _dot(picked.astype(BF16), tri_ref[...]) + run_ref[:, 0:1]
    sub8 = lax.broadcasted_iota(jnp.int32, (8, tm), 0)
    comb8 = jnp.zeros((8, tm), F32)
    for k in range(TOP_K):
        rank = jnp.sum(jnp.where(hits[k], before, 0.0), axis=0, keepdims=True)
        eidx_ref[k:k + 1, :] = ids[k]
        rank_ref[k:k + 1, :] = rank.astype(jnp.int32)
        comb8 = jnp.where(sub8 == k, ws[k], comb8)
    comb_ref[...] = jnp.transpose(
        jnp.concatenate([comb8, jnp.zeros((LANES - 8, tm), F32)], axis=0))
    run_ref[...] = run_ref[...] + jnp.sum(picked, axis=1, keepdims=True)
    cnt_ref[...] = run_ref[...]


def _moe_pre_call(l, x, mod3, mod_row, lw):
    n_tok = x.shape[0]
    tm = TM_MOE_PRE
    tri = np.arange(tm)
    tri = jnp.asarray(tri[:, None] < tri[None, :], BF16)
    row4 = pl.BlockSpec((TOP_K, tm), lambda i: (0, i))
    return pl.pallas_call(
        _moe_pre_kernel,
        grid=(n_tok // tm,),
        in_specs=[pl.BlockSpec((tm, D), lambda i: (i, 0)),
                  pl.BlockSpec((None, 1, 6 * D), lambda i: (mod_row(i * tm), 0, 0)),
                  _layer_spec(l, (1, D)), _layer_spec(l, (N_EXPERTS, D)),
                  _layer_spec(l, (N_EXPERTS, 1)), _const_spec((tm, tm))],
        out_specs=[pl.BlockSpec((tm, D // 2), lambda i: (i, 0)), row4, row4,
                   pl.BlockSpec((tm, LANES), lambda i: (i, 0)),
                   _const_spec((N_EXPERTS, LANES))],
        out_shape=[jax.ShapeDtypeStruct((n_tok, D // 2), U32),
                   jax.ShapeDtypeStruct((TOP_K, n_tok), jnp.int32),
                   jax.ShapeDtypeStruct((TOP_K, n_tok), jnp.int32),
                   jax.ShapeDtypeStruct((n_tok, LANES), F32),
                   jax.ShapeDtypeStruct((N_EXPERTS, LANES), F32)],
        scratch_shapes=[pltpu.VMEM((N_EXPERTS, LANES), F32)],
        compiler_params=_cparams(("arbitrary",)),
        name="moe_pre",
    )(x, mod3, lw["g_pre2"], lw["w_router_t"], lw["b_router"], tri)


def _moe_plan_kernel(eidx_ref, rank_ref, cnt_ref, dest_ref, te_ref, tv_ref, tn_ref):
    tm = eidx_ref.shape[1]
    cnt = cnt_ref[...]
    padded = jnp.ceil(cnt * (1.0 / TMX)) * TMX
    row = lax.broadcasted_iota(jnp.int32, cnt.shape, 0)
    incl = padded
    shift = 1
    while shift < N_EXPERTS:
        incl = incl + jnp.where(row >= shift, pltpu.roll(incl, shift, 0), 0.0)
        shift *= 2
    start = (incl - padded)[:, 0:1]
    end = incl[:, 0:1]
    erow = lax.broadcasted_iota(jnp.int32, (N_EXPERTS, tm), 0)
    for k in range(TOP_K):
        mine = erow == eidx_ref[k:k + 1, :]
        base = jnp.sum(jnp.where(mine, start, 0.0), axis=0, keepdims=True)
        dest_ref[k:k + 1, :] = rank_ref[k:k + 1, :] + base.astype(jnp.int32)

    @pl.when(pl.program_id(0) == 0)
    def _():
        tile0 = (_lane_iota((1, LANES)) * TMX).astype(F32)
        owner = jnp.sum(jnp.where(end <= tile0, 1.0, 0.0), axis=0, keepdims=True)
        owner = jnp.minimum(owner, N_EXPERTS - 1.0)
        erow_t = lax.broadcasted_iota(jnp.int32, (N_EXPERTS, LANES), 0).astype(F32)
        left = jnp.sum(jnp.where(erow_t == owner, cnt[:, 0:1] - (tile0 - start), 0.0),
                       axis=0, keepdims=True)
        te_ref[...] = owner.astype(jnp.int32)
        tv_ref[...] = jnp.clip(left, 0.0, float(TMX)).astype(jnp.int32)
        tn_ref[...] = jnp.full(tn_ref.shape, N_EXPERTS, jnp.int32)
        nxt = owner
        for k in range(W_SLOTS - 1):
            later = (erow_t > nxt) & (cnt[:, 0:1] > 0.0)
            nxt = jnp.min(jnp.where(later, erow_t, float(N_EXPERTS)), axis=0, keepdims=True)
            tn_ref[k:k + 1, :] = nxt.astype(jnp.int32)


def _moe_plan_call(eidx, rank, cnt):
    n_tok = eidx.shape[1]
    tm = TM_MOE_PRE
    row4 = pl.BlockSpec((TOP_K, tm), lambda i: (0, i))
    tiles = jax.ShapeDtypeStruct((1, LANES), jnp.int32)
    return pl.pallas_call(
        _moe_plan_kernel,
        grid=(n_tok // tm,),
        in_specs=[row4, row4, _const_spec((N_EXPERTS, LANES))],
        out_specs=[row4, _const_spec((1, LANES)), _const_spec((1, LANES)), _const_spec((8, LANES))],
        out_shape=[jax.ShapeDtypeStruct((TOP_K, n_tok), jnp.int32), tiles, tiles,
                   jax.ShapeDtypeStruct((8, LANES), jnp.int32)],
        compiler_params=_cparams(("arbitrary",)),
        name="moe_plan",
    )(eidx, rank, cnt)


def _experts_kernel(l, te_ref, tv_ref, tn_ref, xs_ref, wgu_hbm, wdn_hbm, ys_ref,
                    wgu_f, wdn_f, wgu_b, wdn_b, sem, group_ref):
    j = pl.program_id(0)
    valid = tv_ref[j]
    expert = te_ref[j]

    def fetch(e, slot):
        return (pltpu.make_async_copy(wgu_hbm.at[l, e], wgu_f.at[slot], sem.at[slot, 0]),
                pltpu.make_async_copy(wdn_hbm.at[l, e], wdn_f.at[slot], sem.at[slot, 1]))

    def start_if_any(e, slot):
        @pl.when(e < N_EXPERTS)
        def _():
            for cp in fetch(e, slot):
                cp.start()

    @pl.when(j == 0)
    def _():
        group_ref[0] = 0
        start_if_any(expert, 0)
        for k in range(W_SLOTS - 2):
            start_if_any(tn_ref[k, 0], k + 1)

    first_tile = (j == 0) | (expert != te_ref[jnp.maximum(j - 1, 0)])

    @pl.when(first_tile & (valid > 0))
    def _():
        group = group_ref[0]
        slot = lax.rem(group, W_SLOTS)
        for cp in fetch(expert, slot):
            cp.wait()
        wgu_b[...] = wgu_f[slot].astype(BF16)
        wdn_b[...] = wdn_f[slot].astype(BF16)
        start_if_any(tn_ref[W_SLOTS - 2, j], lax.rem(group + W_SLOTS - 1, W_SLOTS))
        group_ref[0] = group + 1

    @pl.when(valid > 0)
    def _():
        rows = lax.broadcasted_iota(jnp.int32, (TMX, D), 0)
        x = jnp.where(rows < valid, _unpack_rows(xs_ref[...]), 0.0).astype(BF16)
        gu = _dot(x, wgu_b[...])
        a = _silu(gu[:, 0:EXPERT_FF]) * gu[:, EXPERT_FF:2 * EXPERT_FF]
        ys_ref[...] = _pack_rows(_dot(a.astype(BF16), wdn_b[...]))

    @pl.when(valid <= 0)
    def _():
        ys_ref[...] = jnp.zeros_like(ys_ref)


def _experts_call(l, xs, te, tv, tn, w_gu, w_dn):
    n_tiles = xs.shape[0] // TMX
    grid_spec = pltpu.PrefetchScalarGridSpec(
        num_scalar_prefetch=3,
        grid=(n_tiles,),
        in_specs=[pl.BlockSpec((TMX, D // 2), lambda j, *_: (j, 0)),
                  pl.BlockSpec(memory_space=pl.ANY), pl.BlockSpec(memory_space=pl.ANY)],
        out_specs=pl.BlockSpec((TMX, D // 2), lambda j, *_: (j, 0)),
        scratch_shapes=[pltpu.VMEM((W_SLOTS, D, 2 * EXPERT_FF), F32),
                        pltpu.VMEM((W_SLOTS, EXPERT_FF, D), F32),
                        pltpu.VMEM((D, 2 * EXPERT_FF), BF16), pltpu.VMEM((EXPERT_FF, D), BF16),
                        pltpu.SemaphoreType.DMA((W_SLOTS, 2)), pltpu.SMEM((1,), jnp.int32)])
    return pl.pallas_call(
        functools.partial(_experts_kernel, l),
        grid_spec=grid_spec,
        out_shape=jax.ShapeDtypeStruct(xs.shape, U32),
        compiler_params=_cparams(("arbitrary",)),
        name="moe_experts",
    )(te, tv, tn, xs, w_gu, w_dn)


def _moe_post_kernel(x_ref, mod_ref, hp_ref, yg_ref, comb_ref, wsgu_ref, wsdn_ref, gpost_ref,
                     o_ref):
    hb = _unpack_rows(hp_ref[...]).astype(BF16)
    sgu = _dot(hb, wsgu_ref[...])
    sa = _silu(sgu[:, 0:SHARED_FF]) * sgu[:, SHARED_FF:2 * SHARED_FF]
    acc = _dot(sa.astype(BF16), wsdn_ref[...])
    comb = comb_ref[...]
    for k in range(TOP_K):
        acc = acc + comb[:, k:k + 1] * _unpack_rows(yg_ref[k])
    g2 = mod_ref[...][:, 5 * D:6 * D]
    o_ref[...] = x_ref[...] + g2 * _rms(acc, gpost_ref[...])


def _moe_post_call(l, x, mod3, mod_row, hp, yg, comb, lw):
    n_tok = x.shape[0]
    tm = TM_MOE_PRE
    return pl.pallas_call(
        _moe_post_kernel,
        grid=(n_tok // tm,),
        in_specs=[pl.BlockSpec((tm, D), lambda i: (i, 0)),
                  pl.BlockSpec((None, 1, 6 * D), lambda i: (mod_row(i * tm), 0, 0)),
                  pl.BlockSpec((tm, D // 2), lambda i: (i, 0)),
                  pl.BlockSpec((TOP_K, tm, D // 2), lambda i: (0, i, 0)),
                  pl.BlockSpec((tm, LANES), lambda i: (i, 0)),
                  _layer_spec(l, (D, 2 * SHARED_FF)), _layer_spec(l, (SHARED_FF, D)),
                  _layer_spec(l, (1, D))],
        out_specs=pl.BlockSpec((tm, D), lambda i: (i, 0)),
        out_shape=jax.ShapeDtypeStruct((n_tok, D), F32),
        compiler_params=_cparams(("arbitrary",)),
        name="moe_post",
    )(x, mod3, hp, yg, comb, lw["w_sh_gu"], lw["w_sh_down"], lw["g_post2"])


def _moe_call(l, x, mod3, mod_row, lw):
    n_tok = x.shape[0]
    n_slots = -(-(TOP_K * n_tok + N_EXPERTS * (TMX - 1)) // TMX) * TMX
    assert n_slots // TMX <= LANES
    hp, eidx, rank, comb, cnt = _moe_pre_call(l, x, mod3, mod_row, lw)
    dest, te, tv, tn = _moe_plan_call(eidx, rank, cnt)
    dest = dest.reshape(TOP_K * n_tok)
    xs = _sc_scatter_rows(hp, dest, n_slots)
    ys = _experts_call(l, xs, te[0], tv[0], tn, lw["w_exp_gu"], lw["w_exp_down"])
    yg = _sc_gather_rows(ys, dest).reshape(TOP_K, n_tok, D // 2)
    return _moe_post_call(l, x, mod3, mod_row, hp, yg, comb, lw)


SC_CORES, SC_SUBCORES = 2, 16
SC_WORKERS = SC_CORES * SC_SUBCORES


def _sc_gather_rows(table, idx, chunk=64):
    n_out, width = idx.shape[0], table.shape[1]
    per_worker = n_out // SC_WORKERS
    n_chunks = per_worker // chunk
    assert per_worker * SC_WORKERS == n_out and n_chunks * chunk == per_worker
    mesh = plsc.VectorSubcoreMesh(core_axis_name="c", subcore_axis_name="s",
                                  num_cores=SC_CORES, num_subcores=SC_SUBCORES)

    @functools.partial(
        pl.kernel, mesh=mesh,
        out_type=jax.ShapeDtypeStruct((n_out, width), table.dtype),
        scratch_types=[pltpu.VMEM((chunk,), jnp.int32), pltpu.VMEM((chunk, width), table.dtype),
                       pltpu.SemaphoreType.DMA],
        name="sc_gather")
    def gather(table_hbm, idx_hbm, out_hbm, idx_v, rows_v, sem):
        base = (lax.axis_index("s") * SC_CORES + lax.axis_index("c")) * per_worker

        @pl.loop(0, n_chunks)
        def _(j):
            off = base + j * chunk
            pltpu.sync_copy(idx_hbm.at[pl.ds(off, chunk)], idx_v)
            pltpu.async_copy(table_hbm.at[idx_v], rows_v, sem).wait()
            pltpu.sync_copy(rows_v, out_hbm.at[pl.ds(off, chunk)])

    return gather(table, idx)


def _sc_scatter_rows(rows, dest, n_slots, chunk=64):
    n_tok, width = rows.shape
    per_worker = n_tok // SC_WORKERS
    n_chunks = per_worker // chunk
    assert per_worker * SC_WORKERS == n_tok and n_chunks * chunk == per_worker
    mesh = plsc.VectorSubcoreMesh(core_axis_name="c", subcore_axis_name="s",
                                  num_cores=SC_CORES, num_subcores=SC_SUBCORES)

    @functools.partial(
        pl.kernel, mesh=mesh,
        out_type=jax.ShapeDtypeStruct((n_slots, width), rows.dtype),
        scratch_types=[pltpu.VMEM((chunk,), jnp.int32), pltpu.VMEM((chunk, width), rows.dtype)],
        name="sc_scatter")
    def scatter(rows_hbm, dest_hbm, out_hbm, idx_v, rows_v):
        base = (lax.axis_index("s") * SC_CORES + lax.axis_index("c")) * per_worker

        @pl.loop(0, n_chunks)
        def _(j):
            off = base + j * chunk
            pltpu.sync_copy(rows_hbm.at[pl.ds(off, chunk)], rows_v)
            for k in range(TOP_K):
                pltpu.sync_copy(dest_hbm.at[pl.ds(k * n_tok + off, chunk)], idx_v)
                pltpu.sync_copy(rows_v, out_hbm.at[idx_v])

    return scatter(rows, dest)


def _rope_tables(t_len):
    pos = np.arange(t_len)
    row, col = pos // GRID_W, pos % GRID_W

    def tab(r):
        half = r // 2
        freq = ROPE_BASE ** (-np.arange(half, dtype=np.float64) / half)
        sign = np.concatenate([-np.ones(half), np.ones(half)])
        cs, sn = [], []
        for p in (row, col):
            ang = p[:, None].astype(np.float64) * freq[None, :]
            cs.append(np.concatenate([np.cos(ang), np.cos(ang)], axis=1))
            sn.append(np.concatenate([np.sin(ang), np.sin(ang)], axis=1) * sign[None, :])
        return np.concatenate(cs, axis=1), np.concatenate(sn, axis=1)

    c64, s64 = tab(GQA_HD // 2)
    cpe, spe = tab(MLA_ROPE // 2)
    out = (np.tile(c64, (1, 2)), np.tile(s64, (1, 2)), np.tile(cpe, (1, 4)), np.tile(spe, (1, 4)))
    return tuple(jnp.asarray(a, F32) for a in out)


def _prep_weights(p):
    n_l = p["w_in"].shape[0]

    def row(name):
        return p[name].reshape(n_l, 1, -1)

    w_uq = p["w_mla_uq"].reshape(n_l, MLA_Q_LORA, MLA_HEADS, MLA_NOPE + MLA_ROPE)
    w_uq = jnp.concatenate([w_uq[..., :MLA_NOPE].reshape(n_l, MLA_Q_LORA, -1),
                            w_uq[..., MLA_NOPE:].reshape(n_l, MLA_Q_LORA, -1)], axis=-1)
    w_ukv = p["w_mla_ukv"].reshape(n_l, MLA_KV_LORA, MLA_HEADS, MLA_NOPE + MLA_V)
    w_ukv = jnp.concatenate([w_ukv[..., :MLA_NOPE].reshape(n_l, MLA_KV_LORA, -1),
                             w_ukv[..., MLA_NOPE:].reshape(n_l, MLA_KV_LORA, -1)], axis=-1)
    w_br = p["w_br"]
    w_br_gqa = w_br[:, 1].reshape(n_l, GQA_HEADS, GQA_HD, D)[:, jnp.array(GQA_ORDER)]
    w_br = jnp.concatenate([w_br[:, 0:1], w_br_gqa.reshape(n_l, 1, BRANCH_W, D), w_br[:, 2:4]], axis=1)
    blk = np.arange(512) // GQA_HD
    return {
        "g_pre1": row("g_pre1"), "g_post1": row("g_post1"),
        "g_pre2": row("g_pre2"), "g_post2": row("g_post2"),
        "w_in": jnp.swapaxes(p["w_in"], 1, 2).astype(BF16),
        "g_mla_q": row("g_mla_q"), "w_uq": w_uq.astype(BF16),
        "g_mla_kv": row("g_mla_kv"), "w_ukv": w_ukv.astype(BF16),
        "g_gqa_q": jnp.tile(p["g_gqa_q"], (1, GQA_HEADS)).reshape(n_l, 1, -1),
        "g_gqa_k": jnp.tile(p["g_gqa_k"], (1, GQA_KV_HEADS)).reshape(n_l, 1, -1),
        "bd": jnp.asarray(blk[:, None] == blk[None, :], BF16),
        "ret_decay": p["ret_decay"],
        "g_ret": row("g_ret"),
        "diff_lambda": p["diff_lambda"], "g_diff": row("g_diff"),
        "w_br": w_br.astype(BF16), "w_out": p["w_out"].astype(BF16),
        "w_router_t": jnp.swapaxes(p["w_router"], 1, 2),
        "b_router": p["b_router"].reshape(n_l, N_EXPERTS, 1),
        "w_exp_gu": p["w_exp_gu"], "w_exp_down": p["w_exp_down"],
        "w_sh_gu": p["w_sh_gu"].astype(BF16), "w_sh_down": p["w_sh_down"].astype(BF16),
    }


def _mixers(latent, l, x, mod3, mod_row, lw, n_b, t_len, tabs=None, past=None, s0=None):
    lam_init = 0.8 - 0.6 * math.exp(-0.3 * l)
    outs = _inprep_call(latent, l, x, mod3, mod_row, lw, tabs, t_len)
    qm, kvm, gq, gkv, dq, dkv, ret, rg, gates = outs[:9]
    br = _attn_call(l, lam_init, qm, kvm, gq, gkv, dq, dkv, lw["diff_lambda"], lw["g_diff"],
                    n_b, t_len, past)
    r = _ret_call(latent, l, lw["ret_decay"], ret, rg, lw["g_ret"], s0, n_b, t_len)
    y = _merge_call(l, x, mod3, mod_row, br, r[0], gates, lw)
    cache = None if latent else tuple(outs[9:]) + (r[1],)
    return y, cache


def kernel(x_prompt, x_sample, cache_mla_ckv, cache_mla_kpe, cache_gqa_k, cache_gqa_v, cache_diff_k, cache_diff_v, state_ret, c, c_ctx, w_mod, b_mod, g_pre1, g_post1, g_pre2, g_post2, w_in, g_mla_q, w_mla_uq, g_mla_kv, w_mla_ukv, g_gqa_q, g_gqa_k, ret_decay, g_ret, diff_lambda, g_diff, w_br, w_out, w_router, b_router, w_exp_gu, w_exp_down, w_sh_gu, w_sh_down):
    params = dict(w_in=w_in, g_pre1=g_pre1, g_post1=g_post1, g_pre2=g_pre2,
                  g_post2=g_post2, g_mla_q=g_mla_q, w_mla_uq=w_mla_uq,
                  g_mla_kv=g_mla_kv, w_mla_ukv=w_mla_ukv, g_gqa_q=g_gqa_q, g_gqa_k=g_gqa_k,
                  ret_decay=ret_decay, g_ret=g_ret, diff_lambda=diff_lambda, g_diff=g_diff,
                  w_br=w_br, w_out=w_out, w_router=w_router, b_router=b_router,
                  w_exp_gu=w_exp_gu, w_exp_down=w_exp_down, w_sh_gu=w_sh_gu, w_sh_down=w_sh_down)
    n_bc, t_c, _ = x_prompt.shape
    n_bl, t_l, _ = x_sample.shape
    p_len = cache_mla_ckv.shape[2]
    tabs = _rope_tables(t_l)
    n_cond = 8
    cond = jnp.concatenate([c_ctx[None, :], c, jnp.zeros((n_cond - 1 - n_bl, D), F32)], axis=0)
    assert t_l % TM_MERGE == 0 and (t_c * n_bc) % TM_MERGE == 0
    assert t_l % TM_MOE_PRE == 0 and (t_c * n_bc) % TM_MOE_PRE == 0

    yp = x_prompt.reshape(n_bc * t_c, D)
    ys = x_sample.reshape(n_bl * t_l, D)
    caches = []
    lw = _prep_weights(params)
    for l in range(DEPTH):
        mod3 = _mod_call(l, cond, w_mod, b_mod).reshape(n_cond, 1, 6 * D)
        yp, cache = _mixers(False, l, yp, mod3, lambda i: 0, lw, n_bc, t_c)
        yp = _moe_call(l, yp, mod3, lambda i: 0, lw)
        caches.append(cache)
        past_kvm = _pastkv_call(l, cache_mla_ckv[:, l].reshape(n_bl * p_len, -1),
                                jnp.tile(cache_mla_kpe[:, l].reshape(n_bl * p_len, -1), (1, 4)),
                                lw["w_ukv"])
        past_gkv = jnp.concatenate([cache_gqa_k[:, l].reshape(n_bl * p_len, -1),
                                    cache_gqa_v[:, l].reshape(n_bl * p_len, -1),
                                    jnp.ones((n_bl * p_len, LANES), F32)], axis=-1).astype(BF16)
        past_dv = jnp.concatenate([cache_diff_v[:, l], jnp.ones_like(cache_diff_v[:, l])], axis=-1)
        past_dkv = jnp.concatenate([cache_diff_k[:, l].reshape(n_bl * p_len, -1),
                                    past_dv.reshape(n_bl * p_len, -1)], axis=-1).astype(BF16)
        s0 = state_ret[:, l].reshape(n_bl, 2, RET_HEADS // 2, 2 * RET_DK, RET_DV)
        ys, _ = _mixers(True, l, ys, mod3, lambda t: 1 + t // t_l, lw, n_bl, t_l, tabs=tabs,
                        past=(past_kvm, past_gkv, past_dkv), s0=s0)
        ys = _moe_call(l, ys, mod3, lambda t: 1 + t // t_l, lw)

    def stack(k, shape):
        return jnp.stack([caches[l][k].reshape((n_bc, t_c) + shape) for l in range(DEPTH)], axis=1)

    new_ret = jnp.stack([caches[l][6] for l in range(DEPTH)], axis=1)
    return (yp.reshape(n_bc, t_c, D), ys.reshape(n_bl, t_l, D),
            stack(0, (MLA_KV_LORA,)), stack(1, (MLA_ROPE,)),
            stack(2, (GQA_KV_HEADS, GQA_HD)), stack(3, (GQA_KV_HEADS, GQA_HD)),
            stack(4, (DIFF_HEADS, 2, DIFF_D)), stack(5, (DIFF_HEADS, DIFF_DV)), new_ret)
```

```python
import functools
import math

import numpy as np
import jax
import jax.numpy as jnp
from jax import lax
from jax.experimental import pallas as pl
from jax.experimental.pallas import tpu as pltpu
from jax.experimental.pallas import tpu_sc as plsc

F32 = jnp.float32
BF16 = jnp.bfloat16

D = 1024
DEPTH = 2
GRID_W = 64
ROPE_BASE = 10000.0
EPS = 1e-6

MLA_HEADS, MLA_NOPE, MLA_ROPE, MLA_V = 8, 64, 32, 64
MLA_Q_LORA, MLA_KV_LORA = 384, 256
GQA_HEADS, GQA_KV_HEADS, GQA_HD = 8, 2, 64
RET_HEADS, RET_DK, RET_DV = 4, 64, 128
DIFF_HEADS, DIFF_D, DIFF_DV = 4, 64, 128
N_BRANCH, BRANCH_W = 4, 512
N_EXPERTS, TOP_K, N_GROUPS, TOPK_GROUPS = 32, 4, 4, 2
EXPERT_FF, SHARED_FF = 256, 256
ROUTE_SCALE = 2.5
GROUP_SIZE = N_EXPERTS // N_GROUPS

LANES = 128
HALF_LANES = 64
VMEM_LIMIT = 56 * 1024 * 1024

C_CQ, C_CKV, C_KPE, C_GQ, C_GK, C_GV = 0, 384, 640, 768, 1280, 1408
C_DQ, C_DK, C_DV, C_RQ, C_RK, C_RV, C_RG, C_GL, C_END = (
    1536, 2048, 2560, 3072, 3328, 3584, 4096, 4608, 8704)
O_CQ, O_CKV, O_KPE, O_GQ, O_GK, O_GV = 0, 384, 640, 672, 1184, 1312
O_RQ, O_RK, O_RV, O_RG, O_DQ, O_DK, O_DV, O_GL, O_END = (
    1440, 1696, 1952, 2464, 2976, 3488, 4000, 4512, 8608)
GQA_ORDER = (0, 4, 1, 5, 2, 6, 3, 7)

KVM_W = 8 * 256
GKV_W = 3 * LANES
DKV_W = 512 + 4 * 256
LOG2E = 1.4426950408889634
TM = 256
TM_MERGE = 512
TQ = 256
TM_MOE_PRE = 512
TMX = 256
W_SLOTS = 4


def _cparams(sem):
    return pltpu.CompilerParams(dimension_semantics=sem, vmem_limit_bytes=VMEM_LIMIT)


def _const_spec(shape):
    nd = len(shape)
    return pl.BlockSpec(shape, lambda *_: (0,) * nd)


def _layer_spec(l, shape):
    nd = len(shape)
    return pl.BlockSpec((None,) + tuple(shape), lambda *_: (l,) + (0,) * nd)


def _rms(x, g):
    return x * lax.rsqrt(jnp.mean(x * x, axis=-1, keepdims=True) + EPS) * g


def _dot(a, b):
    return jnp.dot(a, b, preferred_element_type=F32)


def _dot_nt(a, b):
    return lax.dot_general(a, b, (((1,), (1,)), ((), ())), preferred_element_type=F32)


def _silu(x):
    return x * jax.nn.sigmoid(x)


def _lane_iota(shape):
    return lax.broadcasted_iota(jnp.int32, shape, len(shape) - 1)


def _seg_meansq(x, bd_ref, width):
    sq = x * x
    hi = sq.astype(BF16)
    lo = (sq - hi.astype(F32)).astype(BF16)
    bd = bd_ref[0:width, 0:width]
    return (_dot(hi, bd) + _dot(lo, bd)) * (1.0 / GQA_HD)


def _rope(x, cos, sin_signed, half):
    width = x.shape[-1]
    first = (_lane_iota(x.shape) % (2 * half)) < half
    partner = jnp.where(first, pltpu.roll(x, width - half, 1), pltpu.roll(x, half, 1))
    return x * cos + partner * sin_signed


def _tile_lanes(t, reps):
    return t if reps == 1 else jnp.concatenate([t] * reps, axis=1)


def _store_kvm(kvm_ref, kv, kpe_b):
    ones = jnp.ones(kpe_b.shape, BF16)
    for p in range(4):
        kvm_ref[:, p * 256:p * 256 + LANES] = kv[:, p * LANES:(p + 1) * LANES].astype(BF16)
        kvm_ref[:, p * 256 + LANES:(p + 1) * 256] = kpe_b
        kvm_ref[:, 1024 + p * 256:1024 + p * 256 + LANES] = (
            kv[:, 512 + p * LANES:512 + (p + 1) * LANES].astype(BF16))
        kvm_ref[:, 1024 + p * 256 + LANES:1024 + (p + 1) * 256] = ones


def _mod_kernel(c_ref, w_ref, b_ref, o_ref):
    a = _silu(c_ref[...]).astype(BF16)
    o_ref[...] = _dot(a, w_ref[...].astype(BF16)) + b_ref[...]


def _mod_call(l, cond, w_mod, b_mod):
    n_l, _, n = w_mod.shape
    tn = 1536
    return pl.pallas_call(
        _mod_kernel,
        grid=(n // tn,),
        in_specs=[_const_spec(cond.shape),
                  pl.BlockSpec((None, D, tn), lambda j: (l, 0, j)),
                  pl.BlockSpec((None, 1, tn), lambda j: (l, 0, j))],
        out_specs=pl.BlockSpec((cond.shape[0], tn), lambda j: (0, j)),
        out_shape=jax.ShapeDtypeStruct((cond.shape[0], n), F32),
        compiler_params=_cparams(("arbitrary",)),
        name="mod",
    )(cond, w_mod, b_mod.reshape(n_l, 1, n))


def _inprep_kernel(latent, *refs):
    (x_ref, mod_ref, gpre_ref, win_ref, gmq_ref, wuq_ref, gmkv_ref, wukv_ref,
     ggq_ref, ggk_ref, bd_ref) = refs[:11]
    refs = refs[11:]
    if latent:
        cos64_ref, sin64_ref, cospe_ref, sinpe_ref = refs[:4]
        refs = refs[4:]
    (qm_ref, kvm_ref, gqo_ref, gkv_ref, dqo_ref, dkv_ref, ret_ref, rg_ref, gate_ref) = refs[:9]
    refs = refs[9:]
    if not latent:
        ckv_o, kpe_o, gk_o, gv_o, dk_o, dv_o = refs

    x = x_ref[...]
    mod = mod_ref[...]
    sh1 = mod[:, 0:D]
    sc1 = mod[:, D:2 * D]
    hb = (_rms(x, gpre_ref[...]) * (1.0 + sc1) + sh1).astype(BF16)

    def z(a, b):
        return _dot_nt(hb, win_ref[a:b, :])

    if latent:
        cos64, sin64 = cos64_ref[...], sin64_ref[...]
        cospe, sinpe = cospe_ref[...], sinpe_ref[...]

    cqn = _rms(z(O_CQ, O_CKV), gmq_ref[...]).astype(BF16)
    q = _dot(cqn, wuq_ref[...]) * ((MLA_NOPE + MLA_ROPE) ** -0.5 * LOG2E)
    q_nope, q_pe = q[:, 0:512], q[:, 512:768]
    if latent:
        q_pe = _rope(q_pe, _tile_lanes(cospe, 2), _tile_lanes(sinpe, 2), MLA_ROPE // 4)
    qm_ref[:, 0:512] = q_nope.astype(BF16)
    qm_ref[:, 512:768] = q_pe.astype(BF16)

    ckvn = _rms(z(O_CKV, O_KPE), gmkv_ref[...])
    kv = _dot(ckvn.astype(BF16), wukv_ref[...])
    kpe4 = _dot_nt(hb, jnp.concatenate([win_ref[O_KPE:O_GQ, :]] * 4, axis=0))
    if latent:
        kpe4 = _rope(kpe4, cospe, sinpe, MLA_ROPE // 4)
    else:
        ckv_o[...] = ckvn
        kpe_o[...] = kpe4[:, 0:MLA_ROPE]
    _store_kvm(kvm_ref, kv, kpe4.astype(BF16))

    gq = _dot_nt(hb, jnp.concatenate(
        [win_ref[O_GQ + h * GQA_HD:O_GQ + (h + 1) * GQA_HD, :] for h in GQA_ORDER], axis=0))
    gq = gq * lax.rsqrt(_seg_meansq(gq, bd_ref, 512) + EPS) * ggq_ref[...]
    gk = z(O_GK, O_GV)
    gk = gk * lax.rsqrt(_seg_meansq(gk, bd_ref, LANES) + EPS) * ggk_ref[...]
    gv = z(O_GV, O_RQ)
    if latent:
        gq = _rope(gq, _tile_lanes(cos64, 4), _tile_lanes(sin64, 4), GQA_HD // 4)
        gk = _rope(gk, cos64, sin64, GQA_HD // 4)
    else:
        gk_o[...] = gk
        gv_o[...] = gv
    gqo_ref[...] = (gq * (GQA_HD ** -0.5 * LOG2E)).astype(BF16)
    gkv_ref[:, 0:LANES] = gk.astype(BF16)
    gkv_ref[:, LANES:2 * LANES] = gv.astype(BF16)
    gkv_ref[:, 2 * LANES:3 * LANES] = jnp.ones(gv.shape, BF16)

    dq = z(O_DQ, O_DK)
    dk = z(O_DK, O_DV)
    dv = z(O_DV, O_GL)
    if latent:
        dq = _rope(dq, _tile_lanes(cos64, 4), _tile_lanes(sin64, 4), DIFF_D // 4)
        dk = _rope(dk, _tile_lanes(cos64, 4), _tile_lanes(sin64, 4), DIFF_D // 4)
    else:
        dk_o[...] = dk
        dv_o[...] = dv
    dqo_ref[...] = (dq * (DIFF_D ** -0.5 * LOG2E)).astype(BF16)
    dkv_ref[:, 0:512] = dk.astype(BF16)
    for h in range(DIFF_HEADS):
        dkv_ref[:, 512 + h * 256:512 + h * 256 + LANES] = dv[:, h * LANES:(h + 1) * LANES].astype(BF16)
        dkv_ref[:, 512 + h * 256 + LANES:512 + (h + 1) * 256] = jnp.ones((dv.shape[0], LANES), BF16)

    ret_ref[:, 0:256] = z(O_RQ, O_RK).astype(BF16)
    ret_ref[:, 256:512] = (z(O_RK, O_RV) * (RET_DK ** -0.5)).astype(BF16)
    ret_ref[:, 512:1024] = z(O_RV, O_RG).astype(BF16)
    rg_ref[...] = z(O_RG, O_DQ).astype(BF16)

    for n in range(N_BRANCH):
        gate_ref[:, n * D:(n + 1) * D] = jax.nn.sigmoid(
            z(O_GL + n * D, O_GL + (n + 1) * D)).astype(BF16)


def _inprep_call(latent, l, x, mod3, mod_row, lw, tabs, t_len):
    n_tok = x.shape[0]
    nblk = n_tok // TM
    blk_per_seq = t_len // TM

    def tok(w):
        return pl.BlockSpec((TM, w), lambda i: (i, 0))

    in_specs = [tok(D),
                pl.BlockSpec((None, 1, 6 * D), lambda i: (mod_row(i * TM), 0, 0)),
                _layer_spec(l, (1, D)),
                pl.BlockSpec((None, O_END, D), lambda i: (l, 0, 0), pipeline_mode=pl.Buffered(1)),
                _layer_spec(l, (1, MLA_Q_LORA)), _layer_spec(l, (MLA_Q_LORA, 768)),
                _layer_spec(l, (1, MLA_KV_LORA)), _layer_spec(l, (MLA_KV_LORA, 1024)),
                _layer_spec(l, (1, 512)), _layer_spec(l, (1, LANES)), _const_spec((512, 512))]
    args = [x, mod3, lw["g_pre1"], lw["w_in"], lw["g_mla_q"], lw["w_uq"], lw["g_mla_kv"],
            lw["w_ukv"], lw["g_gqa_q"], lw["g_gqa_k"], lw["bd"]]
    if latent:
        tab_spec = pl.BlockSpec((TM, LANES), lambda i: (i % blk_per_seq, 0))
        in_specs += [tab_spec] * 4
        args += list(tabs)
    widths = [768, KVM_W, 512, GKV_W, 512, DKV_W, 1024, 512, 4 * D]
    out_specs = [tok(w) for w in widths]
    out_shape = [jax.ShapeDtypeStruct((n_tok, w), BF16) for w in widths]
    if not latent:
        cw = [MLA_KV_LORA, MLA_ROPE, 128, 128, 512, 512]
        out_specs += [tok(w) for w in cw]
        out_shape += [jax.ShapeDtypeStruct((n_tok, w), F32) for w in cw]
    return pl.pallas_call(
        functools.partial(_inprep_kernel, latent),
        grid=(nblk,),
        in_specs=in_specs, out_specs=out_specs, out_shape=out_shape,
        compiler_params=_cparams(("arbitrary",)),
        name="inprep_lat" if latent else "inprep_ctx",
    )(*args)


def _pastkv_kernel(ckv_ref, kpe_ref, wukv_ref, o_ref):
    kv = _dot(ckv_ref[...].astype(BF16), wukv_ref[...])
    _store_kvm(o_ref, kv, kpe_ref[...].astype(BF16))


def _pastkv_call(l, ckv, kpe4, w_ukv):
    n = ckv.shape[0]
    return pl.pallas_call(
        _pastkv_kernel,
        grid=(n // TM,),
        in_specs=[pl.BlockSpec((TM, MLA_KV_LORA), lambda i: (i, 0)),
                  pl.BlockSpec((TM, LANES), lambda i: (i, 0)),
                  _layer_spec(l, (MLA_KV_LORA, 1024))],
        out_specs=pl.BlockSpec((TM, KVM_W), lambda i: (i, 0)),
        out_shape=jax.ShapeDtypeStruct((n, KVM_W), BF16),
        compiler_params=_cparams(("arbitrary",)),
        name="pastkv",
    )(ckv, kpe4, w_ukv)


def _softmax_pv(s, v_ones):
    m = jnp.max(s, axis=-1, keepdims=True)
    p = jnp.exp2(s - m).astype(BF16)
    o = _dot(p, v_ones)
    return o[:, 0:LANES] / o[:, LANES:2 * LANES]


def _attn_kernel(lam_init, n_past, qm_ref, kvm_ref, gq_ref, gkv_ref, dq_ref, dkv_ref, *refs):
    if n_past:
        past_refs, refs = refs[:3], refs[3:]
        lam_ref, gdiff_ref, o_ref = refs[:3]
        joined = refs[3:]

        @pl.when(pl.program_id(1) == 0)
        def _():
            for dst, past, new in zip(joined, past_refs, (kvm_ref, gkv_ref, dkv_ref)):
                dst[0:n_past, :] = past[...]
                dst[n_past:, :] = new[...]

        kvm_ref, gkv_ref, dkv_ref = joined
    else:
        lam_ref, gdiff_ref, o_ref = refs
    tq = qm_ref.shape[0]
    lane = _lane_iota((tq, LANES))
    low = lane < HALF_LANES
    zero = jnp.zeros((tq, LANES), BF16)

    for p in range(MLA_HEADS // 2):
        qn = qm_ref[:, p * LANES:(p + 1) * LANES]
        g = p // 2
        qpe = qm_ref[:, 512 + g * LANES:512 + (g + 1) * LANES]
        kk = kvm_ref[:, p * 256:(p + 1) * 256]
        vv = kvm_ref[:, 1024 + p * 256:1024 + (p + 1) * 256]
        outs = []
        for half in range(2):
            h = 2 * p + half
            slot = h % 4
            in_slot = (lane >= slot * MLA_ROPE) & (lane < (slot + 1) * MLA_ROPE)
            lhs = jnp.concatenate(
                [jnp.where(low if half == 0 else ~low, qn, zero),
                 jnp.where(in_slot, qpe, zero)], axis=1)
            outs.append(_softmax_pv(_dot_nt(lhs, kk), vv))
        o_ref[:, p * LANES:(p + 1) * LANES] = jnp.where(low, outs[0], outs[1]).astype(BF16)

    kk = gkv_ref[:, 0:LANES]
    vv = gkv_ref[:, LANES:3 * LANES]
    for g in range(GQA_HEADS // 2):
        qg = gq_ref[:, g * LANES:(g + 1) * LANES]
        o_lo = _softmax_pv(_dot_nt(jnp.where(low, qg, zero), kk), vv)
        o_hi = _softmax_pv(_dot_nt(jnp.where(low, zero, qg), kk), vv)
        o_ref[:, 512 + g * LANES:512 + (g + 1) * LANES] = jnp.where(low, o_lo, o_hi).astype(BF16)

    lp = lam_ref[...]
    lam = (jnp.exp(jnp.sum(lp[0:1] * lp[1:2], axis=-1, keepdims=True))
           - jnp.exp(jnp.sum(lp[2:3] * lp[3:4], axis=-1, keepdims=True)) + lam_init)
    for h in range(DIFF_HEADS):
        qh = dq_ref[:, h * LANES:(h + 1) * LANES]
        kk = dkv_ref[:, h * LANES:(h + 1) * LANES]
        vv = dkv_ref[:, 512 + h * 256:512 + (h + 1) * 256]
        a1 =_softmax_pv(_dot_nt(jnp.where(low, qh, zero), kk), vv)
        a2 = _softmax_pv(_dot_nt(jnp.where(low, zero, qh), kk), vv)
        od = _rms(a1 - lam * a2, gdiff_ref[...]) * (1.0 - lam_init)
        o_ref[:, 1024 + h * LANES:1024 + (h + 1) * LANES] = od.astype(BF16)


def _attn_call(l, lam_init, qm, kvm, gq, gkv, dq, dkv, lam_p, g_diff, n_b, t_len, past=None):
    nq = t_len // TQ
    n_past = 0 if past is None else past[0].shape[0] // n_b

    def qspec(w):
        return pl.BlockSpec((TQ, w), lambda b, i: (b * nq + i, 0))

    def kspec(w, rows=t_len):
        return pl.BlockSpec((rows, w), lambda b, i: (b, 0))

    in_specs = [qspec(768), kspec(KVM_W), qspec(512), kspec(GKV_W), qspec(512), kspec(DKV_W)]
    args = [qm, kvm, gq, gkv, dq, dkv]
    scratch = []
    if n_past:
        in_specs += [kspec(KVM_W, n_past), kspec(GKV_W, n_past), kspec(DKV_W, n_past)]
        args += list(past)
        scratch = [pltpu.VMEM((n_past + t_len, w), BF16) for w in (KVM_W, GKV_W, DKV_W)]
    in_specs += [_layer_spec(l, (4, DIFF_D)), _layer_spec(l, (1, DIFF_DV))]
    args += [lam_p, g_diff]
    return pl.pallas_call(
        functools.partial(_attn_kernel, lam_init, n_past),
        grid=(n_b, nq),
        in_specs=in_specs,
        out_specs=qspec(3 * BRANCH_W),
        out_shape=jax.ShapeDtypeStruct((n_b * t_len, 3 * BRANCH_W), BF16),
        scratch_shapes=scratch,
        compiler_params=_cparams(("arbitrary", "arbitrary")),
        name="attn",
    )(*args)


def _log_sigmoid(x):
    return jnp.minimum(x, 0.0) - jnp.log(1.0 + jnp.exp(-jnp.abs(x)))


def _log_gamma(dec_ref, l, d, h):
    return _log_sigmoid(jnp.full((1, 1), dec_ref[l, d, h], F32))


def _ret_kernel(latent, l, t_len, dec_ref, q_ref, k_ref, v_ref, rg_ref, gret_ref, *refs):
    if latent:
        s0_ref, o_ref = refs
    else:
        o_ref, st_ref = refs
    tq = q_ref.shape[0]
    t0 = pl.program_id(1) * tq
    lane = _lane_iota((tq, LANES))
    low = lane < HALF_LANES
    zero = jnp.zeros((tq, LANES), BF16)
    t_idx = (t0 + lax.broadcasted_iota(jnp.int32, (tq, t_len), 0)).astype(F32)
    s_idx = lax.broadcasted_iota(jnp.int32, (tq, t_len), 1).astype(F32)
    dist = t_idx - s_idx
    past = dist >= 0
    diag = jnp.where(dist == 0, 1.0, 0.0)
    t_col = (t0 + lax.broadcasted_iota(jnp.int32, (tq, 1), 0)).astype(F32)

    def lg(d, h):
        return _log_gamma(dec_ref, l, d, h)

    for h in range(RET_HEADS):
        p, half = h // 2, h % 2
        qp = q_ref[:, p * LANES:(p + 1) * LANES]
        qm = jnp.where(low if half == 0 else ~low, qp, zero)
        kp = k_ref[:, p * LANES:(p + 1) * LANES]
        vh = v_ref[:, h * LANES:(h + 1) * LANES]
        lgf, lgb = lg(0, h), lg(1, h)
        dmask = jnp.exp(jnp.where(past, lgf, -lgb) * dist) + diag
        o = _dot((_dot_nt(qm, kp) * dmask).astype(BF16), vh)
        if latent:
            sf =s0_ref[0, p].astype(BF16)
            sb = s0_ref[1, p].astype(BF16)
            o = o + _dot(qm, sf) * jnp.exp(lgf * (t_col + 1.0))
            o = o + _dot(qm, sb) * jnp.exp(lgb * (float(t_len) - t_col))
        mu = jnp.mean(o, axis=-1, keepdims=True)
        oc = o - mu
        y = oc * lax.rsqrt(jnp.mean(oc * oc, axis=-1, keepdims=True) + EPS)
        y = y * gret_ref[:, h * LANES:(h + 1) * LANES]
        rg = rg_ref[:, h * LANES:(h + 1) * LANES].astype(F32)
        o_ref[:, h * LANES:(h + 1) * LANES] = (y * _silu(rg)).astype(BF16)

    if not latent:
        s_col = lax.broadcasted_iota(jnp.int32, (t_len, 1), 0).astype(F32)
        lane_t = _lane_iota((1, LANES)) < HALF_LANES
        for p in range(RET_HEADS // 2):
            kp = k_ref[:, p * LANES:(p + 1) * LANES].astype(F32)
            for d in range(2):
                lg_lane = jnp.where(lane_t, lg(d, 2 * p), lg(d, 2 * p + 1))
                expo = (float(t_len) - 1.0 - s_col) if d == 0 else s_col
                kdec_t = jnp.transpose(kp * jnp.exp(lg_lane * expo)).astype(BF16)
                for half in range(2):
                    h = 2 * p + half
                    st = _dot(kdec_t, v_ref[:, h * LANES:(h + 1) * LANES])
                    st_ref[d, h] = st[half * RET_DK:(half + 1) * RET_DK, :]


def _ret_call(latent, l, dec, ret, rg, g_ret, s0, n_b, t_len):
    nq = t_len // TQ
    assert latent or nq == 1
    in_specs = [pl.BlockSpec(memory_space=pltpu.SMEM),
                pl.BlockSpec((TQ, 256), lambda b, i: (b * nq + i, 0)),
                pl.BlockSpec((t_len, 256), lambda b, i: (b, 1)),
                pl.BlockSpec((t_len, 512), lambda b, i: (b, 1)),
                pl.BlockSpec((TQ, 512), lambda b, i: (b * nq + i, 0)),
                _layer_spec(l, (1, 512))]
    args = [dec, ret, ret, ret, rg, g_ret]
    out_specs = [pl.BlockSpec((TQ, 512), lambda b, i: (b * nq + i, 0))]
    out_shape = [jax.ShapeDtypeStruct((n_b * t_len, 512), BF16)]
    if latent:
        in_specs.append(pl.BlockSpec((None, 2, 2, LANES, LANES), lambda b, i: (b, 0, 0, 0, 0)))
        args.append(s0)
    else:
        out_specs.append(pl.BlockSpec((None, 2, RET_HEADS, RET_DK, RET_DV),
                                      lambda b, i: (b, 0, 0, 0, 0)))
        out_shape.append(jax.ShapeDtypeStruct((n_b, 2, RET_HEADS, RET_DK, RET_DV), F32))
    return pl.pallas_call(
        functools.partial(_ret_kernel, latent, l, t_len),
        grid=(n_b, nq),
        in_specs=in_specs, out_specs=out_specs, out_shape=out_shape,
        compiler_params=_cparams(("arbitrary", "arbitrary")),
        name="ret_lat" if latent else "ret_ctx",
    )(*args)


def _merge_kernel(x_ref, mod_ref, br_ref, or_ref, gate_ref, wbr_ref, wout_ref, gpost_ref, o_ref):
    merged = None
    for n in range(N_BRANCH):
        if n < 2:
            b = br_ref[:, n * BRANCH_W:(n + 1) * BRANCH_W]
        elif n == 2:
            b = or_ref[...]
        else:
            b = br_ref[:, 2 * BRANCH_W:3 * BRANCH_W]
        t = gate_ref[:, n * D:(n + 1) * D].astype(F32) * _dot(b, wbr_ref[n])
        merged = t if merged is None else merged + t
    out = _dot(merged.astype(BF16), wout_ref[...])
    g1 = mod_ref[...][:, 2 * D:3 * D]
    o_ref[...] = x_ref[...] + g1 * _rms(out, gpost_ref[...])


def _merge_call(l, x, mod3, mod_row, br, o_r, gates, lw):
    n_tok = x.shape[0]

    def tok(w):
        return pl.BlockSpec((TM_MERGE, w), lambda i: (i, 0))

    return pl.pallas_call(
        _merge_kernel,
        grid=(n_tok // TM_MERGE,),
        in_specs=[tok(D), pl.BlockSpec((None, 1, 6 * D), lambda i: (mod_row(i * TM_MERGE), 0, 0)),
                  tok(3 * BRANCH_W), tok(BRANCH_W), tok(4 * D),
                  _layer_spec(l, (N_BRANCH, BRANCH_W, D)), _layer_spec(l, (D, D)),
                  _layer_spec(l, (1, D))],
        out_specs=tok(D),
        out_shape=jax.ShapeDtypeStruct((n_tok, D), F32),
        compiler_params=_cparams(("arbitrary",)),
        name="merge",
    )(x, mod3, br, o_r, gates, lw["w_br"], lw["w_out"], lw["g_post1"])


def _route(logits_t, bias):
    n = logits_t.shape[1]
    scores = jax.nn.sigmoid(logits_t)
    sel = scores + bias
    neg = -jnp.inf
    sub = lax.broadcasted_iota(jnp.int32, (GROUP_SIZE, n), 0)
    grp = []
    for g in range(N_GROUPS):
        blk = sel[g * GROUP_SIZE:(g + 1) * GROUP_SIZE]
        m1 = jnp.max(blk, axis=0, keepdims=True)
        i1 = jnp.min(jnp.where(blk == m1, sub, GROUP_SIZE), axis=0, keepdims=True)
        m2 = jnp.max(jnp.where(sub == i1, neg, blk), axis=0, keepdims=True)
        grp.append(m1 + m2)
    parts = []
    for g in range(N_GROUPS):
        beaten = jnp.zeros((1, n), jnp.int32)
        for o in range(N_GROUPS):
            if o == g:
                continue
            wins = (grp[o] > grp[g]) | (grp[o] == grp[g]) if o < g else (grp[o] > grp[g])
            beaten = beaten + wins.astype(jnp.int32)
        keep = beaten < TOPK_GROUPS
        parts.append(jnp.where(keep, sel[g * GROUP_SIZE:(g + 1) * GROUP_SIZE], neg))
    cur = jnp.concatenate(parts, axis=0)
    eidx = lax.broadcasted_iota(jnp.int32, (N_EXPERTS, n), 0)
    hits, ids, ws = [], [], []
    for _ in range(TOP_K):
        m = jnp.max(cur, axis=0, keepdims=True)
        i = jnp.min(jnp.where(cur == m, eidx, N_EXPERTS), axis=0, keepdims=True)
        hit = eidx == i
        hits.append(hit)
        ids.append(i)
        ws.append(jnp.sum(jnp.where(hit, scores, 0.0), axis=0, keepdims=True))
        cur = jnp.where(hit, neg, cur)
    wsum = ws[0] + ws[1] + ws[2] + ws[3]
    return hits, ids, [w / wsum * ROUTE_SCALE for w in ws]


U32 = jnp.uint32
HIGH16 = np.uint32(0xFFFF0000)


def _bf16_bits(v):
    return lax.bitcast_convert_type(v.astype(BF16).astype(F32), U32)


def _pack_rows(v):
    return (_bf16_bits(v[:, 0:D // 2]) >> 16) | _bf16_bits(v[:, D // 2:D])


def _unpack_rows(p):
    lo = lax.bitcast_convert_type(p << 16, F32)
    hi = lax.bitcast_convert_type(p & HIGH16, F32)
    return jnp.concatenate([lo, hi], axis=1)


def _moe_pre_kernel(x_ref, mod_ref, gpre_ref, wr_ref, br_ref, tri_ref,
                    hp_ref, eidx_ref, rank_ref, comb_ref, cnt_ref, run_ref):
    tm = x_ref.shape[0]

    @pl.when(pl.program_id(0) == 0)
    def _():
        run_ref[...] = jnp.zeros_like(run_ref)

    mod = mod_ref[...]
    sh2, sc2 = mod[:, 3 * D:4 * D], mod[:, 4 * D:5 * D]
    h = _rms(x_ref[...], gpre_ref[...]) * (1.0 + sc2) + sh2
    hp_ref[...] = _pack_rows(h)
    hb = h.astype(BF16)
    h_lo = (h - hb.astype(F32)).astype(BF16)
    wr = wr_ref[...]
    wr_hi = wr.astype(BF16)
    wr_lo = (wr - wr_hi.astype(F32)).astype(BF16)
    logits_t = _dot_nt(wr_hi, hb) + _dot_nt(wr_hi, h_lo) + _dot_nt(wr_lo, hb)
    hits, ids, ws = _route(logits_t, br_ref[...])

    picked = jnp.zeros((N_EXPERTS, tm), F32)
    for hit in hits:
        picked = jnp.where(hit, 1.0, picked)
    before = _dot(picked.astype(BF16), tri_ref[...]) + run_ref[:, 0:1]
    sub8 = lax.broadcasted_iota(jnp.int32, (8, tm), 0)
    comb8 = jnp.zeros((8, tm), F32)
    for k in range(TOP_K):
        rank = jnp.sum(jnp.where(hits[k], before, 0.0), axis=0, keepdims=True)
        eidx_ref[k:k + 1, :] = ids[k]
        rank_ref[k:k + 1, :] = rank.astype(jnp.int32)
        comb8 = jnp.where(sub8 == k, ws[k], comb8)
    comb_ref[...] = jnp.transpose(
        jnp.concatenate([comb8, jnp.zeros((LANES - 8, tm), F32)], axis=0))
    run_ref[...] = run_ref[...] + jnp.sum(picked, axis=1, keepdims=True)
    cnt_ref[...] = run_ref[...]


def _moe_pre_call(l, x, mod3, mod_row, lw):
    n_tok = x.shape[0]
    tm = TM_MOE_PRE
    tri = np.arange(tm)
    tri = jnp.asarray(tri[:, None] < tri[None, :], BF16)
    row4 = pl.BlockSpec((TOP_K, tm), lambda i: (0, i))
    return pl.pallas_call(
        _moe_pre_kernel,
        grid=(n_tok // tm,),
        in_specs=[pl.BlockSpec((tm, D), lambda i: (i, 0)),
                  pl.BlockSpec((None, 1, 6 * D), lambda i: (mod_row(i * tm), 0, 0)),
                  _layer_spec(l, (1, D)), _layer_spec(l, (N_EXPERTS, D)),
                  _layer_spec(l, (N_EXPERTS, 1)), _const_spec((tm, tm))],
        out_specs=[pl.BlockSpec((tm, D // 2), lambda i: (i, 0)), row4, row4,
                   pl.BlockSpec((tm, LANES), lambda i: (i, 0)),
                   _const_spec((N_EXPERTS, LANES))],
        out_shape=[jax.ShapeDtypeStruct((n_tok, D // 2), U32),
                   jax.ShapeDtypeStruct((TOP_K, n_tok), jnp.int32),
                   jax.ShapeDtypeStruct((TOP_K, n_tok), jnp.int32),
                   jax.ShapeDtypeStruct((n_tok, LANES), F32),
                   jax.ShapeDtypeStruct((N_EXPERTS, LANES), F32)],
        scratch_shapes=[pltpu.VMEM((N_EXPERTS, LANES), F32)],
        compiler_params=_cparams(("arbitrary",)),
        name="moe_pre",
    )(x, mod3, lw["g_pre2"], lw["w_router_t"], lw["b_router"], tri)


def _moe_plan_kernel(eidx_ref, rank_ref, cnt_ref, dest_ref, te_ref, tv_ref, tn_ref):
    tm = eidx_ref.shape[1]
    cnt = cnt_ref[...]
    padded = jnp.ceil(cnt * (1.0 / TMX)) * TMX
    row = lax.broadcasted_iota(jnp.int32, cnt.shape, 0)
    incl = padded
    shift = 1
    while shift < N_EXPERTS:
        incl = incl + jnp.where(row >= shift, pltpu.roll(incl, shift, 0), 0.0)
        shift *= 2
    start = (incl - padded)[:, 0:1]
    end = incl[:, 0:1]
    erow = lax.broadcasted_iota(jnp.int32, (N_EXPERTS, tm), 0)
    for k in range(TOP_K):
        mine = erow == eidx_ref[k:k + 1, :]
        base = jnp.sum(jnp.where(mine, start, 0.0), axis=0, keepdims=True)
        dest_ref[k:k + 1, :] = rank_ref[k:k + 1, :] + base.astype(jnp.int32)

    @pl.when(pl.program_id(0) == 0)
    def _():
        tile0 = (_lane_iota((1, LANES)) * TMX).astype(F32)
        owner = jnp.sum(jnp.where(end <= tile0, 1.0, 0.0), axis=0, keepdims=True)
        owner = jnp.minimum(owner, N_EXPERTS - 1.0)
        erow_t = lax.broadcasted_iota(jnp.int32, (N_EXPERTS, LANES), 0).astype(F32)
        left = jnp.sum(jnp.where(erow_t == owner, cnt[:, 0:1] - (tile0 - start), 0.0),
                       axis=0, keepdims=True)
        te_ref[...] = owner.astype(jnp.int32)
        tv_ref[...] = jnp.clip(left, 0.0, float(TMX)).astype(jnp.int32)
        tn_ref[...] = jnp.full(tn_ref.shape, N_EXPERTS, jnp.int32)
        nxt = owner
        for k in range(W_SLOTS - 1):
            later = (erow_t > nxt) & (cnt[:, 0:1] > 0.0)
            nxt = jnp.min(jnp.where(later, erow_t, float(N_EXPERTS)), axis=0, keepdims=True)
            tn_ref[k:k + 1, :] = nxt.astype(jnp.int32)
        n_used = jnp.sum(jnp.where(left > 0.0, 1.0, 0.0), axis=1, keepdims=True)
        tn_ref[7:8, :] = jnp.minimum(tile0 * (1.0 / TMX), n_used - 1.0).astype(jnp.int32)


def _moe_plan_call(eidx, rank, cnt):
    n_tok = eidx.shape[1]
    tm = TM_MOE_PRE
    row4 = pl.BlockSpec((TOP_K, tm), lambda i: (0, i))
    tiles = jax.ShapeDtypeStruct((1, LANES), jnp.int32)
    return pl.pallas_call(
        _moe_plan_kernel,
        grid=(n_tok // tm,),
        in_specs=[row4, row4, _const_spec((N_EXPERTS, LANES))],
        out_specs=[row4, _const_spec((1, LANES)), _const_spec((1, LANES)), _const_spec((8, LANES))],
        out_shape=[jax.ShapeDtypeStruct((TOP_K, n_tok), jnp.int32), tiles, tiles,
                   jax.ShapeDtypeStruct((8, LANES), jnp.int32)],
        compiler_params=_cparams(("arbitrary",)),
        name="moe_plan",
    )(eidx, rank, cnt)


def _experts_kernel(l, te_ref, tv_ref, tn_ref, xs_ref, wgu_hbm, wdn_hbm, ys_ref,
                    wgu_f, wdn_f, wgu_b, wdn_b, sem, group_ref):
    j = pl.program_id(0)
    valid = tv_ref[j]
    expert = te_ref[j]

    def fetch(e, slot):
        return (pltpu.make_async_copy(wgu_hbm.at[l, e], wgu_f.at[slot], sem.at[slot, 0]),
                pltpu.make_async_copy(wdn_hbm.at[l, e], wdn_f.at[slot], sem.at[slot, 1]))

    def start_if_any(e, slot):
        @pl.when(e < N_EXPERTS)
        def _():
            for cp in fetch(e, slot):
                cp.start()

    @pl.when(j == 0)
    def _():
        group_ref[0] = 0
        start_if_any(expert, 0)
        for k in range(W_SLOTS - 2):
            start_if_any(tn_ref[k, 0], k + 1)

    first_tile = (j == 0) | (expert != te_ref[jnp.maximum(j - 1, 0)])

    @pl.when(first_tile & (valid > 0))
    def _():
        group = group_ref[0]
        slot = lax.rem(group, W_SLOTS)
        for cp in fetch(expert, slot):
            cp.wait()
        wgu_b[...] = wgu_f[slot].astype(BF16)
        wdn_b[...] = wdn_f[slot].astype(BF16)
        start_if_any(tn_ref[W_SLOTS - 2, j], lax.rem(group + W_SLOTS - 1, W_SLOTS))
        group_ref[0] = group + 1

    @pl.when(valid > 0)
    def _():
        rows = lax.broadcasted_iota(jnp.int32, (TMX, D), 0)
        x = jnp.where(rows < valid, _unpack_rows(xs_ref[...]), 0.0).astype(BF16)
        gu = _dot(x, wgu_b[...])
        a = _silu(gu[:, 0:EXPERT_FF]) * gu[:, EXPERT_FF:2 * EXPERT_FF]
        ys_ref[...] = _pack_rows(_dot(a.astype(BF16), wdn_b[...]))


def _experts_call(l, xs, te, tv, tn, w_gu, w_dn):
    n_tiles = xs.shape[0] // TMX
    grid_spec = pltpu.PrefetchScalarGridSpec(
        num_scalar_prefetch=3,
        grid=(n_tiles,),
        in_specs=[pl.BlockSpec((TMX, D // 2), lambda j, te, tv, tn: (tn[7, j], 0)),
                  pl.BlockSpec(memory_space=pl.ANY), pl.BlockSpec(memory_space=pl.ANY)],
        out_specs=pl.BlockSpec((TMX, D // 2), lambda j, te, tv, tn: (tn[7, j], 0)),
        scratch_shapes=[pltpu.VMEM((W_SLOTS, D, 2 * EXPERT_FF), F32),
                        pltpu.VMEM((W_SLOTS, EXPERT_FF, D), F32),
                        pltpu.VMEM((D, 2 * EXPERT_FF), BF16), pltpu.VMEM((EXPERT_FF, D), BF16),
                        pltpu.SemaphoreType.DMA((W_SLOTS, 2)), pltpu.SMEM((1,), jnp.int32)])
    return pl.pallas_call(
        functools.partial(_experts_kernel, l),
        grid_spec=grid_spec,
        out_shape=jax.ShapeDtypeStruct(xs.shape, U32),
        compiler_params=_cparams(("arbitrary",)),
        name="moe_experts",
    )(te, tv, tn, xs, w_gu, w_dn)


def _moe_post_kernel(x_ref, mod_ref, hp_ref, yg_ref, comb_ref, wsgu_ref, wsdn_ref, gpost_ref,
                     o_ref):
    hb = _unpack_rows(hp_ref[...]).astype(BF16)
    sgu = _dot(hb, wsgu_ref[...])
    sa = _silu(sgu[:, 0:SHARED_FF]) * sgu[:, SHARED_FF:2 * SHARED_FF]
    acc = _dot(sa.astype(BF16), wsdn_ref[...])
    comb = comb_ref[...]
    for k in range(TOP_K):
        acc = acc + comb[:, k:k + 1] * _unpack_rows(yg_ref[k])
    g2 = mod_ref[...][:, 5 * D:6 * D]
    o_ref[...] = x_ref[...] + g2 * _rms(acc, gpost_ref[...])


def _moe_post_call(l, x, mod3, mod_row, hp, yg, comb, lw):
    n_tok = x.shape[0]
    tm = TM_MOE_PRE
    return pl.pallas_call(
        _moe_post_kernel,
        grid=(n_tok // tm,),
        in_specs=[pl.BlockSpec((tm, D), lambda i: (i, 0)),
                  pl.BlockSpec((None, 1, 6 * D), lambda i: (mod_row(i * tm), 0, 0)),
                  pl.BlockSpec((tm, D // 2), lambda i: (i, 0)),
                  pl.BlockSpec((TOP_K, tm, D // 2), lambda i: (0, i, 0)),
                  pl.BlockSpec((tm, LANES), lambda i: (i, 0)),
                  _layer_spec(l, (D, 2 * SHARED_FF)), _layer_spec(l, (SHARED_FF, D)),
                  _layer_spec(l, (1, D))],
        out_specs=pl.BlockSpec((tm, D), lambda i: (i, 0)),
        out_shape=jax.ShapeDtypeStruct((n_tok, D), F32),
        compiler_params=_cparams(("arbitrary",)),
        name="moe_post",
    )(x, mod3, hp, yg, comb, lw["w_sh_gu"], lw["w_sh_down"], lw["g_post2"])


def _moe_call(l, x, mod3, mod_row, lw):
    n_tok = x.shape[0]
    n_slots = -(-(TOP_K * n_tok + N_EXPERTS * (TMX - 1)) // TMX) * TMX
    assert n_slots // TMX <= LANES
    hp, eidx, rank, comb, cnt = _moe_pre_call(l, x, mod3, mod_row, lw)
    dest, te, tv, tn = _moe_plan_call(eidx, rank, cnt)
    dest = dest.reshape(TOP_K * n_tok)
    xs = _sc_scatter_rows(hp, dest, n_slots)
    ys = _experts_call(l, xs, te[0], tv[0], tn, lw["w_exp_gu"], lw["w_exp_down"])
    yg = _sc_gather_rows(ys, dest).reshape(TOP_K, n_tok, D // 2)
    return _moe_post_call(l, x, mod3, mod_row, hp, yg, comb, lw)


SC_CORES, SC_SUBCORES = 2, 16
SC_WORKERS = SC_CORES * SC_SUBCORES


def _sc_gather_rows(table, idx, chunk=64):
    n_out, width = idx.shape[0], table.shape[1]
    per_worker = n_out // SC_WORKERS
    n_chunks = per_worker // chunk
    assert per_worker * SC_WORKERS == n_out and n_chunks * chunk == per_worker
    mesh = plsc.VectorSubcoreMesh(core_axis_name="c", subcore_axis_name="s",
                                  num_cores=SC_CORES, num_subcores=SC_SUBCORES)

    @functools.partial(
        pl.kernel, mesh=mesh,
        out_type=jax.ShapeDtypeStruct((n_out, width), table.dtype),
        scratch_types=[pltpu.VMEM((chunk,), jnp.int32), pltpu.VMEM((chunk, width), table.dtype),
                       pltpu.SemaphoreType.DMA],
        name="sc_gather")
    def gather(table_hbm, idx_hbm, out_hbm, idx_v, rows_v, sem):
        base = (lax.axis_index("s") * SC_CORES + lax.axis_index("c")) * per_worker

        @pl.loop(0, n_chunks)
        def _(j):
            off = base + j * chunk
            pltpu.sync_copy(idx_hbm.at[pl.ds(off, chunk)], idx_v)
            pltpu.async_copy(table_hbm.at[idx_v], rows_v, sem).wait()
            pltpu.sync_copy(rows_v, out_hbm.at[pl.ds(off, chunk)])

    return gather(table, idx)


def _sc_scatter_rows(rows, dest, n_slots, chunk=64):
    n_tok, width = rows.shape
    per_worker = n_tok // SC_WORKERS
    n_chunks = per_worker // chunk
    assert per_worker * SC_WORKERS == n_tok and n_chunks * chunk == per_worker
    mesh = plsc.VectorSubcoreMesh(core_axis_name="c", subcore_axis_name="s",
                                  num_cores=SC_CORES, num_subcores=SC_SUBCORES)

    @functools.partial(
        pl.kernel, mesh=mesh,
        out_type=jax.ShapeDtypeStruct((n_slots, width), rows.dtype),
        scratch_types=[pltpu.VMEM((chunk,), jnp.int32), pltpu.VMEM((chunk, width), rows.dtype)],
        name="sc_scatter")
    def scatter(rows_hbm, dest_hbm, out_hbm, idx_v, rows_v):
        base = (lax.axis_index("s") * SC_CORES + lax.axis_index("c")) * per_worker

        @pl.loop(0, n_chunks)
        def _(j):
            off = base + j * chunk
            pltpu.sync_copy(rows_hbm.at[pl.ds(off, chunk)], rows_v)
            for k in range(TOP_K):
                pltpu.sync_copy(dest_hbm.at[pl.ds(k * n_tok + off, chunk)], idx_v)
                pltpu.sync_copy(rows_v, out_hbm.at[idx_v])

    return scatter(rows, dest)


def _rope_tables(t_len):
    pos = np.arange(t_len)
    row, col = pos // GRID_W, pos % GRID_W

    def tab(r):
        half = r // 2
        freq = ROPE_BASE ** (-np.arange(half, dtype=np.float64) / half)
        sign = np.concatenate([-np.ones(half), np.ones(half)])
        cs, sn = [], []
        for p in (row, col):
            ang = p[:, None].astype(np.float64) * freq[None, :]
            cs.append(np.concatenate([np.cos(ang), np.cos(ang)], axis=1))
            sn.append(np.concatenate([np.sin(ang), np.sin(ang)], axis=1) * sign[None, :])
        return np.concatenate(cs, axis=1), np.concatenate(sn, axis=1)

    c64, s64 = tab(GQA_HD // 2)
    cpe, spe = tab(MLA_ROPE // 2)
    out = (np.tile(c64, (1, 2)), np.tile(s64, (1, 2)), np.tile(cpe, (1, 4)), np.tile(spe, (1, 4)))
    return tuple(jnp.asarray(a, F32) for a in out)


def _prep_weights(p):
    n_l = p["w_in"].shape[0]

    def row(name):
        return p[name].reshape(n_l, 1, -1)

    w_uq = p["w_mla_uq"].reshape(n_l, MLA_Q_LORA, MLA_HEADS, MLA_NOPE + MLA_ROPE)
    w_uq = jnp.concatenate([w_uq[..., :MLA_NOPE].reshape(n_l, MLA_Q_LORA, -1),
                            w_uq[..., MLA_NOPE:].reshape(n_l, MLA_Q_LORA, -1)], axis=-1)
    w_ukv = p["w_mla_ukv"].reshape(n_l, MLA_KV_LORA, MLA_HEADS, MLA_NOPE + MLA_V)
    w_ukv = jnp.concatenate([w_ukv[..., :MLA_NOPE].reshape(n_l, MLA_KV_LORA, -1),
                             w_ukv[..., MLA_NOPE:].reshape(n_l, MLA_KV_LORA, -1)], axis=-1)
    w_br = p["w_br"]
    w_br_gqa = w_br[:, 1].reshape(n_l, GQA_HEADS, GQA_HD, D)[:, jnp.array(GQA_ORDER)]
    w_br = jnp.concatenate([w_br[:, 0:1], w_br_gqa.reshape(n_l, 1, BRANCH_W, D), w_br[:, 2:4]], axis=1)
    blk = np.arange(512) // GQA_HD
    return {
        "g_pre1": row("g_pre1"), "g_post1": row("g_post1"),
        "g_pre2": row("g_pre2"), "g_post2": row("g_post2"),
        "w_in": jnp.swapaxes(p["w_in"], 1, 2).astype(BF16),
        "g_mla_q": row("g_mla_q"), "w_uq": w_uq.astype(BF16),
        "g_mla_kv": row("g_mla_kv"), "w_ukv": w_ukv.astype(BF16),
        "g_gqa_q": jnp.tile(p["g_gqa_q"], (1, GQA_HEADS)).reshape(n_l, 1, -1),
        "g_gqa_k": jnp.tile(p["g_gqa_k"], (1, GQA_KV_HEADS)).reshape(n_l, 1, -1),
        "bd": jnp.asarray(blk[:, None] == blk[None, :], BF16),
        "ret_decay": p["ret_decay"],
        "g_ret": row("g_ret"),
        "diff_lambda": p["diff_lambda"], "g_diff": row("g_diff"),
        "w_br": w_br.astype(BF16), "w_out": p["w_out"].astype(BF16),
        "w_router_t": jnp.swapaxes(p["w_router"], 1, 2),
        "b_router": p["b_router"].reshape(n_l, N_EXPERTS, 1),
        "w_exp_gu": p["w_exp_gu"], "w_exp_down": p["w_exp_down"],
        "w_sh_gu": p["w_sh_gu"].astype(BF16), "w_sh_down": p["w_sh_down"].astype(BF16),
    }


def _mixers(latent, l, x, mod3, mod_row, lw, n_b, t_len, tabs=None, past=None, s0=None):
    lam_init = 0.8 - 0.6 * math.exp(-0.3 * l)
    outs = _inprep_call(latent, l, x, mod3, mod_row, lw, tabs, t_len)
    qm, kvm, gq, gkv, dq, dkv, ret, rg, gates = outs[:9]
    br = _attn_call(l, lam_init, qm, kvm, gq, gkv, dq, dkv, lw["diff_lambda"], lw["g_diff"],
                    n_b, t_len, past)
    r = _ret_call(latent, l, lw["ret_decay"], ret, rg, lw["g_ret"], s0, n_b, t_len)
    y = _merge_call(l, x, mod3, mod_row, br, r[0], gates, lw)
    cache = None if latent else tuple(outs[9:]) + (r[1],)
    return y, cache


def kernel(x_prompt, x_sample, cache_mla_ckv, cache_mla_kpe, cache_gqa_k, cache_gqa_v, cache_diff_k, cache_diff_v, state_ret, c, c_ctx, w_mod, b_mod, g_pre1, g_post1, g_pre2, g_post2, w_in, g_mla_q, w_mla_uq, g_mla_kv, w_mla_ukv, g_gqa_q, g_gqa_k, ret_decay, g_ret, diff_lambda, g_diff, w_br, w_out, w_router, b_router, w_exp_gu, w_exp_down, w_sh_gu, w_sh_down):
    params = dict(w_in=w_in, g_pre1=g_pre1, g_post1=g_post1, g_pre2=g_pre2,
                  g_post2=g_post2, g_mla_q=g_mla_q, w_mla_uq=w_mla_uq,
                  g_mla_kv=g_mla_kv, w_mla_ukv=w_mla_ukv, g_gqa_q=g_gqa_q, g_gqa_k=g_gqa_k,
                  ret_decay=ret_decay, g_ret=g_ret, diff_lambda=diff_lambda, g_diff=g_diff,
                  w_br=w_br, w_out=w_out, w_router=w_router, b_router=b_router,
                  w_exp_gu=w_exp_gu, w_exp_down=w_exp_down, w_sh_gu=w_sh_gu, w_sh_down=w_sh_down)
    n_bc, t_c, _ = x_prompt.shape
    n_bl, t_l, _ = x_sample.shape
    p_len = cache_mla_ckv.shape[2]
    tabs = _rope_tables(t_l)
    n_cond = 8
    cond = jnp.concatenate([c_ctx[None, :], c, jnp.zeros((n_cond - 1 - n_bl, D), F32)], axis=0)
    assert t_l % TM_MERGE == 0 and (t_c * n_bc) % TM_MERGE == 0
    assert t_l % TM_MOE_PRE == 0 and (t_c * n_bc) % TM_MOE_PRE == 0

    yp = x_prompt.reshape(n_bc * t_c, D)
    ys = x_sample.reshape(n_bl * t_l, D)
    caches = []
    lw = _prep_weights(params)
    for l in range(DEPTH):
        mod3 = _mod_call(l, cond, w_mod, b_mod).reshape(n_cond, 1, 6 * D)
        yp, cache = _mixers(False, l, yp, mod3, lambda i: 0, lw, n_bc, t_c)
        yp = _moe_call(l, yp, mod3, lambda i: 0, lw)
        caches.append(cache)
        past_kvm = _pastkv_call(l, cache_mla_ckv[:, l].reshape(n_bl * p_len, -1),
                                jnp.tile(cache_mla_kpe[:, l].reshape(n_bl * p_len, -1), (1, 4)),
                                lw["w_ukv"])
        past_gkv = jnp.concatenate([cache_gqa_k[:, l].reshape(n_bl * p_len, -1),
                                    cache_gqa_v[:, l].reshape(n_bl * p_len, -1),
                                    jnp.ones((n_bl * p_len, LANES), F32)], axis=-1).astype(BF16)
        past_dv = jnp.concatenate([cache_diff_v[:, l], jnp.ones_like(cache_diff_v[:, l])], axis=-1)
        past_dkv = jnp.concatenate([cache_diff_k[:, l].reshape(n_bl * p_len, -1),
                                    past_dv.reshape(n_bl * p_len, -1)], axis=-1).astype(BF16)
        s0 = state_ret[:, l].reshape(n_bl, 2, RET_HEADS // 2, 2 * RET_DK, RET_DV)
        ys, _ = _mixers(True, l, ys, mod3, lambda t: 1 + t // t_l, lw, n_bl, t_l, tabs=tabs,
                        past=(past_kvm, past_gkv, past_dkv), s0=s0)
        ys = _moe_call(l, ys, mod3, lambda t: 1 + t // t_l, lw)

    def stack(k, shape):
        return jnp.stack([caches[l][k].reshape((n_bc, t_c) + shape) for l in range(DEPTH)], axis=1)

    new_ret = jnp.stack([caches[l][6] for l in range(DEPTH)], axis=1)
    return (yp.reshape(n_bc, t_c, D), ys.reshape(n_bl, t_l, D),
            stack(0, (MLA_KV_LORA,)), stack(1, (MLA_ROPE,)),
            stack(2, (GQA_KV_HEADS, GQA_HD)), stack(3, (GQA_KV_HEADS, GQA_HD)),
            stack(4, (DIFF_HEADS, 2, DIFF_D)), stack(5, (DIFF_HEADS, DIFF_DV)), new_ret)
```

```python
import functools
import math

import numpy as np
import jax
import jax.numpy as jnp
from jax import lax
from jax.experimental import pallas as pl
from jax.experimental.pallas import tpu as pltpu
from jax.experimental.pallas import tpu_sc as plsc

F32 = jnp.float32
BF16 = jnp.bfloat16

D = 1024
DEPTH = 2
GRID_W = 64
ROPE_BASE = 10000.0
EPS = 1e-6

MLA_HEADS, MLA_NOPE, MLA_ROPE, MLA_V = 8, 64, 32, 64
MLA_Q_LORA, MLA_KV_LORA = 384, 256
GQA_HEADS, GQA_KV_HEADS, GQA_HD = 8, 2, 64
RET_HEADS, RET_DK, RET_DV = 4, 64, 128
DIFF_HEADS, DIFF_D, DIFF_DV = 4, 64, 128
N_BRANCH, BRANCH_W = 4, 512
N_EXPERTS, TOP_K, N_GROUPS, TOPK_GROUPS = 32, 4, 4, 2
EXPERT_FF, SHARED_FF = 256, 256
ROUTE_SCALE = 2.5
GROUP_SIZE = N_EXPERTS // N_GROUPS

LANES = 128
HALF_LANES = 64
VMEM_LIMIT = 56 * 1024 * 1024

C_CQ, C_CKV, C_KPE, C_GQ, C_GK, C_GV = 0, 384, 640, 768, 1280, 1408
C_DQ, C_DK, C_DV, C_RQ, C_RK, C_RV, C_RG, C_GL, C_END = (
    1536, 2048, 2560, 3072, 3328, 3584, 4096, 4608, 8704)
O_CQ, O_CKV, O_KPE, O_GQ, O_GK, O_GV = 0, 384, 640, 672, 1184, 1312
O_RQ, O_RK, O_RV, O_RG, O_DQ, O_DK, O_DV, O_GL, O_END = (
    1440, 1696, 1952, 2464, 2976, 3488, 4000, 4512, 8608)
GQA_ORDER = (0, 4, 1, 5, 2, 6, 3, 7)

KVM_W = 8 * 256
GKV_W = 3 * LANES
DKV_W = 512 + 4 * 256
LOG2E = 1.4426950408889634
TM = 256
TM_MERGE = 512
TQ = 256
TM_MOE_PRE = 512
TMX = 256
W_SLOTS = 4


def _cparams(sem):
    return pltpu.CompilerParams(dimension_semantics=sem, vmem_limit_bytes=VMEM_LIMIT)


def _const_spec(shape):
    nd = len(shape)
    return pl.BlockSpec(shape, lambda *_: (0,) * nd)


def _layer_spec(l, shape):
    nd = len(shape)
    return pl.BlockSpec((None,) + tuple(shape), lambda *_: (l,) + (0,) * nd)


def _rms(x, g):
    return x * lax.rsqrt(jnp.mean(x * x, axis=-1, keepdims=True) + EPS) * g


def _dot(a, b):
    return jnp.dot(a, b, preferred_element_type=F32)


def _dot_nt(a, b):
    return lax.dot_general(a, b, (((1,), (1,)), ((), ())), preferred_element_type=F32)


def _silu(x):
    return x * jax.nn.sigmoid(x)


def _lane_iota(shape):
    return lax.broadcasted_iota(jnp.int32, shape, len(shape) - 1)


def _seg_meansq(x, bd_ref, width):
    sq = x * x
    hi = sq.astype(BF16)
    lo = (sq - hi.astype(F32)).astype(BF16)
    bd = bd_ref[0:width, 0:width]
    return (_dot(hi, bd) + _dot(lo, bd)) * (1.0 / GQA_HD)


def _rope(x, cos, sin_signed, half):
    width = x.shape[-1]
    first = (_lane_iota(x.shape) % (2 * half)) < half
    partner = jnp.where(first, pltpu.roll(x, width - half, 1), pltpu.roll(x, half, 1))
    return x * cos + partner * sin_signed


def _tile_lanes(t, reps):
    return t if reps == 1 else jnp.concatenate([t] * reps, axis=1)


def _store_kvm(kvm_ref, kv, kpe_b):
    ones = jnp.ones(kpe_b.shape, BF16)
    for p in range(4):
        kvm_ref[:, p * 256:p * 256 + LANES] = kv[:, p * LANES:(p + 1) * LANES].astype(BF16)
        kvm_ref[:, p * 256 + LANES:(p + 1) * 256] = kpe_b
        kvm_ref[:, 1024 + p * 256:1024 + p * 256 + LANES] = (
            kv[:, 512 + p * LANES:512 + (p + 1) * LANES].astype(BF16))
        kvm_ref[:, 1024 + p * 256 + LANES:1024 + (p + 1) * 256] = ones


def _mod_kernel(c_ref, w_ref, b_ref, o_ref):
    a = _silu(c_ref[...]).astype(BF16)
    o_ref[...] = _dot(a, w_ref[...].astype(BF16)) + b_ref[...]


def _mod_call(l, cond, w_mod, b_mod):
    n_l, _, n = w_mod.shape
    tn = 1536
    return pl.pallas_call(
        _mod_kernel,
        grid=(n // tn,),
        in_specs=[_const_spec(cond.shape),
                  pl.BlockSpec((None, D, tn), lambda j: (l, 0, j)),
                  pl.BlockSpec((None, 1, tn), lambda j: (l, 0, j))],
        out_specs=pl.BlockSpec((cond.shape[0], tn), lambda j: (0, j)),
        out_shape=jax.ShapeDtypeStruct((cond.shape[0], n), F32),
        compiler_params=_cparams(("arbitrary",)),
        name="mod",
    )(cond, w_mod, b_mod.reshape(n_l, 1, n))


def _inprep_kernel(latent, n_aliased, *refs):
    (x_ref, mod_ref, gpre_ref, win_ref, gmq_ref, wuq_ref, gmkv_ref, wukv_ref,
     ggq_ref, ggk_ref, bd_ref) = refs[:11]
    refs = refs[11:]
    if latent:
        cos64_ref, sin64_ref, cospe_ref, sinpe_ref = refs[:4]
        refs = refs[4:]
    refs = refs[n_aliased:]
    (qm_ref, kvm_ref, gqo_ref, gkv_ref, dqo_ref, dkv_ref, ret_ref, rg_ref, gate_ref) = refs[:9]
    refs = refs[9:]
    if not latent:
        ckv_o, kpe_o, gk_o, gv_o, dk_o, dv_o = refs

    x = x_ref[...]
    mod = mod_ref[...]
    sh1 = mod[:, 0:D]
    sc1 = mod[:, D:2 * D]
    hb = (_rms(x, gpre_ref[...]) * (1.0 + sc1) + sh1).astype(BF16)

    def z(a, b):
        return _dot_nt(hb, win_ref[a:b, :])

    if latent:
        cos64, sin64 = cos64_ref[...], sin64_ref[...]
        cospe, sinpe = cospe_ref[...], sinpe_ref[...]

    cqn = _rms(z(O_CQ, O_CKV), gmq_ref[...]).astype(BF16)
    q = _dot(cqn, wuq_ref[...]) * ((MLA_NOPE + MLA_ROPE) ** -0.5 * LOG2E)
    q_nope, q_pe = q[:, 0:512], q[:, 512:768]
    if latent:
        q_pe = _rope(q_pe, _tile_lanes(cospe, 2), _tile_lanes(sinpe, 2), MLA_ROPE // 4)
    qm_ref[:, 0:512] = q_nope.astype(BF16)
    qm_ref[:, 512:768] = q_pe.astype(BF16)

    ckvn = _rms(z(O_CKV, O_KPE), gmkv_ref[...])
    kv = _dot(ckvn.astype(BF16), wukv_ref[...])
    kpe4 = _dot_nt(hb, jnp.concatenate([win_ref[O_KPE:O_GQ, :]] * 4, axis=0))
    if latent:
        kpe4 = _rope(kpe4, cospe, sinpe, MLA_ROPE // 4)
    else:
        ckv_o[...] = ckvn
        kpe_o[...] = kpe4[:, 0:MLA_ROPE]
    _store_kvm(kvm_ref, kv, kpe4.astype(BF16))

    gq = _dot_nt(hb, jnp.concatenate(
        [win_ref[O_GQ + h * GQA_HD:O_GQ + (h + 1) * GQA_HD, :] for h in GQA_ORDER], axis=0))
    gq = gq * lax.rsqrt(_seg_meansq(gq, bd_ref, 512) + EPS) * ggq_ref[...]
    gk = z(O_GK, O_GV)
    gk = gk * lax.rsqrt(_seg_meansq(gk, bd_ref, LANES) + EPS) * ggk_ref[...]
    gv = z(O_GV, O_RQ)
    if latent:
        gq = _rope(gq, _tile_lanes(cos64, 4), _tile_lanes(sin64, 4), GQA_HD // 4)
        gk = _rope(gk, cos64, sin64, GQA_HD // 4)
    else:
        gk_o[...] = jnp.transpose(gk)
        gv_o[...] = jnp.transpose(gv)
    gqo_ref[...] = (gq * (GQA_HD ** -0.5 * LOG2E)).astype(BF16)
    gkv_ref[:, 0:LANES] = gk.astype(BF16)
    gkv_ref[:, LANES:2 * LANES] = gv.astype(BF16)
    gkv_ref[:, 2 * LANES:3 * LANES] = jnp.ones(gv.shape, BF16)

    dq = z(O_DQ, O_DK)
    dk = z(O_DK, O_DV)
    dv = z(O_DV, O_GL)
    if latent:
        dq = _rope(dq, _tile_lanes(cos64, 4), _tile_lanes(sin64, 4), DIFF_D // 4)
        dk = _rope(dk, _tile_lanes(cos64, 4), _tile_lanes(sin64, 4), DIFF_D // 4)
    else:
        dk_o[...] = jnp.transpose(dk)
        dv_o[...] = dv
    dqo_ref[...] = (dq * (DIFF_D ** -0.5 * LOG2E)).astype(BF16)
    dkv_ref[:, 0:512] = dk.astype(BF16)
    for h in range(DIFF_HEADS):
        dkv_ref[:, 512 + h * 256:512 + h * 256 + LANES] = dv[:, h * LANES:(h + 1) * LANES].astype(BF16)
        dkv_ref[:, 512 + h * 256 + LANES:512 + (h + 1) * 256] = jnp.ones((dv.shape[0], LANES), BF16)

    ret_ref[:, 0:256] = z(O_RQ, O_RK).astype(BF16)
    ret_ref[:, 256:512] = (z(O_RK, O_RV) * (RET_DK ** -0.5)).astype(BF16)
    ret_ref[:, 512:1024] = z(O_RV, O_RG).astype(BF16)
    rg_ref[...] = z(O_RG, O_DQ).astype(BF16)

    for n in range(N_BRANCH):
        gate_ref[:, n * D:(n + 1) * D] = jax.nn.sigmoid(
            z(O_GL + n * D, O_GL + (n + 1) * D)).astype(BF16)


def _inprep_call(latent, l, x, mod3, mod_row, lw, tabs, t_len, prev_caches=None):
    n_tok = x.shape[0]
    nblk = n_tok // TM
    blk_per_seq = t_len // TM

    def tok(w):
        return pl.BlockSpec((TM, w), lambda i: (i, 0))

    in_specs = [tok(D),
                pl.BlockSpec((None, 1, 6 * D), lambda i: (mod_row(i * TM), 0, 0)),
                _layer_spec(l, (1, D)),
                pl.BlockSpec((None, O_END, D), lambda i: (l, 0, 0), pipeline_mode=pl.Buffered(1)),
                _layer_spec(l, (1, MLA_Q_LORA)), _layer_spec(l, (MLA_Q_LORA, 768)),
                _layer_spec(l, (1, MLA_KV_LORA)), _layer_spec(l, (MLA_KV_LORA, 1024)),
                _layer_spec(l, (1, 512)), _layer_spec(l, (1, LANES)), _const_spec((512, 512))]
    args = [x, mod3, lw["g_pre1"], lw["w_in"], lw["g_mla_q"], lw["w_uq"], lw["g_mla_kv"],
            lw["w_ukv"], lw["g_gqa_q"], lw["g_gqa_k"], lw["bd"]]
    if latent:
        tab_spec = pl.BlockSpec((TM, LANES), lambda i: (i % blk_per_seq, 0))
        in_specs += [tab_spec] * 4
        args += list(tabs)
    widths = [768, KVM_W, 512, GKV_W, 512, DKV_W, 1024, 512, 4 * D]
    out_specs = [tok(w) for w in widths]
    out_shape = [jax.ShapeDtypeStruct((n_tok, w), BF16) for w in widths]
    aliases = {}
    if not latent:
        assert TM == t_len
        n_seq = n_tok // t_len
        for w in (MLA_KV_LORA, MLA_ROPE):
            out_specs.append(pl.BlockSpec((None, TM, w), lambda i: (l, i, 0)))
            out_shape.append(jax.ShapeDtypeStruct((DEPTH, n_tok, w), F32))
        for w in (LANES, LANES, 512):
            out_specs.append(pl.BlockSpec((None, None, w, t_len), lambda i: (l, i, 0, 0)))
            out_shape.append(jax.ShapeDtypeStruct((DEPTH, n_seq, w, t_len), F32))
        out_specs.append(pl.BlockSpec((None, TM, 512), lambda i: (l, i, 0)))
        out_shape.append(jax.ShapeDtypeStruct((DEPTH, n_tok, 512), F32))
        if prev_caches is not None:
            n_in = len(args)
            in_specs += [pl.BlockSpec(memory_space=pl.ANY)] * len(prev_caches)
            args += list(prev_caches)
            aliases = {n_in + k: len(widths) + k for k in range(len(prev_caches))}
    return pl.pallas_call(
        functools.partial(_inprep_kernel, latent, len(aliases)),
        grid=(nblk,),
        in_specs=in_specs, out_specs=out_specs, out_shape=out_shape,
        input_output_aliases=aliases,
        compiler_params=_cparams(("arbitrary",)),
        name="inprep_lat" if latent else "inprep_ctx",
    )(*args)


def _pastkv_kernel(ckv_ref, kpe_ref, wukv_ref, o_ref):
    kv = _dot(ckv_ref[...].astype(BF16), wukv_ref[...])
    _store_kvm(o_ref, kv, kpe_ref[...].astype(BF16))


def _pastkv_call(l, ckv, kpe4, w_ukv):
    n = ckv.shape[0]
    return pl.pallas_call(
        _pastkv_kernel,
        grid=(n // TM,),
        in_specs=[pl.BlockSpec((TM, MLA_KV_LORA), lambda i: (i, 0)),
                  pl.BlockSpec((TM, LANES), lambda i: (i, 0)),
                  _layer_spec(l, (MLA_KV_LORA, 1024))],
        out_specs=pl.BlockSpec((TM, KVM_W), lambda i: (i, 0)),
        out_shape=jax.ShapeDtypeStruct((n, KVM_W), BF16),
        compiler_params=_cparams(("arbitrary",)),
        name="pastkv",
    )(ckv, kpe4, w_ukv)


def _softmax_pv(s, v_ones):
    m = jnp.max(s, axis=-1, keepdims=True)
    p = jnp.exp2(s - m).astype(BF16)
    o = _dot(p, v_ones)
    return o[:, 0:LANES] / o[:, LANES:2 * LANES]


def _attn_kernel(lam_init, n_past, qm_ref, kvm_ref, gq_ref, gkv_ref, dq_ref, dkv_ref, *refs):
    if n_past:
        past_refs, refs = refs[:3], refs[3:]
        lam_ref, gdiff_ref, o_ref = refs[:3]
        joined = refs[3:]

        @pl.when(pl.program_id(1) == 0)
        def _():
            for dst, past, new in zip(joined, past_refs, (kvm_ref, gkv_ref, dkv_ref)):
                dst[0:n_past, :] = past[...]
                dst[n_past:, :] = new[...]

        kvm_ref, gkv_ref, dkv_ref = joined
    else:
        lam_ref, gdiff_ref, o_ref = refs
    tq = qm_ref.shape[0]
    lane = _lane_iota((tq, LANES))
    low = lane < HALF_LANES
    zero = jnp.zeros((tq, LANES), BF16)

    for p in range(MLA_HEADS // 2):
        qn = qm_ref[:, p * LANES:(p + 1) * LANES]
        g = p // 2
        qpe = qm_ref[:, 512 + g * LANES:512 + (g + 1) * LANES]
        kk = kvm_ref[:, p * 256:(p + 1) * 256]
        vv = kvm_ref[:, 1024 + p * 256:1024 + (p + 1) * 256]
        outs = []
        for half in range(2):
            h = 2 * p + half
            slot = h % 4
            in_slot = (lane >= slot * MLA_ROPE) & (lane < (slot + 1) * MLA_ROPE)
            lhs = jnp.concatenate(
                [jnp.where(low if half == 0 else ~low, qn, zero),
                 jnp.where(in_slot, qpe, zero)], axis=1)
            outs.append(_softmax_pv(_dot_nt(lhs, kk), vv))
        o_ref[:, p * LANES:(p + 1) * LANES] = jnp.where(low, outs[0], outs[1]).astype(BF16)

    kk = gkv_ref[:, 0:LANES]
    vv = gkv_ref[:, LANES:3 * LANES]
    for g in range(GQA_HEADS // 2):
        qg = gq_ref[:, g * LANES:(g + 1) * LANES]
        o_lo = _softmax_pv(_dot_nt(jnp.where(low, qg, zero), kk), vv)
        o_hi = _softmax_pv(_dot_nt(jnp.where(low, zero, qg), kk), vv)
        o_ref[:, 512 + g * LANES:512 + (g + 1) * LANES] = jnp.where(low, o_lo, o_hi).astype(BF16)

    lp = lam_ref[...]
    lam = (jnp.exp(jnp.sum(lp[0:1] * lp[1:2], axis=-1, keepdims=True))
           - jnp.exp(jnp.sum(lp[2:3] * lp[3:4], axis=-1, keepdims=True)) + lam_init)
    for h in range(DIFF_HEADS):
        qh = dq_ref[:, h * LANES:(h + 1) * LANES]
        kk = dkv_ref[:, h * LANES:(h + 1) * LANES]
        vv = dkv_ref[:, 512 + h * 256:512 + (h + 1) * 256]
        a1 =_softmax_pv(_dot_nt(jnp.where(low, qh, zero), kk), vv)
        a2 = _softmax_pv(_dot_nt(jnp.where(low, zero, qh), kk), vv)
        od = _rms(a1 - lam * a2, gdiff_ref[...]) * (1.0 - lam_init)
        o_ref[:, 1024 + h * LANES:1024 + (h + 1) * LANES] = od.astype(BF16)


def _attn_call(l, lam_init, qm, kvm, gq, gkv, dq, dkv, lam_p, g_diff, n_b, t_len, past=None):
    nq = t_len // TQ
    n_past = 0 if past is None else past[0].shape[0] // n_b

    def qspec(w):
        return pl.BlockSpec((TQ, w), lambda b, i: (b * nq + i, 0))

    def kspec(w, rows=t_len):
        return pl.BlockSpec((rows, w), lambda b, i: (b, 0))

    in_specs = [qspec(768), kspec(KVM_W), qspec(512), kspec(GKV_W), qspec(512), kspec(DKV_W)]
    args = [qm, kvm, gq, gkv, dq, dkv]
    scratch = []
    if n_past:
        in_specs += [kspec(KVM_W, n_past), kspec(GKV_W, n_past), kspec(DKV_W, n_past)]
        args += list(past)
        scratch = [pltpu.VMEM((n_past + t_len, w), BF16) for w in (KVM_W, GKV_W, DKV_W)]
    in_specs += [_layer_spec(l, (4, DIFF_D)), _layer_spec(l, (1, DIFF_DV))]
    args += [lam_p, g_diff]
    return pl.pallas_call(
        functools.partial(_attn_kernel, lam_init, n_past),
        grid=(n_b, nq),
        in_specs=in_specs,
        out_specs=qspec(3 * BRANCH_W),
        out_shape=jax.ShapeDtypeStruct((n_b * t_len, 3 * BRANCH_W), BF16),
        scratch_shapes=scratch,
        compiler_params=_cparams(("arbitrary", "arbitrary")),
        name="attn",
    )(*args)


def _log_sigmoid(x):
    return jnp.minimum(x, 0.0) - jnp.log(1.0 + jnp.exp(-jnp.abs(x)))


def _log_gamma(dec_ref, l, d, h):
    return _log_sigmoid(jnp.full((1, 1), dec_ref[l, d, h], F32))


def _ret_kernel(latent, l, t_len, dec_ref, q_ref, k_ref, v_ref, rg_ref, gret_ref, *refs):
    if latent:
        s0_ref, o_ref = refs
    else:
        o_ref, st_ref = refs[-2:]
    tq = q_ref.shape[0]
    t0 = pl.program_id(1) * tq
    lane = _lane_iota((tq, LANES))
    low = lane < HALF_LANES
    zero = jnp.zeros((tq, LANES), BF16)
    t_idx = (t0 + lax.broadcasted_iota(jnp.int32, (tq, t_len), 0)).astype(F32)
    s_idx = lax.broadcasted_iota(jnp.int32, (tq, t_len), 1).astype(F32)
    dist = t_idx - s_idx
    past = dist >= 0
    diag = jnp.where(dist == 0, 1.0, 0.0)
    t_col = (t0 + lax.broadcasted_iota(jnp.int32, (tq, 1), 0)).astype(F32)

    def lg(d, h):
        return _log_gamma(dec_ref, l, d, h)

    for h in range(RET_HEADS):
        p, half = h // 2, h % 2
        qp = q_ref[:, p * LANES:(p + 1) * LANES]
        qm = jnp.where(low if half == 0 else ~low, qp, zero)
        kp = k_ref[:, p * LANES:(p + 1) * LANES]
        vh = v_ref[:, h * LANES:(h + 1) * LANES]
        lgf, lgb = lg(0, h), lg(1, h)
        dmask = jnp.exp(jnp.where(past, lgf, -lgb) * dist) + diag
        o = _dot((_dot_nt(qm, kp) * dmask).astype(BF16), vh)
        if latent:
            sf =s0_ref[0, p].astype(BF16)
            sb = s0_ref[1, p].astype(BF16)
            o = o + _dot(qm, sf) * jnp.exp(lgf * (t_col + 1.0))
            o = o + _dot(qm, sb) * jnp.exp(lgb * (float(t_len) - t_col))
        mu = jnp.mean(o, axis=-1, keepdims=True)
        oc = o - mu
        y = oc * lax.rsqrt(jnp.mean(oc * oc, axis=-1, keepdims=True) + EPS)
        y = y * gret_ref[:, h * LANES:(h + 1) * LANES]
        rg = rg_ref[:, h * LANES:(h + 1) * LANES].astype(F32)
        o_ref[:, h * LANES:(h + 1) * LANES] = (y * _silu(rg)).astype(BF16)

    if not latent:
        s_col = lax.broadcasted_iota(jnp.int32, (t_len, 1), 0).astype(F32)
        lane_t = _lane_iota((1, LANES)) < HALF_LANES
        for p in range(RET_HEADS // 2):
            kp = k_ref[:, p * LANES:(p + 1) * LANES].astype(F32)
            for d in range(2):
                lg_lane = jnp.where(lane_t, lg(d, 2 * p), lg(d, 2 * p + 1))
                expo = (float(t_len) - 1.0 - s_col) if d == 0 else s_col
                kdec_t = jnp.transpose(kp * jnp.exp(lg_lane * expo)).astype(BF16)
                for half in range(2):
                    h = 2 * p + half
                    st = _dot(kdec_t, v_ref[:, h * LANES:(h + 1) * LANES])
                    st_ref[d, h] = st[half * RET_DK:(half + 1) * RET_DK, :]


def _ret_call(latent, l, dec, ret, rg, g_ret, s0, n_b, t_len, prev_state=None):
    nq = t_len // TQ
    aliases = {}
    assert latent or nq == 1
    in_specs = [pl.BlockSpec(memory_space=pltpu.SMEM),
                pl.BlockSpec((TQ, 256), lambda b, i: (b * nq + i, 0)),
                pl.BlockSpec((t_len, 256), lambda b, i: (b, 1)),
                pl.BlockSpec((t_len, 512), lambda b, i: (b, 1)),
                pl.BlockSpec((TQ, 512), lambda b, i: (b * nq + i, 0)),
                _layer_spec(l, (1, 512))]
    args = [dec, ret, ret, ret, rg, g_ret]
    out_specs = [pl.BlockSpec((TQ, 512), lambda b, i: (b * nq + i, 0))]
    out_shape = [jax.ShapeDtypeStruct((n_b * t_len, 512), BF16)]
    if latent:
        in_specs.append(pl.BlockSpec((None, 2, 2, LANES, LANES), lambda b, i: (b, 0, 0, 0, 0)))
        args.append(s0)
    else:
        out_specs.append(pl.BlockSpec((None, None, 2, RET_HEADS, RET_DK, RET_DV),
                                      lambda b, i: (l, b, 0, 0, 0, 0)))
        out_shape.append(jax.ShapeDtypeStruct((DEPTH, n_b, 2, RET_HEADS, RET_DK, RET_DV), F32))
        if prev_state is not None:
            aliases = {len(args): 1}
            in_specs.append(pl.BlockSpec(memory_space=pl.ANY))
            args.append(prev_state)
    return pl.pallas_call(
        functools.partial(_ret_kernel, latent, l, t_len),
        grid=(n_b, nq),
        in_specs=in_specs, out_specs=out_specs, out_shape=out_shape,
        input_output_aliases=aliases,
        compiler_params=_cparams(("arbitrary", "arbitrary")),
        name="ret_lat" if latent else "ret_ctx",
    )(*args)


def _merge_kernel(x_ref, mod_ref, br_ref, or_ref, gate_ref, wbr_ref, wout_ref, gpost_ref, o_ref):
    merged = None
    for n in range(N_BRANCH):
        if n < 2:
            b = br_ref[:, n * BRANCH_W:(n + 1) * BRANCH_W]
        elif n == 2:
            b = or_ref[...]
        else:
            b = br_ref[:, 2 * BRANCH_W:3 * BRANCH_W]
        t = gate_ref[:, n * D:(n + 1) * D].astype(F32) * _dot(b, wbr_ref[n])
        merged = t if merged is None else merged + t
    out = _dot(merged.astype(BF16), wout_ref[...])
    g1 = mod_ref[...][:, 2 * D:3 * D]
    o_ref[...] = x_ref[...] + g1 * _rms(out, gpost_ref[...])


def _merge_call(l, x, mod3, mod_row, br, o_r, gates, lw):
    n_tok = x.shape[0]

    def tok(w):
        return pl.BlockSpec((TM_MERGE, w), lambda i: (i, 0))

    return pl.pallas_call(
        _merge_kernel,
        grid=(n_tok // TM_MERGE,),
        in_specs=[tok(D), pl.BlockSpec((None, 1, 6 * D), lambda i: (mod_row(i * TM_MERGE), 0, 0)),
                  tok(3 * BRANCH_W), tok(BRANCH_W), tok(4 * D),
                  _layer_spec(l, (N_BRANCH, BRANCH_W, D)), _layer_spec(l, (D, D)),
                  _layer_spec(l, (1, D))],
        out_specs=tok(D),
        out_shape=jax.ShapeDtypeStruct((n_tok, D), F32),
        compiler_params=_cparams(("arbitrary",)),
        name="merge",
    )(x, mod3, br, o_r, gates, lw["w_br"], lw["w_out"], lw["g_post1"])


def _route(logits_t, bias):
    n = logits_t.shape[1]
    scores = jax.nn.sigmoid(logits_t)
    sel = scores + bias
    neg = -jnp.inf
    sub = lax.broadcasted_iota(jnp.int32, (GROUP_SIZE, n), 0)
    grp = []
    for g in range(N_GROUPS):
        blk = sel[g * GROUP_SIZE:(g + 1) * GROUP_SIZE]
        m1 = jnp.max(blk, axis=0, keepdims=True)
        i1 = jnp.min(jnp.where(blk == m1, sub, GROUP_SIZE), axis=0, keepdims=True)
        m2 = jnp.max(jnp.where(sub == i1, neg, blk), axis=0, keepdims=True)
        grp.append(m1 + m2)
    parts = []
    for g in range(N_GROUPS):
        beaten = jnp.zeros((1, n), jnp.int32)
        for o in range(N_GROUPS):
            if o == g:
                continue
            wins = (grp[o] > grp[g]) | (grp[o] == grp[g]) if o < g else (grp[o] > grp[g])
            beaten = beaten + wins.astype(jnp.int32)
        keep = beaten < TOPK_GROUPS
        parts.append(jnp.where(keep, sel[g * GROUP_SIZE:(g + 1) * GROUP_SIZE], neg))
    cur = jnp.concatenate(parts, axis=0)
    eidx = lax.broadcasted_iota(jnp.int32, (N_EXPERTS, n), 0)
    hits, ids, ws = [], [], []
    for _ in range(TOP_K):
        m = jnp.max(cur, axis=0, keepdims=True)
        i = jnp.min(jnp.where(cur == m, eidx, N_EXPERTS), axis=0, keepdims=True)
        hit = eidx == i
        hits.append(hit)
        ids.append(i)
        ws.append(jnp.sum(jnp.where(hit, scores, 0.0), axis=0, keepdims=True))
        cur = jnp.where(hit, neg, cur)
    wsum = ws[0] + ws[1] + ws[2] + ws[3]
    return hits, ids, [w / wsum * ROUTE_SCALE for w in ws]


U32 = jnp.uint32
HIGH16 = np.uint32(0xFFFF0000)


def _bf16_bits(v):
    return lax.bitcast_convert_type(v.astype(BF16).astype(F32), U32)


def _pack_rows(v):
    return (_bf16_bits(v[:, 0:D // 2]) >> 16) | _bf16_bits(v[:, D // 2:D])


def _unpack_rows(p):
    lo = lax.bitcast_convert_type(p << 16, F32)
    hi = lax.bitcast_convert_type(p & HIGH16, F32)
    return jnp.concatenate([lo, hi], axis=1)


def _moe_pre_kernel(x_ref, mod_ref, gpre_ref, wr_ref, br_ref, tri_ref,
                    hp_ref, eidx_ref, rank_ref, comb_ref, cnt_ref, run_ref):
    tm = x_ref.shape[0]

    @pl.when(pl.program_id(0) == 0)
    def _():
        run_ref[...] = jnp.zeros_like(run_ref)

    mod = mod_ref[...]
    sh2, sc2 = mod[:, 3 * D:4 * D], mod[:, 4 * D:5 * D]
    h = _rms(x_ref[...], gpre_ref[...]) * (1.0 + sc2) + sh2
    hp_ref[...] = _pack_rows(h)
    hb = h.astype(BF16)
    h_lo = (h - hb.astype(F32)).astype(BF16)
    wr = wr_ref[...]
    wr_hi = wr.astype(BF16)
    wr_lo = (wr - wr_hi.astype(F32)).astype(BF16)
    logits_t = _dot_nt(wr_hi, hb) + _dot_nt(wr_hi, h_lo) + _dot_nt(wr_lo, hb)
    hits, ids, ws = _route(logits_t, br_ref[...])

    picked = jnp.zeros((N_EXPERTS, tm), F32)
    for hit in hits:
        picked = jnp.where(hit, 1.0, picked)
    before = _dot(picked.astype(BF16), tri_ref[...]) + run_ref[:, 0:1]
    sub8 = lax.broadcasted_iota(jnp.int32, (8, tm), 0)
    comb8 = jnp.zeros((8, tm), F32)
    for k in range(TOP_K):
        rank = jnp.sum(jnp.where(hits[k], before, 0.0), axis=0, keepdims=True)
        eidx_ref[k:k + 1, :] = ids[k]
        rank_ref[k:k + 1, :] = rank.astype(jnp.int32)
        comb8 = jnp.where(sub8 == k, ws[k], comb8)
    comb_ref[...] = jnp.transpose(
        jnp.concatenate([comb8, jnp.zeros((LANES - 8, tm), F32)], axis=0))
    run_ref[...] = run_ref[...] + jnp.sum(picked, axis=1, keepdims=True)
    cnt_ref[...] = run_ref[...]


def _moe_pre_call(l, x, mod3, mod_row, lw):
    n_tok = x.shape[0]
    tm = TM_MOE_PRE
    tri = np.arange(tm)
    tri = jnp.asarray(tri[:, None] < tri[None, :], BF16)
    row4 = pl.BlockSpec((TOP_K, tm), lambda i: (0, i))
    return pl.pallas_call(
        _moe_pre_kernel,
        grid=(n_tok // tm,),
        in_specs=[pl.BlockSpec((tm, D), lambda i: (i, 0)),
                  pl.BlockSpec((None, 1, 6 * D), lambda i: (mod_row(i * tm), 0, 0)),
                  _layer_spec(l, (1, D)), _layer_spec(l, (N_EXPERTS, D)),
                  _layer_spec(l, (N_EXPERTS, 1)), _const_spec((tm, tm))],
        out_specs=[pl.BlockSpec((tm, D // 2), lambda i: (i, 0)), row4, row4,
                   pl.BlockSpec((tm, LANES), lambda i: (i, 0)),
                   _const_spec((N_EXPERTS, LANES))],
        out_shape=[jax.ShapeDtypeStruct((n_tok, D // 2), U32),
                   jax.ShapeDtypeStruct((TOP_K, n_tok), jnp.int32),
                   jax.ShapeDtypeStruct((TOP_K, n_tok), jnp.int32),
                   jax.ShapeDtypeStruct((n_tok, LANES), F32),
                   jax.ShapeDtypeStruct((N_EXPERTS, LANES), F32)],
        scratch_shapes=[pltpu.VMEM((N_EXPERTS, LANES), F32)],
        compiler_params=_cparams(("arbitrary",)),
        name="moe_pre",
    )(x, mod3, lw["g_pre2"], lw["w_router_t"], lw["b_router"], tri)


def _moe_plan_kernel(eidx_ref, rank_ref, cnt_ref, dest_ref, te_ref, tv_ref, tn_ref):
    tm = eidx_ref.shape[1]
    cnt = cnt_ref[...]
    padded = jnp.ceil(cnt * (1.0 / TMX)) * TMX
    row = lax.broadcasted_iota(jnp.int32, cnt.shape, 0)
    incl = padded
    shift = 1
    while shift < N_EXPERTS:
        incl = incl + jnp.where(row >= shift, pltpu.roll(incl, shift, 0), 0.0)
        shift *= 2
    start = (incl - padded)[:, 0:1]
    end = incl[:, 0:1]
    erow = lax.broadcasted_iota(jnp.int32, (N_EXPERTS, tm), 0)
    for k in range(TOP_K):
        mine = erow == eidx_ref[k:k + 1, :]
        base = jnp.sum(jnp.where(mine, start, 0.0), axis=0, keepdims=True)
        dest_ref[k:k + 1, :] = rank_ref[k:k + 1, :] + base.astype(jnp.int32)

    @pl.when(pl.program_id(0) == 0)
    def _():
        tile0 = (_lane_iota((1, LANES)) * TMX).astype(F32)
        owner = jnp.sum(jnp.where(end <= tile0, 1.0, 0.0), axis=0, keepdims=True)
        owner = jnp.minimum(owner, N_EXPERTS - 1.0)
        erow_t = lax.broadcasted_iota(jnp.int32, (N_EXPERTS, LANES), 0).astype(F32)
        left = jnp.sum(jnp.where(erow_t == owner, cnt[:, 0:1] - (tile0 - start), 0.0),
                       axis=0, keepdims=True)
        te_ref[...] = owner.astype(jnp.int32)
        tv_ref[...] = jnp.clip(left, 0.0, float(TMX)).astype(jnp.int32)
        tn_ref[...] = jnp.full(tn_ref.shape, N_EXPERTS, jnp.int32)
        nxt = owner
        for k in range(W_SLOTS - 1):
            later = (erow_t > nxt) & (cnt[:, 0:1] > 0.0)
            nxt = jnp.min(jnp.where(later, erow_t, float(N_EXPERTS)), axis=0, keepdims=True)
            tn_ref[k:k + 1, :] = nxt.astype(jnp.int32)
        n_used = jnp.sum(jnp.where(left > 0.0, 1.0, 0.0), axis=1, keepdims=True)
        tn_ref[7:8, :] = jnp.minimum(tile0 * (1.0 / TMX), n_used - 1.0).astype(jnp.int32)


def _moe_plan_call(eidx, rank, cnt):
    n_tok = eidx.shape[1]
    tm = TM_MOE_PRE
    row4 = pl.BlockSpec((TOP_K, tm), lambda i: (0, i))
    tiles = jax.ShapeDtypeStruct((1, LANES), jnp.int32)
    return pl.pallas_call(
        _moe_plan_kernel,
        grid=(n_tok // tm,),
        in_specs=[row4, row4, _const_spec((N_EXPERTS, LANES))],
        out_specs=[row4, _const_spec((1, LANES)), _const_spec((1, LANES)), _const_spec((8, LANES))],
        out_shape=[jax.ShapeDtypeStruct((TOP_K, n_tok), jnp.int32), tiles, tiles,
                   jax.ShapeDtypeStruct((8, LANES), jnp.int32)],
        compiler_params=_cparams(("arbitrary",)),
        name="moe_plan",
    )(eidx, rank, cnt)


def _experts_kernel(l, te_ref, tv_ref, tn_ref, xs_ref, wgu_hbm, wdn_hbm, ys_ref,
                    wgu_f, wdn_f, wgu_b, wdn_b, sem, group_ref):
    j = pl.program_id(0)
    valid = tv_ref[j]
    expert = te_ref[j]

    def fetch(e, slot):
        return (pltpu.make_async_copy(wgu_hbm.at[l, e], wgu_f.at[slot], sem.at[slot, 0]),
                pltpu.make_async_copy(wdn_hbm.at[l, e], wdn_f.at[slot], sem.at[slot, 1]))

    def start_if_any(e, slot):
        @pl.when(e < N_EXPERTS)
        def _():
            for cp in fetch(e, slot):
                cp.start()

    @pl.when(j == 0)
    def _():
        group_ref[0] = 0
        start_if_any(expert, 0)
        for k in range(W_SLOTS - 2):
            start_if_any(tn_ref[k, 0], k + 1)

    first_tile = (j == 0) | (expert != te_ref[jnp.maximum(j - 1, 0)])

    @pl.when(first_tile & (valid > 0))
    def _():
        group = group_ref[0]
        slot = lax.rem(group, W_SLOTS)
        for cp in fetch(expert, slot):
            cp.wait()
        wgu_b[...] = wgu_f[slot].astype(BF16)
        wdn_b[...] = wdn_f[slot].astype(BF16)
        start_if_any(tn_ref[W_SLOTS - 2, j], lax.rem(group + W_SLOTS - 1, W_SLOTS))
        group_ref[0] = group + 1

    @pl.when(valid > 0)
    def _():
        rows = lax.broadcasted_iota(jnp.int32, (TMX, D), 0)
        x = jnp.where(rows < valid, _unpack_rows(xs_ref[...]), 0.0).astype(BF16)
        gu = _dot(x, wgu_b[...])
        a = _silu(gu[:, 0:EXPERT_FF]) * gu[:, EXPERT_FF:2 * EXPERT_FF]
        ys_ref[...] = _pack_rows(_dot(a.astype(BF16), wdn_b[...]))


def _experts_call(l, xs, te, tv, tn, w_gu, w_dn):
    n_tiles = xs.shape[0] // TMX
    grid_spec = pltpu.PrefetchScalarGridSpec(
        num_scalar_prefetch=3,
        grid=(n_tiles,),
        in_specs=[pl.BlockSpec((TMX, D // 2), lambda j, te, tv, tn: (tn[7, j], 0)),
                  pl.BlockSpec(memory_space=pl.ANY), pl.BlockSpec(memory_space=pl.ANY)],
        out_specs=pl.BlockSpec((TMX, D // 2), lambda j, te, tv, tn: (tn[7, j], 0)),
        scratch_shapes=[pltpu.VMEM((W_SLOTS, D, 2 * EXPERT_FF), F32),
                        pltpu.VMEM((W_SLOTS, EXPERT_FF, D), F32),
                        pltpu.VMEM((D, 2 * EXPERT_FF), BF16), pltpu.VMEM((EXPERT_FF, D), BF16),
                        pltpu.SemaphoreType.DMA((W_SLOTS, 2)), pltpu.SMEM((1,), jnp.int32)])
    return pl.pallas_call(
        functools.partial(_experts_kernel, l),
        grid_spec=grid_spec,
        out_shape=jax.ShapeDtypeStruct(xs.shape, U32),
        compiler_params=_cparams(("arbitrary",)),
        name="moe_experts",
    )(te, tv, tn, xs, w_gu, w_dn)


def _moe_post_kernel(x_ref, mod_ref, hp_ref, yg_ref, comb_ref, wsgu_ref, wsdn_ref, gpost_ref,
                     o_ref):
    hb = _unpack_rows(hp_ref[...]).astype(BF16)
    sgu = _dot(hb, wsgu_ref[...])
    sa = _silu(sgu[:, 0:SHARED_FF]) * sgu[:, SHARED_FF:2 * SHARED_FF]
    acc = _dot(sa.astype(BF16), wsdn_ref[...])
    comb = comb_ref[...]
    for k in range(TOP_K):
        acc = acc + comb[:, k:k + 1] * _unpack_rows(yg_ref[k])
    g2 = mod_ref[...][:, 5 * D:6 * D]
    o_ref[...] = x_ref[...] + g2 * _rms(acc, gpost_ref[...])


def _moe_post_call(l, x, mod3, mod_row, hp, yg, comb, lw):
    n_tok = x.shape[0]
    tm = TM_MOE_PRE
    return pl.pallas_call(
        _moe_post_kernel,
        grid=(n_tok // tm,),
        in_specs=[pl.BlockSpec((tm, D), lambda i: (i, 0)),
                  pl.BlockSpec((None, 1, 6 * D), lambda i: (mod_row(i * tm), 0, 0)),
                  pl.BlockSpec((tm, D // 2), lambda i: (i, 0)),
                  pl.BlockSpec((TOP_K, tm, D // 2), lambda i: (0, i, 0)),
                  pl.BlockSpec((tm, LANES), lambda i: (i, 0)),
                  _layer_spec(l, (D, 2 * SHARED_FF)), _layer_spec(l, (SHARED_FF, D)),
                  _layer_spec(l, (1, D))],
        out_specs=pl.BlockSpec((tm, D), lambda i: (i, 0)),
        out_shape=jax.ShapeDtypeStruct((n_tok, D), F32),
        compiler_params=_cparams(("arbitrary",)),
        name="moe_post",
    )(x, mod3, hp, yg, comb, lw["w_sh_gu"], lw["w_sh_down"], lw["g_post2"])


def _moe_call(l, x, mod3, mod_row, lw):
    n_tok = x.shape[0]
    n_slots = -(-(TOP_K * n_tok + N_EXPERTS * (TMX - 1)) // TMX) * TMX
    assert n_slots // TMX <= LANES
    hp, eidx, rank, comb, cnt = _moe_pre_call(l, x, mod3, mod_row, lw)
    dest, te, tv, tn = _moe_plan_call(eidx, rank, cnt)
    dest = dest.reshape(TOP_K * n_tok)
    xs = _sc_scatter_rows(hp, dest, n_slots)
    ys = _experts_call(l, xs, te[0], tv[0], tn, lw["w_exp_gu"], lw["w_exp_down"])
    yg = _sc_gather_rows(ys, dest).reshape(TOP_K, n_tok, D // 2)
    return _moe_post_call(l, x, mod3, mod_row, hp, yg, comb, lw)


SC_CORES, SC_SUBCORES = 2, 16
SC_WORKERS = SC_CORES * SC_SUBCORES


def _sc_gather_rows(table, idx, chunk=64):
    n_out, width = idx.shape[0], table.shape[1]
    per_worker = n_out // SC_WORKERS
    n_chunks = per_worker // chunk
    assert per_worker * SC_WORKERS == n_out and n_chunks * chunk == per_worker
    mesh = plsc.VectorSubcoreMesh(core_axis_name="c", subcore_axis_name="s",
                                  num_cores=SC_CORES, num_subcores=SC_SUBCORES)

    @functools.partial(
        pl.kernel, mesh=mesh,
        out_type=jax.ShapeDtypeStruct((n_out, width), table.dtype),
        scratch_types=[pltpu.VMEM((chunk,), jnp.int32), pltpu.VMEM((chunk, width), table.dtype),
                       pltpu.SemaphoreType.DMA],
        name="sc_gather")
    def gather(table_hbm, idx_hbm, out_hbm, idx_v, rows_v, sem):
        base = (lax.axis_index("s") * SC_CORES + lax.axis_index("c")) * per_worker

        @pl.loop(0, n_chunks)
        def _(j):
            off = base + j * chunk
            pltpu.sync_copy(idx_hbm.at[pl.ds(off, chunk)], idx_v)
            pltpu.async_copy(table_hbm.at[idx_v], rows_v, sem).wait()
            pltpu.sync_copy(rows_v, out_hbm.at[pl.ds(off, chunk)])

    return gather(table, idx)


def _sc_scatter_rows(rows, dest, n_slots, chunk=64):
    n_tok, width = rows.shape
    per_worker = n_tok // SC_WORKERS
    n_chunks = per_worker // chunk
    assert per_worker * SC_WORKERS == n_tok and n_chunks * chunk == per_worker
    mesh = plsc.VectorSubcoreMesh(core_axis_name="c", subcore_axis_name="s",
                                  num_cores=SC_CORES, num_subcores=SC_SUBCORES)

    @functools.partial(
        pl.kernel, mesh=mesh,
        out_type=jax.ShapeDtypeStruct((n_slots, width), rows.dtype),
        scratch_types=[pltpu.VMEM((chunk,), jnp.int32), pltpu.VMEM((chunk, width), rows.dtype)],
        name="sc_scatter")
    def scatter(rows_hbm, dest_hbm, out_hbm, idx_v, rows_v):
        base = (lax.axis_index("s") * SC_CORES + lax.axis_index("c")) * per_worker

        @pl.loop(0, n_chunks)
        def _(j):
            off = base + j * chunk
            pltpu.sync_copy(rows_hbm.at[pl.ds(off, chunk)], rows_v)
            for k in range(TOP_K):
                pltpu.sync_copy(dest_hbm.at[pl.ds(k * n_tok + off, chunk)], idx_v)
                pltpu.sync_copy(rows_v, out_hbm.at[idx_v])

    return scatter(rows, dest)


def _rope_tables(t_len):
    pos = np.arange(t_len)
    row, col = pos // GRID_W, pos % GRID_W

    def tab(r):
        half = r // 2
        freq = ROPE_BASE ** (-np.arange(half, dtype=np.float64) / half)
        sign = np.concatenate([-np.ones(half), np.ones(half)])
        cs, sn = [], []
        for p in (row, col):
            ang = p[:, None].astype(np.float64) * freq[None, :]
            cs.append(np.concatenate([np.cos(ang), np.cos(ang)], axis=1))
            sn.append(np.concatenate([np.sin(ang), np.sin(ang)], axis=1) * sign[None, :])
        return np.concatenate(cs, axis=1), np.concatenate(sn, axis=1)

    c64, s64 = tab(GQA_HD // 2)
    cpe, spe = tab(MLA_ROPE // 2)
    out = (np.tile(c64, (1, 2)), np.tile(s64, (1, 2)), np.tile(cpe, (1, 4)), np.tile(spe, (1, 4)))
    return tuple(jnp.asarray(a, F32) for a in out)


def _prep_weights(p):
    n_l = p["w_in"].shape[0]

    def row(name):
        return p[name].reshape(n_l, 1, -1)

    w_uq = p["w_mla_uq"].reshape(n_l, MLA_Q_LORA, MLA_HEADS, MLA_NOPE + MLA_ROPE)
    w_uq = jnp.concatenate([w_uq[..., :MLA_NOPE].reshape(n_l, MLA_Q_LORA, -1),
                            w_uq[..., MLA_NOPE:].reshape(n_l, MLA_Q_LORA, -1)], axis=-1)
    w_ukv = p["w_mla_ukv"].reshape(n_l, MLA_KV_LORA, MLA_HEADS, MLA_NOPE + MLA_V)
    w_ukv = jnp.concatenate([w_ukv[..., :MLA_NOPE].reshape(n_l, MLA_KV_LORA, -1),
                             w_ukv[..., MLA_NOPE:].reshape(n_l, MLA_KV_LORA, -1)], axis=-1)
    w_br = p["w_br"]
    w_br_gqa = w_br[:, 1].reshape(n_l, GQA_HEADS, GQA_HD, D)[:, jnp.array(GQA_ORDER)]
    w_br = jnp.concatenate([w_br[:, 0:1], w_br_gqa.reshape(n_l, 1, BRANCH_W, D), w_br[:, 2:4]], axis=1)
    blk = np.arange(512) // GQA_HD
    return {
        "g_pre1": row("g_pre1"), "g_post1": row("g_post1"),
        "g_pre2": row("g_pre2"), "g_post2": row("g_post2"),
        "w_in": jnp.swapaxes(p["w_in"], 1, 2).astype(BF16),
        "g_mla_q": row("g_mla_q"), "w_uq": w_uq.astype(BF16),
        "g_mla_kv": row("g_mla_kv"), "w_ukv": w_ukv.astype(BF16),
        "g_gqa_q": jnp.tile(p["g_gqa_q"], (1, GQA_HEADS)).reshape(n_l, 1, -1),
        "g_gqa_k": jnp.tile(p["g_gqa_k"], (1, GQA_KV_HEADS)).reshape(n_l, 1, -1),
        "bd": jnp.asarray(blk[:, None] == blk[None, :], BF16),
        "ret_decay": p["ret_decay"],
        "g_ret": row("g_ret"),
        "diff_lambda": p["diff_lambda"], "g_diff": row("g_diff"),
        "w_br": w_br.astype(BF16), "w_out": p["w_out"].astype(BF16),
        "w_router_t": jnp.swapaxes(p["w_router"], 1, 2),
        "b_router": p["b_router"].reshape(n_l, N_EXPERTS, 1),
        "w_exp_gu": p["w_exp_gu"], "w_exp_down": p["w_exp_down"],
        "w_sh_gu": p["w_sh_gu"].astype(BF16), "w_sh_down": p["w_sh_down"].astype(BF16),
    }


def _mixers(latent, l, x, mod3, mod_row, lw, n_b, t_len, tabs=None, past=None, s0=None,
            prev_cache=None):
    lam_init = 0.8 - 0.6 * math.exp(-0.3 * l)
    outs = _inprep_call(latent, l, x, mod3, mod_row, lw, tabs, t_len,
                        None if prev_cache is None else prev_cache[:6])
    qm, kvm, gq, gkv, dq, dkv, ret, rg, gates = outs[:9]
    br = _attn_call(l, lam_init, qm, kvm, gq, gkv, dq, dkv, lw["diff_lambda"], lw["g_diff"],
                    n_b, t_len, past)
    r = _ret_call(latent, l, lw["ret_decay"], ret, rg, lw["g_ret"], s0, n_b, t_len,
                  None if prev_cache is None else prev_cache[6])
    y = _merge_call(l, x, mod3, mod_row, br, r[0], gates, lw)
    cache = None if latent else tuple(outs[9:]) + (r[1],)
    return y, cache


def kernel(x_prompt, x_sample, cache_mla_ckv, cache_mla_kpe, cache_gqa_k, cache_gqa_v, cache_diff_k, cache_diff_v, state_ret, c, c_ctx, w_mod, b_mod, g_pre1, g_post1, g_pre2, g_post2, w_in, g_mla_q, w_mla_uq, g_mla_kv, w_mla_ukv, g_gqa_q, g_gqa_k, ret_decay, g_ret, diff_lambda, g_diff, w_br, w_out, w_router, b_router, w_exp_gu, w_exp_down, w_sh_gu, w_sh_down):
    params = dict(w_in=w_in, g_pre1=g_pre1, g_post1=g_post1, g_pre2=g_pre2,
                  g_post2=g_post2, g_mla_q=g_mla_q, w_mla_uq=w_mla_uq,
                  g_mla_kv=g_mla_kv, w_mla_ukv=w_mla_ukv, g_gqa_q=g_gqa_q, g_gqa_k=g_gqa_k,
                  ret_decay=ret_decay, g_ret=g_ret, diff_lambda=diff_lambda, g_diff=g_diff,
                  w_br=w_br, w_out=w_out, w_router=w_router, b_router=b_router,
                  w_exp_gu=w_exp_gu, w_exp_down=w_exp_down, w_sh_gu=w_sh_gu, w_sh_down=w_sh_down)
    n_bc, t_c, _ = x_prompt.shape
    n_bl, t_l, _ = x_sample.shape
    p_len = cache_mla_ckv.shape[2]
    tabs = _rope_tables(t_l)
    n_cond = 8
    cond = jnp.concatenate([c_ctx[None, :], c, jnp.zeros((n_cond - 1 - n_bl, D), F32)], axis=0)
    assert t_l % TM_MERGE == 0 and (t_c * n_bc) % TM_MERGE == 0
    assert t_l % TM_MOE_PRE == 0 and (t_c * n_bc) % TM_MOE_PRE == 0

    yp = x_prompt.reshape(n_bc * t_c, D)
    ys = x_sample.reshape(n_bl * t_l, D)
    cache = None
    lw = _prep_weights(params)
    for l in range(DEPTH):
        mod3 = _mod_call(l, cond, w_mod, b_mod).reshape(n_cond, 1, 6 * D)
        yp, cache = _mixers(False, l, yp, mod3, lambda i: 0, lw, n_bc, t_c, prev_cache=cache)
        yp = _moe_call(l, yp, mod3, lambda i: 0, lw)
        past_kvm = _pastkv_call(l, cache_mla_ckv[:, l].reshape(n_bl * p_len, -1),
                                jnp.tile(cache_mla_kpe[:, l].reshape(n_bl * p_len, -1), (1, 4)),
                                lw["w_ukv"])
        past_gkv = jnp.concatenate([cache_gqa_k[:, l].reshape(n_bl * p_len, -1),
                                    cache_gqa_v[:, l].reshape(n_bl * p_len, -1),
                                    jnp.ones((n_bl * p_len, LANES), F32)], axis=-1).astype(BF16)
        past_dv = jnp.concatenate([cache_diff_v[:, l], jnp.ones_like(cache_diff_v[:, l])], axis=-1)
        past_dkv = jnp.concatenate([cache_diff_k[:, l].reshape(n_bl * p_len, -1),
                                    past_dv.reshape(n_bl * p_len, -1)], axis=-1).astype(BF16)
        s0 = state_ret[:, l].reshape(n_bl, 2, RET_HEADS // 2, 2 * RET_DK, RET_DV)
        ys, _ = _mixers(True, l, ys, mod3, lambda t: 1 + t // t_l, lw, n_bl, t_l, tabs=tabs,
                        past=(past_kvm, past_gkv, past_dkv), s0=s0)
        ys = _moe_call(l, ys, mod3, lambda t: 1 + t // t_l, lw)

    ckv, kpe, gk_t, gv_t, dk_t, dv, ret_state = cache

    def rows(a, shape):
        return jnp.swapaxes(a.reshape((DEPTH, n_bc, t_c) + shape), 0, 1)

    def time_minor(a, shape):
        a = a.reshape((DEPTH, n_bc) + shape + (t_c,))
        return jnp.transpose(a, (1, 0, a.ndim - 1) + tuple(range(2, a.ndim - 1)))

    return (yp.reshape(n_bc, t_c, D), ys.reshape(n_bl, t_l, D),
            rows(ckv, (MLA_KV_LORA,)), rows(kpe, (MLA_ROPE,)),
            time_minor(gk_t, (GQA_KV_HEADS, GQA_HD)), time_minor(gv_t, (GQA_KV_HEADS, GQA_HD)),
            time_minor(dk_t, (DIFF_HEADS, 2, DIFF_D)), rows(dv, (DIFF_HEADS, DIFF_DV)),
            jnp.swapaxes(ret_state, 0, 1))
```

```python
import functools
import math

import numpy as np
import jax
import jax.numpy as jnp
from jax import lax
from jax.experimental import pallas as pl
from jax.experimental.pallas import tpu as pltpu
from jax.experimental.pallas import tpu_sc as plsc

F32 = jnp.float32
BF16 = jnp.bfloat16

D = 1024
DEPTH = 2
GRID_W = 64
ROPE_BASE = 10000.0
EPS = 1e-6

MLA_HEADS, MLA_NOPE, MLA_ROPE, MLA_V = 8, 64, 32, 64
MLA_Q_LORA, MLA_KV_LORA = 384, 256
GQA_HEADS, GQA_KV_HEADS, GQA_HD = 8, 2, 64
RET_HEADS, RET_DK, RET_DV = 4, 64, 128
DIFF_HEADS, DIFF_D, DIFF_DV = 4, 64, 128
N_BRANCH, BRANCH_W = 4, 512
N_EXPERTS, TOP_K, N_GROUPS, TOPK_GROUPS = 32, 4, 4, 2
EXPERT_FF, SHARED_FF = 256, 256
ROUTE_SCALE = 2.5
GROUP_SIZE = N_EXPERTS // N_GROUPS

LANES = 128
HALF_LANES = 64
VMEM_LIMIT = 56 * 1024 * 1024

C_CQ, C_CKV, C_KPE, C_GQ, C_GK, C_GV = 0, 384, 640, 768, 1280, 1408
C_DQ, C_DK, C_DV, C_RQ, C_RK, C_RV, C_RG, C_GL, C_END = (
    1536, 2048, 2560, 3072, 3328, 3584, 4096, 4608, 8704)
O_CQ, O_CKV, O_KPE, O_GQ, O_GK, O_GV = 0, 384, 640, 672, 1184, 1312
O_RQ, O_RK, O_RV, O_RG, O_DQ, O_DK, O_DV, O_GL, O_END = (
    1440, 1696, 1952, 2464, 2976, 3488, 4000, 4512, 8608)
GQA_ORDER = (0, 4, 1, 5, 2, 6, 3, 7)

KVM_W = 8 * 256
GKV_W = 3 * LANES
DKV_W = 512 + 4 * 256
LOG2E = 1.4426950408889634
TM = 256
TM_MERGE = 512
TQ = 256
TM_MOE_PRE = 512
TMX = 256
W_SLOTS = 4


def _cparams(sem):
    return pltpu.CompilerParams(dimension_semantics=sem, vmem_limit_bytes=VMEM_LIMIT)


def _const_spec(shape):
    nd = len(shape)
    return pl.BlockSpec(shape, lambda *_: (0,) * nd)


def _layer_spec(l, shape):
    nd = len(shape)
    return pl.BlockSpec((None,) + tuple(shape), lambda *_: (l,) + (0,) * nd)


def _rms(x, g):
    return x * lax.rsqrt(jnp.mean(x * x, axis=-1, keepdims=True) + EPS) * g


def _dot(a, b):
    return jnp.dot(a, b, preferred_element_type=F32)


def _dot_nt(a, b):
    return lax.dot_general(a, b, (((1,), (1,)), ((), ())), preferred_element_type=F32)


def _silu(x):
    return x * jax.nn.sigmoid(x)


def _lane_iota(shape):
    return lax.broadcasted_iota(jnp.int32, shape, len(shape) - 1)


def _seg_meansq(x, bd_ref, width):
    sq = x * x
    hi = sq.astype(BF16)
    lo = (sq - hi.astype(F32)).astype(BF16)
    bd = bd_ref[0:width, 0:width]
    return (_dot(hi, bd) + _dot(lo, bd)) * (1.0 / GQA_HD)


def _rope(x, cos, sin_signed, half):
    width = x.shape[-1]
    first = (_lane_iota(x.shape) % (2 * half)) < half
    partner = jnp.where(first, pltpu.roll(x, width - half, 1), pltpu.roll(x, half, 1))
    return x * cos + partner * sin_signed


def _tile_lanes(t, reps):
    return t if reps == 1 else jnp.concatenate([t] * reps, axis=1)


def _put_layer(o_ref, val, whole_stack):
    if whole_stack:
        o_ref[0] = val
        for k in range(1, o_ref.shape[0]):
            o_ref[k] = jnp.zeros_like(val)
    else:
        o_ref[...] = val


def _store_kvm(kvm_ref, kv, kpe_b):
    ones = jnp.ones(kpe_b.shape, BF16)
    for p in range(4):
        kvm_ref[:, p * 256:p * 256 + LANES] = kv[:, p * LANES:(p + 1) * LANES].astype(BF16)
        kvm_ref[:, p * 256 + LANES:(p + 1) * 256] = kpe_b
        kvm_ref[:, 1024 + p * 256:1024 + p * 256 + LANES] = (
            kv[:, 512 + p * LANES:512 + (p + 1) * LANES].astype(BF16))
        kvm_ref[:, 1024 + p * 256 + LANES:1024 + (p + 1) * 256] = ones


def _mod_kernel(c_ref, w_ref, b_ref, o_ref):
    a = _silu(c_ref[...]).astype(BF16)
    o_ref[...] = _dot(a, w_ref[...].astype(BF16)) + b_ref[...]


def _mod_call(l, cond, w_mod, b_mod):
    n_l, _, n = w_mod.shape
    tn = 1536
    return pl.pallas_call(
        _mod_kernel,
        grid=(n // tn,),
        in_specs=[_const_spec(cond.shape),
                  pl.BlockSpec((None, D, tn), lambda j: (l, 0, j)),
                  pl.BlockSpec((None, 1, tn), lambda j: (l, 0, j))],
        out_specs=pl.BlockSpec((cond.shape[0], tn), lambda j: (0, j)),
        out_shape=jax.ShapeDtypeStruct((cond.shape[0], n), F32),
        compiler_params=_cparams(("arbitrary",)),
        name="mod",
    )(cond, w_mod, b_mod.reshape(n_l, 1, n))


def _inprep_kernel(latent, n_aliased, *refs):
    (x_ref, mod_ref, gpre_ref, win_ref, gmq_ref, wuq_ref, gmkv_ref, wukv_ref,
     ggq_ref, ggk_ref, bd_ref) = refs[:11]
    refs = refs[11:]
    if latent:
        cos64_ref, sin64_ref, cospe_ref, sinpe_ref = refs[:4]
        refs = refs[4:]
    refs = refs[n_aliased:]
    first_layer = n_aliased == 0
    (qm_ref, kvm_ref, gqo_ref, gkv_ref, dqo_ref, dkv_ref, ret_ref, rg_ref, gate_ref) = refs[:9]
    refs = refs[9:]
    if not latent:
        ckv_o, kpe_o, gk_o, gv_o, dk_o, dv_o = refs

    x = x_ref[...]
    mod = mod_ref[...]
    sh1 = mod[:, 0:D]
    sc1 = mod[:, D:2 * D]
    hb = (_rms(x, gpre_ref[...]) * (1.0 + sc1) + sh1).astype(BF16)

    def z(a, b):
        return _dot_nt(hb, win_ref[a:b, :])

    if latent:
        cos64, sin64 = cos64_ref[...], sin64_ref[...]
        cospe, sinpe = cospe_ref[...], sinpe_ref[...]

    cqn = _rms(z(O_CQ, O_CKV), gmq_ref[...]).astype(BF16)
    q = _dot(cqn, wuq_ref[...]) * ((MLA_NOPE + MLA_ROPE) ** -0.5 * LOG2E)
    q_nope, q_pe = q[:, 0:512], q[:, 512:768]
    if latent:
        q_pe = _rope(q_pe, _tile_lanes(cospe, 2), _tile_lanes(sinpe, 2), MLA_ROPE // 4)
    qm_ref[:, 0:512] = q_nope.astype(BF16)
    qm_ref[:, 512:768] = q_pe.astype(BF16)

    ckvn = _rms(z(O_CKV, O_KPE), gmkv_ref[...])
    kv = _dot(ckvn.astype(BF16), wukv_ref[...])
    kpe4 = _dot_nt(hb, jnp.concatenate([win_ref[O_KPE:O_GQ, :]] * 4, axis=0))
    if latent:
        kpe4 = _rope(kpe4, cospe, sinpe, MLA_ROPE // 4)
    else:
        _put_layer(ckv_o, ckvn, first_layer)
        _put_layer(kpe_o, kpe4[:, 0:MLA_ROPE], first_layer)
    _store_kvm(kvm_ref, kv, kpe4.astype(BF16))

    gq = _dot_nt(hb, jnp.concatenate(
        [win_ref[O_GQ + h * GQA_HD:O_GQ + (h + 1) * GQA_HD, :] for h in GQA_ORDER], axis=0))
    gq = gq * lax.rsqrt(_seg_meansq(gq, bd_ref, 512) + EPS) * ggq_ref[...]
    gk = z(O_GK, O_GV)
    gk = gk * lax.rsqrt(_seg_meansq(gk, bd_ref, LANES) + EPS) * ggk_ref[...]
    gv = z(O_GV, O_RQ)
    if latent:
        gq = _rope(gq, _tile_lanes(cos64, 4), _tile_lanes(sin64, 4), GQA_HD // 4)
        gk = _rope(gk, cos64, sin64, GQA_HD // 4)
    else:
        _put_layer(gk_o, jnp.transpose(gk), first_layer)
        _put_layer(gv_o, jnp.transpose(gv), first_layer)
    gqo_ref[...] = (gq * (GQA_HD ** -0.5 * LOG2E)).astype(BF16)
    gkv_ref[:, 0:LANES] = gk.astype(BF16)
    gkv_ref[:, LANES:2 * LANES] = gv.astype(BF16)
    gkv_ref[:, 2 * LANES:3 * LANES] = jnp.ones(gv.shape, BF16)

    dq = z(O_DQ, O_DK)
    dk = z(O_DK, O_DV)
    dv = z(O_DV, O_GL)
    if latent:
        dq = _rope(dq, _tile_lanes(cos64, 4), _tile_lanes(sin64, 4), DIFF_D // 4)
        dk = _rope(dk, _tile_lanes(cos64, 4), _tile_lanes(sin64, 4), DIFF_D // 4)
    else:
        _put_layer(dk_o, jnp.transpose(dk), first_layer)
        _put_layer(dv_o, dv, first_layer)
    dqo_ref[...] = (dq * (DIFF_D ** -0.5 * LOG2E)).astype(BF16)
    dkv_ref[:, 0:512] = dk.astype(BF16)
    for h in range(DIFF_HEADS):
        dkv_ref[:, 512 + h * 256:512 + h * 256 + LANES] = dv[:, h * LANES:(h + 1) * LANES].astype(BF16)
        dkv_ref[:, 512 + h * 256 + LANES:512 + (h + 1) * 256] = jnp.ones((dv.shape[0], LANES), BF16)

    ret_ref[:, 0:256] = z(O_RQ, O_RK).astype(BF16)
    ret_ref[:, 256:512] = (z(O_RK, O_RV) * (RET_DK ** -0.5)).astype(BF16)
    ret_ref[:, 512:1024] = z(O_RV, O_RG).astype(BF16)
    rg_ref[...] = z(O_RG, O_DQ).astype(BF16)

    for n in range(N_BRANCH):
        gate_ref[:, n * D:(n + 1) * D] = jax.nn.sigmoid(
            z(O_GL + n * D, O_GL + (n + 1) * D)).astype(BF16)


def _inprep_call(latent, l, x, mod3, mod_row, lw, tabs, t_len, prev_caches=None):
    n_tok = x.shape[0]
    nblk = n_tok // TM
    blk_per_seq = t_len // TM

    def tok(w):
        return pl.BlockSpec((TM, w), lambda i: (i, 0))

    in_specs = [tok(D),
                pl.BlockSpec((None, 1, 6 * D), lambda i: (mod_row(i * TM), 0, 0)),
                _layer_spec(l, (1, D)),
                pl.BlockSpec((None, O_END, D), lambda i: (l, 0, 0), pipeline_mode=pl.Buffered(1)),
                _layer_spec(l, (1, MLA_Q_LORA)), _layer_spec(l, (MLA_Q_LORA, 768)),
                _layer_spec(l, (1, MLA_KV_LORA)), _layer_spec(l, (MLA_KV_LORA, 1024)),
                _layer_spec(l, (1, 512)), _layer_spec(l, (1, LANES)), _const_spec((512, 512))]
    args = [x, mod3, lw["g_pre1"], lw["w_in"], lw["g_mla_q"], lw["w_uq"], lw["g_mla_kv"],
            lw["w_ukv"], lw["g_gqa_q"], lw["g_gqa_k"], lw["bd"]]
    if latent:
        tab_spec = pl.BlockSpec((TM, LANES), lambda i: (i % blk_per_seq, 0))
        in_specs += [tab_spec] * 4
        args += list(tabs)
    widths = [768, KVM_W, 512, GKV_W, 512, DKV_W, 1024, 512, 4 * D]
    out_specs = [tok(w) for w in widths]
    out_shape = [jax.ShapeDtypeStruct((n_tok, w), BF16) for w in widths]
    aliases = {}
    if not latent:
        assert TM == t_len and (prev_caches is None) == (l == 0)
        n_seq = n_tok // t_len
        lead, at = ((DEPTH,), 0) if l == 0 else ((None,), l)

        def row_major(w):
            out_specs.append(pl.BlockSpec(lead + (TM, w), lambda i: (at, i, 0)))
            out_shape.append(jax.ShapeDtypeStruct((DEPTH, n_tok, w), F32))

        row_major(MLA_KV_LORA)
        row_major(MLA_ROPE)
        for w in (LANES, LANES, 512):
            out_specs.append(pl.BlockSpec(lead + (None, w, t_len), lambda i: (at, i, 0, 0)))
            out_shape.append(jax.ShapeDtypeStruct((DEPTH, n_seq, w, t_len), F32))
        row_major(512)
        if prev_caches is not None:
            n_in = len(args)
            in_specs += [pl.BlockSpec(memory_space=pl.ANY)] * len(prev_caches)
            args += list(prev_caches)
            aliases = {n_in + k: len(widths) + k for k in range(len(prev_caches))}
    return pl.pallas_call(
        functools.partial(_inprep_kernel, latent, len(aliases)),
        grid=(nblk,),
        in_specs=in_specs, out_specs=out_specs, out_shape=out_shape,
        input_output_aliases=aliases,
        compiler_params=_cparams(("arbitrary",)),
        name="inprep_lat" if latent else "inprep_ctx",
    )(*args)


def _pastkv_kernel(ckv_ref, kpe_ref, wukv_ref, o_ref):
    kv = _dot(ckv_ref[...].astype(BF16), wukv_ref[...])
    _store_kvm(o_ref, kv, kpe_ref[...].astype(BF16))


def _pastkv_call(l, ckv, kpe4, w_ukv):
    n = ckv.shape[0]
    return pl.pallas_call(
        _pastkv_kernel,
        grid=(n // TM,),
        in_specs=[pl.BlockSpec((TM, MLA_KV_LORA), lambda i: (i, 0)),
                  pl.BlockSpec((TM, LANES), lambda i: (i, 0)),
                  _layer_spec(l, (MLA_KV_LORA, 1024))],
        out_specs=pl.BlockSpec((TM, KVM_W), lambda i: (i, 0)),
        out_shape=jax.ShapeDtypeStruct((n, KVM_W), BF16),
        compiler_params=_cparams(("arbitrary",)),
        name="pastkv",
    )(ckv, kpe4, w_ukv)


def _softmax_pv(s, v_ones):
    m = jnp.max(s, axis=-1, keepdims=True)
    p = jnp.exp2(s - m).astype(BF16)
    o = _dot(p, v_ones)
    return o[:, 0:LANES] / o[:, LANES:2 * LANES]


def _attn_kernel(lam_init, n_past, qm_ref, kvm_ref, gq_ref, gkv_ref, dq_ref, dkv_ref, *refs):
    if n_past:
        past_refs, refs = refs[:3], refs[3:]
        lam_ref, gdiff_ref, o_ref = refs[:3]
        joined = refs[3:]

        @pl.when(pl.program_id(1) == 0)
        def _():
            for dst, past, new in zip(joined, past_refs, (kvm_ref, gkv_ref, dkv_ref)):
                dst[0:n_past, :] = past[...]
                dst[n_past:, :] = new[...]

        kvm_ref, gkv_ref, dkv_ref = joined
    else:
        lam_ref, gdiff_ref, o_ref = refs
    tq = qm_ref.shape[0]
    lane = _lane_iota((tq, LANES))
    low = lane < HALF_LANES
    zero = jnp.zeros((tq, LANES), BF16)

    for p in range(MLA_HEADS // 2):
        qn = qm_ref[:, p * LANES:(p + 1) * LANES]
        g = p // 2
        qpe = qm_ref[:, 512 + g * LANES:512 + (g + 1) * LANES]
        kk = kvm_ref[:, p * 256:(p + 1) * 256]
        vv = kvm_ref[:, 1024 + p * 256:1024 + (p + 1) * 256]
        outs = []
        for half in range(2):
            h = 2 * p + half
            slot = h % 4
            in_slot = (lane >= slot * MLA_ROPE) & (lane < (slot + 1) * MLA_ROPE)
            lhs = jnp.concatenate(
                [jnp.where(low if half == 0 else ~low, qn, zero),
                 jnp.where(in_slot, qpe, zero)], axis=1)
            outs.append(_softmax_pv(_dot_nt(lhs, kk), vv))
        o_ref[:, p * LANES:(p + 1) * LANES] = jnp.where(low, outs[0], outs[1]).astype(BF16)

    kk = gkv_ref[:, 0:LANES]
    vv = gkv_ref[:, LANES:3 * LANES]
    for g in range(GQA_HEADS // 2):
        qg = gq_ref[:, g * LANES:(g + 1) * LANES]
        o_lo = _softmax_pv(_dot_nt(jnp.where(low, qg, zero), kk), vv)
        o_hi = _softmax_pv(_dot_nt(jnp.where(low, zero, qg), kk), vv)
        o_ref[:, 512 + g * LANES:512 + (g + 1) * LANES] = jnp.where(low, o_lo, o_hi).astype(BF16)

    lp = lam_ref[...]
    lam = (jnp.exp(jnp.sum(lp[0:1] * lp[1:2], axis=-1, keepdims=True))
           - jnp.exp(jnp.sum(lp[2:3] * lp[3:4], axis=-1, keepdims=True)) + lam_init)
    for h in range(DIFF_HEADS):
        qh = dq_ref[:, h * LANES:(h + 1) * LANES]
        kk = dkv_ref[:, h * LANES:(h + 1) * LANES]
        vv = dkv_ref[:, 512 + h * 256:512 + (h + 1) * 256]
        a1 =_softmax_pv(_dot_nt(jnp.where(low, qh, zero), kk), vv)
        a2 = _softmax_pv(_dot_nt(jnp.where(low, zero, qh), kk), vv)
        od = _rms(a1 - lam * a2, gdiff_ref[...]) * (1.0 - lam_init)
        o_ref[:, 1024 + h * LANES:1024 + (h + 1) * LANES] = od.astype(BF16)


def _attn_call(l, lam_init, qm, kvm, gq, gkv, dq, dkv, lam_p, g_diff, n_b, t_len, past=None):
    nq = t_len // TQ
    n_past = 0 if past is None else past[0].shape[0] // n_b

    def qspec(w):
        return pl.BlockSpec((TQ, w), lambda b, i: (b * nq + i, 0))

    def kspec(w, rows=t_len):
        return pl.BlockSpec((rows, w), lambda b, i: (b, 0))

    in_specs = [qspec(768), kspec(KVM_W), qspec(512), kspec(GKV_W), qspec(512), kspec(DKV_W)]
    args = [qm, kvm, gq, gkv, dq, dkv]
    scratch = []
    if n_past:
        in_specs += [kspec(KVM_W, n_past), kspec(GKV_W, n_past), kspec(DKV_W, n_past)]
        args += list(past)
        scratch = [pltpu.VMEM((n_past + t_len, w), BF16) for w in (KVM_W, GKV_W, DKV_W)]
    in_specs += [_layer_spec(l, (4, DIFF_D)), _layer_spec(l, (1, DIFF_DV))]
    args += [lam_p, g_diff]
    return pl.pallas_call(
        functools.partial(_attn_kernel, lam_init, n_past),
        grid=(n_b, nq),
        in_specs=in_specs,
        out_specs=qspec(3 * BRANCH_W),
        out_shape=jax.ShapeDtypeStruct((n_b * t_len, 3 * BRANCH_W), BF16),
        scratch_shapes=scratch,
        compiler_params=_cparams(("arbitrary", "arbitrary")),
        name="attn",
    )(*args)


def _log_sigmoid(x):
    return jnp.minimum(x, 0.0) - jnp.log(1.0 + jnp.exp(-jnp.abs(x)))


def _log_gamma(dec_ref, l, d, h):
    return _log_sigmoid(jnp.full((1, 1), dec_ref[l, d, h], F32))


def _ret_kernel(latent, l, t_len, dec_ref, q_ref, k_ref, v_ref, rg_ref, gret_ref, *refs):
    if latent:
        s0_ref, o_ref = refs
    else:
        o_ref, st_ref = refs[-2:]
    tq = q_ref.shape[0]
    t0 = pl.program_id(1) * tq
    lane = _lane_iota((tq, LANES))
    low = lane < HALF_LANES
    zero = jnp.zeros((tq, LANES), BF16)
    t_idx = (t0 + lax.broadcasted_iota(jnp.int32, (tq, t_len), 0)).astype(F32)
    s_idx = lax.broadcasted_iota(jnp.int32, (tq, t_len), 1).astype(F32)
    dist = t_idx - s_idx
    past = dist >= 0
    diag = jnp.where(dist == 0, 1.0, 0.0)
    t_col = (t0 + lax.broadcasted_iota(jnp.int32, (tq, 1), 0)).astype(F32)

    def lg(d, h):
        return _log_gamma(dec_ref, l, d, h)

    for h in range(RET_HEADS):
        p, half = h // 2, h % 2
        qp = q_ref[:, p * LANES:(p + 1) * LANES]
        qm = jnp.where(low if half == 0 else ~low, qp, zero)
        kp = k_ref[:, p * LANES:(p + 1) * LANES]
        vh = v_ref[:, h * LANES:(h + 1) * LANES]
        lgf, lgb = lg(0, h), lg(1, h)
        dmask = jnp.exp(jnp.where(past, lgf, -lgb) * dist) + diag
        o = _dot((_dot_nt(qm, kp) * dmask).astype(BF16), vh)
        if latent:
            sf =s0_ref[0, p].astype(BF16)
            sb = s0_ref[1, p].astype(BF16)
            o = o + _dot(qm, sf) * jnp.exp(lgf * (t_col + 1.0))
            o = o + _dot(qm, sb) * jnp.exp(lgb * (float(t_len) - t_col))
        mu = jnp.mean(o, axis=-1, keepdims=True)
        oc = o - mu
        y = oc * lax.rsqrt(jnp.mean(oc * oc, axis=-1, keepdims=True) + EPS)
        y = y * gret_ref[:, h * LANES:(h + 1) * LANES]
        rg = rg_ref[:, h * LANES:(h + 1) * LANES].astype(F32)
        o_ref[:, h * LANES:(h + 1) * LANES] = (y * _silu(rg)).astype(BF16)

    if not latent:
        s_col = lax.broadcasted_iota(jnp.int32, (t_len, 1), 0).astype(F32)
        lane_t = _lane_iota((1, LANES)) < HALF_LANES
        for p in range(RET_HEADS // 2):
            kp = k_ref[:, p * LANES:(p + 1) * LANES].astype(F32)
            for d in range(2):
                lg_lane = jnp.where(lane_t, lg(d, 2 * p), lg(d, 2 * p + 1))
                expo = (float(t_len) - 1.0 - s_col) if d == 0 else s_col
                kdec_t = jnp.transpose(kp * jnp.exp(lg_lane * expo)).astype(BF16)
                for half in range(2):
                    h = 2 * p + half
                    st = _dot(kdec_t, v_ref[:, h * LANES:(h + 1) * LANES])
                    st = st[half * RET_DK:(half + 1) * RET_DK, :]
                    if l == 0:
                        st_ref[0, d, h] = st
                        for k in range(1, st_ref.shape[0]):
                            st_ref[k, d, h] = jnp.zeros_like(st)
                    else:
                        st_ref[d, h] = st


def _ret_call(latent, l, dec, ret, rg, g_ret, s0, n_b, t_len, prev_state=None):
    nq = t_len // TQ
    aliases = {}
    assert latent or nq == 1
    in_specs = [pl.BlockSpec(memory_space=pltpu.SMEM),
                pl.BlockSpec((TQ, 256), lambda b, i: (b * nq + i, 0)),
                pl.BlockSpec((t_len, 256), lambda b, i: (b, 1)),
                pl.BlockSpec((t_len, 512), lambda b, i: (b, 1)),
                pl.BlockSpec((TQ, 512), lambda b, i: (b * nq + i, 0)),
                _layer_spec(l, (1, 512))]
    args = [dec, ret, ret, ret, rg, g_ret]
    out_specs = [pl.BlockSpec((TQ, 512), lambda b, i: (b * nq + i, 0))]
    out_shape = [jax.ShapeDtypeStruct((n_b * t_len, 512), BF16)]
    if latent:
        in_specs.append(pl.BlockSpec((None, 2, 2, LANES, LANES), lambda b, i: (b, 0, 0, 0, 0)))
        args.append(s0)
    else:
        assert (prev_state is None) == (l == 0)
        lead, at = ((DEPTH,), 0) if l == 0 else ((None,), l)
        out_specs.append(pl.BlockSpec(lead + (None, 2, RET_HEADS, RET_DK, RET_DV),
                                      lambda b, i: (at, b, 0, 0, 0, 0)))
        out_shape.append(jax.ShapeDtypeStruct((DEPTH, n_b, 2, RET_HEADS, RET_DK, RET_DV), F32))
        if prev_state is not None:
            aliases = {len(args): 1}
            in_specs.append(pl.BlockSpec(memory_space=pl.ANY))
            args.append(prev_state)
    return pl.pallas_call(
        functools.partial(_ret_kernel, latent, l, t_len),
        grid=(n_b, nq),
        in_specs=in_specs, out_specs=out_specs, out_shape=out_shape,
        input_output_aliases=aliases,
        compiler_params=_cparams(("arbitrary", "arbitrary")),
        name="ret_lat" if latent else "ret_ctx",
    )(*args)


def _merge_kernel(x_ref, mod_ref, br_ref, or_ref, gate_ref, wbr_ref, wout_ref, gpost_ref, o_ref):
    merged = None
    for n in range(N_BRANCH):
        if n < 2:
            b = br_ref[:, n * BRANCH_W:(n + 1) * BRANCH_W]
        elif n == 2:
            b = or_ref[...]
        else:
            b = br_ref[:, 2 * BRANCH_W:3 * BRANCH_W]
        t = gate_ref[:, n * D:(n + 1) * D].astype(F32) * _dot(b, wbr_ref[n])
        merged = t if merged is None else merged + t
    out = _dot(merged.astype(BF16), wout_ref[...])
    g1 = mod_ref[...][:, 2 * D:3 * D]
    o_ref[...] = x_ref[...] + g1 * _rms(out, gpost_ref[...])


def _merge_call(l, x, mod3, mod_row, br, o_r, gates, lw):
    n_tok = x.shape[0]

    def tok(w):
        return pl.BlockSpec((TM_MERGE, w), lambda i: (i, 0))

    return pl.pallas_call(
        _merge_kernel,
        grid=(n_tok // TM_MERGE,),
        in_specs=[tok(D), pl.BlockSpec((None, 1, 6 * D), lambda i: (mod_row(i * TM_MERGE), 0, 0)),
                  tok(3 * BRANCH_W), tok(BRANCH_W), tok(4 * D),
                  _layer_spec(l, (N_BRANCH, BRANCH_W, D)), _layer_spec(l, (D, D)),
                  _layer_spec(l, (1, D))],
        out_specs=tok(D),
        out_shape=jax.ShapeDtypeStruct((n_tok, D), F32),
        compiler_params=_cparams(("arbitrary",)),
        name="merge",
    )(x, mod3, br, o_r, gates, lw["w_br"], lw["w_out"], lw["g_post1"])


def _route(logits_t, bias):
    n = logits_t.shape[1]
    scores = jax.nn.sigmoid(logits_t)
    sel = scores + bias
    neg = -jnp.inf
    sub = lax.broadcasted_iota(jnp.int32, (GROUP_SIZE, n), 0)
    grp = []
    for g in range(N_GROUPS):
        blk = sel[g * GROUP_SIZE:(g + 1) * GROUP_SIZE]
        m1 = jnp.max(blk, axis=0, keepdims=True)
        i1 = jnp.min(jnp.where(blk == m1, sub, GROUP_SIZE), axis=0, keepdims=True)
        m2 = jnp.max(jnp.where(sub == i1, neg, blk), axis=0, keepdims=True)
        grp.append(m1 + m2)
    parts = []
    for g in range(N_GROUPS):
        beaten = jnp.zeros((1, n), jnp.int32)
        for o in range(N_GROUPS):
            if o == g:
                continue
            wins = (grp[o] > grp[g]) | (grp[o] == grp[g]) if o < g else (grp[o] > grp[g])
            beaten = beaten + wins.astype(jnp.int32)
        keep = beaten < TOPK_GROUPS
        parts.append(jnp.where(keep, sel[g * GROUP_SIZE:(g + 1) * GROUP_SIZE], neg))
    cur = jnp.concatenate(parts, axis=0)
    eidx = lax.broadcasted_iota(jnp.int32, (N_EXPERTS, n), 0)
    hits, ids, ws = [], [], []
    for _ in range(TOP_K):
        m = jnp.max(cur, axis=0, keepdims=True)
        i = jnp.min(jnp.where(cur == m, eidx, N_EXPERTS), axis=0, keepdims=True)
        hit = eidx == i
        hits.append(hit)
        ids.append(i)
        ws.append(jnp.sum(jnp.where(hit, scores, 0.0), axis=0, keepdims=True))
        cur = jnp.where(hit, neg, cur)
    wsum = ws[0] + ws[1] + ws[2] + ws[3]
    return hits, ids, [w / wsum * ROUTE_SCALE for w in ws]


U32 = jnp.uint32
HIGH16 = np.uint32(0xFFFF0000)


def _bf16_bits(v):
    return lax.bitcast_convert_type(v.astype(BF16).astype(F32), U32)


def _pack_rows(v):
    return (_bf16_bits(v[:, 0:D // 2]) >> 16) | _bf16_bits(v[:, D // 2:D])


def _unpack_rows(p):
    lo = lax.bitcast_convert_type(p << 16, F32)
    hi = lax.bitcast_convert_type(p & HIGH16, F32)
    return jnp.concatenate([lo, hi], axis=1)


def _moe_pre_kernel(x_ref, mod_ref, gpre_ref, wr_ref, br_ref, tri_ref,
                    hp_ref, eidx_ref, rank_ref, comb_ref, cnt_ref, run_ref):
    tm = x_ref.shape[0]

    @pl.when(pl.program_id(0) == 0)
    def _():
        run_ref[...] = jnp.zeros_like(run_ref)

    mod = mod_ref[...]
    sh2, sc2 = mod[:, 3 * D:4 * D], mod[:, 4 * D:5 * D]
    h = _rms(x_ref[...], gpre_ref[...]) * (1.0 + sc2) + sh2
    hp_ref[...] = _pack_rows(h)
    hb = h.astype(BF16)
    h_lo = (h - hb.astype(F32)).astype(BF16)
    wr = wr_ref[...]
    wr_hi = wr.astype(BF16)
    wr_lo = (wr - wr_hi.astype(F32)).astype(BF16)
    logits_t = _dot_nt(wr_hi, hb) + _dot_nt(wr_hi, h_lo) + _dot_nt(wr_lo, hb)
    hits, ids, ws = _route(logits_t, br_ref[...])

    picked = jnp.zeros((N_EXPERTS, tm), F32)
    for hit in hits:
        picked = jnp.where(hit, 1.0, picked)
    before = _dot(picked.astype(BF16), tri_ref[...]) + run_ref[:, 0:1]
    sub8 = lax.broadcasted_iota(jnp.int32, (8, tm), 0)
    comb8 = jnp.zeros((8, tm), F32)
    for k in range(TOP_K):
        rank = jnp.sum(jnp.where(hits[k], before, 0.0), axis=0, keepdims=True)
        eidx_ref[k:k + 1, :] = ids[k]
        rank_ref[k:k + 1, :] = rank.astype(jnp.int32)
        comb8 = jnp.where(sub8 == k, ws[k], comb8)
    comb_ref[...] = jnp.transpose(
        jnp.concatenate([comb8, jnp.zeros((LANES - 8, tm), F32)], axis=0))
    run_ref[...] = run_ref[...] + jnp.sum(picked, axis=1, keepdims=True)
    cnt_ref[...] = run_ref[...]


def _moe_pre_call(l, x, mod3, mod_row, lw):
    n_tok = x.shape[0]
    tm = TM_MOE_PRE
    tri = np.arange(tm)
    tri = jnp.asarray(tri[:, None] < tri[None, :], BF16)
    row4 = pl.BlockSpec((TOP_K, tm), lambda i: (0, i))
    return pl.pallas_call(
        _moe_pre_kernel,
        grid=(n_tok // tm,),
        in_specs=[pl.BlockSpec((tm, D), lambda i: (i, 0)),
                  pl.BlockSpec((None, 1, 6 * D), lambda i: (mod_row(i * tm), 0, 0)),
                  _layer_spec(l, (1, D)), _layer_spec(l, (N_EXPERTS, D)),
                  _layer_spec(l, (N_EXPERTS, 1)), _const_spec((tm, tm))],
        out_specs=[pl.BlockSpec((tm, D // 2), lambda i: (i, 0)), row4, row4,
                   pl.BlockSpec((tm, LANES), lambda i: (i, 0)),
                   _const_spec((N_EXPERTS, LANES))],
        out_shape=[jax.ShapeDtypeStruct((n_tok, D // 2), U32),
                   jax.ShapeDtypeStruct((TOP_K, n_tok), jnp.int32),
                   jax.ShapeDtypeStruct((TOP_K, n_tok), jnp.int32),
                   jax.ShapeDtypeStruct((n_tok, LANES), F32),
                   jax.ShapeDtypeStruct((N_EXPERTS, LANES), F32)],
        scratch_shapes=[pltpu.VMEM((N_EXPERTS, LANES), F32)],
        compiler_params=_cparams(("arbitrary",)),
        name="moe_pre",
    )(x, mod3, lw["g_pre2"], lw["w_router_t"], lw["b_router"], tri)


def _moe_plan_kernel(eidx_ref, rank_ref, cnt_ref, dest_ref, te_ref, tv_ref, tn_ref):
    tm = eidx_ref.shape[1]
    cnt = cnt_ref[...]
    padded = jnp.ceil(cnt * (1.0 / TMX)) * TMX
    row = lax.broadcasted_iota(jnp.int32, cnt.shape, 0)
    incl = padded
    shift = 1
    while shift < N_EXPERTS:
        incl = incl + jnp.where(row >= shift, pltpu.roll(incl, shift, 0), 0.0)
        shift *= 2
    start = (incl - padded)[:, 0:1]
    end = incl[:, 0:1]
    erow = lax.broadcasted_iota(jnp.int32, (N_EXPERTS, tm), 0)
    for k in range(TOP_K):
        mine = erow == eidx_ref[k:k + 1, :]
        base = jnp.sum(jnp.where(mine, start, 0.0), axis=0, keepdims=True)
        dest_ref[k:k + 1, :] = rank_ref[k:k + 1, :] + base.astype(jnp.int32)

    @pl.when(pl.program_id(0) == 0)
    def _():
        tile0 = (_lane_iota((1, LANES)) * TMX).astype(F32)
        owner = jnp.sum(jnp.where(end <= tile0, 1.0, 0.0), axis=0, keepdims=True)
        owner = jnp.minimum(owner, N_EXPERTS - 1.0)
        erow_t = lax.broadcasted_iota(jnp.int32, (N_EXPERTS, LANES), 0).astype(F32)
        left = jnp.sum(jnp.where(erow_t == owner, cnt[:, 0:1] - (tile0 - start), 0.0),
                       axis=0, keepdims=True)
        te_ref[...] = owner.astype(jnp.int32)
        tv_ref[...] = jnp.clip(left, 0.0, float(TMX)).astype(jnp.int32)
        tn_ref[...] = jnp.full(tn_ref.shape, N_EXPERTS, jnp.int32)
        nxt = owner
        for k in range(W_SLOTS - 1):
            later = (erow_t > nxt) & (cnt[:, 0:1] > 0.0)
            nxt = jnp.min(jnp.where(later, erow_t, float(N_EXPERTS)), axis=0, keepdims=True)
            tn_ref[k:k + 1, :] = nxt.astype(jnp.int32)
        n_used = jnp.sum(jnp.where(left > 0.0, 1.0, 0.0), axis=1, keepdims=True)
        tn_ref[7:8, :] = jnp.minimum(tile0 * (1.0 / TMX), n_used - 1.0).astype(jnp.int32)


def _moe_plan_call(eidx, rank, cnt):
    n_tok = eidx.shape[1]
    tm = TM_MOE_PRE
    row4 = pl.BlockSpec((TOP_K, tm), lambda i: (0, i))
    tiles = jax.ShapeDtypeStruct((1, LANES), jnp.int32)
    return pl.pallas_call(
        _moe_plan_kernel,
        grid=(n_tok // tm,),
        in_specs=[row4, row4, _const_spec((N_EXPERTS, LANES))],
        out_specs=[row4, _const_spec((1, LANES)), _const_spec((1, LANES)), _const_spec((8, LANES))],
        out_shape=[jax.ShapeDtypeStruct((TOP_K, n_tok), jnp.int32), tiles, tiles,
                   jax.ShapeDtypeStruct((8, LANES), jnp.int32)],
        compiler_params=_cparams(("arbitrary",)),
        name="moe_plan",
    )(eidx, rank, cnt)


def _experts_kernel(l, te_ref, tv_ref, tn_ref, xs_ref, wgu_hbm, wdn_hbm, ys_ref,
                    wgu_f, wdn_f, wgu_b, wdn_b, sem, group_ref):
    j = pl.program_id(0)
    valid = tv_ref[j]
    expert = te_ref[j]

    def fetch(e, slot):
        return (pltpu.make_async_copy(wgu_hbm.at[l, e], wgu_f.at[slot], sem.at[slot, 0]),
                pltpu.make_async_copy(wdn_hbm.at[l, e], wdn_f.at[slot], sem.at[slot, 1]))

    def start_if_any(e, slot):
        @pl.when(e < N_EXPERTS)
        def _():
            for cp in fetch(e, slot):
                cp.start()

    @pl.when(j == 0)
    def _():
        group_ref[0] = 0
        start_if_any(expert, 0)
        for k in range(W_SLOTS - 2):
            start_if_any(tn_ref[k, 0], k + 1)

    first_tile = (j == 0) | (expert != te_ref[jnp.maximum(j - 1, 0)])

    @pl.when(first_tile & (valid > 0))
    def _():
        group = group_ref[0]
        slot = lax.rem(group, W_SLOTS)
        for cp in fetch(expert, slot):
            cp.wait()
        wgu_b[...] = wgu_f[slot].astype(BF16)
        wdn_b[...] = wdn_f[slot].astype(BF16)
        start_if_any(tn_ref[W_SLOTS - 2, j], lax.rem(group + W_SLOTS - 1, W_SLOTS))
        group_ref[0] = group + 1

    @pl.when(valid > 0)
    def _():
        rows = lax.broadcasted_iota(jnp.int32, (TMX, D), 0)
        x = jnp.where(rows < valid, _unpack_rows(xs_ref[...]), 0.0).astype(BF16)
        gu = _dot(x, wgu_b[...])
        a = _silu(gu[:, 0:EXPERT_FF]) * gu[:, EXPERT_FF:2 * EXPERT_FF]
        ys_ref[...] = _pack_rows(_dot(a.astype(BF16), wdn_b[...]))


def _experts_call(l, xs, te, tv, tn, w_gu, w_dn):
    n_tiles = xs.shape[0] // TMX
    grid_spec = pltpu.PrefetchScalarGridSpec(
        num_scalar_prefetch=3,
        grid=(n_tiles,),
        in_specs=[pl.BlockSpec((TMX, D // 2), lambda j, te, tv, tn: (tn[7, j], 0)),
                  pl.BlockSpec(memory_space=pl.ANY), pl.BlockSpec(memory_space=pl.ANY)],
        out_specs=pl.BlockSpec((TMX, D // 2), lambda j, te, tv, tn: (tn[7, j], 0)),
        scratch_shapes=[pltpu.VMEM((W_SLOTS, D, 2 * EXPERT_FF), F32),
                        pltpu.VMEM((W_SLOTS, EXPERT_FF, D), F32),
                        pltpu.VMEM((D, 2 * EXPERT_FF), BF16), pltpu.VMEM((EXPERT_FF, D), BF16),
                        pltpu.SemaphoreType.DMA((W_SLOTS, 2)), pltpu.SMEM((1,), jnp.int32)])
    return pl.pallas_call(
        functools.partial(_experts_kernel, l),
        grid_spec=grid_spec,
        out_shape=jax.ShapeDtypeStruct(xs.shape, U32),
        compiler_params=_cparams(("arbitrary",)),
        name="moe_experts",
    )(te, tv, tn, xs, w_gu, w_dn)


def _moe_post_kernel(x_ref, mod_ref, hp_ref, yg_ref, comb_ref, wsgu_ref, wsdn_ref, gpost_ref,
                     o_ref):
    hb = _unpack_rows(hp_ref[...]).astype(BF16)
    sgu = _dot(hb, wsgu_ref[...])
    sa = _silu(sgu[:, 0:SHARED_FF]) * sgu[:, SHARED_FF:2 * SHARED_FF]
    acc = _dot(sa.astype(BF16), wsdn_ref[...])
    comb = comb_ref[...]
    for k in range(TOP_K):
        acc = acc + comb[:, k:k + 1] * _unpack_rows(yg_ref[k])
    g2 = mod_ref[...][:, 5 * D:6 * D]
    o_ref[...] = x_ref[...] + g2 * _rms(acc, gpost_ref[...])


def _moe_post_call(l, x, mod3, mod_row, hp, yg, comb, lw):
    n_tok = x.shape[0]
    tm = TM_MOE_PRE
    return pl.pallas_call(
        _moe_post_kernel,
        grid=(n_tok // tm,),
        in_specs=[pl.BlockSpec((tm, D), lambda i: (i, 0)),
                  pl.BlockSpec((None, 1, 6 * D), lambda i: (mod_row(i * tm), 0, 0)),
                  pl.BlockSpec((tm, D // 2), lambda i: (i, 0)),
                  pl.BlockSpec((TOP_K, tm, D // 2), lambda i: (0, i, 0)),
                  pl.BlockSpec((tm, LANES), lambda i: (i, 0)),
                  _layer_spec(l, (D, 2 * SHARED_FF)), _layer_spec(l, (SHARED_FF, D)),
                  _layer_spec(l, (1, D))],
        out_specs=pl.BlockSpec((tm, D), lambda i: (i, 0)),
        out_shape=jax.ShapeDtypeStruct((n_tok, D), F32),
        compiler_params=_cparams(("arbitrary",)),
        name="moe_post",
    )(x, mod3, hp, yg, comb, lw["w_sh_gu"], lw["w_sh_down"], lw["g_post2"])


def _moe_call(l, x, mod3, mod_row, lw):
    n_tok = x.shape[0]
    n_slots = -(-(TOP_K * n_tok + N_EXPERTS * (TMX - 1)) // TMX) * TMX
    assert n_slots // TMX <= LANES
    hp, eidx, rank, comb, cnt = _moe_pre_call(l, x, mod3, mod_row, lw)
    dest, te, tv, tn = _moe_plan_call(eidx, rank, cnt)
    dest = dest.reshape(TOP_K * n_tok)
    xs = _sc_scatter_rows(hp, dest, n_slots)
    ys = _experts_call(l, xs, te[0], tv[0], tn, lw["w_exp_gu"], lw["w_exp_down"])
    yg = _sc_gather_rows(ys, dest).reshape(TOP_K, n_tok, D // 2)
    return _moe_post_call(l, x, mod3, mod_row, hp, yg, comb, lw)


SC_CORES, SC_SUBCORES = 2, 16
SC_WORKERS = SC_CORES * SC_SUBCORES


def _sc_gather_rows(table, idx, chunk=64):
    n_out, width = idx.shape[0], table.shape[1]
    per_worker = n_out // SC_WORKERS
    n_chunks = per_worker // chunk
    assert per_worker * SC_WORKERS == n_out and n_chunks * chunk == per_worker
    mesh = plsc.VectorSubcoreMesh(core_axis_name="c", subcore_axis_name="s",
                                  num_cores=SC_CORES, num_subcores=SC_SUBCORES)

    @functools.partial(
        pl.kernel, mesh=mesh,
        out_type=jax.ShapeDtypeStruct((n_out, width), table.dtype),
        scratch_types=[pltpu.VMEM((chunk,), jnp.int32), pltpu.VMEM((chunk, width), table.dtype),
                       pltpu.SemaphoreType.DMA],
        name="sc_gather")
    def gather(table_hbm, idx_hbm, out_hbm, idx_v, rows_v, sem):
        base = (lax.axis_index("s") * SC_CORES + lax.axis_index("c")) * per_worker

        @pl.loop(0, n_chunks)
        def _(j):
            off = base + j * chunk
            pltpu.sync_copy(idx_hbm.at[pl.ds(off, chunk)], idx_v)
            pltpu.async_copy(table_hbm.at[idx_v], rows_v, sem).wait()
            pltpu.sync_copy(rows_v, out_hbm.at[pl.ds(off, chunk)])

    return gather(table, idx)


def _sc_scatter_rows(rows, dest, n_slots, chunk=64):
    n_tok, width = rows.shape
    per_worker = n_tok // SC_WORKERS
    n_chunks = per_worker // chunk
    assert per_worker * SC_WORKERS == n_tok and n_chunks * chunk == per_worker
    mesh = plsc.VectorSubcoreMesh(core_axis_name="c", subcore_axis_name="s",
                                  num_cores=SC_CORES, num_subcores=SC_SUBCORES)

    @functools.partial(
        pl.kernel, mesh=mesh,
        out_type=jax.ShapeDtypeStruct((n_slots, width), rows.dtype),
        scratch_types=[pltpu.VMEM((chunk,), jnp.int32), pltpu.VMEM((chunk, width), rows.dtype)],
        name="sc_scatter")
    def scatter(rows_hbm, dest_hbm, out_hbm, idx_v, rows_v):
        base = (lax.axis_index("s") * SC_CORES + lax.axis_index("c")) * per_worker

        @pl.loop(0, n_chunks)
        def _(j):
            off = base + j * chunk
            pltpu.sync_copy(rows_hbm.at[pl.ds(off, chunk)], rows_v)
            for k in range(TOP_K):
                pltpu.sync_copy(dest_hbm.at[pl.ds(k * n_tok + off, chunk)], idx_v)
                pltpu.sync_copy(rows_v, out_hbm.at[idx_v])

    return scatter(rows, dest)


def _rope_tables(t_len):
    pos = np.arange(t_len)
    row, col = pos // GRID_W, pos % GRID_W

    def tab(r):
        half = r // 2
        freq = ROPE_BASE ** (-np.arange(half, dtype=np.float64) / half)
        sign = np.concatenate([-np.ones(half), np.ones(half)])
        cs, sn = [], []
        for p in (row, col):
            ang = p[:, None].astype(np.float64) * freq[None, :]
            cs.append(np.concatenate([np.cos(ang), np.cos(ang)], axis=1))
            sn.append(np.concatenate([np.sin(ang), np.sin(ang)], axis=1) * sign[None, :])
        return np.concatenate(cs, axis=1), np.concatenate(sn, axis=1)

    c64, s64 = tab(GQA_HD // 2)
    cpe, spe = tab(MLA_ROPE // 2)
    out = (np.tile(c64, (1, 2)), np.tile(s64, (1, 2)), np.tile(cpe, (1, 4)), np.tile(spe, (1, 4)))
    return tuple(jnp.asarray(a, F32) for a in out)


def _prep_weights(p):
    n_l = p["w_in"].shape[0]

    def row(name):
        return p[name].reshape(n_l, 1, -1)

    w_uq = p["w_mla_uq"].reshape(n_l, MLA_Q_LORA, MLA_HEADS, MLA_NOPE + MLA_ROPE)
    w_uq = jnp.concatenate([w_uq[..., :MLA_NOPE].reshape(n_l, MLA_Q_LORA, -1),
                            w_uq[..., MLA_NOPE:].reshape(n_l, MLA_Q_LORA, -1)], axis=-1)
    w_ukv = p["w_mla_ukv"].reshape(n_l, MLA_KV_LORA, MLA_HEADS, MLA_NOPE + MLA_V)
    w_ukv = jnp.concatenate([w_ukv[..., :MLA_NOPE].reshape(n_l, MLA_KV_LORA, -1),
                             w_ukv[..., MLA_NOPE:].reshape(n_l, MLA_KV_LORA, -1)], axis=-1)
    w_br = p["w_br"]
    w_br_gqa = w_br[:, 1].reshape(n_l, GQA_HEADS, GQA_HD, D)[:, jnp.array(GQA_ORDER)]
    w_br = jnp.concatenate([w_br[:, 0:1], w_br_gqa.reshape(n_l, 1, BRANCH_W, D), w_br[:, 2:4]], axis=1)
    blk = np.arange(512) // GQA_HD
    return {
        "g_pre1": row("g_pre1"), "g_post1": row("g_post1"),
        "g_pre2": row("g_pre2"), "g_post2": row("g_post2"),
        "w_in": jnp.swapaxes(p["w_in"], 1, 2).astype(BF16),
        "g_mla_q": row("g_mla_q"), "w_uq": w_uq.astype(BF16),
        "g_mla_kv": row("g_mla_kv"), "w_ukv": w_ukv.astype(BF16),
        "g_gqa_q": jnp.tile(p["g_gqa_q"], (1, GQA_HEADS)).reshape(n_l, 1, -1),
        "g_gqa_k": jnp.tile(p["g_gqa_k"], (1, GQA_KV_HEADS)).reshape(n_l, 1, -1),
        "bd": jnp.asarray(blk[:, None] == blk[None, :], BF16),
        "ret_decay": p["ret_decay"],
        "g_ret": row("g_ret"),
        "diff_lambda": p["diff_lambda"], "g_diff": row("g_diff"),
        "w_br": w_br.astype(BF16), "w_out": p["w_out"].astype(BF16),
        "w_router_t": jnp.swapaxes(p["w_router"], 1, 2),
        "b_router": p["b_router"].reshape(n_l, N_EXPERTS, 1),
        "w_exp_gu": p["w_exp_gu"], "w_exp_down": p["w_exp_down"],
        "w_sh_gu": p["w_sh_gu"].astype(BF16), "w_sh_down": p["w_sh_down"].astype(BF16),
    }


def _mixers(latent, l, x, mod3, mod_row, lw, n_b, t_len, tabs=None, past=None, s0=None,
            prev_cache=None):
    lam_init = 0.8 - 0.6 * math.exp(-0.3 * l)
    outs = _inprep_call(latent, l, x, mod3, mod_row, lw, tabs, t_len,
                        None if prev_cache is None else prev_cache[:6])
    qm, kvm, gq, gkv, dq, dkv, ret, rg, gates = outs[:9]
    br = _attn_call(l, lam_init, qm, kvm, gq, gkv, dq, dkv, lw["diff_lambda"], lw["g_diff"],
                    n_b, t_len, past)
    r = _ret_call(latent, l, lw["ret_decay"], ret, rg, lw["g_ret"], s0, n_b, t_len,
                  None if prev_cache is None else prev_cache[6])
    y = _merge_call(l, x, mod3, mod_row, br, r[0], gates, lw)
    cache = None if latent else tuple(outs[9:]) + (r[1],)
    return y, cache


def kernel(x_prompt, x_sample, cache_mla_ckv, cache_mla_kpe, cache_gqa_k, cache_gqa_v, cache_diff_k, cache_diff_v, state_ret, c, c_ctx, w_mod, b_mod, g_pre1, g_post1, g_pre2, g_post2, w_in, g_mla_q, w_mla_uq, g_mla_kv, w_mla_ukv, g_gqa_q, g_gqa_k, ret_decay, g_ret, diff_lambda, g_diff, w_br, w_out, w_router, b_router, w_exp_gu, w_exp_down, w_sh_gu, w_sh_down):
    params = dict(w_in=w_in, g_pre1=g_pre1, g_post1=g_post1, g_pre2=g_pre2,
                  g_post2=g_post2, g_mla_q=g_mla_q, w_mla_uq=w_mla_uq,
                  g_mla_kv=g_mla_kv, w_mla_ukv=w_mla_ukv, g_gqa_q=g_gqa_q, g_gqa_k=g_gqa_k,
                  ret_decay=ret_decay, g_ret=g_ret, diff_lambda=diff_lambda, g_diff=g_diff,
                  w_br=w_br, w_out=w_out, w_router=w_router, b_router=b_router,
                  w_exp_gu=w_exp_gu, w_exp_down=w_exp_down, w_sh_gu=w_sh_gu, w_sh_down=w_sh_down)
    n_bc, t_c, _ = x_prompt.shape
    n_bl, t_l, _ = x_sample.shape
    p_len = cache_mla_ckv.shape[2]
    tabs = _rope_tables(t_l)
    n_cond = 8
    cond = jnp.concatenate([c_ctx[None, :], c, jnp.zeros((n_cond - 1 - n_bl, D), F32)], axis=0)
    assert t_l % TM_MERGE == 0 and (t_c * n_bc) % TM_MERGE == 0
    assert t_l % TM_MOE_PRE == 0 and (t_c * n_bc) % TM_MOE_PRE == 0

    yp = x_prompt.reshape(n_bc * t_c, D)
    ys = x_sample.reshape(n_bl * t_l, D)
    cache = None
    lw = _prep_weights(params)
    for l in range(DEPTH):
        mod3 = _mod_call(l, cond, w_mod, b_mod).reshape(n_cond, 1, 6 * D)
        yp, cache = _mixers(False, l, yp, mod3, lambda i: 0, lw, n_bc, t_c, prev_cache=cache)
        yp = _moe_call(l, yp, mod3, lambda i: 0, lw)
        past_kvm = _pastkv_call(l, cache_mla_ckv[:, l].reshape(n_bl * p_len, -1),
                                jnp.tile(cache_mla_kpe[:, l].reshape(n_bl * p_len, -1), (1, 4)),
                                lw["w_ukv"])
        past_gkv = jnp.concatenate([cache_gqa_k[:, l].reshape(n_bl * p_len, -1),
                                    cache_gqa_v[:, l].reshape(n_bl * p_len, -1),
                                    jnp.ones((n_bl * p_len, LANES), F32)], axis=-1).astype(BF16)
        past_dv = jnp.concatenate([cache_diff_v[:, l], jnp.ones_like(cache_diff_v[:, l])], axis=-1)
        past_dkv = jnp.concatenate([cache_diff_k[:, l].reshape(n_bl * p_len, -1),
                                    past_dv.reshape(n_bl * p_len, -1)], axis=-1).astype(BF16)
        s0 = state_ret[:, l].reshape(n_bl, 2, RET_HEADS // 2, 2 * RET_DK, RET_DV)
        ys, _ = _mixers(True, l, ys, mod3, lambda t: 1 + t // t_l, lw, n_bl, t_l, tabs=tabs,
                        past=(past_kvm, past_gkv, past_dkv), s0=s0)
        ys = _moe_call(l, ys, mod3, lambda t: 1 + t // t_l, lw)

    ckv, kpe, gk_t, gv_t, dk_t, dv, ret_state = cache

    def rows(a, shape):
        return jnp.swapaxes(a.reshape((DEPTH, n_bc, t_c) + shape), 0, 1)

    def time_minor(a, shape):
        a = a.reshape((DEPTH, n_bc) + shape + (t_c,))
        return jnp.transpose(a, (1, 0, a.ndim - 1) + tuple(range(2, a.ndim - 1)))

    return (yp.reshape(n_bc, t_c, D), ys.reshape(n_bl, t_l, D),
            rows(ckv, (MLA_KV_LORA,)), rows(kpe, (MLA_ROPE,)),
            time_minor(gk_t, (GQA_KV_HEADS, GQA_HD)), time_minor(gv_t, (GQA_KV_HEADS, GQA_HD)),
            time_minor(dk_t, (DIFF_HEADS, 2, DIFF_D)), rows(dv, (DIFF_HEADS, DIFF_DV)),
            jnp.swapaxes(ret_state, 0, 1))
```

```python
import functools
import math

import numpy as np
import jax
import jax.numpy as jnp
from jax import lax
from jax.experimental import pallas as pl
from jax.experimental.pallas import tpu as pltpu
from jax.experimental.pallas import tpu_sc as plsc

F32 = jnp.float32
BF16 = jnp.bfloat16

D = 1024
DEPTH = 2
GRID_W = 64
ROPE_BASE = 10000.0
EPS = 1e-6

MLA_HEADS, MLA_NOPE, MLA_ROPE, MLA_V = 8, 64, 32, 64
MLA_Q_LORA, MLA_KV_LORA = 384, 256
GQA_HEADS, GQA_KV_HEADS, GQA_HD = 8, 2, 64
RET_HEADS, RET_DK, RET_DV = 4, 64, 128
DIFF_HEADS, DIFF_D, DIFF_DV = 4, 64, 128
N_BRANCH, BRANCH_W = 4, 512
N_EXPERTS, TOP_K, N_GROUPS, TOPK_GROUPS = 32, 4, 4, 2
EXPERT_FF, SHARED_FF = 256, 256
ROUTE_SCALE = 2.5
GROUP_SIZE = N_EXPERTS // N_GROUPS

LANES = 128
HALF_LANES = 64
VMEM_LIMIT = 56 * 1024 * 1024

C_CQ, C_CKV, C_KPE, C_GQ, C_GK, C_GV = 0, 384, 640, 768, 1280, 1408
C_DQ, C_DK, C_DV, C_RQ, C_RK, C_RV, C_RG, C_GL, C_END = (
    1536, 2048, 2560, 3072, 3328, 3584, 4096, 4608, 8704)
O_CQ, O_CKV, O_KPE, O_GQ, O_GK, O_GV = 0, 384, 640, 672, 1184, 1312
O_RQ, O_RK, O_RV, O_RG, O_DQ, O_DK, O_DV, O_GL, O_END = (
    1440, 1696, 1952, 2464, 2976, 3488, 4000, 4512, 8608)
GQA_ORDER = (0, 4, 1, 5, 2, 6, 3, 7)

KVM_W = 8 * 256
GKV_W = 3 * LANES
DKV_W = 512 + 4 * 256
LOG2E = 1.4426950408889634
TM = 256
TM_MERGE = 512
TQ = 256
TM_MOE_PRE = 512
TMX = 256
W_SLOTS = 4


def _cparams(sem):
    return pltpu.CompilerParams(dimension_semantics=sem, vmem_limit_bytes=VMEM_LIMIT)


def _const_spec(shape):
    nd = len(shape)
    return pl.BlockSpec(shape, lambda *_: (0,) * nd)


def _layer_spec(l, shape):
    nd = len(shape)
    return pl.BlockSpec((None,) + tuple(shape), lambda *_: (l,) + (0,) * nd)


def _rms(x, g):
    return x * lax.rsqrt(jnp.mean(x * x, axis=-1, keepdims=True) + EPS) * g


def _dot(a, b):
    return jnp.dot(a, b, preferred_element_type=F32)


def _dot_nt(a, b):
    return lax.dot_general(a, b, (((1,), (1,)), ((), ())), preferred_element_type=F32)


def _silu(x):
    return x * jax.nn.sigmoid(x)


def _lane_iota(shape):
    return lax.broadcasted_iota(jnp.int32, shape, len(shape) - 1)


def _seg_meansq(x, bd_ref, width):
    sq = x * x
    hi = sq.astype(BF16)
    lo = (sq - hi.astype(F32)).astype(BF16)
    bd = bd_ref[0:width, 0:width]
    return (_dot(hi, bd) + _dot(lo, bd)) * (1.0 / GQA_HD)


def _rope(x, cos, sin_signed, half):
    width = x.shape[-1]
    first = (_lane_iota(x.shape) % (2 * half)) < half
    partner = jnp.where(first, pltpu.roll(x, width - half, 1), pltpu.roll(x, half, 1))
    return x * cos + partner * sin_signed


def _tile_lanes(t, reps):
    return t if reps == 1 else jnp.concatenate([t] * reps, axis=1)


def _put_layer(o_ref, val, whole_stack):
    if whole_stack:
        o_ref[0] = val
        for k in range(1, o_ref.shape[0]):
            o_ref[k] = jnp.zeros_like(val)
    else:
        o_ref[...] = val


def _store_kvm(kvm_ref, kv, kpe_b):
    ones = jnp.ones(kpe_b.shape, BF16)
    for p in range(4):
        kvm_ref[:, p * 256:p * 256 + LANES] = kv[:, p * LANES:(p + 1) * LANES].astype(BF16)
        kvm_ref[:, p * 256 + LANES:(p + 1) * 256] = kpe_b
        kvm_ref[:, 1024 + p * 256:1024 + p * 256 + LANES] = (
            kv[:, 512 + p * LANES:512 + (p + 1) * LANES].astype(BF16))
        kvm_ref[:, 1024 + p * 256 + LANES:1024 + (p + 1) * 256] = ones


def _mod_kernel(c_ref, w_ref, b_ref, o_ref):
    a = _silu(c_ref[...]).astype(BF16)
    o_ref[...] = _dot(a, w_ref[...].astype(BF16)) + b_ref[...]


def _mod_call(l, cond, w_mod, b_mod):
    n_l, _, n = w_mod.shape
    tn = 1536
    return pl.pallas_call(
        _mod_kernel,
        grid=(n // tn,),
        in_specs=[_const_spec(cond.shape),
                  pl.BlockSpec((None, D, tn), lambda j: (l, 0, j)),
                  pl.BlockSpec((None, 1, tn), lambda j: (l, 0, j))],
        out_specs=pl.BlockSpec((cond.shape[0], tn), lambda j: (0, j)),
        out_shape=jax.ShapeDtypeStruct((cond.shape[0], n), F32),
        compiler_params=_cparams(("arbitrary",)),
        name="mod",
    )(cond, w_mod, b_mod.reshape(n_l, 1, n))


def _inprep_kernel(latent, n_aliased, *refs):
    (x_ref, mod_ref, gpre_ref, win_ref, gmq_ref, wuq_ref, gmkv_ref, wukv_ref,
     ggq_ref, ggk_ref, bd_ref) = refs[:11]
    refs = refs[11:]
    if latent:
        cos64_ref, sin64_ref, cospe_ref, sinpe_ref = refs[:4]
        refs = refs[4:]
    refs = refs[n_aliased:]
    first_layer = n_aliased == 0
    (qm_ref, kvm_ref, gqo_ref, gkv_ref, dqo_ref, dkv_ref, ret_ref, rg_ref, gate_ref) = refs[:9]
    refs = refs[9:]
    if not latent:
        ckv_o, kpe_o, gk_o, gv_o, dk_o, dv_o = refs

    x = x_ref[...]
    mod = mod_ref[...]
    sh1 = mod[:, 0:D]
    sc1 = mod[:, D:2 * D]
    hb = (_rms(x, gpre_ref[...]) * (1.0 + sc1) + sh1).astype(BF16)

    def z(a, b):
        return _dot_nt(hb, win_ref[a:b, :])

    if latent:
        cos64, sin64 = cos64_ref[...], sin64_ref[...]
        cospe, sinpe = cospe_ref[...], sinpe_ref[...]

    cqn = _rms(z(O_CQ, O_CKV), gmq_ref[...]).astype(BF16)
    q = _dot(cqn, wuq_ref[...]) * ((MLA_NOPE + MLA_ROPE) ** -0.5 * LOG2E)
    q_nope, q_pe = q[:, 0:512], q[:, 512:768]
    if latent:
        q_pe = _rope(q_pe, _tile_lanes(cospe, 2), _tile_lanes(sinpe, 2), MLA_ROPE // 4)
    qm_ref[:, 0:512] = q_nope.astype(BF16)
    qm_ref[:, 512:768] = q_pe.astype(BF16)

    ckvn = _rms(z(O_CKV, O_KPE), gmkv_ref[...])
    kv = _dot(ckvn.astype(BF16), wukv_ref[...])
    kpe4 = _dot_nt(hb, jnp.concatenate([win_ref[O_KPE:O_GQ, :]] * 4, axis=0))
    if latent:
        kpe4 = _rope(kpe4, cospe, sinpe, MLA_ROPE // 4)
    else:
        _put_layer(ckv_o, ckvn, first_layer)
        _put_layer(kpe_o, kpe4[:, 0:MLA_ROPE], first_layer)
    _store_kvm(kvm_ref, kv, kpe4.astype(BF16))

    gq = _dot_nt(hb, jnp.concatenate(
        [win_ref[O_GQ + h * GQA_HD:O_GQ + (h + 1) * GQA_HD, :] for h in GQA_ORDER], axis=0))
    gq = gq * lax.rsqrt(_seg_meansq(gq, bd_ref, 512) + EPS) * ggq_ref[...]
    gk = z(O_GK, O_GV)
    gk = gk * lax.rsqrt(_seg_meansq(gk, bd_ref, LANES) + EPS) * ggk_ref[...]
    gv = z(O_GV, O_RQ)
    if latent:
        gq = _rope(gq, _tile_lanes(cos64, 4), _tile_lanes(sin64, 4), GQA_HD // 4)
        gk = _rope(gk, cos64, sin64, GQA_HD // 4)
    else:
        _put_layer(gk_o, jnp.transpose(gk), first_layer)
        _put_layer(gv_o, jnp.transpose(gv), first_layer)
    gqo_ref[...] = (gq * (GQA_HD ** -0.5 * LOG2E)).astype(BF16)
    gkv_ref[:, 0:LANES] = gk.astype(BF16)
    gkv_ref[:, LANES:2 * LANES] = gv.astype(BF16)
    gkv_ref[:, 2 * LANES:3 * LANES] = jnp.ones(gv.shape, BF16)

    dq = z(O_DQ, O_DK)
    dk = z(O_DK, O_DV)
    dv = z(O_DV, O_GL)
    if latent:
        dq = _rope(dq, _tile_lanes(cos64, 4), _tile_lanes(sin64, 4), DIFF_D // 4)
        dk = _rope(dk, _tile_lanes(cos64, 4), _tile_lanes(sin64, 4), DIFF_D // 4)
    else:
        _put_layer(dk_o, jnp.transpose(dk), first_layer)
        _put_layer(dv_o, dv, first_layer)
    dqo_ref[...] = (dq * (DIFF_D ** -0.5 * LOG2E)).astype(BF16)
    dkv_ref[:, 0:512] = dk.astype(BF16)
    for h in range(DIFF_HEADS):
        dkv_ref[:, 512 + h * 256:512 + h * 256 + LANES] = dv[:, h * LANES:(h + 1) * LANES].astype(BF16)
        dkv_ref[:, 512 + h * 256 + LANES:512 + (h + 1) * 256] = jnp.ones((dv.shape[0], LANES), BF16)

    ret_ref[:, 0:256] = z(O_RQ, O_RK).astype(BF16)
    ret_ref[:, 256:512] = (z(O_RK, O_RV) * (RET_DK ** -0.5)).astype(BF16)
    ret_ref[:, 512:1024] = z(O_RV, O_RG).astype(BF16)
    rg_ref[...] = z(O_RG, O_DQ).astype(BF16)

    for n in range(N_BRANCH):
        gate_ref[:, n * D:(n + 1) * D] = jax.nn.sigmoid(
            z(O_GL + n * D, O_GL + (n + 1) * D)).astype(BF16)


def _inprep_call(latent, l, x, mod3, mod_row, lw, tabs, t_len, prev_caches=None):
    n_tok = x.shape[0]
    nblk = n_tok // TM
    blk_per_seq = t_len // TM

    def tok(w):
        return pl.BlockSpec((TM, w), lambda i: (i, 0))

    in_specs = [tok(D),
                pl.BlockSpec((None, 1, 6 * D), lambda i: (mod_row(i * TM), 0, 0)),
                _layer_spec(l, (1, D)),
                pl.BlockSpec((None, O_END, D), lambda i: (l, 0, 0), pipeline_mode=pl.Buffered(1)),
                _layer_spec(l, (1, MLA_Q_LORA)), _layer_spec(l, (MLA_Q_LORA, 768)),
                _layer_spec(l, (1, MLA_KV_LORA)), _layer_spec(l, (MLA_KV_LORA, 1024)),
                _layer_spec(l, (1, 512)), _layer_spec(l, (1, LANES)), _const_spec((512, 512))]
    args = [x, mod3, lw["g_pre1"], lw["w_in"], lw["g_mla_q"], lw["w_uq"], lw["g_mla_kv"],
            lw["w_ukv"], lw["g_gqa_q"], lw["g_gqa_k"], lw["bd"]]
    if latent:
        tab_spec = pl.BlockSpec((TM, LANES), lambda i: (i % blk_per_seq, 0))
        in_specs += [tab_spec] * 4
        args += list(tabs)
    widths = [768, KVM_W, 512, GKV_W, 512, DKV_W, 1024, 512, 4 * D]
    out_specs = [tok(w) for w in widths]
    out_shape = [jax.ShapeDtypeStruct((n_tok, w), BF16) for w in widths]
    aliases = {}
    if not latent:
        assert TM == t_len and (prev_caches is None) == (l == 0)
        n_seq = n_tok // t_len
        lead, at = ((DEPTH,), 0) if l == 0 else ((None,), l)

        def row_major(w):
            out_specs.append(pl.BlockSpec((None,) + lead + (t_len, w), lambda i: (i, at, 0, 0)))
            out_shape.append(jax.ShapeDtypeStruct((n_seq, DEPTH, t_len, w), F32))

        row_major(MLA_KV_LORA)
        row_major(MLA_ROPE)
        for w in (LANES, LANES, 512):
            out_specs.append(pl.BlockSpec((None,) + lead + (w, t_len), lambda i: (i, at, 0, 0)))
            out_shape.append(jax.ShapeDtypeStruct((n_seq, DEPTH, w, t_len), F32))
        row_major(512)
        if prev_caches is not None:
            n_in = len(args)
            in_specs += [pl.BlockSpec(memory_space=pl.ANY)] * len(prev_caches)
            args += list(prev_caches)
            aliases = {n_in + k: len(widths) + k for k in range(len(prev_caches))}
    return pl.pallas_call(
        functools.partial(_inprep_kernel, latent, len(aliases)),
        grid=(nblk,),
        in_specs=in_specs, out_specs=out_specs, out_shape=out_shape,
        input_output_aliases=aliases,
        compiler_params=_cparams(("arbitrary",)),
        name="inprep_lat" if latent else "inprep_ctx",
    )(*args)


def _pastkv_kernel(ckv_ref, kpe_ref, wukv_ref, o_ref):
    kv = _dot(ckv_ref[...].astype(BF16), wukv_ref[...])
    _store_kvm(o_ref, kv, kpe_ref[...].astype(BF16))


def _pastkv_call(l, ckv, kpe4, w_ukv):
    n = ckv.shape[0]
    return pl.pallas_call(
        _pastkv_kernel,
        grid=(n // TM,),
        in_specs=[pl.BlockSpec((TM, MLA_KV_LORA), lambda i: (i, 0)),
                  pl.BlockSpec((TM, LANES), lambda i: (i, 0)),
                  _layer_spec(l, (MLA_KV_LORA, 1024))],
        out_specs=pl.BlockSpec((TM, KVM_W), lambda i: (i, 0)),
        out_shape=jax.ShapeDtypeStruct((n, KVM_W), BF16),
        compiler_params=_cparams(("arbitrary",)),
        name="pastkv",
    )(ckv, kpe4, w_ukv)


def _softmax_pv(s, v_ones):
    m = jnp.max(s, axis=-1, keepdims=True)
    p = jnp.exp2(s - m).astype(BF16)
    o = _dot(p, v_ones)
    return o[:, 0:LANES] / o[:, LANES:2 * LANES]


def _attn_kernel(lam_init, n_past, qm_ref, kvm_ref, gq_ref, gkv_ref, dq_ref, dkv_ref, *refs):
    if n_past:
        past_refs, refs = refs[:3], refs[3:]
        lam_ref, gdiff_ref, o_ref = refs[:3]
        joined = refs[3:]

        @pl.when(pl.program_id(1) == 0)
        def _():
            for dst, past, new in zip(joined, past_refs, (kvm_ref, gkv_ref, dkv_ref)):
                dst[0:n_past, :] = past[...]
                dst[n_past:, :] = new[...]

        kvm_ref, gkv_ref, dkv_ref = joined
    else:
        lam_ref, gdiff_ref, o_ref = refs
    tq = qm_ref.shape[0]
    lane = _lane_iota((tq, LANES))
    low = lane < HALF_LANES
    zero = jnp.zeros((tq, LANES), BF16)

    for p in range(MLA_HEADS // 2):
        qn = qm_ref[:, p * LANES:(p + 1) * LANES]
        g = p // 2
        qpe = qm_ref[:, 512 + g * LANES:512 + (g + 1) * LANES]
        kk = kvm_ref[:, p * 256:(p + 1) * 256]
        vv = kvm_ref[:, 1024 + p * 256:1024 + (p + 1) * 256]
        outs = []
        for half in range(2):
            h = 2 * p + half
            slot = h % 4
            in_slot = (lane >= slot * MLA_ROPE) & (lane < (slot + 1) * MLA_ROPE)
            lhs = jnp.concatenate(
                [jnp.where(low if half == 0 else ~low, qn, zero),
                 jnp.where(in_slot, qpe, zero)], axis=1)
            outs.append(_softmax_pv(_dot_nt(lhs, kk), vv))
        o_ref[:, p * LANES:(p + 1) * LANES] = jnp.where(low, outs[0], outs[1]).astype(BF16)

    kk = gkv_ref[:, 0:LANES]
    vv = gkv_ref[:, LANES:3 * LANES]
    for g in range(GQA_HEADS // 2):
        qg = gq_ref[:, g * LANES:(g + 1) * LANES]
        o_lo = _softmax_pv(_dot_nt(jnp.where(low, qg, zero), kk), vv)
        o_hi = _softmax_pv(_dot_nt(jnp.where(low, zero, qg), kk), vv)
        o_ref[:, 512 + g * LANES:512 + (g + 1) * LANES] = jnp.where(low, o_lo, o_hi).astype(BF16)

    lp = lam_ref[...]
    lam = (jnp.exp(jnp.sum(lp[0:1] * lp[1:2], axis=-1, keepdims=True))
           - jnp.exp(jnp.sum(lp[2:3] * lp[3:4], axis=-1, keepdims=True)) + lam_init)
    for h in range(DIFF_HEADS):
        qh = dq_ref[:, h * LANES:(h + 1) * LANES]
        kk = dkv_ref[:, h * LANES:(h + 1) * LANES]
        vv = dkv_ref[:, 512 + h * 256:512 + (h + 1) * 256]
        a1 =_softmax_pv(_dot_nt(jnp.where(low, qh, zero), kk), vv)
        a2 = _softmax_pv(_dot_nt(jnp.where(low, zero, qh), kk), vv)
        od = _rms(a1 - lam * a2, gdiff_ref[...]) * (1.0 - lam_init)
        o_ref[:, 1024 + h * LANES:1024 + (h + 1) * LANES] = od.astype(BF16)


def _attn_call(l, lam_init, qm, kvm, gq, gkv, dq, dkv, lam_p, g_diff, n_b, t_len, past=None):
    nq = t_len // TQ
    n_past = 0 if past is None else past[0].shape[0] // n_b

    def qspec(w):
        return pl.BlockSpec((TQ, w), lambda b, i: (b * nq + i, 0))

    def kspec(w, rows=t_len):
        return pl.BlockSpec((rows, w), lambda b, i: (b, 0))

    in_specs = [qspec(768), kspec(KVM_W), qspec(512), kspec(GKV_W), qspec(512), kspec(DKV_W)]
    args = [qm, kvm, gq, gkv, dq, dkv]
    scratch = []
    if n_past:
        in_specs += [kspec(KVM_W, n_past), kspec(GKV_W, n_past), kspec(DKV_W, n_past)]
        args += list(past)
        scratch = [pltpu.VMEM((n_past + t_len, w), BF16) for w in (KVM_W, GKV_W, DKV_W)]
    in_specs += [_layer_spec(l, (4, DIFF_D)), _layer_spec(l, (1, DIFF_DV))]
    args += [lam_p, g_diff]
    return pl.pallas_call(
        functools.partial(_attn_kernel, lam_init, n_past),
        grid=(n_b, nq),
        in_specs=in_specs,
        out_specs=qspec(3 * BRANCH_W),
        out_shape=jax.ShapeDtypeStruct((n_b * t_len, 3 * BRANCH_W), BF16),
        scratch_shapes=scratch,
        compiler_params=_cparams(("arbitrary", "arbitrary")),
        name="attn",
    )(*args)


def _log_sigmoid(x):
    return jnp.minimum(x, 0.0) - jnp.log(1.0 + jnp.exp(-jnp.abs(x)))


def _log_gamma(dec_ref, l, d, h):
    return _log_sigmoid(jnp.full((1, 1), dec_ref[l, d, h], F32))


def _ret_kernel(latent, l, t_len, dec_ref, q_ref, k_ref, v_ref, rg_ref, gret_ref, *refs):
    if latent:
        s0_ref, o_ref = refs
    else:
        o_ref, st_ref = refs[-2:]
    tq = q_ref.shape[0]
    t0 = pl.program_id(1) * tq
    lane = _lane_iota((tq, LANES))
    low = lane < HALF_LANES
    zero = jnp.zeros((tq, LANES), BF16)
    t_idx = (t0 + lax.broadcasted_iota(jnp.int32, (tq, t_len), 0)).astype(F32)
    s_idx = lax.broadcasted_iota(jnp.int32, (tq, t_len), 1).astype(F32)
    dist = t_idx - s_idx
    past = dist >= 0
    diag = jnp.where(dist == 0, 1.0, 0.0)
    t_col = (t0 + lax.broadcasted_iota(jnp.int32, (tq, 1), 0)).astype(F32)

    def lg(d, h):
        return _log_gamma(dec_ref, l, d, h)

    for h in range(RET_HEADS):
        p, half = h // 2, h % 2
        qp = q_ref[:, p * LANES:(p + 1) * LANES]
        qm = jnp.where(low if half == 0 else ~low, qp, zero)
        kp = k_ref[:, p * LANES:(p + 1) * LANES]
        vh = v_ref[:, h * LANES:(h + 1) * LANES]
        lgf, lgb = lg(0, h), lg(1, h)
        dmask = jnp.exp(jnp.where(past, lgf, -lgb) * dist) + diag
        o = _dot((_dot_nt(qm, kp) * dmask).astype(BF16), vh)
        if latent:
            sf =s0_ref[0, p].astype(BF16)
            sb = s0_ref[1, p].astype(BF16)
            o = o + _dot(qm, sf) * jnp.exp(lgf * (t_col + 1.0))
            o = o + _dot(qm, sb) * jnp.exp(lgb * (float(t_len) - t_col))
        mu = jnp.mean(o, axis=-1, keepdims=True)
        oc = o - mu
        y = oc * lax.rsqrt(jnp.mean(oc * oc, axis=-1, keepdims=True) + EPS)
        y = y * gret_ref[:, h * LANES:(h + 1) * LANES]
        rg = rg_ref[:, h * LANES:(h + 1) * LANES].astype(F32)
        o_ref[:, h * LANES:(h + 1) * LANES] = (y * _silu(rg)).astype(BF16)

    if not latent:
        s_col = lax.broadcasted_iota(jnp.int32, (t_len, 1), 0).astype(F32)
        lane_t = _lane_iota((1, LANES)) < HALF_LANES
        for p in range(RET_HEADS // 2):
            kp = k_ref[:, p * LANES:(p + 1) * LANES].astype(F32)
            for d in range(2):
                lg_lane = jnp.where(lane_t, lg(d, 2 * p), lg(d, 2 * p + 1))
                expo = (float(t_len) - 1.0 - s_col) if d == 0 else s_col
                kdec_t = jnp.transpose(kp * jnp.exp(lg_lane * expo)).astype(BF16)
                for half in range(2):
                    h = 2 * p + half
                    st = _dot(kdec_t, v_ref[:, h * LANES:(h + 1) * LANES])
                    st = st[half * RET_DK:(half + 1) * RET_DK, :]
                    if l == 0:
                        st_ref[0, d, h] = st
                        for k in range(1, st_ref.shape[0]):
                            st_ref[k, d, h] = jnp.zeros_like(st)
                    else:
                        st_ref[d, h] = st


def _ret_call(latent, l, dec, ret, rg, g_ret, s0, n_b, t_len, prev_state=None):
    nq = t_len // TQ
    aliases = {}
    assert latent or nq == 1
    in_specs = [pl.BlockSpec(memory_space=pltpu.SMEM),
                pl.BlockSpec((TQ, 256), lambda b, i: (b * nq + i, 0)),
                pl.BlockSpec((t_len, 256), lambda b, i: (b, 1)),
                pl.BlockSpec((t_len, 512), lambda b, i: (b, 1)),
                pl.BlockSpec((TQ, 512), lambda b, i: (b * nq + i, 0)),
                _layer_spec(l, (1, 512))]
    args = [dec, ret, ret, ret, rg, g_ret]
    out_specs = [pl.BlockSpec((TQ, 512), lambda b, i: (b * nq + i, 0))]
    out_shape = [jax.ShapeDtypeStruct((n_b * t_len, 512), BF16)]
    if latent:
        in_specs.append(pl.BlockSpec((None, 2, 2, LANES, LANES), lambda b, i: (b, 0, 0, 0, 0)))
        args.append(s0)
    else:
        assert (prev_state is None) == (l == 0)
        lead, at = ((DEPTH,), 0) if l == 0 else ((None,), l)
        out_specs.append(pl.BlockSpec((None,) + lead + (2, RET_HEADS, RET_DK, RET_DV),
                                      lambda b, i: (b, at, 0, 0, 0, 0)))
        out_shape.append(jax.ShapeDtypeStruct((n_b, DEPTH, 2, RET_HEADS, RET_DK, RET_DV), F32))
        if prev_state is not None:
            aliases = {len(args): 1}
            in_specs.append(pl.BlockSpec(memory_space=pl.ANY))
            args.append(prev_state)
    return pl.pallas_call(
        functools.partial(_ret_kernel, latent, l, t_len),
        grid=(n_b, nq),
        in_specs=in_specs, out_specs=out_specs, out_shape=out_shape,
        input_output_aliases=aliases,
        compiler_params=_cparams(("arbitrary", "arbitrary")),
        name="ret_lat" if latent else "ret_ctx",
    )(*args)


def _merge_kernel(x_ref, mod_ref, br_ref, or_ref, gate_ref, wbr_ref, wout_ref, gpost_ref, o_ref):
    merged = None
    for n in range(N_BRANCH):
        if n < 2:
            b = br_ref[:, n * BRANCH_W:(n + 1) * BRANCH_W]
        elif n == 2:
            b = or_ref[...]
        else:
            b = br_ref[:, 2 * BRANCH_W:3 * BRANCH_W]
        t = gate_ref[:, n * D:(n + 1) * D].astype(F32) * _dot(b, wbr_ref[n])
        merged = t if merged is None else merged + t
    out = _dot(merged.astype(BF16), wout_ref[...])
    g1 = mod_ref[...][:, 2 * D:3 * D]
    o_ref[...] = x_ref[...] + g1 * _rms(out, gpost_ref[...])


def _merge_call(l, x, mod3, mod_row, br, o_r, gates, lw):
    n_tok = x.shape[0]

    def tok(w):
        return pl.BlockSpec((TM_MERGE, w), lambda i: (i, 0))

    return pl.pallas_call(
        _merge_kernel,
        grid=(n_tok // TM_MERGE,),
        in_specs=[tok(D), pl.BlockSpec((None, 1, 6 * D), lambda i: (mod_row(i * TM_MERGE), 0, 0)),
                  tok(3 * BRANCH_W), tok(BRANCH_W), tok(4 * D),
                  _layer_spec(l, (N_BRANCH, BRANCH_W, D)), _layer_spec(l, (D, D)),
                  _layer_spec(l, (1, D))],
        out_specs=tok(D),
        out_shape=jax.ShapeDtypeStruct((n_tok, D), F32),
        compiler_params=_cparams(("arbitrary",)),
        name="merge",
    )(x, mod3, br, o_r, gates, lw["w_br"], lw["w_out"], lw["g_post1"])


def _route(logits_t, bias):
    n = logits_t.shape[1]
    scores = jax.nn.sigmoid(logits_t)
    sel = scores + bias
    neg = -jnp.inf
    sub = lax.broadcasted_iota(jnp.int32, (GROUP_SIZE, n), 0)
    grp = []
    for g in range(N_GROUPS):
        blk = sel[g * GROUP_SIZE:(g + 1) * GROUP_SIZE]
        m1 = jnp.max(blk, axis=0, keepdims=True)
        i1 = jnp.min(jnp.where(blk == m1, sub, GROUP_SIZE), axis=0, keepdims=True)
        m2 = jnp.max(jnp.where(sub == i1, neg, blk), axis=0, keepdims=True)
        grp.append(m1 + m2)
    parts = []
    for g in range(N_GROUPS):
        beaten = jnp.zeros((1, n), jnp.int32)
        for o in range(N_GROUPS):
            if o == g:
                continue
            wins = (grp[o] > grp[g]) | (grp[o] == grp[g]) if o < g else (grp[o] > grp[g])
            beaten = beaten + wins.astype(jnp.int32)
        keep = beaten < TOPK_GROUPS
        parts.append(jnp.where(keep, sel[g * GROUP_SIZE:(g + 1) * GROUP_SIZE], neg))
    cur = jnp.concatenate(parts, axis=0)
    eidx = lax.broadcasted_iota(jnp.int32, (N_EXPERTS, n), 0)
    hits, ids, ws = [], [], []
    for _ in range(TOP_K):
        m = jnp.max(cur, axis=0, keepdims=True)
        i = jnp.min(jnp.where(cur == m, eidx, N_EXPERTS), axis=0, keepdims=True)
        hit = eidx == i
        hits.append(hit)
        ids.append(i)
        ws.append(jnp.sum(jnp.where(hit, scores, 0.0), axis=0, keepdims=True))
        cur = jnp.where(hit, neg, cur)
    wsum = ws[0] + ws[1] + ws[2] + ws[3]
    return hits, ids, [w / wsum * ROUTE_SCALE for w in ws]


U32 = jnp.uint32
HIGH16 = np.uint32(0xFFFF0000)


def _bf16_bits(v):
    return lax.bitcast_convert_type(v.astype(BF16).astype(F32), U32)


def _pack_rows(v):
    return (_bf16_bits(v[:, 0:D // 2]) >> 16) | _bf16_bits(v[:, D // 2:D])


def _unpack_rows(p):
    lo = lax.bitcast_convert_type(p << 16, F32)
    hi = lax.bitcast_convert_type(p & HIGH16, F32)
    return jnp.concatenate([lo, hi], axis=1)


def _moe_pre_kernel(x_ref, mod_ref, gpre_ref, wr_ref, br_ref, tri_ref,
                    hp_ref, eidx_ref, rank_ref, comb_ref, cnt_ref, run_ref):
    tm = x_ref.shape[0]

    @pl.when(pl.program_id(0) == 0)
    def _():
        run_ref[...] = jnp.zeros_like(run_ref)

    mod = mod_ref[...]
    sh2, sc2 = mod[:, 3 * D:4 * D], mod[:, 4 * D:5 * D]
    h = _rms(x_ref[...], gpre_ref[...]) * (1.0 + sc2) + sh2
    hp_ref[...] = _pack_rows(h)
    hb = h.astype(BF16)
    h_lo = (h - hb.astype(F32)).astype(BF16)
    wr = wr_ref[...]
    wr_hi = wr.astype(BF16)
    wr_lo = (wr - wr_hi.astype(F32)).astype(BF16)
    logits_t = _dot_nt(wr_hi, hb) + _dot_nt(wr_hi, h_lo) + _dot_nt(wr_lo, hb)
    hits, ids, ws = _route(logits_t, br_ref[...])

    picked = jnp.zeros((N_EXPERTS, tm), F32)
    for hit in hits:
        picked = jnp.where(hit, 1.0, picked)
    before = _dot(picked.astype(BF16), tri_ref[...]) + run_ref[:, 0:1]
    sub8 = lax.broadcasted_iota(jnp.int32, (8, tm), 0)
    comb8 = jnp.zeros((8, tm), F32)
    for k in range(TOP_K):
        rank = jnp.sum(jnp.where(hits[k], before, 0.0), axis=0, keepdims=True)
        eidx_ref[k:k + 1, :] = ids[k]
        rank_ref[k:k + 1, :] = rank.astype(jnp.int32)
        comb8 = jnp.where(sub8 == k, ws[k], comb8)
    comb_ref[...] = jnp.transpose(
        jnp.concatenate([comb8, jnp.zeros((LANES - 8, tm), F32)], axis=0))
    run_ref[...] = run_ref[...] + jnp.sum(picked, axis=1, keepdims=True)
    cnt_ref[...] = run_ref[...]


def _moe_pre_call(l, x, mod3, mod_row, lw):
    n_tok = x.shape[0]
    tm = TM_MOE_PRE
    tri = np.arange(tm)
    tri = jnp.asarray(tri[:, None] < tri[None, :], BF16)
    row4 = pl.BlockSpec((TOP_K, tm), lambda i: (0, i))
    return pl.pallas_call(
        _moe_pre_kernel,
        grid=(n_tok // tm,),
        in_specs=[pl.BlockSpec((tm, D), lambda i: (i, 0)),
                  pl.BlockSpec((None, 1, 6 * D), lambda i: (mod_row(i * tm), 0, 0)),
                  _layer_spec(l, (1, D)), _layer_spec(l, (N_EXPERTS, D)),
                  _layer_spec(l, (N_EXPERTS, 1)), _const_spec((tm, tm))],
        out_specs=[pl.BlockSpec((tm, D // 2), lambda i: (i, 0)), row4, row4,
                   pl.BlockSpec((tm, LANES), lambda i: (i, 0)),
                   _const_spec((N_EXPERTS, LANES))],
        out_shape=[jax.ShapeDtypeStruct((n_tok, D // 2), U32),
                   jax.ShapeDtypeStruct((TOP_K, n_tok), jnp.int32),
                   jax.ShapeDtypeStruct((TOP_K, n_tok), jnp.int32),
                   jax.ShapeDtypeStruct((n_tok, LANES), F32),
                   jax.ShapeDtypeStruct((N_EXPERTS, LANES), F32)],
        scratch_shapes=[pltpu.VMEM((N_EXPERTS, LANES), F32)],
        compiler_params=_cparams(("arbitrary",)),
        name="moe_pre",
    )(x, mod3, lw["g_pre2"], lw["w_router_t"], lw["b_router"], tri)


def _moe_plan_kernel(eidx_ref, rank_ref, cnt_ref, dest_ref, te_ref, tv_ref, tn_ref):
    tm = eidx_ref.shape[1]
    cnt = cnt_ref[...]
    padded = jnp.ceil(cnt * (1.0 / TMX)) * TMX
    row = lax.broadcasted_iota(jnp.int32, cnt.shape, 0)
    incl = padded
    shift = 1
    while shift < N_EXPERTS:
        incl = incl + jnp.where(row >= shift, pltpu.roll(incl, shift, 0), 0.0)
        shift *= 2
    start = (incl - padded)[:, 0:1]
    end = incl[:, 0:1]
    erow = lax.broadcasted_iota(jnp.int32, (N_EXPERTS, tm), 0)
    for k in range(TOP_K):
        mine = erow == eidx_ref[k:k + 1, :]
        base = jnp.sum(jnp.where(mine, start, 0.0), axis=0, keepdims=True)
        dest_ref[k:k + 1, :] = rank_ref[k:k + 1, :] + base.astype(jnp.int32)

    @pl.when(pl.program_id(0) == 0)
    def _():
        tile0 = (_lane_iota((1, LANES)) * TMX).astype(F32)
        owner = jnp.sum(jnp.where(end <= tile0, 1.0, 0.0), axis=0, keepdims=True)
        owner = jnp.minimum(owner, N_EXPERTS - 1.0)
        erow_t = lax.broadcasted_iota(jnp.int32, (N_EXPERTS, LANES), 0).astype(F32)
        left = jnp.sum(jnp.where(erow_t == owner, cnt[:, 0:1] - (tile0 - start), 0.0),
                       axis=0, keepdims=True)
        te_ref[...] = owner.astype(jnp.int32)
        tv_ref[...] = jnp.clip(left, 0.0, float(TMX)).astype(jnp.int32)
        tn_ref[...] = jnp.full(tn_ref.shape, N_EXPERTS, jnp.int32)
        nxt = owner
        for k in range(W_SLOTS - 1):
            later = (erow_t > nxt) & (cnt[:, 0:1] > 0.0)
            nxt = jnp.min(jnp.where(later, erow_t, float(N_EXPERTS)), axis=0, keepdims=True)
            tn_ref[k:k + 1, :] = nxt.astype(jnp.int32)
        n_used = jnp.sum(jnp.where(left > 0.0, 1.0, 0.0), axis=1, keepdims=True)
        tn_ref[7:8, :] = jnp.minimum(tile0 * (1.0 / TMX), n_used - 1.0).astype(jnp.int32)


def _moe_plan_call(eidx, rank, cnt):
    n_tok = eidx.shape[1]
    tm = TM_MOE_PRE
    row4 = pl.BlockSpec((TOP_K, tm), lambda i: (0, i))
    tiles = jax.ShapeDtypeStruct((1, LANES), jnp.int32)
    return pl.pallas_call(
        _moe_plan_kernel,
        grid=(n_tok // tm,),
        in_specs=[row4, row4, _const_spec((N_EXPERTS, LANES))],
        out_specs=[row4, _const_spec((1, LANES)), _const_spec((1, LANES)), _const_spec((8, LANES))],
        out_shape=[jax.ShapeDtypeStruct((TOP_K, n_tok), jnp.int32), tiles, tiles,
                   jax.ShapeDtypeStruct((8, LANES), jnp.int32)],
        compiler_params=_cparams(("arbitrary",)),
        name="moe_plan",
    )(eidx, rank, cnt)


def _experts_kernel(l, te_ref, tv_ref, tn_ref, xs_ref, wgu_hbm, wdn_hbm, ys_ref,
                    wgu_f, wdn_f, wgu_b, wdn_b, sem, group_ref):
    j = pl.program_id(0)
    valid = tv_ref[j]
    expert = te_ref[j]

    def fetch(e, slot):
        return (pltpu.make_async_copy(wgu_hbm.at[l, e], wgu_f.at[slot], sem.at[slot, 0]),
                pltpu.make_async_copy(wdn_hbm.at[l, e], wdn_f.at[slot], sem.at[slot, 1]))

    def start_if_any(e, slot):
        @pl.when(e < N_EXPERTS)
        def _():
            for cp in fetch(e, slot):
                cp.start()

    @pl.when(j == 0)
    def _():
        group_ref[0] = 0
        start_if_any(expert, 0)
        for k in range(W_SLOTS - 2):
            start_if_any(tn_ref[k, 0], k + 1)

    first_tile = (j == 0) | (expert != te_ref[jnp.maximum(j - 1, 0)])

    @pl.when(first_tile & (valid > 0))
    def _():
        group = group_ref[0]
        slot = lax.rem(group, W_SLOTS)
        for cp in fetch(expert, slot):
            cp.wait()
        wgu_b[...] = wgu_f[slot].astype(BF16)
        wdn_b[...] = wdn_f[slot].astype(BF16)
        start_if_any(tn_ref[W_SLOTS - 2, j], lax.rem(group + W_SLOTS - 1, W_SLOTS))
        group_ref[0] = group + 1

    @pl.when(valid > 0)
    def _():
        rows = lax.broadcasted_iota(jnp.int32, (TMX, D), 0)
        x = jnp.where(rows < valid, _unpack_rows(xs_ref[...]), 0.0).astype(BF16)
        gu = _dot(x, wgu_b[...])
        a = _silu(gu[:, 0:EXPERT_FF]) * gu[:, EXPERT_FF:2 * EXPERT_FF]
        ys_ref[...] = _pack_rows(_dot(a.astype(BF16), wdn_b[...]))


def _experts_call(l, xs, te, tv, tn, w_gu, w_dn):
    n_tiles = xs.shape[0] // TMX
    grid_spec = pltpu.PrefetchScalarGridSpec(
        num_scalar_prefetch=3,
        grid=(n_tiles,),
        in_specs=[pl.BlockSpec((TMX, D // 2), lambda j, te, tv, tn: (tn[7, j], 0)),
                  pl.BlockSpec(memory_space=pl.ANY), pl.BlockSpec(memory_space=pl.ANY)],
        out_specs=pl.BlockSpec((TMX, D // 2), lambda j, te, tv, tn: (tn[7, j], 0)),
        scratch_shapes=[pltpu.VMEM((W_SLOTS, D, 2 * EXPERT_FF), F32),
                        pltpu.VMEM((W_SLOTS, EXPERT_FF, D), F32),
                        pltpu.VMEM((D, 2 * EXPERT_FF), BF16), pltpu.VMEM((EXPERT_FF, D), BF16),
                        pltpu.SemaphoreType.DMA((W_SLOTS, 2)), pltpu.SMEM((1,), jnp.int32)])
    return pl.pallas_call(
        functools.partial(_experts_kernel, l),
        grid_spec=grid_spec,
        out_shape=jax.ShapeDtypeStruct(xs.shape, U32),
        compiler_params=_cparams(("arbitrary",)),
        name="moe_experts",
    )(te, tv, tn, xs, w_gu, w_dn)


def _moe_post_kernel(x_ref, mod_ref, hp_ref, yg_ref, comb_ref, wsgu_ref, wsdn_ref, gpost_ref,
                     o_ref):
    hb = _unpack_rows(hp_ref[...]).astype(BF16)
    sgu = _dot(hb, wsgu_ref[...])
    sa = _silu(sgu[:, 0:SHARED_FF]) * sgu[:, SHARED_FF:2 * SHARED_FF]
    acc = _dot(sa.astype(BF16), wsdn_ref[...])
    comb = comb_ref[...]
    for k in range(TOP_K):
        acc = acc + comb[:, k:k + 1] * _unpack_rows(yg_ref[k])
    g2 = mod_ref[...][:, 5 * D:6 * D]
    o_ref[...] = x_ref[...] + g2 * _rms(acc, gpost_ref[...])


def _moe_post_call(l, x, mod3, mod_row, hp, yg, comb, lw):
    n_tok = x.shape[0]
    tm = TM_MOE_PRE
    return pl.pallas_call(
        _moe_post_kernel,
        grid=(n_tok // tm,),
        in_specs=[pl.BlockSpec((tm, D), lambda i: (i, 0)),
                  pl.BlockSpec((None, 1, 6 * D), lambda i: (mod_row(i * tm), 0, 0)),
                  pl.BlockSpec((tm, D // 2), lambda i: (i, 0)),
                  pl.BlockSpec((TOP_K, tm, D // 2), lambda i: (0, i, 0)),
                  pl.BlockSpec((tm, LANES), lambda i: (i, 0)),
                  _layer_spec(l, (D, 2 * SHARED_FF)), _layer_spec(l, (SHARED_FF, D)),
                  _layer_spec(l, (1, D))],
        out_specs=pl.BlockSpec((tm, D), lambda i: (i, 0)),
        out_shape=jax.ShapeDtypeStruct((n_tok, D), F32),
        compiler_params=_cparams(("arbitrary",)),
        name="moe_post",
    )(x, mod3, hp, yg, comb, lw["w_sh_gu"], lw["w_sh_down"], lw["g_post2"])


def _moe_call(l, x, mod3, mod_row, lw):
    n_tok = x.shape[0]
    n_slots = -(-(TOP_K * n_tok + N_EXPERTS * (TMX - 1)) // TMX) * TMX
    assert n_slots // TMX <= LANES
    hp, eidx, rank, comb, cnt = _moe_pre_call(l, x, mod3, mod_row, lw)
    dest, te, tv, tn = _moe_plan_call(eidx, rank, cnt)
    dest = dest.reshape(TOP_K * n_tok)
    xs = _sc_scatter_rows(hp, dest, n_slots)
    ys = _experts_call(l, xs, te[0], tv[0], tn, lw["w_exp_gu"], lw["w_exp_down"])
    yg = _sc_gather_rows(ys, dest).reshape(TOP_K, n_tok, D // 2)
    return _moe_post_call(l, x, mod3, mod_row, hp, yg, comb, lw)


SC_CORES, SC_SUBCORES = 2, 16
SC_WORKERS = SC_CORES * SC_SUBCORES


def _sc_gather_rows(table, idx, chunk=64):
    n_out, width = idx.shape[0], table.shape[1]
    per_worker = n_out // SC_WORKERS
    n_chunks = per_worker // chunk
    assert per_worker * SC_WORKERS == n_out and n_chunks * chunk == per_worker
    mesh = plsc.VectorSubcoreMesh(core_axis_name="c", subcore_axis_name="s",
                                  num_cores=SC_CORES, num_subcores=SC_SUBCORES)

    @functools.partial(
        pl.kernel, mesh=mesh,
        out_type=jax.ShapeDtypeStruct((n_out, width), table.dtype),
        scratch_types=[pltpu.VMEM((chunk,), jnp.int32), pltpu.VMEM((chunk, width), table.dtype),
                       pltpu.SemaphoreType.DMA],
        name="sc_gather")
    def gather(table_hbm, idx_hbm, out_hbm, idx_v, rows_v, sem):
        base = (lax.axis_index("s") * SC_CORES + lax.axis_index("c")) * per_worker

        @pl.loop(0, n_chunks)
        def _(j):
            off = base + j * chunk
            pltpu.sync_copy(idx_hbm.at[pl.ds(off, chunk)], idx_v)
            pltpu.async_copy(table_hbm.at[idx_v], rows_v, sem).wait()
            pltpu.sync_copy(rows_v, out_hbm.at[pl.ds(off, chunk)])

    return gather(table, idx)


def _sc_scatter_rows(rows, dest, n_slots, chunk=64):
    n_tok, width = rows.shape
    per_worker = n_tok // SC_WORKERS
    n_chunks = per_worker // chunk
    assert per_worker * SC_WORKERS == n_tok and n_chunks * chunk == per_worker
    mesh = plsc.VectorSubcoreMesh(core_axis_name="c", subcore_axis_name="s",
                                  num_cores=SC_CORES, num_subcores=SC_SUBCORES)

    @functools.partial(
        pl.kernel, mesh=mesh,
        out_type=jax.ShapeDtypeStruct((n_slots, width), rows.dtype),
        scratch_types=[pltpu.VMEM((chunk,), jnp.int32), pltpu.VMEM((chunk, width), rows.dtype)],
        name="sc_scatter")
    def scatter(rows_hbm, dest_hbm, out_hbm, idx_v, rows_v):
        base = (lax.axis_index("s") * SC_CORES + lax.axis_index("c")) * per_worker

        @pl.loop(0, n_chunks)
        def _(j):
            off = base + j * chunk
            pltpu.sync_copy(rows_hbm.at[pl.ds(off, chunk)], rows_v)
            for k in range(TOP_K):
                pltpu.sync_copy(dest_hbm.at[pl.ds(k * n_tok + off, chunk)], idx_v)
                pltpu.sync_copy(rows_v, out_hbm.at[idx_v])

    return scatter(rows, dest)


def _rope_tables(t_len):
    pos = np.arange(t_len)
    row, col = pos // GRID_W, pos % GRID_W

    def tab(r):
        half = r // 2
        freq = ROPE_BASE ** (-np.arange(half, dtype=np.float64) / half)
        sign = np.concatenate([-np.ones(half), np.ones(half)])
        cs, sn = [], []
        for p in (row, col):
            ang = p[:, None].astype(np.float64) * freq[None, :]
            cs.append(np.concatenate([np.cos(ang), np.cos(ang)], axis=1))
            sn.append(np.concatenate([np.sin(ang), np.sin(ang)], axis=1) * sign[None, :])
        return np.concatenate(cs, axis=1), np.concatenate(sn, axis=1)

    c64, s64 = tab(GQA_HD // 2)
    cpe, spe = tab(MLA_ROPE // 2)
    out = (np.tile(c64, (1, 2)), np.tile(s64, (1, 2)), np.tile(cpe, (1, 4)), np.tile(spe, (1, 4)))
    return tuple(jnp.asarray(a, F32) for a in out)


def _prep_weights(p):
    n_l = p["w_in"].shape[0]

    def row(name):
        return p[name].reshape(n_l, 1, -1)

    w_uq = p["w_mla_uq"].reshape(n_l, MLA_Q_LORA, MLA_HEADS, MLA_NOPE + MLA_ROPE)
    w_uq = jnp.concatenate([w_uq[..., :MLA_NOPE].reshape(n_l, MLA_Q_LORA, -1),
                            w_uq[..., MLA_NOPE:].reshape(n_l, MLA_Q_LORA, -1)], axis=-1)
    w_ukv = p["w_mla_ukv"].reshape(n_l, MLA_KV_LORA, MLA_HEADS, MLA_NOPE + MLA_V)
    w_ukv = jnp.concatenate([w_ukv[..., :MLA_NOPE].reshape(n_l, MLA_KV_LORA, -1),
                             w_ukv[..., MLA_NOPE:].reshape(n_l, MLA_KV_LORA, -1)], axis=-1)
    w_br = p["w_br"]
    w_br_gqa = w_br[:, 1].reshape(n_l, GQA_HEADS, GQA_HD, D)[:, jnp.array(GQA_ORDER)]
    w_br = jnp.concatenate([w_br[:, 0:1], w_br_gqa.reshape(n_l, 1, BRANCH_W, D), w_br[:, 2:4]], axis=1)
    blk = np.arange(512) // GQA_HD
    return {
        "g_pre1": row("g_pre1"), "g_post1": row("g_post1"),
        "g_pre2": row("g_pre2"), "g_post2": row("g_post2"),
        "w_in": jnp.swapaxes(p["w_in"], 1, 2).astype(BF16),
        "g_mla_q": row("g_mla_q"), "w_uq": w_uq.astype(BF16),
        "g_mla_kv": row("g_mla_kv"), "w_ukv": w_ukv.astype(BF16),
        "g_gqa_q": jnp.tile(p["g_gqa_q"], (1, GQA_HEADS)).reshape(n_l, 1, -1),
        "g_gqa_k": jnp.tile(p["g_gqa_k"], (1, GQA_KV_HEADS)).reshape(n_l, 1, -1),
        "bd": jnp.asarray(blk[:, None] == blk[None, :], BF16),
        "ret_decay": p["ret_decay"],
        "g_ret": row("g_ret"),
        "diff_lambda": p["diff_lambda"], "g_diff": row("g_diff"),
        "w_br": w_br.astype(BF16), "w_out": p["w_out"].astype(BF16),
        "w_router_t": jnp.swapaxes(p["w_router"], 1, 2),
        "b_router": p["b_router"].reshape(n_l, N_EXPERTS, 1),
        "w_exp_gu": p["w_exp_gu"], "w_exp_down": p["w_exp_down"],
        "w_sh_gu": p["w_sh_gu"].astype(BF16), "w_sh_down": p["w_sh_down"].astype(BF16),
    }


def _mixers(latent, l, x, mod3, mod_row, lw, n_b, t_len, tabs=None, past=None, s0=None,
            prev_cache=None):
    lam_init = 0.8 - 0.6 * math.exp(-0.3 * l)
    outs = _inprep_call(latent, l, x, mod3, mod_row, lw, tabs, t_len,
                        None if prev_cache is None else prev_cache[:6])
    qm, kvm, gq, gkv, dq, dkv, ret, rg, gates = outs[:9]
    br = _attn_call(l, lam_init, qm, kvm, gq, gkv, dq, dkv, lw["diff_lambda"], lw["g_diff"],
                    n_b, t_len, past)
    r = _ret_call(latent, l, lw["ret_decay"], ret, rg, lw["g_ret"], s0, n_b, t_len,
                  None if prev_cache is None else prev_cache[6])
    y = _merge_call(l, x, mod3, mod_row, br, r[0], gates, lw)
    cache = None if latent else tuple(outs[9:]) + (r[1],)
    return y, cache


def kernel(x_prompt, x_sample, cache_mla_ckv, cache_mla_kpe, cache_gqa_k, cache_gqa_v, cache_diff_k, cache_diff_v, state_ret, c, c_ctx, w_mod, b_mod, g_pre1, g_post1, g_pre2, g_post2, w_in, g_mla_q, w_mla_uq, g_mla_kv, w_mla_ukv, g_gqa_q, g_gqa_k, ret_decay, g_ret, diff_lambda, g_diff, w_br, w_out, w_router, b_router, w_exp_gu, w_exp_down, w_sh_gu, w_sh_down):
    params = dict(w_in=w_in, g_pre1=g_pre1, g_post1=g_post1, g_pre2=g_pre2,
                  g_post2=g_post2, g_mla_q=g_mla_q, w_mla_uq=w_mla_uq,
                  g_mla_kv=g_mla_kv, w_mla_ukv=w_mla_ukv, g_gqa_q=g_gqa_q, g_gqa_k=g_gqa_k,
                  ret_decay=ret_decay, g_ret=g_ret, diff_lambda=diff_lambda, g_diff=g_diff,
                  w_br=w_br, w_out=w_out, w_router=w_router, b_router=b_router,
                  w_exp_gu=w_exp_gu, w_exp_down=w_exp_down, w_sh_gu=w_sh_gu, w_sh_down=w_sh_down)
    n_bc, t_c, _ = x_prompt.shape
    n_bl, t_l, _ = x_sample.shape
    p_len = cache_mla_ckv.shape[2]
    tabs = _rope_tables(t_l)
    n_cond = 8
    cond = jnp.concatenate([c_ctx[None, :], c, jnp.zeros((n_cond - 1 - n_bl, D), F32)], axis=0)
    assert t_l % TM_MERGE == 0 and (t_c * n_bc) % TM_MERGE == 0
    assert t_l % TM_MOE_PRE == 0 and (t_c * n_bc) % TM_MOE_PRE == 0

    yp = x_prompt.reshape(n_bc * t_c, D)
    ys = x_sample.reshape(n_bl * t_l, D)
    cache = None
    lw = _prep_weights(params)
    for l in range(DEPTH):
        mod3 = _mod_call(l, cond, w_mod, b_mod).reshape(n_cond, 1, 6 * D)
        yp, cache = _mixers(False, l, yp, mod3, lambda i: 0, lw, n_bc, t_c, prev_cache=cache)
        yp = _moe_call(l, yp, mod3, lambda i: 0, lw)
        past_kvm = _pastkv_call(l, cache_mla_ckv[:, l].reshape(n_bl * p_len, -1),
                                jnp.tile(cache_mla_kpe[:, l].reshape(n_bl * p_len, -1), (1, 4)),
                                lw["w_ukv"])
        past_gkv = jnp.concatenate([cache_gqa_k[:, l].reshape(n_bl * p_len, -1),
                                    cache_gqa_v[:, l].reshape(n_bl * p_len, -1),
                                    jnp.ones((n_bl * p_len, LANES), F32)], axis=-1).astype(BF16)
        past_dv = jnp.concatenate([cache_diff_v[:, l], jnp.ones_like(cache_diff_v[:, l])], axis=-1)
        past_dkv = jnp.concatenate([cache_diff_k[:, l].reshape(n_bl * p_len, -1),
                                    past_dv.reshape(n_bl * p_len, -1)], axis=-1).astype(BF16)
        s0 = state_ret[:, l].reshape(n_bl, 2, RET_HEADS // 2, 2 * RET_DK, RET_DV)
        ys, _ = _mixers(True, l, ys, mod3, lambda t: 1 + t // t_l, lw, n_bl, t_l, tabs=tabs,
                        past=(past_kvm, past_gkv, past_dkv), s0=s0)
        ys = _moe_call(l, ys, mod3, lambda t: 1 + t // t_l, lw)

    ckv, kpe, gk_t, gv_t, dk_t, dv, ret_state = cache

    def time_minor(a, shape):
        a = a.reshape((n_bc, DEPTH) + shape + (t_c,))
        return jnp.transpose(a, (0, 1, a.ndim - 1) + tuple(range(2, a.ndim - 1)))

    return (yp.reshape(n_bc, t_c, D), ys.reshape(n_bl, t_l, D), ckv, kpe,
            time_minor(gk_t, (GQA_KV_HEADS, GQA_HD)), time_minor(gv_t, (GQA_KV_HEADS, GQA_HD)),
            time_minor(dk_t, (DIFF_HEADS, 2, DIFF_D)),
            dv.reshape(n_bc, DEPTH, t_c, DIFF_HEADS, DIFF_DV), ret_state)
```

```python
import functools
import math

import numpy as np
import jax
import jax.numpy as jnp
from jax import lax
from jax.experimental import pallas as pl
from jax.experimental.pallas import tpu as pltpu
from jax.experimental.pallas import tpu_sc as plsc

F32 = jnp.float32
BF16 = jnp.bfloat16

D = 1024
DEPTH = 2
GRID_W = 64
ROPE_BASE = 10000.0
EPS = 1e-6

MLA_HEADS, MLA_NOPE, MLA_ROPE, MLA_V = 8, 64, 32, 64
MLA_Q_LORA, MLA_KV_LORA = 384, 256
GQA_HEADS, GQA_KV_HEADS, GQA_HD = 8, 2, 64
RET_HEADS, RET_DK, RET_DV = 4, 64, 128
DIFF_HEADS, DIFF_D, DIFF_DV = 4, 64, 128
N_BRANCH, BRANCH_W = 4, 512
N_EXPERTS, TOP_K, N_GROUPS, TOPK_GROUPS = 32, 4, 4, 2
EXPERT_FF, SHARED_FF = 256, 256
ROUTE_SCALE = 2.5
GROUP_SIZE = N_EXPERTS // N_GROUPS

LANES = 128
HALF_LANES = 64
VMEM_LIMIT = 56 * 1024 * 1024

C_CQ, C_CKV, C_KPE, C_GQ, C_GK, C_GV = 0, 384, 640, 768, 1280, 1408
C_DQ, C_DK, C_DV, C_RQ, C_RK, C_RV, C_RG, C_GL, C_END = (
    1536, 2048, 2560, 3072, 3328, 3584, 4096, 4608, 8704)
O_CQ, O_CKV, O_KPE, O_GQ, O_GK, O_GV = 0, 384, 640, 672, 1184, 1312
O_RQ, O_RK, O_RV, O_RG, O_DQ, O_DK, O_DV, O_GL, O_END = (
    1440, 1696, 1952, 2464, 2976, 3488, 4000, 4512, 8608)
GQA_ORDER = (0, 4, 1, 5, 2, 6, 3, 7)

KVM_W = 8 * 256
GKV_W = 3 * LANES
DKV_W = 512 + 4 * 256
LOG2E = 1.4426950408889634
TM = 256
TM_LAT = 512
TM_MERGE = 512
TQ = 256
TM_MOE_PRE = 1024
TMX = 256
W_SLOTS = 4


def _cparams(sem):
    return pltpu.CompilerParams(dimension_semantics=sem, vmem_limit_bytes=VMEM_LIMIT)


def _const_spec(shape):
    nd = len(shape)
    return pl.BlockSpec(shape, lambda *_: (0,) * nd)


def _layer_spec(l, shape):
    nd = len(shape)
    return pl.BlockSpec((None,) + tuple(shape), lambda *_: (l,) + (0,) * nd)


def _rms(x, g):
    return x * lax.rsqrt(jnp.mean(x * x, axis=-1, keepdims=True) + EPS) * g


def _dot(a, b):
    return jnp.dot(a, b, preferred_element_type=F32)


def _dot_nt(a, b):
    return lax.dot_general(a, b, (((1,), (1,)), ((), ())), preferred_element_type=F32)


def _silu(x):
    return x * jax.nn.sigmoid(x)


def _lane_iota(shape):
    return lax.broadcasted_iota(jnp.int32, shape, len(shape) - 1)


def _seg_meansq(x, bd_ref, width):
    sq = x * x
    hi = sq.astype(BF16)
    lo = (sq - hi.astype(F32)).astype(BF16)
    bd = bd_ref[0:width, 0:width]
    return (_dot(hi, bd) + _dot(lo, bd)) * (1.0 / GQA_HD)


def _rope(x, cos, sin_signed, half):
    width = x.shape[-1]
    first = (_lane_iota(x.shape) % (2 * half)) < half
    partner = jnp.where(first, pltpu.roll(x, width - half, 1), pltpu.roll(x, half, 1))
    return x * cos + partner * sin_signed


def _tile_lanes(t, reps):
    return t if reps == 1 else jnp.concatenate([t] * reps, axis=1)


def _put_layer(o_ref, val, whole_stack):
    if whole_stack:
        o_ref[0] = val
        for k in range(1, o_ref.shape[0]):
            o_ref[k] = jnp.zeros_like(val)
    else:
        o_ref[...] = val


def _store_kvm(kvm_ref, kv, kpe_b):
    ones = jnp.ones(kpe_b.shape, BF16)
    for p in range(4):
        kvm_ref[:, p * 256:p * 256 + LANES] = kv[:, p * LANES:(p + 1) * LANES].astype(BF16)
        kvm_ref[:, p * 256 + LANES:(p + 1) * 256] = kpe_b
        kvm_ref[:, 1024 + p * 256:1024 + p * 256 + LANES] = (
            kv[:, 512 + p * LANES:512 + (p + 1) * LANES].astype(BF16))
        kvm_ref[:, 1024 + p * 256 + LANES:1024 + (p + 1) * 256] = ones


def _mod_kernel(c_ref, w_ref, b_ref, o_ref):
    a = _silu(c_ref[...]).astype(BF16)
    o_ref[...] = _dot(a, w_ref[...].astype(BF16)) + b_ref[...]


def _mod_call(l, cond, w_mod, b_mod):
    n_l, _, n = w_mod.shape
    tn = 1536
    return pl.pallas_call(
        _mod_kernel,
        grid=(n // tn,),
        in_specs=[_const_spec(cond.shape),
                  pl.BlockSpec((None, D, tn), lambda j: (l, 0, j)),
                  pl.BlockSpec((None, 1, tn), lambda j: (l, 0, j))],
        out_specs=pl.BlockSpec((cond.shape[0], tn), lambda j: (0, j)),
        out_shape=jax.ShapeDtypeStruct((cond.shape[0], n), F32),
        compiler_params=_cparams(("arbitrary",)),
        name="mod",
    )(cond, w_mod, b_mod.reshape(n_l, 1, n))


def _inprep_kernel(latent, n_aliased, *refs):
    (x_ref, mod_ref, gpre_ref, win_ref, gmq_ref, wuq_ref, gmkv_ref, wukv_ref,
     ggq_ref, ggk_ref, bd_ref) = refs[:11]
    refs = refs[11:]
    if latent:
        cos64_ref, sin64_ref, cospe_ref, sinpe_ref = refs[:4]
        refs = refs[4:]
    refs = refs[n_aliased:]
    first_layer = n_aliased == 0
    (qm_ref, kvm_ref, gqo_ref, gkv_ref, dqo_ref, dkv_ref, ret_ref, rg_ref, gate_ref) = refs[:9]
    refs = refs[9:]
    if not latent:
        ckv_o, kpe_o, gk_o, gv_o, dk_o, dv_o = refs

    x = x_ref[...]
    mod = mod_ref[...]
    sh1 = mod[:, 0:D]
    sc1 = mod[:, D:2 * D]
    hb = (_rms(x, gpre_ref[...]) * (1.0 + sc1) + sh1).astype(BF16)

    def z(a, b):
        return _dot_nt(hb, win_ref[a:b, :])

    if latent:
        cos64, sin64 = cos64_ref[...], sin64_ref[...]
        cospe, sinpe = cospe_ref[...], sinpe_ref[...]

    cqn = _rms(z(O_CQ, O_CKV), gmq_ref[...]).astype(BF16)
    q = _dot(cqn, wuq_ref[...]) * ((MLA_NOPE + MLA_ROPE) ** -0.5 * LOG2E)
    q_nope, q_pe = q[:, 0:512], q[:, 512:768]
    if latent:
        q_pe = _rope(q_pe, _tile_lanes(cospe, 2), _tile_lanes(sinpe, 2), MLA_ROPE // 4)
    qm_ref[:, 0:512] = q_nope.astype(BF16)
    qm_ref[:, 512:768] = q_pe.astype(BF16)

    ckvn = _rms(z(O_CKV, O_KPE), gmkv_ref[...])
    kv = _dot(ckvn.astype(BF16), wukv_ref[...])
    kpe4 = _dot_nt(hb, jnp.concatenate([win_ref[O_KPE:O_GQ, :]] * 4, axis=0))
    if latent:
        kpe4 = _rope(kpe4, cospe, sinpe, MLA_ROPE // 4)
    else:
        _put_layer(ckv_o, ckvn, first_layer)
        _put_layer(kpe_o, kpe4[:, 0:MLA_ROPE], first_layer)
    _store_kvm(kvm_ref, kv, kpe4.astype(BF16))

    gq = _dot_nt(hb, jnp.concatenate(
        [win_ref[O_GQ + h * GQA_HD:O_GQ + (h + 1) * GQA_HD, :] for h in GQA_ORDER], axis=0))
    gq = gq * lax.rsqrt(_seg_meansq(gq, bd_ref, 512) + EPS) * ggq_ref[...]
    gk = z(O_GK, O_GV)
    gk = gk * lax.rsqrt(_seg_meansq(gk, bd_ref, LANES) + EPS) * ggk_ref[...]
    gv = z(O_GV, O_RQ)
    if latent:
        gq = _rope(gq, _tile_lanes(cos64, 4), _tile_lanes(sin64, 4), GQA_HD // 4)
        gk = _rope(gk, cos64, sin64, GQA_HD // 4)
    else:
        _put_layer(gk_o, jnp.transpose(gk), first_layer)
        _put_layer(gv_o, jnp.transpose(gv), first_layer)
    gqo_ref[...] = (gq * (GQA_HD ** -0.5 * LOG2E)).astype(BF16)
    gkv_ref[:, 0:LANES] = gk.astype(BF16)
    gkv_ref[:, LANES:2 * LANES] = gv.astype(BF16)
    gkv_ref[:, 2 * LANES:3 * LANES] = jnp.ones(gv.shape, BF16)

    dq = z(O_DQ, O_DK)
    dk = z(O_DK, O_DV)
    dv = z(O_DV, O_GL)
    if latent:
        dq = _rope(dq, _tile_lanes(cos64, 4), _tile_lanes(sin64, 4), DIFF_D // 4)
        dk = _rope(dk, _tile_lanes(cos64, 4), _tile_lanes(sin64, 4), DIFF_D // 4)
    else:
        _put_layer(dk_o, jnp.transpose(dk), first_layer)
        _put_layer(dv_o, dv, first_layer)
    dqo_ref[...] = (dq * (DIFF_D ** -0.5 * LOG2E)).astype(BF16)
    dkv_ref[:, 0:512] = dk.astype(BF16)
    for h in range(DIFF_HEADS):
        dkv_ref[:, 512 + h * 256:512 + h * 256 + LANES] = dv[:, h * LANES:(h + 1) * LANES].astype(BF16)
        dkv_ref[:, 512 + h * 256 + LANES:512 + (h + 1) * 256] = jnp.ones((dv.shape[0], LANES), BF16)

    ret_ref[:, 0:256] = z(O_RQ, O_RK).astype(BF16)
    ret_ref[:, 256:512] = (z(O_RK, O_RV) * (RET_DK ** -0.5)).astype(BF16)
    ret_ref[:, 512:1024] = z(O_RV, O_RG).astype(BF16)
    rg_ref[...] = z(O_RG, O_DQ).astype(BF16)

    for n in range(N_BRANCH):
        gate_ref[:, n * D:(n + 1) * D] = jax.nn.sigmoid(
            z(O_GL + n * D, O_GL + (n + 1) * D)).astype(BF16)


def _inprep_call(latent, l, x, mod3, mod_row, lw, tabs, t_len, prev_caches=None):
    n_tok = x.shape[0]
    tm = TM_LAT if latent else TM
    nblk = n_tok // tm
    blk_per_seq = t_len // tm

    def tok(w):
        return pl.BlockSpec((tm, w), lambda i: (i, 0))

    in_specs = [tok(D),
                pl.BlockSpec((None, 1, 6 * D), lambda i: (mod_row(i * tm), 0, 0)),
                _layer_spec(l, (1, D)),
                pl.BlockSpec((None, O_END, D), lambda i: (l, 0, 0), pipeline_mode=pl.Buffered(1)),
                _layer_spec(l, (1, MLA_Q_LORA)), _layer_spec(l, (MLA_Q_LORA, 768)),
                _layer_spec(l, (1, MLA_KV_LORA)), _layer_spec(l, (MLA_KV_LORA, 1024)),
                _layer_spec(l, (1, 512)), _layer_spec(l, (1, LANES)), _const_spec((512, 512))]
    args = [x, mod3, lw["g_pre1"], lw["w_in"], lw["g_mla_q"], lw["w_uq"], lw["g_mla_kv"],
            lw["w_ukv"], lw["g_gqa_q"], lw["g_gqa_k"], lw["bd"]]
    if latent:
        tab_spec = pl.BlockSpec((tm, LANES), lambda i: (i % blk_per_seq, 0))
        in_specs += [tab_spec] * 4
        args += list(tabs)
    widths = [768, KVM_W, 512, GKV_W, 512, DKV_W, 1024, 512, 4 * D]
    out_specs = [tok(w) for w in widths]
    out_shape = [jax.ShapeDtypeStruct((n_tok, w), BF16) for w in widths]
    aliases = {}
    if not latent:
        assert TM == t_len and (prev_caches is None) == (l == 0)
        n_seq = n_tok // t_len
        lead, at = ((DEPTH,), 0) if l == 0 else ((None,), l)

        def row_major(w):
            out_specs.append(pl.BlockSpec((None,) + lead + (t_len, w), lambda i: (i, at, 0, 0)))
            out_shape.append(jax.ShapeDtypeStruct((n_seq, DEPTH, t_len, w), F32))

        row_major(MLA_KV_LORA)
        row_major(MLA_ROPE)
        for w in (LANES, LANES, 512):
            out_specs.append(pl.BlockSpec((None,) + lead + (w, t_len), lambda i: (i, at, 0, 0)))
            out_shape.append(jax.ShapeDtypeStruct((n_seq, DEPTH, w, t_len), F32))
        row_major(512)
        if prev_caches is not None:
            n_in = len(args)
            in_specs += [pl.BlockSpec(memory_space=pl.ANY)] * len(prev_caches)
            args += list(prev_caches)
            aliases = {n_in + k: len(widths) + k for k in range(len(prev_caches))}
    return pl.pallas_call(
        functools.partial(_inprep_kernel, latent, len(aliases)),
        grid=(nblk,),
        in_specs=in_specs, out_specs=out_specs, out_shape=out_shape,
        input_output_aliases=aliases,
        compiler_params=_cparams(("arbitrary",)),
        name="inprep_lat" if latent else "inprep_ctx",
    )(*args)


def _pastkv_kernel(ckv_ref, kpe_ref, wukv_ref, o_ref):
    kv = _dot(ckv_ref[...].astype(BF16), wukv_ref[...])
    _store_kvm(o_ref, kv, kpe_ref[...].astype(BF16))


def _pastkv_call(l, ckv, kpe4, w_ukv):
    n = ckv.shape[0]
    return pl.pallas_call(
        _pastkv_kernel,
        grid=(n // TM,),
        in_specs=[pl.BlockSpec((TM, MLA_KV_LORA), lambda i: (i, 0)),
                  pl.BlockSpec((TM, LANES), lambda i: (i, 0)),
                  _layer_spec(l, (MLA_KV_LORA, 1024))],
        out_specs=pl.BlockSpec((TM, KVM_W), lambda i: (i, 0)),
        out_shape=jax.ShapeDtypeStruct((n, KVM_W), BF16),
        compiler_params=_cparams(("arbitrary",)),
        name="pastkv",
    )(ckv, kpe4, w_ukv)


def _softmax_pv(s, v_ones):
    m = jnp.max(s, axis=-1, keepdims=True)
    p = jnp.exp2(s - m).astype(BF16)
    o = _dot(p, v_ones)
    return o[:, 0:LANES] / o[:, LANES:2 * LANES]


def _attn_kernel(lam_init, n_past, qm_ref, kvm_ref, gq_ref, gkv_ref, dq_ref, dkv_ref, *refs):
    if n_past:
        past_refs, refs = refs[:3], refs[3:]
        lam_ref, gdiff_ref, o_ref = refs[:3]
        joined = refs[3:]

        @pl.when(pl.program_id(1) == 0)
        def _():
            for dst, past, new in zip(joined, past_refs, (kvm_ref, gkv_ref, dkv_ref)):
                dst[0:n_past, :] = past[...]
                dst[n_past:, :] = new[...]

        kvm_ref, gkv_ref, dkv_ref = joined
    else:
        lam_ref, gdiff_ref, o_ref = refs
    tq = qm_ref.shape[0]
    lane = _lane_iota((tq, LANES))
    low = lane < HALF_LANES
    zero = jnp.zeros((tq, LANES), BF16)

    for p in range(MLA_HEADS // 2):
        qn = qm_ref[:, p * LANES:(p + 1) * LANES]
        g = p // 2
        qpe = qm_ref[:, 512 + g * LANES:512 + (g + 1) * LANES]
        kk = kvm_ref[:, p * 256:(p + 1) * 256]
        vv = kvm_ref[:, 1024 + p * 256:1024 + (p + 1) * 256]
        outs = []
        for half in range(2):
            h = 2 * p + half
            slot = h % 4
            in_slot = (lane >= slot * MLA_ROPE) & (lane < (slot + 1) * MLA_ROPE)
            lhs = jnp.concatenate(
                [jnp.where(low if half == 0 else ~low, qn, zero),
                 jnp.where(in_slot, qpe, zero)], axis=1)
            outs.append(_softmax_pv(_dot_nt(lhs, kk), vv))
        o_ref[:, p * LANES:(p + 1) * LANES] = jnp.where(low, outs[0], outs[1]).astype(BF16)

    kk = gkv_ref[:, 0:LANES]
    vv = gkv_ref[:, LANES:3 * LANES]
    for g in range(GQA_HEADS // 2):
        qg = gq_ref[:, g * LANES:(g + 1) * LANES]
        o_lo = _softmax_pv(_dot_nt(jnp.where(low, qg, zero), kk), vv)
        o_hi = _softmax_pv(_dot_nt(jnp.where(low, zero, qg), kk), vv)
        o_ref[:, 512 + g * LANES:512 + (g + 1) * LANES] = jnp.where(low, o_lo, o_hi).astype(BF16)

    lp = lam_ref[...]
    lam = (jnp.exp(jnp.sum(lp[0:1] * lp[1:2], axis=-1, keepdims=True))
           - jnp.exp(jnp.sum(lp[2:3] * lp[3:4], axis=-1, keepdims=True)) + lam_init)
    for h in range(DIFF_HEADS):
        qh = dq_ref[:, h * LANES:(h + 1) * LANES]
        kk = dkv_ref[:, h * LANES:(h + 1) * LANES]
        vv = dkv_ref[:, 512 + h * 256:512 + (h + 1) * 256]
        a1 =_softmax_pv(_dot_nt(jnp.where(low, qh, zero), kk), vv)
        a2 = _softmax_pv(_dot_nt(jnp.where(low, zero, qh), kk), vv)
        od = _rms(a1 - lam * a2, gdiff_ref[...]) * (1.0 - lam_init)
        o_ref[:, 1024 + h * LANES:1024 + (h + 1) * LANES] = od.astype(BF16)


def _attn_call(l, lam_init, qm, kvm, gq, gkv, dq, dkv, lam_p, g_diff, n_b, t_len, past=None):
    nq = t_len // TQ
    n_past = 0 if past is None else past[0].shape[0] // n_b

    def qspec(w):
        return pl.BlockSpec((TQ, w), lambda b, i: (b * nq + i, 0))

    def kspec(w, rows=t_len):
        return pl.BlockSpec((rows, w), lambda b, i: (b, 0))

    in_specs = [qspec(768), kspec(KVM_W), qspec(512), kspec(GKV_W), qspec(512), kspec(DKV_W)]
    args = [qm, kvm, gq, gkv, dq, dkv]
    scratch = []
    if n_past:
        in_specs += [kspec(KVM_W, n_past), kspec(GKV_W, n_past), kspec(DKV_W, n_past)]
        args += list(past)
        scratch = [pltpu.VMEM((n_past + t_len, w), BF16) for w in (KVM_W, GKV_W, DKV_W)]
    in_specs += [_layer_spec(l, (4, DIFF_D)), _layer_spec(l, (1, DIFF_DV))]
    args += [lam_p, g_diff]
    return pl.pallas_call(
        functools.partial(_attn_kernel, lam_init, n_past),
        grid=(n_b, nq),
        in_specs=in_specs,
        out_specs=qspec(3 * BRANCH_W),
        out_shape=jax.ShapeDtypeStruct((n_b * t_len, 3 * BRANCH_W), BF16),
        scratch_shapes=scratch,
        compiler_params=_cparams(("arbitrary", "arbitrary")),
        name="attn",
    )(*args)


def _log_sigmoid(x):
    return jnp.minimum(x, 0.0) - jnp.log(1.0 + jnp.exp(-jnp.abs(x)))


def _log_gamma(dec_ref, l, d, h):
    return _log_sigmoid(jnp.full((1, 1), dec_ref[l, d, h], F32))


def _ret_kernel(latent, l, t_len, dec_ref, q_ref, k_ref, v_ref, rg_ref, gret_ref, *refs):
    if latent:
        s0_ref, o_ref = refs
    else:
        o_ref, st_ref = refs[-2:]
    tq = q_ref.shape[0]
    t0 = pl.program_id(1) * tq
    lane = _lane_iota((tq, LANES))
    low = lane < HALF_LANES
    zero = jnp.zeros((tq, LANES), BF16)
    t_idx = (t0 + lax.broadcasted_iota(jnp.int32, (tq, t_len), 0)).astype(F32)
    s_idx = lax.broadcasted_iota(jnp.int32, (tq, t_len), 1).astype(F32)
    dist = t_idx - s_idx
    past = dist >= 0
    diag = jnp.where(dist == 0, 1.0, 0.0)
    t_col = (t0 + lax.broadcasted_iota(jnp.int32, (tq, 1), 0)).astype(F32)

    def lg(d, h):
        return _log_gamma(dec_ref, l, d, h)

    for h in range(RET_HEADS):
        p, half = h // 2, h % 2
        qp = q_ref[:, p * LANES:(p + 1) * LANES]
        qm = jnp.where(low if half == 0 else ~low, qp, zero)
        kp = k_ref[:, p * LANES:(p + 1) * LANES]
        vh = v_ref[:, h * LANES:(h + 1) * LANES]
        lgf, lgb = lg(0, h), lg(1, h)
        dmask = jnp.exp(jnp.where(past, lgf, -lgb) * dist) + diag
        o = _dot((_dot_nt(qm, kp) * dmask).astype(BF16), vh)
        if latent:
            sf =s0_ref[0, p].astype(BF16)
            sb = s0_ref[1, p].astype(BF16)
            o = o + _dot(qm, sf) * jnp.exp(lgf * (t_col + 1.0))
            o = o + _dot(qm, sb) * jnp.exp(lgb * (float(t_len) - t_col))
        mu = jnp.mean(o, axis=-1, keepdims=True)
        oc = o - mu
        y = oc * lax.rsqrt(jnp.mean(oc * oc, axis=-1, keepdims=True) + EPS)
        y = y * gret_ref[:, h * LANES:(h + 1) * LANES]
        rg = rg_ref[:, h * LANES:(h + 1) * LANES].astype(F32)
        o_ref[:, h * LANES:(h + 1) * LANES] = (y * _silu(rg)).astype(BF16)

    if not latent:
        s_col = lax.broadcasted_iota(jnp.int32, (t_len, 1), 0).astype(F32)
        lane_t = _lane_iota((1, LANES)) < HALF_LANES
        for p in range(RET_HEADS // 2):
            kp = k_ref[:, p * LANES:(p + 1) * LANES].astype(F32)
            for d in range(2):
                lg_lane = jnp.where(lane_t, lg(d, 2 * p), lg(d, 2 * p + 1))
                expo = (float(t_len) - 1.0 - s_col) if d == 0 else s_col
                kdec_t = jnp.transpose(kp * jnp.exp(lg_lane * expo)).astype(BF16)
                for half in range(2):
                    h = 2 * p + half
                    st = _dot(kdec_t, v_ref[:, h * LANES:(h + 1) * LANES])
                    st = st[half * RET_DK:(half + 1) * RET_DK, :]
                    if l == 0:
                        st_ref[0, d, h] = st
                        for k in range(1, st_ref.shape[0]):
                            st_ref[k, d, h] = jnp.zeros_like(st)
                    else:
                        st_ref[d, h] = st


def _ret_call(latent, l, dec, ret, rg, g_ret, s0, n_b, t_len, prev_state=None):
    nq = t_len // TQ
    aliases = {}
    assert latent or nq == 1
    in_specs = [pl.BlockSpec(memory_space=pltpu.SMEM),
                pl.BlockSpec((TQ, 256), lambda b, i: (b * nq + i, 0)),
                pl.BlockSpec((t_len, 256), lambda b, i: (b, 1)),
                pl.BlockSpec((t_len, 512), lambda b, i: (b, 1)),
                pl.BlockSpec((TQ, 512), lambda b, i: (b * nq + i, 0)),
                _layer_spec(l, (1, 512))]
    args = [dec, ret, ret, ret, rg, g_ret]
    out_specs = [pl.BlockSpec((TQ, 512), lambda b, i: (b * nq + i, 0))]
    out_shape = [jax.ShapeDtypeStruct((n_b * t_len, 512), BF16)]
    if latent:
        in_specs.append(pl.BlockSpec((None, 2, 2, LANES, LANES), lambda b, i: (b, 0, 0, 0, 0)))
        args.append(s0)
    else:
        assert (prev_state is None) == (l == 0)
        lead, at = ((DEPTH,), 0) if l == 0 else ((None,), l)
        out_specs.append(pl.BlockSpec((None,) + lead + (2, RET_HEADS, RET_DK, RET_DV),
                                      lambda b, i: (b, at, 0, 0, 0, 0)))
        out_shape.append(jax.ShapeDtypeStruct((n_b, DEPTH, 2, RET_HEADS, RET_DK, RET_DV), F32))
        if prev_state is not None:
            aliases = {len(args): 1}
            in_specs.append(pl.BlockSpec(memory_space=pl.ANY))
            args.append(prev_state)
    return pl.pallas_call(
        functools.partial(_ret_kernel, latent, l, t_len),
        grid=(n_b, nq),
        in_specs=in_specs, out_specs=out_specs, out_shape=out_shape,
        input_output_aliases=aliases,
        compiler_params=_cparams(("arbitrary", "arbitrary")),
        name="ret_lat" if latent else "ret_ctx",
    )(*args)


def _merge_kernel(x_ref, mod_ref, br_ref, or_ref, gate_ref, wbr_ref, wout_ref, gpost_ref, o_ref):
    merged = None
    for n in range(N_BRANCH):
        if n < 2:
            b = br_ref[:, n * BRANCH_W:(n + 1) * BRANCH_W]
        elif n == 2:
            b = or_ref[...]
        else:
            b = br_ref[:, 2 * BRANCH_W:3 * BRANCH_W]
        t = gate_ref[:, n * D:(n + 1) * D].astype(F32) * _dot(b, wbr_ref[n])
        merged = t if merged is None else merged + t
    out = _dot(merged.astype(BF16), wout_ref[...])
    g1 = mod_ref[...][:, 2 * D:3 * D]
    o_ref[...] = x_ref[...] + g1 * _rms(out, gpost_ref[...])


def _merge_call(l, x, mod3, mod_row, br, o_r, gates, lw):
    n_tok = x.shape[0]

    def tok(w):
        return pl.BlockSpec((TM_MERGE, w), lambda i: (i, 0))

    return pl.pallas_call(
        _merge_kernel,
        grid=(n_tok // TM_MERGE,),
        in_specs=[tok(D), pl.BlockSpec((None, 1, 6 * D), lambda i: (mod_row(i * TM_MERGE), 0, 0)),
                  tok(3 * BRANCH_W), tok(BRANCH_W), tok(4 * D),
                  _layer_spec(l, (N_BRANCH, BRANCH_W, D)), _layer_spec(l, (D, D)),
                  _layer_spec(l, (1, D))],
        out_specs=tok(D),
        out_shape=jax.ShapeDtypeStruct((n_tok, D), F32),
        compiler_params=_cparams(("arbitrary",)),
        name="merge",
    )(x, mod3, br, o_r, gates, lw["w_br"], lw["w_out"], lw["g_post1"])


def _route(logits_t, bias):
    n = logits_t.shape[1]
    scores = jax.nn.sigmoid(logits_t)
    sel = scores + bias
    neg = -jnp.inf
    sub = lax.broadcasted_iota(jnp.int32, (GROUP_SIZE, n), 0)
    grp = []
    for g in range(N_GROUPS):
        blk = sel[g * GROUP_SIZE:(g + 1) * GROUP_SIZE]
        m1 = jnp.max(blk, axis=0, keepdims=True)
        i1 = jnp.min(jnp.where(blk == m1, sub, GROUP_SIZE), axis=0, keepdims=True)
        m2 = jnp.max(jnp.where(sub == i1, neg, blk), axis=0, keepdims=True)
        grp.append(m1 + m2)
    parts = []
    for g in range(N_GROUPS):
        beaten = jnp.zeros((1, n), jnp.int32)
        for o in range(N_GROUPS):
            if o == g:
                continue
            wins = (grp[o] > grp[g]) | (grp[o] == grp[g]) if o < g else (grp[o] > grp[g])
            beaten = beaten + wins.astype(jnp.int32)
        keep = beaten < TOPK_GROUPS
        parts.append(jnp.where(keep, sel[g * GROUP_SIZE:(g + 1) * GROUP_SIZE], neg))
    cur = jnp.concatenate(parts, axis=0)
    eidx = lax.broadcasted_iota(jnp.int32, (N_EXPERTS, n), 0)
    hits, ids, ws = [], [], []
    for _ in range(TOP_K):
        m = jnp.max(cur, axis=0, keepdims=True)
        i = jnp.min(jnp.where(cur == m, eidx, N_EXPERTS), axis=0, keepdims=True)
        hit = eidx == i
        hits.append(hit)
        ids.append(i)
        ws.append(jnp.sum(jnp.where(hit, scores, 0.0), axis=0, keepdims=True))
        cur = jnp.where(hit, neg, cur)
    wsum = ws[0] + ws[1] + ws[2] + ws[3]
    return hits, ids, [w / wsum * ROUTE_SCALE for w in ws]


U32 = jnp.uint32
HIGH16 = np.uint32(0xFFFF0000)


def _bf16_bits(v):
    return lax.bitcast_convert_type(v.astype(BF16).astype(F32), U32)


def _pack_rows(v):
    return (_bf16_bits(v[:, 0:D // 2]) >> 16) | _bf16_bits(v[:, D // 2:D])


def _unpack_rows(p):
    lo = lax.bitcast_convert_type(p << 16, F32)
    hi = lax.bitcast_convert_type(p & HIGH16, F32)
    return jnp.concatenate([lo, hi], axis=1)


def _moe_pre_kernel(x_ref, mod_ref, gpre_ref, wr_ref, br_ref, tri_ref,
                    hp_ref, eidx_ref, rank_ref, comb_ref, cnt_ref, run_ref):
    tm = x_ref.shape[0]

    @pl.when(pl.program_id(0) == 0)
    def _():
        run_ref[...] = jnp.zeros_like(run_ref)

    mod = mod_ref[...]
    sh2, sc2 = mod[:, 3 * D:4 * D], mod[:, 4 * D:5 * D]
    h = _rms(x_ref[...], gpre_ref[...]) * (1.0 + sc2) + sh2
    hp_ref[...] = _pack_rows(h)
    hb = h.astype(BF16)
    h_lo = (h - hb.astype(F32)).astype(BF16)
    wr = wr_ref[...]
    wr_hi = wr.astype(BF16)
    wr_lo = (wr - wr_hi.astype(F32)).astype(BF16)
    logits_t = _dot_nt(wr_hi, hb) + _dot_nt(wr_hi, h_lo) + _dot_nt(wr_lo, hb)
    hits, ids, ws = _route(logits_t, br_ref[...])

    picked = jnp.zeros((N_EXPERTS, tm), F32)
    for hit in hits:
        picked = jnp.where(hit, 1.0, picked)
    before = _dot(picked.astype(BF16), tri_ref[...]) + run_ref[:, 0:1]
    sub8 = lax.broadcasted_iota(jnp.int32, (8, tm), 0)
    comb8 = jnp.zeros((8, tm), F32)
    for k in range(TOP_K):
        rank = jnp.sum(jnp.where(hits[k], before, 0.0), axis=0, keepdims=True)
        eidx_ref[k:k + 1, :] = ids[k]
        rank_ref[k:k + 1, :] = rank.astype(jnp.int32)
        comb8 = jnp.where(sub8 == k, ws[k], comb8)
    comb_ref[...] = jnp.transpose(
        jnp.concatenate([comb8, jnp.zeros((LANES - 8, tm), F32)], axis=0))
    run_ref[...] = run_ref[...] + jnp.sum(picked, axis=1, keepdims=True)
    cnt_ref[...] = run_ref[...]


def _moe_pre_call(l, x, mod3, mod_row, lw):
    n_tok = x.shape[0]
    tm = TM_MOE_PRE
    tri = np.arange(tm)
    tri = jnp.asarray(tri[:, None] < tri[None, :], BF16)
    row4 = pl.BlockSpec((TOP_K, tm), lambda i: (0, i))
    return pl.pallas_call(
        _moe_pre_kernel,
        grid=(n_tok // tm,),
        in_specs=[pl.BlockSpec((tm, D), lambda i: (i, 0)),
                  pl.BlockSpec((None, 1, 6 * D), lambda i: (mod_row(i * tm), 0, 0)),
                  _layer_spec(l, (1, D)), _layer_spec(l, (N_EXPERTS, D)),
                  _layer_spec(l, (N_EXPERTS, 1)), _const_spec((tm, tm))],
        out_specs=[pl.BlockSpec((tm, D // 2), lambda i: (i, 0)), row4, row4,
                   pl.BlockSpec((tm, LANES), lambda i: (i, 0)),
                   _const_spec((N_EXPERTS, LANES))],
        out_shape=[jax.ShapeDtypeStruct((n_tok, D // 2), U32),
                   jax.ShapeDtypeStruct((TOP_K, n_tok), jnp.int32),
                   jax.ShapeDtypeStruct((TOP_K, n_tok), jnp.int32),
                   jax.ShapeDtypeStruct((n_tok, LANES), F32),
                   jax.ShapeDtypeStruct((N_EXPERTS, LANES), F32)],
        scratch_shapes=[pltpu.VMEM((N_EXPERTS, LANES), F32)],
        compiler_params=_cparams(("arbitrary",)),
        name="moe_pre",
    )(x, mod3, lw["g_pre2"], lw["w_router_t"], lw["b_router"], tri)


def _moe_plan_kernel(eidx_ref, rank_ref, cnt_ref, dest_ref, te_ref, tv_ref, tn_ref):
    tm = eidx_ref.shape[1]
    cnt = cnt_ref[...]
    padded = jnp.ceil(cnt * (1.0 / TMX)) * TMX
    row = lax.broadcasted_iota(jnp.int32, cnt.shape, 0)
    incl = padded
    shift = 1
    while shift < N_EXPERTS:
        incl = incl + jnp.where(row >= shift, pltpu.roll(incl, shift, 0), 0.0)
        shift *= 2
    start = (incl - padded)[:, 0:1]
    end = incl[:, 0:1]
    erow = lax.broadcasted_iota(jnp.int32, (N_EXPERTS, tm), 0)
    for k in range(TOP_K):
        mine = erow == eidx_ref[k:k + 1, :]
        base = jnp.sum(jnp.where(mine, start, 0.0), axis=0, keepdims=True)
        dest_ref[k:k + 1, :] = rank_ref[k:k + 1, :] + base.astype(jnp.int32)

    @pl.when(pl.program_id(0) == 0)
    def _():
        tile0 = (_lane_iota((1, LANES)) * TMX).astype(F32)
        owner = jnp.sum(jnp.where(end <= tile0, 1.0, 0.0), axis=0, keepdims=True)
        owner = jnp.minimum(owner, N_EXPERTS - 1.0)
        erow_t = lax.broadcasted_iota(jnp.int32, (N_EXPERTS, LANES), 0).astype(F32)
        left = jnp.sum(jnp.where(erow_t == owner, cnt[:, 0:1] - (tile0 - start), 0.0),
                       axis=0, keepdims=True)
        te_ref[...] = owner.astype(jnp.int32)
        tv_ref[...] = jnp.clip(left, 0.0, float(TMX)).astype(jnp.int32)
        tn_ref[...] = jnp.full(tn_ref.shape, N_EXPERTS, jnp.int32)
        nxt = owner
        for k in range(W_SLOTS - 1):
            later = (erow_t > nxt) & (cnt[:, 0:1] > 0.0)
            nxt = jnp.min(jnp.where(later, erow_t, float(N_EXPERTS)), axis=0, keepdims=True)
            tn_ref[k:k + 1, :] = nxt.astype(jnp.int32)
        n_used = jnp.sum(jnp.where(left > 0.0, 1.0, 0.0), axis=1, keepdims=True)
        tn_ref[7:8, :] = jnp.minimum(tile0 * (1.0 / TMX), n_used - 1.0).astype(jnp.int32)


def _moe_plan_call(eidx, rank, cnt):
    n_tok = eidx.shape[1]
    tm = TM_MOE_PRE
    row4 = pl.BlockSpec((TOP_K, tm), lambda i: (0, i))
    tiles = jax.ShapeDtypeStruct((1, LANES), jnp.int32)
    return pl.pallas_call(
        _moe_plan_kernel,
        grid=(n_tok // tm,),
        in_specs=[row4, row4, _const_spec((N_EXPERTS, LANES))],
        out_specs=[row4, _const_spec((1, LANES)), _const_spec((1, LANES)), _const_spec((8, LANES))],
        out_shape=[jax.ShapeDtypeStruct((TOP_K, n_tok), jnp.int32), tiles, tiles,
                   jax.ShapeDtypeStruct((8, LANES), jnp.int32)],
        compiler_params=_cparams(("arbitrary",)),
        name="moe_plan",
    )(eidx, rank, cnt)


def _experts_kernel(l, te_ref, tv_ref, tn_ref, xs_ref, wgu_hbm, wdn_hbm, ys_ref,
                    wgu_f, wdn_f, wgu_b, wdn_b, sem, group_ref):
    j = pl.program_id(0)
    valid = tv_ref[j]
    expert = te_ref[j]

    def fetch(e, slot):
        return (pltpu.make_async_copy(wgu_hbm.at[l, e], wgu_f.at[slot], sem.at[slot, 0]),
                pltpu.make_async_copy(wdn_hbm.at[l, e], wdn_f.at[slot], sem.at[slot, 1]))

    def start_if_any(e, slot):
        @pl.when(e < N_EXPERTS)
        def _():
            for cp in fetch(e, slot):
                cp.start()

    @pl.when(j == 0)
    def _():
        group_ref[0] = 0
        start_if_any(expert, 0)
        for k in range(W_SLOTS - 2):
            start_if_any(tn_ref[k, 0], k + 1)

    first_tile = (j == 0) | (expert != te_ref[jnp.maximum(j - 1, 0)])

    @pl.when(first_tile & (valid > 0))
    def _():
        group = group_ref[0]
        slot = lax.rem(group, W_SLOTS)
        for cp in fetch(expert, slot):
            cp.wait()
        wgu_b[...] = wgu_f[slot].astype(BF16)
        wdn_b[...] = wdn_f[slot].astype(BF16)
        start_if_any(tn_ref[W_SLOTS - 2, j], lax.rem(group + W_SLOTS - 1, W_SLOTS))
        group_ref[0] = group + 1

    @pl.when(valid > 0)
    def _():
        rows = lax.broadcasted_iota(jnp.int32, (TMX, D), 0)
        x = jnp.where(rows < valid, _unpack_rows(xs_ref[...]), 0.0).astype(BF16)
        gu = _dot(x, wgu_b[...])
        a = _silu(gu[:, 0:EXPERT_FF]) * gu[:, EXPERT_FF:2 * EXPERT_FF]
        ys_ref[...] = _pack_rows(_dot(a.astype(BF16), wdn_b[...]))


def _experts_call(l, xs, te, tv, tn, w_gu, w_dn):
    n_tiles = xs.shape[0] // TMX
    grid_spec = pltpu.PrefetchScalarGridSpec(
        num_scalar_prefetch=3,
        grid=(n_tiles,),
        in_specs=[pl.BlockSpec((TMX, D // 2), lambda j, te, tv, tn: (tn[7, j], 0)),
                  pl.BlockSpec(memory_space=pl.ANY), pl.BlockSpec(memory_space=pl.ANY)],
        out_specs=pl.BlockSpec((TMX, D // 2), lambda j, te, tv, tn: (tn[7, j], 0)),
        scratch_shapes=[pltpu.VMEM((W_SLOTS, D, 2 * EXPERT_FF), F32),
                        pltpu.VMEM((W_SLOTS, EXPERT_FF, D), F32),
                        pltpu.VMEM((D, 2 * EXPERT_FF), BF16), pltpu.VMEM((EXPERT_FF, D), BF16),
                        pltpu.SemaphoreType.DMA((W_SLOTS, 2)), pltpu.SMEM((1,), jnp.int32)])
    return pl.pallas_call(
        functools.partial(_experts_kernel, l),
        grid_spec=grid_spec,
        out_shape=jax.ShapeDtypeStruct(xs.shape, U32),
        compiler_params=_cparams(("arbitrary",)),
        name="moe_experts",
    )(te, tv, tn, xs, w_gu, w_dn)


def _moe_post_kernel(x_ref, mod_ref, hp_ref, yg_ref, comb_ref, wsgu_ref, wsdn_ref, gpost_ref,
                     o_ref):
    hb = _unpack_rows(hp_ref[...]).astype(BF16)
    sgu = _dot(hb, wsgu_ref[...])
    sa = _silu(sgu[:, 0:SHARED_FF]) * sgu[:, SHARED_FF:2 * SHARED_FF]
    acc = _dot(sa.astype(BF16), wsdn_ref[...])
    comb = comb_ref[...]
    for k in range(TOP_K):
        acc = acc + comb[:, k:k + 1] * _unpack_rows(yg_ref[k])
    g2 = mod_ref[...][:, 5 * D:6 * D]
    o_ref[...] = x_ref[...] + g2 * _rms(acc, gpost_ref[...])


def _moe_post_call(l, x, mod3, mod_row, hp, yg, comb, lw):
    n_tok = x.shape[0]
    tm = TM_MOE_PRE
    return pl.pallas_call(
        _moe_post_kernel,
        grid=(n_tok // tm,),
        in_specs=[pl.BlockSpec((tm, D), lambda i: (i, 0)),
                  pl.BlockSpec((None, 1, 6 * D), lambda i: (mod_row(i * tm), 0, 0)),
                  pl.BlockSpec((tm, D // 2), lambda i: (i, 0)),
                  pl.BlockSpec((TOP_K, tm, D // 2), lambda i: (0, i, 0)),
                  pl.BlockSpec((tm, LANES), lambda i: (i, 0)),
                  _layer_spec(l, (D, 2 * SHARED_FF)), _layer_spec(l, (SHARED_FF, D)),
                  _layer_spec(l, (1, D))],
        out_specs=pl.BlockSpec((tm, D), lambda i: (i, 0)),
        out_shape=jax.ShapeDtypeStruct((n_tok, D), F32),
        compiler_params=_cparams(("arbitrary",)),
        name="moe_post",
    )(x, mod3, hp, yg, comb, lw["w_sh_gu"], lw["w_sh_down"], lw["g_post2"])


def _moe_call(l, x, mod3, mod_row, lw):
    n_tok = x.shape[0]
    n_slots = -(-(TOP_K * n_tok + N_EXPERTS * (TMX - 1)) // TMX) * TMX
    assert n_slots // TMX <= LANES
    hp, eidx, rank, comb, cnt = _moe_pre_call(l, x, mod3, mod_row, lw)
    dest, te, tv, tn = _moe_plan_call(eidx, rank, cnt)
    dest = dest.reshape(TOP_K * n_tok)
    xs = _sc_scatter_rows(hp, dest, n_slots)
    ys = _experts_call(l, xs, te[0], tv[0], tn, lw["w_exp_gu"], lw["w_exp_down"])
    yg = _sc_gather_rows(ys, dest).reshape(TOP_K, n_tok, D // 2)
    return _moe_post_call(l, x, mod3, mod_row, hp, yg, comb, lw)


SC_CORES, SC_SUBCORES = 2, 16
SC_WORKERS = SC_CORES * SC_SUBCORES


def _sc_gather_rows(table, idx, chunk=64):
    n_out, width = idx.shape[0], table.shape[1]
    per_worker = n_out // SC_WORKERS
    n_chunks = per_worker // chunk
    assert per_worker * SC_WORKERS == n_out and n_chunks * chunk == per_worker
    mesh = plsc.VectorSubcoreMesh(core_axis_name="c", subcore_axis_name="s",
                                  num_cores=SC_CORES, num_subcores=SC_SUBCORES)

    @functools.partial(
        pl.kernel, mesh=mesh,
        out_type=jax.ShapeDtypeStruct((n_out, width), table.dtype),
        scratch_types=[pltpu.VMEM((chunk,), jnp.int32), pltpu.VMEM((chunk, width), table.dtype),
                       pltpu.SemaphoreType.DMA],
        name="sc_gather")
    def gather(table_hbm, idx_hbm, out_hbm, idx_v, rows_v, sem):
        base = (lax.axis_index("s") * SC_CORES + lax.axis_index("c")) * per_worker

        @pl.loop(0, n_chunks)
        def _(j):
            off = base + j * chunk
            pltpu.sync_copy(idx_hbm.at[pl.ds(off, chunk)], idx_v)
            pltpu.async_copy(table_hbm.at[idx_v], rows_v, sem).wait()
            pltpu.sync_copy(rows_v, out_hbm.at[pl.ds(off, chunk)])

    return gather(table, idx)


def _sc_scatter_rows(rows, dest, n_slots, chunk=64):
    n_tok, width = rows.shape
    per_worker = n_tok // SC_WORKERS
    n_chunks = per_worker // chunk
    assert per_worker * SC_WORKERS == n_tok and n_chunks * chunk == per_worker
    mesh = plsc.VectorSubcoreMesh(core_axis_name="c", subcore_axis_name="s",
                                  num_cores=SC_CORES, num_subcores=SC_SUBCORES)

    @functools.partial(
        pl.kernel, mesh=mesh,
        out_type=jax.ShapeDtypeStruct((n_slots, width), rows.dtype),
        scratch_types=[pltpu.VMEM((chunk,), jnp.int32), pltpu.VMEM((chunk, width), rows.dtype)],
        name="sc_scatter")
    def scatter(rows_hbm, dest_hbm, out_hbm, idx_v, rows_v):
        base = (lax.axis_index("s") * SC_CORES + lax.axis_index("c")) * per_worker

        @pl.loop(0, n_chunks)
        def _(j):
            off = base + j * chunk
            pltpu.sync_copy(rows_hbm.at[pl.ds(off, chunk)], rows_v)
            for k in range(TOP_K):
                pltpu.sync_copy(dest_hbm.at[pl.ds(k * n_tok + off, chunk)], idx_v)
                pltpu.sync_copy(rows_v, out_hbm.at[idx_v])

    return scatter(rows, dest)


def _rope_tables(t_len):
    pos = np.arange(t_len)
    row, col = pos // GRID_W, pos % GRID_W

    def tab(r):
        half = r // 2
        freq = ROPE_BASE ** (-np.arange(half, dtype=np.float64) / half)
        sign = np.concatenate([-np.ones(half), np.ones(half)])
        cs, sn = [], []
        for p in (row, col):
            ang = p[:, None].astype(np.float64) * freq[None, :]
            cs.append(np.concatenate([np.cos(ang), np.cos(ang)], axis=1))
            sn.append(np.concatenate([np.sin(ang), np.sin(ang)], axis=1) * sign[None, :])
        return np.concatenate(cs, axis=1), np.concatenate(sn, axis=1)

    c64, s64 = tab(GQA_HD // 2)
    cpe, spe = tab(MLA_ROPE // 2)
    out = (np.tile(c64, (1, 2)), np.tile(s64, (1, 2)), np.tile(cpe, (1, 4)), np.tile(spe, (1, 4)))
    return tuple(jnp.asarray(a, F32) for a in out)


def _prep_weights(p):
    n_l = p["w_in"].shape[0]

    def row(name):
        return p[name].reshape(n_l, 1, -1)

    w_uq = p["w_mla_uq"].reshape(n_l, MLA_Q_LORA, MLA_HEADS, MLA_NOPE + MLA_ROPE)
    w_uq = jnp.concatenate([w_uq[..., :MLA_NOPE].reshape(n_l, MLA_Q_LORA, -1),
                            w_uq[..., MLA_NOPE:].reshape(n_l, MLA_Q_LORA, -1)], axis=-1)
    w_ukv = p["w_mla_ukv"].reshape(n_l, MLA_KV_LORA, MLA_HEADS, MLA_NOPE + MLA_V)
    w_ukv = jnp.concatenate([w_ukv[..., :MLA_NOPE].reshape(n_l, MLA_KV_LORA, -1),
                             w_ukv[..., MLA_NOPE:].reshape(n_l, MLA_KV_LORA, -1)], axis=-1)
    w_br = p["w_br"]
    w_br_gqa = w_br[:, 1].reshape(n_l, GQA_HEADS, GQA_HD, D)[:, jnp.array(GQA_ORDER)]
    w_br = jnp.concatenate([w_br[:, 0:1], w_br_gqa.reshape(n_l, 1, BRANCH_W, D), w_br[:, 2:4]], axis=1)
    blk = np.arange(512) // GQA_HD
    return {
        "g_pre1": row("g_pre1"), "g_post1": row("g_post1"),
        "g_pre2": row("g_pre2"), "g_post2": row("g_post2"),
        "w_in": jnp.swapaxes(p["w_in"], 1, 2).astype(BF16),
        "g_mla_q": row("g_mla_q"), "w_uq": w_uq.astype(BF16),
        "g_mla_kv": row("g_mla_kv"), "w_ukv": w_ukv.astype(BF16),
        "g_gqa_q": jnp.tile(p["g_gqa_q"], (1, GQA_HEADS)).reshape(n_l, 1, -1),
        "g_gqa_k": jnp.tile(p["g_gqa_k"], (1, GQA_KV_HEADS)).reshape(n_l, 1, -1),
        "bd": jnp.asarray(blk[:, None] == blk[None, :], BF16),
        "ret_decay": p["ret_decay"],
        "g_ret": row("g_ret"),
        "diff_lambda": p["diff_lambda"], "g_diff": row("g_diff"),
        "w_br": w_br.astype(BF16), "w_out": p["w_out"].astype(BF16),
        "w_router_t": jnp.swapaxes(p["w_router"], 1, 2),
        "b_router": p["b_router"].reshape(n_l, N_EXPERTS, 1),
        "w_exp_gu": p["w_exp_gu"], "w_exp_down": p["w_exp_down"],
        "w_sh_gu": p["w_sh_gu"].astype(BF16), "w_sh_down": p["w_sh_down"].astype(BF16),
    }


def _mixers(latent, l, x, mod3, mod_row, lw, n_b, t_len, tabs=None, past=None, s0=None,
            prev_cache=None):
    lam_init = 0.8 - 0.6 * math.exp(-0.3 * l)
    outs = _inprep_call(latent, l, x, mod3, mod_row, lw, tabs, t_len,
                        None if prev_cache is None else prev_cache[:6])
    qm, kvm, gq, gkv, dq, dkv, ret, rg, gates = outs[:9]
    br = _attn_call(l, lam_init, qm, kvm, gq, gkv, dq, dkv, lw["diff_lambda"], lw["g_diff"],
                    n_b, t_len, past)
    r = _ret_call(latent, l, lw["ret_decay"], ret, rg, lw["g_ret"], s0, n_b, t_len,
                  None if prev_cache is None else prev_cache[6])
    y = _merge_call(l, x, mod3, mod_row, br, r[0], gates, lw)
    cache = None if latent else tuple(outs[9:]) + (r[1],)
    return y, cache


def kernel(x_prompt, x_sample, cache_mla_ckv, cache_mla_kpe, cache_gqa_k, cache_gqa_v, cache_diff_k, cache_diff_v, state_ret, c, c_ctx, w_mod, b_mod, g_pre1, g_post1, g_pre2, g_post2, w_in, g_mla_q, w_mla_uq, g_mla_kv, w_mla_ukv, g_gqa_q, g_gqa_k, ret_decay, g_ret, diff_lambda, g_diff, w_br, w_out, w_router, b_router, w_exp_gu, w_exp_down, w_sh_gu, w_sh_down):
    params = dict(w_in=w_in, g_pre1=g_pre1, g_post1=g_post1, g_pre2=g_pre2,
                  g_post2=g_post2, g_mla_q=g_mla_q, w_mla_uq=w_mla_uq,
                  g_mla_kv=g_mla_kv, w_mla_ukv=w_mla_ukv, g_gqa_q=g_gqa_q, g_gqa_k=g_gqa_k,
                  ret_decay=ret_decay, g_ret=g_ret, diff_lambda=diff_lambda, g_diff=g_diff,
                  w_br=w_br, w_out=w_out, w_router=w_router, b_router=b_router,
                  w_exp_gu=w_exp_gu, w_exp_down=w_exp_down, w_sh_gu=w_sh_gu, w_sh_down=w_sh_down)
    n_bc, t_c, _ = x_prompt.shape
    n_bl, t_l, _ = x_sample.shape
    p_len = cache_mla_ckv.shape[2]
    tabs = _rope_tables(t_l)
    n_cond = 8
    cond = jnp.concatenate([c_ctx[None, :], c, jnp.zeros((n_cond - 1 - n_bl, D), F32)], axis=0)
    assert t_l % TM_MERGE == 0 and (t_c * n_bc) % TM_MERGE == 0
    assert t_l % TM_MOE_PRE == 0 and (t_c * n_bc) % TM_MOE_PRE == 0

    yp = x_prompt.reshape(n_bc * t_c, D)
    ys = x_sample.reshape(n_bl * t_l, D)
    cache = None
    lw = _prep_weights(params)
    for l in range(DEPTH):
        mod3 = _mod_call(l, cond, w_mod, b_mod).reshape(n_cond, 1, 6 * D)
        yp, cache = _mixers(False, l, yp, mod3, lambda i: 0, lw, n_bc, t_c, prev_cache=cache)
        yp = _moe_call(l, yp, mod3, lambda i: 0, lw)
        past_kvm = _pastkv_call(l, cache_mla_ckv[:, l].reshape(n_bl * p_len, -1),
                                jnp.tile(cache_mla_kpe[:, l].reshape(n_bl * p_len, -1), (1, 4)),
                                lw["w_ukv"])
        past_gkv = jnp.concatenate([cache_gqa_k[:, l].reshape(n_bl * p_len, -1),
                                    cache_gqa_v[:, l].reshape(n_bl * p_len, -1),
                                    jnp.ones((n_bl * p_len, LANES), F32)], axis=-1).astype(BF16)
        past_dv = jnp.concatenate([cache_diff_v[:, l], jnp.ones_like(cache_diff_v[:, l])], axis=-1)
        past_dkv = jnp.concatenate([cache_diff_k[:, l].reshape(n_bl * p_len, -1),
                                    past_dv.reshape(n_bl * p_len, -1)], axis=-1).astype(BF16)
        s0 = state_ret[:, l].reshape(n_bl, 2, RET_HEADS // 2, 2 * RET_DK, RET_DV)
        ys, _ = _mixers(True, l, ys, mod3, lambda t: 1 + t // t_l, lw, n_bl, t_l, tabs=tabs,
                        past=(past_kvm, past_gkv, past_dkv), s0=s0)
        ys = _moe_call(l, ys, mod3, lambda t: 1 + t // t_l, lw)

    ckv, kpe, gk_t, gv_t, dk_t, dv, ret_state = cache

    def time_minor(a, shape):
        a = a.reshape((n_bc, DEPTH) + shape + (t_c,))
        return jnp.transpose(a, (0, 1, a.ndim - 1) + tuple(range(2, a.ndim - 1)))

    return (yp.reshape(n_bc, t_c, D), ys.reshape(n_bl, t_l, D), ckv, kpe,
            time_minor(gk_t, (GQA_KV_HEADS, GQA_HD)), time_minor(gv_t, (GQA_KV_HEADS, GQA_HD)),
            time_minor(dk_t, (DIFF_HEADS, 2, DIFF_D)),
            dv.reshape(n_bc, DEPTH, t_c, DIFF_HEADS, DIFF_DV), ret_state)
```

```python
import functools
import math

import numpy as np
import jax
import jax.numpy as jnp
from jax import lax
from jax.experimental import pallas as pl
from jax.experimental.pallas import tpu as pltpu
from jax.experimental.pallas import tpu_sc as plsc

F32 = jnp.float32
BF16 = jnp.bfloat16

D = 1024
DEPTH = 2
GRID_W = 64
ROPE_BASE = 10000.0
EPS = 1e-6

MLA_HEADS, MLA_NOPE, MLA_ROPE, MLA_V = 8, 64, 32, 64
MLA_Q_LORA, MLA_KV_LORA = 384, 256
GQA_HEADS, GQA_KV_HEADS, GQA_HD = 8, 2, 64
RET_HEADS, RET_DK, RET_DV = 4, 64, 128
DIFF_HEADS, DIFF_D, DIFF_DV = 4, 64, 128
N_BRANCH, BRANCH_W = 4, 512
N_EXPERTS, TOP_K, N_GROUPS, TOPK_GROUPS = 32, 4, 4, 2
EXPERT_FF, SHARED_FF = 256, 256
ROUTE_SCALE = 2.5
GROUP_SIZE = N_EXPERTS // N_GROUPS

LANES = 128
HALF_LANES = 64
VMEM_LIMIT = 56 * 1024 * 1024

O_CQ, O_CKV, O_KPE, O_GQ, O_GK, O_GV = 0, 384, 640, 672, 1184, 1312
O_RQ, O_RK, O_RV, O_RG, O_DQ, O_DK, O_DV, O_GL, O_END = (
    1440, 1696, 1952, 2464, 2976, 3488, 4000, 4512, 8608)
GQA_ORDER = (0, 4, 1, 5, 2, 6, 3, 7)

KVM_W = 8 * 256
GKV_W = 3 * LANES
DKV_W = 512 + 4 * 256
LOG2E = 1.4426950408889634
TM = 256
TM_LAT = 512
TM_MERGE = 512
TQ = 256
TM_MOE_PRE = 1024
TMX = 256
W_SLOTS = 4


def _cparams(sem):
    return pltpu.CompilerParams(dimension_semantics=sem, vmem_limit_bytes=VMEM_LIMIT)


def _const_spec(shape):
    nd = len(shape)
    return pl.BlockSpec(shape, lambda *_: (0,) * nd)


def _layer_spec(l, shape):
    nd = len(shape)
    return pl.BlockSpec((None,) + tuple(shape), lambda *_: (l,) + (0,) * nd)


def _rms(x, g):
    return x * lax.rsqrt(jnp.mean(x * x, axis=-1, keepdims=True) + EPS) * g


def _dot(a, b):
    return jnp.dot(a, b, preferred_element_type=F32)


def _dot_nt(a, b):
    return lax.dot_general(a, b, (((1,), (1,)), ((), ())), preferred_element_type=F32)


def _silu(x):
    return x * jax.nn.sigmoid(x)


def _lane_iota(shape):
    return lax.broadcasted_iota(jnp.int32, shape, len(shape) - 1)


def _seg_meansq(x, bd_ref, width):
    sq = x * x
    hi = sq.astype(BF16)
    lo = (sq - hi.astype(F32)).astype(BF16)
    bd = bd_ref[0:width, 0:width]
    return (_dot(hi, bd) + _dot(lo, bd)) * (1.0 / GQA_HD)


def _rope(x, cos, sin_signed, half):
    width = x.shape[-1]
    first = (_lane_iota(x.shape) % (2 * half)) < half
    partner = jnp.where(first, pltpu.roll(x, width - half, 1), pltpu.roll(x, half, 1))
    return x * cos + partner * sin_signed


def _tile_lanes(t, reps):
    return t if reps == 1 else jnp.concatenate([t] * reps, axis=1)


def _put_layer(o_ref, val, whole_stack):
    if whole_stack:
        o_ref[0] = val
        for k in range(1, o_ref.shape[0]):
            o_ref[k] = jnp.zeros_like(val)
    else:
        o_ref[...] = val


def _store_kvm(kvm_ref, kv, kpe_b):
    ones = jnp.ones(kpe_b.shape, BF16)
    for p in range(4):
        kvm_ref[:, p * 256:p * 256 + LANES] = kv[:, p * LANES:(p + 1) * LANES].astype(BF16)
        kvm_ref[:, p * 256 + LANES:(p + 1) * 256] = kpe_b
        kvm_ref[:, 1024 + p * 256:1024 + p * 256 + LANES] = (
            kv[:, 512 + p * LANES:512 + (p + 1) * LANES].astype(BF16))
        kvm_ref[:, 1024 + p * 256 + LANES:1024 + (p + 1) * 256] = ones


def _mod_kernel(c_ref, w_ref, b_ref, o_ref):
    a = _silu(c_ref[...]).astype(BF16)
    o_ref[...] = _dot(a, w_ref[...].astype(BF16)) + b_ref[...]


def _mod_call(l, cond, w_mod, b_mod):
    n_l, _, n = w_mod.shape
    tn = 1536
    return pl.pallas_call(
        _mod_kernel,
        grid=(n // tn,),
        in_specs=[_const_spec(cond.shape),
                  pl.BlockSpec((None, D, tn), lambda j: (l, 0, j)),
                  pl.BlockSpec((None, 1, tn), lambda j: (l, 0, j))],
        out_specs=pl.BlockSpec((cond.shape[0], tn), lambda j: (0, j)),
        out_shape=jax.ShapeDtypeStruct((cond.shape[0], n), F32),
        compiler_params=_cparams(("arbitrary",)),
        name="mod",
    )(cond, w_mod, b_mod.reshape(n_l, 1, n))


def _inprep_kernel(latent, n_aliased, *refs):
    (x_ref, mod_ref, gpre_ref, win_ref, gmq_ref, wuq_ref, gmkv_ref, wukv_ref,
     ggq_ref, ggk_ref, bd_ref) = refs[:11]
    refs = refs[11:]
    if latent:
        cos64_ref, sin64_ref, cospe_ref, sinpe_ref = refs[:4]
        refs = refs[4:]
    refs = refs[n_aliased:]
    first_layer = n_aliased == 0
    (qm_ref, kvm_ref, gqo_ref, gkv_ref, dqo_ref, dkv_ref, ret_ref, rg_ref, gate_ref) = refs[:9]
    refs = refs[9:]
    if not latent:
        ckv_o, kpe_o, gk_o, gv_o, dk_o, dv_o = refs

    x = x_ref[...]
    mod = mod_ref[...]
    sh1 = mod[:, 0:D]
    sc1 = mod[:, D:2 * D]
    hb = (_rms(x, gpre_ref[...]) * (1.0 + sc1) + sh1).astype(BF16)

    def z(a, b):
        return _dot_nt(hb, win_ref[a:b, :])

    if latent:
        cos64, sin64 = cos64_ref[...], sin64_ref[...]
        cospe, sinpe = cospe_ref[...], sinpe_ref[...]

    cqn = _rms(z(O_CQ, O_CKV), gmq_ref[...]).astype(BF16)
    q = _dot(cqn, wuq_ref[...]) * ((MLA_NOPE + MLA_ROPE) ** -0.5 * LOG2E)
    q_nope, q_pe = q[:, 0:512], q[:, 512:768]
    if latent:
        q_pe = _rope(q_pe, _tile_lanes(cospe, 2), _tile_lanes(sinpe, 2), MLA_ROPE // 4)
    qm_ref[:, 0:512] = q_nope.astype(BF16)
    qm_ref[:, 512:768] = q_pe.astype(BF16)

    ckvn = _rms(z(O_CKV, O_KPE), gmkv_ref[...])
    kv = _dot(ckvn.astype(BF16), wukv_ref[...])
    kpe4 = _dot_nt(hb, jnp.concatenate([win_ref[O_KPE:O_GQ, :]] * 4, axis=0))
    if latent:
        kpe4 = _rope(kpe4, cospe, sinpe, MLA_ROPE // 4)
    else:
        _put_layer(ckv_o, ckvn, first_layer)
        _put_layer(kpe_o, kpe4[:, 0:MLA_ROPE], first_layer)
    _store_kvm(kvm_ref, kv, kpe4.astype(BF16))

    gq = _dot_nt(hb, jnp.concatenate(
        [win_ref[O_GQ + h * GQA_HD:O_GQ + (h + 1) * GQA_HD, :] for h in GQA_ORDER], axis=0))
    gq = gq * lax.rsqrt(_seg_meansq(gq, bd_ref, 512) + EPS) * ggq_ref[...]
    gk = z(O_GK, O_GV)
    gk = gk * lax.rsqrt(_seg_meansq(gk, bd_ref, LANES) + EPS) * ggk_ref[...]
    gv = z(O_GV, O_RQ)
    if latent:
        gq = _rope(gq, _tile_lanes(cos64, 4), _tile_lanes(sin64, 4), GQA_HD // 4)
        gk = _rope(gk, cos64, sin64, GQA_HD // 4)
    else:
        _put_layer(gk_o, jnp.transpose(gk), first_layer)
        _put_layer(gv_o, jnp.transpose(gv), first_layer)
    gqo_ref[...] = (gq * (GQA_HD ** -0.5 * LOG2E)).astype(BF16)
    gkv_ref[:, 0:LANES] = gk.astype(BF16)
    gkv_ref[:, LANES:2 * LANES] = gv.astype(BF16)
    gkv_ref[:, 2 * LANES:3 * LANES] = jnp.ones(gv.shape, BF16)

    dq = z(O_DQ, O_DK)
    dk = z(O_DK, O_DV)
    dv = z(O_DV, O_GL)
    if latent:
        dq = _rope(dq, _tile_lanes(cos64, 4), _tile_lanes(sin64, 4), DIFF_D // 4)
        dk = _rope(dk, _tile_lanes(cos64, 4), _tile_lanes(sin64, 4), DIFF_D // 4)
    else:
        _put_layer(dk_o, jnp.transpose(dk), first_layer)
        _put_layer(dv_o, dv, first_layer)
    dqo_ref[...] = (dq * (DIFF_D ** -0.5 * LOG2E)).astype(BF16)
    dkv_ref[:, 0:512] = dk.astype(BF16)
    for h in range(DIFF_HEADS):
        dkv_ref[:, 512 + h * 256:512 + h * 256 + LANES] = dv[:, h * LANES:(h + 1) * LANES].astype(BF16)
        dkv_ref[:, 512 + h * 256 + LANES:512 + (h + 1) * 256] = jnp.ones((dv.shape[0], LANES), BF16)

    ret_ref[:, 0:256] = z(O_RQ, O_RK).astype(BF16)
    ret_ref[:, 256:512] = (z(O_RK, O_RV) * (RET_DK ** -0.5)).astype(BF16)
    ret_ref[:, 512:1024] = z(O_RV, O_RG).astype(BF16)
    rg_ref[...] = z(O_RG, O_DQ).astype(BF16)

    for n in range(N_BRANCH):
        gate_ref[:, n * D:(n + 1) * D] = jax.nn.sigmoid(
            z(O_GL + n * D, O_GL + (n + 1) * D)).astype(BF16)


def _inprep_call(latent, l, x, mod3, mod_row, lw, tabs, t_len, prev_caches=None):
    n_tok = x.shape[0]
    tm = TM_LAT if latent else TM
    nblk = n_tok // tm
    blk_per_seq = t_len // tm

    def tok(w):
        return pl.BlockSpec((tm, w), lambda i: (i, 0))

    in_specs = [tok(D),
                pl.BlockSpec((None, 1, 6 * D), lambda i: (mod_row(i * tm), 0, 0)),
                _layer_spec(l, (1, D)),
                pl.BlockSpec((None, O_END, D), lambda i: (l, 0, 0), pipeline_mode=pl.Buffered(1)),
                _layer_spec(l, (1, MLA_Q_LORA)), _layer_spec(l, (MLA_Q_LORA, 768)),
                _layer_spec(l, (1, MLA_KV_LORA)), _layer_spec(l, (MLA_KV_LORA, 1024)),
                _layer_spec(l, (1, 512)), _layer_spec(l, (1, LANES)), _const_spec((512, 512))]
    args = [x, mod3, lw["g_pre1"], lw["w_in"], lw["g_mla_q"], lw["w_uq"], lw["g_mla_kv"],
            lw["w_ukv"], lw["g_gqa_q"], lw["g_gqa_k"], lw["bd"]]
    if latent:
        tab_spec = pl.BlockSpec((tm, LANES), lambda i: (i % blk_per_seq, 0))
        in_specs += [tab_spec] * 4
        args += list(tabs)
    widths = [768, KVM_W, 512, GKV_W, 512, DKV_W, 1024, 512, 4 * D]
    out_specs = [tok(w) for w in widths]
    out_shape = [jax.ShapeDtypeStruct((n_tok, w), BF16) for w in widths]
    aliases = {}
    if not latent:
        assert TM == t_len and (prev_caches is None) == (l == 0)
        n_seq = n_tok // t_len
        lead, at = ((DEPTH,), 0) if l == 0 else ((None,), l)

        def row_major(w):
            out_specs.append(pl.BlockSpec((None,) + lead + (t_len, w), lambda i: (i, at, 0, 0)))
            out_shape.append(jax.ShapeDtypeStruct((n_seq, DEPTH, t_len, w), F32))

        row_major(MLA_KV_LORA)
        row_major(MLA_ROPE)
        for w in (LANES, LANES, 512):
            out_specs.append(pl.BlockSpec((None,) + lead + (w, t_len), lambda i: (i, at, 0, 0)))
            out_shape.append(jax.ShapeDtypeStruct((n_seq, DEPTH, w, t_len), F32))
        row_major(512)
        if prev_caches is not None:
            n_in = len(args)
            in_specs += [pl.BlockSpec(memory_space=pl.ANY)] * len(prev_caches)
            args += list(prev_caches)
            aliases = {n_in + k: len(widths) + k for k in range(len(prev_caches))}
    return pl.pallas_call(
        functools.partial(_inprep_kernel, latent, len(aliases)),
        grid=(nblk,),
        in_specs=in_specs, out_specs=out_specs, out_shape=out_shape,
        input_output_aliases=aliases,
        compiler_params=_cparams(("arbitrary",)),
        name="inprep_lat" if latent else "inprep_ctx",
    )(*args)


def _pastkv_kernel(ckv_ref, kpe_ref, wukv_ref, o_ref):
    kv = _dot(ckv_ref[...].astype(BF16), wukv_ref[...])
    _store_kvm(o_ref, kv, kpe_ref[...].astype(BF16))


def _pastkv_call(l, ckv, kpe4, w_ukv):
    n = ckv.shape[0]
    return pl.pallas_call(
        _pastkv_kernel,
        grid=(n // TM,),
        in_specs=[pl.BlockSpec((TM, MLA_KV_LORA), lambda i: (i, 0)),
                  pl.BlockSpec((TM, LANES), lambda i: (i, 0)),
                  _layer_spec(l, (MLA_KV_LORA, 1024))],
        out_specs=pl.BlockSpec((TM, KVM_W), lambda i: (i, 0)),
        out_shape=jax.ShapeDtypeStruct((n, KVM_W), BF16),
        compiler_params=_cparams(("arbitrary",)),
        name="pastkv",
    )(ckv, kpe4, w_ukv)


def _softmax_pv(s, v_ones):
    m = jnp.max(s, axis=-1, keepdims=True)
    p = jnp.exp2(s - m).astype(BF16)
    o = _dot(p, v_ones)
    return o[:, 0:LANES] / o[:, LANES:2 * LANES]


def _attn_kernel(lam_init, n_past, qm_ref, kvm_ref, gq_ref, gkv_ref, dq_ref, dkv_ref, *refs):
    if n_past:
        past_refs, refs = refs[:3], refs[3:]
        lam_ref, gdiff_ref, o_ref = refs[:3]
        joined = refs[3:]

        @pl.when(pl.program_id(1) == 0)
        def _():
            for dst, past, new in zip(joined, past_refs, (kvm_ref, gkv_ref, dkv_ref)):
                dst[0:n_past, :] = past[...]
                dst[n_past:, :] = new[...]

        kvm_ref, gkv_ref, dkv_ref = joined
    else:
        lam_ref, gdiff_ref, o_ref = refs
    tq = qm_ref.shape[0]
    lane = _lane_iota((tq, LANES))
    low = lane < HALF_LANES
    zero = jnp.zeros((tq, LANES), BF16)

    for p in range(MLA_HEADS // 2):
        qn = qm_ref[:, p * LANES:(p + 1) * LANES]
        g = p // 2
        qpe = qm_ref[:, 512 + g * LANES:512 + (g + 1) * LANES]
        kk = kvm_ref[:, p * 256:(p + 1) * 256]
        vv = kvm_ref[:, 1024 + p * 256:1024 + (p + 1) * 256]
        outs = []
        for half in range(2):
            h = 2 * p + half
            slot = h % 4
            in_slot = (lane >= slot * MLA_ROPE) & (lane < (slot + 1) * MLA_ROPE)
            lhs = jnp.concatenate(
                [jnp.where(low if half == 0 else ~low, qn, zero),
                 jnp.where(in_slot, qpe, zero)], axis=1)
            outs.append(_softmax_pv(_dot_nt(lhs, kk), vv))
        o_ref[:, p * LANES:(p + 1) * LANES] = jnp.where(low, outs[0], outs[1]).astype(BF16)

    kk = gkv_ref[:, 0:LANES]
    vv = gkv_ref[:, LANES:3 * LANES]
    for g in range(GQA_HEADS // 2):
        qg = gq_ref[:, g * LANES:(g + 1) * LANES]
        o_lo = _softmax_pv(_dot_nt(jnp.where(low, qg, zero), kk), vv)
        o_hi = _softmax_pv(_dot_nt(jnp.where(low, zero, qg), kk), vv)
        o_ref[:, 512 + g * LANES:512 + (g + 1) * LANES] = jnp.where(low, o_lo, o_hi).astype(BF16)

    lp = lam_ref[...]
    lam = (jnp.exp(jnp.sum(lp[0:1] * lp[1:2], axis=-1, keepdims=True))
           - jnp.exp(jnp.sum(lp[2:3] * lp[3:4], axis=-1, keepdims=True)) + lam_init)
    for h in range(DIFF_HEADS):
        qh = dq_ref[:, h * LANES:(h + 1) * LANES]
        kk = dkv_ref[:, h * LANES:(h + 1) * LANES]
        vv = dkv_ref[:, 512 + h * 256:512 + (h + 1) * 256]
        a1 =_softmax_pv(_dot_nt(jnp.where(low, qh, zero), kk), vv)
        a2 = _softmax_pv(_dot_nt(jnp.where(low, zero, qh), kk), vv)
        od = _rms(a1 - lam * a2, gdiff_ref[...]) * (1.0 - lam_init)
        o_ref[:, 1024 + h * LANES:1024 + (h + 1) * LANES] = od.astype(BF16)


def _attn_call(l, lam_init, qm, kvm, gq, gkv, dq, dkv, lam_p, g_diff, n_b, t_len, past=None):
    nq = t_len // TQ
    n_past = 0 if past is None else past[0].shape[0] // n_b

    def qspec(w):
        return pl.BlockSpec((TQ, w), lambda b, i: (b * nq + i, 0))

    def kspec(w, rows=t_len):
        return pl.BlockSpec((rows, w), lambda b, i: (b, 0))

    in_specs = [qspec(768), kspec(KVM_W), qspec(512), kspec(GKV_W), qspec(512), kspec(DKV_W)]
    args = [qm, kvm, gq, gkv, dq, dkv]
    scratch = []
    if n_past:
        in_specs += [kspec(KVM_W, n_past), kspec(GKV_W, n_past), kspec(DKV_W, n_past)]
        args += list(past)
        scratch = [pltpu.VMEM((n_past + t_len, w), BF16) for w in (KVM_W, GKV_W, DKV_W)]
    in_specs += [_layer_spec(l, (4, DIFF_D)), _layer_spec(l, (1, DIFF_DV))]
    args += [lam_p, g_diff]
    return pl.pallas_call(
        functools.partial(_attn_kernel, lam_init, n_past),
        grid=(n_b, nq),
        in_specs=in_specs,
        out_specs=qspec(3 * BRANCH_W),
        out_shape=jax.ShapeDtypeStruct((n_b * t_len, 3 * BRANCH_W), BF16),
        scratch_shapes=scratch,
        compiler_params=_cparams(("arbitrary", "arbitrary")),
        name="attn",
    )(*args)


def _log_sigmoid(x):
    return jnp.minimum(x, 0.0) - jnp.log(1.0 + jnp.exp(-jnp.abs(x)))


def _log_gamma(dec_ref, l, d, h):
    return _log_sigmoid(jnp.full((1, 1), dec_ref[l, d, h], F32))


def _ret_kernel(latent, l, t_len, dec_ref, q_ref, k_ref, v_ref, rg_ref, gret_ref, *refs):
    if latent:
        s0_ref, o_ref = refs
    else:
        o_ref, st_ref = refs[-2:]
    tq = q_ref.shape[0]
    t0 = pl.program_id(1) * tq
    lane = _lane_iota((tq, LANES))
    low = lane < HALF_LANES
    zero = jnp.zeros((tq, LANES), BF16)
    t_idx = (t0 + lax.broadcasted_iota(jnp.int32, (tq, t_len), 0)).astype(F32)
    s_idx = lax.broadcasted_iota(jnp.int32, (tq, t_len), 1).astype(F32)
    dist = t_idx - s_idx
    past = dist >= 0
    diag = jnp.where(dist == 0, 1.0, 0.0)
    t_col = (t0 + lax.broadcasted_iota(jnp.int32, (tq, 1), 0)).astype(F32)

    def lg(d, h):
        return _log_gamma(dec_ref, l, d, h)

    for h in range(RET_HEADS):
        p, half = h // 2, h % 2
        qp = q_ref[:, p * LANES:(p + 1) * LANES]
        qm = jnp.where(low if half == 0 else ~low, qp, zero)
        kp = k_ref[:, p * LANES:(p + 1) * LANES]
        vh = v_ref[:, h * LANES:(h + 1) * LANES]
        lgf, lgb = lg(0, h), lg(1, h)
        dmask = jnp.exp(jnp.where(past, lgf, -lgb) * dist) + diag
        o = _dot((_dot_nt(qm, kp) * dmask).astype(BF16), vh)
        if latent:
            sf =s0_ref[0, p].astype(BF16)
            sb = s0_ref[1, p].astype(BF16)
            o = o + _dot(qm, sf) * jnp.exp(lgf * (t_col + 1.0))
            o = o + _dot(qm, sb) * jnp.exp(lgb * (float(t_len) - t_col))
        mu = jnp.mean(o, axis=-1, keepdims=True)
        oc = o - mu
        y = oc * lax.rsqrt(jnp.mean(oc * oc, axis=-1, keepdims=True) + EPS)
        y = y * gret_ref[:, h * LANES:(h + 1) * LANES]
        rg = rg_ref[:, h * LANES:(h + 1) * LANES].astype(F32)
        o_ref[:, h * LANES:(h + 1) * LANES] = (y * _silu(rg)).astype(BF16)

    if not latent:
        s_col = lax.broadcasted_iota(jnp.int32, (t_len, 1), 0).astype(F32)
        lane_t = _lane_iota((1, LANES)) < HALF_LANES
        for p in range(RET_HEADS // 2):
            kp = k_ref[:, p * LANES:(p + 1) * LANES].astype(F32)
            for d in range(2):
                lg_lane = jnp.where(lane_t, lg(d, 2 * p), lg(d, 2 * p + 1))
                expo = (float(t_len) - 1.0 - s_col) if d == 0 else s_col
                kdec_t = jnp.transpose(kp * jnp.exp(lg_lane * expo)).astype(BF16)
                for half in range(2):
                    h = 2 * p + half
                    st = _dot(kdec_t, v_ref[:, h * LANES:(h + 1) * LANES])
                    st = st[half * RET_DK:(half + 1) * RET_DK, :]
                    if l == 0:
                        st_ref[0, d, h] = st
                        for k in range(1, st_ref.shape[0]):
                            st_ref[k, d, h] = jnp.zeros_like(st)
                    else:
                        st_ref[d, h] = st


def _ret_call(latent, l, dec, ret, rg, g_ret, s0, n_b, t_len, prev_state=None):
    nq = t_len // TQ
    aliases = {}
    assert latent or nq == 1
    in_specs = [pl.BlockSpec(memory_space=pltpu.SMEM),
                pl.BlockSpec((TQ, 256), lambda b, i: (b * nq + i, 0)),
                pl.BlockSpec((t_len, 256), lambda b, i: (b, 1)),
                pl.BlockSpec((t_len, 512), lambda b, i: (b, 1)),
                pl.BlockSpec((TQ, 512), lambda b, i: (b * nq + i, 0)),
                _layer_spec(l, (1, 512))]
    args = [dec, ret, ret, ret, rg, g_ret]
    out_specs = [pl.BlockSpec((TQ, 512), lambda b, i: (b * nq + i, 0))]
    out_shape = [jax.ShapeDtypeStruct((n_b * t_len, 512), BF16)]
    if latent:
        in_specs.append(pl.BlockSpec((None, 2, 2, LANES, LANES), lambda b, i: (b, 0, 0, 0, 0)))
        args.append(s0)
    else:
        assert (prev_state is None) == (l == 0)
        lead, at = ((DEPTH,), 0) if l == 0 else ((None,), l)
        out_specs.append(pl.BlockSpec((None,) + lead + (2, RET_HEADS, RET_DK, RET_DV),
                                      lambda b, i: (b, at, 0, 0, 0, 0)))
        out_shape.append(jax.ShapeDtypeStruct((n_b, DEPTH, 2, RET_HEADS, RET_DK, RET_DV), F32))
        if prev_state is not None:
            aliases = {len(args): 1}
            in_specs.append(pl.BlockSpec(memory_space=pl.ANY))
            args.append(prev_state)
    return pl.pallas_call(
        functools.partial(_ret_kernel, latent, l, t_len),
        grid=(n_b, nq),
        in_specs=in_specs, out_specs=out_specs, out_shape=out_shape,
        input_output_aliases=aliases,
        compiler_params=_cparams(("arbitrary", "arbitrary")),
        name="ret_lat" if latent else "ret_ctx",
    )(*args)


def _merge_kernel(x_ref, mod_ref, br_ref, or_ref, gate_ref, wbr_ref, wout_ref, gpost_ref, o_ref):
    merged = None
    for n in range(N_BRANCH):
        if n < 2:
            b = br_ref[:, n * BRANCH_W:(n + 1) * BRANCH_W]
        elif n == 2:
            b = or_ref[...]
        else:
            b = br_ref[:, 2 * BRANCH_W:3 * BRANCH_W]
        t = gate_ref[:, n * D:(n + 1) * D].astype(F32) * _dot(b, wbr_ref[n])
        merged = t if merged is None else merged + t
    out = _dot(merged.astype(BF16), wout_ref[...])
    g1 = mod_ref[...][:, 2 * D:3 * D]
    o_ref[...] = x_ref[...] + g1 * _rms(out, gpost_ref[...])


def _merge_call(l, x, mod3, mod_row, br, o_r, gates, lw):
    n_tok = x.shape[0]

    def tok(w):
        return pl.BlockSpec((TM_MERGE, w), lambda i: (i, 0))

    return pl.pallas_call(
        _merge_kernel,
        grid=(n_tok // TM_MERGE,),
        in_specs=[tok(D), pl.BlockSpec((None, 1, 6 * D), lambda i: (mod_row(i * TM_MERGE), 0, 0)),
                  tok(3 * BRANCH_W), tok(BRANCH_W), tok(4 * D),
                  _layer_spec(l, (N_BRANCH, BRANCH_W, D)), _layer_spec(l, (D, D)),
                  _layer_spec(l, (1, D))],
        out_specs=tok(D),
        out_shape=jax.ShapeDtypeStruct((n_tok, D), F32),
        compiler_params=_cparams(("arbitrary",)),
        name="merge",
    )(x, mod3, br, o_r, gates, lw["w_br"], lw["w_out"], lw["g_post1"])


def _route(logits_t, bias):
    n = logits_t.shape[1]
    scores = jax.nn.sigmoid(logits_t)
    sel = scores + bias
    neg = -jnp.inf
    sub = lax.broadcasted_iota(jnp.int32, (GROUP_SIZE, n), 0)
    grp = []
    for g in range(N_GROUPS):
        blk = sel[g * GROUP_SIZE:(g + 1) * GROUP_SIZE]
        m1 = jnp.max(blk, axis=0, keepdims=True)
        i1 = jnp.min(jnp.where(blk == m1, sub, GROUP_SIZE), axis=0, keepdims=True)
        m2 = jnp.max(jnp.where(sub == i1, neg, blk), axis=0, keepdims=True)
        grp.append(m1 + m2)
    parts = []
    for g in range(N_GROUPS):
        beaten = jnp.zeros((1, n), jnp.int32)
        for o in range(N_GROUPS):
            if o == g:
                continue
            wins = (grp[o] > grp[g]) | (grp[o] == grp[g]) if o < g else (grp[o] > grp[g])
            beaten = beaten + wins.astype(jnp.int32)
        keep = beaten < TOPK_GROUPS
        parts.append(jnp.where(keep, sel[g * GROUP_SIZE:(g + 1) * GROUP_SIZE], neg))
    cur = jnp.concatenate(parts, axis=0)
    eidx = lax.broadcasted_iota(jnp.int32, (N_EXPERTS, n), 0)
    hits, ids, ws = [], [], []
    for _ in range(TOP_K):
        m = jnp.max(cur, axis=0, keepdims=True)
        i = jnp.min(jnp.where(cur == m, eidx, N_EXPERTS), axis=0, keepdims=True)
        hit = eidx == i
        hits.append(hit)
        ids.append(i)
        ws.append(jnp.sum(jnp.where(hit, scores, 0.0), axis=0, keepdims=True))
        cur = jnp.where(hit, neg, cur)
    wsum = ws[0] + ws[1] + ws[2] + ws[3]
    return hits, ids, [w / wsum * ROUTE_SCALE for w in ws]


U32 = jnp.uint32
HIGH16 = np.uint32(0xFFFF0000)


def _bf16_bits(v):
    return lax.bitcast_convert_type(v.astype(BF16).astype(F32), U32)


def _pack_rows(v):
    return (_bf16_bits(v[:, 0:D // 2]) >> 16) | _bf16_bits(v[:, D // 2:D])


def _unpack_rows(p):
    lo = lax.bitcast_convert_type(p << 16, F32)
    hi = lax.bitcast_convert_type(p & HIGH16, F32)
    return jnp.concatenate([lo, hi], axis=1)


def _moe_pre_kernel(x_ref, mod_ref, gpre_ref, wr_ref, br_ref, tri_ref,
                    hp_ref, eidx_ref, rank_ref, comb_ref, cnt_ref, run_ref):
    tm = x_ref.shape[0]

    @pl.when(pl.program_id(0) == 0)
    def _():
        run_ref[...] = jnp.zeros_like(run_ref)

    mod = mod_ref[...]
    sh2, sc2 = mod[:, 3 * D:4 * D], mod[:, 4 * D:5 * D]
    h = _rms(x_ref[...], gpre_ref[...]) * (1.0 + sc2) + sh2
    hp_ref[...] = _pack_rows(h)
    hb = h.astype(BF16)
    h_lo = (h - hb.astype(F32)).astype(BF16)
    wr = wr_ref[...]
    wr_hi = wr.astype(BF16)
    wr_lo = (wr - wr_hi.astype(F32)).astype(BF16)
    logits_t = _dot_nt(wr_hi, hb) + _dot_nt(wr_hi, h_lo) + _dot_nt(wr_lo, hb)
    hits, ids, ws = _route(logits_t, br_ref[...])

    picked = jnp.zeros((N_EXPERTS, tm), F32)
    for hit in hits:
        picked = jnp.where(hit, 1.0, picked)
    before = _dot(picked.astype(BF16), tri_ref[...]) + run_ref[:, 0:1]
    sub8 = lax.broadcasted_iota(jnp.int32, (8, tm), 0)
    comb8 = jnp.zeros((8, tm), F32)
    for k in range(TOP_K):
        rank = jnp.sum(jnp.where(hits[k], before, 0.0), axis=0, keepdims=True)
        eidx_ref[k:k + 1, :] = ids[k]
        rank_ref[k:k + 1, :] = rank.astype(jnp.int32)
        comb8 = jnp.where(sub8 == k, ws[k], comb8)
    comb_ref[...] = jnp.transpose(
        jnp.concatenate([comb8, jnp.zeros((LANES - 8, tm), F32)], axis=0))
    run_ref[...] = run_ref[...] + jnp.sum(picked, axis=1, keepdims=True)
    cnt_ref[...] = run_ref[...]


def _moe_pre_call(l, x, mod3, mod_row, lw):
    n_tok = x.shape[0]
    tm = TM_MOE_PRE
    tri = np.arange(tm)
    tri = jnp.asarray(tri[:, None] < tri[None, :], BF16)
    row4 = pl.BlockSpec((TOP_K, tm), lambda i: (0, i))
    return pl.pallas_call(
        _moe_pre_kernel,
        grid=(n_tok // tm,),
        in_specs=[pl.BlockSpec((tm, D), lambda i: (i, 0)),
                  pl.BlockSpec((None, 1, 6 * D), lambda i: (mod_row(i * tm), 0, 0)),
                  _layer_spec(l, (1, D)), _layer_spec(l, (N_EXPERTS, D)),
                  _layer_spec(l, (N_EXPERTS, 1)), _const_spec((tm, tm))],
        out_specs=[pl.BlockSpec((tm, D // 2), lambda i: (i, 0)), row4, row4,
                   pl.BlockSpec((tm, LANES), lambda i: (i, 0)),
                   _const_spec((N_EXPERTS, LANES))],
        out_shape=[jax.ShapeDtypeStruct((n_tok, D // 2), U32),
                   jax.ShapeDtypeStruct((TOP_K, n_tok), jnp.int32),
                   jax.ShapeDtypeStruct((TOP_K, n_tok), jnp.int32),
                   jax.ShapeDtypeStruct((n_tok, LANES), F32),
                   jax.ShapeDtypeStruct((N_EXPERTS, LANES), F32)],
        scratch_shapes=[pltpu.VMEM((N_EXPERTS, LANES), F32)],
        compiler_params=_cparams(("arbitrary",)),
        name="moe_pre",
    )(x, mod3, lw["g_pre2"], lw["w_router_t"], lw["b_router"], tri)


def _moe_plan_kernel(eidx_ref, rank_ref, cnt_ref, dest_ref, te_ref, tv_ref, tn_ref):
    tm = eidx_ref.shape[1]
    cnt = cnt_ref[...]
    padded = jnp.ceil(cnt * (1.0 / TMX)) * TMX
    row = lax.broadcasted_iota(jnp.int32, cnt.shape, 0)
    incl = padded
    shift = 1
    while shift < N_EXPERTS:
        incl = incl + jnp.where(row >= shift, pltpu.roll(incl, shift, 0), 0.0)
        shift *= 2
    start = (incl - padded)[:, 0:1]
    end = incl[:, 0:1]
    erow = lax.broadcasted_iota(jnp.int32, (N_EXPERTS, tm), 0)
    for k in range(TOP_K):
        mine = erow == eidx_ref[k:k + 1, :]
        base = jnp.sum(jnp.where(mine, start, 0.0), axis=0, keepdims=True)
        dest_ref[k:k + 1, :] = rank_ref[k:k + 1, :] + base.astype(jnp.int32)

    @pl.when(pl.program_id(0) == 0)
    def _():
        tile0 = (_lane_iota((1, LANES)) * TMX).astype(F32)
        owner = jnp.sum(jnp.where(end <= tile0, 1.0, 0.0), axis=0, keepdims=True)
        owner = jnp.minimum(owner, N_EXPERTS - 1.0)
        erow_t = lax.broadcasted_iota(jnp.int32, (N_EXPERTS, LANES), 0).astype(F32)
        left = jnp.sum(jnp.where(erow_t == owner, cnt[:, 0:1] - (tile0 - start), 0.0),
                       axis=0, keepdims=True)
        te_ref[...] = owner.astype(jnp.int32)
        tv_ref[...] = jnp.clip(left, 0.0, float(TMX)).astype(jnp.int32)
        tn_ref[...] = jnp.full(tn_ref.shape, N_EXPERTS, jnp.int32)
        nxt = owner
        for k in range(W_SLOTS - 1):
            later = (erow_t > nxt) & (cnt[:, 0:1] > 0.0)
            nxt = jnp.min(jnp.where(later, erow_t, float(N_EXPERTS)), axis=0, keepdims=True)
            tn_ref[k:k + 1, :] = nxt.astype(jnp.int32)
        n_used = jnp.sum(jnp.where(left > 0.0, 1.0, 0.0), axis=1, keepdims=True)
        tn_ref[7:8, :] = jnp.minimum(tile0 * (1.0 / TMX), n_used - 1.0).astype(jnp.int32)


def _moe_plan_call(eidx, rank, cnt):
    n_tok = eidx.shape[1]
    tm = TM_MOE_PRE
    row4 = pl.BlockSpec((TOP_K, tm), lambda i: (0, i))
    tiles = jax.ShapeDtypeStruct((1, LANES), jnp.int32)
    return pl.pallas_call(
        _moe_plan_kernel,
        grid=(n_tok // tm,),
        in_specs=[row4, row4, _const_spec((N_EXPERTS, LANES))],
        out_specs=[row4, _const_spec((1, LANES)), _const_spec((1, LANES)), _const_spec((8, LANES))],
        out_shape=[jax.ShapeDtypeStruct((TOP_K, n_tok), jnp.int32), tiles, tiles,
                   jax.ShapeDtypeStruct((8, LANES), jnp.int32)],
        compiler_params=_cparams(("arbitrary",)),
        name="moe_plan",
    )(eidx, rank, cnt)


def _experts_kernel(l, te_ref, tv_ref, tn_ref, xs_ref, wgu_hbm, wdn_hbm, ys_ref,
                    wgu_f, wdn_f, wgu_b, wdn_b, sem, group_ref):
    j = pl.program_id(0)
    valid = tv_ref[j]
    expert = te_ref[j]

    def fetch(e, slot):
        return (pltpu.make_async_copy(wgu_hbm.at[l, e], wgu_f.at[slot], sem.at[slot, 0]),
                pltpu.make_async_copy(wdn_hbm.at[l, e], wdn_f.at[slot], sem.at[slot, 1]))

    def start_if_any(e, slot):
        @pl.when(e < N_EXPERTS)
        def _():
            for cp in fetch(e, slot):
                cp.start()

    @pl.when(j == 0)
    def _():
        group_ref[0] = 0
        start_if_any(expert, 0)
        for k in range(W_SLOTS - 2):
            start_if_any(tn_ref[k, 0], k + 1)

    first_tile = (j == 0) | (expert != te_ref[jnp.maximum(j - 1, 0)])

    @pl.when(first_tile & (valid > 0))
    def _():
        group = group_ref[0]
        slot = lax.rem(group, W_SLOTS)
        for cp in fetch(expert, slot):
            cp.wait()
        wgu_b[...] = wgu_f[slot].astype(BF16)
        wdn_b[...] = wdn_f[slot].astype(BF16)
        start_if_any(tn_ref[W_SLOTS - 2, j], lax.rem(group + W_SLOTS - 1, W_SLOTS))
        group_ref[0] = group + 1

    @pl.when(valid > 0)
    def _():
        rows = lax.broadcasted_iota(jnp.int32, (TMX, D), 0)
        x = jnp.where(rows < valid, _unpack_rows(xs_ref[...]), 0.0).astype(BF16)
        gu = _dot(x, wgu_b[...])
        a = _silu(gu[:, 0:EXPERT_FF]) * gu[:, EXPERT_FF:2 * EXPERT_FF]
        ys_ref[...] = _pack_rows(_dot(a.astype(BF16), wdn_b[...]))


def _experts_call(l, xs, te, tv, tn, w_gu, w_dn):
    n_tiles = xs.shape[0] // TMX
    grid_spec = pltpu.PrefetchScalarGridSpec(
        num_scalar_prefetch=3,
        grid=(n_tiles,),
        in_specs=[pl.BlockSpec((TMX, D // 2), lambda j, te, tv, tn: (tn[7, j], 0)),
                  pl.BlockSpec(memory_space=pl.ANY), pl.BlockSpec(memory_space=pl.ANY)],
        out_specs=pl.BlockSpec((TMX, D // 2), lambda j, te, tv, tn: (tn[7, j], 0)),
        scratch_shapes=[pltpu.VMEM((W_SLOTS, D, 2 * EXPERT_FF), F32),
                        pltpu.VMEM((W_SLOTS, EXPERT_FF, D), F32),
                        pltpu.VMEM((D, 2 * EXPERT_FF), BF16), pltpu.VMEM((EXPERT_FF, D), BF16),
                        pltpu.SemaphoreType.DMA((W_SLOTS, 2)), pltpu.SMEM((1,), jnp.int32)])
    return pl.pallas_call(
        functools.partial(_experts_kernel, l),
        grid_spec=grid_spec,
        out_shape=jax.ShapeDtypeStruct(xs.shape, U32),
        compiler_params=_cparams(("arbitrary",)),
        name="moe_experts",
    )(te, tv, tn, xs, w_gu, w_dn)


def _moe_post_kernel(x_ref, mod_ref, hp_ref, yg_ref, comb_ref, wsgu_ref, wsdn_ref, gpost_ref,
                     o_ref):
    hb = _unpack_rows(hp_ref[...]).astype(BF16)
    sgu = _dot(hb, wsgu_ref[...])
    sa = _silu(sgu[:, 0:SHARED_FF]) * sgu[:, SHARED_FF:2 * SHARED_FF]
    acc = _dot(sa.astype(BF16), wsdn_ref[...])
    comb = comb_ref[...]
    for k in range(TOP_K):
        acc = acc + comb[:, k:k + 1] * _unpack_rows(yg_ref[k])
    g2 = mod_ref[...][:, 5 * D:6 * D]
    o_ref[...] = x_ref[...] + g2 * _rms(acc, gpost_ref[...])


def _moe_post_call(l, x, mod3, mod_row, hp, yg, comb, lw):
    n_tok = x.shape[0]
    tm = TM_MOE_PRE
    return pl.pallas_call(
        _moe_post_kernel,
        grid=(n_tok // tm,),
        in_specs=[pl.BlockSpec((tm, D), lambda i: (i, 0)),
                  pl.BlockSpec((None, 1, 6 * D), lambda i: (mod_row(i * tm), 0, 0)),
                  pl.BlockSpec((tm, D // 2), lambda i: (i, 0)),
                  pl.BlockSpec((TOP_K, tm, D // 2), lambda i: (0, i, 0)),
                  pl.BlockSpec((tm, LANES), lambda i: (i, 0)),
                  _layer_spec(l, (D, 2 * SHARED_FF)), _layer_spec(l, (SHARED_FF, D)),
                  _layer_spec(l, (1, D))],
        out_specs=pl.BlockSpec((tm, D), lambda i: (i, 0)),
        out_shape=jax.ShapeDtypeStruct((n_tok, D), F32),
        compiler_params=_cparams(("arbitrary",)),
        name="moe_post",
    )(x, mod3, hp, yg, comb, lw["w_sh_gu"], lw["w_sh_down"], lw["g_post2"])


def _moe_call(l, x, mod3, mod_row, lw):
    n_tok = x.shape[0]
    n_slots = -(-(TOP_K * n_tok + N_EXPERTS * (TMX - 1)) // TMX) * TMX
    assert n_slots // TMX <= LANES
    hp, eidx, rank, comb, cnt = _moe_pre_call(l, x, mod3, mod_row, lw)
    dest, te, tv, tn = _moe_plan_call(eidx, rank, cnt)
    dest = dest.reshape(TOP_K * n_tok)
    xs = _sc_scatter_rows(hp, dest, n_slots)
    ys = _experts_call(l, xs, te[0], tv[0], tn, lw["w_exp_gu"], lw["w_exp_down"])
    yg = _sc_gather_rows(ys, dest).reshape(TOP_K, n_tok, D // 2)
    return _moe_post_call(l, x, mod3, mod_row, hp, yg, comb, lw)


SC_CORES, SC_SUBCORES = 2, 16
SC_WORKERS = SC_CORES * SC_SUBCORES


def _sc_gather_rows(table, idx, chunk=64):
    n_out, width = idx.shape[0], table.shape[1]
    per_worker = n_out // SC_WORKERS
    n_chunks = per_worker // chunk
    assert per_worker * SC_WORKERS == n_out and n_chunks * chunk == per_worker
    mesh = plsc.VectorSubcoreMesh(core_axis_name="c", subcore_axis_name="s",
                                  num_cores=SC_CORES, num_subcores=SC_SUBCORES)

    @functools.partial(
        pl.kernel, mesh=mesh,
        out_type=jax.ShapeDtypeStruct((n_out, width), table.dtype),
        scratch_types=[pltpu.VMEM((chunk,), jnp.int32), pltpu.VMEM((chunk, width), table.dtype),
                       pltpu.SemaphoreType.DMA],
        name="sc_gather")
    def gather(table_hbm, idx_hbm, out_hbm, idx_v, rows_v, sem):
        base = (lax.axis_index("s") * SC_CORES + lax.axis_index("c")) * per_worker

        @pl.loop(0, n_chunks)
        def _(j):
            off = base + j * chunk
            pltpu.sync_copy(idx_hbm.at[pl.ds(off, chunk)], idx_v)
            pltpu.async_copy(table_hbm.at[idx_v], rows_v, sem).wait()
            pltpu.sync_copy(rows_v, out_hbm.at[pl.ds(off, chunk)])

    return gather(table, idx)


def _sc_scatter_rows(rows, dest, n_slots, chunk=64):
    n_tok, width = rows.shape
    per_worker = n_tok // SC_WORKERS
    n_chunks = per_worker // chunk
    assert per_worker * SC_WORKERS == n_tok and n_chunks * chunk == per_worker
    mesh = plsc.VectorSubcoreMesh(core_axis_name="c", subcore_axis_name="s",
                                  num_cores=SC_CORES, num_subcores=SC_SUBCORES)

    @functools.partial(
        pl.kernel, mesh=mesh,
        out_type=jax.ShapeDtypeStruct((n_slots, width), rows.dtype),
        scratch_types=[pltpu.VMEM((chunk,), jnp.int32), pltpu.VMEM((chunk, width), rows.dtype)],
        name="sc_scatter")
    def scatter(rows_hbm, dest_hbm, out_hbm, idx_v, rows_v):
        base = (lax.axis_index("s") * SC_CORES + lax.axis_index("c")) * per_worker

        @pl.loop(0, n_chunks)
        def _(j):
            off = base + j * chunk
            pltpu.sync_copy(rows_hbm.at[pl.ds(off, chunk)], rows_v)
            for k in range(TOP_K):
                pltpu.sync_copy(dest_hbm.at[pl.ds(k * n_tok + off, chunk)], idx_v)
                pltpu.sync_copy(rows_v, out_hbm.at[idx_v])

    return scatter(rows, dest)


def _rope_tables(t_len):
    pos = np.arange(t_len)
    row, col = pos // GRID_W, pos % GRID_W

    def tab(r):
        half = r // 2
        freq = ROPE_BASE ** (-np.arange(half, dtype=np.float64) / half)
        sign = np.concatenate([-np.ones(half), np.ones(half)])
        cs, sn = [], []
        for p in (row, col):
            ang = p[:, None].astype(np.float64) * freq[None, :]
            cs.append(np.concatenate([np.cos(ang), np.cos(ang)], axis=1))
            sn.append(np.concatenate([np.sin(ang), np.sin(ang)], axis=1) * sign[None, :])
        return np.concatenate(cs, axis=1), np.concatenate(sn, axis=1)

    c64, s64 = tab(GQA_HD // 2)
    cpe, spe = tab(MLA_ROPE // 2)
    out = (np.tile(c64, (1, 2)), np.tile(s64, (1, 2)), np.tile(cpe, (1, 4)), np.tile(spe, (1, 4)))
    return tuple(jnp.asarray(a, F32) for a in out)


def _prep_weights(p):
    n_l = p["w_in"].shape[0]

    def row(name):
        return p[name].reshape(n_l, 1, -1)

    w_uq = p["w_mla_uq"].reshape(n_l, MLA_Q_LORA, MLA_HEADS, MLA_NOPE + MLA_ROPE)
    w_uq = jnp.concatenate([w_uq[..., :MLA_NOPE].reshape(n_l, MLA_Q_LORA, -1),
                            w_uq[..., MLA_NOPE:].reshape(n_l, MLA_Q_LORA, -1)], axis=-1)
    w_ukv = p["w_mla_ukv"].reshape(n_l, MLA_KV_LORA, MLA_HEADS, MLA_NOPE + MLA_V)
    w_ukv = jnp.concatenate([w_ukv[..., :MLA_NOPE].reshape(n_l, MLA_KV_LORA, -1),
                             w_ukv[..., MLA_NOPE:].reshape(n_l, MLA_KV_LORA, -1)], axis=-1)
    w_br = p["w_br"]
    w_br_gqa = w_br[:, 1].reshape(n_l, GQA_HEADS, GQA_HD, D)[:, jnp.array(GQA_ORDER)]
    w_br = jnp.concatenate([w_br[:, 0:1], w_br_gqa.reshape(n_l, 1, BRANCH_W, D), w_br[:, 2:4]], axis=1)
    blk = np.arange(512) // GQA_HD
    return {
        "g_pre1": row("g_pre1"), "g_post1": row("g_post1"),
        "g_pre2": row("g_pre2"), "g_post2": row("g_post2"),
        "w_in": jnp.swapaxes(p["w_in"], 1, 2).astype(BF16),
        "g_mla_q": row("g_mla_q"), "w_uq": w_uq.astype(BF16),
        "g_mla_kv": row("g_mla_kv"), "w_ukv": w_ukv.astype(BF16),
        "g_gqa_q": jnp.tile(p["g_gqa_q"], (1, GQA_HEADS)).reshape(n_l, 1, -1),
        "g_gqa_k": jnp.tile(p["g_gqa_k"], (1, GQA_KV_HEADS)).reshape(n_l, 1, -1),
        "bd": jnp.asarray(blk[:, None] == blk[None, :], BF16),
        "ret_decay": p["ret_decay"],
        "g_ret": row("g_ret"),
        "diff_lambda": p["diff_lambda"], "g_diff": row("g_diff"),
        "w_br": w_br.astype(BF16), "w_out": p["w_out"].astype(BF16),
        "w_router_t": jnp.swapaxes(p["w_router"], 1, 2),
        "b_router": p["b_router"].reshape(n_l, N_EXPERTS, 1),
        "w_exp_gu": p["w_exp_gu"], "w_exp_down": p["w_exp_down"],
        "w_sh_gu": p["w_sh_gu"].astype(BF16), "w_sh_down": p["w_sh_down"].astype(BF16),
    }


def _mixers(latent, l, x, mod3, mod_row, lw, n_b, t_len, tabs=None, past=None, s0=None,
            prev_cache=None):
    lam_init = 0.8 - 0.6 * math.exp(-0.3 * l)
    outs = _inprep_call(latent, l, x, mod3, mod_row, lw, tabs, t_len,
                        None if prev_cache is None else prev_cache[:6])
    qm, kvm, gq, gkv, dq, dkv, ret, rg, gates = outs[:9]
    br = _attn_call(l, lam_init, qm, kvm, gq, gkv, dq, dkv, lw["diff_lambda"], lw["g_diff"],
                    n_b, t_len, past)
    r = _ret_call(latent, l, lw["ret_decay"], ret, rg, lw["g_ret"], s0, n_b, t_len,
                  None if prev_cache is None else prev_cache[6])
    y = _merge_call(l, x, mod3, mod_row, br, r[0], gates, lw)
    cache = None if latent else tuple(outs[9:]) + (r[1],)
    return y, cache


def kernel(x_prompt, x_sample, cache_mla_ckv, cache_mla_kpe, cache_gqa_k, cache_gqa_v, cache_diff_k, cache_diff_v, state_ret, c, c_ctx, w_mod, b_mod, g_pre1, g_post1, g_pre2, g_post2, w_in, g_mla_q, w_mla_uq, g_mla_kv, w_mla_ukv, g_gqa_q, g_gqa_k, ret_decay, g_ret, diff_lambda, g_diff, w_br, w_out, w_router, b_router, w_exp_gu, w_exp_down, w_sh_gu, w_sh_down):
    params = dict(w_in=w_in, g_pre1=g_pre1, g_post1=g_post1, g_pre2=g_pre2,
                  g_post2=g_post2, g_mla_q=g_mla_q, w_mla_uq=w_mla_uq,
                  g_mla_kv=g_mla_kv, w_mla_ukv=w_mla_ukv, g_gqa_q=g_gqa_q, g_gqa_k=g_gqa_k,
                  ret_decay=ret_decay, g_ret=g_ret, diff_lambda=diff_lambda, g_diff=g_diff,
                  w_br=w_br, w_out=w_out, w_router=w_router, b_router=b_router,
                  w_exp_gu=w_exp_gu, w_exp_down=w_exp_down, w_sh_gu=w_sh_gu, w_sh_down=w_sh_down)
    n_bc, t_c, _ = x_prompt.shape
    n_bl, t_l, _ = x_sample.shape
    p_len = cache_mla_ckv.shape[2]
    tabs = _rope_tables(t_l)
    n_cond = 8
    cond = jnp.concatenate([c_ctx[None, :], c, jnp.zeros((n_cond - 1 - n_bl, D), F32)], axis=0)
    assert t_l % TM_MERGE == 0 and (t_c * n_bc) % TM_MERGE == 0
    assert t_l % TM_MOE_PRE == 0 and (t_c * n_bc) % TM_MOE_PRE == 0

    yp = x_prompt.reshape(n_bc * t_c, D)
    ys = x_sample.reshape(n_bl * t_l, D)
    cache = None
    lw = _prep_weights(params)
    for l in range(DEPTH):
        mod3 = _mod_call(l, cond, w_mod, b_mod).reshape(n_cond, 1, 6 * D)
        yp, cache = _mixers(False, l, yp, mod3, lambda i: 0, lw, n_bc, t_c, prev_cache=cache)
        yp = _moe_call(l, yp, mod3, lambda i: 0, lw)
        past_kvm = _pastkv_call(l, cache_mla_ckv[:, l].reshape(n_bl * p_len, -1),
                                jnp.tile(cache_mla_kpe[:, l].reshape(n_bl * p_len, -1), (1, 4)),
                                lw["w_ukv"])
        past_gkv = jnp.concatenate([cache_gqa_k[:, l].reshape(n_bl * p_len, -1),
                                    cache_gqa_v[:, l].reshape(n_bl * p_len, -1),
                                    jnp.ones((n_bl * p_len, LANES), F32)], axis=-1).astype(BF16)
        past_dv = jnp.concatenate([cache_diff_v[:, l], jnp.ones_like(cache_diff_v[:, l])], axis=-1)
        past_dkv = jnp.concatenate([cache_diff_k[:, l].reshape(n_bl * p_len, -1),
                                    past_dv.reshape(n_bl * p_len, -1)], axis=-1).astype(BF16)
        s0 = state_ret[:, l].reshape(n_bl, 2, RET_HEADS // 2, 2 * RET_DK, RET_DV)
        ys, _ = _mixers(True, l, ys, mod3, lambda t: 1 + t // t_l, lw, n_bl, t_l, tabs=tabs,
                        past=(past_kvm, past_gkv, past_dkv), s0=s0)
        ys = _moe_call(l, ys, mod3, lambda t: 1 + t // t_l, lw)

    ckv, kpe, gk_t, gv_t, dk_t, dv, ret_state = cache

    def time_minor(a, shape):
        a = a.reshape((n_bc, DEPTH) + shape + (t_c,))
        return jnp.transpose(a, (0, 1, a.ndim - 1) + tuple(range(2, a.ndim - 1)))

    return (yp.reshape(n_bc, t_c, D), ys.reshape(n_bl, t_l, D), ckv, kpe,
            time_minor(gk_t, (GQA_KV_HEADS, GQA_HD)), time_minor(gv_t, (GQA_KV_HEADS, GQA_HD)),
            time_minor(dk_t, (DIFF_HEADS, 2, DIFF_D)),
            dv.reshape(n_bc, DEPTH, t_c, DIFF_HEADS, DIFF_DV), ret_state)
```

```python
import functools
import math

import numpy as np
import jax
import jax.numpy as jnp
from jax import lax
from jax.experimental import pallas as pl
from jax.experimental.pallas import tpu as pltpu
from jax.experimental.pallas import tpu_sc as plsc

F32 = jnp.float32
BF16 = jnp.bfloat16

D = 1024
DEPTH = 2
GRID_W = 64
ROPE_BASE = 10000.0
EPS = 1e-6

MLA_HEADS, MLA_NOPE, MLA_ROPE, MLA_V = 8, 64, 32, 64
MLA_Q_LORA, MLA_KV_LORA = 384, 256
GQA_HEADS, GQA_KV_HEADS, GQA_HD = 8, 2, 64
RET_HEADS, RET_DK, RET_DV = 4, 64, 128
DIFF_HEADS, DIFF_D, DIFF_DV = 4, 64, 128
N_BRANCH, BRANCH_W = 4, 512
N_EXPERTS, TOP_K, N_GROUPS, TOPK_GROUPS = 32, 4, 4, 2
EXPERT_FF, SHARED_FF = 256, 256
ROUTE_SCALE = 2.5
GROUP_SIZE = N_EXPERTS // N_GROUPS

LANES = 128
HALF_LANES = 64
VMEM_LIMIT = 56 * 1024 * 1024

O_CQ, O_CKV, O_KPE, O_GQ, O_GK, O_GV = 0, 384, 640, 672, 1184, 1312
O_RQ, O_RK, O_RV, O_RG, O_DQ, O_DK, O_DV, O_GL, O_END = (
    1440, 1696, 1952, 2464, 2976, 3488, 4000, 4512, 8608)
GQA_ORDER = (0, 4, 1, 5, 2, 6, 3, 7)

KVM_W = 8 * 256
GKV_W = 3 * LANES
DKV_W = 512 + 4 * 256
LOG2E = 1.4426950408889634
TM = 256
TM_LAT = 512
TM_MERGE = 512
TQ = 256
TM_MOE_PRE = 1024
TMX = 256
W_SLOTS = 4


def _cparams(sem):
    return pltpu.CompilerParams(dimension_semantics=sem, vmem_limit_bytes=VMEM_LIMIT)


def _const_spec(shape):
    nd = len(shape)
    return pl.BlockSpec(shape, lambda *_: (0,) * nd)


def _layer_spec(l, shape):
    nd = len(shape)
    return pl.BlockSpec((None,) + tuple(shape), lambda *_: (l,) + (0,) * nd)


def _rms(x, g):
    return x * lax.rsqrt(jnp.mean(x * x, axis=-1, keepdims=True) + EPS) * g


def _dot(a, b):
    return jnp.dot(a, b, preferred_element_type=F32)


def _dot_nt(a, b):
    return lax.dot_general(a, b, (((1,), (1,)), ((), ())), preferred_element_type=F32)


def _silu(x):
    return x * jax.nn.sigmoid(x)


def _lane_iota(shape):
    return lax.broadcasted_iota(jnp.int32, shape, len(shape) - 1)


def _seg_meansq(x, bd_ref, width):
    sq = x * x
    hi = sq.astype(BF16)
    lo = (sq - hi.astype(F32)).astype(BF16)
    bd = bd_ref[0:width, 0:width]
    return (_dot(hi, bd) + _dot(lo, bd)) * (1.0 / GQA_HD)


def _rope(x, cos, sin_signed, half):
    width = x.shape[-1]
    first = (_lane_iota(x.shape) % (2 * half)) < half
    partner = jnp.where(first, pltpu.roll(x, width - half, 1), pltpu.roll(x, half, 1))
    return x * cos + partner * sin_signed


def _tile_lanes(t, reps):
    return t if reps == 1 else jnp.concatenate([t] * reps, axis=1)


def _put_layer(o_ref, val, whole_stack):
    if whole_stack:
        o_ref[0] = val
        for k in range(1, o_ref.shape[0]):
            o_ref[k] = jnp.zeros_like(val)
    else:
        o_ref[...] = val


def _store_kvm(kvm_ref, kv, kpe_b):
    ones = jnp.ones(kpe_b.shape, BF16)
    for p in range(4):
        kvm_ref[:, p * 256:p * 256 + LANES] = kv[:, p * LANES:(p + 1) * LANES].astype(BF16)
        kvm_ref[:, p * 256 + LANES:(p + 1) * 256] = kpe_b
        kvm_ref[:, 1024 + p * 256:1024 + p * 256 + LANES] = (
            kv[:, 512 + p * LANES:512 + (p + 1) * LANES].astype(BF16))
        kvm_ref[:, 1024 + p * 256 + LANES:1024 + (p + 1) * 256] = ones


def _mod_kernel(c_ref, w_ref, b_ref, o_ref):
    a = _silu(c_ref[...]).astype(BF16)
    o_ref[...] = _dot(a, w_ref[...].astype(BF16)) + b_ref[...]


def _mod_call(l, cond, w_mod, b_mod):
    n_l, _, n = w_mod.shape
    tn = 1536
    return pl.pallas_call(
        _mod_kernel,
        grid=(n // tn,),
        in_specs=[_const_spec(cond.shape),
                  pl.BlockSpec((None, D, tn), lambda j: (l, 0, j)),
                  pl.BlockSpec((None, 1, tn), lambda j: (l, 0, j))],
        out_specs=pl.BlockSpec((cond.shape[0], tn), lambda j: (0, j)),
        out_shape=jax.ShapeDtypeStruct((cond.shape[0], n), F32),
        compiler_params=_cparams(("arbitrary",)),
        name="mod",
    )(cond, w_mod, b_mod.reshape(n_l, 1, n))


def _inprep_kernel(latent, n_aliased, *refs):
    (x_ref, mod_ref, gpre_ref, win_ref, gmq_ref, wuq_ref, gmkv_ref, wukv_ref,
     ggq_ref, ggk_ref, bd_ref) = refs[:11]
    refs = refs[11:]
    if latent:
        cos64_ref, sin64_ref, cospe_ref, sinpe_ref = refs[:4]
        refs = refs[4:]
    refs = refs[n_aliased:]
    first_layer = n_aliased == 0
    (qm_ref, kvm_ref, gqo_ref, gkv_ref, dqo_ref, dkv_ref, ret_ref, rg_ref, gate_ref) = refs[:9]
    refs = refs[9:]
    if not latent:
        ckv_o, kpe_o, gk_o, gv_o, dk_o, dv_o = refs

    x = x_ref[...]
    mod = mod_ref[...]
    sh1 = mod[:, 0:D]
    sc1 = mod[:, D:2 * D]
    hb = (_rms(x, gpre_ref[...]) * (1.0 + sc1) + sh1).astype(BF16)

    def z(a, b):
        return _dot_nt(hb, win_ref[a:b, :])

    if latent:
        cos64, sin64 = cos64_ref[...], sin64_ref[...]
        cospe, sinpe = cospe_ref[...], sinpe_ref[...]

    cqn = _rms(z(O_CQ, O_CKV), gmq_ref[...]).astype(BF16)
    q = _dot(cqn, wuq_ref[...]) * ((MLA_NOPE + MLA_ROPE) ** -0.5 * LOG2E)
    q_nope, q_pe = q[:, 0:512], q[:, 512:768]
    if latent:
        q_pe = _rope(q_pe, _tile_lanes(cospe, 2), _tile_lanes(sinpe, 2), MLA_ROPE // 4)
    qm_ref[:, 0:512] = q_nope.astype(BF16)
    qm_ref[:, 512:768] = q_pe.astype(BF16)

    ckvn = _rms(z(O_CKV, O_KPE), gmkv_ref[...])
    kv = _dot(ckvn.astype(BF16), wukv_ref[...])
    kpe4 = _dot_nt(hb, jnp.concatenate([win_ref[O_KPE:O_GQ, :]] * 4, axis=0))
    if latent:
        kpe4 = _rope(kpe4, cospe, sinpe, MLA_ROPE // 4)
    else:
        _put_layer(ckv_o, ckvn, first_layer)
        _put_layer(kpe_o, kpe4[:, 0:MLA_ROPE], first_layer)
    _store_kvm(kvm_ref, kv, kpe4.astype(BF16))

    gq = _dot_nt(hb, jnp.concatenate(
        [win_ref[O_GQ + h * GQA_HD:O_GQ + (h + 1) * GQA_HD, :] for h in GQA_ORDER], axis=0))
    gq = gq * lax.rsqrt(_seg_meansq(gq, bd_ref, 512) + EPS) * ggq_ref[...]
    gk = z(O_GK, O_GV)
    gk = gk * lax.rsqrt(_seg_meansq(gk, bd_ref, LANES) + EPS) * ggk_ref[...]
    gv = z(O_GV, O_RQ)
    if latent:
        gq = _rope(gq, _tile_lanes(cos64, 4), _tile_lanes(sin64, 4), GQA_HD // 4)
        gk = _rope(gk, cos64, sin64, GQA_HD // 4)
    else:
        _put_layer(gk_o, jnp.transpose(gk), first_layer)
        _put_layer(gv_o, jnp.transpose(gv), first_layer)
    gqo_ref[...] = (gq * (GQA_HD ** -0.5 * LOG2E)).astype(BF16)
    gkv_ref[:, 0:LANES] = gk.astype(BF16)
    gkv_ref[:, LANES:2 * LANES] = gv.astype(BF16)
    gkv_ref[:, 2 * LANES:3 * LANES] = jnp.ones(gv.shape, BF16)

    dq = z(O_DQ, O_DK)
    dk = z(O_DK, O_DV)
    dv = z(O_DV, O_GL)
    if latent:
        dq = _rope(dq, _tile_lanes(cos64, 4), _tile_lanes(sin64, 4), DIFF_D // 4)
        dk = _rope(dk, _tile_lanes(cos64, 4), _tile_lanes(sin64, 4), DIFF_D // 4)
    else:
        _put_layer(dk_o, jnp.transpose(dk), first_layer)
        _put_layer(dv_o, dv, first_layer)
    dqo_ref[...] = (dq * (DIFF_D ** -0.5 * LOG2E)).astype(BF16)
    dkv_ref[:, 0:512] = dk.astype(BF16)
    for h in range(DIFF_HEADS):
        dkv_ref[:, 512 + h * 256:512 + h * 256 + LANES] = dv[:, h * LANES:(h + 1) * LANES].astype(BF16)
        dkv_ref[:, 512 + h * 256 + LANES:512 + (h + 1) * 256] = jnp.ones((dv.shape[0], LANES), BF16)

    ret_ref[:, 0:256] = z(O_RQ, O_RK).astype(BF16)
    ret_ref[:, 256:512] = (z(O_RK, O_RV) * (RET_DK ** -0.5)).astype(BF16)
    ret_ref[:, 512:1024] = z(O_RV, O_RG).astype(BF16)
    rg_ref[...] = z(O_RG, O_DQ).astype(BF16)

    for n in range(N_BRANCH):
        gate_ref[:, n * D:(n + 1) * D] = jax.nn.sigmoid(
            z(O_GL + n * D, O_GL + (n + 1) * D)).astype(BF16)


def _inprep_call(latent, l, x, mod3, mod_row, lw, tabs, t_len, prev_caches=None):
    n_tok = x.shape[0]
    tm = TM_LAT if latent else TM
    nblk = n_tok // tm
    blk_per_seq = t_len // tm

    def tok(w):
        return pl.BlockSpec((tm, w), lambda i: (i, 0))

    in_specs = [tok(D),
                pl.BlockSpec((None, 1, 6 * D), lambda i: (mod_row(i * tm), 0, 0)),
                _layer_spec(l, (1, D)),
                pl.BlockSpec((None, O_END, D), lambda i: (l, 0, 0), pipeline_mode=pl.Buffered(1)),
                _layer_spec(l, (1, MLA_Q_LORA)), _layer_spec(l, (MLA_Q_LORA, 768)),
                _layer_spec(l, (1, MLA_KV_LORA)), _layer_spec(l, (MLA_KV_LORA, 1024)),
                _layer_spec(l, (1, 512)), _layer_spec(l, (1, LANES)), _const_spec((512, 512))]
    args = [x, mod3, lw["g_pre1"], lw["w_in"], lw["g_mla_q"], lw["w_uq"], lw["g_mla_kv"],
            lw["w_ukv"], lw["g_gqa_q"], lw["g_gqa_k"], lw["bd"]]
    if latent:
        tab_spec = pl.BlockSpec((tm, LANES), lambda i: (i % blk_per_seq, 0))
        in_specs += [tab_spec] * 4
        args += list(tabs)
    widths = [768, KVM_W, 512, GKV_W, 512, DKV_W, 1024, 512, 4 * D]
    out_specs = [tok(w) for w in widths]
    out_shape = [jax.ShapeDtypeStruct((n_tok, w), BF16) for w in widths]
    aliases = {}
    if not latent:
        assert TM == t_len and (prev_caches is None) == (l == 0)
        n_seq = n_tok // t_len
        lead, at = ((DEPTH,), 0) if l == 0 else ((None,), l)

        def row_major(w):
            out_specs.append(pl.BlockSpec((None,) + lead + (t_len, w), lambda i: (i, at, 0, 0)))
            out_shape.append(jax.ShapeDtypeStruct((n_seq, DEPTH, t_len, w), F32))

        row_major(MLA_KV_LORA)
        row_major(MLA_ROPE)
        for w in (LANES, LANES, 512):
            out_specs.append(pl.BlockSpec((None,) + lead + (w, t_len), lambda i: (i, at, 0, 0)))
            out_shape.append(jax.ShapeDtypeStruct((n_seq, DEPTH, w, t_len), F32))
        row_major(512)
        if prev_caches is not None:
            n_in = len(args)
            in_specs += [pl.BlockSpec(memory_space=pl.ANY)] * len(prev_caches)
            args += list(prev_caches)
            aliases = {n_in + k: len(widths) + k for k in range(len(prev_caches))}
    return pl.pallas_call(
        functools.partial(_inprep_kernel, latent, len(aliases)),
        grid=(nblk,),
        in_specs=in_specs, out_specs=out_specs, out_shape=out_shape,
        input_output_aliases=aliases,
        compiler_params=_cparams(("arbitrary",)),
        name="inprep_lat" if latent else "inprep_ctx",
    )(*args)


def _pastkv_kernel(ckv_ref, kpe_ref, wukv_ref, o_ref):
    kv = _dot(ckv_ref[...].astype(BF16), wukv_ref[...])
    _store_kvm(o_ref, kv, kpe_ref[...].astype(BF16))


def _pastkv_call(l, ckv, kpe4, w_ukv):
    n = ckv.shape[0]
    return pl.pallas_call(
        _pastkv_kernel,
        grid=(n // TM,),
        in_specs=[pl.BlockSpec((TM, MLA_KV_LORA), lambda i: (i, 0)),
                  pl.BlockSpec((TM, LANES), lambda i: (i, 0)),
                  _layer_spec(l, (MLA_KV_LORA, 1024))],
        out_specs=pl.BlockSpec((TM, KVM_W), lambda i: (i, 0)),
        out_shape=jax.ShapeDtypeStruct((n, KVM_W), BF16),
        compiler_params=_cparams(("arbitrary",)),
        name="pastkv",
    )(ckv, kpe4, w_ukv)


def _softmax_pv(s, v_ones):
    m = jnp.max(s, axis=-1, keepdims=True)
    p = jnp.exp2(s - m).astype(BF16)
    o = _dot(p, v_ones)
    return o[:, 0:LANES] / o[:, LANES:2 * LANES]


def _attn_kernel(lam_init, n_past, qm_ref, kvm_ref, gq_ref, gkv_ref, dq_ref, dkv_ref, *refs):
    if n_past:
        past_refs, refs = refs[:3], refs[3:]
        lam_ref, gdiff_ref, o_ref = refs[:3]
        joined = refs[3:]

        @pl.when(pl.program_id(1) == 0)
        def _():
            for dst, past, new in zip(joined, past_refs, (kvm_ref, gkv_ref, dkv_ref)):
                dst[0:n_past, :] = past[...]
                dst[n_past:, :] = new[...]

        kvm_ref, gkv_ref, dkv_ref = joined
    else:
        lam_ref, gdiff_ref, o_ref = refs
    tq = qm_ref.shape[0]
    lane = _lane_iota((tq, LANES))
    low = lane < HALF_LANES
    zero = jnp.zeros((tq, LANES), BF16)

    for p in range(MLA_HEADS // 2):
        qn = qm_ref[:, p * LANES:(p + 1) * LANES]
        g = p // 2
        qpe = qm_ref[:, 512 + g * LANES:512 + (g + 1) * LANES]
        kk = kvm_ref[:, p * 256:(p + 1) * 256]
        vv = kvm_ref[:, 1024 + p * 256:1024 + (p + 1) * 256]
        outs = []
        for half in range(2):
            h = 2 * p + half
            slot = h % 4
            in_slot = (lane >= slot * MLA_ROPE) & (lane < (slot + 1) * MLA_ROPE)
            lhs = jnp.concatenate(
                [jnp.where(low if half == 0 else ~low, qn, zero),
                 jnp.where(in_slot, qpe, zero)], axis=1)
            outs.append(_softmax_pv(_dot_nt(lhs, kk), vv))
        o_ref[:, p * LANES:(p + 1) * LANES] = jnp.where(low, outs[0], outs[1]).astype(BF16)

    kk = gkv_ref[:, 0:LANES]
    vv = gkv_ref[:, LANES:3 * LANES]
    for g in range(GQA_HEADS // 2):
        qg = gq_ref[:, g * LANES:(g + 1) * LANES]
        o_lo = _softmax_pv(_dot_nt(jnp.where(low, qg, zero), kk), vv)
        o_hi = _softmax_pv(_dot_nt(jnp.where(low, zero, qg), kk), vv)
        o_ref[:, 512 + g * LANES:512 + (g + 1) * LANES] = jnp.where(low, o_lo, o_hi).astype(BF16)

    lp = lam_ref[...]
    lam = (jnp.exp(jnp.sum(lp[0:1] * lp[1:2], axis=-1, keepdims=True))
           - jnp.exp(jnp.sum(lp[2:3] * lp[3:4], axis=-1, keepdims=True)) + lam_init)
    for h in range(DIFF_HEADS):
        qh = dq_ref[:, h * LANES:(h + 1) * LANES]
        kk = dkv_ref[:, h * LANES:(h + 1) * LANES]
        vv = dkv_ref[:, 512 + h * 256:512 + (h + 1) * 256]
        a1 =_softmax_pv(_dot_nt(jnp.where(low, qh, zero), kk), vv)
        a2 = _softmax_pv(_dot_nt(jnp.where(low, zero, qh), kk), vv)
        od = _rms(a1 - lam * a2, gdiff_ref[...]) * (1.0 - lam_init)
        o_ref[:, 1024 + h * LANES:1024 + (h + 1) * LANES] = od.astype(BF16)


def _attn_call(l, lam_init, qm, kvm, gq, gkv, dq, dkv, lam_p, g_diff, n_b, t_len, past=None):
    nq = t_len // TQ
    n_past = 0 if past is None else past[0].shape[0] // n_b

    def qspec(w):
        return pl.BlockSpec((TQ, w), lambda b, i: (b * nq + i, 0))

    def kspec(w, rows=t_len):
        return pl.BlockSpec((rows, w), lambda b, i: (b, 0))

    in_specs = [qspec(768), kspec(KVM_W), qspec(512), kspec(GKV_W), qspec(512), kspec(DKV_W)]
    args = [qm, kvm, gq, gkv, dq, dkv]
    scratch = []
    if n_past:
        in_specs += [kspec(KVM_W, n_past), kspec(GKV_W, n_past), kspec(DKV_W, n_past)]
        args += list(past)
        scratch = [pltpu.VMEM((n_past + t_len, w), BF16) for w in (KVM_W, GKV_W, DKV_W)]
    in_specs += [_layer_spec(l, (4, DIFF_D)), _layer_spec(l, (1, DIFF_DV))]
    args += [lam_p, g_diff]
    return dict(kernel=functools.partial(_attn_kernel, lam_init, n_past), in_specs=in_specs, args=args,
                out_specs=[qspec(3 * BRANCH_W)],
                out_shape=[jax.ShapeDtypeStruct((n_b * t_len, 3 * BRANCH_W), BF16)],
                scratch=scratch, aliases={})


def _log_sigmoid(x):
    return jnp.minimum(x, 0.0) - jnp.log(1.0 + jnp.exp(-jnp.abs(x)))


def _log_gamma(dec_ref, l, d, h):
    return _log_sigmoid(jnp.full((1, 1), dec_ref[l, d, h], F32))


def _ret_kernel(latent, l, t_len, dec_ref, q_ref, k_ref, v_ref, rg_ref, gret_ref, *refs):
    if latent:
        s0_ref, o_ref = refs
    else:
        o_ref, st_ref = refs[-2:]
    tq = q_ref.shape[0]
    t0 = pl.program_id(1) * tq
    lane = _lane_iota((tq, LANES))
    low = lane < HALF_LANES
    zero = jnp.zeros((tq, LANES), BF16)
    t_idx = (t0 + lax.broadcasted_iota(jnp.int32, (tq, t_len), 0)).astype(F32)
    s_idx = lax.broadcasted_iota(jnp.int32, (tq, t_len), 1).astype(F32)
    dist = t_idx - s_idx
    past = dist >= 0
    diag = jnp.where(dist == 0, 1.0, 0.0)
    t_col = (t0 + lax.broadcasted_iota(jnp.int32, (tq, 1), 0)).astype(F32)

    def lg(d, h):
        return _log_gamma(dec_ref, l, d, h)

    for h in range(RET_HEADS):
        p, half = h // 2, h % 2
        qp = q_ref[:, p * LANES:(p + 1) * LANES]
        qm = jnp.where(low if half == 0 else ~low, qp, zero)
        kp = k_ref[:, p * LANES:(p + 1) * LANES]
        vh = v_ref[:, h * LANES:(h + 1) * LANES]
        lgf, lgb = lg(0, h), lg(1, h)
        dmask = jnp.exp(jnp.where(past, lgf, -lgb) * dist) + diag
        o = _dot((_dot_nt(qm, kp) * dmask).astype(BF16), vh)
        if latent:
            sf =s0_ref[0, p].astype(BF16)
            sb = s0_ref[1, p].astype(BF16)
            o = o + _dot(qm, sf) * jnp.exp(lgf * (t_col + 1.0))
            o = o + _dot(qm, sb) * jnp.exp(lgb * (float(t_len) - t_col))
        mu = jnp.mean(o, axis=-1, keepdims=True)
        oc = o - mu
        y = oc * lax.rsqrt(jnp.mean(oc * oc, axis=-1, keepdims=True) + EPS)
        y = y * gret_ref[:, h * LANES:(h + 1) * LANES]
        rg = rg_ref[:, h * LANES:(h + 1) * LANES].astype(F32)
        o_ref[:, h * LANES:(h + 1) * LANES] = (y * _silu(rg)).astype(BF16)

    if not latent:
        s_col = lax.broadcasted_iota(jnp.int32, (t_len, 1), 0).astype(F32)
        lane_t = _lane_iota((1, LANES)) < HALF_LANES
        for p in range(RET_HEADS // 2):
            kp = k_ref[:, p * LANES:(p + 1) * LANES].astype(F32)
            for d in range(2):
                lg_lane = jnp.where(lane_t, lg(d, 2 * p), lg(d, 2 * p + 1))
                expo = (float(t_len) - 1.0 - s_col) if d == 0 else s_col
                kdec_t = jnp.transpose(kp * jnp.exp(lg_lane * expo)).astype(BF16)
                for half in range(2):
                    h = 2 * p + half
                    st = _dot(kdec_t, v_ref[:, h * LANES:(h + 1) * LANES])
                    st = st[half * RET_DK:(half + 1) * RET_DK, :]
                    if l == 0:
                        st_ref[0, d, h] = st
                        for k in range(1, st_ref.shape[0]):
                            st_ref[k, d, h] = jnp.zeros_like(st)
                    else:
                        st_ref[d, h] = st


def _ret_call(latent, l, dec, ret, rg, g_ret, s0, n_b, t_len, prev_state=None):
    nq = t_len // TQ
    aliases = {}
    assert latent or nq == 1
    in_specs = [pl.BlockSpec(memory_space=pltpu.SMEM),
                pl.BlockSpec((TQ, 256), lambda b, i: (b * nq + i, 0)),
                pl.BlockSpec((t_len, 256), lambda b, i: (b, 1)),
                pl.BlockSpec((t_len, 512), lambda b, i: (b, 1)),
                pl.BlockSpec((TQ, 512), lambda b, i: (b * nq + i, 0)),
                _layer_spec(l, (1, 512))]
    args = [dec, ret, ret, ret, rg, g_ret]
    out_specs = [pl.BlockSpec((TQ, 512), lambda b, i: (b * nq + i, 0))]
    out_shape = [jax.ShapeDtypeStruct((n_b * t_len, 512), BF16)]
    if latent:
        in_specs.append(pl.BlockSpec((None, 2, 2, LANES, LANES), lambda b, i: (b, 0, 0, 0, 0)))
        args.append(s0)
    else:
        assert (prev_state is None) == (l == 0)
        lead, at = ((DEPTH,), 0) if l == 0 else ((None,), l)
        out_specs.append(pl.BlockSpec((None,) + lead + (2, RET_HEADS, RET_DK, RET_DV),
                                      lambda b, i: (b, at, 0, 0, 0, 0)))
        out_shape.append(jax.ShapeDtypeStruct((n_b, DEPTH, 2, RET_HEADS, RET_DK, RET_DV), F32))
        if prev_state is not None:
            aliases = {len(args): 1}
            in_specs.append(pl.BlockSpec(memory_space=pl.ANY))
            args.append(prev_state)
    return dict(kernel=functools.partial(_ret_kernel, latent, l, t_len), in_specs=in_specs, args=args,
                out_specs=out_specs, out_shape=out_shape, scratch=[], aliases=aliases)


def _mixers_call(attn, ret, n_b, nq, name):
    n_ai, n_ri = len(attn["args"]), len(ret["args"])
    n_ao, n_ro = len(attn["out_specs"]), len(ret["out_specs"])

    def kernel(*refs):
        a_in, r_in = refs[:n_ai], refs[n_ai:n_ai + n_ri]
        outs = refs[n_ai + n_ri:n_ai + n_ri + n_ao + n_ro]
        scratch = refs[n_ai + n_ri + n_ao + n_ro:]
        attn["kernel"](*a_in, *outs[:n_ao], *scratch)
        ret["kernel"](*r_in, *outs[n_ao:])

    aliases = {n_ai + k: n_ao + v for k, v in ret["aliases"].items()}
    return pl.pallas_call(
        kernel,
        grid=(n_b, nq),
        in_specs=attn["in_specs"] + ret["in_specs"],
        out_specs=attn["out_specs"] + ret["out_specs"],
        out_shape=attn["out_shape"] + ret["out_shape"],
        scratch_shapes=attn["scratch"],
        input_output_aliases=aliases,
        compiler_params=_cparams(("arbitrary", "arbitrary")),
        name=name,
    )(*attn["args"], *ret["args"])


def _merge_kernel(x_ref, mod_ref, br_ref, or_ref, gate_ref, wbr_ref, wout_ref, gpost_ref, o_ref):
    merged = None
    for n in range(N_BRANCH):
        if n < 2:
            b = br_ref[:, n * BRANCH_W:(n + 1) * BRANCH_W]
        elif n == 2:
            b = or_ref[...]
        else:
            b = br_ref[:, 2 * BRANCH_W:3 * BRANCH_W]
        t = gate_ref[:, n * D:(n + 1) * D].astype(F32) * _dot(b, wbr_ref[n])
        merged = t if merged is None else merged + t
    out = _dot(merged.astype(BF16), wout_ref[...])
    g1 = mod_ref[...][:, 2 * D:3 * D]
    o_ref[...] = x_ref[...] + g1 * _rms(out, gpost_ref[...])


def _merge_call(l, x, mod3, mod_row, br, o_r, gates, lw):
    n_tok = x.shape[0]

    def tok(w):
        return pl.BlockSpec((TM_MERGE, w), lambda i: (i, 0))

    return pl.pallas_call(
        _merge_kernel,
        grid=(n_tok // TM_MERGE,),
        in_specs=[tok(D), pl.BlockSpec((None, 1, 6 * D), lambda i: (mod_row(i * TM_MERGE), 0, 0)),
                  tok(3 * BRANCH_W), tok(BRANCH_W), tok(4 * D),
                  _layer_spec(l, (N_BRANCH, BRANCH_W, D)), _layer_spec(l, (D, D)),
                  _layer_spec(l, (1, D))],
        out_specs=tok(D),
        out_shape=jax.ShapeDtypeStruct((n_tok, D), F32),
        compiler_params=_cparams(("arbitrary",)),
        name="merge",
    )(x, mod3, br, o_r, gates, lw["w_br"], lw["w_out"], lw["g_post1"])


def _route(logits_t, bias):
    n = logits_t.shape[1]
    scores = jax.nn.sigmoid(logits_t)
    sel = scores + bias
    neg = -jnp.inf
    sub = lax.broadcasted_iota(jnp.int32, (GROUP_SIZE, n), 0)
    grp = []
    for g in range(N_GROUPS):
        blk = sel[g * GROUP_SIZE:(g + 1) * GROUP_SIZE]
        m1 = jnp.max(blk, axis=0, keepdims=True)
        i1 = jnp.min(jnp.where(blk == m1, sub, GROUP_SIZE), axis=0, keepdims=True)
        m2 = jnp.max(jnp.where(sub == i1, neg, blk), axis=0, keepdims=True)
        grp.append(m1 + m2)
    parts = []
    for g in range(N_GROUPS):
        beaten = jnp.zeros((1, n), jnp.int32)
        for o in range(N_GROUPS):
            if o == g:
                continue
            wins = (grp[o] > grp[g]) | (grp[o] == grp[g]) if o < g else (grp[o] > grp[g])
            beaten = beaten + wins.astype(jnp.int32)
        keep = beaten < TOPK_GROUPS
        parts.append(jnp.where(keep, sel[g * GROUP_SIZE:(g + 1) * GROUP_SIZE], neg))
    cur = jnp.concatenate(parts, axis=0)
    eidx = lax.broadcasted_iota(jnp.int32, (N_EXPERTS, n), 0)
    hits, ids, ws = [], [], []
    for _ in range(TOP_K):
        m = jnp.max(cur, axis=0, keepdims=True)
        i = jnp.min(jnp.where(cur == m, eidx, N_EXPERTS), axis=0, keepdims=True)
        hit = eidx == i
        hits.append(hit)
        ids.append(i)
        ws.append(jnp.sum(jnp.where(hit, scores, 0.0), axis=0, keepdims=True))
        cur = jnp.where(hit, neg, cur)
    wsum = ws[0] + ws[1] + ws[2] + ws[3]
    return hits, ids, [w / wsum * ROUTE_SCALE for w in ws]


U32 = jnp.uint32
HIGH16 = np.uint32(0xFFFF0000)


def _bf16_bits(v):
    return lax.bitcast_convert_type(v.astype(BF16).astype(F32), U32)


def _pack_rows(v):
    return (_bf16_bits(v[:, 0:D // 2]) >> 16) | _bf16_bits(v[:, D // 2:D])


def _unpack_rows(p):
    lo = lax.bitcast_convert_type(p << 16, F32)
    hi = lax.bitcast_convert_type(p & HIGH16, F32)
    return jnp.concatenate([lo, hi], axis=1)


def _moe_pre_kernel(x_ref, mod_ref, gpre_ref, wr_ref, br_ref, tri_ref,
                    hp_ref, eidx_ref, rank_ref, comb_ref, cnt_ref, run_ref):
    tm = x_ref.shape[0]

    @pl.when(pl.program_id(0) == 0)
    def _():
        run_ref[...] = jnp.zeros_like(run_ref)

    mod = mod_ref[...]
    sh2, sc2 = mod[:, 3 * D:4 * D], mod[:, 4 * D:5 * D]
    h = _rms(x_ref[...], gpre_ref[...]) * (1.0 + sc2) + sh2
    hp_ref[...] = _pack_rows(h)
    hb = h.astype(BF16)
    h_lo = (h - hb.astype(F32)).astype(BF16)
    wr = wr_ref[...]
    wr_hi = wr.astype(BF16)
    wr_lo = (wr - wr_hi.astype(F32)).astype(BF16)
    logits_t = _dot_nt(wr_hi, hb) + _dot_nt(wr_hi, h_lo) + _dot_nt(wr_lo, hb)
    hits, ids, ws = _route(logits_t, br_ref[...])

    picked = jnp.zeros((N_EXPERTS, tm), F32)
    for hit in hits:
        picked = jnp.where(hit, 1.0, picked)
    before = _dot(picked.astype(BF16), tri_ref[...]) + run_ref[:, 0:1]
    sub8 = lax.broadcasted_iota(jnp.int32, (8, tm), 0)
    comb8 = jnp.zeros((8, tm), F32)
    for k in range(TOP_K):
        rank = jnp.sum(jnp.where(hits[k], before, 0.0), axis=0, keepdims=True)
        eidx_ref[k:k + 1, :] = ids[k]
        rank_ref[k:k + 1, :] = rank.astype(jnp.int32)
        comb8 = jnp.where(sub8 == k, ws[k], comb8)
    comb_ref[...] = jnp.transpose(
        jnp.concatenate([comb8, jnp.zeros((LANES - 8, tm), F32)], axis=0))
    run_ref[...] = run_ref[...] + jnp.sum(picked, axis=1, keepdims=True)
    cnt_ref[...] = run_ref[...]


def _moe_pre_call(l, x, mod3, mod_row, lw):
    n_tok = x.shape[0]
    tm = TM_MOE_PRE
    tri = np.arange(tm)
    tri = jnp.asarray(tri[:, None] < tri[None, :], BF16)
    row4 = pl.BlockSpec((TOP_K, tm), lambda i: (0, i))
    return pl.pallas_call(
        _moe_pre_kernel,
        grid=(n_tok // tm,),
        in_specs=[pl.BlockSpec((tm, D), lambda i: (i, 0)),
                  pl.BlockSpec((None, 1, 6 * D), lambda i: (mod_row(i * tm), 0, 0)),
                  _layer_spec(l, (1, D)), _layer_spec(l, (N_EXPERTS, D)),
                  _layer_spec(l, (N_EXPERTS, 1)), _const_spec((tm, tm))],
        out_specs=[pl.BlockSpec((tm, D // 2), lambda i: (i, 0)), row4, row4,
                   pl.BlockSpec((tm, LANES), lambda i: (i, 0)),
                   _const_spec((N_EXPERTS, LANES))],
        out_shape=[jax.ShapeDtypeStruct((n_tok, D // 2), U32),
                   jax.ShapeDtypeStruct((TOP_K, n_tok), jnp.int32),
                   jax.ShapeDtypeStruct((TOP_K, n_tok), jnp.int32),
                   jax.ShapeDtypeStruct((n_tok, LANES), F32),
                   jax.ShapeDtypeStruct((N_EXPERTS, LANES), F32)],
        scratch_shapes=[pltpu.VMEM((N_EXPERTS, LANES), F32)],
        compiler_params=_cparams(("arbitrary",)),
        name="moe_pre",
    )(x, mod3, lw["g_pre2"], lw["w_router_t"], lw["b_router"], tri)


def _moe_plan_kernel(eidx_ref, rank_ref, cnt_ref, dest_ref, te_ref, tv_ref, tn_ref):
    tm = eidx_ref.shape[1]
    cnt = cnt_ref[...]
    padded = jnp.ceil(cnt * (1.0 / TMX)) * TMX
    row = lax.broadcasted_iota(jnp.int32, cnt.shape, 0)
    incl = padded
    shift = 1
    while shift < N_EXPERTS:
        incl = incl + jnp.where(row >= shift, pltpu.roll(incl, shift, 0), 0.0)
        shift *= 2
    start = (incl - padded)[:, 0:1]
    end = incl[:, 0:1]
    erow = lax.broadcasted_iota(jnp.int32, (N_EXPERTS, tm), 0)
    for k in range(TOP_K):
        mine = erow == eidx_ref[k:k + 1, :]
        base = jnp.sum(jnp.where(mine, start, 0.0), axis=0, keepdims=True)
        dest_ref[k:k + 1, :] = rank_ref[k:k + 1, :] + base.astype(jnp.int32)

    @pl.when(pl.program_id(0) == 0)
    def _():
        tile0 = (_lane_iota((1, LANES)) * TMX).astype(F32)
        owner = jnp.sum(jnp.where(end <= tile0, 1.0, 0.0), axis=0, keepdims=True)
        owner = jnp.minimum(owner, N_EXPERTS - 1.0)
        erow_t = lax.broadcasted_iota(jnp.int32, (N_EXPERTS, LANES), 0).astype(F32)
        left = jnp.sum(jnp.where(erow_t == owner, cnt[:, 0:1] - (tile0 - start), 0.0),
                       axis=0, keepdims=True)
        te_ref[...] = owner.astype(jnp.int32)
        tv_ref[...] = jnp.clip(left, 0.0, float(TMX)).astype(jnp.int32)
        tn_ref[...] = jnp.full(tn_ref.shape, N_EXPERTS, jnp.int32)
        nxt = owner
        for k in range(W_SLOTS - 1):
            later = (erow_t > nxt) & (cnt[:, 0:1] > 0.0)
            nxt = jnp.min(jnp.where(later, erow_t, float(N_EXPERTS)), axis=0, keepdims=True)
            tn_ref[k:k + 1, :] = nxt.astype(jnp.int32)
        n_used = jnp.sum(jnp.where(left > 0.0, 1.0, 0.0), axis=1, keepdims=True)
        tn_ref[7:8, :] = jnp.minimum(tile0 * (1.0 / TMX), n_used - 1.0).astype(jnp.int32)


def _moe_plan_call(eidx, rank, cnt):
    n_tok = eidx.shape[1]
    tm = TM_MOE_PRE
    row4 = pl.BlockSpec((TOP_K, tm), lambda i: (0, i))
    tiles = jax.ShapeDtypeStruct((1, LANES), jnp.int32)
    return pl.pallas_call(
        _moe_plan_kernel,
        grid=(n_tok // tm,),
        in_specs=[row4, row4, _const_spec((N_EXPERTS, LANES))],
        out_specs=[row4, _const_spec((1, LANES)), _const_spec((1, LANES)), _const_spec((8, LANES))],
        out_shape=[jax.ShapeDtypeStruct((TOP_K, n_tok), jnp.int32), tiles, tiles,
                   jax.ShapeDtypeStruct((8, LANES), jnp.int32)],
        compiler_params=_cparams(("arbitrary",)),
        name="moe_plan",
    )(eidx, rank, cnt)


def _experts_kernel(l, te_ref, tv_ref, tn_ref, xs_ref, wgu_hbm, wdn_hbm, ys_ref,
                    wgu_f, wdn_f, wgu_b, wdn_b, sem, group_ref):
    j = pl.program_id(0)
    valid = tv_ref[j]
    expert = te_ref[j]

    def fetch(e, slot):
        return (pltpu.make_async_copy(wgu_hbm.at[l, e], wgu_f.at[slot], sem.at[slot, 0]),
                pltpu.make_async_copy(wdn_hbm.at[l, e], wdn_f.at[slot], sem.at[slot, 1]))

    def start_if_any(e, slot):
        @pl.when(e < N_EXPERTS)
        def _():
            for cp in fetch(e, slot):
                cp.start()

    @pl.when(j == 0)
    def _():
        group_ref[0] = 0
        start_if_any(expert, 0)
        for k in range(W_SLOTS - 2):
            start_if_any(tn_ref[k, 0], k + 1)

    first_tile = (j == 0) | (expert != te_ref[jnp.maximum(j - 1, 0)])

    @pl.when(first_tile & (valid > 0))
    def _():
        group = group_ref[0]
        slot = lax.rem(group, W_SLOTS)
        for cp in fetch(expert, slot):
            cp.wait()
        wgu_b[...] = wgu_f[slot].astype(BF16)
        wdn_b[...] = wdn_f[slot].astype(BF16)
        start_if_any(tn_ref[W_SLOTS - 2, j], lax.rem(group + W_SLOTS - 1, W_SLOTS))
        group_ref[0] = group + 1

    @pl.when(valid > 0)
    def _():
        rows = lax.broadcasted_iota(jnp.int32, (TMX, D), 0)
        x = jnp.where(rows < valid, _unpack_rows(xs_ref[...]), 0.0).astype(BF16)
        gu = _dot(x, wgu_b[...])
        a = _silu(gu[:, 0:EXPERT_FF]) * gu[:, EXPERT_FF:2 * EXPERT_FF]
        ys_ref[...] = _pack_rows(_dot(a.astype(BF16), wdn_b[...]))


def _experts_call(l, xs, te, tv, tn, w_gu, w_dn):
    n_tiles = xs.shape[0] // TMX
    grid_spec = pltpu.PrefetchScalarGridSpec(
        num_scalar_prefetch=3,
        grid=(n_tiles,),
        in_specs=[pl.BlockSpec((TMX, D // 2), lambda j, te, tv, tn: (tn[7, j], 0)),
                  pl.BlockSpec(memory_space=pl.ANY), pl.BlockSpec(memory_space=pl.ANY)],
        out_specs=pl.BlockSpec((TMX, D // 2), lambda j, te, tv, tn: (tn[7, j], 0)),
        scratch_shapes=[pltpu.VMEM((W_SLOTS, D, 2 * EXPERT_FF), F32),
                        pltpu.VMEM((W_SLOTS, EXPERT_FF, D), F32),
                        pltpu.VMEM((D, 2 * EXPERT_FF), BF16), pltpu.VMEM((EXPERT_FF, D), BF16),
                        pltpu.SemaphoreType.DMA((W_SLOTS, 2)), pltpu.SMEM((1,), jnp.int32)])
    return pl.pallas_call(
        functools.partial(_experts_kernel, l),
        grid_spec=grid_spec,
        out_shape=jax.ShapeDtypeStruct(xs.shape, U32),
        compiler_params=_cparams(("arbitrary",)),
        name="moe_experts",
    )(te, tv, tn, xs, w_gu, w_dn)


def _moe_post_kernel(x_ref, mod_ref, hp_ref, yg_ref, comb_ref, wsgu_ref, wsdn_ref, gpost_ref,
                     o_ref):
    hb = _unpack_rows(hp_ref[...]).astype(BF16)
    sgu = _dot(hb, wsgu_ref[...])
    sa = _silu(sgu[:, 0:SHARED_FF]) * sgu[:, SHARED_FF:2 * SHARED_FF]
    acc = _dot(sa.astype(BF16), wsdn_ref[...])
    comb = comb_ref[...]
    for k in range(TOP_K):
        acc = acc + comb[:, k:k + 1] * _unpack_rows(yg_ref[k])
    g2 = mod_ref[...][:, 5 * D:6 * D]
    o_ref[...] = x_ref[...] + g2 * _rms(acc, gpost_ref[...])


def _moe_post_call(l, x, mod3, mod_row, hp, yg, comb, lw):
    n_tok = x.shape[0]
    tm = TM_MOE_PRE
    return pl.pallas_call(
        _moe_post_kernel,
        grid=(n_tok // tm,),
        in_specs=[pl.BlockSpec((tm, D), lambda i: (i, 0)),
                  pl.BlockSpec((None, 1, 6 * D), lambda i: (mod_row(i * tm), 0, 0)),
                  pl.BlockSpec((tm, D // 2), lambda i: (i, 0)),
                  pl.BlockSpec((TOP_K, tm, D // 2), lambda i: (0, i, 0)),
                  pl.BlockSpec((tm, LANES), lambda i: (i, 0)),
                  _layer_spec(l, (D, 2 * SHARED_FF)), _layer_spec(l, (SHARED_FF, D)),
                  _layer_spec(l, (1, D))],
        out_specs=pl.BlockSpec((tm, D), lambda i: (i, 0)),
        out_shape=jax.ShapeDtypeStruct((n_tok, D), F32),
        compiler_params=_cparams(("arbitrary",)),
        name="moe_post",
    )(x, mod3, hp, yg, comb, lw["w_sh_gu"], lw["w_sh_down"], lw["g_post2"])


def _moe_call(l, x, mod3, mod_row, lw):
    n_tok = x.shape[0]
    n_slots = -(-(TOP_K * n_tok + N_EXPERTS * (TMX - 1)) // TMX) * TMX
    assert n_slots // TMX <= LANES
    hp, eidx, rank, comb, cnt = _moe_pre_call(l, x, mod3, mod_row, lw)
    dest, te, tv, tn = _moe_plan_call(eidx, rank, cnt)
    dest = dest.reshape(TOP_K * n_tok)
    xs = _sc_scatter_rows(hp, dest, n_slots)
    ys = _experts_call(l, xs, te[0], tv[0], tn, lw["w_exp_gu"], lw["w_exp_down"])
    yg = _sc_gather_rows(ys, dest).reshape(TOP_K, n_tok, D // 2)
    return _moe_post_call(l, x, mod3, mod_row, hp, yg, comb, lw)


SC_CORES, SC_SUBCORES = 2, 16
SC_WORKERS = SC_CORES * SC_SUBCORES


def _sc_gather_rows(table, idx, chunk=64):
    n_out, width = idx.shape[0], table.shape[1]
    per_worker = n_out // SC_WORKERS
    n_chunks = per_worker // chunk
    assert per_worker * SC_WORKERS == n_out and n_chunks * chunk == per_worker
    mesh = plsc.VectorSubcoreMesh(core_axis_name="c", subcore_axis_name="s",
                                  num_cores=SC_CORES, num_subcores=SC_SUBCORES)

    @functools.partial(
        pl.kernel, mesh=mesh,
        out_type=jax.ShapeDtypeStruct((n_out, width), table.dtype),
        scratch_types=[pltpu.VMEM((chunk,), jnp.int32), pltpu.VMEM((chunk, width), table.dtype),
                       pltpu.SemaphoreType.DMA],
        name="sc_gather")
    def gather(table_hbm, idx_hbm, out_hbm, idx_v, rows_v, sem):
        base = (lax.axis_index("s") * SC_CORES + lax.axis_index("c")) * per_worker

        @pl.loop(0, n_chunks)
        def _(j):
            off = base + j * chunk
            pltpu.sync_copy(idx_hbm.at[pl.ds(off, chunk)], idx_v)
            pltpu.async_copy(table_hbm.at[idx_v], rows_v, sem).wait()
            pltpu.sync_copy(rows_v, out_hbm.at[pl.ds(off, chunk)])

    return gather(table, idx)


def _sc_scatter_rows(rows, dest, n_slots, chunk=64):
    n_tok, width = rows.shape
    per_worker = n_tok // SC_WORKERS
    n_chunks = per_worker // chunk
    assert per_worker * SC_WORKERS == n_tok and n_chunks * chunk == per_worker
    mesh = plsc.VectorSubcoreMesh(core_axis_name="c", subcore_axis_name="s",
                                  num_cores=SC_CORES, num_subcores=SC_SUBCORES)

    @functools.partial(
        pl.kernel, mesh=mesh,
        out_type=jax.ShapeDtypeStruct((n_slots, width), rows.dtype),
        scratch_types=[pltpu.VMEM((chunk,), jnp.int32), pltpu.VMEM((chunk, width), rows.dtype)],
        name="sc_scatter")
    def scatter(rows_hbm, dest_hbm, out_hbm, idx_v, rows_v):
        base = (lax.axis_index("s") * SC_CORES + lax.axis_index("c")) * per_worker

        @pl.loop(0, n_chunks)
        def _(j):
            off = base + j * chunk
            pltpu.sync_copy(rows_hbm.at[pl.ds(off, chunk)], rows_v)
            for k in range(TOP_K):
                pltpu.sync_copy(dest_hbm.at[pl.ds(k * n_tok + off, chunk)], idx_v)
                pltpu.sync_copy(rows_v, out_hbm.at[idx_v])

    return scatter(rows, dest)


def _rope_tables(t_len):
    pos = np.arange(t_len)
    row, col = pos // GRID_W, pos % GRID_W

    def tab(r):
        half = r // 2
        freq = ROPE_BASE ** (-np.arange(half, dtype=np.float64) / half)
        sign = np.concatenate([-np.ones(half), np.ones(half)])
        cs, sn = [], []
        for p in (row, col):
            ang = p[:, None].astype(np.float64) * freq[None, :]
            cs.append(np.concatenate([np.cos(ang), np.cos(ang)], axis=1))
            sn.append(np.concatenate([np.sin(ang), np.sin(ang)], axis=1) * sign[None, :])
        return np.concatenate(cs, axis=1), np.concatenate(sn, axis=1)

    c64, s64 = tab(GQA_HD // 2)
    cpe, spe = tab(MLA_ROPE // 2)
    out = (np.tile(c64, (1, 2)), np.tile(s64, (1, 2)), np.tile(cpe, (1, 4)), np.tile(spe, (1, 4)))
    return tuple(jnp.asarray(a, F32) for a in out)


def _prep_weights(p):
    n_l = p["w_in"].shape[0]

    def row(name):
        return p[name].reshape(n_l, 1, -1)

    w_uq = p["w_mla_uq"].reshape(n_l, MLA_Q_LORA, MLA_HEADS, MLA_NOPE + MLA_ROPE)
    w_uq = jnp.concatenate([w_uq[..., :MLA_NOPE].reshape(n_l, MLA_Q_LORA, -1),
                            w_uq[..., MLA_NOPE:].reshape(n_l, MLA_Q_LORA, -1)], axis=-1)
    w_ukv = p["w_mla_ukv"].reshape(n_l, MLA_KV_LORA, MLA_HEADS, MLA_NOPE + MLA_V)
    w_ukv = jnp.concatenate([w_ukv[..., :MLA_NOPE].reshape(n_l, MLA_KV_LORA, -1),
                             w_ukv[..., MLA_NOPE:].reshape(n_l, MLA_KV_LORA, -1)], axis=-1)
    w_br = p["w_br"]
    w_br_gqa = w_br[:, 1].reshape(n_l, GQA_HEADS, GQA_HD, D)[:, jnp.array(GQA_ORDER)]
    w_br = jnp.concatenate([w_br[:, 0:1], w_br_gqa.reshape(n_l, 1, BRANCH_W, D), w_br[:, 2:4]], axis=1)
    blk = np.arange(512) // GQA_HD
    return {
        "g_pre1": row("g_pre1"), "g_post1": row("g_post1"),
        "g_pre2": row("g_pre2"), "g_post2": row("g_post2"),
        "w_in": jnp.swapaxes(p["w_in"], 1, 2).astype(BF16),
        "g_mla_q": row("g_mla_q"), "w_uq": w_uq.astype(BF16),
        "g_mla_kv": row("g_mla_kv"), "w_ukv": w_ukv.astype(BF16),
        "g_gqa_q": jnp.tile(p["g_gqa_q"], (1, GQA_HEADS)).reshape(n_l, 1, -1),
        "g_gqa_k": jnp.tile(p["g_gqa_k"], (1, GQA_KV_HEADS)).reshape(n_l, 1, -1),
        "bd": jnp.asarray(blk[:, None] == blk[None, :], BF16),
        "ret_decay": p["ret_decay"],
        "g_ret": row("g_ret"),
        "diff_lambda": p["diff_lambda"], "g_diff": row("g_diff"),
        "w_br": w_br.astype(BF16), "w_out": p["w_out"].astype(BF16),
        "w_router_t": jnp.swapaxes(p["w_router"], 1, 2),
        "b_router": p["b_router"].reshape(n_l, N_EXPERTS, 1),
        "w_exp_gu": p["w_exp_gu"], "w_exp_down": p["w_exp_down"],
        "w_sh_gu": p["w_sh_gu"].astype(BF16), "w_sh_down": p["w_sh_down"].astype(BF16),
    }


def _mixers(latent, l, x, mod3, mod_row, lw, n_b, t_len, tabs=None, past=None, s0=None,
            prev_cache=None):
    lam_init = 0.8 - 0.6 * math.exp(-0.3 * l)
    outs = _inprep_call(latent, l, x, mod3, mod_row, lw, tabs, t_len,
                        None if prev_cache is None else prev_cache[:6])
    qm, kvm, gq, gkv, dq, dkv, ret, rg, gates = outs[:9]
    attn = _attn_call(l, lam_init, qm, kvm, gq, gkv, dq, dkv, lw["diff_lambda"], lw["g_diff"],
                      n_b, t_len, past)
    retn = _ret_call(latent, l, lw["ret_decay"], ret, rg, lw["g_ret"], s0, n_b, t_len,
                     None if prev_cache is None else prev_cache[6])
    mixed = _mixers_call(attn, retn, n_b, t_len // TQ, "mix_lat" if latent else "mix_ctx")
    br, r = mixed[0], mixed[1:]
    y = _merge_call(l, x, mod3, mod_row, br, r[0], gates, lw)
    cache = None if latent else tuple(outs[9:]) + (r[1],)
    return y, cache


def kernel(x_prompt, x_sample, cache_mla_ckv, cache_mla_kpe, cache_gqa_k, cache_gqa_v, cache_diff_k, cache_diff_v, state_ret, c, c_ctx, w_mod, b_mod, g_pre1, g_post1, g_pre2, g_post2, w_in, g_mla_q, w_mla_uq, g_mla_kv, w_mla_ukv, g_gqa_q, g_gqa_k, ret_decay, g_ret, diff_lambda, g_diff, w_br, w_out, w_router, b_router, w_exp_gu, w_exp_down, w_sh_gu, w_sh_down):
    params = dict(w_in=w_in, g_pre1=g_pre1, g_post1=g_post1, g_pre2=g_pre2,
                  g_post2=g_post2, g_mla_q=g_mla_q, w_mla_uq=w_mla_uq,
                  g_mla_kv=g_mla_kv, w_mla_ukv=w_mla_ukv, g_gqa_q=g_gqa_q, g_gqa_k=g_gqa_k,
                  ret_decay=ret_decay, g_ret=g_ret, diff_lambda=diff_lambda, g_diff=g_diff,
                  w_br=w_br, w_out=w_out, w_router=w_router, b_router=b_router,
                  w_exp_gu=w_exp_gu, w_exp_down=w_exp_down, w_sh_gu=w_sh_gu, w_sh_down=w_sh_down)
    n_bc, t_c, _ = x_prompt.shape
    n_bl, t_l, _ = x_sample.shape
    p_len = cache_mla_ckv.shape[2]
    tabs = _rope_tables(t_l)
    n_cond = 8
    cond = jnp.concatenate([c_ctx[None, :], c, jnp.zeros((n_cond - 1 - n_bl, D), F32)], axis=0)
    assert t_l % TM_MERGE == 0 and (t_c * n_bc) % TM_MERGE == 0
    assert t_l % TM_MOE_PRE == 0 and (t_c * n_bc) % TM_MOE_PRE == 0

    yp = x_prompt.reshape(n_bc * t_c, D)
    ys = x_sample.reshape(n_bl * t_l, D)
    cache = None
    lw = _prep_weights(params)
    for l in range(DEPTH):
        mod3 = _mod_call(l, cond, w_mod, b_mod).reshape(n_cond, 1, 6 * D)
        yp, cache = _mixers(False, l, yp, mod3, lambda i: 0, lw, n_bc, t_c, prev_cache=cache)
        yp = _moe_call(l, yp, mod3, lambda i: 0, lw)
        past_kvm = _pastkv_call(l, cache_mla_ckv[:, l].reshape(n_bl * p_len, -1),
                                jnp.tile(cache_mla_kpe[:, l].reshape(n_bl * p_len, -1), (1, 4)),
                                lw["w_ukv"])
        past_gkv = jnp.concatenate([cache_gqa_k[:, l].reshape(n_bl * p_len, -1),
                                    cache_gqa_v[:, l].reshape(n_bl * p_len, -1),
                                    jnp.ones((n_bl * p_len, LANES), F32)], axis=-1).astype(BF16)
        past_dv = jnp.concatenate([cache_diff_v[:, l], jnp.ones_like(cache_diff_v[:, l])], axis=-1)
        past_dkv = jnp.concatenate([cache_diff_k[:, l].reshape(n_bl * p_len, -1),
                                    past_dv.reshape(n_bl * p_len, -1)], axis=-1).astype(BF16)
        s0 = state_ret[:, l].reshape(n_bl, 2, RET_HEADS // 2, 2 * RET_DK, RET_DV)
        ys, _ = _mixers(True, l, ys, mod3, lambda t: 1 + t // t_l, lw, n_bl, t_l, tabs=tabs,
                        past=(past_kvm, past_gkv, past_dkv), s0=s0)
        ys = _moe_call(l, ys, mod3, lambda t: 1 + t // t_l, lw)

    ckv, kpe, gk_t, gv_t, dk_t, dv, ret_state = cache

    def time_minor(a, shape):
        a = a.reshape((n_bc, DEPTH) + shape + (t_c,))
        return jnp.transpose(a, (0, 1, a.ndim - 1) + tuple(range(2, a.ndim - 1)))

    return (yp.reshape(n_bc, t_c, D), ys.reshape(n_bl, t_l, D), ckv, kpe,
            time_minor(gk_t, (GQA_KV_HEADS, GQA_HD)), time_minor(gv_t, (GQA_KV_HEADS, GQA_HD)),
            time_minor(dk_t, (DIFF_HEADS, 2, DIFF_D)),
            dv.reshape(n_bc, DEPTH, t_c, DIFF_HEADS, DIFF_DV), ret_state)
```

```python
import functools
import math

import numpy as np
import jax
import jax.numpy as jnp
from jax import lax
from jax.experimental import pallas as pl
from jax.experimental.pallas import tpu as pltpu
from jax.experimental.pallas import tpu_sc as plsc

F32 = jnp.float32
BF16 = jnp.bfloat16

D = 1024
DEPTH = 2
GRID_W = 64
ROPE_BASE = 10000.0
EPS = 1e-6

MLA_HEADS, MLA_NOPE, MLA_ROPE, MLA_V = 8, 64, 32, 64
MLA_Q_LORA, MLA_KV_LORA = 384, 256
GQA_HEADS, GQA_KV_HEADS, GQA_HD = 8, 2, 64
RET_HEADS, RET_DK, RET_DV = 4, 64, 128
DIFF_HEADS, DIFF_D, DIFF_DV = 4, 64, 128
N_BRANCH, BRANCH_W = 4, 512
N_EXPERTS, TOP_K, N_GROUPS, TOPK_GROUPS = 32, 4, 4, 2
EXPERT_FF, SHARED_FF = 256, 256
ROUTE_SCALE = 2.5
GROUP_SIZE = N_EXPERTS // N_GROUPS

LANES = 128
HALF_LANES = 64
VMEM_LIMIT = 56 * 1024 * 1024

O_CQ, O_CKV, O_KPE, O_GQ, O_GK, O_GV = 0, 384, 640, 672, 1184, 1312
O_RQ, O_RK, O_RV, O_RG, O_DQ, O_DK, O_DV, O_GL, O_END = (
    1440, 1696, 1952, 2464, 2976, 3488, 4000, 4512, 8608)
GQA_ORDER = (0, 4, 1, 5, 2, 6, 3, 7)

KVM_W = 8 * 256
GKV_W = 3 * LANES
DKV_W = 512 + 4 * 256
LOG2E = 1.4426950408889634
TM = 256
TM_LAT = 512
TM_MERGE = 512
TQ = 512


def _query_block(t_len):
    return min(TQ, t_len)
TM_MOE_PRE = 1024
TMX = 256
W_SLOTS = 4


def _cparams(sem):
    return pltpu.CompilerParams(dimension_semantics=sem, vmem_limit_bytes=VMEM_LIMIT)


def _const_spec(shape):
    nd = len(shape)
    return pl.BlockSpec(shape, lambda *_: (0,) * nd)


def _layer_spec(l, shape):
    nd = len(shape)
    return pl.BlockSpec((None,) + tuple(shape), lambda *_: (l,) + (0,) * nd)


def _rms(x, g):
    return x * lax.rsqrt(jnp.mean(x * x, axis=-1, keepdims=True) + EPS) * g


def _dot(a, b):
    return jnp.dot(a, b, preferred_element_type=F32)


def _dot_nt(a, b):
    return lax.dot_general(a, b, (((1,), (1,)), ((), ())), preferred_element_type=F32)


def _silu(x):
    return x * jax.nn.sigmoid(x)


def _lane_iota(shape):
    return lax.broadcasted_iota(jnp.int32, shape, len(shape) - 1)


def _seg_meansq(x, bd_ref, width):
    sq = x * x
    hi = sq.astype(BF16)
    lo = (sq - hi.astype(F32)).astype(BF16)
    bd = bd_ref[0:width, 0:width]
    return (_dot(hi, bd) + _dot(lo, bd)) * (1.0 / GQA_HD)


def _rope(x, cos, sin_signed, half):
    width = x.shape[-1]
    first = (_lane_iota(x.shape) % (2 * half)) < half
    partner = jnp.where(first, pltpu.roll(x, width - half, 1), pltpu.roll(x, half, 1))
    return x * cos + partner * sin_signed


def _tile_lanes(t, reps):
    return t if reps == 1 else jnp.concatenate([t] * reps, axis=1)


def _put_layer(o_ref, val, whole_stack):
    if whole_stack:
        o_ref[0] = val
        for k in range(1, o_ref.shape[0]):
            o_ref[k] = jnp.zeros_like(val)
    else:
        o_ref[...] = val


def _store_kvm(kvm_ref, kv, kpe_b):
    ones = jnp.ones(kpe_b.shape, BF16)
    for p in range(4):
        kvm_ref[:, p * 256:p * 256 + LANES] = kv[:, p * LANES:(p + 1) * LANES].astype(BF16)
        kvm_ref[:, p * 256 + LANES:(p + 1) * 256] = kpe_b
        kvm_ref[:, 1024 + p * 256:1024 + p * 256 + LANES] = (
            kv[:, 512 + p * LANES:512 + (p + 1) * LANES].astype(BF16))
        kvm_ref[:, 1024 + p * 256 + LANES:1024 + (p + 1) * 256] = ones


def _mod_kernel(c_ref, w_ref, b_ref, o_ref):
    a = _silu(c_ref[...]).astype(BF16)
    o_ref[...] = _dot(a, w_ref[...].astype(BF16)) + b_ref[...]


def _mod_call(l, cond, w_mod, b_mod):
    n_l, _, n = w_mod.shape
    tn = 1536
    return pl.pallas_call(
        _mod_kernel,
        grid=(n // tn,),
        in_specs=[_const_spec(cond.shape),
                  pl.BlockSpec((None, D, tn), lambda j: (l, 0, j)),
                  pl.BlockSpec((None, 1, tn), lambda j: (l, 0, j))],
        out_specs=pl.BlockSpec((cond.shape[0], tn), lambda j: (0, j)),
        out_shape=jax.ShapeDtypeStruct((cond.shape[0], n), F32),
        compiler_params=_cparams(("arbitrary",)),
        name="mod",
    )(cond, w_mod, b_mod.reshape(n_l, 1, n))


def _inprep_kernel(latent, n_aliased, *refs):
    (x_ref, mod_ref, gpre_ref, win_ref, gmq_ref, wuq_ref, gmkv_ref, wukv_ref,
     ggq_ref, ggk_ref, bd_ref) = refs[:11]
    refs = refs[11:]
    if latent:
        cos64_ref, sin64_ref, cospe_ref, sinpe_ref = refs[:4]
        refs = refs[4:]
    refs = refs[n_aliased:]
    first_layer = n_aliased == 0
    (qm_ref, kvm_ref, gqo_ref, gkv_ref, dqo_ref, dkv_ref, ret_ref, rg_ref, gate_ref) = refs[:9]
    refs = refs[9:]
    if not latent:
        ckv_o, kpe_o, gk_o, gv_o, dk_o, dv_o = refs

    x = x_ref[...]
    mod = mod_ref[...]
    sh1 = mod[:, 0:D]
    sc1 = mod[:, D:2 * D]
    hb = (_rms(x, gpre_ref[...]) * (1.0 + sc1) + sh1).astype(BF16)

    def z(a, b):
        return _dot_nt(hb, win_ref[a:b, :])

    if latent:
        cos64, sin64 = cos64_ref[...], sin64_ref[...]
        cospe, sinpe = cospe_ref[...], sinpe_ref[...]

    cqn = _rms(z(O_CQ, O_CKV), gmq_ref[...]).astype(BF16)
    q = _dot(cqn, wuq_ref[...]) * ((MLA_NOPE + MLA_ROPE) ** -0.5 * LOG2E)
    q_nope, q_pe = q[:, 0:512], q[:, 512:768]
    if latent:
        q_pe = _rope(q_pe, _tile_lanes(cospe, 2), _tile_lanes(sinpe, 2), MLA_ROPE // 4)
    qm_ref[:, 0:512] = q_nope.astype(BF16)
    qm_ref[:, 512:768] = q_pe.astype(BF16)

    ckvn = _rms(z(O_CKV, O_KPE), gmkv_ref[...])
    kv = _dot(ckvn.astype(BF16), wukv_ref[...])
    kpe4 = _dot_nt(hb, jnp.concatenate([win_ref[O_KPE:O_GQ, :]] * 4, axis=0))
    if latent:
        kpe4 = _rope(kpe4, cospe, sinpe, MLA_ROPE // 4)
    else:
        _put_layer(ckv_o, ckvn, first_layer)
        _put_layer(kpe_o, kpe4[:, 0:MLA_ROPE], first_layer)
    _store_kvm(kvm_ref, kv, kpe4.astype(BF16))

    gq = _dot_nt(hb, jnp.concatenate(
        [win_ref[O_GQ + h * GQA_HD:O_GQ + (h + 1) * GQA_HD, :] for h in GQA_ORDER], axis=0))
    gq = gq * lax.rsqrt(_seg_meansq(gq, bd_ref, 512) + EPS) * ggq_ref[...]
    gk = z(O_GK, O_GV)
    gk = gk * lax.rsqrt(_seg_meansq(gk, bd_ref, LANES) + EPS) * ggk_ref[...]
    gv = z(O_GV, O_RQ)
    if latent:
        gq = _rope(gq, _tile_lanes(cos64, 4), _tile_lanes(sin64, 4), GQA_HD // 4)
        gk = _rope(gk, cos64, sin64, GQA_HD // 4)
    else:
        _put_layer(gk_o, jnp.transpose(gk), first_layer)
        _put_layer(gv_o, jnp.transpose(gv), first_layer)
    gqo_ref[...] = (gq * (GQA_HD ** -0.5 * LOG2E)).astype(BF16)
    gkv_ref[:, 0:LANES] = gk.astype(BF16)
    gkv_ref[:, LANES:2 * LANES] = gv.astype(BF16)
    gkv_ref[:, 2 * LANES:3 * LANES] = jnp.ones(gv.shape, BF16)

    dq = z(O_DQ, O_DK)
    dk = z(O_DK, O_DV)
    dv = z(O_DV, O_GL)
    if latent:
        dq = _rope(dq, _tile_lanes(cos64, 4), _tile_lanes(sin64, 4), DIFF_D // 4)
        dk = _rope(dk, _tile_lanes(cos64, 4), _tile_lanes(sin64, 4), DIFF_D // 4)
    else:
        _put_layer(dk_o, jnp.transpose(dk), first_layer)
        _put_layer(dv_o, dv, first_layer)
    dqo_ref[...] = (dq * (DIFF_D ** -0.5 * LOG2E)).astype(BF16)
    dkv_ref[:, 0:512] = dk.astype(BF16)
    for h in range(DIFF_HEADS):
        dkv_ref[:, 512 + h * 256:512 + h * 256 + LANES] = dv[:, h * LANES:(h + 1) * LANES].astype(BF16)
        dkv_ref[:, 512 + h * 256 + LANES:512 + (h + 1) * 256] = jnp.ones((dv.shape[0], LANES), BF16)

    ret_ref[:, 0:256] = z(O_RQ, O_RK).astype(BF16)
    ret_ref[:, 256:512] = (z(O_RK, O_RV) * (RET_DK ** -0.5)).astype(BF16)
    ret_ref[:, 512:1024] = z(O_RV, O_RG).astype(BF16)
    rg_ref[...] = z(O_RG, O_DQ).astype(BF16)

    for n in range(N_BRANCH):
        gate_ref[:, n * D:(n + 1) * D] = jax.nn.sigmoid(
            z(O_GL + n * D, O_GL + (n + 1) * D)).astype(BF16)


def _inprep_call(latent, l, x, mod3, mod_row, lw, tabs, t_len, prev_caches=None):
    n_tok = x.shape[0]
    tm = TM_LAT if latent else TM
    nblk = n_tok // tm
    blk_per_seq = t_len // tm

    def tok(w):
        return pl.BlockSpec((tm, w), lambda i: (i, 0))

    in_specs = [tok(D),
                pl.BlockSpec((None, 1, 6 * D), lambda i: (mod_row(i * tm), 0, 0)),
                _layer_spec(l, (1, D)),
                pl.BlockSpec((None, O_END, D), lambda i: (l, 0, 0), pipeline_mode=pl.Buffered(1)),
                _layer_spec(l, (1, MLA_Q_LORA)), _layer_spec(l, (MLA_Q_LORA, 768)),
                _layer_spec(l, (1, MLA_KV_LORA)), _layer_spec(l, (MLA_KV_LORA, 1024)),
                _layer_spec(l, (1, 512)), _layer_spec(l, (1, LANES)), _const_spec((512, 512))]
    args = [x, mod3, lw["g_pre1"], lw["w_in"], lw["g_mla_q"], lw["w_uq"], lw["g_mla_kv"],
            lw["w_ukv"], lw["g_gqa_q"], lw["g_gqa_k"], lw["bd"]]
    if latent:
        tab_spec = pl.BlockSpec((tm, LANES), lambda i: (i % blk_per_seq, 0))
        in_specs += [tab_spec] * 4
        args += list(tabs)
    widths = [768, KVM_W, 512, GKV_W, 512, DKV_W, 1024, 512, 4 * D]
    out_specs = [tok(w) for w in widths]
    out_shape = [jax.ShapeDtypeStruct((n_tok, w), BF16) for w in widths]
    aliases = {}
    if not latent:
        assert TM == t_len and (prev_caches is None) == (l == 0)
        n_seq = n_tok // t_len
        lead, at = ((DEPTH,), 0) if l == 0 else ((None,), l)

        def row_major(w):
            out_specs.append(pl.BlockSpec((None,) + lead + (t_len, w), lambda i: (i, at, 0, 0)))
            out_shape.append(jax.ShapeDtypeStruct((n_seq, DEPTH, t_len, w), F32))

        row_major(MLA_KV_LORA)
        row_major(MLA_ROPE)
        for w in (LANES, LANES, 512):
            out_specs.append(pl.BlockSpec((None,) + lead + (w, t_len), lambda i: (i, at, 0, 0)))
            out_shape.append(jax.ShapeDtypeStruct((n_seq, DEPTH, w, t_len), F32))
        row_major(512)
        if prev_caches is not None:
            n_in = len(args)
            in_specs += [pl.BlockSpec(memory_space=pl.ANY)] * len(prev_caches)
            args += list(prev_caches)
            aliases = {n_in + k: len(widths) + k for k in range(len(prev_caches))}
    return pl.pallas_call(
        functools.partial(_inprep_kernel, latent, len(aliases)),
        grid=(nblk,),
        in_specs=in_specs, out_specs=out_specs, out_shape=out_shape,
        input_output_aliases=aliases,
        compiler_params=_cparams(("arbitrary",)),
        name="inprep_lat" if latent else "inprep_ctx",
    )(*args)


def _pastkv_kernel(ckv_ref, kpe_ref, wukv_ref, o_ref):
    kv = _dot(ckv_ref[...].astype(BF16), wukv_ref[...])
    _store_kvm(o_ref, kv, kpe_ref[...].astype(BF16))


def _pastkv_call(l, ckv, kpe4, w_ukv):
    n = ckv.shape[0]
    return pl.pallas_call(
        _pastkv_kernel,
        grid=(n // TM,),
        in_specs=[pl.BlockSpec((TM, MLA_KV_LORA), lambda i: (i, 0)),
                  pl.BlockSpec((TM, LANES), lambda i: (i, 0)),
                  _layer_spec(l, (MLA_KV_LORA, 1024))],
        out_specs=pl.BlockSpec((TM, KVM_W), lambda i: (i, 0)),
        out_shape=jax.ShapeDtypeStruct((n, KVM_W), BF16),
        compiler_params=_cparams(("arbitrary",)),
        name="pastkv",
    )(ckv, kpe4, w_ukv)


def _softmax_pv(s, v_ones):
    m = jnp.max(s, axis=-1, keepdims=True)
    p = jnp.exp2(s - m).astype(BF16)
    o = _dot(p, v_ones)
    return o[:, 0:LANES] / o[:, LANES:2 * LANES]


def _attn_kernel(lam_init, n_past, qm_ref, kvm_ref, gq_ref, gkv_ref, dq_ref, dkv_ref, *refs):
    if n_past:
        past_refs, refs = refs[:3], refs[3:]
        lam_ref, gdiff_ref, o_ref = refs[:3]
        joined = refs[3:]

        @pl.when(pl.program_id(1) == 0)
        def _():
            for dst, past, new in zip(joined, past_refs, (kvm_ref, gkv_ref, dkv_ref)):
                dst[0:n_past, :] = past[...]
                dst[n_past:, :] = new[...]

        kvm_ref, gkv_ref, dkv_ref = joined
    else:
        lam_ref, gdiff_ref, o_ref = refs
    tq = qm_ref.shape[0]
    lane = _lane_iota((tq, LANES))
    low = lane < HALF_LANES
    zero = jnp.zeros((tq, LANES), BF16)

    for p in range(MLA_HEADS // 2):
        qn = qm_ref[:, p * LANES:(p + 1) * LANES]
        g = p // 2
        qpe = qm_ref[:, 512 + g * LANES:512 + (g + 1) * LANES]
        kk = kvm_ref[:, p * 256:(p + 1) * 256]
        vv = kvm_ref[:, 1024 + p * 256:1024 + (p + 1) * 256]
        outs = []
        for half in range(2):
            h = 2 * p + half
            slot = h % 4
            in_slot = (lane >= slot * MLA_ROPE) & (lane < (slot + 1) * MLA_ROPE)
            lhs = jnp.concatenate(
                [jnp.where(low if half == 0 else ~low, qn, zero),
                 jnp.where(in_slot, qpe, zero)], axis=1)
            outs.append(_softmax_pv(_dot_nt(lhs, kk), vv))
        o_ref[:, p * LANES:(p + 1) * LANES] = jnp.where(low, outs[0], outs[1]).astype(BF16)

    kk = gkv_ref[:, 0:LANES]
    vv = gkv_ref[:, LANES:3 * LANES]
    for g in range(GQA_HEADS // 2):
        qg = gq_ref[:, g * LANES:(g + 1) * LANES]
        o_lo = _softmax_pv(_dot_nt(jnp.where(low, qg, zero), kk), vv)
        o_hi = _softmax_pv(_dot_nt(jnp.where(low, zero, qg), kk), vv)
        o_ref[:, 512 + g * LANES:512 + (g + 1) * LANES] = jnp.where(low, o_lo, o_hi).astype(BF16)

    lp = lam_ref[...]
    lam = (jnp.exp(jnp.sum(lp[0:1] * lp[1:2], axis=-1, keepdims=True))
           - jnp.exp(jnp.sum(lp[2:3] * lp[3:4], axis=-1, keepdims=True)) + lam_init)
    for h in range(DIFF_HEADS):
        qh = dq_ref[:, h * LANES:(h + 1) * LANES]
        kk = dkv_ref[:, h * LANES:(h + 1) * LANES]
        vv = dkv_ref[:, 512 + h * 256:512 + (h + 1) * 256]
        a1 =_softmax_pv(_dot_nt(jnp.where(low, qh, zero), kk), vv)
        a2 = _softmax_pv(_dot_nt(jnp.where(low, zero, qh), kk), vv)
        od = _rms(a1 - lam * a2, gdiff_ref[...]) * (1.0 - lam_init)
        o_ref[:, 1024 + h * LANES:1024 + (h + 1) * LANES] = od.astype(BF16)


def _attn_call(l, lam_init, qm, kvm, gq, gkv, dq, dkv, lam_p, g_diff, n_b, t_len, past=None):
    tq = _query_block(t_len)
    nq = t_len // tq
    n_past = 0 if past is None else past[0].shape[0] // n_b

    def qspec(w):
        return pl.BlockSpec((tq, w), lambda b, i: (b * nq + i, 0))

    def kspec(w, rows=t_len):
        return pl.BlockSpec((rows, w), lambda b, i: (b, 0))

    in_specs = [qspec(768), kspec(KVM_W), qspec(512), kspec(GKV_W), qspec(512), kspec(DKV_W)]
    args = [qm, kvm, gq, gkv, dq, dkv]
    scratch = []
    if n_past:
        in_specs += [kspec(KVM_W, n_past), kspec(GKV_W, n_past), kspec(DKV_W, n_past)]
        args += list(past)
        scratch = [pltpu.VMEM((n_past + t_len, w), BF16) for w in (KVM_W, GKV_W, DKV_W)]
    in_specs += [_layer_spec(l, (4, DIFF_D)), _layer_spec(l, (1, DIFF_DV))]
    args += [lam_p, g_diff]
    return dict(kernel=functools.partial(_attn_kernel, lam_init, n_past), in_specs=in_specs, args=args,
                out_specs=[qspec(3 * BRANCH_W)],
                out_shape=[jax.ShapeDtypeStruct((n_b * t_len, 3 * BRANCH_W), BF16)],
                scratch=scratch, aliases={})


def _log_sigmoid(x):
    return jnp.minimum(x, 0.0) - jnp.log(1.0 + jnp.exp(-jnp.abs(x)))


def _log_gamma(dec_ref, l, d, h):
    return _log_sigmoid(jnp.full((1, 1), dec_ref[l, d, h], F32))


def _ret_kernel(latent, l, t_len, dec_ref, q_ref, k_ref, v_ref, rg_ref, gret_ref, *refs):
    if latent:
        s0_ref, o_ref = refs
    else:
        o_ref, st_ref = refs[-2:]
    tq = q_ref.shape[0]
    t0 = pl.program_id(1) * tq
    lane = _lane_iota((tq, LANES))
    low = lane < HALF_LANES
    zero = jnp.zeros((tq, LANES), BF16)
    t_idx = (t0 + lax.broadcasted_iota(jnp.int32, (tq, t_len), 0)).astype(F32)
    s_idx = lax.broadcasted_iota(jnp.int32, (tq, t_len), 1).astype(F32)
    dist = t_idx - s_idx
    past = dist >= 0
    diag = jnp.where(dist == 0, 1.0, 0.0)
    t_col = (t0 + lax.broadcasted_iota(jnp.int32, (tq, 1), 0)).astype(F32)

    def lg(d, h):
        return _log_gamma(dec_ref, l, d, h)

    for h in range(RET_HEADS):
        p, half = h // 2, h % 2
        qp = q_ref[:, p * LANES:(p + 1) * LANES]
        qm = jnp.where(low if half == 0 else ~low, qp, zero)
        kp = k_ref[:, p * LANES:(p + 1) * LANES]
        vh = v_ref[:, h * LANES:(h + 1) * LANES]
        lgf, lgb = lg(0, h), lg(1, h)
        dmask = jnp.exp(jnp.where(past, lgf, -lgb) * dist) + diag
        o = _dot((_dot_nt(qm, kp) * dmask).astype(BF16), vh)
        if latent:
            sf =s0_ref[0, p].astype(BF16)
            sb = s0_ref[1, p].astype(BF16)
            o = o + _dot(qm, sf) * jnp.exp(lgf * (t_col + 1.0))
            o = o + _dot(qm, sb) * jnp.exp(lgb * (float(t_len) - t_col))
        mu = jnp.mean(o, axis=-1, keepdims=True)
        oc = o - mu
        y = oc * lax.rsqrt(jnp.mean(oc * oc, axis=-1, keepdims=True) + EPS)
        y = y * gret_ref[:, h * LANES:(h + 1) * LANES]
        rg = rg_ref[:, h * LANES:(h + 1) * LANES].astype(F32)
        o_ref[:, h * LANES:(h + 1) * LANES] = (y * _silu(rg)).astype(BF16)

    if not latent:
        s_col = lax.broadcasted_iota(jnp.int32, (t_len, 1), 0).astype(F32)
        lane_t = _lane_iota((1, LANES)) < HALF_LANES
        for p in range(RET_HEADS // 2):
            kp = k_ref[:, p * LANES:(p + 1) * LANES].astype(F32)
            for d in range(2):
                lg_lane = jnp.where(lane_t, lg(d, 2 * p), lg(d, 2 * p + 1))
                expo = (float(t_len) - 1.0 - s_col) if d == 0 else s_col
                kdec_t = jnp.transpose(kp * jnp.exp(lg_lane * expo)).astype(BF16)
                for half in range(2):
                    h = 2 * p + half
                    st = _dot(kdec_t, v_ref[:, h * LANES:(h + 1) * LANES])
                    st = st[half * RET_DK:(half + 1) * RET_DK, :]
                    if l == 0:
                        st_ref[0, d, h] = st
                        for k in range(1, st_ref.shape[0]):
                            st_ref[k, d, h] = jnp.zeros_like(st)
                    else:
                        st_ref[d, h] = st


def _ret_call(latent, l, dec, ret, rg, g_ret, s0, n_b, t_len, prev_state=None):
    tq = _query_block(t_len)
    nq = t_len // tq
    aliases = {}
    assert latent or nq == 1
    in_specs = [pl.BlockSpec(memory_space=pltpu.SMEM),
                pl.BlockSpec((tq, 256), lambda b, i: (b * nq + i, 0)),
                pl.BlockSpec((t_len, 256), lambda b, i: (b, 1)),
                pl.BlockSpec((t_len, 512), lambda b, i: (b, 1)),
                pl.BlockSpec((tq, 512), lambda b, i: (b * nq + i, 0)),
                _layer_spec(l, (1, 512))]
    args = [dec, ret, ret, ret, rg, g_ret]
    out_specs = [pl.BlockSpec((tq, 512), lambda b, i: (b * nq + i, 0))]
    out_shape = [jax.ShapeDtypeStruct((n_b * t_len, 512), BF16)]
    if latent:
        in_specs.append(pl.BlockSpec((None, 2, 2, LANES, LANES), lambda b, i: (b, 0, 0, 0, 0)))
        args.append(s0)
    else:
        assert (prev_state is None) == (l == 0)
        lead, at = ((DEPTH,), 0) if l == 0 else ((None,), l)
        out_specs.append(pl.BlockSpec((None,) + lead + (2, RET_HEADS, RET_DK, RET_DV),
                                      lambda b, i: (b, at, 0, 0, 0, 0)))
        out_shape.append(jax.ShapeDtypeStruct((n_b, DEPTH, 2, RET_HEADS, RET_DK, RET_DV), F32))
        if prev_state is not None:
            aliases = {len(args): 1}
            in_specs.append(pl.BlockSpec(memory_space=pl.ANY))
            args.append(prev_state)
    return dict(kernel=functools.partial(_ret_kernel, latent, l, t_len), in_specs=in_specs, args=args,
                out_specs=out_specs, out_shape=out_shape, scratch=[], aliases=aliases)


def _mixers_call(attn, ret, n_b, nq, name):
    n_ai, n_ri = len(attn["args"]), len(ret["args"])
    n_ao, n_ro = len(attn["out_specs"]), len(ret["out_specs"])

    def kernel(*refs):
        a_in, r_in = refs[:n_ai], refs[n_ai:n_ai + n_ri]
        outs = refs[n_ai + n_ri:n_ai + n_ri + n_ao + n_ro]
        scratch = refs[n_ai + n_ri + n_ao + n_ro:]
        attn["kernel"](*a_in, *outs[:n_ao], *scratch)
        ret["kernel"](*r_in, *outs[n_ao:])

    aliases = {n_ai + k: n_ao + v for k, v in ret["aliases"].items()}
    return pl.pallas_call(
        kernel,
        grid=(n_b, nq),
        in_specs=attn["in_specs"] + ret["in_specs"],
        out_specs=attn["out_specs"] + ret["out_specs"],
        out_shape=attn["out_shape"] + ret["out_shape"],
        scratch_shapes=attn["scratch"],
        input_output_aliases=aliases,
        compiler_params=_cparams(("arbitrary", "arbitrary")),
        name=name,
    )(*attn["args"], *ret["args"])


def _merge_kernel(x_ref, mod_ref, br_ref, or_ref, gate_ref, wbr_ref, wout_ref, gpost_ref, o_ref):
    merged = None
    for n in range(N_BRANCH):
        if n < 2:
            b = br_ref[:, n * BRANCH_W:(n + 1) * BRANCH_W]
        elif n == 2:
            b = or_ref[...]
        else:
            b = br_ref[:, 2 * BRANCH_W:3 * BRANCH_W]
        t = gate_ref[:, n * D:(n + 1) * D].astype(F32) * _dot(b, wbr_ref[n])
        merged = t if merged is None else merged + t
    out = _dot(merged.astype(BF16), wout_ref[...])
    g1 = mod_ref[...][:, 2 * D:3 * D]
    o_ref[...] = x_ref[...] + g1 * _rms(out, gpost_ref[...])


def _merge_call(l, x, mod3, mod_row, br, o_r, gates, lw):
    n_tok = x.shape[0]

    def tok(w):
        return pl.BlockSpec((TM_MERGE, w), lambda i: (i, 0))

    return pl.pallas_call(
        _merge_kernel,
        grid=(n_tok // TM_MERGE,),
        in_specs=[tok(D), pl.BlockSpec((None, 1, 6 * D), lambda i: (mod_row(i * TM_MERGE), 0, 0)),
                  tok(3 * BRANCH_W), tok(BRANCH_W), tok(4 * D),
                  _layer_spec(l, (N_BRANCH, BRANCH_W, D)), _layer_spec(l, (D, D)),
                  _layer_spec(l, (1, D))],
        out_specs=tok(D),
        out_shape=jax.ShapeDtypeStruct((n_tok, D), F32),
        compiler_params=_cparams(("arbitrary",)),
        name="merge",
    )(x, mod3, br, o_r, gates, lw["w_br"], lw["w_out"], lw["g_post1"])


def _route(logits_t, bias):
    n = logits_t.shape[1]
    scores = jax.nn.sigmoid(logits_t)
    sel = scores + bias
    neg = -jnp.inf
    sub = lax.broadcasted_iota(jnp.int32, (GROUP_SIZE, n), 0)
    grp = []
    for g in range(N_GROUPS):
        blk = sel[g * GROUP_SIZE:(g + 1) * GROUP_SIZE]
        m1 = jnp.max(blk, axis=0, keepdims=True)
        i1 = jnp.min(jnp.where(blk == m1, sub, GROUP_SIZE), axis=0, keepdims=True)
        m2 = jnp.max(jnp.where(sub == i1, neg, blk), axis=0, keepdims=True)
        grp.append(m1 + m2)
    parts = []
    for g in range(N_GROUPS):
        beaten = jnp.zeros((1, n), jnp.int32)
        for o in range(N_GROUPS):
            if o == g:
                continue
            wins = (grp[o] > grp[g]) | (grp[o] == grp[g]) if o < g else (grp[o] > grp[g])
            beaten = beaten + wins.astype(jnp.int32)
        keep = beaten < TOPK_GROUPS
        parts.append(jnp.where(keep, sel[g * GROUP_SIZE:(g + 1) * GROUP_SIZE], neg))
    cur = jnp.concatenate(parts, axis=0)
    eidx = lax.broadcasted_iota(jnp.int32, (N_EXPERTS, n), 0)
    hits, ids, ws = [], [], []
    for _ in range(TOP_K):
        m = jnp.max(cur, axis=0, keepdims=True)
        i = jnp.min(jnp.where(cur == m, eidx, N_EXPERTS), axis=0, keepdims=True)
        hit = eidx == i
        hits.append(hit)
        ids.append(i)
        ws.append(jnp.sum(jnp.where(hit, scores, 0.0), axis=0, keepdims=True))
        cur = jnp.where(hit, neg, cur)
    wsum = ws[0] + ws[1] + ws[2] + ws[3]
    return hits, ids, [w / wsum * ROUTE_SCALE for w in ws]


U32 = jnp.uint32
HIGH16 = np.uint32(0xFFFF0000)


def _bf16_bits(v):
    return lax.bitcast_convert_type(v.astype(BF16).astype(F32), U32)


def _pack_rows(v):
    return (_bf16_bits(v[:, 0:D // 2]) >> 16) | _bf16_bits(v[:, D // 2:D])


def _unpack_rows(p):
    lo = lax.bitcast_convert_type(p << 16, F32)
    hi = lax.bitcast_convert_type(p & HIGH16, F32)
    return jnp.concatenate([lo, hi], axis=1)


def _moe_pre_kernel(x_ref, mod_ref, gpre_ref, wr_ref, br_ref, tri_ref,
                    hp_ref, eidx_ref, rank_ref, comb_ref, cnt_ref, run_ref):
    tm = x_ref.shape[0]

    @pl.when(pl.program_id(0) == 0)
    def _():
        run_ref[...] = jnp.zeros_like(run_ref)

    mod = mod_ref[...]
    sh2, sc2 = mod[:, 3 * D:4 * D], mod[:, 4 * D:5 * D]
    h = _rms(x_ref[...], gpre_ref[...]) * (1.0 + sc2) + sh2
    hp_ref[...] = _pack_rows(h)
    hb = h.astype(BF16)
    h_lo = (h - hb.astype(F32)).astype(BF16)
    wr = wr_ref[...]
    wr_hi = wr.astype(BF16)
    wr_lo = (wr - wr_hi.astype(F32)).astype(BF16)
    logits_t = _dot_nt(wr_hi, hb) + _dot_nt(wr_hi, h_lo) + _dot_nt(wr_lo, hb)
    hits, ids, ws = _route(logits_t, br_ref[...])

    picked = jnp.zeros((N_EXPERTS, tm), F32)
    for hit in hits:
        picked = jnp.where(hit, 1.0, picked)
    before = _dot(picked.astype(BF16), tri_ref[...]) + run_ref[:, 0:1]
    sub8 = lax.broadcasted_iota(jnp.int32, (8, tm), 0)
    comb8 = jnp.zeros((8, tm), F32)
    for k in range(TOP_K):
        rank = jnp.sum(jnp.where(hits[k], before, 0.0), axis=0, keepdims=True)
        eidx_ref[k:k + 1, :] = ids[k]
        rank_ref[k:k + 1, :] = rank.astype(jnp.int32)
        comb8 = jnp.where(sub8 == k, ws[k], comb8)
    comb_ref[...] = jnp.transpose(
        jnp.concatenate([comb8, jnp.zeros((LANES - 8, tm), F32)], axis=0))
    run_ref[...] = run_ref[...] + jnp.sum(picked, axis=1, keepdims=True)
    cnt_ref[...] = run_ref[...]


def _moe_pre_call(l, x, mod3, mod_row, lw):
    n_tok = x.shape[0]
    tm = TM_MOE_PRE
    tri = np.arange(tm)
    tri = jnp.asarray(tri[:, None] < tri[None, :], BF16)
    row4 = pl.BlockSpec((TOP_K, tm), lambda i: (0, i))
    return pl.pallas_call(
        _moe_pre_kernel,
        grid=(n_tok // tm,),
        in_specs=[pl.BlockSpec((tm, D), lambda i: (i, 0)),
                  pl.BlockSpec((None, 1, 6 * D), lambda i: (mod_row(i * tm), 0, 0)),
                  _layer_spec(l, (1, D)), _layer_spec(l, (N_EXPERTS, D)),
                  _layer_spec(l, (N_EXPERTS, 1)), _const_spec((tm, tm))],
        out_specs=[pl.BlockSpec((tm, D // 2), lambda i: (i, 0)), row4, row4,
                   pl.BlockSpec((tm, LANES), lambda i: (i, 0)),
                   _const_spec((N_EXPERTS, LANES))],
        out_shape=[jax.ShapeDtypeStruct((n_tok, D // 2), U32),
                   jax.ShapeDtypeStruct((TOP_K, n_tok), jnp.int32),
                   jax.ShapeDtypeStruct((TOP_K, n_tok), jnp.int32),
                   jax.ShapeDtypeStruct((n_tok, LANES), F32),
                   jax.ShapeDtypeStruct((N_EXPERTS, LANES), F32)],
        scratch_shapes=[pltpu.VMEM((N_EXPERTS, LANES), F32)],
        compiler_params=_cparams(("arbitrary",)),
        name="moe_pre",
    )(x, mod3, lw["g_pre2"], lw["w_router_t"], lw["b_router"], tri)


def _moe_plan_kernel(eidx_ref, rank_ref, cnt_ref, dest_ref, te_ref, tv_ref, tn_ref):
    tm = eidx_ref.shape[1]
    cnt = cnt_ref[...]
    padded = jnp.ceil(cnt * (1.0 / TMX)) * TMX
    row = lax.broadcasted_iota(jnp.int32, cnt.shape, 0)
    incl = padded
    shift = 1
    while shift < N_EXPERTS:
        incl = incl + jnp.where(row >= shift, pltpu.roll(incl, shift, 0), 0.0)
        shift *= 2
    start = (incl - padded)[:, 0:1]
    end = incl[:, 0:1]
    erow = lax.broadcasted_iota(jnp.int32, (N_EXPERTS, tm), 0)
    for k in range(TOP_K):
        mine = erow == eidx_ref[k:k + 1, :]
        base = jnp.sum(jnp.where(mine, start, 0.0), axis=0, keepdims=True)
        dest_ref[k:k + 1, :] = rank_ref[k:k + 1, :] + base.astype(jnp.int32)

    @pl.when(pl.program_id(0) == 0)
    def _():
        tile0 = (_lane_iota((1, LANES)) * TMX).astype(F32)
        owner = jnp.sum(jnp.where(end <= tile0, 1.0, 0.0), axis=0, keepdims=True)
        owner = jnp.minimum(owner, N_EXPERTS - 1.0)
        erow_t = lax.broadcasted_iota(jnp.int32, (N_EXPERTS, LANES), 0).astype(F32)
        left = jnp.sum(jnp.where(erow_t == owner, cnt[:, 0:1] - (tile0 - start), 0.0),
                       axis=0, keepdims=True)
        te_ref[...] = owner.astype(jnp.int32)
        tv_ref[...] = jnp.clip(left, 0.0, float(TMX)).astype(jnp.int32)
        tn_ref[...] = jnp.full(tn_ref.shape, N_EXPERTS, jnp.int32)
        nxt = owner
        for k in range(W_SLOTS - 1):
            later = (erow_t > nxt) & (cnt[:, 0:1] > 0.0)
            nxt = jnp.min(jnp.where(later, erow_t, float(N_EXPERTS)), axis=0, keepdims=True)
            tn_ref[k:k + 1, :] = nxt.astype(jnp.int32)
        n_used = jnp.sum(jnp.where(left > 0.0, 1.0, 0.0), axis=1, keepdims=True)
        tn_ref[7:8, :] = jnp.minimum(tile0 * (1.0 / TMX), n_used - 1.0).astype(jnp.int32)


def _moe_plan_call(eidx, rank, cnt):
    n_tok = eidx.shape[1]
    tm = TM_MOE_PRE
    row4 = pl.BlockSpec((TOP_K, tm), lambda i: (0, i))
    tiles = jax.ShapeDtypeStruct((1, LANES), jnp.int32)
    return pl.pallas_call(
        _moe_plan_kernel,
        grid=(n_tok // tm,),
        in_specs=[row4, row4, _const_spec((N_EXPERTS, LANES))],
        out_specs=[row4, _const_spec((1, LANES)), _const_spec((1, LANES)), _const_spec((8, LANES))],
        out_shape=[jax.ShapeDtypeStruct((TOP_K, n_tok), jnp.int32), tiles, tiles,
                   jax.ShapeDtypeStruct((8, LANES), jnp.int32)],
        compiler_params=_cparams(("arbitrary",)),
        name="moe_plan",
    )(eidx, rank, cnt)


def _experts_kernel(l, te_ref, tv_ref, tn_ref, xs_ref, wgu_hbm, wdn_hbm, ys_ref,
                    wgu_f, wdn_f, wgu_b, wdn_b, sem, group_ref):
    j = pl.program_id(0)
    valid = tv_ref[j]
    expert = te_ref[j]

    def fetch(e, slot):
        return (pltpu.make_async_copy(wgu_hbm.at[l, e], wgu_f.at[slot], sem.at[slot, 0]),
                pltpu.make_async_copy(wdn_hbm.at[l, e], wdn_f.at[slot], sem.at[slot, 1]))

    def start_if_any(e, slot):
        @pl.when(e < N_EXPERTS)
        def _():
            for cp in fetch(e, slot):
                cp.start()

    @pl.when(j == 0)
    def _():
        group_ref[0] = 0
        start_if_any(expert, 0)
        for k in range(W_SLOTS - 2):
            start_if_any(tn_ref[k, 0], k + 1)

    first_tile = (j == 0) | (expert != te_ref[jnp.maximum(j - 1, 0)])

    @pl.when(first_tile & (valid > 0))
    def _():
        group = group_ref[0]
        slot = lax.rem(group, W_SLOTS)
        for cp in fetch(expert, slot):
            cp.wait()
        wgu_b[...] = wgu_f[slot].astype(BF16)
        wdn_b[...] = wdn_f[slot].astype(BF16)
        start_if_any(tn_ref[W_SLOTS - 2, j], lax.rem(group + W_SLOTS - 1, W_SLOTS))
        group_ref[0] = group + 1

    @pl.when(valid > 0)
    def _():
        rows = lax.broadcasted_iota(jnp.int32, (TMX, D), 0)
        x = jnp.where(rows < valid, _unpack_rows(xs_ref[...]), 0.0).astype(BF16)
        gu = _dot(x, wgu_b[...])
        a = _silu(gu[:, 0:EXPERT_FF]) * gu[:, EXPERT_FF:2 * EXPERT_FF]
        ys_ref[...] = _pack_rows(_dot(a.astype(BF16), wdn_b[...]))


def _experts_call(l, xs, te, tv, tn, w_gu, w_dn):
    n_tiles = xs.shape[0] // TMX
    grid_spec = pltpu.PrefetchScalarGridSpec(
        num_scalar_prefetch=3,
        grid=(n_tiles,),
        in_specs=[pl.BlockSpec((TMX, D // 2), lambda j, te, tv, tn: (tn[7, j], 0)),
                  pl.BlockSpec(memory_space=pl.ANY), pl.BlockSpec(memory_space=pl.ANY)],
        out_specs=pl.BlockSpec((TMX, D // 2), lambda j, te, tv, tn: (tn[7, j], 0)),
        scratch_shapes=[pltpu.VMEM((W_SLOTS, D, 2 * EXPERT_FF), F32),
                        pltpu.VMEM((W_SLOTS, EXPERT_FF, D), F32),
                        pltpu.VMEM((D, 2 * EXPERT_FF), BF16), pltpu.VMEM((EXPERT_FF, D), BF16),
                        pltpu.SemaphoreType.DMA((W_SLOTS, 2)), pltpu.SMEM((1,), jnp.int32)])
    return pl.pallas_call(
        functools.partial(_experts_kernel, l),
        grid_spec=grid_spec,
        out_shape=jax.ShapeDtypeStruct(xs.shape, U32),
        compiler_params=_cparams(("arbitrary",)),
        name="moe_experts",
    )(te, tv, tn, xs, w_gu, w_dn)


def _moe_post_kernel(x_ref, mod_ref, hp_ref, yg_ref, comb_ref, wsgu_ref, wsdn_ref, gpost_ref,
                     o_ref):
    hb = _unpack_rows(hp_ref[...]).astype(BF16)
    sgu = _dot(hb, wsgu_ref[...])
    sa = _silu(sgu[:, 0:SHARED_FF]) * sgu[:, SHARED_FF:2 * SHARED_FF]
    acc = _dot(sa.astype(BF16), wsdn_ref[...])
    comb = comb_ref[...]
    for k in range(TOP_K):
        acc = acc + comb[:, k:k + 1] * _unpack_rows(yg_ref[k])
    g2 = mod_ref[...][:, 5 * D:6 * D]
    o_ref[...] = x_ref[...] + g2 * _rms(acc, gpost_ref[...])


def _moe_post_call(l, x, mod3, mod_row, hp, yg, comb, lw):
    n_tok = x.shape[0]
    tm = TM_MOE_PRE
    return pl.pallas_call(
        _moe_post_kernel,
        grid=(n_tok // tm,),
        in_specs=[pl.BlockSpec((tm, D), lambda i: (i, 0)),
                  pl.BlockSpec((None, 1, 6 * D), lambda i: (mod_row(i * tm), 0, 0)),
                  pl.BlockSpec((tm, D // 2), lambda i: (i, 0)),
                  pl.BlockSpec((TOP_K, tm, D // 2), lambda i: (0, i, 0)),
                  pl.BlockSpec((tm, LANES), lambda i: (i, 0)),
                  _layer_spec(l, (D, 2 * SHARED_FF)), _layer_spec(l, (SHARED_FF, D)),
                  _layer_spec(l, (1, D))],
        out_specs=pl.BlockSpec((tm, D), lambda i: (i, 0)),
        out_shape=jax.ShapeDtypeStruct((n_tok, D), F32),
        compiler_params=_cparams(("arbitrary",)),
        name="moe_post",
    )(x, mod3, hp, yg, comb, lw["w_sh_gu"], lw["w_sh_down"], lw["g_post2"])


def _moe_call(l, x, mod3, mod_row, lw):
    n_tok = x.shape[0]
    n_slots = -(-(TOP_K * n_tok + N_EXPERTS * (TMX - 1)) // TMX) * TMX
    assert n_slots // TMX <= LANES
    hp, eidx, rank, comb, cnt = _moe_pre_call(l, x, mod3, mod_row, lw)
    dest, te, tv, tn = _moe_plan_call(eidx, rank, cnt)
    dest = dest.reshape(TOP_K * n_tok)
    xs = _sc_scatter_rows(hp, dest, n_slots)
    ys = _experts_call(l, xs, te[0], tv[0], tn, lw["w_exp_gu"], lw["w_exp_down"])
    yg = _sc_gather_rows(ys, dest).reshape(TOP_K, n_tok, D // 2)
    return _moe_post_call(l, x, mod3, mod_row, hp, yg, comb, lw)


SC_CORES, SC_SUBCORES = 2, 16
SC_WORKERS = SC_CORES * SC_SUBCORES


def _sc_gather_rows(table, idx, chunk=64):
    n_out, width = idx.shape[0], table.shape[1]
    per_worker = n_out // SC_WORKERS
    n_chunks = per_worker // chunk
    assert per_worker * SC_WORKERS == n_out and n_chunks * chunk == per_worker
    mesh = plsc.VectorSubcoreMesh(core_axis_name="c", subcore_axis_name="s",
                                  num_cores=SC_CORES, num_subcores=SC_SUBCORES)

    @functools.partial(
        pl.kernel, mesh=mesh,
        out_type=jax.ShapeDtypeStruct((n_out, width), table.dtype),
        scratch_types=[pltpu.VMEM((chunk,), jnp.int32), pltpu.VMEM((chunk, width), table.dtype),
                       pltpu.SemaphoreType.DMA],
        name="sc_gather")
    def gather(table_hbm, idx_hbm, out_hbm, idx_v, rows_v, sem):
        base = (lax.axis_index("s") * SC_CORES + lax.axis_index("c")) * per_worker

        @pl.loop(0, n_chunks)
        def _(j):
            off = base + j * chunk
            pltpu.sync_copy(idx_hbm.at[pl.ds(off, chunk)], idx_v)
            pltpu.async_copy(table_hbm.at[idx_v], rows_v, sem).wait()
            pltpu.sync_copy(rows_v, out_hbm.at[pl.ds(off, chunk)])

    return gather(table, idx)


def _sc_scatter_rows(rows, dest, n_slots, chunk=64):
    n_tok, width = rows.shape
    per_worker = n_tok // SC_WORKERS
    n_chunks = per_worker // chunk
    assert per_worker * SC_WORKERS == n_tok and n_chunks * chunk == per_worker
    mesh = plsc.VectorSubcoreMesh(core_axis_name="c", subcore_axis_name="s",
                                  num_cores=SC_CORES, num_subcores=SC_SUBCORES)

    @functools.partial(
        pl.kernel, mesh=mesh,
        out_type=jax.ShapeDtypeStruct((n_slots, width), rows.dtype),
        scratch_types=[pltpu.VMEM((chunk,), jnp.int32), pltpu.VMEM((chunk, width), rows.dtype)],
        name="sc_scatter")
    def scatter(rows_hbm, dest_hbm, out_hbm, idx_v, rows_v):
        base = (lax.axis_index("s") * SC_CORES + lax.axis_index("c")) * per_worker

        @pl.loop(0, n_chunks)
        def _(j):
            off = base + j * chunk
            pltpu.sync_copy(rows_hbm.at[pl.ds(off, chunk)], rows_v)
            for k in range(TOP_K):
                pltpu.sync_copy(dest_hbm.at[pl.ds(k * n_tok + off, chunk)], idx_v)
                pltpu.sync_copy(rows_v, out_hbm.at[idx_v])

    return scatter(rows, dest)


def _rope_tables(t_len):
    pos = np.arange(t_len)
    row, col = pos // GRID_W, pos % GRID_W

    def tab(r):
        half = r // 2
        freq = ROPE_BASE ** (-np.arange(half, dtype=np.float64) / half)
        sign = np.concatenate([-np.ones(half), np.ones(half)])
        cs, sn = [], []
        for p in (row, col):
            ang = p[:, None].astype(np.float64) * freq[None, :]
            cs.append(np.concatenate([np.cos(ang), np.cos(ang)], axis=1))
            sn.append(np.concatenate([np.sin(ang), np.sin(ang)], axis=1) * sign[None, :])
        return np.concatenate(cs, axis=1), np.concatenate(sn, axis=1)

    c64, s64 = tab(GQA_HD // 2)
    cpe, spe = tab(MLA_ROPE // 2)
    out = (np.tile(c64, (1, 2)), np.tile(s64, (1, 2)), np.tile(cpe, (1, 4)), np.tile(spe, (1, 4)))
    return tuple(jnp.asarray(a, F32) for a in out)


def _prep_weights(p):
    n_l = p["w_in"].shape[0]

    def row(name):
        return p[name].reshape(n_l, 1, -1)

    w_uq = p["w_mla_uq"].reshape(n_l, MLA_Q_LORA, MLA_HEADS, MLA_NOPE + MLA_ROPE)
    w_uq = jnp.concatenate([w_uq[..., :MLA_NOPE].reshape(n_l, MLA_Q_LORA, -1),
                            w_uq[..., MLA_NOPE:].reshape(n_l, MLA_Q_LORA, -1)], axis=-1)
    w_ukv = p["w_mla_ukv"].reshape(n_l, MLA_KV_LORA, MLA_HEADS, MLA_NOPE + MLA_V)
    w_ukv = jnp.concatenate([w_ukv[..., :MLA_NOPE].reshape(n_l, MLA_KV_LORA, -1),
                             w_ukv[..., MLA_NOPE:].reshape(n_l, MLA_KV_LORA, -1)], axis=-1)
    w_br = p["w_br"]
    w_br_gqa = w_br[:, 1].reshape(n_l, GQA_HEADS, GQA_HD, D)[:, jnp.array(GQA_ORDER)]
    w_br = jnp.concatenate([w_br[:, 0:1], w_br_gqa.reshape(n_l, 1, BRANCH_W, D), w_br[:, 2:4]], axis=1)
    blk = np.arange(512) // GQA_HD
    return {
        "g_pre1": row("g_pre1"), "g_post1": row("g_post1"),
        "g_pre2": row("g_pre2"), "g_post2": row("g_post2"),
        "w_in": jnp.swapaxes(p["w_in"], 1, 2).astype(BF16),
        "g_mla_q": row("g_mla_q"), "w_uq": w_uq.astype(BF16),
        "g_mla_kv": row("g_mla_kv"), "w_ukv": w_ukv.astype(BF16),
        "g_gqa_q": jnp.tile(p["g_gqa_q"], (1, GQA_HEADS)).reshape(n_l, 1, -1),
        "g_gqa_k": jnp.tile(p["g_gqa_k"], (1, GQA_KV_HEADS)).reshape(n_l, 1, -1),
        "bd": jnp.asarray(blk[:, None] == blk[None, :], BF16),
        "ret_decay": p["ret_decay"],
        "g_ret": row("g_ret"),
        "diff_lambda": p["diff_lambda"], "g_diff": row("g_diff"),
        "w_br": w_br.astype(BF16), "w_out": p["w_out"].astype(BF16),
        "w_router_t": jnp.swapaxes(p["w_router"], 1, 2),
        "b_router": p["b_router"].reshape(n_l, N_EXPERTS, 1),
        "w_exp_gu": p["w_exp_gu"], "w_exp_down": p["w_exp_down"],
        "w_sh_gu": p["w_sh_gu"].astype(BF16), "w_sh_down": p["w_sh_down"].astype(BF16),
    }


def _mixers(latent, l, x, mod3, mod_row, lw, n_b, t_len, tabs=None, past=None, s0=None,
            prev_cache=None):
    lam_init = 0.8 - 0.6 * math.exp(-0.3 * l)
    outs = _inprep_call(latent, l, x, mod3, mod_row, lw, tabs, t_len,
                        None if prev_cache is None else prev_cache[:6])
    qm, kvm, gq, gkv, dq, dkv, ret, rg, gates = outs[:9]
    attn = _attn_call(l, lam_init, qm, kvm, gq, gkv, dq, dkv, lw["diff_lambda"], lw["g_diff"],
                      n_b, t_len, past)
    retn = _ret_call(latent, l, lw["ret_decay"], ret, rg, lw["g_ret"], s0, n_b, t_len,
                     None if prev_cache is None else prev_cache[6])
    mixed = _mixers_call(attn, retn, n_b, t_len // _query_block(t_len),
                         "mix_lat" if latent else "mix_ctx")
    br, r = mixed[0], mixed[1:]
    y = _merge_call(l, x, mod3, mod_row, br, r[0], gates, lw)
    cache = None if latent else tuple(outs[9:]) + (r[1],)
    return y, cache


def kernel(x_prompt, x_sample, cache_mla_ckv, cache_mla_kpe, cache_gqa_k, cache_gqa_v, cache_diff_k, cache_diff_v, state_ret, c, c_ctx, w_mod, b_mod, g_pre1, g_post1, g_pre2, g_post2, w_in, g_mla_q, w_mla_uq, g_mla_kv, w_mla_ukv, g_gqa_q, g_gqa_k, ret_decay, g_ret, diff_lambda, g_diff, w_br, w_out, w_router, b_router, w_exp_gu, w_exp_down, w_sh_gu, w_sh_down):
    params = dict(w_in=w_in, g_pre1=g_pre1, g_post1=g_post1, g_pre2=g_pre2,
                  g_post2=g_post2, g_mla_q=g_mla_q, w_mla_uq=w_mla_uq,
                  g_mla_kv=g_mla_kv, w_mla_ukv=w_mla_ukv, g_gqa_q=g_gqa_q, g_gqa_k=g_gqa_k,
                  ret_decay=ret_decay, g_ret=g_ret, diff_lambda=diff_lambda, g_diff=g_diff,
                  w_br=w_br, w_out=w_out, w_router=w_router, b_router=b_router,
                  w_exp_gu=w_exp_gu, w_exp_down=w_exp_down, w_sh_gu=w_sh_gu, w_sh_down=w_sh_down)
    n_bc, t_c, _ = x_prompt.shape
    n_bl, t_l, _ = x_sample.shape
    p_len = cache_mla_ckv.shape[2]
    tabs = _rope_tables(t_l)
    n_cond = 8
    cond = jnp.concatenate([c_ctx[None, :], c, jnp.zeros((n_cond - 1 - n_bl, D), F32)], axis=0)
    assert t_l % TM_MERGE == 0 and (t_c * n_bc) % TM_MERGE == 0
    assert t_l % TM_MOE_PRE == 0 and (t_c * n_bc) % TM_MOE_PRE == 0

    yp = x_prompt.reshape(n_bc * t_c, D)
    ys = x_sample.reshape(n_bl * t_l, D)
    cache = None
    lw = _prep_weights(params)
    for l in range(DEPTH):
        mod3 = _mod_call(l, cond, w_mod, b_mod).reshape(n_cond, 1, 6 * D)
        yp, cache = _mixers(False, l, yp, mod3, lambda i: 0, lw, n_bc, t_c, prev_cache=cache)
        yp = _moe_call(l, yp, mod3, lambda i: 0, lw)
        past_kvm = _pastkv_call(l, cache_mla_ckv[:, l].reshape(n_bl * p_len, -1),
                                jnp.tile(cache_mla_kpe[:, l].reshape(n_bl * p_len, -1), (1, 4)),
                                lw["w_ukv"])
        past_gkv = jnp.concatenate([cache_gqa_k[:, l].reshape(n_bl * p_len, -1),
                                    cache_gqa_v[:, l].reshape(n_bl * p_len, -1),
                                    jnp.ones((n_bl * p_len, LANES), F32)], axis=-1).astype(BF16)
        past_dv = jnp.concatenate([cache_diff_v[:, l], jnp.ones_like(cache_diff_v[:, l])], axis=-1)
        past_dkv = jnp.concatenate([cache_diff_k[:, l].reshape(n_bl * p_len, -1),
                                    past_dv.reshape(n_bl * p_len, -1)], axis=-1).astype(BF16)
        s0 = state_ret[:, l].reshape(n_bl, 2, RET_HEADS // 2, 2 * RET_DK, RET_DV)
        ys, _ = _mixers(True, l, ys, mod3, lambda t: 1 + t // t_l, lw, n_bl, t_l, tabs=tabs,
                        past=(past_kvm, past_gkv, past_dkv), s0=s0)
        ys = _moe_call(l, ys, mod3, lambda t: 1 + t // t_l, lw)

    ckv, kpe, gk_t, gv_t, dk_t, dv, ret_state = cache

    def time_minor(a, shape):
        a = a.reshape((n_bc, DEPTH) + shape + (t_c,))
        return jnp.transpose(a, (0, 1, a.ndim - 1) + tuple(range(2, a.ndim - 1)))

    return (yp.reshape(n_bc, t_c, D), ys.reshape(n_bl, t_l, D), ckv, kpe,
            time_minor(gk_t, (GQA_KV_HEADS, GQA_HD)), time_minor(gv_t, (GQA_KV_HEADS, GQA_HD)),
            time_minor(dk_t, (DIFF_HEADS, 2, DIFF_D)),
            dv.reshape(n_bc, DEPTH, t_c, DIFF_HEADS, DIFF_DV), ret_state)
```

```python
import functools
import math

import numpy as np
import jax
import jax.numpy as jnp
from jax import lax
from jax.experimental import pallas as pl
from jax.experimental.pallas import tpu as pltpu
from jax.experimental.pallas import tpu_sc as plsc

F32 = jnp.float32
BF16 = jnp.bfloat16

D = 1024
DEPTH = 2
GRID_W = 64
ROPE_BASE = 10000.0
EPS = 1e-6

MLA_HEADS, MLA_NOPE, MLA_ROPE, MLA_V = 8, 64, 32, 64
MLA_Q_LORA, MLA_KV_LORA = 384, 256
GQA_HEADS, GQA_KV_HEADS, GQA_HD = 8, 2, 64
RET_HEADS, RET_DK, RET_DV = 4, 64, 128
DIFF_HEADS, DIFF_D, DIFF_DV = 4, 64, 128
N_BRANCH, BRANCH_W = 4, 512
N_EXPERTS, TOP_K, N_GROUPS, TOPK_GROUPS = 32, 4, 4, 2
EXPERT_FF, SHARED_FF = 256, 256
ROUTE_SCALE = 2.5
GROUP_SIZE = N_EXPERTS // N_GROUPS

LANES = 128
HALF_LANES = 64
VMEM_LIMIT = 56 * 1024 * 1024

O_CQ, O_CKV, O_KPE, O_GQ, O_GK, O_GV = 0, 384, 640, 672, 1184, 1312
O_RQ, O_RK, O_RV, O_RG, O_DQ, O_DK, O_DV, O_GL, O_END = (
    1440, 1696, 1952, 2464, 2976, 3488, 4000, 4512, 8608)
GQA_ORDER = (0, 4, 1, 5, 2, 6, 3, 7)

KVM_W = 8 * 256
GKV_W = 3 * LANES
DKV_W = 512 + 4 * 256
LOG2E = 1.4426950408889634
TM = 256
TM_LAT = 512
TM_MERGE = 512
TQ = 256
TM_MOE_PRE = 1024
TMX = 256
W_SLOTS = 4


def _cparams(sem):
    return pltpu.CompilerParams(dimension_semantics=sem, vmem_limit_bytes=VMEM_LIMIT)


def _const_spec(shape):
    nd = len(shape)
    return pl.BlockSpec(shape, lambda *_: (0,) * nd)


def _layer_spec(l, shape):
    nd = len(shape)
    return pl.BlockSpec((None,) + tuple(shape), lambda *_: (l,) + (0,) * nd)


def _rms(x, g):
    return x * lax.rsqrt(jnp.mean(x * x, axis=-1, keepdims=True) + EPS) * g


def _dot(a, b):
    return jnp.dot(a, b, preferred_element_type=F32)


def _dot_nt(a, b):
    return lax.dot_general(a, b, (((1,), (1,)), ((), ())), preferred_element_type=F32)


def _silu(x):
    return x * jax.nn.sigmoid(x)


def _lane_iota(shape):
    return lax.broadcasted_iota(jnp.int32, shape, len(shape) - 1)


def _seg_meansq(x, bd_ref, width):
    sq = x * x
    hi = sq.astype(BF16)
    lo = (sq - hi.astype(F32)).astype(BF16)
    bd = bd_ref[0:width, 0:width]
    return (_dot(hi, bd) + _dot(lo, bd)) * (1.0 / GQA_HD)


def _rope(x, cos, sin_signed, half):
    width = x.shape[-1]
    first = (_lane_iota(x.shape) % (2 * half)) < half
    partner = jnp.where(first, pltpu.roll(x, width - half, 1), pltpu.roll(x, half, 1))
    return x * cos + partner * sin_signed


def _tile_lanes(t, reps):
    return t if reps == 1 else jnp.concatenate([t] * reps, axis=1)


def _put_layer(o_ref, val, whole_stack):
    if whole_stack:
        o_ref[0] = val
        for k in range(1, o_ref.shape[0]):
            o_ref[k] = jnp.zeros_like(val)
    else:
        o_ref[...] = val


def _store_kvm(kvm_ref, kv, kpe_b):
    ones = jnp.ones(kpe_b.shape, BF16)
    for p in range(4):
        kvm_ref[:, p * 256:p * 256 + LANES] = kv[:, p * LANES:(p + 1) * LANES].astype(BF16)
        kvm_ref[:, p * 256 + LANES:(p + 1) * 256] = kpe_b
        kvm_ref[:, 1024 + p * 256:1024 + p * 256 + LANES] = (
            kv[:, 512 + p * LANES:512 + (p + 1) * LANES].astype(BF16))
        kvm_ref[:, 1024 + p * 256 + LANES:1024 + (p + 1) * 256] = ones


def _mod_kernel(c_ref, w_ref, b_ref, o_ref):
    a = _silu(c_ref[...]).astype(BF16)
    o_ref[...] = _dot(a, w_ref[...].astype(BF16)) + b_ref[...]


def _mod_call(l, cond, w_mod, b_mod):
    n_l, _, n = w_mod.shape
    tn = 1536
    return pl.pallas_call(
        _mod_kernel,
        grid=(n // tn,),
        in_specs=[_const_spec(cond.shape),
                  pl.BlockSpec((None, D, tn), lambda j: (l, 0, j)),
                  pl.BlockSpec((None, 1, tn), lambda j: (l, 0, j))],
        out_specs=pl.BlockSpec((cond.shape[0], tn), lambda j: (0, j)),
        out_shape=jax.ShapeDtypeStruct((cond.shape[0], n), F32),
        compiler_params=_cparams(("arbitrary",)),
        name="mod",
    )(cond, w_mod, b_mod.reshape(n_l, 1, n))


def _inprep_kernel(latent, n_aliased, *refs):
    (x_ref, mod_ref, gpre_ref, win_ref, gmq_ref, wuq_ref, gmkv_ref, wukv_ref,
     ggq_ref, ggk_ref, bd_ref) = refs[:11]
    refs = refs[11:]
    if latent:
        cos64_ref, sin64_ref, cospe_ref, sinpe_ref = refs[:4]
        refs = refs[4:]
    refs = refs[n_aliased:]
    first_layer = n_aliased == 0
    (qm_ref, kvm_ref, gqo_ref, gkv_ref, dqo_ref, dkv_ref, ret_ref, rg_ref, gate_ref) = refs[:9]
    refs = refs[9:]
    if not latent:
        ckv_o, kpe_o, gk_o, gv_o, dk_o, dv_o = refs

    x = x_ref[...]
    mod = mod_ref[...]
    sh1 = mod[:, 0:D]
    sc1 = mod[:, D:2 * D]
    hb = (_rms(x, gpre_ref[...]) * (1.0 + sc1) + sh1).astype(BF16)

    def z(a, b):
        return _dot_nt(hb, win_ref[a:b, :])

    if latent:
        cos64, sin64 = cos64_ref[...], sin64_ref[...]
        cospe, sinpe = cospe_ref[...], sinpe_ref[...]

    cqn = _rms(z(O_CQ, O_CKV), gmq_ref[...]).astype(BF16)
    q = _dot(cqn, wuq_ref[...]) * ((MLA_NOPE + MLA_ROPE) ** -0.5 * LOG2E)
    q_nope, q_pe = q[:, 0:512], q[:, 512:768]
    if latent:
        q_pe = _rope(q_pe, _tile_lanes(cospe, 2), _tile_lanes(sinpe, 2), MLA_ROPE // 4)
    qm_ref[:, 0:512] = q_nope.astype(BF16)
    qm_ref[:, 512:768] = q_pe.astype(BF16)

    ckvn = _rms(z(O_CKV, O_KPE), gmkv_ref[...])
    kv = _dot(ckvn.astype(BF16), wukv_ref[...])
    kpe4 = _dot_nt(hb, jnp.concatenate([win_ref[O_KPE:O_GQ, :]] * 4, axis=0))
    if latent:
        kpe4 = _rope(kpe4, cospe, sinpe, MLA_ROPE // 4)
    else:
        _put_layer(ckv_o, ckvn, first_layer)
        _put_layer(kpe_o, kpe4[:, 0:MLA_ROPE], first_layer)
    _store_kvm(kvm_ref, kv, kpe4.astype(BF16))

    gq = _dot_nt(hb, jnp.concatenate(
        [win_ref[O_GQ + h * GQA_HD:O_GQ + (h + 1) * GQA_HD, :] for h in GQA_ORDER], axis=0))
    gq = gq * lax.rsqrt(_seg_meansq(gq, bd_ref, 512) + EPS) * ggq_ref[...]
    gk = z(O_GK, O_GV)
    gk = gk * lax.rsqrt(_seg_meansq(gk, bd_ref, LANES) + EPS) * ggk_ref[...]
    gv = z(O_GV, O_RQ)
    if latent:
        gq = _rope(gq, _tile_lanes(cos64, 4), _tile_lanes(sin64, 4), GQA_HD // 4)
        gk = _rope(gk, cos64, sin64, GQA_HD // 4)
    else:
        _put_layer(gk_o, jnp.transpose(gk), first_layer)
        _put_layer(gv_o, jnp.transpose(gv), first_layer)
    gqo_ref[...] = (gq * (GQA_HD ** -0.5 * LOG2E)).astype(BF16)
    gkv_ref[:, 0:LANES] = gk.astype(BF16)
    gkv_ref[:, LANES:2 * LANES] = gv.astype(BF16)
    gkv_ref[:, 2 * LANES:3 * LANES] = jnp.ones(gv.shape, BF16)

    dq = z(O_DQ, O_DK)
    dk = z(O_DK, O_DV)
    dv = z(O_DV, O_GL)
    if latent:
        dq = _rope(dq, _tile_lanes(cos64, 4), _tile_lanes(sin64, 4), DIFF_D // 4)
        dk = _rope(dk, _tile_lanes(cos64, 4), _tile_lanes(sin64, 4), DIFF_D // 4)
    else:
        _put_layer(dk_o, jnp.transpose(dk), first_layer)
        _put_layer(dv_o, dv, first_layer)
    dqo_ref[...] = (dq * (DIFF_D ** -0.5 * LOG2E)).astype(BF16)
    dkv_ref[:, 0:512] = dk.astype(BF16)
    for h in range(DIFF_HEADS):
        dkv_ref[:, 512 + h * 256:512 + h * 256 + LANES] = dv[:, h * LANES:(h + 1) * LANES].astype(BF16)
        dkv_ref[:, 512 + h * 256 + LANES:512 + (h + 1) * 256] = jnp.ones((dv.shape[0], LANES), BF16)

    ret_ref[:, 0:256] = z(O_RQ, O_RK).astype(BF16)
    ret_ref[:, 256:512] = (z(O_RK, O_RV) * (RET_DK ** -0.5)).astype(BF16)
    ret_ref[:, 512:1024] = z(O_RV, O_RG).astype(BF16)
    rg_ref[...] = z(O_RG, O_DQ).astype(BF16)

    for n in range(N_BRANCH):
        gate_ref[:, n * D:(n + 1) * D] = jax.nn.sigmoid(
            z(O_GL + n * D, O_GL + (n + 1) * D)).astype(BF16)


def _inprep_call(latent, l, x, mod3, mod_row, lw, tabs, t_len, prev_caches=None):
    n_tok = x.shape[0]
    tm = TM_LAT if latent else TM
    nblk = n_tok // tm
    blk_per_seq = t_len // tm

    def tok(w):
        return pl.BlockSpec((tm, w), lambda i: (i, 0))

    in_specs = [tok(D),
                pl.BlockSpec((None, 1, 6 * D), lambda i: (mod_row(i * tm), 0, 0)),
                _layer_spec(l, (1, D)),
                pl.BlockSpec((None, O_END, D), lambda i: (l, 0, 0), pipeline_mode=pl.Buffered(1)),
                _layer_spec(l, (1, MLA_Q_LORA)), _layer_spec(l, (MLA_Q_LORA, 768)),
                _layer_spec(l, (1, MLA_KV_LORA)), _layer_spec(l, (MLA_KV_LORA, 1024)),
                _layer_spec(l, (1, 512)), _layer_spec(l, (1, LANES)), _const_spec((512, 512))]
    args = [x, mod3, lw["g_pre1"], lw["w_in"], lw["g_mla_q"], lw["w_uq"], lw["g_mla_kv"],
            lw["w_ukv"], lw["g_gqa_q"], lw["g_gqa_k"], lw["bd"]]
    if latent:
        tab_spec = pl.BlockSpec((tm, LANES), lambda i: (i % blk_per_seq, 0))
        in_specs += [tab_spec] * 4
        args += list(tabs)
    widths = [768, KVM_W, 512, GKV_W, 512, DKV_W, 1024, 512, 4 * D]
    out_specs = [tok(w) for w in widths]
    out_shape = [jax.ShapeDtypeStruct((n_tok, w), BF16) for w in widths]
    aliases = {}
    if not latent:
        assert TM == t_len and (prev_caches is None) == (l == 0)
        n_seq = n_tok // t_len
        lead, at = ((DEPTH,), 0) if l == 0 else ((None,), l)

        def row_major(w):
            out_specs.append(pl.BlockSpec((None,) + lead + (t_len, w), lambda i: (i, at, 0, 0)))
            out_shape.append(jax.ShapeDtypeStruct((n_seq, DEPTH, t_len, w), F32))

        row_major(MLA_KV_LORA)
        row_major(MLA_ROPE)
        for w in (LANES, LANES, 512):
            out_specs.append(pl.BlockSpec((None,) + lead + (w, t_len), lambda i: (i, at, 0, 0)))
            out_shape.append(jax.ShapeDtypeStruct((n_seq, DEPTH, w, t_len), F32))
        row_major(512)
        if prev_caches is not None:
            n_in = len(args)
            in_specs += [pl.BlockSpec(memory_space=pl.ANY)] * len(prev_caches)
            args += list(prev_caches)
            aliases = {n_in + k: len(widths) + k for k in range(len(prev_caches))}
    return pl.pallas_call(
        functools.partial(_inprep_kernel, latent, len(aliases)),
        grid=(nblk,),
        in_specs=in_specs, out_specs=out_specs, out_shape=out_shape,
        input_output_aliases=aliases,
        compiler_params=_cparams(("arbitrary",)),
        name="inprep_lat" if latent else "inprep_ctx",
    )(*args)


def _pastkv_kernel(ckv_ref, kpe_ref, wukv_ref, o_ref):
    kv = _dot(ckv_ref[...].astype(BF16), wukv_ref[...])
    _store_kvm(o_ref, kv, kpe_ref[...].astype(BF16))


def _pastkv_call(l, ckv, kpe4, w_ukv):
    n = ckv.shape[0]
    return pl.pallas_call(
        _pastkv_kernel,
        grid=(n // TM,),
        in_specs=[pl.BlockSpec((TM, MLA_KV_LORA), lambda i: (i, 0)),
                  pl.BlockSpec((TM, LANES), lambda i: (i, 0)),
                  _layer_spec(l, (MLA_KV_LORA, 1024))],
        out_specs=pl.BlockSpec((TM, KVM_W), lambda i: (i, 0)),
        out_shape=jax.ShapeDtypeStruct((n, KVM_W), BF16),
        compiler_params=_cparams(("arbitrary",)),
        name="pastkv",
    )(ckv, kpe4, w_ukv)


def _softmax_pv(s, v_ones):
    m = jnp.max(s, axis=-1, keepdims=True)
    p = jnp.exp2(s - m).astype(BF16)
    o = _dot(p, v_ones)
    return o[:, 0:LANES] / o[:, LANES:2 * LANES]


def _attn_kernel(lam_init, n_past, qm_ref, kvm_ref, gq_ref, gkv_ref, dq_ref, dkv_ref, *refs):
    if n_past:
        past_refs, refs = refs[:3], refs[3:]
        lam_ref, gdiff_ref, o_ref = refs[:3]
        joined = refs[3:]

        @pl.when(pl.program_id(1) == 0)
        def _():
            for dst, past, new in zip(joined, past_refs, (kvm_ref, gkv_ref, dkv_ref)):
                dst[0:n_past, :] = past[...]
                dst[n_past:, :] = new[...]

        kvm_ref, gkv_ref, dkv_ref = joined
    else:
        lam_ref, gdiff_ref, o_ref = refs
    tq = qm_ref.shape[0]
    lane = _lane_iota((tq, LANES))
    low = lane < HALF_LANES
    zero = jnp.zeros((tq, LANES), BF16)

    for p in range(MLA_HEADS // 2):
        qn = qm_ref[:, p * LANES:(p + 1) * LANES]
        g = p // 2
        qpe = qm_ref[:, 512 + g * LANES:512 + (g + 1) * LANES]
        kk = kvm_ref[:, p * 256:(p + 1) * 256]
        vv = kvm_ref[:, 1024 + p * 256:1024 + (p + 1) * 256]
        outs = []
        for half in range(2):
            h = 2 * p + half
            slot = h % 4
            in_slot = (lane >= slot * MLA_ROPE) & (lane < (slot + 1) * MLA_ROPE)
            lhs = jnp.concatenate(
                [jnp.where(low if half == 0 else ~low, qn, zero),
                 jnp.where(in_slot, qpe, zero)], axis=1)
            outs.append(_softmax_pv(_dot_nt(lhs, kk), vv))
        o_ref[:, p * LANES:(p + 1) * LANES] = jnp.where(low, outs[0], outs[1]).astype(BF16)

    kk = gkv_ref[:, 0:LANES]
    vv = gkv_ref[:, LANES:3 * LANES]
    for g in range(GQA_HEADS // 2):
        qg = gq_ref[:, g * LANES:(g + 1) * LANES]
        o_lo = _softmax_pv(_dot_nt(jnp.where(low, qg, zero), kk), vv)
        o_hi = _softmax_pv(_dot_nt(jnp.where(low, zero, qg), kk), vv)
        o_ref[:, 512 + g * LANES:512 + (g + 1) * LANES] = jnp.where(low, o_lo, o_hi).astype(BF16)

    lp = lam_ref[...]
    lam = (jnp.exp(jnp.sum(lp[0:1] * lp[1:2], axis=-1, keepdims=True))
           - jnp.exp(jnp.sum(lp[2:3] * lp[3:4], axis=-1, keepdims=True)) + lam_init)
    for h in range(DIFF_HEADS):
        qh = dq_ref[:, h * LANES:(h + 1) * LANES]
        kk = dkv_ref[:, h * LANES:(h + 1) * LANES]
        vv = dkv_ref[:, 512 + h * 256:512 + (h + 1) * 256]
        a1 =_softmax_pv(_dot_nt(jnp.where(low, qh, zero), kk), vv)
        a2 = _softmax_pv(_dot_nt(jnp.where(low, zero, qh), kk), vv)
        od = _rms(a1 - lam * a2, gdiff_ref[...]) * (1.0 - lam_init)
        o_ref[:, 1024 + h * LANES:1024 + (h + 1) * LANES] = od.astype(BF16)


def _attn_call(l, lam_init, qm, kvm, gq, gkv, dq, dkv, lam_p, g_diff, n_b, t_len, past=None):
    nq = t_len // TQ
    n_past = 0 if past is None else past[0].shape[0] // n_b

    def qspec(w):
        return pl.BlockSpec((TQ, w), lambda b, i: (b * nq + i, 0))

    def kspec(w, rows=t_len):
        return pl.BlockSpec((rows, w), lambda b, i: (b, 0))

    in_specs = [qspec(768), kspec(KVM_W), qspec(512), kspec(GKV_W), qspec(512), kspec(DKV_W)]
    args = [qm, kvm, gq, gkv, dq, dkv]
    scratch = []
    if n_past:
        in_specs += [kspec(KVM_W, n_past), kspec(GKV_W, n_past), kspec(DKV_W, n_past)]
        args += list(past)
        scratch = [pltpu.VMEM((n_past + t_len, w), BF16) for w in (KVM_W, GKV_W, DKV_W)]
    in_specs += [_layer_spec(l, (4, DIFF_D)), _layer_spec(l, (1, DIFF_DV))]
    args += [lam_p, g_diff]
    return dict(kernel=functools.partial(_attn_kernel, lam_init, n_past), in_specs=in_specs, args=args,
                out_specs=[qspec(3 * BRANCH_W)],
                out_shape=[jax.ShapeDtypeStruct((n_b * t_len, 3 * BRANCH_W), BF16)],
                scratch=scratch, aliases={})


def _log_sigmoid(x):
    return jnp.minimum(x, 0.0) - jnp.log(1.0 + jnp.exp(-jnp.abs(x)))


def _log_gamma(dec_ref, l, d, h):
    return _log_sigmoid(jnp.full((1, 1), dec_ref[l, d, h], F32))


def _ret_kernel(latent, l, t_len, dec_ref, q_ref, k_ref, v_ref, rg_ref, gret_ref, *refs):
    if latent:
        s0_ref, o_ref = refs
    else:
        o_ref, st_ref = refs[-2:]
    tq = q_ref.shape[0]
    t0 = pl.program_id(1) * tq
    lane = _lane_iota((tq, LANES))
    low = lane < HALF_LANES
    zero = jnp.zeros((tq, LANES), BF16)
    t_idx = (t0 + lax.broadcasted_iota(jnp.int32, (tq, t_len), 0)).astype(F32)
    s_idx = lax.broadcasted_iota(jnp.int32, (tq, t_len), 1).astype(F32)
    dist = t_idx - s_idx
    past = dist >= 0
    diag = jnp.where(dist == 0, 1.0, 0.0)
    t_col = (t0 + lax.broadcasted_iota(jnp.int32, (tq, 1), 0)).astype(F32)

    def lg(d, h):
        return _log_gamma(dec_ref, l, d, h)

    for h in range(RET_HEADS):
        p, half = h // 2, h % 2
        qp = q_ref[:, p * LANES:(p + 1) * LANES]
        qm = jnp.where(low if half == 0 else ~low, qp, zero)
        kp = k_ref[:, p * LANES:(p + 1) * LANES]
        vh = v_ref[:, h * LANES:(h + 1) * LANES]
        lgf, lgb = lg(0, h), lg(1, h)
        dmask = jnp.exp(jnp.where(past, lgf, -lgb) * dist) + diag
        o = _dot((_dot_nt(qm, kp) * dmask).astype(BF16), vh)
        if latent:
            sf =s0_ref[0, p].astype(BF16)
            sb = s0_ref[1, p].astype(BF16)
            o = o + _dot(qm, sf) * jnp.exp(lgf * (t_col + 1.0))
            o = o + _dot(qm, sb) * jnp.exp(lgb * (float(t_len) - t_col))
        mu = jnp.mean(o, axis=-1, keepdims=True)
        oc = o - mu
        y = oc * lax.rsqrt(jnp.mean(oc * oc, axis=-1, keepdims=True) + EPS)
        y = y * gret_ref[:, h * LANES:(h + 1) * LANES]
        rg = rg_ref[:, h * LANES:(h + 1) * LANES].astype(F32)
        o_ref[:, h * LANES:(h + 1) * LANES] = (y * _silu(rg)).astype(BF16)

    if not latent:
        s_col = lax.broadcasted_iota(jnp.int32, (t_len, 1), 0).astype(F32)
        lane_t = _lane_iota((1, LANES)) < HALF_LANES
        for p in range(RET_HEADS // 2):
            kp = k_ref[:, p * LANES:(p + 1) * LANES].astype(F32)
            for d in range(2):
                lg_lane = jnp.where(lane_t, lg(d, 2 * p), lg(d, 2 * p + 1))
                expo = (float(t_len) - 1.0 - s_col) if d == 0 else s_col
                kdec_t = jnp.transpose(kp * jnp.exp(lg_lane * expo)).astype(BF16)
                for half in range(2):
                    h = 2 * p + half
                    st = _dot(kdec_t, v_ref[:, h * LANES:(h + 1) * LANES])
                    st = st[half * RET_DK:(half + 1) * RET_DK, :]
                    if l == 0:
                        st_ref[0, d, h] = st
                        for k in range(1, st_ref.shape[0]):
                            st_ref[k, d, h] = jnp.zeros_like(st)
                    else:
                        st_ref[d, h] = st


def _ret_call(latent, l, dec, ret, rg, g_ret, s0, n_b, t_len, prev_state=None):
    nq = t_len // TQ
    aliases = {}
    assert latent or nq == 1
    in_specs = [pl.BlockSpec(memory_space=pltpu.SMEM),
                pl.BlockSpec((TQ, 256), lambda b, i: (b * nq + i, 0)),
                pl.BlockSpec((t_len, 256), lambda b, i: (b, 1)),
                pl.BlockSpec((t_len, 512), lambda b, i: (b, 1)),
                pl.BlockSpec((TQ, 512), lambda b, i: (b * nq + i, 0)),
                _layer_spec(l, (1, 512))]
    args = [dec, ret, ret, ret, rg, g_ret]
    out_specs = [pl.BlockSpec((TQ, 512), lambda b, i: (b * nq + i, 0))]
    out_shape = [jax.ShapeDtypeStruct((n_b * t_len, 512), BF16)]
    if latent:
        in_specs.append(pl.BlockSpec((None, 2, 2, LANES, LANES), lambda b, i: (b, 0, 0, 0, 0)))
        args.append(s0)
    else:
        assert (prev_state is None) == (l == 0)
        lead, at = ((DEPTH,), 0) if l == 0 else ((None,), l)
        out_specs.append(pl.BlockSpec((None,) + lead + (2, RET_HEADS, RET_DK, RET_DV),
                                      lambda b, i: (b, at, 0, 0, 0, 0)))
        out_shape.append(jax.ShapeDtypeStruct((n_b, DEPTH, 2, RET_HEADS, RET_DK, RET_DV), F32))
        if prev_state is not None:
            aliases = {len(args): 1}
            in_specs.append(pl.BlockSpec(memory_space=pl.ANY))
            args.append(prev_state)
    return dict(kernel=functools.partial(_ret_kernel, latent, l, t_len), in_specs=in_specs, args=args,
                out_specs=out_specs, out_shape=out_shape, scratch=[], aliases=aliases)


def _mixers_call(attn, ret, n_b, nq, name):
    n_ai, n_ri = len(attn["args"]), len(ret["args"])
    n_ao, n_ro = len(attn["out_specs"]), len(ret["out_specs"])

    def kernel(*refs):
        a_in, r_in = refs[:n_ai], refs[n_ai:n_ai + n_ri]
        outs = refs[n_ai + n_ri:n_ai + n_ri + n_ao + n_ro]
        scratch = refs[n_ai + n_ri + n_ao + n_ro:]
        attn["kernel"](*a_in, *outs[:n_ao], *scratch)
        ret["kernel"](*r_in, *outs[n_ao:])

    aliases = {n_ai + k: n_ao + v for k, v in ret["aliases"].items()}
    return pl.pallas_call(
        kernel,
        grid=(n_b, nq),
        in_specs=attn["in_specs"] + ret["in_specs"],
        out_specs=attn["out_specs"] + ret["out_specs"],
        out_shape=attn["out_shape"] + ret["out_shape"],
        scratch_shapes=attn["scratch"],
        input_output_aliases=aliases,
        compiler_params=_cparams(("arbitrary", "arbitrary")),
        name=name,
    )(*attn["args"], *ret["args"])


def _merge_kernel(x_ref, mod_ref, br_ref, or_ref, gate_ref, wbr_ref, wout_ref, gpost_ref, o_ref):
    merged = None
    for n in range(N_BRANCH):
        if n < 2:
            b = br_ref[:, n * BRANCH_W:(n + 1) * BRANCH_W]
        elif n == 2:
            b = or_ref[...]
        else:
            b = br_ref[:, 2 * BRANCH_W:3 * BRANCH_W]
        t = gate_ref[:, n * D:(n + 1) * D].astype(F32) * _dot(b, wbr_ref[n])
        merged = t if merged is None else merged + t
    out = _dot(merged.astype(BF16), wout_ref[...])
    g1 = mod_ref[...][:, 2 * D:3 * D]
    o_ref[...] = x_ref[...] + g1 * _rms(out, gpost_ref[...])


def _merge_call(l, x, mod3, mod_row, br, o_r, gates, lw):
    n_tok = x.shape[0]

    def tok(w):
        return pl.BlockSpec((TM_MERGE, w), lambda i: (i, 0))

    return pl.pallas_call(
        _merge_kernel,
        grid=(n_tok // TM_MERGE,),
        in_specs=[tok(D), pl.BlockSpec((None, 1, 6 * D), lambda i: (mod_row(i * TM_MERGE), 0, 0)),
                  tok(3 * BRANCH_W), tok(BRANCH_W), tok(4 * D),
                  _layer_spec(l, (N_BRANCH, BRANCH_W, D)), _layer_spec(l, (D, D)),
                  _layer_spec(l, (1, D))],
        out_specs=tok(D),
        out_shape=jax.ShapeDtypeStruct((n_tok, D), F32),
        compiler_params=_cparams(("arbitrary",)),
        name="merge",
    )(x, mod3, br, o_r, gates, lw["w_br"], lw["w_out"], lw["g_post1"])


def _route(logits_t, bias):
    n = logits_t.shape[1]
    scores = jax.nn.sigmoid(logits_t)
    sel = scores + bias
    neg = -jnp.inf
    sub = lax.broadcasted_iota(jnp.int32, (GROUP_SIZE, n), 0)
    grp = []
    for g in range(N_GROUPS):
        blk = sel[g * GROUP_SIZE:(g + 1) * GROUP_SIZE]
        m1 = jnp.max(blk, axis=0, keepdims=True)
        i1 = jnp.min(jnp.where(blk == m1, sub, GROUP_SIZE), axis=0, keepdims=True)
        m2 = jnp.max(jnp.where(sub == i1, neg, blk), axis=0, keepdims=True)
        grp.append(m1 + m2)
    parts = []
    for g in range(N_GROUPS):
        beaten = jnp.zeros((1, n), jnp.int32)
        for o in range(N_GROUPS):
            if o == g:
                continue
            wins = (grp[o] > grp[g]) | (grp[o] == grp[g]) if o < g else (grp[o] > grp[g])
            beaten = beaten + wins.astype(jnp.int32)
        keep = beaten < TOPK_GROUPS
        parts.append(jnp.where(keep, sel[g * GROUP_SIZE:(g + 1) * GROUP_SIZE], neg))
    cur = jnp.concatenate(parts, axis=0)
    eidx = lax.broadcasted_iota(jnp.int32, (N_EXPERTS, n), 0)
    hits, ids, ws = [], [], []
    for _ in range(TOP_K):
        m = jnp.max(cur, axis=0, keepdims=True)
        i = jnp.min(jnp.where(cur == m, eidx, N_EXPERTS), axis=0, keepdims=True)
        hit = eidx == i
        hits.append(hit)
        ids.append(i)
        ws.append(jnp.sum(jnp.where(hit, scores, 0.0), axis=0, keepdims=True))
        cur = jnp.where(hit, neg, cur)
    wsum = ws[0] + ws[1] + ws[2] + ws[3]
    return hits, ids, [w / wsum * ROUTE_SCALE for w in ws]


U32 = jnp.uint32
HIGH16 = np.uint32(0xFFFF0000)


def _bf16_bits(v):
    return lax.bitcast_convert_type(v.astype(BF16).astype(F32), U32)


def _pack_rows(v):
    return (_bf16_bits(v[:, 0:D // 2]) >> 16) | _bf16_bits(v[:, D // 2:D])


def _unpack_rows(p):
    lo = lax.bitcast_convert_type(p << 16, F32)
    hi = lax.bitcast_convert_type(p & HIGH16, F32)
    return jnp.concatenate([lo, hi], axis=1)


def _moe_pre_kernel(x_ref, mod_ref, gpre_ref, wr_ref, br_ref, tri_ref,
                    hp_ref, comb_ref, dest_ref, te_ref, tv_ref, tn_ref, run_ref, eidx_ref, rank_ref):
    tm = x_ref.shape[0]
    step = pl.program_id(0)
    cols = pl.ds(pl.multiple_of(step * tm, tm), tm)

    @pl.when(pl.program_id(0) == 0)
    def _():
        run_ref[...] = jnp.zeros_like(run_ref)

    mod = mod_ref[...]
    sh2, sc2 = mod[:, 3 * D:4 * D], mod[:, 4 * D:5 * D]
    h = _rms(x_ref[...], gpre_ref[...]) * (1.0 + sc2) + sh2
    hp_ref[...] = _pack_rows(h)
    hb = h.astype(BF16)
    h_lo = (h - hb.astype(F32)).astype(BF16)
    wr = wr_ref[...]
    wr_hi = wr.astype(BF16)
    wr_lo = (wr - wr_hi.astype(F32)).astype(BF16)
    logits_t = _dot_nt(wr_hi, hb) + _dot_nt(wr_hi, h_lo) + _dot_nt(wr_lo, hb)
    hits, ids, ws = _route(logits_t, br_ref[...])

    picked = jnp.zeros((N_EXPERTS, tm), F32)
    for hit in hits:
        picked = jnp.where(hit, 1.0, picked)
    before = _dot(picked.astype(BF16), tri_ref[...]) + run_ref[:, 0:1]
    sub8 = lax.broadcasted_iota(jnp.int32, (8, tm), 0)
    comb8 = jnp.zeros((8, tm), F32)
    for k in range(TOP_K):
        rank = jnp.sum(jnp.where(hits[k], before, 0.0), axis=0, keepdims=True)
        eidx_ref[k:k + 1, cols] = ids[k]
        rank_ref[k:k + 1, cols] = rank.astype(jnp.int32)
        comb8 = jnp.where(sub8 == k, ws[k], comb8)
    comb_ref[...] = jnp.transpose(
        jnp.concatenate([comb8, jnp.zeros((LANES - 8, tm), F32)], axis=0))
    run_ref[...] = run_ref[...] + jnp.sum(picked, axis=1, keepdims=True)

    @pl.when(step == pl.num_programs(0) - 1)
    def _():
        _moe_plan(eidx_ref, rank_ref, run_ref, dest_ref, te_ref, tv_ref, tn_ref)


def _moe_pre_call(l, x, mod3, mod_row, lw):
    n_tok = x.shape[0]
    tm = TM_MOE_PRE
    tri = np.arange(tm)
    tri = jnp.asarray(tri[:, None] < tri[None, :], BF16)
    tiles = jax.ShapeDtypeStruct((1, LANES), jnp.int32)
    return pl.pallas_call(
        _moe_pre_kernel,
        grid=(n_tok // tm,),
        in_specs=[pl.BlockSpec((tm, D), lambda i: (i, 0)),
                  pl.BlockSpec((None, 1, 6 * D), lambda i: (mod_row(i * tm), 0, 0)),
                  _layer_spec(l, (1, D)), _layer_spec(l, (N_EXPERTS, D)),
                  _layer_spec(l, (N_EXPERTS, 1)), _const_spec((tm, tm))],
        out_specs=[pl.BlockSpec((tm, D // 2), lambda i: (i, 0)),
                   pl.BlockSpec((tm, LANES), lambda i: (i, 0)),
                   _const_spec((TOP_K, n_tok)), _const_spec((1, LANES)), _const_spec((1, LANES)),
                   _const_spec((8, LANES))],
        out_shape=[jax.ShapeDtypeStruct((n_tok, D // 2), U32),
                   jax.ShapeDtypeStruct((n_tok, LANES), F32),
                   jax.ShapeDtypeStruct((TOP_K, n_tok), jnp.int32), tiles, tiles,
                   jax.ShapeDtypeStruct((8, LANES), jnp.int32)],
        scratch_shapes=[pltpu.VMEM((N_EXPERTS, LANES), F32),
                        pltpu.VMEM((TOP_K, n_tok), jnp.int32), pltpu.VMEM((TOP_K, n_tok), jnp.int32)],
        compiler_params=_cparams(("arbitrary",)),
        name="moe_pre",
    )(x, mod3, lw["g_pre2"], lw["w_router_t"], lw["b_router"], tri)


def _moe_plan(eidx_ref, rank_ref, cnt_ref, dest_ref, te_ref, tv_ref, tn_ref):
    tm = eidx_ref.shape[1]
    cnt = cnt_ref[...]
    padded = jnp.ceil(cnt * (1.0 / TMX)) * TMX
    row = lax.broadcasted_iota(jnp.int32, cnt.shape, 0)
    incl = padded
    shift = 1
    while shift < N_EXPERTS:
        incl = incl + jnp.where(row >= shift, pltpu.roll(incl, shift, 0), 0.0)
        shift *= 2
    start = (incl - padded)[:, 0:1]
    end = incl[:, 0:1]
    erow = lax.broadcasted_iota(jnp.int32, (N_EXPERTS, tm), 0)
    for k in range(TOP_K):
        mine = erow == eidx_ref[k:k + 1, :]
        base = jnp.sum(jnp.where(mine, start, 0.0), axis=0, keepdims=True)
        dest_ref[k:k + 1, :] = rank_ref[k:k + 1, :] + base.astype(jnp.int32)

    tile0 = (_lane_iota((1, LANES)) * TMX).astype(F32)
    owner = jnp.sum(jnp.where(end <= tile0, 1.0, 0.0), axis=0, keepdims=True)
    owner = jnp.minimum(owner, N_EXPERTS - 1.0)
    erow_t = lax.broadcasted_iota(jnp.int32, (N_EXPERTS, LANES), 0).astype(F32)
    left = jnp.sum(jnp.where(erow_t == owner, cnt[:, 0:1] - (tile0 - start), 0.0),
                   axis=0, keepdims=True)
    te_ref[...] = owner.astype(jnp.int32)
    tv_ref[...] = jnp.clip(left, 0.0, float(TMX)).astype(jnp.int32)
    tn_ref[...] = jnp.full(tn_ref.shape, N_EXPERTS, jnp.int32)
    nxt = owner
    for k in range(W_SLOTS - 1):
        later = (erow_t > nxt) & (cnt[:, 0:1] > 0.0)
        nxt = jnp.min(jnp.where(later, erow_t, float(N_EXPERTS)), axis=0, keepdims=True)
        tn_ref[k:k + 1, :] = nxt.astype(jnp.int32)
    n_used = jnp.sum(jnp.where(left > 0.0, 1.0, 0.0), axis=1, keepdims=True)
    tn_ref[7:8, :] = jnp.minimum(tile0 * (1.0 / TMX), n_used - 1.0).astype(jnp.int32)


def _experts_kernel(l, te_ref, tv_ref, tn_ref, xs_ref, wgu_hbm, wdn_hbm, ys_ref,
                    wgu_f, wdn_f, wgu_b, wdn_b, sem, group_ref):
    j = pl.program_id(0)
    valid = tv_ref[j]
    expert = te_ref[j]

    def fetch(e, slot):
        return (pltpu.make_async_copy(wgu_hbm.at[l, e], wgu_f.at[slot], sem.at[slot, 0]),
                pltpu.make_async_copy(wdn_hbm.at[l, e], wdn_f.at[slot], sem.at[slot, 1]))

    def start_if_any(e, slot):
        @pl.when(e < N_EXPERTS)
        def _():
            for cp in fetch(e, slot):
                cp.start()

    @pl.when(j == 0)
    def _():
        group_ref[0] = 0
        start_if_any(expert, 0)
        for k in range(W_SLOTS - 2):
            start_if_any(tn_ref[k, 0], k + 1)

    first_tile = (j == 0) | (expert != te_ref[jnp.maximum(j - 1, 0)])

    @pl.when(first_tile & (valid > 0))
    def _():
        group = group_ref[0]
        slot = lax.rem(group, W_SLOTS)
        for cp in fetch(expert, slot):
            cp.wait()
        wgu_b[...] = wgu_f[slot].astype(BF16)
        wdn_b[...] = wdn_f[slot].astype(BF16)
        start_if_any(tn_ref[W_SLOTS - 2, j], lax.rem(group + W_SLOTS - 1, W_SLOTS))
        group_ref[0] = group + 1

    @pl.when(valid > 0)
    def _():
        rows = lax.broadcasted_iota(jnp.int32, (TMX, D), 0)
        x = jnp.where(rows < valid, _unpack_rows(xs_ref[...]), 0.0).astype(BF16)
        gu = _dot(x, wgu_b[...])
        a = _silu(gu[:, 0:EXPERT_FF]) * gu[:, EXPERT_FF:2 * EXPERT_FF]
        ys_ref[...] = _pack_rows(_dot(a.astype(BF16), wdn_b[...]))


def _experts_call(l, xs, te, tv, tn, w_gu, w_dn):
    n_tiles = xs.shape[0] // TMX
    grid_spec = pltpu.PrefetchScalarGridSpec(
        num_scalar_prefetch=3,
        grid=(n_tiles,),
        in_specs=[pl.BlockSpec((TMX, D // 2), lambda j, te, tv, tn: (tn[7, j], 0)),
                  pl.BlockSpec(memory_space=pl.ANY), pl.BlockSpec(memory_space=pl.ANY)],
        out_specs=pl.BlockSpec((TMX, D // 2), lambda j, te, tv, tn: (tn[7, j], 0)),
        scratch_shapes=[pltpu.VMEM((W_SLOTS, D, 2 * EXPERT_FF), F32),
                        pltpu.VMEM((W_SLOTS, EXPERT_FF, D), F32),
                        pltpu.VMEM((D, 2 * EXPERT_FF), BF16), pltpu.VMEM((EXPERT_FF, D), BF16),
                        pltpu.SemaphoreType.DMA((W_SLOTS, 2)), pltpu.SMEM((1,), jnp.int32)])
    return pl.pallas_call(
        functools.partial(_experts_kernel, l),
        grid_spec=grid_spec,
        out_shape=jax.ShapeDtypeStruct(xs.shape, U32),
        compiler_params=_cparams(("arbitrary",)),
        name="moe_experts",
    )(te, tv, tn, xs, w_gu, w_dn)


def _moe_post_kernel(x_ref, mod_ref, hp_ref, yg_ref, comb_ref, wsgu_ref, wsdn_ref, gpost_ref,
                     o_ref):
    hb = _unpack_rows(hp_ref[...]).astype(BF16)
    sgu = _dot(hb, wsgu_ref[...])
    sa = _silu(sgu[:, 0:SHARED_FF]) * sgu[:, SHARED_FF:2 * SHARED_FF]
    acc = _dot(sa.astype(BF16), wsdn_ref[...])
    comb = comb_ref[...]
    for k in range(TOP_K):
        acc = acc + comb[:, k:k + 1] * _unpack_rows(yg_ref[k])
    g2 = mod_ref[...][:, 5 * D:6 * D]
    o_ref[...] = x_ref[...] + g2 * _rms(acc, gpost_ref[...])


def _moe_post_call(l, x, mod3, mod_row, hp, yg, comb, lw):
    n_tok = x.shape[0]
    tm = TM_MOE_PRE
    return pl.pallas_call(
        _moe_post_kernel,
        grid=(n_tok // tm,),
        in_specs=[pl.BlockSpec((tm, D), lambda i: (i, 0)),
                  pl.BlockSpec((None, 1, 6 * D), lambda i: (mod_row(i * tm), 0, 0)),
                  pl.BlockSpec((tm, D // 2), lambda i: (i, 0)),
                  pl.BlockSpec((TOP_K, tm, D // 2), lambda i: (0, i, 0)),
                  pl.BlockSpec((tm, LANES), lambda i: (i, 0)),
                  _layer_spec(l, (D, 2 * SHARED_FF)), _layer_spec(l, (SHARED_FF, D)),
                  _layer_spec(l, (1, D))],
        out_specs=pl.BlockSpec((tm, D), lambda i: (i, 0)),
        out_shape=jax.ShapeDtypeStruct((n_tok, D), F32),
        compiler_params=_cparams(("arbitrary",)),
        name="moe_post",
    )(x, mod3, hp, yg, comb, lw["w_sh_gu"], lw["w_sh_down"], lw["g_post2"])


def _moe_call(l, x, mod3, mod_row, lw):
    n_tok = x.shape[0]
    n_slots = -(-(TOP_K * n_tok + N_EXPERTS * (TMX - 1)) // TMX) * TMX
    assert n_slots // TMX <= LANES
    hp, comb, dest, te, tv, tn = _moe_pre_call(l, x, mod3, mod_row, lw)
    dest = dest.reshape(TOP_K * n_tok)
    xs = _sc_scatter_rows(hp, dest, n_slots)
    ys = _experts_call(l, xs, te[0], tv[0], tn, lw["w_exp_gu"], lw["w_exp_down"])
    yg = _sc_gather_rows(ys, dest).reshape(TOP_K, n_tok, D // 2)
    return _moe_post_call(l, x, mod3, mod_row, hp, yg, comb, lw)


SC_CORES, SC_SUBCORES = 2, 16
SC_WORKERS = SC_CORES * SC_SUBCORES


def _sc_gather_rows(table, idx, chunk=64):
    n_out, width = idx.shape[0], table.shape[1]
    per_worker = n_out // SC_WORKERS
    n_chunks = per_worker // chunk
    assert per_worker * SC_WORKERS == n_out and n_chunks * chunk == per_worker
    mesh = plsc.VectorSubcoreMesh(core_axis_name="c", subcore_axis_name="s",
                                  num_cores=SC_CORES, num_subcores=SC_SUBCORES)

    @functools.partial(
        pl.kernel, mesh=mesh,
        out_type=jax.ShapeDtypeStruct((n_out, width), table.dtype),
        scratch_types=[pltpu.VMEM((chunk,), jnp.int32), pltpu.VMEM((chunk, width), table.dtype),
                       pltpu.SemaphoreType.DMA],
        name="sc_gather")
    def gather(table_hbm, idx_hbm, out_hbm, idx_v, rows_v, sem):
        base = (lax.axis_index("s") * SC_CORES + lax.axis_index("c")) * per_worker

        @pl.loop(0, n_chunks)
        def _(j):
            off = base + j * chunk
            pltpu.sync_copy(idx_hbm.at[pl.ds(off, chunk)], idx_v)
            pltpu.async_copy(table_hbm.at[idx_v], rows_v, sem).wait()
            pltpu.sync_copy(rows_v, out_hbm.at[pl.ds(off, chunk)])

    return gather(table, idx)


def _sc_scatter_rows(rows, dest, n_slots, chunk=64):
    n_tok, width = rows.shape
    per_worker = n_tok // SC_WORKERS
    n_chunks = per_worker // chunk
    assert per_worker * SC_WORKERS == n_tok and n_chunks * chunk == per_worker
    mesh = plsc.VectorSubcoreMesh(core_axis_name="c", subcore_axis_name="s",
                                  num_cores=SC_CORES, num_subcores=SC_SUBCORES)

    @functools.partial(
        pl.kernel, mesh=mesh,
        out_type=jax.ShapeDtypeStruct((n_slots, width), rows.dtype),
        scratch_types=[pltpu.VMEM((chunk,), jnp.int32), pltpu.VMEM((chunk, width), rows.dtype)],
        name="sc_scatter")
    def scatter(rows_hbm, dest_hbm, out_hbm, idx_v, rows_v):
        base = (lax.axis_index("s") * SC_CORES + lax.axis_index("c")) * per_worker

        @pl.loop(0, n_chunks)
        def _(j):
            off = base + j * chunk
            pltpu.sync_copy(rows_hbm.at[pl.ds(off, chunk)], rows_v)
            for k in range(TOP_K):
                pltpu.sync_copy(dest_hbm.at[pl.ds(k * n_tok + off, chunk)], idx_v)
                pltpu.sync_copy(rows_v, out_hbm.at[idx_v])

    return scatter(rows, dest)


def _rope_tables(t_len):
    pos = np.arange(t_len)
    row, col = pos // GRID_W, pos % GRID_W

    def tab(r):
        half = r // 2
        freq = ROPE_BASE ** (-np.arange(half, dtype=np.float64) / half)
        sign = np.concatenate([-np.ones(half), np.ones(half)])
        cs, sn = [], []
        for p in (row, col):
            ang = p[:, None].astype(np.float64) * freq[None, :]
            cs.append(np.concatenate([np.cos(ang), np.cos(ang)], axis=1))
            sn.append(np.concatenate([np.sin(ang), np.sin(ang)], axis=1) * sign[None, :])
        return np.concatenate(cs, axis=1), np.concatenate(sn, axis=1)

    c64, s64 = tab(GQA_HD // 2)
    cpe, spe = tab(MLA_ROPE // 2)
    out = (np.tile(c64, (1, 2)), np.tile(s64, (1, 2)), np.tile(cpe, (1, 4)), np.tile(spe, (1, 4)))
    return tuple(jnp.asarray(a, F32) for a in out)


def _prep_weights(p):
    n_l = p["w_in"].shape[0]

    def row(name):
        return p[name].reshape(n_l, 1, -1)

    w_uq = p["w_mla_uq"].reshape(n_l, MLA_Q_LORA, MLA_HEADS, MLA_NOPE + MLA_ROPE)
    w_uq = jnp.concatenate([w_uq[..., :MLA_NOPE].reshape(n_l, MLA_Q_LORA, -1),
                            w_uq[..., MLA_NOPE:].reshape(n_l, MLA_Q_LORA, -1)], axis=-1)
    w_ukv = p["w_mla_ukv"].reshape(n_l, MLA_KV_LORA, MLA_HEADS, MLA_NOPE + MLA_V)
    w_ukv = jnp.concatenate([w_ukv[..., :MLA_NOPE].reshape(n_l, MLA_KV_LORA, -1),
                             w_ukv[..., MLA_NOPE:].reshape(n_l, MLA_KV_LORA, -1)], axis=-1)
    w_br = p["w_br"]
    w_br_gqa = w_br[:, 1].reshape(n_l, GQA_HEADS, GQA_HD, D)[:, jnp.array(GQA_ORDER)]
    w_br = jnp.concatenate([w_br[:, 0:1], w_br_gqa.reshape(n_l, 1, BRANCH_W, D), w_br[:, 2:4]], axis=1)
    blk = np.arange(512) // GQA_HD
    return {
        "g_pre1": row("g_pre1"), "g_post1": row("g_post1"),
        "g_pre2": row("g_pre2"), "g_post2": row("g_post2"),
        "w_in": jnp.swapaxes(p["w_in"], 1, 2).astype(BF16),
        "g_mla_q": row("g_mla_q"), "w_uq": w_uq.astype(BF16),
        "g_mla_kv": row("g_mla_kv"), "w_ukv": w_ukv.astype(BF16),
        "g_gqa_q": jnp.tile(p["g_gqa_q"], (1, GQA_HEADS)).reshape(n_l, 1, -1),
        "g_gqa_k": jnp.tile(p["g_gqa_k"], (1, GQA_KV_HEADS)).reshape(n_l, 1, -1),
        "bd": jnp.asarray(blk[:, None] == blk[None, :], BF16),
        "ret_decay": p["ret_decay"],
        "g_ret": row("g_ret"),
        "diff_lambda": p["diff_lambda"], "g_diff": row("g_diff"),
        "w_br": w_br.astype(BF16), "w_out": p["w_out"].astype(BF16),
        "w_router_t": jnp.swapaxes(p["w_router"], 1, 2),
        "b_router": p["b_router"].reshape(n_l, N_EXPERTS, 1),
        "w_exp_gu": p["w_exp_gu"], "w_exp_down": p["w_exp_down"],
        "w_sh_gu": p["w_sh_gu"].astype(BF16), "w_sh_down": p["w_sh_down"].astype(BF16),
    }


def _mixers(latent, l, x, mod3, mod_row, lw, n_b, t_len, tabs=None, past=None, s0=None,
            prev_cache=None):
    lam_init = 0.8 - 0.6 * math.exp(-0.3 * l)
    outs = _inprep_call(latent, l, x, mod3, mod_row, lw, tabs, t_len,
                        None if prev_cache is None else prev_cache[:6])
    qm, kvm, gq, gkv, dq, dkv, ret, rg, gates = outs[:9]
    attn = _attn_call(l, lam_init, qm, kvm, gq, gkv, dq, dkv, lw["diff_lambda"], lw["g_diff"],
                      n_b, t_len, past)
    retn = _ret_call(latent, l, lw["ret_decay"], ret, rg, lw["g_ret"], s0, n_b, t_len,
                     None if prev_cache is None else prev_cache[6])
    mixed = _mixers_call(attn, retn, n_b, t_len // TQ, "mix_lat" if latent else "mix_ctx")
    br, r = mixed[0], mixed[1:]
    y = _merge_call(l, x, mod3, mod_row, br, r[0], gates, lw)
    cache = None if latent else tuple(outs[9:]) + (r[1],)
    return y, cache


def kernel(x_prompt, x_sample, cache_mla_ckv, cache_mla_kpe, cache_gqa_k, cache_gqa_v, cache_diff_k, cache_diff_v, state_ret, c, c_ctx, w_mod, b_mod, g_pre1, g_post1, g_pre2, g_post2, w_in, g_mla_q, w_mla_uq, g_mla_kv, w_mla_ukv, g_gqa_q, g_gqa_k, ret_decay, g_ret, diff_lambda, g_diff, w_br, w_out, w_router, b_router, w_exp_gu, w_exp_down, w_sh_gu, w_sh_down):
    params = dict(w_in=w_in, g_pre1=g_pre1, g_post1=g_post1, g_pre2=g_pre2,
                  g_post2=g_post2, g_mla_q=g_mla_q, w_mla_uq=w_mla_uq,
                  g_mla_kv=g_mla_kv, w_mla_ukv=w_mla_ukv, g_gqa_q=g_gqa_q, g_gqa_k=g_gqa_k,
                  ret_decay=ret_decay, g_ret=g_ret, diff_lambda=diff_lambda, g_diff=g_diff,
                  w_br=w_br, w_out=w_out, w_router=w_router, b_router=b_router,
                  w_exp_gu=w_exp_gu, w_exp_down=w_exp_down, w_sh_gu=w_sh_gu, w_sh_down=w_sh_down)
    n_bc, t_c, _ = x_prompt.shape
    n_bl, t_l, _ = x_sample.shape
    p_len = cache_mla_ckv.shape[2]
    tabs = _rope_tables(t_l)
    n_cond = 8
    cond = jnp.concatenate([c_ctx[None, :], c, jnp.zeros((n_cond - 1 - n_bl, D), F32)], axis=0)
    assert t_l % TM_MERGE == 0 and (t_c * n_bc) % TM_MERGE == 0
    assert t_l % TM_MOE_PRE == 0 and (t_c * n_bc) % TM_MOE_PRE == 0

    yp = x_prompt.reshape(n_bc * t_c, D)
    ys = x_sample.reshape(n_bl * t_l, D)
    cache = None
    lw = _prep_weights(params)
    for l in range(DEPTH):
        mod3 = _mod_call(l, cond, w_mod, b_mod).reshape(n_cond, 1, 6 * D)
        yp, cache = _mixers(False, l, yp, mod3, lambda i: 0, lw, n_bc, t_c, prev_cache=cache)
        yp = _moe_call(l, yp, mod3, lambda i: 0, lw)
        past_kvm = _pastkv_call(l, cache_mla_ckv[:, l].reshape(n_bl * p_len, -1),
                                jnp.tile(cache_mla_kpe[:, l].reshape(n_bl * p_len, -1), (1, 4)),
                                lw["w_ukv"])
        past_gkv = jnp.concatenate([cache_gqa_k[:, l].reshape(n_bl * p_len, -1),
                                    cache_gqa_v[:, l].reshape(n_bl * p_len, -1),
                                    jnp.ones((n_bl * p_len, LANES), F32)], axis=-1).astype(BF16)
        past_dv = jnp.concatenate([cache_diff_v[:, l], jnp.ones_like(cache_diff_v[:, l])], axis=-1)
        past_dkv = jnp.concatenate([cache_diff_k[:, l].reshape(n_bl * p_len, -1),
                                    past_dv.reshape(n_bl * p_len, -1)], axis=-1).astype(BF16)
        s0 = state_ret[:, l].reshape(n_bl, 2, RET_HEADS // 2, 2 * RET_DK, RET_DV)
        ys, _ = _mixers(True, l, ys, mod3, lambda t: 1 + t // t_l, lw, n_bl, t_l, tabs=tabs,
                        past=(past_kvm, past_gkv, past_dkv), s0=s0)
        ys = _moe_call(l, ys, mod3, lambda t: 1 + t // t_l, lw)

    ckv, kpe, gk_t, gv_t, dk_t, dv, ret_state = cache

    def time_minor(a, shape):
        a = a.reshape((n_bc, DEPTH) + shape + (t_c,))
        return jnp.transpose(a, (0, 1, a.ndim - 1) + tuple(range(2, a.ndim - 1)))

    return (yp.reshape(n_bc, t_c, D), ys.reshape(n_bl, t_l, D), ckv, kpe,
            time_minor(gk_t, (GQA_KV_HEADS, GQA_HD)), time_minor(gv_t, (GQA_KV_HEADS, GQA_HD)),
            time_minor(dk_t, (DIFF_HEADS, 2, DIFF_D)),
            dv.reshape(n_bc, DEPTH, t_c, DIFF_HEADS, DIFF_DV), ret_state)
```

```python
import functools
import math

import numpy as np
import jax
import jax.numpy as jnp
from jax import lax
from jax.experimental import pallas as pl
from jax.experimental.pallas import tpu as pltpu
from jax.experimental.pallas import tpu_sc as plsc

F32 = jnp.float32
BF16 = jnp.bfloat16

D = 1024
DEPTH = 2
GRID_W = 64
ROPE_BASE = 10000.0
EPS = 1e-6

MLA_HEADS, MLA_NOPE, MLA_ROPE, MLA_V = 8, 64, 32, 64
MLA_Q_LORA, MLA_KV_LORA = 384, 256
GQA_HEADS, GQA_KV_HEADS, GQA_HD = 8, 2, 64
RET_HEADS, RET_DK, RET_DV = 4, 64, 128
DIFF_HEADS, DIFF_D, DIFF_DV = 4, 64, 128
N_BRANCH, BRANCH_W = 4, 512
N_EXPERTS, TOP_K, N_GROUPS, TOPK_GROUPS = 32, 4, 4, 2
EXPERT_FF, SHARED_FF = 256, 256
ROUTE_SCALE = 2.5
GROUP_SIZE = N_EXPERTS // N_GROUPS

LANES = 128
HALF_LANES = 64
VMEM_LIMIT = 56 * 1024 * 1024

O_CQ, O_CKV, O_KPE, O_GQ, O_GK, O_GV = 0, 384, 640, 672, 1184, 1312
O_RQ, O_RK, O_RV, O_RG, O_DQ, O_DK, O_DV, O_GL, O_END = (
    1440, 1696, 1952, 2464, 2976, 3488, 4000, 4512, 8608)
GQA_ORDER = (0, 4, 1, 5, 2, 6, 3, 7)

KVM_W = 8 * 256
GKV_W = 3 * LANES
DKV_W = 512 + 4 * 256
LOG2E = 1.4426950408889634
TM = 256
TM_LAT = 512
TM_MERGE = 512
TQ = 512


def _query_block(t_len):
    return min(TQ, t_len)
TM_MOE_PRE = 1024
TMX = 256
W_SLOTS = 4


def _cparams(sem):
    return pltpu.CompilerParams(dimension_semantics=sem, vmem_limit_bytes=VMEM_LIMIT)


def _const_spec(shape):
    nd = len(shape)
    return pl.BlockSpec(shape, lambda *_: (0,) * nd)


def _layer_spec(l, shape):
    nd = len(shape)
    return pl.BlockSpec((None,) + tuple(shape), lambda *_: (l,) + (0,) * nd)


def _rms(x, g):
    return x * lax.rsqrt(jnp.mean(x * x, axis=-1, keepdims=True) + EPS) * g


def _dot(a, b):
    return jnp.dot(a, b, preferred_element_type=F32)


def _dot_nt(a, b):
    return lax.dot_general(a, b, (((1,), (1,)), ((), ())), preferred_element_type=F32)


def _silu(x):
    return x * jax.nn.sigmoid(x)


def _lane_iota(shape):
    return lax.broadcasted_iota(jnp.int32, shape, len(shape) - 1)


def _seg_meansq(x, bd_ref, width):
    sq = x * x
    hi = sq.astype(BF16)
    lo = (sq - hi.astype(F32)).astype(BF16)
    bd = bd_ref[0:width, 0:width]
    return (_dot(hi, bd) + _dot(lo, bd)) * (1.0 / GQA_HD)


def _rope(x, cos, sin_signed, half):
    width = x.shape[-1]
    first = (_lane_iota(x.shape) % (2 * half)) < half
    partner = jnp.where(first, pltpu.roll(x, width - half, 1), pltpu.roll(x, half, 1))
    return x * cos + partner * sin_signed


def _tile_lanes(t, reps):
    return t if reps == 1 else jnp.concatenate([t] * reps, axis=1)


def _put_layer(o_ref, val, whole_stack):
    if whole_stack:
        o_ref[0] = val
        for k in range(1, o_ref.shape[0]):
            o_ref[k] = jnp.zeros_like(val)
    else:
        o_ref[...] = val


def _store_kvm(kvm_ref, kv, kpe_b):
    ones = jnp.ones(kpe_b.shape, BF16)
    for p in range(4):
        kvm_ref[:, p * 256:p * 256 + LANES] = kv[:, p * LANES:(p + 1) * LANES].astype(BF16)
        kvm_ref[:, p * 256 + LANES:(p + 1) * 256] = kpe_b
        kvm_ref[:, 1024 + p * 256:1024 + p * 256 + LANES] = (
            kv[:, 512 + p * LANES:512 + (p + 1) * LANES].astype(BF16))
        kvm_ref[:, 1024 + p * 256 + LANES:1024 + (p + 1) * 256] = ones


def _mod_kernel(c_ref, w_ref, b_ref, o_ref):
    a = _silu(c_ref[...]).astype(BF16)
    o_ref[...] = _dot(a, w_ref[...].astype(BF16)) + b_ref[...]


def _mod_call(l, cond, w_mod, b_mod):
    n_l, _, n = w_mod.shape
    tn = 1536
    return pl.pallas_call(
        _mod_kernel,
        grid=(n // tn,),
        in_specs=[_const_spec(cond.shape),
                  pl.BlockSpec((None, D, tn), lambda j: (l, 0, j)),
                  pl.BlockSpec((None, 1, tn), lambda j: (l, 0, j))],
        out_specs=pl.BlockSpec((cond.shape[0], tn), lambda j: (0, j)),
        out_shape=jax.ShapeDtypeStruct((cond.shape[0], n), F32),
        compiler_params=_cparams(("arbitrary",)),
        name="mod",
    )(cond, w_mod, b_mod.reshape(n_l, 1, n))


def _inprep_kernel(latent, n_aliased, *refs):
    (x_ref, mod_ref, gpre_ref, win_ref, gmq_ref, wuq_ref, gmkv_ref, wukv_ref,
     ggq_ref, ggk_ref, bd_ref) = refs[:11]
    refs = refs[11:]
    if latent:
        cos64_ref, sin64_ref, cospe_ref, sinpe_ref = refs[:4]
        refs = refs[4:]
    refs = refs[n_aliased:]
    first_layer = n_aliased == 0
    (qm_ref, kvm_ref, gqo_ref, gkv_ref, dqo_ref, dkv_ref, ret_ref, rg_ref, gate_ref) = refs[:9]
    refs = refs[9:]
    if not latent:
        ckv_o, kpe_o, gk_o, gv_o, dk_o, dv_o = refs

    x = x_ref[...]
    mod = mod_ref[...]
    sh1 = mod[:, 0:D]
    sc1 = mod[:, D:2 * D]
    hb = (_rms(x, gpre_ref[...]) * (1.0 + sc1) + sh1).astype(BF16)

    def z(a, b):
        return _dot_nt(hb, win_ref[a:b, :])

    if latent:
        cos64, sin64 = cos64_ref[...], sin64_ref[...]
        cospe, sinpe = cospe_ref[...], sinpe_ref[...]

    cqn = _rms(z(O_CQ, O_CKV), gmq_ref[...]).astype(BF16)
    q = _dot(cqn, wuq_ref[...]) * ((MLA_NOPE + MLA_ROPE) ** -0.5 * LOG2E)
    q_nope, q_pe = q[:, 0:512], q[:, 512:768]
    if latent:
        q_pe = _rope(q_pe, _tile_lanes(cospe, 2), _tile_lanes(sinpe, 2), MLA_ROPE // 4)
    qm_ref[:, 0:512] = q_nope.astype(BF16)
    qm_ref[:, 512:768] = q_pe.astype(BF16)

    ckvn = _rms(z(O_CKV, O_KPE), gmkv_ref[...])
    kv = _dot(ckvn.astype(BF16), wukv_ref[...])
    kpe4 = _dot_nt(hb, jnp.concatenate([win_ref[O_KPE:O_GQ, :]] * 4, axis=0))
    if latent:
        kpe4 = _rope(kpe4, cospe, sinpe, MLA_ROPE // 4)
    else:
        _put_layer(ckv_o, ckvn, first_layer)
        _put_layer(kpe_o, kpe4[:, 0:MLA_ROPE], first_layer)
    _store_kvm(kvm_ref, kv, kpe4.astype(BF16))

    gq = _dot_nt(hb, jnp.concatenate(
        [win_ref[O_GQ + h * GQA_HD:O_GQ + (h + 1) * GQA_HD, :] for h in GQA_ORDER], axis=0))
    gq = gq * lax.rsqrt(_seg_meansq(gq, bd_ref, 512) + EPS) * ggq_ref[...]
    gk = z(O_GK, O_GV)
    gk = gk * lax.rsqrt(_seg_meansq(gk, bd_ref, LANES) + EPS) * ggk_ref[...]
    gv = z(O_GV, O_RQ)
    if latent:
        gq = _rope(gq, _tile_lanes(cos64, 4), _tile_lanes(sin64, 4), GQA_HD // 4)
        gk = _rope(gk, cos64, sin64, GQA_HD // 4)
    else:
        _put_layer(gk_o, jnp.transpose(gk), first_layer)
        _put_layer(gv_o, jnp.transpose(gv), first_layer)
    gqo_ref[...] = (gq * (GQA_HD ** -0.5 * LOG2E)).astype(BF16)
    gkv_ref[:, 0:LANES] = gk.astype(BF16)
    gkv_ref[:, LANES:2 * LANES] = gv.astype(BF16)
    gkv_ref[:, 2 * LANES:3 * LANES] = jnp.ones(gv.shape, BF16)

    dq = z(O_DQ, O_DK)
    dk = z(O_DK, O_DV)
    dv = z(O_DV, O_GL)
    if latent:
        dq = _rope(dq, _tile_lanes(cos64, 4), _tile_lanes(sin64, 4), DIFF_D // 4)
        dk = _rope(dk, _tile_lanes(cos64, 4), _tile_lanes(sin64, 4), DIFF_D // 4)
    else:
        _put_layer(dk_o, jnp.transpose(dk), first_layer)
        _put_layer(dv_o, dv, first_layer)
    dqo_ref[...] = (dq * (DIFF_D ** -0.5 * LOG2E)).astype(BF16)
    dkv_ref[:, 0:512] = dk.astype(BF16)
    for h in range(DIFF_HEADS):
        dkv_ref[:, 512 + h * 256:512 + h * 256 + LANES] = dv[:, h * LANES:(h + 1) * LANES].astype(BF16)
        dkv_ref[:, 512 + h * 256 + LANES:512 + (h + 1) * 256] = jnp.ones((dv.shape[0], LANES), BF16)

    ret_ref[:, 0:256] = z(O_RQ, O_RK).astype(BF16)
    ret_ref[:, 256:512] = (z(O_RK, O_RV) * (RET_DK ** -0.5)).astype(BF16)
    ret_ref[:, 512:1024] = z(O_RV, O_RG).astype(BF16)
    rg_ref[...] = z(O_RG, O_DQ).astype(BF16)

    for n in range(N_BRANCH):
        gate_ref[:, n * D:(n + 1) * D] = jax.nn.sigmoid(
            z(O_GL + n * D, O_GL + (n + 1) * D)).astype(BF16)


def _inprep_call(latent, l, x, mod3, mod_row, lw, tabs, t_len, prev_caches=None):
    n_tok = x.shape[0]
    tm = TM_LAT if latent else TM
    nblk = n_tok // tm
    blk_per_seq = t_len // tm

    def tok(w):
        return pl.BlockSpec((tm, w), lambda i: (i, 0))

    in_specs = [tok(D),
                pl.BlockSpec((None, 1, 6 * D), lambda i: (mod_row(i * tm), 0, 0)),
                _layer_spec(l, (1, D)),
                pl.BlockSpec((None, O_END, D), lambda i: (l, 0, 0), pipeline_mode=pl.Buffered(1)),
                _layer_spec(l, (1, MLA_Q_LORA)), _layer_spec(l, (MLA_Q_LORA, 768)),
                _layer_spec(l, (1, MLA_KV_LORA)), _layer_spec(l, (MLA_KV_LORA, 1024)),
                _layer_spec(l, (1, 512)), _layer_spec(l, (1, LANES)), _const_spec((512, 512))]
    args = [x, mod3, lw["g_pre1"], lw["w_in"], lw["g_mla_q"], lw["w_uq"], lw["g_mla_kv"],
            lw["w_ukv"], lw["g_gqa_q"], lw["g_gqa_k"], lw["bd"]]
    if latent:
        tab_spec = pl.BlockSpec((tm, LANES), lambda i: (i % blk_per_seq, 0))
        in_specs += [tab_spec] * 4
        args += list(tabs)
    widths = [768, KVM_W, 512, GKV_W, 512, DKV_W, 1024, 512, 4 * D]
    out_specs = [tok(w) for w in widths]
    out_shape = [jax.ShapeDtypeStruct((n_tok, w), BF16) for w in widths]
    aliases = {}
    if not latent:
        assert TM == t_len and (prev_caches is None) == (l == 0)
        n_seq = n_tok // t_len
        lead, at = ((DEPTH,), 0) if l == 0 else ((None,), l)

        def row_major(w):
            out_specs.append(pl.BlockSpec((None,) + lead + (t_len, w), lambda i: (i, at, 0, 0)))
            out_shape.append(jax.ShapeDtypeStruct((n_seq, DEPTH, t_len, w), F32))

        row_major(MLA_KV_LORA)
        row_major(MLA_ROPE)
        for w in (LANES, LANES, 512):
            out_specs.append(pl.BlockSpec((None,) + lead + (w, t_len), lambda i: (i, at, 0, 0)))
            out_shape.append(jax.ShapeDtypeStruct((n_seq, DEPTH, w, t_len), F32))
        row_major(512)
        if prev_caches is not None:
            n_in = len(args)
            in_specs += [pl.BlockSpec(memory_space=pl.ANY)] * len(prev_caches)
            args += list(prev_caches)
            aliases = {n_in + k: len(widths) + k for k in range(len(prev_caches))}
    return pl.pallas_call(
        functools.partial(_inprep_kernel, latent, len(aliases)),
        grid=(nblk,),
        in_specs=in_specs, out_specs=out_specs, out_shape=out_shape,
        input_output_aliases=aliases,
        compiler_params=_cparams(("arbitrary",)),
        name="inprep_lat" if latent else "inprep_ctx",
    )(*args)


def _pastkv_kernel(ckv_ref, kpe_ref, wukv_ref, o_ref):
    kv = _dot(ckv_ref[...].astype(BF16), wukv_ref[...])
    _store_kvm(o_ref, kv, kpe_ref[...].astype(BF16))


def _pastkv_call(l, ckv, kpe4, w_ukv):
    n = ckv.shape[0]
    return pl.pallas_call(
        _pastkv_kernel,
        grid=(n // TM,),
        in_specs=[pl.BlockSpec((TM, MLA_KV_LORA), lambda i: (i, 0)),
                  pl.BlockSpec((TM, LANES), lambda i: (i, 0)),
                  _layer_spec(l, (MLA_KV_LORA, 1024))],
        out_specs=pl.BlockSpec((TM, KVM_W), lambda i: (i, 0)),
        out_shape=jax.ShapeDtypeStruct((n, KVM_W), BF16),
        compiler_params=_cparams(("arbitrary",)),
        name="pastkv",
    )(ckv, kpe4, w_ukv)


def _softmax_pv(s, v_ones):
    m = jnp.max(s, axis=-1, keepdims=True)
    p = jnp.exp2(s - m).astype(BF16)
    o = _dot(p, v_ones)
    return o[:, 0:LANES] / o[:, LANES:2 * LANES]


def _attn_kernel(lam_init, n_past, qm_ref, kvm_ref, gq_ref, gkv_ref, dq_ref, dkv_ref, *refs):
    if n_past:
        past_refs, refs = refs[:3], refs[3:]
        lam_ref, gdiff_ref, o_ref = refs[:3]
        joined = refs[3:]

        @pl.when(pl.program_id(1) == 0)
        def _():
            for dst, past, new in zip(joined, past_refs, (kvm_ref, gkv_ref, dkv_ref)):
                dst[0:n_past, :] = past[...]
                dst[n_past:, :] = new[...]

        kvm_ref, gkv_ref, dkv_ref = joined
    else:
        lam_ref, gdiff_ref, o_ref = refs
    tq = qm_ref.shape[0]
    lane = _lane_iota((tq, LANES))
    low = lane < HALF_LANES
    zero = jnp.zeros((tq, LANES), BF16)

    for p in range(MLA_HEADS // 2):
        qn = qm_ref[:, p * LANES:(p + 1) * LANES]
        g = p // 2
        qpe = qm_ref[:, 512 + g * LANES:512 + (g + 1) * LANES]
        kk = kvm_ref[:, p * 256:(p + 1) * 256]
        vv = kvm_ref[:, 1024 + p * 256:1024 + (p + 1) * 256]
        outs = []
        for half in range(2):
            h = 2 * p + half
            slot = h % 4
            in_slot = (lane >= slot * MLA_ROPE) & (lane < (slot + 1) * MLA_ROPE)
            lhs = jnp.concatenate(
                [jnp.where(low if half == 0 else ~low, qn, zero),
                 jnp.where(in_slot, qpe, zero)], axis=1)
            outs.append(_softmax_pv(_dot_nt(lhs, kk), vv))
        o_ref[:, p * LANES:(p + 1) * LANES] = jnp.where(low, outs[0], outs[1]).astype(BF16)

    kk = gkv_ref[:, 0:LANES]
    vv = gkv_ref[:, LANES:3 * LANES]
    for g in range(GQA_HEADS // 2):
        qg = gq_ref[:, g * LANES:(g + 1) * LANES]
        o_lo = _softmax_pv(_dot_nt(jnp.where(low, qg, zero), kk), vv)
        o_hi = _softmax_pv(_dot_nt(jnp.where(low, zero, qg), kk), vv)
        o_ref[:, 512 + g * LANES:512 + (g + 1) * LANES] = jnp.where(low, o_lo, o_hi).astype(BF16)

    lp = lam_ref[...]
    lam = (jnp.exp(jnp.sum(lp[0:1] * lp[1:2], axis=-1, keepdims=True))
           - jnp.exp(jnp.sum(lp[2:3] * lp[3:4], axis=-1, keepdims=True)) + lam_init)
    for h in range(DIFF_HEADS):
        qh = dq_ref[:, h * LANES:(h + 1) * LANES]
        kk = dkv_ref[:, h * LANES:(h + 1) * LANES]
        vv = dkv_ref[:, 512 + h * 256:512 + (h + 1) * 256]
        a1 =_softmax_pv(_dot_nt(jnp.where(low, qh, zero), kk), vv)
        a2 = _softmax_pv(_dot_nt(jnp.where(low, zero, qh), kk), vv)
        od = _rms(a1 - lam * a2, gdiff_ref[...]) * (1.0 - lam_init)
        o_ref[:, 1024 + h * LANES:1024 + (h + 1) * LANES] = od.astype(BF16)


def _attn_call(l, lam_init, qm, kvm, gq, gkv, dq, dkv, lam_p, g_diff, n_b, t_len, past=None):
    tq = _query_block(t_len)
    nq = t_len // tq
    n_past = 0 if past is None else past[0].shape[0] // n_b

    def qspec(w):
        return pl.BlockSpec((tq, w), lambda b, i: (b * nq + i, 0))

    def kspec(w, rows=t_len):
        return pl.BlockSpec((rows, w), lambda b, i: (b, 0))

    in_specs = [qspec(768), kspec(KVM_W), qspec(512), kspec(GKV_W), qspec(512), kspec(DKV_W)]
    args = [qm, kvm, gq, gkv, dq, dkv]
    scratch = []
    if n_past:
        in_specs += [kspec(KVM_W, n_past), kspec(GKV_W, n_past), kspec(DKV_W, n_past)]
        args += list(past)
        scratch = [pltpu.VMEM((n_past + t_len, w), BF16) for w in (KVM_W, GKV_W, DKV_W)]
    in_specs += [_layer_spec(l, (4, DIFF_D)), _layer_spec(l, (1, DIFF_DV))]
    args += [lam_p, g_diff]
    return dict(kernel=functools.partial(_attn_kernel, lam_init, n_past), in_specs=in_specs, args=args,
                out_specs=[qspec(3 * BRANCH_W)],
                out_shape=[jax.ShapeDtypeStruct((n_b * t_len, 3 * BRANCH_W), BF16)],
                scratch=scratch, aliases={})


def _log_sigmoid(x):
    return jnp.minimum(x, 0.0) - jnp.log(1.0 + jnp.exp(-jnp.abs(x)))


def _log_gamma(dec_ref, l, d, h):
    return _log_sigmoid(jnp.full((1, 1), dec_ref[l, d, h], F32))


def _ret_kernel(latent, l, t_len, dec_ref, q_ref, k_ref, v_ref, rg_ref, gret_ref, *refs):
    if latent:
        s0_ref, o_ref = refs
    else:
        o_ref, st_ref = refs[-2:]
    tq = q_ref.shape[0]
    t0 = pl.program_id(1) * tq
    lane = _lane_iota((tq, LANES))
    low = lane < HALF_LANES
    zero = jnp.zeros((tq, LANES), BF16)
    t_idx = (t0 + lax.broadcasted_iota(jnp.int32, (tq, t_len), 0)).astype(F32)
    s_idx = lax.broadcasted_iota(jnp.int32, (tq, t_len), 1).astype(F32)
    dist = t_idx - s_idx
    past = dist >= 0
    diag = jnp.where(dist == 0, 1.0, 0.0)
    t_col = (t0 + lax.broadcasted_iota(jnp.int32, (tq, 1), 0)).astype(F32)

    def lg(d, h):
        return _log_gamma(dec_ref, l, d, h)

    for h in range(RET_HEADS):
        p, half = h // 2, h % 2
        qp = q_ref[:, p * LANES:(p + 1) * LANES]
        qm = jnp.where(low if half == 0 else ~low, qp, zero)
        kp = k_ref[:, p * LANES:(p + 1) * LANES]
        vh = v_ref[:, h * LANES:(h + 1) * LANES]
        lgf, lgb = lg(0, h), lg(1, h)
        dmask = jnp.exp(jnp.where(past, lgf, -lgb) * dist) + diag
        o = _dot((_dot_nt(qm, kp) * dmask).astype(BF16), vh)
        if latent:
            sf =s0_ref[0, p].astype(BF16)
            sb = s0_ref[1, p].astype(BF16)
            o = o + _dot(qm, sf) * jnp.exp(lgf * (t_col + 1.0))
            o = o + _dot(qm, sb) * jnp.exp(lgb * (float(t_len) - t_col))
        mu = jnp.mean(o, axis=-1, keepdims=True)
        oc = o - mu
        y = oc * lax.rsqrt(jnp.mean(oc * oc, axis=-1, keepdims=True) + EPS)
        y = y * gret_ref[:, h * LANES:(h + 1) * LANES]
        rg = rg_ref[:, h * LANES:(h + 1) * LANES].astype(F32)
        o_ref[:, h * LANES:(h + 1) * LANES] = (y * _silu(rg)).astype(BF16)

    if not latent:
        s_col = lax.broadcasted_iota(jnp.int32, (t_len, 1), 0).astype(F32)
        lane_t = _lane_iota((1, LANES)) < HALF_LANES
        for p in range(RET_HEADS // 2):
            kp = k_ref[:, p * LANES:(p + 1) * LANES].astype(F32)
            for d in range(2):
                lg_lane = jnp.where(lane_t, lg(d, 2 * p), lg(d, 2 * p + 1))
                expo = (float(t_len) - 1.0 - s_col) if d == 0 else s_col
                kdec_t = jnp.transpose(kp * jnp.exp(lg_lane * expo)).astype(BF16)
                for half in range(2):
                    h = 2 * p + half
                    st = _dot(kdec_t, v_ref[:, h * LANES:(h + 1) * LANES])
                    st = st[half * RET_DK:(half + 1) * RET_DK, :]
                    if l == 0:
                        st_ref[0, d, h] = st
                        for k in range(1, st_ref.shape[0]):
                            st_ref[k, d, h] = jnp.zeros_like(st)
                    else:
                        st_ref[d, h] = st


def _ret_call(latent, l, dec, ret, rg, g_ret, s0, n_b, t_len, prev_state=None):
    tq = _query_block(t_len)
    nq = t_len // tq
    aliases = {}
    assert latent or nq == 1
    in_specs = [pl.BlockSpec(memory_space=pltpu.SMEM),
                pl.BlockSpec((tq, 256), lambda b, i: (b * nq + i, 0)),
                pl.BlockSpec((t_len, 256), lambda b, i: (b, 1)),
                pl.BlockSpec((t_len, 512), lambda b, i: (b, 1)),
                pl.BlockSpec((tq, 512), lambda b, i: (b * nq + i, 0)),
                _layer_spec(l, (1, 512))]
    args = [dec, ret, ret, ret, rg, g_ret]
    out_specs = [pl.BlockSpec((tq, 512), lambda b, i: (b * nq + i, 0))]
    out_shape = [jax.ShapeDtypeStruct((n_b * t_len, 512), BF16)]
    if latent:
        in_specs.append(pl.BlockSpec((None, 2, 2, LANES, LANES), lambda b, i: (b, 0, 0, 0, 0)))
        args.append(s0)
    else:
        assert (prev_state is None) == (l == 0)
        lead, at = ((DEPTH,), 0) if l == 0 else ((None,), l)
        out_specs.append(pl.BlockSpec((None,) + lead + (2, RET_HEADS, RET_DK, RET_DV),
                                      lambda b, i: (b, at, 0, 0, 0, 0)))
        out_shape.append(jax.ShapeDtypeStruct((n_b, DEPTH, 2, RET_HEADS, RET_DK, RET_DV), F32))
        if prev_state is not None:
            aliases = {len(args): 1}
            in_specs.append(pl.BlockSpec(memory_space=pl.ANY))
            args.append(prev_state)
    return dict(kernel=functools.partial(_ret_kernel, latent, l, t_len), in_specs=in_specs, args=args,
                out_specs=out_specs, out_shape=out_shape, scratch=[], aliases=aliases)


def _mixers_call(attn, ret, n_b, nq, name):
    n_ai, n_ri = len(attn["args"]), len(ret["args"])
    n_ao, n_ro = len(attn["out_specs"]), len(ret["out_specs"])

    def kernel(*refs):
        a_in, r_in = refs[:n_ai], refs[n_ai:n_ai + n_ri]
        outs = refs[n_ai + n_ri:n_ai + n_ri + n_ao + n_ro]
        scratch = refs[n_ai + n_ri + n_ao + n_ro:]
        attn["kernel"](*a_in, *outs[:n_ao], *scratch)
        ret["kernel"](*r_in, *outs[n_ao:])

    aliases = {n_ai + k: n_ao + v for k, v in ret["aliases"].items()}
    return pl.pallas_call(
        kernel,
        grid=(n_b, nq),
        in_specs=attn["in_specs"] + ret["in_specs"],
        out_specs=attn["out_specs"] + ret["out_specs"],
        out_shape=attn["out_shape"] + ret["out_shape"],
        scratch_shapes=attn["scratch"],
        input_output_aliases=aliases,
        compiler_params=_cparams(("arbitrary", "arbitrary")),
        name=name,
    )(*attn["args"], *ret["args"])


def _merge_kernel(x_ref, mod_ref, br_ref, or_ref, gate_ref, wbr_ref, wout_ref, gpost_ref, o_ref):
    merged = None
    for n in range(N_BRANCH):
        if n < 2:
            b = br_ref[:, n * BRANCH_W:(n + 1) * BRANCH_W]
        elif n == 2:
            b = or_ref[...]
        else:
            b = br_ref[:, 2 * BRANCH_W:3 * BRANCH_W]
        t = gate_ref[:, n * D:(n + 1) * D].astype(F32) * _dot(b, wbr_ref[n])
        merged = t if merged is None else merged + t
    out = _dot(merged.astype(BF16), wout_ref[...])
    g1 = mod_ref[...][:, 2 * D:3 * D]
    o_ref[...] = x_ref[...] + g1 * _rms(out, gpost_ref[...])


def _merge_call(l, x, mod3, mod_row, br, o_r, gates, lw):
    n_tok = x.shape[0]

    def tok(w):
        return pl.BlockSpec((TM_MERGE, w), lambda i: (i, 0))

    return pl.pallas_call(
        _merge_kernel,
        grid=(n_tok // TM_MERGE,),
        in_specs=[tok(D), pl.BlockSpec((None, 1, 6 * D), lambda i: (mod_row(i * TM_MERGE), 0, 0)),
                  tok(3 * BRANCH_W), tok(BRANCH_W), tok(4 * D),
                  _layer_spec(l, (N_BRANCH, BRANCH_W, D)), _layer_spec(l, (D, D)),
                  _layer_spec(l, (1, D))],
        out_specs=tok(D),
        out_shape=jax.ShapeDtypeStruct((n_tok, D), F32),
        compiler_params=_cparams(("arbitrary",)),
        name="merge",
    )(x, mod3, br, o_r, gates, lw["w_br"], lw["w_out"], lw["g_post1"])


def _route(logits_t, bias):
    n = logits_t.shape[1]
    scores = jax.nn.sigmoid(logits_t)
    sel = scores + bias
    neg = -jnp.inf
    sub = lax.broadcasted_iota(jnp.int32, (GROUP_SIZE, n), 0)
    grp = []
    for g in range(N_GROUPS):
        blk = sel[g * GROUP_SIZE:(g + 1) * GROUP_SIZE]
        m1 = jnp.max(blk, axis=0, keepdims=True)
        i1 = jnp.min(jnp.where(blk == m1, sub, GROUP_SIZE), axis=0, keepdims=True)
        m2 = jnp.max(jnp.where(sub == i1, neg, blk), axis=0, keepdims=True)
        grp.append(m1 + m2)
    parts = []
    for g in range(N_GROUPS):
        beaten = jnp.zeros((1, n), jnp.int32)
        for o in range(N_GROUPS):
            if o == g:
                continue
            wins = (grp[o] > grp[g]) | (grp[o] == grp[g]) if o < g else (grp[o] > grp[g])
            beaten = beaten + wins.astype(jnp.int32)
        keep = beaten < TOPK_GROUPS
        parts.append(jnp.where(keep, sel[g * GROUP_SIZE:(g + 1) * GROUP_SIZE], neg))
    cur = jnp.concatenate(parts, axis=0)
    eidx = lax.broadcasted_iota(jnp.int32, (N_EXPERTS, n), 0)
    hits, ids, ws = [], [], []
    for _ in range(TOP_K):
        m = jnp.max(cur, axis=0, keepdims=True)
        i = jnp.min(jnp.where(cur == m, eidx, N_EXPERTS), axis=0, keepdims=True)
        hit = eidx == i
        hits.append(hit)
        ids.append(i)
        ws.append(jnp.sum(jnp.where(hit, scores, 0.0), axis=0, keepdims=True))
        cur = jnp.where(hit, neg, cur)
    wsum = ws[0] + ws[1] + ws[2] + ws[3]
    return hits, ids, [w / wsum * ROUTE_SCALE for w in ws]


U32 = jnp.uint32
HIGH16 = np.uint32(0xFFFF0000)


def _bf16_bits(v):
    return lax.bitcast_convert_type(v.astype(BF16).astype(F32), U32)


def _pack_rows(v):
    return (_bf16_bits(v[:, 0:D // 2]) >> 16) | _bf16_bits(v[:, D // 2:D])


def _unpack_rows(p):
    lo = lax.bitcast_convert_type(p << 16, F32)
    hi = lax.bitcast_convert_type(p & HIGH16, F32)
    return jnp.concatenate([lo, hi], axis=1)


def _moe_pre_kernel(x_ref, mod_ref, gpre_ref, wr_ref, br_ref, tri_ref,
                    hp_ref, comb_ref, dest_ref, te_ref, tv_ref, tn_ref, run_ref, eidx_ref, rank_ref):
    tm = x_ref.shape[0]
    step = pl.program_id(0)
    cols = pl.ds(pl.multiple_of(step * tm, tm), tm)

    @pl.when(pl.program_id(0) == 0)
    def _():
        run_ref[...] = jnp.zeros_like(run_ref)

    mod = mod_ref[...]
    sh2, sc2 = mod[:, 3 * D:4 * D], mod[:, 4 * D:5 * D]
    h = _rms(x_ref[...], gpre_ref[...]) * (1.0 + sc2) + sh2
    hp_ref[...] = _pack_rows(h)
    hb = h.astype(BF16)
    h_lo = (h - hb.astype(F32)).astype(BF16)
    wr = wr_ref[...]
    wr_hi = wr.astype(BF16)
    wr_lo = (wr - wr_hi.astype(F32)).astype(BF16)
    logits_t = _dot_nt(wr_hi, hb) + _dot_nt(wr_hi, h_lo) + _dot_nt(wr_lo, hb)
    hits, ids, ws = _route(logits_t, br_ref[...])

    picked = jnp.zeros((N_EXPERTS, tm), F32)
    for hit in hits:
        picked = jnp.where(hit, 1.0, picked)
    before = _dot(picked.astype(BF16), tri_ref[...]) + run_ref[:, 0:1]
    sub8 = lax.broadcasted_iota(jnp.int32, (8, tm), 0)
    comb8 = jnp.zeros((8, tm), F32)
    for k in range(TOP_K):
        rank = jnp.sum(jnp.where(hits[k], before, 0.0), axis=0, keepdims=True)
        eidx_ref[k:k + 1, cols] = ids[k]
        rank_ref[k:k + 1, cols] = rank.astype(jnp.int32)
        comb8 = jnp.where(sub8 == k, ws[k], comb8)
    comb_ref[...] = jnp.transpose(
        jnp.concatenate([comb8, jnp.zeros((LANES - 8, tm), F32)], axis=0))
    run_ref[...] = run_ref[...] + jnp.sum(picked, axis=1, keepdims=True)

    @pl.when(step == pl.num_programs(0) - 1)
    def _():
        _moe_plan(eidx_ref, rank_ref, run_ref, dest_ref, te_ref, tv_ref, tn_ref)


def _moe_pre_call(l, x, mod3, mod_row, lw):
    n_tok = x.shape[0]
    tm = TM_MOE_PRE
    tri = np.arange(tm)
    tri = jnp.asarray(tri[:, None] < tri[None, :], BF16)
    tiles = jax.ShapeDtypeStruct((1, LANES), jnp.int32)
    return pl.pallas_call(
        _moe_pre_kernel,
        grid=(n_tok // tm,),
        in_specs=[pl.BlockSpec((tm, D), lambda i: (i, 0)),
                  pl.BlockSpec((None, 1, 6 * D), lambda i: (mod_row(i * tm), 0, 0)),
                  _layer_spec(l, (1, D)), _layer_spec(l, (N_EXPERTS, D)),
                  _layer_spec(l, (N_EXPERTS, 1)), _const_spec((tm, tm))],
        out_specs=[pl.BlockSpec((tm, D // 2), lambda i: (i, 0)),
                   pl.BlockSpec((tm, LANES), lambda i: (i, 0)),
                   _const_spec((TOP_K, n_tok)), _const_spec((1, LANES)), _const_spec((1, LANES)),
                   _const_spec((8, LANES))],
        out_shape=[jax.ShapeDtypeStruct((n_tok, D // 2), U32),
                   jax.ShapeDtypeStruct((n_tok, LANES), F32),
                   jax.ShapeDtypeStruct((TOP_K, n_tok), jnp.int32), tiles, tiles,
                   jax.ShapeDtypeStruct((8, LANES), jnp.int32)],
        scratch_shapes=[pltpu.VMEM((N_EXPERTS, LANES), F32),
                        pltpu.VMEM((TOP_K, n_tok), jnp.int32), pltpu.VMEM((TOP_K, n_tok), jnp.int32)],
        compiler_params=_cparams(("arbitrary",)),
        name="moe_pre",
    )(x, mod3, lw["g_pre2"], lw["w_router_t"], lw["b_router"], tri)


def _moe_plan(eidx_ref, rank_ref, cnt_ref, dest_ref, te_ref, tv_ref, tn_ref):
    tm = eidx_ref.shape[1]
    cnt = cnt_ref[...]
    padded = jnp.ceil(cnt * (1.0 / TMX)) * TMX
    row = lax.broadcasted_iota(jnp.int32, cnt.shape, 0)
    incl = padded
    shift = 1
    while shift < N_EXPERTS:
        incl = incl + jnp.where(row >= shift, pltpu.roll(incl, shift, 0), 0.0)
        shift *= 2
    start = (incl - padded)[:, 0:1]
    end = incl[:, 0:1]
    erow = lax.broadcasted_iota(jnp.int32, (N_EXPERTS, tm), 0)
    for k in range(TOP_K):
        mine = erow == eidx_ref[k:k + 1, :]
        base = jnp.sum(jnp.where(mine, start, 0.0), axis=0, keepdims=True)
        dest_ref[k:k + 1, :] = rank_ref[k:k + 1, :] + base.astype(jnp.int32)

    tile0 = (_lane_iota((1, LANES)) * TMX).astype(F32)
    owner = jnp.sum(jnp.where(end <= tile0, 1.0, 0.0), axis=0, keepdims=True)
    owner = jnp.minimum(owner, N_EXPERTS - 1.0)
    erow_t = lax.broadcasted_iota(jnp.int32, (N_EXPERTS, LANES), 0).astype(F32)
    left = jnp.sum(jnp.where(erow_t == owner, cnt[:, 0:1] - (tile0 - start), 0.0),
                   axis=0, keepdims=True)
    te_ref[...] = owner.astype(jnp.int32)
    tv_ref[...] = jnp.clip(left, 0.0, float(TMX)).astype(jnp.int32)
    tn_ref[...] = jnp.full(tn_ref.shape, N_EXPERTS, jnp.int32)
    nxt = owner
    for k in range(W_SLOTS - 1):
        later = (erow_t > nxt) & (cnt[:, 0:1] > 0.0)
        nxt = jnp.min(jnp.where(later, erow_t, float(N_EXPERTS)), axis=0, keepdims=True)
        tn_ref[k:k + 1, :] = nxt.astype(jnp.int32)
    n_used = jnp.sum(jnp.where(left > 0.0, 1.0, 0.0), axis=1, keepdims=True)
    tn_ref[7:8, :] = jnp.minimum(tile0 * (1.0 / TMX), n_used - 1.0).astype(jnp.int32)


def _experts_kernel(l, te_ref, tv_ref, tn_ref, xs_ref, wgu_hbm, wdn_hbm, ys_ref,
                    wgu_f, wdn_f, wgu_b, wdn_b, sem, group_ref):
    j = pl.program_id(0)
    valid = tv_ref[j]
    expert = te_ref[j]

    def fetch(e, slot):
        return (pltpu.make_async_copy(wgu_hbm.at[l, e], wgu_f.at[slot], sem.at[slot, 0]),
                pltpu.make_async_copy(wdn_hbm.at[l, e], wdn_f.at[slot], sem.at[slot, 1]))

    def start_if_any(e, slot):
        @pl.when(e < N_EXPERTS)
        def _():
            for cp in fetch(e, slot):
                cp.start()

    @pl.when(j == 0)
    def _():
        group_ref[0] = 0
        start_if_any(expert, 0)
        for k in range(W_SLOTS - 2):
            start_if_any(tn_ref[k, 0], k + 1)

    first_tile = (j == 0) | (expert != te_ref[jnp.maximum(j - 1, 0)])

    @pl.when(first_tile & (valid > 0))
    def _():
        group = group_ref[0]
        slot = lax.rem(group, W_SLOTS)
        for cp in fetch(expert, slot):
            cp.wait()
        wgu_b[...] = wgu_f[slot].astype(BF16)
        wdn_b[...] = wdn_f[slot].astype(BF16)
        start_if_any(tn_ref[W_SLOTS - 2, j], lax.rem(group + W_SLOTS - 1, W_SLOTS))
        group_ref[0] = group + 1

    @pl.when(valid > 0)
    def _():
        rows = lax.broadcasted_iota(jnp.int32, (TMX, D), 0)
        x = jnp.where(rows < valid, _unpack_rows(xs_ref[...]), 0.0).astype(BF16)
        gu = _dot(x, wgu_b[...])
        a = _silu(gu[:, 0:EXPERT_FF]) * gu[:, EXPERT_FF:2 * EXPERT_FF]
        ys_ref[...] = _pack_rows(_dot(a.astype(BF16), wdn_b[...]))


def _experts_call(l, xs, te, tv, tn, w_gu, w_dn):
    n_tiles = xs.shape[0] // TMX
    grid_spec = pltpu.PrefetchScalarGridSpec(
        num_scalar_prefetch=3,
        grid=(n_tiles,),
        in_specs=[pl.BlockSpec((TMX, D // 2), lambda j, te, tv, tn: (tn[7, j], 0)),
                  pl.BlockSpec(memory_space=pl.ANY), pl.BlockSpec(memory_space=pl.ANY)],
        out_specs=pl.BlockSpec((TMX, D // 2), lambda j, te, tv, tn: (tn[7, j], 0)),
        scratch_shapes=[pltpu.VMEM((W_SLOTS, D, 2 * EXPERT_FF), F32),
                        pltpu.VMEM((W_SLOTS, EXPERT_FF, D), F32),
                        pltpu.VMEM((D, 2 * EXPERT_FF), BF16), pltpu.VMEM((EXPERT_FF, D), BF16),
                        pltpu.SemaphoreType.DMA((W_SLOTS, 2)), pltpu.SMEM((1,), jnp.int32)])
    return pl.pallas_call(
        functools.partial(_experts_kernel, l),
        grid_spec=grid_spec,
        out_shape=jax.ShapeDtypeStruct(xs.shape, U32),
        compiler_params=_cparams(("arbitrary",)),
        name="moe_experts",
    )(te, tv, tn, xs, w_gu, w_dn)


def _moe_post_kernel(x_ref, mod_ref, hp_ref, yg_ref, comb_ref, wsgu_ref, wsdn_ref, gpost_ref,
                     o_ref):
    hb = _unpack_rows(hp_ref[...]).astype(BF16)
    sgu = _dot(hb, wsgu_ref[...])
    sa = _silu(sgu[:, 0:SHARED_FF]) * sgu[:, SHARED_FF:2 * SHARED_FF]
    acc = _dot(sa.astype(BF16), wsdn_ref[...])
    comb = comb_ref[...]
    for k in range(TOP_K):
        acc = acc + comb[:, k:k + 1] * _unpack_rows(yg_ref[k])
    g2 = mod_ref[...][:, 5 * D:6 * D]
    o_ref[...] = x_ref[...] + g2 * _rms(acc, gpost_ref[...])


def _moe_post_call(l, x, mod3, mod_row, hp, yg, comb, lw):
    n_tok = x.shape[0]
    tm = TM_MOE_PRE
    return pl.pallas_call(
        _moe_post_kernel,
        grid=(n_tok // tm,),
        in_specs=[pl.BlockSpec((tm, D), lambda i: (i, 0)),
                  pl.BlockSpec((None, 1, 6 * D), lambda i: (mod_row(i * tm), 0, 0)),
                  pl.BlockSpec((tm, D // 2), lambda i: (i, 0)),
                  pl.BlockSpec((TOP_K, tm, D // 2), lambda i: (0, i, 0)),
                  pl.BlockSpec((tm, LANES), lambda i: (i, 0)),
                  _layer_spec(l, (D, 2 * SHARED_FF)), _layer_spec(l, (SHARED_FF, D)),
                  _layer_spec(l, (1, D))],
        out_specs=pl.BlockSpec((tm, D), lambda i: (i, 0)),
        out_shape=jax.ShapeDtypeStruct((n_tok, D), F32),
        compiler_params=_cparams(("arbitrary",)),
        name="moe_post",
    )(x, mod3, hp, yg, comb, lw["w_sh_gu"], lw["w_sh_down"], lw["g_post2"])


def _moe_call(l, x, mod3, mod_row, lw):
    n_tok = x.shape[0]
    n_slots = -(-(TOP_K * n_tok + N_EXPERTS * (TMX - 1)) // TMX) * TMX
    assert n_slots // TMX <= LANES
    hp, comb, dest, te, tv, tn = _moe_pre_call(l, x, mod3, mod_row, lw)
    dest = dest.reshape(TOP_K * n_tok)
    xs = _sc_scatter_rows(hp, dest, n_slots)
    ys = _experts_call(l, xs, te[0], tv[0], tn, lw["w_exp_gu"], lw["w_exp_down"])
    yg = _sc_gather_rows(ys, dest).reshape(TOP_K, n_tok, D // 2)
    return _moe_post_call(l, x, mod3, mod_row, hp, yg, comb, lw)


SC_CORES, SC_SUBCORES = 2, 16
SC_WORKERS = SC_CORES * SC_SUBCORES


def _sc_gather_rows(table, idx, chunk=64):
    n_out, width = idx.shape[0], table.shape[1]
    per_worker = n_out // SC_WORKERS
    n_chunks = per_worker // chunk
    assert per_worker * SC_WORKERS == n_out and n_chunks * chunk == per_worker
    mesh = plsc.VectorSubcoreMesh(core_axis_name="c", subcore_axis_name="s",
                                  num_cores=SC_CORES, num_subcores=SC_SUBCORES)

    @functools.partial(
        pl.kernel, mesh=mesh,
        out_type=jax.ShapeDtypeStruct((n_out, width), table.dtype),
        scratch_types=[pltpu.VMEM((chunk,), jnp.int32), pltpu.VMEM((chunk, width), table.dtype),
                       pltpu.SemaphoreType.DMA],
        name="sc_gather")
    def gather(table_hbm, idx_hbm, out_hbm, idx_v, rows_v, sem):
        base = (lax.axis_index("s") * SC_CORES + lax.axis_index("c")) * per_worker

        @pl.loop(0, n_chunks)
        def _(j):
            off = base + j * chunk
            pltpu.sync_copy(idx_hbm.at[pl.ds(off, chunk)], idx_v)
            pltpu.async_copy(table_hbm.at[idx_v], rows_v, sem).wait()
            pltpu.sync_copy(rows_v, out_hbm.at[pl.ds(off, chunk)])

    return gather(table, idx)


def _sc_scatter_rows(rows, dest, n_slots, chunk=64):
    n_tok, width = rows.shape
    per_worker = n_tok // SC_WORKERS
    n_chunks = per_worker // chunk
    assert per_worker * SC_WORKERS == n_tok and n_chunks * chunk == per_worker
    mesh = plsc.VectorSubcoreMesh(core_axis_name="c", subcore_axis_name="s",
                                  num_cores=SC_CORES, num_subcores=SC_SUBCORES)

    @functools.partial(
        pl.kernel, mesh=mesh,
        out_type=jax.ShapeDtypeStruct((n_slots, width), rows.dtype),
        scratch_types=[pltpu.VMEM((chunk,), jnp.int32), pltpu.VMEM((chunk, width), rows.dtype)],
        name="sc_scatter")
    def scatter(rows_hbm, dest_hbm, out_hbm, idx_v, rows_v):
        base = (lax.axis_index("s") * SC_CORES + lax.axis_index("c")) * per_worker

        @pl.loop(0, n_chunks)
        def _(j):
            off = base + j * chunk
            pltpu.sync_copy(rows_hbm.at[pl.ds(off, chunk)], rows_v)
            for k in range(TOP_K):
                pltpu.sync_copy(dest_hbm.at[pl.ds(k * n_tok + off, chunk)], idx_v)
                pltpu.sync_copy(rows_v, out_hbm.at[idx_v])

    return scatter(rows, dest)


def _rope_tables(t_len):
    pos = np.arange(t_len)
    row, col = pos // GRID_W, pos % GRID_W

    def tab(r):
        half = r // 2
        freq = ROPE_BASE ** (-np.arange(half, dtype=np.float64) / half)
        sign = np.concatenate([-np.ones(half), np.ones(half)])
        cs, sn = [], []
        for p in (row, col):
            ang = p[:, None].astype(np.float64) * freq[None, :]
            cs.append(np.concatenate([np.cos(ang), np.cos(ang)], axis=1))
            sn.append(np.concatenate([np.sin(ang), np.sin(ang)], axis=1) * sign[None, :])
        return np.concatenate(cs, axis=1), np.concatenate(sn, axis=1)

    c64, s64 = tab(GQA_HD // 2)
    cpe, spe = tab(MLA_ROPE // 2)
    out = (np.tile(c64, (1, 2)), np.tile(s64, (1, 2)), np.tile(cpe, (1, 4)), np.tile(spe, (1, 4)))
    return tuple(jnp.asarray(a, F32) for a in out)


def _prep_weights(p):
    n_l = p["w_in"].shape[0]

    def row(name):
        return p[name].reshape(n_l, 1, -1)

    w_uq = p["w_mla_uq"].reshape(n_l, MLA_Q_LORA, MLA_HEADS, MLA_NOPE + MLA_ROPE)
    w_uq = jnp.concatenate([w_uq[..., :MLA_NOPE].reshape(n_l, MLA_Q_LORA, -1),
                            w_uq[..., MLA_NOPE:].reshape(n_l, MLA_Q_LORA, -1)], axis=-1)
    w_ukv = p["w_mla_ukv"].reshape(n_l, MLA_KV_LORA, MLA_HEADS, MLA_NOPE + MLA_V)
    w_ukv = jnp.concatenate([w_ukv[..., :MLA_NOPE].reshape(n_l, MLA_KV_LORA, -1),
                             w_ukv[..., MLA_NOPE:].reshape(n_l, MLA_KV_LORA, -1)], axis=-1)
    w_br = p["w_br"]
    w_br_gqa = w_br[:, 1].reshape(n_l, GQA_HEADS, GQA_HD, D)[:, jnp.array(GQA_ORDER)]
    w_br = jnp.concatenate([w_br[:, 0:1], w_br_gqa.reshape(n_l, 1, BRANCH_W, D), w_br[:, 2:4]], axis=1)
    blk = np.arange(512) // GQA_HD
    return {
        "g_pre1": row("g_pre1"), "g_post1": row("g_post1"),
        "g_pre2": row("g_pre2"), "g_post2": row("g_post2"),
        "w_in": jnp.swapaxes(p["w_in"], 1, 2).astype(BF16),
        "g_mla_q": row("g_mla_q"), "w_uq": w_uq.astype(BF16),
        "g_mla_kv": row("g_mla_kv"), "w_ukv": w_ukv.astype(BF16),
        "g_gqa_q": jnp.tile(p["g_gqa_q"], (1, GQA_HEADS)).reshape(n_l, 1, -1),
        "g_gqa_k": jnp.tile(p["g_gqa_k"], (1, GQA_KV_HEADS)).reshape(n_l, 1, -1),
        "bd": jnp.asarray(blk[:, None] == blk[None, :], BF16),
        "ret_decay": p["ret_decay"],
        "g_ret": row("g_ret"),
        "diff_lambda": p["diff_lambda"], "g_diff": row("g_diff"),
        "w_br": w_br.astype(BF16), "w_out": p["w_out"].astype(BF16),
        "w_router_t": jnp.swapaxes(p["w_router"], 1, 2),
        "b_router": p["b_router"].reshape(n_l, N_EXPERTS, 1),
        "w_exp_gu": p["w_exp_gu"], "w_exp_down": p["w_exp_down"],
        "w_sh_gu": p["w_sh_gu"].astype(BF16), "w_sh_down": p["w_sh_down"].astype(BF16),
    }


def _mixers(latent, l, x, mod3, mod_row, lw, n_b, t_len, tabs=None, past=None, s0=None,
            prev_cache=None):
    lam_init = 0.8 - 0.6 * math.exp(-0.3 * l)
    outs = _inprep_call(latent, l, x, mod3, mod_row, lw, tabs, t_len,
                        None if prev_cache is None else prev_cache[:6])
    qm, kvm, gq, gkv, dq, dkv, ret, rg, gates = outs[:9]
    attn = _attn_call(l, lam_init, qm, kvm, gq, gkv, dq, dkv, lw["diff_lambda"], lw["g_diff"],
                      n_b, t_len, past)
    retn = _ret_call(latent, l, lw["ret_decay"], ret, rg, lw["g_ret"], s0, n_b, t_len,
                     None if prev_cache is None else prev_cache[6])
    mixed = _mixers_call(attn, retn, n_b, t_len // _query_block(t_len),
                         "mix_lat" if latent else "mix_ctx")
    br, r = mixed[0], mixed[1:]
    y = _merge_call(l, x, mod3, mod_row, br, r[0], gates, lw)
    cache = None if latent else tuple(outs[9:]) + (r[1],)
    return y, cache


def kernel(x_prompt, x_sample, cache_mla_ckv, cache_mla_kpe, cache_gqa_k, cache_gqa_v, cache_diff_k, cache_diff_v, state_ret, c, c_ctx, w_mod, b_mod, g_pre1, g_post1, g_pre2, g_post2, w_in, g_mla_q, w_mla_uq, g_mla_kv, w_mla_ukv, g_gqa_q, g_gqa_k, ret_decay, g_ret, diff_lambda, g_diff, w_br, w_out, w_router, b_router, w_exp_gu, w_exp_down, w_sh_gu, w_sh_down):
    params = dict(w_in=w_in, g_pre1=g_pre1, g_post1=g_post1, g_pre2=g_pre2,
                  g_post2=g_post2, g_mla_q=g_mla_q, w_mla_uq=w_mla_uq,
                  g_mla_kv=g_mla_kv, w_mla_ukv=w_mla_ukv, g_gqa_q=g_gqa_q, g_gqa_k=g_gqa_k,
                  ret_decay=ret_decay, g_ret=g_ret, diff_lambda=diff_lambda, g_diff=g_diff,
                  w_br=w_br, w_out=w_out, w_router=w_router, b_router=b_router,
                  w_exp_gu=w_exp_gu, w_exp_down=w_exp_down, w_sh_gu=w_sh_gu, w_sh_down=w_sh_down)
    n_bc, t_c, _ = x_prompt.shape
    n_bl, t_l, _ = x_sample.shape
    p_len = cache_mla_ckv.shape[2]
    tabs = _rope_tables(t_l)
    n_cond = 8
    cond = jnp.concatenate([c_ctx[None, :], c, jnp.zeros((n_cond - 1 - n_bl, D), F32)], axis=0)
    assert t_l % TM_MERGE == 0 and (t_c * n_bc) % TM_MERGE == 0
    assert t_l % TM_MOE_PRE == 0 and (t_c * n_bc) % TM_MOE_PRE == 0

    yp = x_prompt.reshape(n_bc * t_c, D)
    ys = x_sample.reshape(n_bl * t_l, D)
    cache = None
    lw = _prep_weights(params)
    for l in range(DEPTH):
        mod3 = _mod_call(l, cond, w_mod, b_mod).reshape(n_cond, 1, 6 * D)
        yp, cache = _mixers(False, l, yp, mod3, lambda i: 0, lw, n_bc, t_c, prev_cache=cache)
        yp = _moe_call(l, yp, mod3, lambda i: 0, lw)
        past_kvm = _pastkv_call(l, cache_mla_ckv[:, l].reshape(n_bl * p_len, -1),
                                jnp.tile(cache_mla_kpe[:, l].reshape(n_bl * p_len, -1), (1, 4)),
                                lw["w_ukv"])
        past_gkv = jnp.concatenate([cache_gqa_k[:, l].reshape(n_bl * p_len, -1),
                                    cache_gqa_v[:, l].reshape(n_bl * p_len, -1),
                                    jnp.ones((n_bl * p_len, LANES), F32)], axis=-1).astype(BF16)
        past_dv = jnp.concatenate([cache_diff_v[:, l], jnp.ones_like(cache_diff_v[:, l])], axis=-1)
        past_dkv = jnp.concatenate([cache_diff_k[:, l].reshape(n_bl * p_len, -1),
                                    past_dv.reshape(n_bl * p_len, -1)], axis=-1).astype(BF16)
        s0 = state_ret[:, l].reshape(n_bl, 2, RET_HEADS // 2, 2 * RET_DK, RET_DV)
        ys, _ = _mixers(True, l, ys, mod3, lambda t: 1 + t // t_l, lw, n_bl, t_l, tabs=tabs,
                        past=(past_kvm, past_gkv, past_dkv), s0=s0)
        ys = _moe_call(l, ys, mod3, lambda t: 1 + t // t_l, lw)

    ckv, kpe, gk_t, gv_t, dk_t, dv, ret_state = cache

    def time_minor(a, shape):
        a = a.reshape((n_bc, DEPTH) + shape + (t_c,))
        return jnp.transpose(a, (0, 1, a.ndim - 1) + tuple(range(2, a.ndim - 1)))

    return (yp.reshape(n_bc, t_c, D), ys.reshape(n_bl, t_l, D), ckv, kpe,
            time_minor(gk_t, (GQA_KV_HEADS, GQA_HD)), time_minor(gv_t, (GQA_KV_HEADS, GQA_HD)),
            time_minor(dk_t, (DIFF_HEADS, 2, DIFF_D)),
            dv.reshape(n_bc, DEPTH, t_c, DIFF_HEADS, DIFF_DV), ret_state)
```

```python
import functools
import math

import numpy as np
import jax
import jax.numpy as jnp
from jax import lax
from jax.experimental import pallas as pl
from jax.experimental.pallas import tpu as pltpu
from jax.experimental.pallas import tpu_sc as plsc

F32 = jnp.float32
BF16 = jnp.bfloat16

D = 1024
DEPTH = 2
GRID_W = 64
ROPE_BASE = 10000.0
EPS = 1e-6

MLA_HEADS, MLA_NOPE, MLA_ROPE, MLA_V = 8, 64, 32, 64
MLA_Q_LORA, MLA_KV_LORA = 384, 256
GQA_HEADS, GQA_KV_HEADS, GQA_HD = 8, 2, 64
RET_HEADS, RET_DK, RET_DV = 4, 64, 128
DIFF_HEADS, DIFF_D, DIFF_DV = 4, 64, 128
N_BRANCH, BRANCH_W = 4, 512
N_EXPERTS, TOP_K, N_GROUPS, TOPK_GROUPS = 32, 4, 4, 2
EXPERT_FF, SHARED_FF = 256, 256
ROUTE_SCALE = 2.5
GROUP_SIZE = N_EXPERTS // N_GROUPS

LANES = 128
HALF_LANES = 64
VMEM_LIMIT = 56 * 1024 * 1024

O_CQ, O_CKV, O_KPE, O_GQ, O_GK, O_GV = 0, 384, 640, 672, 1184, 1312
O_RQ, O_RK, O_RV, O_RG, O_DQ, O_DK, O_DV, O_GL, O_END = (
    1440, 1696, 1952, 2464, 2976, 3488, 4000, 4512, 8608)
GQA_ORDER = (0, 4, 1, 5, 2, 6, 3, 7)

KVM_W = 8 * 256
GKV_W = 3 * LANES
DKV_W = 512 + 4 * 256
LOG2E = 1.4426950408889634
TM = 256
TM_LAT = 512
TM_MERGE = 512
TQ = 512


def _query_block(t_len):
    return min(TQ, t_len)
TM_MOE_PRE = 1024
TMX = 256
W_SLOTS = 4


def _cparams(sem):
    return pltpu.CompilerParams(dimension_semantics=sem, vmem_limit_bytes=VMEM_LIMIT)


def _const_spec(shape):
    nd = len(shape)
    return pl.BlockSpec(shape, lambda *_: (0,) * nd)


def _layer_spec(l, shape):
    nd = len(shape)
    return pl.BlockSpec((None,) + tuple(shape), lambda *_: (l,) + (0,) * nd)


def _rms(x, g):
    return x * lax.rsqrt(jnp.mean(x * x, axis=-1, keepdims=True) + EPS) * g


def _dot(a, b):
    return jnp.dot(a, b, preferred_element_type=F32)


def _dot_nt(a, b):
    return lax.dot_general(a, b, (((1,), (1,)), ((), ())), preferred_element_type=F32)


def _silu(x):
    return x * jax.nn.sigmoid(x)


def _lane_iota(shape):
    return lax.broadcasted_iota(jnp.int32, shape, len(shape) - 1)


def _seg_meansq(x, bd_ref, width):
    sq = x * x
    hi = sq.astype(BF16)
    lo = (sq - hi.astype(F32)).astype(BF16)
    bd = bd_ref[0:width, 0:width]
    return (_dot(hi, bd) + _dot(lo, bd)) * (1.0 / GQA_HD)


def _rope(x, cos, sin_signed, half):
    width = x.shape[-1]
    first = (_lane_iota(x.shape) % (2 * half)) < half
    partner = jnp.where(first, pltpu.roll(x, width - half, 1), pltpu.roll(x, half, 1))
    return x * cos + partner * sin_signed


def _tile_lanes(t, reps):
    return t if reps == 1 else jnp.concatenate([t] * reps, axis=1)


def _put_layer(o_ref, val, whole_stack):
    if whole_stack:
        o_ref[0] = val
        for k in range(1, o_ref.shape[0]):
            o_ref[k] = jnp.zeros_like(val)
    else:
        o_ref[...] = val


def _store_kvm(kvm_ref, kv, kpe_b):
    ones = jnp.ones(kpe_b.shape, BF16)
    for p in range(4):
        kvm_ref[:, p * 256:p * 256 + LANES] = kv[:, p * LANES:(p + 1) * LANES].astype(BF16)
        kvm_ref[:, p * 256 + LANES:(p + 1) * 256] = kpe_b
        kvm_ref[:, 1024 + p * 256:1024 + p * 256 + LANES] = (
            kv[:, 512 + p * LANES:512 + (p + 1) * LANES].astype(BF16))
        kvm_ref[:, 1024 + p * 256 + LANES:1024 + (p + 1) * 256] = ones


def _mod_kernel(c_ref, w_ref, b_ref, o_ref):
    a = _silu(c_ref[...]).astype(BF16)
    o_ref[...] = _dot(a, w_ref[...].astype(BF16)) + b_ref[...]


def _mod_call(l, cond, w_mod, b_mod):
    n_l, _, n = w_mod.shape
    tn = 1536
    return pl.pallas_call(
        _mod_kernel,
        grid=(n // tn,),
        in_specs=[_const_spec(cond.shape),
                  pl.BlockSpec((None, D, tn), lambda j: (l, 0, j)),
                  pl.BlockSpec((None, 1, tn), lambda j: (l, 0, j))],
        out_specs=pl.BlockSpec((cond.shape[0], tn), lambda j: (0, j)),
        out_shape=jax.ShapeDtypeStruct((cond.shape[0], n), F32),
        compiler_params=_cparams(("arbitrary",)),
        name="mod",
    )(cond, w_mod, b_mod.reshape(n_l, 1, n))


def _inprep_kernel(latent, n_aliased, *refs):
    (x_ref, mod_ref, gpre_ref, win_ref, gmq_ref, wuq_ref, gmkv_ref, wukv_ref,
     ggq_ref, ggk_ref, bd_ref) = refs[:11]
    refs = refs[11:]
    if latent:
        cos64_ref, sin64_ref, cospe_ref, sinpe_ref = refs[:4]
        refs = refs[4:]
    refs = refs[n_aliased:]
    first_layer = n_aliased == 0
    (qm_ref, kvm_ref, gqo_ref, gkv_ref, dqo_ref, dkv_ref, ret_ref, rg_ref, gate_ref) = refs[:9]
    refs = refs[9:]
    if not latent:
        ckv_o, kpe_o, gk_o, gv_o, dk_o, dv_o = refs

    x = x_ref[...]
    mod = mod_ref[...]
    sh1 = mod[:, 0:D]
    sc1 = mod[:, D:2 * D]
    hb = (_rms(x, gpre_ref[...]) * (1.0 + sc1) + sh1).astype(BF16)

    def z(a, b):
        return _dot_nt(hb, win_ref[a:b, :])

    if latent:
        cos64, sin64 = cos64_ref[...], sin64_ref[...]
        cospe, sinpe = cospe_ref[...], sinpe_ref[...]

    cqn = _rms(z(O_CQ, O_CKV), gmq_ref[...]).astype(BF16)
    q = _dot(cqn, wuq_ref[...]) * ((MLA_NOPE + MLA_ROPE) ** -0.5 * LOG2E)
    q_nope, q_pe = q[:, 0:512], q[:, 512:768]
    if latent:
        q_pe = _rope(q_pe, _tile_lanes(cospe, 2), _tile_lanes(sinpe, 2), MLA_ROPE // 4)
    qm_ref[:, 0:512] = q_nope.astype(BF16)
    qm_ref[:, 512:768] = q_pe.astype(BF16)

    ckvn = _rms(z(O_CKV, O_KPE), gmkv_ref[...])
    kv = _dot(ckvn.astype(BF16), wukv_ref[...])
    kpe4 = _dot_nt(hb, jnp.concatenate([win_ref[O_KPE:O_GQ, :]] * 4, axis=0))
    if latent:
        kpe4 = _rope(kpe4, cospe, sinpe, MLA_ROPE // 4)
    else:
        _put_layer(ckv_o, ckvn, first_layer)
        _put_layer(kpe_o, kpe4[:, 0:MLA_ROPE], first_layer)
    _store_kvm(kvm_ref, kv, kpe4.astype(BF16))

    gq = _dot_nt(hb, jnp.concatenate(
        [win_ref[O_GQ + h * GQA_HD:O_GQ + (h + 1) * GQA_HD, :] for h in GQA_ORDER], axis=0))
    gq = gq * lax.rsqrt(_seg_meansq(gq, bd_ref, 512) + EPS) * ggq_ref[...]
    gk = z(O_GK, O_GV)
    gk = gk * lax.rsqrt(_seg_meansq(gk, bd_ref, LANES) + EPS) * ggk_ref[...]
    gv = z(O_GV, O_RQ)
    if latent:
        gq = _rope(gq, _tile_lanes(cos64, 4), _tile_lanes(sin64, 4), GQA_HD // 4)
        gk = _rope(gk, cos64, sin64, GQA_HD // 4)
    else:
        _put_layer(gk_o, jnp.transpose(gk), first_layer)
        _put_layer(gv_o, jnp.transpose(gv), first_layer)
    gqo_ref[...] = (gq * (GQA_HD ** -0.5 * LOG2E)).astype(BF16)
    gkv_ref[:, 0:LANES] = gk.astype(BF16)
    gkv_ref[:, LANES:2 * LANES] = gv.astype(BF16)
    gkv_ref[:, 2 * LANES:3 * LANES] = jnp.ones(gv.shape, BF16)

    dq = z(O_DQ, O_DK)
    dk = z(O_DK, O_DV)
    dv = z(O_DV, O_GL)
    if latent:
        dq = _rope(dq, _tile_lanes(cos64, 4), _tile_lanes(sin64, 4), DIFF_D // 4)
        dk = _rope(dk, _tile_lanes(cos64, 4), _tile_lanes(sin64, 4), DIFF_D // 4)
    else:
        _put_layer(dk_o, jnp.transpose(dk), first_layer)
        _put_layer(dv_o, dv, first_layer)
    dqo_ref[...] = (dq * (DIFF_D ** -0.5 * LOG2E)).astype(BF16)
    dkv_ref[:, 0:512] = dk.astype(BF16)
    for h in range(DIFF_HEADS):
        dkv_ref[:, 512 + h * 256:512 + h * 256 + LANES] = dv[:, h * LANES:(h + 1) * LANES].astype(BF16)
        dkv_ref[:, 512 + h * 256 + LANES:512 + (h + 1) * 256] = jnp.ones((dv.shape[0], LANES), BF16)

    ret_ref[:, 0:256] = z(O_RQ, O_RK).astype(BF16)
    ret_ref[:, 256:512] = (z(O_RK, O_RV) * (RET_DK ** -0.5)).astype(BF16)
    ret_ref[:, 512:1024] = z(O_RV, O_RG).astype(BF16)
    rg_ref[...] = z(O_RG, O_DQ).astype(BF16)

    for n in range(N_BRANCH):
        gate_ref[:, n * D:(n + 1) * D] = jax.nn.sigmoid(
            z(O_GL + n * D, O_GL + (n + 1) * D)).astype(BF16)


def _inprep_call(latent, l, x, mod3, mod_row, lw, tabs, t_len, prev_caches=None):
    n_tok = x.shape[0]
    tm = TM_LAT if latent else TM
    nblk = n_tok // tm
    blk_per_seq = t_len // tm

    def tok(w):
        return pl.BlockSpec((tm, w), lambda i: (i, 0))

    in_specs = [tok(D),
                pl.BlockSpec((None, 1, 6 * D), lambda i: (mod_row(i * tm), 0, 0)),
                _layer_spec(l, (1, D)),
                pl.BlockSpec((None, O_END, D), lambda i: (l, 0, 0), pipeline_mode=pl.Buffered(1)),
                _layer_spec(l, (1, MLA_Q_LORA)), _layer_spec(l, (MLA_Q_LORA, 768)),
                _layer_spec(l, (1, MLA_KV_LORA)), _layer_spec(l, (MLA_KV_LORA, 1024)),
                _layer_spec(l, (1, 512)), _layer_spec(l, (1, LANES)), _const_spec((512, 512))]
    args = [x, mod3, lw["g_pre1"], lw["w_in"], lw["g_mla_q"], lw["w_uq"], lw["g_mla_kv"],
            lw["w_ukv"], lw["g_gqa_q"], lw["g_gqa_k"], lw["bd"]]
    if latent:
        tab_spec = pl.BlockSpec((tm, LANES), lambda i: (i % blk_per_seq, 0))
        in_specs += [tab_spec] * 4
        args += list(tabs)
    widths = [768, KVM_W, 512, GKV_W, 512, DKV_W, 1024, 512, 4 * D]
    out_specs = [tok(w) for w in widths]
    out_shape = [jax.ShapeDtypeStruct((n_tok, w), BF16) for w in widths]
    aliases = {}
    if not latent:
        assert TM == t_len and (prev_caches is None) == (l == 0)
        n_seq = n_tok // t_len
        lead, at = ((DEPTH,), 0) if l == 0 else ((None,), l)

        def row_major(w):
            out_specs.append(pl.BlockSpec((None,) + lead + (t_len, w), lambda i: (i, at, 0, 0)))
            out_shape.append(jax.ShapeDtypeStruct((n_seq, DEPTH, t_len, w), F32))

        row_major(MLA_KV_LORA)
        row_major(MLA_ROPE)
        for w in (LANES, LANES, 512):
            out_specs.append(pl.BlockSpec((None,) + lead + (w, t_len), lambda i: (i, at, 0, 0)))
            out_shape.append(jax.ShapeDtypeStruct((n_seq, DEPTH, w, t_len), F32))
        row_major(512)
        if prev_caches is not None:
            n_in = len(args)
            in_specs += [pl.BlockSpec(memory_space=pl.ANY)] * len(prev_caches)
            args += list(prev_caches)
            aliases = {n_in + k: len(widths) + k for k in range(len(prev_caches))}
    return pl.pallas_call(
        functools.partial(_inprep_kernel, latent, len(aliases)),
        grid=(nblk,),
        in_specs=in_specs, out_specs=out_specs, out_shape=out_shape,
        input_output_aliases=aliases,
        compiler_params=_cparams(("arbitrary",)),
        name="inprep_lat" if latent else "inprep_ctx",
    )(*args)


def _pastkv_kernel(ckv_ref, kpe_ref, wukv_ref, o_ref):
    kv = _dot(ckv_ref[...].astype(BF16), wukv_ref[...])
    _store_kvm(o_ref, kv, kpe_ref[...].astype(BF16))


def _pastkv_call(l, ckv, kpe4, w_ukv):
    n = ckv.shape[0]
    return pl.pallas_call(
        _pastkv_kernel,
        grid=(n // TM,),
        in_specs=[pl.BlockSpec((TM, MLA_KV_LORA), lambda i: (i, 0)),
                  pl.BlockSpec((TM, LANES), lambda i: (i, 0)),
                  _layer_spec(l, (MLA_KV_LORA, 1024))],
        out_specs=pl.BlockSpec((TM, KVM_W), lambda i: (i, 0)),
        out_shape=jax.ShapeDtypeStruct((n, KVM_W), BF16),
        compiler_params=_cparams(("arbitrary",)),
        name="pastkv",
    )(ckv, kpe4, w_ukv)


def _softmax_pv(s, v_ones):
    m = jnp.max(s, axis=-1, keepdims=True)
    p = jnp.exp2(s - m).astype(BF16)
    o = _dot(p, v_ones)
    return o[:, 0:LANES] / o[:, LANES:2 * LANES]


def _attn_kernel(lam_init, n_past, qm_ref, kvm_ref, gq_ref, gkv_ref, dq_ref, dkv_ref, *refs):
    if n_past:
        past_refs, refs = refs[:3], refs[3:]
        lam_ref, gdiff_ref, o_ref = refs[:3]
        joined = refs[3:]

        @pl.when(pl.program_id(1) == 0)
        def _():
            for dst, past, new in zip(joined, past_refs, (kvm_ref, gkv_ref, dkv_ref)):
                dst[0:n_past, :] = past[...]
                dst[n_past:, :] = new[...]

        kvm_ref, gkv_ref, dkv_ref = joined
    else:
        lam_ref, gdiff_ref, o_ref = refs
    tq = qm_ref.shape[0]
    lane = _lane_iota((tq, LANES))
    low = lane < HALF_LANES
    zero = jnp.zeros((tq, LANES), BF16)

    for p in range(MLA_HEADS // 2):
        qn = qm_ref[:, p * LANES:(p + 1) * LANES]
        g = p // 2
        qpe = qm_ref[:, 512 + g * LANES:512 + (g + 1) * LANES]
        kk = kvm_ref[:, p * 256:(p + 1) * 256]
        vv = kvm_ref[:, 1024 + p * 256:1024 + (p + 1) * 256]
        outs = []
        for half in range(2):
            h = 2 * p + half
            slot = h % 4
            in_slot = (lane >= slot * MLA_ROPE) & (lane < (slot + 1) * MLA_ROPE)
            lhs = jnp.concatenate(
                [jnp.where(low if half == 0 else ~low, qn, zero),
                 jnp.where(in_slot, qpe, zero)], axis=1)
            outs.append(_softmax_pv(_dot_nt(lhs, kk), vv))
        o_ref[:, p * LANES:(p + 1) * LANES] = jnp.where(low, outs[0], outs[1]).astype(BF16)

    kk = gkv_ref[:, 0:LANES]
    vv = gkv_ref[:, LANES:3 * LANES]
    for g in range(GQA_HEADS // 2):
        qg = gq_ref[:, g * LANES:(g + 1) * LANES]
        o_lo = _softmax_pv(_dot_nt(jnp.where(low, qg, zero), kk), vv)
        o_hi = _softmax_pv(_dot_nt(jnp.where(low, zero, qg), kk), vv)
        o_ref[:, 512 + g * LANES:512 + (g + 1) * LANES] = jnp.where(low, o_lo, o_hi).astype(BF16)

    lp = lam_ref[...]
    lam = (jnp.exp(jnp.sum(lp[0:1] * lp[1:2], axis=-1, keepdims=True))
           - jnp.exp(jnp.sum(lp[2:3] * lp[3:4], axis=-1, keepdims=True)) + lam_init)
    for h in range(DIFF_HEADS):
        qh = dq_ref[:, h * LANES:(h + 1) * LANES]
        kk = dkv_ref[:, h * LANES:(h + 1) * LANES]
        vv = dkv_ref[:, 512 + h * 256:512 + (h + 1) * 256]
        a1 =_softmax_pv(_dot_nt(jnp.where(low, qh, zero), kk), vv)
        a2 = _softmax_pv(_dot_nt(jnp.where(low, zero, qh), kk), vv)
        od = _rms(a1 - lam * a2, gdiff_ref[...]) * (1.0 - lam_init)
        o_ref[:, 1024 + h * LANES:1024 + (h + 1) * LANES] = od.astype(BF16)


def _attn_call(l, lam_init, qm, kvm, gq, gkv, dq, dkv, lam_p, g_diff, n_b, t_len, past=None):
    tq = _query_block(t_len)
    nq = t_len // tq
    n_past = 0 if past is None else past[0].shape[0] // n_b

    def qspec(w):
        return pl.BlockSpec((tq, w), lambda b, i: (b * nq + i, 0))

    def kspec(w, rows=t_len):
        return pl.BlockSpec((rows, w), lambda b, i: (b, 0))

    in_specs = [qspec(768), kspec(KVM_W), qspec(512), kspec(GKV_W), qspec(512), kspec(DKV_W)]
    args = [qm, kvm, gq, gkv, dq, dkv]
    scratch = []
    if n_past:
        in_specs += [kspec(KVM_W, n_past), kspec(GKV_W, n_past), kspec(DKV_W, n_past)]
        args += list(past)
        scratch = [pltpu.VMEM((n_past + t_len, w), BF16) for w in (KVM_W, GKV_W, DKV_W)]
    in_specs += [_layer_spec(l, (4, DIFF_D)), _layer_spec(l, (1, DIFF_DV))]
    args += [lam_p, g_diff]
    return dict(kernel=functools.partial(_attn_kernel, lam_init, n_past), in_specs=in_specs, args=args,
                out_specs=[qspec(3 * BRANCH_W)],
                out_shape=[jax.ShapeDtypeStruct((n_b * t_len, 3 * BRANCH_W), BF16)],
                scratch=scratch, aliases={})


def _log_sigmoid(x):
    return jnp.minimum(x, 0.0) - jnp.log(1.0 + jnp.exp(-jnp.abs(x)))


def _log_gamma(dec_ref, l, d, h):
    return _log_sigmoid(jnp.full((1, 1), dec_ref[l, d, h], F32))


def _ret_kernel(latent, l, t_len, dec_ref, q_ref, k_ref, v_ref, rg_ref, gret_ref, *refs):
    if latent:
        s0_ref, o_ref = refs
    else:
        o_ref, st_ref = refs[-2:]
    tq = q_ref.shape[0]
    t0 = pl.program_id(1) * tq
    lane = _lane_iota((tq, LANES))
    low = lane < HALF_LANES
    zero = jnp.zeros((tq, LANES), BF16)
    t_idx = (t0 + lax.broadcasted_iota(jnp.int32, (tq, t_len), 0)).astype(F32)
    s_idx = lax.broadcasted_iota(jnp.int32, (tq, t_len), 1).astype(F32)
    dist = t_idx - s_idx
    past = dist >= 0
    diag = jnp.where(dist == 0, 1.0, 0.0)
    t_col = (t0 + lax.broadcasted_iota(jnp.int32, (tq, 1), 0)).astype(F32)

    def lg(d, h):
        return _log_gamma(dec_ref, l, d, h)

    for h in range(RET_HEADS):
        p, half = h // 2, h % 2
        qp = q_ref[:, p * LANES:(p + 1) * LANES]
        qm = jnp.where(low if half == 0 else ~low, qp, zero)
        kp = k_ref[:, p * LANES:(p + 1) * LANES]
        vh = v_ref[:, h * LANES:(h + 1) * LANES]
        lgf, lgb = lg(0, h), lg(1, h)
        dmask = jnp.exp(jnp.where(past, lgf, -lgb) * dist) + diag
        o = _dot((_dot_nt(qm, kp) * dmask).astype(BF16), vh)
        if latent:
            sf =s0_ref[0, p].astype(BF16)
            sb = s0_ref[1, p].astype(BF16)
            o = o + _dot(qm, sf) * jnp.exp(lgf * (t_col + 1.0))
            o = o + _dot(qm, sb) * jnp.exp(lgb * (float(t_len) - t_col))
        mu = jnp.mean(o, axis=-1, keepdims=True)
        oc = o - mu
        y = oc * lax.rsqrt(jnp.mean(oc * oc, axis=-1, keepdims=True) + EPS)
        y = y * gret_ref[:, h * LANES:(h + 1) * LANES]
        rg = rg_ref[:, h * LANES:(h + 1) * LANES].astype(F32)
        o_ref[:, h * LANES:(h + 1) * LANES] = (y * _silu(rg)).astype(BF16)

    if not latent:
        s_col = lax.broadcasted_iota(jnp.int32, (t_len, 1), 0).astype(F32)
        lane_t = _lane_iota((1, LANES)) < HALF_LANES
        for p in range(RET_HEADS // 2):
            kp = k_ref[:, p * LANES:(p + 1) * LANES].astype(F32)
            for d in range(2):
                lg_lane = jnp.where(lane_t, lg(d, 2 * p), lg(d, 2 * p + 1))
                expo = (float(t_len) - 1.0 - s_col) if d == 0 else s_col
                kdec_t = jnp.transpose(kp * jnp.exp(lg_lane * expo)).astype(BF16)
                for half in range(2):
                    h = 2 * p + half
                    st = _dot(kdec_t, v_ref[:, h * LANES:(h + 1) * LANES])
                    st = st[half * RET_DK:(half + 1) * RET_DK, :]
                    if l == 0:
                        st_ref[0, d, h] = st
                        for k in range(1, st_ref.shape[0]):
                            st_ref[k, d, h] = jnp.zeros_like(st)
                    else:
                        st_ref[d, h] = st


def _ret_call(latent, l, dec, ret, rg, g_ret, s0, n_b, t_len, prev_state=None):
    tq = _query_block(t_len)
    nq = t_len // tq
    aliases = {}
    assert latent or nq == 1
    in_specs = [pl.BlockSpec(memory_space=pltpu.SMEM),
                pl.BlockSpec((tq, 256), lambda b, i: (b * nq + i, 0)),
                pl.BlockSpec((t_len, 256), lambda b, i: (b, 1)),
                pl.BlockSpec((t_len, 512), lambda b, i: (b, 1)),
                pl.BlockSpec((tq, 512), lambda b, i: (b * nq + i, 0)),
                _layer_spec(l, (1, 512))]
    args = [dec, ret, ret, ret, rg, g_ret]
    out_specs = [pl.BlockSpec((tq, 512), lambda b, i: (b * nq + i, 0))]
    out_shape = [jax.ShapeDtypeStruct((n_b * t_len, 512), BF16)]
    if latent:
        in_specs.append(pl.BlockSpec((None, 2, 2, LANES, LANES), lambda b, i: (b, 0, 0, 0, 0)))
        args.append(s0)
    else:
        assert (prev_state is None) == (l == 0)
        lead, at = ((DEPTH,), 0) if l == 0 else ((None,), l)
        out_specs.append(pl.BlockSpec((None,) + lead + (2, RET_HEADS, RET_DK, RET_DV),
                                      lambda b, i: (b, at, 0, 0, 0, 0)))
        out_shape.append(jax.ShapeDtypeStruct((n_b, DEPTH, 2, RET_HEADS, RET_DK, RET_DV), F32))
        if prev_state is not None:
            aliases = {len(args): 1}
            in_specs.append(pl.BlockSpec(memory_space=pl.ANY))
            args.append(prev_state)
    return dict(kernel=functools.partial(_ret_kernel, latent, l, t_len), in_specs=in_specs, args=args,
                out_specs=out_specs, out_shape=out_shape, scratch=[], aliases=aliases)


def _mixers_call(attn, ret, n_b, nq, name):
    n_ai, n_ri = len(attn["args"]), len(ret["args"])
    n_ao, n_ro = len(attn["out_specs"]), len(ret["out_specs"])

    def kernel(*refs):
        a_in, r_in = refs[:n_ai], refs[n_ai:n_ai + n_ri]
        outs = refs[n_ai + n_ri:n_ai + n_ri + n_ao + n_ro]
        scratch = refs[n_ai + n_ri + n_ao + n_ro:]
        attn["kernel"](*a_in, *outs[:n_ao], *scratch)
        ret["kernel"](*r_in, *outs[n_ao:])

    aliases = {n_ai + k: n_ao + v for k, v in ret["aliases"].items()}
    return pl.pallas_call(
        kernel,
        grid=(n_b, nq),
        in_specs=attn["in_specs"] + ret["in_specs"],
        out_specs=attn["out_specs"] + ret["out_specs"],
        out_shape=attn["out_shape"] + ret["out_shape"],
        scratch_shapes=attn["scratch"],
        input_output_aliases=aliases,
        compiler_params=_cparams(("arbitrary", "arbitrary")),
        name=name,
    )(*attn["args"], *ret["args"])


def _merge_kernel(x_ref, mod_ref, br_ref, or_ref, gate_ref, wbr_ref, wout_ref, gpost_ref, o_ref):
    merged = None
    for n in range(N_BRANCH):
        if n < 2:
            b = br_ref[:, n * BRANCH_W:(n + 1) * BRANCH_W]
        elif n == 2:
            b = or_ref[...]
        else:
            b = br_ref[:, 2 * BRANCH_W:3 * BRANCH_W]
        t = gate_ref[:, n * D:(n + 1) * D].astype(F32) * _dot(b, wbr_ref[n])
        merged = t if merged is None else merged + t
    out = _dot(merged.astype(BF16), wout_ref[...])
    g1 = mod_ref[...][:, 2 * D:3 * D]
    o_ref[...] = x_ref[...] + g1 * _rms(out, gpost_ref[...])


def _run_stages(stages, name):
    steps = stages[0]["steps"]
    assert all(s["steps"] == steps for s in stages)
    n_in = [len(s["args"]) for s in stages]
    n_out = [len(s["out_specs"]) for s in stages]
    n_scr = [len(s["scratch"]) for s in stages]

    def kernel(*refs):
        ins, outs, scr = refs[:sum(n_in)], refs[sum(n_in):sum(n_in) + sum(n_out)], refs[sum(n_in) + sum(n_out):]
        for k, s in enumerate(stages):
            a, b, c = sum(n_in[:k]), sum(n_out[:k]), sum(n_scr[:k])
            s["kernel"](*ins[a:a + n_in[k]], *outs[b:b + n_out[k]], *scr[c:c + n_scr[k]])

    res = pl.pallas_call(
        kernel,
        grid=(steps,),
        in_specs=[sp for s in stages for sp in s["in_specs"]],
        out_specs=[sp for s in stages for sp in s["out_specs"]],
        out_shape=[sh for s in stages for sh in s["out_shape"]],
        scratch_shapes=[sc for s in stages for sc in s["scratch"]],
        compiler_params=_cparams(("arbitrary",)),
        name=name,
    )(*[a for s in stages for a in s["args"]])
    return [res[sum(n_out[:k]):sum(n_out[:k + 1])] for k in range(len(stages))]


def _merge_stage(l, x, mod3, mod_row, br, o_r, gates, lw):
    n_tok = x.shape[0]

    def tok(w):
        return pl.BlockSpec((TM_MERGE, w), lambda i: (i, 0))

    return dict(
        kernel=_merge_kernel, steps=n_tok // TM_MERGE,
        in_specs=[tok(D), pl.BlockSpec((None, 1, 6 * D), lambda i: (mod_row(i * TM_MERGE), 0, 0)),
                  tok(3 * BRANCH_W), tok(BRANCH_W), tok(4 * D),
                  _layer_spec(l, (N_BRANCH, BRANCH_W, D)), _layer_spec(l, (D, D)),
                  _layer_spec(l, (1, D))],
        args=[x, mod3, br, o_r, gates, lw["w_br"], lw["w_out"], lw["g_post1"]],
        out_specs=[tok(D)], out_shape=[jax.ShapeDtypeStruct((n_tok, D), F32)], scratch=[])


def _route(logits_t, bias):
    n = logits_t.shape[1]
    scores = jax.nn.sigmoid(logits_t)
    sel = scores + bias
    neg = -jnp.inf
    sub = lax.broadcasted_iota(jnp.int32, (GROUP_SIZE, n), 0)
    grp = []
    for g in range(N_GROUPS):
        blk = sel[g * GROUP_SIZE:(g + 1) * GROUP_SIZE]
        m1 = jnp.max(blk, axis=0, keepdims=True)
        i1 = jnp.min(jnp.where(blk == m1, sub, GROUP_SIZE), axis=0, keepdims=True)
        m2 = jnp.max(jnp.where(sub == i1, neg, blk), axis=0, keepdims=True)
        grp.append(m1 + m2)
    parts = []
    for g in range(N_GROUPS):
        beaten = jnp.zeros((1, n), jnp.int32)
        for o in range(N_GROUPS):
            if o == g:
                continue
            wins = (grp[o] > grp[g]) | (grp[o] == grp[g]) if o < g else (grp[o] > grp[g])
            beaten = beaten + wins.astype(jnp.int32)
        keep = beaten < TOPK_GROUPS
        parts.append(jnp.where(keep, sel[g * GROUP_SIZE:(g + 1) * GROUP_SIZE], neg))
    cur = jnp.concatenate(parts, axis=0)
    eidx = lax.broadcasted_iota(jnp.int32, (N_EXPERTS, n), 0)
    hits, ids, ws = [], [], []
    for _ in range(TOP_K):
        m = jnp.max(cur, axis=0, keepdims=True)
        i = jnp.min(jnp.where(cur == m, eidx, N_EXPERTS), axis=0, keepdims=True)
        hit = eidx == i
        hits.append(hit)
        ids.append(i)
        ws.append(jnp.sum(jnp.where(hit, scores, 0.0), axis=0, keepdims=True))
        cur = jnp.where(hit, neg, cur)
    wsum = ws[0] + ws[1] + ws[2] + ws[3]
    return hits, ids, [w / wsum * ROUTE_SCALE for w in ws]


U32 = jnp.uint32
HIGH16 = np.uint32(0xFFFF0000)


def _bf16_bits(v):
    return lax.bitcast_convert_type(v.astype(BF16).astype(F32), U32)


def _pack_rows(v):
    return (_bf16_bits(v[:, 0:D // 2]) >> 16) | _bf16_bits(v[:, D // 2:D])


def _unpack_rows(p):
    lo = lax.bitcast_convert_type(p << 16, F32)
    hi = lax.bitcast_convert_type(p & HIGH16, F32)
    return jnp.concatenate([lo, hi], axis=1)


def _moe_pre_kernel(x_ref, mod_ref, gpre_ref, wr_ref, br_ref, tri_ref,
                    hp_ref, comb_ref, dest_ref, te_ref, tv_ref, tn_ref, run_ref, eidx_ref, rank_ref):
    tm = x_ref.shape[0]
    step = pl.program_id(0)
    cols = pl.ds(pl.multiple_of(step * tm, tm), tm)

    @pl.when(pl.program_id(0) == 0)
    def _():
        run_ref[...] = jnp.zeros_like(run_ref)

    mod = mod_ref[...]
    sh2, sc2 = mod[:, 3 * D:4 * D], mod[:, 4 * D:5 * D]
    h = _rms(x_ref[...], gpre_ref[...]) * (1.0 + sc2) + sh2
    hp_ref[...] = _pack_rows(h)
    hb = h.astype(BF16)
    h_lo = (h - hb.astype(F32)).astype(BF16)
    wr = wr_ref[...]
    wr_hi = wr.astype(BF16)
    wr_lo = (wr - wr_hi.astype(F32)).astype(BF16)
    logits_t = _dot_nt(wr_hi, hb) + _dot_nt(wr_hi, h_lo) + _dot_nt(wr_lo, hb)
    hits, ids, ws = _route(logits_t, br_ref[...])

    picked = jnp.zeros((N_EXPERTS, tm), F32)
    for hit in hits:
        picked = jnp.where(hit, 1.0, picked)
    before = _dot(picked.astype(BF16), tri_ref[...]) + run_ref[:, 0:1]
    sub8 = lax.broadcasted_iota(jnp.int32, (8, tm), 0)
    comb8 = jnp.zeros((8, tm), F32)
    for k in range(TOP_K):
        rank = jnp.sum(jnp.where(hits[k], before, 0.0), axis=0, keepdims=True)
        eidx_ref[k:k + 1, cols] = ids[k]
        rank_ref[k:k + 1, cols] = rank.astype(jnp.int32)
        comb8 = jnp.where(sub8 == k, ws[k], comb8)
    comb_ref[...] = jnp.transpose(
        jnp.concatenate([comb8, jnp.zeros((LANES - 8, tm), F32)], axis=0))
    run_ref[...] = run_ref[...] + jnp.sum(picked, axis=1, keepdims=True)

    @pl.when(step == pl.num_programs(0) - 1)
    def _():
        _moe_plan(eidx_ref, rank_ref, run_ref, dest_ref, te_ref, tv_ref, tn_ref)


def _moe_pre_stage(l, x, mod3, mod_row, lw):
    n_tok = x.shape[0]
    tm = TM_MOE_PRE
    tri = np.arange(tm)
    tri = jnp.asarray(tri[:, None] < tri[None, :], BF16)
    tiles = jax.ShapeDtypeStruct((1, LANES), jnp.int32)
    return dict(
        kernel=_moe_pre_kernel, steps=n_tok // tm,
        in_specs=[pl.BlockSpec((tm, D), lambda i: (i, 0)),
                  pl.BlockSpec((None, 1, 6 * D), lambda i: (mod_row(i * tm), 0, 0)),
                  _layer_spec(l, (1, D)), _layer_spec(l, (N_EXPERTS, D)),
                  _layer_spec(l, (N_EXPERTS, 1)), _const_spec((tm, tm))],
        out_specs=[pl.BlockSpec((tm, D // 2), lambda i: (i, 0)),
                   pl.BlockSpec((tm, LANES), lambda i: (i, 0)),
                   _const_spec((TOP_K, n_tok)), _const_spec((1, LANES)), _const_spec((1, LANES)),
                   _const_spec((8, LANES))],
        out_shape=[jax.ShapeDtypeStruct((n_tok, D // 2), U32),
                   jax.ShapeDtypeStruct((n_tok, LANES), F32),
                   jax.ShapeDtypeStruct((TOP_K, n_tok), jnp.int32), tiles, tiles,
                   jax.ShapeDtypeStruct((8, LANES), jnp.int32)],
        scratch=[pltpu.VMEM((N_EXPERTS, LANES), F32),
                 pltpu.VMEM((TOP_K, n_tok), jnp.int32), pltpu.VMEM((TOP_K, n_tok), jnp.int32)],
        args=[x, mod3, lw["g_pre2"], lw["w_router_t"], lw["b_router"], tri])


def _moe_plan(eidx_ref, rank_ref, cnt_ref, dest_ref, te_ref, tv_ref, tn_ref):
    tm = eidx_ref.shape[1]
    cnt = cnt_ref[...]
    padded = jnp.ceil(cnt * (1.0 / TMX)) * TMX
    row = lax.broadcasted_iota(jnp.int32, cnt.shape, 0)
    incl = padded
    shift = 1
    while shift < N_EXPERTS:
        incl = incl + jnp.where(row >= shift, pltpu.roll(incl, shift, 0), 0.0)
        shift *= 2
    start = (incl - padded)[:, 0:1]
    end = incl[:, 0:1]
    erow = lax.broadcasted_iota(jnp.int32, (N_EXPERTS, tm), 0)
    for k in range(TOP_K):
        mine = erow == eidx_ref[k:k + 1, :]
        base = jnp.sum(jnp.where(mine, start, 0.0), axis=0, keepdims=True)
        dest_ref[k:k + 1, :] = rank_ref[k:k + 1, :] + base.astype(jnp.int32)

    tile0 = (_lane_iota((1, LANES)) * TMX).astype(F32)
    owner = jnp.sum(jnp.where(end <= tile0, 1.0, 0.0), axis=0, keepdims=True)
    owner = jnp.minimum(owner, N_EXPERTS - 1.0)
    erow_t = lax.broadcasted_iota(jnp.int32, (N_EXPERTS, LANES), 0).astype(F32)
    left = jnp.sum(jnp.where(erow_t == owner, cnt[:, 0:1] - (tile0 - start), 0.0),
                   axis=0, keepdims=True)
    te_ref[...] = owner.astype(jnp.int32)
    tv_ref[...] = jnp.clip(left, 0.0, float(TMX)).astype(jnp.int32)
    tn_ref[...] = jnp.full(tn_ref.shape, N_EXPERTS, jnp.int32)
    nxt = owner
    for k in range(W_SLOTS - 1):
        later = (erow_t > nxt) & (cnt[:, 0:1] > 0.0)
        nxt = jnp.min(jnp.where(later, erow_t, float(N_EXPERTS)), axis=0, keepdims=True)
        tn_ref[k:k + 1, :] = nxt.astype(jnp.int32)
    n_used = jnp.sum(jnp.where(left > 0.0, 1.0, 0.0), axis=1, keepdims=True)
    tn_ref[7:8, :] = jnp.minimum(tile0 * (1.0 / TMX), n_used - 1.0).astype(jnp.int32)


def _experts_kernel(l, te_ref, tv_ref, tn_ref, xs_ref, wgu_hbm, wdn_hbm, ys_ref,
                    wgu_f, wdn_f, wgu_b, wdn_b, sem, group_ref):
    j = pl.program_id(0)
    valid = tv_ref[j]
    expert = te_ref[j]

    def fetch(e, slot):
        return (pltpu.make_async_copy(wgu_hbm.at[l, e], wgu_f.at[slot], sem.at[slot, 0]),
                pltpu.make_async_copy(wdn_hbm.at[l, e], wdn_f.at[slot], sem.at[slot, 1]))

    def start_if_any(e, slot):
        @pl.when(e < N_EXPERTS)
        def _():
            for cp in fetch(e, slot):
                cp.start()

    @pl.when(j == 0)
    def _():
        group_ref[0] = 0
        start_if_any(expert, 0)
        for k in range(W_SLOTS - 2):
            start_if_any(tn_ref[k, 0], k + 1)

    first_tile = (j == 0) | (expert != te_ref[jnp.maximum(j - 1, 0)])

    @pl.when(first_tile & (valid > 0))
    def _():
        group = group_ref[0]
        slot = lax.rem(group, W_SLOTS)
        for cp in fetch(expert, slot):
            cp.wait()
        wgu_b[...] = wgu_f[slot].astype(BF16)
        wdn_b[...] = wdn_f[slot].astype(BF16)
        start_if_any(tn_ref[W_SLOTS - 2, j], lax.rem(group + W_SLOTS - 1, W_SLOTS))
        group_ref[0] = group + 1

    @pl.when(valid > 0)
    def _():
        rows = lax.broadcasted_iota(jnp.int32, (TMX, D), 0)
        x = jnp.where(rows < valid, _unpack_rows(xs_ref[...]), 0.0).astype(BF16)
        gu = _dot(x, wgu_b[...])
        a = _silu(gu[:, 0:EXPERT_FF]) * gu[:, EXPERT_FF:2 * EXPERT_FF]
        ys_ref[...] = _pack_rows(_dot(a.astype(BF16), wdn_b[...]))


def _experts_call(l, xs, te, tv, tn, w_gu, w_dn):
    n_tiles = xs.shape[0] // TMX
    grid_spec = pltpu.PrefetchScalarGridSpec(
        num_scalar_prefetch=3,
        grid=(n_tiles,),
        in_specs=[pl.BlockSpec((TMX, D // 2), lambda j, te, tv, tn: (tn[7, j], 0)),
                  pl.BlockSpec(memory_space=pl.ANY), pl.BlockSpec(memory_space=pl.ANY)],
        out_specs=pl.BlockSpec((TMX, D // 2), lambda j, te, tv, tn: (tn[7, j], 0)),
        scratch_shapes=[pltpu.VMEM((W_SLOTS, D, 2 * EXPERT_FF), F32),
                        pltpu.VMEM((W_SLOTS, EXPERT_FF, D), F32),
                        pltpu.VMEM((D, 2 * EXPERT_FF), BF16), pltpu.VMEM((EXPERT_FF, D), BF16),
                        pltpu.SemaphoreType.DMA((W_SLOTS, 2)), pltpu.SMEM((1,), jnp.int32)])
    return pl.pallas_call(
        functools.partial(_experts_kernel, l),
        grid_spec=grid_spec,
        out_shape=jax.ShapeDtypeStruct(xs.shape, U32),
        compiler_params=_cparams(("arbitrary",)),
        name="moe_experts",
    )(te, tv, tn, xs, w_gu, w_dn)


def _moe_post_kernel(x_ref, mod_ref, hp_ref, yg_ref, comb_ref, wsgu_ref, wsdn_ref, gpost_ref,
                     o_ref):
    hb = _unpack_rows(hp_ref[...]).astype(BF16)
    sgu = _dot(hb, wsgu_ref[...])
    sa = _silu(sgu[:, 0:SHARED_FF]) * sgu[:, SHARED_FF:2 * SHARED_FF]
    acc = _dot(sa.astype(BF16), wsdn_ref[...])
    comb = comb_ref[...]
    for k in range(TOP_K):
        acc = acc + comb[:, k:k + 1] * _unpack_rows(yg_ref[k])
    g2 = mod_ref[...][:, 5 * D:6 * D]
    o_ref[...] = x_ref[...] + g2 * _rms(acc, gpost_ref[...])


def _moe_post_call(l, x, mod3, mod_row, hp, yg, comb, lw):
    n_tok = x.shape[0]
    tm = TM_MOE_PRE
    return pl.pallas_call(
        _moe_post_kernel,
        grid=(n_tok // tm,),
        in_specs=[pl.BlockSpec((tm, D), lambda i: (i, 0)),
                  pl.BlockSpec((None, 1, 6 * D), lambda i: (mod_row(i * tm), 0, 0)),
                  pl.BlockSpec((tm, D // 2), lambda i: (i, 0)),
                  pl.BlockSpec((TOP_K, tm, D // 2), lambda i: (0, i, 0)),
                  pl.BlockSpec((tm, LANES), lambda i: (i, 0)),
                  _layer_spec(l, (D, 2 * SHARED_FF)), _layer_spec(l, (SHARED_FF, D)),
                  _layer_spec(l, (1, D))],
        out_specs=pl.BlockSpec((tm, D), lambda i: (i, 0)),
        out_shape=jax.ShapeDtypeStruct((n_tok, D), F32),
        compiler_params=_cparams(("arbitrary",)),
        name="moe_post",
    )(x, mod3, hp, yg, comb, lw["w_sh_gu"], lw["w_sh_down"], lw["g_post2"])


def _moe_call(l, x, mod3, mod_row, lw, pre=None):
    n_tok = x.shape[0]
    n_slots = -(-(TOP_K * n_tok + N_EXPERTS * (TMX - 1)) // TMX) * TMX
    assert n_slots // TMX <= LANES
    if pre is None:
        (pre,) = _run_stages([_moe_pre_stage(l, x, mod3, mod_row, lw)], "moe_pre")
    hp, comb, dest, te, tv, tn = pre
    dest = dest.reshape(TOP_K * n_tok)
    xs = _sc_scatter_rows(hp, dest, n_slots)
    ys = _experts_call(l, xs, te[0], tv[0], tn, lw["w_exp_gu"], lw["w_exp_down"])
    yg = _sc_gather_rows(ys, dest).reshape(TOP_K, n_tok, D // 2)
    return _moe_post_call(l, x, mod3, mod_row, hp, yg, comb, lw)


SC_CORES, SC_SUBCORES = 2, 16
SC_WORKERS = SC_CORES * SC_SUBCORES


def _sc_gather_rows(table, idx, chunk=64):
    n_out, width = idx.shape[0], table.shape[1]
    per_worker = n_out // SC_WORKERS
    n_chunks = per_worker // chunk
    assert per_worker * SC_WORKERS == n_out and n_chunks * chunk == per_worker
    mesh = plsc.VectorSubcoreMesh(core_axis_name="c", subcore_axis_name="s",
                                  num_cores=SC_CORES, num_subcores=SC_SUBCORES)

    @functools.partial(
        pl.kernel, mesh=mesh,
        out_type=jax.ShapeDtypeStruct((n_out, width), table.dtype),
        scratch_types=[pltpu.VMEM((chunk,), jnp.int32), pltpu.VMEM((chunk, width), table.dtype),
                       pltpu.SemaphoreType.DMA],
        name="sc_gather")
    def gather(table_hbm, idx_hbm, out_hbm, idx_v, rows_v, sem):
        base = (lax.axis_index("s") * SC_CORES + lax.axis_index("c")) * per_worker

        @pl.loop(0, n_chunks)
        def _(j):
            off = base + j * chunk
            pltpu.sync_copy(idx_hbm.at[pl.ds(off, chunk)], idx_v)
            pltpu.async_copy(table_hbm.at[idx_v], rows_v, sem).wait()
            pltpu.sync_copy(rows_v, out_hbm.at[pl.ds(off, chunk)])

    return gather(table, idx)


def _sc_scatter_rows(rows, dest, n_slots, chunk=64):
    n_tok, width = rows.shape
    per_worker = n_tok // SC_WORKERS
    n_chunks = per_worker // chunk
    assert per_worker * SC_WORKERS == n_tok and n_chunks * chunk == per_worker
    mesh = plsc.VectorSubcoreMesh(core_axis_name="c", subcore_axis_name="s",
                                  num_cores=SC_CORES, num_subcores=SC_SUBCORES)

    @functools.partial(
        pl.kernel, mesh=mesh,
        out_type=jax.ShapeDtypeStruct((n_slots, width), rows.dtype),
        scratch_types=[pltpu.VMEM((chunk,), jnp.int32), pltpu.VMEM((chunk, width), rows.dtype)],
        name="sc_scatter")
    def scatter(rows_hbm, dest_hbm, out_hbm, idx_v, rows_v):
        base = (lax.axis_index("s") * SC_CORES + lax.axis_index("c")) * per_worker

        @pl.loop(0, n_chunks)
        def _(j):
            off = base + j * chunk
            pltpu.sync_copy(rows_hbm.at[pl.ds(off, chunk)], rows_v)
            for k in range(TOP_K):
                pltpu.sync_copy(dest_hbm.at[pl.ds(k * n_tok + off, chunk)], idx_v)
                pltpu.sync_copy(rows_v, out_hbm.at[idx_v])

    return scatter(rows, dest)


def _rope_tables(t_len):
    pos = np.arange(t_len)
    row, col = pos // GRID_W, pos % GRID_W

    def tab(r):
        half = r // 2
        freq = ROPE_BASE ** (-np.arange(half, dtype=np.float64) / half)
        sign = np.concatenate([-np.ones(half), np.ones(half)])
        cs, sn = [], []
        for p in (row, col):
            ang = p[:, None].astype(np.float64) * freq[None, :]
            cs.append(np.concatenate([np.cos(ang), np.cos(ang)], axis=1))
            sn.append(np.concatenate([np.sin(ang), np.sin(ang)], axis=1) * sign[None, :])
        return np.concatenate(cs, axis=1), np.concatenate(sn, axis=1)

    c64, s64 = tab(GQA_HD // 2)
    cpe, spe = tab(MLA_ROPE // 2)
    out = (np.tile(c64, (1, 2)), np.tile(s64, (1, 2)), np.tile(cpe, (1, 4)), np.tile(spe, (1, 4)))
    return tuple(jnp.asarray(a, F32) for a in out)


def _prep_weights(p):
    n_l = p["w_in"].shape[0]

    def row(name):
        return p[name].reshape(n_l, 1, -1)

    w_uq = p["w_mla_uq"].reshape(n_l, MLA_Q_LORA, MLA_HEADS, MLA_NOPE + MLA_ROPE)
    w_uq = jnp.concatenate([w_uq[..., :MLA_NOPE].reshape(n_l, MLA_Q_LORA, -1),
                            w_uq[..., MLA_NOPE:].reshape(n_l, MLA_Q_LORA, -1)], axis=-1)
    w_ukv = p["w_mla_ukv"].reshape(n_l, MLA_KV_LORA, MLA_HEADS, MLA_NOPE + MLA_V)
    w_ukv = jnp.concatenate([w_ukv[..., :MLA_NOPE].reshape(n_l, MLA_KV_LORA, -1),
                             w_ukv[..., MLA_NOPE:].reshape(n_l, MLA_KV_LORA, -1)], axis=-1)
    w_br = p["w_br"]
    w_br_gqa = w_br[:, 1].reshape(n_l, GQA_HEADS, GQA_HD, D)[:, jnp.array(GQA_ORDER)]
    w_br = jnp.concatenate([w_br[:, 0:1], w_br_gqa.reshape(n_l, 1, BRANCH_W, D), w_br[:, 2:4]], axis=1)
    blk = np.arange(512) // GQA_HD
    return {
        "g_pre1": row("g_pre1"), "g_post1": row("g_post1"),
        "g_pre2": row("g_pre2"), "g_post2": row("g_post2"),
        "w_in": jnp.swapaxes(p["w_in"], 1, 2).astype(BF16),
        "g_mla_q": row("g_mla_q"), "w_uq": w_uq.astype(BF16),
        "g_mla_kv": row("g_mla_kv"), "w_ukv": w_ukv.astype(BF16),
        "g_gqa_q": jnp.tile(p["g_gqa_q"], (1, GQA_HEADS)).reshape(n_l, 1, -1),
        "g_gqa_k": jnp.tile(p["g_gqa_k"], (1, GQA_KV_HEADS)).reshape(n_l, 1, -1),
        "bd": jnp.asarray(blk[:, None] == blk[None, :], BF16),
        "ret_decay": p["ret_decay"],
        "g_ret": row("g_ret"),
        "diff_lambda": p["diff_lambda"], "g_diff": row("g_diff"),
        "w_br": w_br.astype(BF16), "w_out": p["w_out"].astype(BF16),
        "w_router_t": jnp.swapaxes(p["w_router"], 1, 2),
        "b_router": p["b_router"].reshape(n_l, N_EXPERTS, 1),
        "w_exp_gu": p["w_exp_gu"], "w_exp_down": p["w_exp_down"],
        "w_sh_gu": p["w_sh_gu"].astype(BF16), "w_sh_down": p["w_sh_down"].astype(BF16),
    }


def _mixers(latent, l, x, mod3, mod_row, lw, n_b, t_len, tabs=None, past=None, s0=None,
            prev_cache=None):
    lam_init = 0.8 - 0.6 * math.exp(-0.3 * l)
    outs = _inprep_call(latent, l, x, mod3, mod_row, lw, tabs, t_len,
                        None if prev_cache is None else prev_cache[:6])
    qm, kvm, gq, gkv, dq, dkv, ret, rg, gates = outs[:9]
    attn = _attn_call(l, lam_init, qm, kvm, gq, gkv, dq, dkv, lw["diff_lambda"], lw["g_diff"],
                      n_b, t_len, past)
    retn = _ret_call(latent, l, lw["ret_decay"], ret, rg, lw["g_ret"], s0, n_b, t_len,
                     None if prev_cache is None else prev_cache[6])
    mixed = _mixers_call(attn, retn, n_b, t_len // _query_block(t_len),
                         "mix_lat" if latent else "mix_ctx")
    br, r = mixed[0], mixed[1:]
    merge = _merge_stage(l, x, mod3, mod_row, br, r[0], gates, lw)
    if latent:
        return merge, None
    ((y,),) = _run_stages([merge], "merge")
    return y, tuple(outs[9:]) + (r[1],)


def kernel(x_prompt, x_sample, cache_mla_ckv, cache_mla_kpe, cache_gqa_k, cache_gqa_v, cache_diff_k, cache_diff_v, state_ret, c, c_ctx, w_mod, b_mod, g_pre1, g_post1, g_pre2, g_post2, w_in, g_mla_q, w_mla_uq, g_mla_kv, w_mla_ukv, g_gqa_q, g_gqa_k, ret_decay, g_ret, diff_lambda, g_diff, w_br, w_out, w_router, b_router, w_exp_gu, w_exp_down, w_sh_gu, w_sh_down):
    params = dict(w_in=w_in, g_pre1=g_pre1, g_post1=g_post1, g_pre2=g_pre2,
                  g_post2=g_post2, g_mla_q=g_mla_q, w_mla_uq=w_mla_uq,
                  g_mla_kv=g_mla_kv, w_mla_ukv=w_mla_ukv, g_gqa_q=g_gqa_q, g_gqa_k=g_gqa_k,
                  ret_decay=ret_decay, g_ret=g_ret, diff_lambda=diff_lambda, g_diff=g_diff,
                  w_br=w_br, w_out=w_out, w_router=w_router, b_router=b_router,
                  w_exp_gu=w_exp_gu, w_exp_down=w_exp_down, w_sh_gu=w_sh_gu, w_sh_down=w_sh_down)
    n_bc, t_c, _ = x_prompt.shape
    n_bl, t_l, _ = x_sample.shape
    p_len = cache_mla_ckv.shape[2]
    tabs = _rope_tables(t_l)
    n_cond = 8
    cond = jnp.concatenate([c_ctx[None, :], c, jnp.zeros((n_cond - 1 - n_bl, D), F32)], axis=0)
    assert t_l % TM_MERGE == 0 and (t_c * n_bc) % TM_MERGE == 0
    assert t_l % TM_MOE_PRE == 0 and (t_c * n_bc) % TM_MOE_PRE == 0

    yp = x_prompt.reshape(n_bc * t_c, D)
    ys = x_sample.reshape(n_bl * t_l, D)
    cache = None
    lw = _prep_weights(params)
    for l in range(DEPTH):
        mod3 = _mod_call(l, cond, w_mod, b_mod).reshape(n_cond, 1, 6 * D)
        yp, cache = _mixers(False, l, yp, mod3, lambda i: 0, lw, n_bc, t_c, prev_cache=cache)
        past_kvm = _pastkv_call(l, cache_mla_ckv[:, l].reshape(n_bl * p_len, -1),
                                jnp.tile(cache_mla_kpe[:, l].reshape(n_bl * p_len, -1), (1, 4)),
                                lw["w_ukv"])
        past_gkv = jnp.concatenate([cache_gqa_k[:, l].reshape(n_bl * p_len, -1),
                                    cache_gqa_v[:, l].reshape(n_bl * p_len, -1),
                                    jnp.ones((n_bl * p_len, LANES), F32)], axis=-1).astype(BF16)
        past_dv = jnp.concatenate([cache_diff_v[:, l], jnp.ones_like(cache_diff_v[:, l])], axis=-1)
        past_dkv = jnp.concatenate([cache_diff_k[:, l].reshape(n_bl * p_len, -1),
                                    past_dv.reshape(n_bl * p_len, -1)], axis=-1).astype(BF16)
        s0 = state_ret[:, l].reshape(n_bl, 2, RET_HEADS // 2, 2 * RET_DK, RET_DV)
        merge_lat, _ = _mixers(True, l, ys, mod3, lambda t: 1 + t // t_l, lw, n_bl, t_l, tabs=tabs,
                               past=(past_kvm, past_gkv, past_dkv), s0=s0)
        pre_ctx, (ys,) = _run_stages([_moe_pre_stage(l, yp, mod3, lambda i: 0, lw), merge_lat],
                                     "pre_ctx_merge_lat")
        yp = _moe_call(l, yp, mod3, lambda i: 0, lw, pre=pre_ctx)
        ys = _moe_call(l, ys, mod3, lambda t: 1 + t // t_l, lw)

    ckv, kpe, gk_t, gv_t, dk_t, dv, ret_state = cache

    def time_minor(a, shape):
        a = a.reshape((n_bc, DEPTH) + shape + (t_c,))
        return jnp.transpose(a, (0, 1, a.ndim - 1) + tuple(range(2, a.ndim - 1)))

    return (yp.reshape(n_bc, t_c, D), ys.reshape(n_bl, t_l, D), ckv, kpe,
            time_minor(gk_t, (GQA_KV_HEADS, GQA_HD)), time_minor(gv_t, (GQA_KV_HEADS, GQA_HD)),
            time_minor(dk_t, (DIFF_HEADS, 2, DIFF_D)),
            dv.reshape(n_bc, DEPTH, t_c, DIFF_HEADS, DIFF_DV), ret_state)
```

```python
import functools
import math

import numpy as np
import jax
import jax.numpy as jnp
from jax import lax
from jax.experimental import pallas as pl
from jax.experimental.pallas import tpu as pltpu
from jax.experimental.pallas import tpu_sc as plsc

F32 = jnp.float32
BF16 = jnp.bfloat16

D = 1024
DEPTH = 2
GRID_W = 64
ROPE_BASE = 10000.0
EPS = 1e-6

MLA_HEADS, MLA_NOPE, MLA_ROPE, MLA_V = 8, 64, 32, 64
MLA_Q_LORA, MLA_KV_LORA = 384, 256
GQA_HEADS, GQA_KV_HEADS, GQA_HD = 8, 2, 64
RET_HEADS, RET_DK, RET_DV = 4, 64, 128
DIFF_HEADS, DIFF_D, DIFF_DV = 4, 64, 128
N_BRANCH, BRANCH_W = 4, 512
N_EXPERTS, TOP_K, N_GROUPS, TOPK_GROUPS = 32, 4, 4, 2
EXPERT_FF, SHARED_FF = 256, 256
ROUTE_SCALE = 2.5
GROUP_SIZE = N_EXPERTS // N_GROUPS

LANES = 128
HALF_LANES = 64
VMEM_LIMIT = 56 * 1024 * 1024

O_CQ, O_CKV, O_KPE, O_GQ, O_GK, O_GV = 0, 384, 640, 672, 1184, 1312
O_RQ, O_RK, O_RV, O_RG, O_DQ, O_DK, O_DV, O_GL, O_END = (
    1440, 1696, 1952, 2464, 2976, 3488, 4000, 4512, 8608)
GQA_ORDER = (0, 4, 1, 5, 2, 6, 3, 7)

KVM_W = 8 * 256
GKV_W = 3 * LANES
DKV_W = 512 + 4 * 256
LOG2E = 1.4426950408889634
TM = 256
TM_LAT = 512
TM_MERGE = 512
TQ = 512


def _query_block(t_len):
    return min(TQ, t_len)
TM_MOE_PRE = 1024
TMX = 256
W_SLOTS = 4


def _cparams(sem):
    return pltpu.CompilerParams(dimension_semantics=sem, vmem_limit_bytes=VMEM_LIMIT)


def _const_spec(shape):
    nd = len(shape)
    return pl.BlockSpec(shape, lambda *_: (0,) * nd)


def _layer_spec(l, shape):
    nd = len(shape)
    return pl.BlockSpec((None,) + tuple(shape), lambda *_: (l,) + (0,) * nd)


def _rms(x, g):
    return x * lax.rsqrt(jnp.mean(x * x, axis=-1, keepdims=True) + EPS) * g


def _dot(a, b):
    return jnp.dot(a, b, preferred_element_type=F32)


def _dot_nt(a, b):
    return lax.dot_general(a, b, (((1,), (1,)), ((), ())), preferred_element_type=F32)


def _silu(x):
    return x * jax.nn.sigmoid(x)


def _lane_iota(shape):
    return lax.broadcasted_iota(jnp.int32, shape, len(shape) - 1)


def _seg_meansq(x, bd_ref, width):
    sq = x * x
    hi = sq.astype(BF16)
    lo = (sq - hi.astype(F32)).astype(BF16)
    bd = bd_ref[0:width, 0:width]
    return (_dot(hi, bd) + _dot(lo, bd)) * (1.0 / GQA_HD)


def _rope(x, cos, sin_signed, half):
    width = x.shape[-1]
    first = (_lane_iota(x.shape) % (2 * half)) < half
    partner = jnp.where(first, pltpu.roll(x, width - half, 1), pltpu.roll(x, half, 1))
    return x * cos + partner * sin_signed


def _tile_lanes(t, reps):
    return t if reps == 1 else jnp.concatenate([t] * reps, axis=1)


def _put_layer(o_ref, val, whole_stack):
    if whole_stack:
        o_ref[0] = val
        for k in range(1, o_ref.shape[0]):
            o_ref[k] = jnp.zeros_like(val)
    else:
        o_ref[...] = val


def _store_kvm(kvm_ref, kv, kpe_b):
    ones = jnp.ones(kpe_b.shape, BF16)
    for p in range(4):
        kvm_ref[:, p * 256:p * 256 + LANES] = kv[:, p * LANES:(p + 1) * LANES].astype(BF16)
        kvm_ref[:, p * 256 + LANES:(p + 1) * 256] = kpe_b
        kvm_ref[:, 1024 + p * 256:1024 + p * 256 + LANES] = (
            kv[:, 512 + p * LANES:512 + (p + 1) * LANES].astype(BF16))
        kvm_ref[:, 1024 + p * 256 + LANES:1024 + (p + 1) * 256] = ones


def _mod_kernel(c_ref, w_ref, b_ref, o_ref):
    a = _silu(c_ref[...]).astype(BF16)
    o_ref[...] = _dot(a, w_ref[...].astype(BF16)) + b_ref[...]


def _mod_call(l, cond, w_mod, b_mod):
    n_l, _, n = w_mod.shape
    tn = 1536
    return pl.pallas_call(
        _mod_kernel,
        grid=(n // tn,),
        in_specs=[_const_spec(cond.shape),
                  pl.BlockSpec((None, D, tn), lambda j: (l, 0, j)),
                  pl.BlockSpec((None, 1, tn), lambda j: (l, 0, j))],
        out_specs=pl.BlockSpec((cond.shape[0], tn), lambda j: (0, j)),
        out_shape=jax.ShapeDtypeStruct((cond.shape[0], n), F32),
        compiler_params=_cparams(("arbitrary",)),
        name="mod",
    )(cond, w_mod, b_mod.reshape(n_l, 1, n))


def _inprep_kernel(latent, n_aliased, *refs):
    (x_ref, mod_ref, gpre_ref, win_ref, gmq_ref, wuq_ref, gmkv_ref, wukv_ref,
     ggq_ref, ggk_ref, bd_ref) = refs[:11]
    refs = refs[11:]
    if latent:
        cos64_ref, sin64_ref, cospe_ref, sinpe_ref = refs[:4]
        refs = refs[4:]
    refs = refs[n_aliased:]
    first_layer = n_aliased == 0
    (qm_ref, kvm_ref, gqo_ref, gkv_ref, dqo_ref, dkv_ref, ret_ref, rg_ref, gate_ref) = refs[:9]
    refs = refs[9:]
    if not latent:
        ckv_o, kpe_o, gk_o, gv_o, dk_o, dv_o = refs

    x = x_ref[...]
    mod = mod_ref[...]
    sh1 = mod[:, 0:D]
    sc1 = mod[:, D:2 * D]
    hb = (_rms(x, gpre_ref[...]) * (1.0 + sc1) + sh1).astype(BF16)

    def z(a, b):
        return _dot_nt(hb, win_ref[a:b, :])

    if latent:
        cos64, sin64 = cos64_ref[...], sin64_ref[...]
        cospe, sinpe = cospe_ref[...], sinpe_ref[...]

    cqn = _rms(z(O_CQ, O_CKV), gmq_ref[...]).astype(BF16)
    q = _dot(cqn, wuq_ref[...]) * ((MLA_NOPE + MLA_ROPE) ** -0.5 * LOG2E)
    q_nope, q_pe = q[:, 0:512], q[:, 512:768]
    if latent:
        q_pe = _rope(q_pe, _tile_lanes(cospe, 2), _tile_lanes(sinpe, 2), MLA_ROPE // 4)
    qm_ref[:, 0:512] = q_nope.astype(BF16)
    qm_ref[:, 512:768] = q_pe.astype(BF16)

    ckvn = _rms(z(O_CKV, O_KPE), gmkv_ref[...])
    kv = _dot(ckvn.astype(BF16), wukv_ref[...])
    kpe4 = _dot_nt(hb, jnp.concatenate([win_ref[O_KPE:O_GQ, :]] * 4, axis=0))
    if latent:
        kpe4 = _rope(kpe4, cospe, sinpe, MLA_ROPE // 4)
    else:
        _put_layer(ckv_o, ckvn, first_layer)
        _put_layer(kpe_o, kpe4[:, 0:MLA_ROPE], first_layer)
    _store_kvm(kvm_ref, kv, kpe4.astype(BF16))

    gq = _dot_nt(hb, jnp.concatenate(
        [win_ref[O_GQ + h * GQA_HD:O_GQ + (h + 1) * GQA_HD, :] for h in GQA_ORDER], axis=0))
    gq = gq * lax.rsqrt(_seg_meansq(gq, bd_ref, 512) + EPS) * ggq_ref[...]
    gk = z(O_GK, O_GV)
    gk = gk * lax.rsqrt(_seg_meansq(gk, bd_ref, LANES) + EPS) * ggk_ref[...]
    gv = z(O_GV, O_RQ)
    if latent:
        gq = _rope(gq, _tile_lanes(cos64, 4), _tile_lanes(sin64, 4), GQA_HD // 4)
        gk = _rope(gk, cos64, sin64, GQA_HD // 4)
    else:
        _put_layer(gk_o, jnp.transpose(gk), first_layer)
        _put_layer(gv_o, jnp.transpose(gv), first_layer)
    gqo_ref[...] = (gq * (GQA_HD ** -0.5 * LOG2E)).astype(BF16)
    gkv_ref[:, 0:LANES] = gk.astype(BF16)
    gkv_ref[:, LANES:2 * LANES] = gv.astype(BF16)
    gkv_ref[:, 2 * LANES:3 * LANES] = jnp.ones(gv.shape, BF16)

    dq = z(O_DQ, O_DK)
    dk = z(O_DK, O_DV)
    dv = z(O_DV, O_GL)
    if latent:
        dq = _rope(dq, _tile_lanes(cos64, 4), _tile_lanes(sin64, 4), DIFF_D // 4)
        dk = _rope(dk, _tile_lanes(cos64, 4), _tile_lanes(sin64, 4), DIFF_D // 4)
    else:
        _put_layer(dk_o, jnp.transpose(dk), first_layer)
        _put_layer(dv_o, dv, first_layer)
    dqo_ref[...] = (dq * (DIFF_D ** -0.5 * LOG2E)).astype(BF16)
    dkv_ref[:, 0:512] = dk.astype(BF16)
    for h in range(DIFF_HEADS):
        dkv_ref[:, 512 + h * 256:512 + h * 256 + LANES] = dv[:, h * LANES:(h + 1) * LANES].astype(BF16)
        dkv_ref[:, 512 + h * 256 + LANES:512 + (h + 1) * 256] = jnp.ones((dv.shape[0], LANES), BF16)

    ret_ref[:, 0:256] = z(O_RQ, O_RK).astype(BF16)
    ret_ref[:, 256:512] = (z(O_RK, O_RV) * (RET_DK ** -0.5)).astype(BF16)
    ret_ref[:, 512:1024] = z(O_RV, O_RG).astype(BF16)
    rg_ref[...] = z(O_RG, O_DQ).astype(BF16)

    for n in range(N_BRANCH):
        gate_ref[:, n * D:(n + 1) * D] = jax.nn.sigmoid(
            z(O_GL + n * D, O_GL + (n + 1) * D)).astype(BF16)


def _inprep_call(latent, l, x, mod3, mod_row, lw, tabs, t_len, prev_caches=None):
    n_tok = x.shape[0]
    tm = TM_LAT if latent else TM
    nblk = n_tok // tm
    blk_per_seq = t_len // tm

    def tok(w):
        return pl.BlockSpec((tm, w), lambda i: (i, 0))

    in_specs = [tok(D),
                pl.BlockSpec((None, 1, 6 * D), lambda i: (mod_row(i * tm), 0, 0)),
                _layer_spec(l, (1, D)),
                pl.BlockSpec((None, O_END, D), lambda i: (l, 0, 0), pipeline_mode=pl.Buffered(1)),
                _layer_spec(l, (1, MLA_Q_LORA)), _layer_spec(l, (MLA_Q_LORA, 768)),
                _layer_spec(l, (1, MLA_KV_LORA)), _layer_spec(l, (MLA_KV_LORA, 1024)),
                _layer_spec(l, (1, 512)), _layer_spec(l, (1, LANES)), _const_spec((512, 512))]
    args = [x, mod3, lw["g_pre1"], lw["w_in"], lw["g_mla_q"], lw["w_uq"], lw["g_mla_kv"],
            lw["w_ukv"], lw["g_gqa_q"], lw["g_gqa_k"], lw["bd"]]
    if latent:
        tab_spec = pl.BlockSpec((tm, LANES), lambda i: (i % blk_per_seq, 0))
        in_specs += [tab_spec] * 4
        args += list(tabs)
    widths = [768, KVM_W, 512, GKV_W, 512, DKV_W, 1024, 512, 4 * D]
    out_specs = [tok(w) for w in widths]
    out_shape = [jax.ShapeDtypeStruct((n_tok, w), BF16) for w in widths]
    aliases = {}
    if not latent:
        assert TM == t_len and (prev_caches is None) == (l == 0)
        n_seq = n_tok // t_len
        lead, at = ((DEPTH,), 0) if l == 0 else ((None,), l)

        def row_major(w):
            out_specs.append(pl.BlockSpec((None,) + lead + (t_len, w), lambda i: (i, at, 0, 0)))
            out_shape.append(jax.ShapeDtypeStruct((n_seq, DEPTH, t_len, w), F32))

        row_major(MLA_KV_LORA)
        row_major(MLA_ROPE)
        for w in (LANES, LANES, 512):
            out_specs.append(pl.BlockSpec((None,) + lead + (w, t_len), lambda i: (i, at, 0, 0)))
            out_shape.append(jax.ShapeDtypeStruct((n_seq, DEPTH, w, t_len), F32))
        row_major(512)
        if prev_caches is not None:
            n_in = len(args)
            in_specs += [pl.BlockSpec(memory_space=pl.ANY)] * len(prev_caches)
            args += list(prev_caches)
            aliases = {n_in + k: len(widths) + k for k in range(len(prev_caches))}
    return pl.pallas_call(
        functools.partial(_inprep_kernel, latent, len(aliases)),
        grid=(nblk,),
        in_specs=in_specs, out_specs=out_specs, out_shape=out_shape,
        input_output_aliases=aliases,
        compiler_params=_cparams(("arbitrary",)),
        name="inprep_lat" if latent else "inprep_ctx",
    )(*args)


def _pastkv_kernel(ckv_ref, kpe_ref, wukv_ref, o_ref):
    kv = _dot(ckv_ref[...].astype(BF16), wukv_ref[...])
    _store_kvm(o_ref, kv, kpe_ref[...].astype(BF16))


def _pastkv_call(l, ckv, kpe4, w_ukv):
    n = ckv.shape[0]
    return pl.pallas_call(
        _pastkv_kernel,
        grid=(n // TM,),
        in_specs=[pl.BlockSpec((TM, MLA_KV_LORA), lambda i: (i, 0)),
                  pl.BlockSpec((TM, LANES), lambda i: (i, 0)),
                  _layer_spec(l, (MLA_KV_LORA, 1024))],
        out_specs=pl.BlockSpec((TM, KVM_W), lambda i: (i, 0)),
        out_shape=jax.ShapeDtypeStruct((n, KVM_W), BF16),
        compiler_params=_cparams(("arbitrary",)),
        name="pastkv",
    )(ckv, kpe4, w_ukv)


def _softmax_pv(s, v_ones):
    m = jnp.max(s, axis=-1, keepdims=True)
    p = jnp.exp2(s - m).astype(BF16)
    o = _dot(p, v_ones)
    return o[:, 0:LANES] / o[:, LANES:2 * LANES]


def _attn_kernel(lam_init, n_past, qm_ref, kvm_ref, gq_ref, gkv_ref, dq_ref, dkv_ref, *refs):
    if n_past:
        past_refs, refs = refs[:3], refs[3:]
        lam_ref, gdiff_ref, o_ref = refs[:3]
        joined = refs[3:]

        @pl.when(pl.program_id(1) == 0)
        def _():
            for dst, past, new in zip(joined, past_refs, (kvm_ref, gkv_ref, dkv_ref)):
                dst[0:n_past, :] = past[...]
                dst[n_past:, :] = new[...]

        kvm_ref, gkv_ref, dkv_ref = joined
    else:
        lam_ref, gdiff_ref, o_ref = refs
    tq = qm_ref.shape[0]
    lane = _lane_iota((tq, LANES))
    low = lane < HALF_LANES
    zero = jnp.zeros((tq, LANES), BF16)

    for p in range(MLA_HEADS // 2):
        qn = qm_ref[:, p * LANES:(p + 1) * LANES]
        g = p // 2
        qpe = qm_ref[:, 512 + g * LANES:512 + (g + 1) * LANES]
        kk = kvm_ref[:, p * 256:(p + 1) * 256]
        vv = kvm_ref[:, 1024 + p * 256:1024 + (p + 1) * 256]
        outs = []
        for half in range(2):
            h = 2 * p + half
            slot = h % 4
            in_slot = (lane >= slot * MLA_ROPE) & (lane < (slot + 1) * MLA_ROPE)
            lhs = jnp.concatenate(
                [jnp.where(low if half == 0 else ~low, qn, zero),
                 jnp.where(in_slot, qpe, zero)], axis=1)
            outs.append(_softmax_pv(_dot_nt(lhs, kk), vv))
        o_ref[:, p * LANES:(p + 1) * LANES] = jnp.where(low, outs[0], outs[1]).astype(BF16)

    kk = gkv_ref[:, 0:LANES]
    vv = gkv_ref[:, LANES:3 * LANES]
    for g in range(GQA_HEADS // 2):
        qg = gq_ref[:, g * LANES:(g + 1) * LANES]
        o_lo = _softmax_pv(_dot_nt(jnp.where(low, qg, zero), kk), vv)
        o_hi = _softmax_pv(_dot_nt(jnp.where(low, zero, qg), kk), vv)
        o_ref[:, 512 + g * LANES:512 + (g + 1) * LANES] = jnp.where(low, o_lo, o_hi).astype(BF16)

    lp = lam_ref[...]
    lam = (jnp.exp(jnp.sum(lp[0:1] * lp[1:2], axis=-1, keepdims=True))
           - jnp.exp(jnp.sum(lp[2:3] * lp[3:4], axis=-1, keepdims=True)) + lam_init)
    for h in range(DIFF_HEADS):
        qh = dq_ref[:, h * LANES:(h + 1) * LANES]
        kk = dkv_ref[:, h * LANES:(h + 1) * LANES]
        vv = dkv_ref[:, 512 + h * 256:512 + (h + 1) * 256]
        a1 =_softmax_pv(_dot_nt(jnp.where(low, qh, zero), kk), vv)
        a2 = _softmax_pv(_dot_nt(jnp.where(low, zero, qh), kk), vv)
        od = _rms(a1 - lam * a2, gdiff_ref[...]) * (1.0 - lam_init)
        o_ref[:, 1024 + h * LANES:1024 + (h + 1) * LANES] = od.astype(BF16)


def _attn_call(l, lam_init, qm, kvm, gq, gkv, dq, dkv, lam_p, g_diff, n_b, t_len, past=None):
    tq = _query_block(t_len)
    nq = t_len // tq
    n_past = 0 if past is None else past[0].shape[0] // n_b

    def qspec(w):
        return pl.BlockSpec((tq, w), lambda b, i: (b * nq + i, 0))

    def kspec(w, rows=t_len):
        return pl.BlockSpec((rows, w), lambda b, i: (b, 0))

    in_specs = [qspec(768), kspec(KVM_W), qspec(512), kspec(GKV_W), qspec(512), kspec(DKV_W)]
    args = [qm, kvm, gq, gkv, dq, dkv]
    scratch = []
    if n_past:
        in_specs += [kspec(KVM_W, n_past), kspec(GKV_W, n_past), kspec(DKV_W, n_past)]
        args += list(past)
        scratch = [pltpu.VMEM((n_past + t_len, w), BF16) for w in (KVM_W, GKV_W, DKV_W)]
    in_specs += [_layer_spec(l, (4, DIFF_D)), _layer_spec(l, (1, DIFF_DV))]
    args += [lam_p, g_diff]
    return dict(kernel=functools.partial(_attn_kernel, lam_init, n_past), in_specs=in_specs, args=args,
                out_specs=[qspec(3 * BRANCH_W)],
                out_shape=[jax.ShapeDtypeStruct((n_b * t_len, 3 * BRANCH_W), BF16)],
                scratch=scratch, aliases={})


def _log_sigmoid(x):
    return jnp.minimum(x, 0.0) - jnp.log(1.0 + jnp.exp(-jnp.abs(x)))


def _log_gamma(dec_ref, l, d, h):
    return _log_sigmoid(jnp.full((1, 1), dec_ref[l, d, h], F32))


def _ret_kernel(latent, l, t_len, dec_ref, q_ref, k_ref, v_ref, rg_ref, gret_ref, *refs):
    if latent:
        s0_ref, o_ref = refs
    else:
        o_ref, st_ref = refs[-2:]
    tq = q_ref.shape[0]
    t0 = pl.program_id(1) * tq
    lane = _lane_iota((tq, LANES))
    low = lane < HALF_LANES
    zero = jnp.zeros((tq, LANES), BF16)
    t_idx = (t0 + lax.broadcasted_iota(jnp.int32, (tq, t_len), 0)).astype(F32)
    s_idx = lax.broadcasted_iota(jnp.int32, (tq, t_len), 1).astype(F32)
    dist = t_idx - s_idx
    past = dist >= 0
    diag = jnp.where(dist == 0, 1.0, 0.0)
    t_col = (t0 + lax.broadcasted_iota(jnp.int32, (tq, 1), 0)).astype(F32)

    def lg(d, h):
        return _log_gamma(dec_ref, l, d, h)

    for h in range(RET_HEADS):
        p, half = h // 2, h % 2
        qp = q_ref[:, p * LANES:(p + 1) * LANES]
        qm = jnp.where(low if half == 0 else ~low, qp, zero)
        kp = k_ref[:, p * LANES:(p + 1) * LANES]
        vh = v_ref[:, h * LANES:(h + 1) * LANES]
        lgf, lgb = lg(0, h), lg(1, h)
        dmask = jnp.exp(jnp.where(past, lgf, -lgb) * dist) + diag
        o = _dot((_dot_nt(qm, kp) * dmask).astype(BF16), vh)
        if latent:
            sf =s0_ref[0, p].astype(BF16)
            sb = s0_ref[1, p].astype(BF16)
            o = o + _dot(qm, sf) * jnp.exp(lgf * (t_col + 1.0))
            o = o + _dot(qm, sb) * jnp.exp(lgb * (float(t_len) - t_col))
        mu = jnp.mean(o, axis=-1, keepdims=True)
        oc = o - mu
        y = oc * lax.rsqrt(jnp.mean(oc * oc, axis=-1, keepdims=True) + EPS)
        y = y * gret_ref[:, h * LANES:(h + 1) * LANES]
        rg = rg_ref[:, h * LANES:(h + 1) * LANES].astype(F32)
        o_ref[:, h * LANES:(h + 1) * LANES] = (y * _silu(rg)).astype(BF16)

    if not latent:
        s_col = lax.broadcasted_iota(jnp.int32, (t_len, 1), 0).astype(F32)
        lane_t = _lane_iota((1, LANES)) < HALF_LANES
        for p in range(RET_HEADS // 2):
            kp = k_ref[:, p * LANES:(p + 1) * LANES].astype(F32)
            for d in range(2):
                lg_lane = jnp.where(lane_t, lg(d, 2 * p), lg(d, 2 * p + 1))
                expo = (float(t_len) - 1.0 - s_col) if d == 0 else s_col
                kdec_t = jnp.transpose(kp * jnp.exp(lg_lane * expo)).astype(BF16)
                for half in range(2):
                    h = 2 * p + half
                    st = _dot(kdec_t, v_ref[:, h * LANES:(h + 1) * LANES])
                    st = st[half * RET_DK:(half + 1) * RET_DK, :]
                    if l == 0:
                        st_ref[0, d, h] = st
                        for k in range(1, st_ref.shape[0]):
                            st_ref[k, d, h] = jnp.zeros_like(st)
                    else:
                        st_ref[d, h] = st


def _ret_call(latent, l, dec, ret, rg, g_ret, s0, n_b, t_len, prev_state=None):
    tq = _query_block(t_len)
    nq = t_len // tq
    aliases = {}
    assert latent or nq == 1
    in_specs = [pl.BlockSpec(memory_space=pltpu.SMEM),
                pl.BlockSpec((tq, 256), lambda b, i: (b * nq + i, 0)),
                pl.BlockSpec((t_len, 256), lambda b, i: (b, 1)),
                pl.BlockSpec((t_len, 512), lambda b, i: (b, 1)),
                pl.BlockSpec((tq, 512), lambda b, i: (b * nq + i, 0)),
                _layer_spec(l, (1, 512))]
    args = [dec, ret, ret, ret, rg, g_ret]
    out_specs = [pl.BlockSpec((tq, 512), lambda b, i: (b * nq + i, 0))]
    out_shape = [jax.ShapeDtypeStruct((n_b * t_len, 512), BF16)]
    if latent:
        in_specs.append(pl.BlockSpec((None, 2, 2, LANES, LANES), lambda b, i: (b, 0, 0, 0, 0)))
        args.append(s0)
    else:
        assert (prev_state is None) == (l == 0)
        lead, at = ((DEPTH,), 0) if l == 0 else ((None,), l)
        out_specs.append(pl.BlockSpec((None,) + lead + (2, RET_HEADS, RET_DK, RET_DV),
                                      lambda b, i: (b, at, 0, 0, 0, 0)))
        out_shape.append(jax.ShapeDtypeStruct((n_b, DEPTH, 2, RET_HEADS, RET_DK, RET_DV), F32))
        if prev_state is not None:
            aliases = {len(args): 1}
            in_specs.append(pl.BlockSpec(memory_space=pl.ANY))
            args.append(prev_state)
    return dict(kernel=functools.partial(_ret_kernel, latent, l, t_len), in_specs=in_specs, args=args,
                out_specs=out_specs, out_shape=out_shape, scratch=[], aliases=aliases)


def _mixers_call(attn, ret, n_b, nq, name):
    n_ai, n_ri = len(attn["args"]), len(ret["args"])
    n_ao, n_ro = len(attn["out_specs"]), len(ret["out_specs"])

    def kernel(*refs):
        a_in, r_in = refs[:n_ai], refs[n_ai:n_ai + n_ri]
        outs = refs[n_ai + n_ri:n_ai + n_ri + n_ao + n_ro]
        scratch = refs[n_ai + n_ri + n_ao + n_ro:]
        attn["kernel"](*a_in, *outs[:n_ao], *scratch)
        ret["kernel"](*r_in, *outs[n_ao:])

    aliases = {n_ai + k: n_ao + v for k, v in ret["aliases"].items()}
    return pl.pallas_call(
        kernel,
        grid=(n_b, nq),
        in_specs=attn["in_specs"] + ret["in_specs"],
        out_specs=attn["out_specs"] + ret["out_specs"],
        out_shape=attn["out_shape"] + ret["out_shape"],
        scratch_shapes=attn["scratch"],
        input_output_aliases=aliases,
        compiler_params=_cparams(("arbitrary", "arbitrary")),
        name=name,
    )(*attn["args"], *ret["args"])


def _merge_kernel(x_ref, mod_ref, br_ref, or_ref, gate_ref, wbr_ref, wout_ref, gpost_ref, o_ref):
    merged = None
    for n in range(N_BRANCH):
        if n < 2:
            b = br_ref[:, n * BRANCH_W:(n + 1) * BRANCH_W]
        elif n == 2:
            b = or_ref[...]
        else:
            b = br_ref[:, 2 * BRANCH_W:3 * BRANCH_W]
        t = gate_ref[:, n * D:(n + 1) * D].astype(F32) * _dot(b, wbr_ref[n])
        merged = t if merged is None else merged + t
    out = _dot(merged.astype(BF16), wout_ref[...])
    g1 = mod_ref[...][:, 2 * D:3 * D]
    o_ref[...] = x_ref[...] + g1 * _rms(out, gpost_ref[...])


def _merge_call(l, x, mod3, mod_row, br, o_r, gates, lw):
    n_tok = x.shape[0]

    def tok(w):
        return pl.BlockSpec((TM_MERGE, w), lambda i: (i, 0))

    return pl.pallas_call(
        _merge_kernel,
        grid=(n_tok // TM_MERGE,),
        in_specs=[tok(D), pl.BlockSpec((None, 1, 6 * D), lambda i: (mod_row(i * TM_MERGE), 0, 0)),
                  tok(3 * BRANCH_W), tok(BRANCH_W), tok(4 * D),
                  _layer_spec(l, (N_BRANCH, BRANCH_W, D)), _layer_spec(l, (D, D)),
                  _layer_spec(l, (1, D))],
        out_specs=tok(D),
        out_shape=jax.ShapeDtypeStruct((n_tok, D), F32),
        compiler_params=_cparams(("arbitrary",)),
        name="merge",
    )(x, mod3, br, o_r, gates, lw["w_br"], lw["w_out"], lw["g_post1"])


def _route(logits_t, bias):
    n = logits_t.shape[1]
    scores = jax.nn.sigmoid(logits_t)
    sel = scores + bias
    neg = -jnp.inf
    sub = lax.broadcasted_iota(jnp.int32, (GROUP_SIZE, n), 0)
    grp = []
    for g in range(N_GROUPS):
        blk = sel[g * GROUP_SIZE:(g + 1) * GROUP_SIZE]
        m1 = jnp.max(blk, axis=0, keepdims=True)
        i1 = jnp.min(jnp.where(blk == m1, sub, GROUP_SIZE), axis=0, keepdims=True)
        m2 = jnp.max(jnp.where(sub == i1, neg, blk), axis=0, keepdims=True)
        grp.append(m1 + m2)
    parts = []
    for g in range(N_GROUPS):
        beaten = jnp.zeros((1, n), jnp.int32)
        for o in range(N_GROUPS):
            if o == g:
                continue
            wins = (grp[o] > grp[g]) | (grp[o] == grp[g]) if o < g else (grp[o] > grp[g])
            beaten = beaten + wins.astype(jnp.int32)
        keep = beaten < TOPK_GROUPS
        parts.append(jnp.where(keep, sel[g * GROUP_SIZE:(g + 1) * GROUP_SIZE], neg))
    cur = jnp.concatenate(parts, axis=0)
    eidx = lax.broadcasted_iota(jnp.int32, (N_EXPERTS, n), 0)
    hits, ids, ws = [], [], []
    for _ in range(TOP_K):
        m = jnp.max(cur, axis=0, keepdims=True)
        i = jnp.min(jnp.where(cur == m, eidx, N_EXPERTS), axis=0, keepdims=True)
        hit = eidx == i
        hits.append(hit)
        ids.append(i)
        ws.append(jnp.sum(jnp.where(hit, scores, 0.0), axis=0, keepdims=True))
        cur = jnp.where(hit, neg, cur)
    wsum = ws[0] + ws[1] + ws[2] + ws[3]
    return hits, ids, [w / wsum * ROUTE_SCALE for w in ws]


U32 = jnp.uint32
HIGH16 = np.uint32(0xFFFF0000)


def _bf16_bits(v):
    return lax.bitcast_convert_type(v.astype(BF16).astype(F32), U32)


def _pack_rows(v):
    return (_bf16_bits(v[:, 0:D // 2]) >> 16) | _bf16_bits(v[:, D // 2:D])


def _unpack_rows(p):
    lo = lax.bitcast_convert_type(p << 16, F32)
    hi = lax.bitcast_convert_type(p & HIGH16, F32)
    return jnp.concatenate([lo, hi], axis=1)


def _moe_pre_kernel(x_ref, mod_ref, gpre_ref, wr_ref, br_ref, tri_ref,
                    hp_ref, comb_ref, dest_ref, te_ref, tv_ref, tn_ref, run_ref, eidx_ref, rank_ref):
    tm = x_ref.shape[0]
    step = pl.program_id(0)
    cols = pl.ds(pl.multiple_of(step * tm, tm), tm)

    @pl.when(pl.program_id(0) == 0)
    def _():
        run_ref[...] = jnp.zeros_like(run_ref)

    mod = mod_ref[...]
    sh2, sc2 = mod[:, 3 * D:4 * D], mod[:, 4 * D:5 * D]
    h = _rms(x_ref[...], gpre_ref[...]) * (1.0 + sc2) + sh2
    hp_ref[...] = _pack_rows(h)
    hb = h.astype(BF16)
    h_lo = (h - hb.astype(F32)).astype(BF16)
    wr = wr_ref[...]
    wr_hi = wr.astype(BF16)
    wr_lo = (wr - wr_hi.astype(F32)).astype(BF16)
    logits_t = _dot_nt(wr_hi, hb) + _dot_nt(wr_hi, h_lo) + _dot_nt(wr_lo, hb)
    hits, ids, ws = _route(logits_t, br_ref[...])

    picked = jnp.zeros((N_EXPERTS, tm), F32)
    for hit in hits:
        picked = jnp.where(hit, 1.0, picked)
    before = _dot(picked.astype(BF16), tri_ref[...]) + run_ref[:, 0:1]
    sub8 = lax.broadcasted_iota(jnp.int32, (8, tm), 0)
    comb8 = jnp.zeros((8, tm), F32)
    for k in range(TOP_K):
        rank = jnp.sum(jnp.where(hits[k], before, 0.0), axis=0, keepdims=True)
        eidx_ref[k:k + 1, cols] = ids[k]
        rank_ref[k:k + 1, cols] = rank.astype(jnp.int32)
        comb8 = jnp.where(sub8 == k, ws[k], comb8)
    comb_ref[...] = jnp.transpose(
        jnp.concatenate([comb8, jnp.zeros((LANES - 8, tm), F32)], axis=0))
    run_ref[...] = run_ref[...] + jnp.sum(picked, axis=1, keepdims=True)

    @pl.when(step == pl.num_programs(0) - 1)
    def _():
        _moe_plan(eidx_ref, rank_ref, run_ref, dest_ref, te_ref, tv_ref, tn_ref)


def _moe_pre_call(l, x, mod3, mod_row, lw):
    n_tok = x.shape[0]
    tm = TM_MOE_PRE
    tri = np.arange(tm)
    tri = jnp.asarray(tri[:, None] < tri[None, :], BF16)
    tiles = jax.ShapeDtypeStruct((1, LANES), jnp.int32)
    return pl.pallas_call(
        _moe_pre_kernel,
        grid=(n_tok // tm,),
        in_specs=[pl.BlockSpec((tm, D), lambda i: (i, 0)),
                  pl.BlockSpec((None, 1, 6 * D), lambda i: (mod_row(i * tm), 0, 0)),
                  _layer_spec(l, (1, D)), _layer_spec(l, (N_EXPERTS, D)),
                  _layer_spec(l, (N_EXPERTS, 1)), _const_spec((tm, tm))],
        out_specs=[pl.BlockSpec((tm, D // 2), lambda i: (i, 0)),
                   pl.BlockSpec((tm, LANES), lambda i: (i, 0)),
                   _const_spec((TOP_K, n_tok)), _const_spec((1, LANES)), _const_spec((1, LANES)),
                   _const_spec((8, LANES))],
        out_shape=[jax.ShapeDtypeStruct((n_tok, D // 2), U32),
                   jax.ShapeDtypeStruct((n_tok, LANES), F32),
                   jax.ShapeDtypeStruct((TOP_K, n_tok), jnp.int32), tiles, tiles,
                   jax.ShapeDtypeStruct((8, LANES), jnp.int32)],
        scratch_shapes=[pltpu.VMEM((N_EXPERTS, LANES), F32),
                        pltpu.VMEM((TOP_K, n_tok), jnp.int32), pltpu.VMEM((TOP_K, n_tok), jnp.int32)],
        compiler_params=_cparams(("arbitrary",)),
        name="moe_pre",
    )(x, mod3, lw["g_pre2"], lw["w_router_t"], lw["b_router"], tri)


def _moe_plan(eidx_ref, rank_ref, cnt_ref, dest_ref, te_ref, tv_ref, tn_ref):
    tm = eidx_ref.shape[1]
    cnt = cnt_ref[...]
    padded = jnp.ceil(cnt * (1.0 / TMX)) * TMX
    row = lax.broadcasted_iota(jnp.int32, cnt.shape, 0)
    incl = padded
    shift = 1
    while shift < N_EXPERTS:
        incl = incl + jnp.where(row >= shift, pltpu.roll(incl, shift, 0), 0.0)
        shift *= 2
    start = (incl - padded)[:, 0:1]
    end = incl[:, 0:1]
    erow = lax.broadcasted_iota(jnp.int32, (N_EXPERTS, tm), 0)
    for k in range(TOP_K):
        mine = erow == eidx_ref[k:k + 1, :]
        base = jnp.sum(jnp.where(mine, start, 0.0), axis=0, keepdims=True)
        dest_ref[k:k + 1, :] = rank_ref[k:k + 1, :] + base.astype(jnp.int32)

    tile0 = (_lane_iota((1, LANES)) * TMX).astype(F32)
    owner = jnp.sum(jnp.where(end <= tile0, 1.0, 0.0), axis=0, keepdims=True)
    owner = jnp.minimum(owner, N_EXPERTS - 1.0)
    erow_t = lax.broadcasted_iota(jnp.int32, (N_EXPERTS, LANES), 0).astype(F32)
    left = jnp.sum(jnp.where(erow_t == owner, cnt[:, 0:1] - (tile0 - start), 0.0),
                   axis=0, keepdims=True)
    te_ref[...] = owner.astype(jnp.int32)
    tv_ref[...] = jnp.clip(left, 0.0, float(TMX)).astype(jnp.int32)
    tn_ref[...] = jnp.full(tn_ref.shape, N_EXPERTS, jnp.int32)
    nxt = owner
    for k in range(W_SLOTS - 1):
        later = (erow_t > nxt) & (cnt[:, 0:1] > 0.0)
        nxt = jnp.min(jnp.where(later, erow_t, float(N_EXPERTS)), axis=0, keepdims=True)
        tn_ref[k:k + 1, :] = nxt.astype(jnp.int32)
    n_used = jnp.sum(jnp.where(left > 0.0, 1.0, 0.0), axis=1, keepdims=True)
    tn_ref[7:8, :] = jnp.minimum(tile0 * (1.0 / TMX), n_used - 1.0).astype(jnp.int32)


def _experts_kernel(l, te_ref, tv_ref, tn_ref, xs_ref, wgu_hbm, wdn_hbm, ys_ref,
                    wgu_f, wdn_f, wgu_b, wdn_b, sem, group_ref):
    j = pl.program_id(0)
    valid = tv_ref[j]
    expert = te_ref[j]

    def fetch(e, slot):
        return (pltpu.make_async_copy(wgu_hbm.at[l, e], wgu_f.at[slot], sem.at[slot, 0]),
                pltpu.make_async_copy(wdn_hbm.at[l, e], wdn_f.at[slot], sem.at[slot, 1]))

    def start_if_any(e, slot):
        @pl.when(e < N_EXPERTS)
        def _():
            for cp in fetch(e, slot):
                cp.start(priority=1)

    @pl.when(j == 0)
    def _():
        group_ref[0] = 0
        start_if_any(expert, 0)
        for k in range(W_SLOTS - 2):
            start_if_any(tn_ref[k, 0], k + 1)

    first_tile = (j == 0) | (expert != te_ref[jnp.maximum(j - 1, 0)])

    @pl.when(first_tile & (valid > 0))
    def _():
        group = group_ref[0]
        slot = lax.rem(group, W_SLOTS)
        for cp in fetch(expert, slot):
            cp.wait()
        wgu_b[...] = wgu_f[slot].astype(BF16)
        wdn_b[...] = wdn_f[slot].astype(BF16)
        start_if_any(tn_ref[W_SLOTS - 2, j], lax.rem(group + W_SLOTS - 1, W_SLOTS))
        group_ref[0] = group + 1

    @pl.when(valid > 0)
    def _():
        rows = lax.broadcasted_iota(jnp.int32, (TMX, D), 0)
        x = jnp.where(rows < valid, _unpack_rows(xs_ref[...]), 0.0).astype(BF16)
        gu = _dot(x, wgu_b[...])
        a = _silu(gu[:, 0:EXPERT_FF]) * gu[:, EXPERT_FF:2 * EXPERT_FF]
        ys_ref[...] = _pack_rows(_dot(a.astype(BF16), wdn_b[...]))


def _experts_call(l, xs, te, tv, tn, w_gu, w_dn):
    n_tiles = xs.shape[0] // TMX
    grid_spec = pltpu.PrefetchScalarGridSpec(
        num_scalar_prefetch=3,
        grid=(n_tiles,),
        in_specs=[pl.BlockSpec((TMX, D // 2), lambda j, te, tv, tn: (tn[7, j], 0)),
                  pl.BlockSpec(memory_space=pl.ANY), pl.BlockSpec(memory_space=pl.ANY)],
        out_specs=pl.BlockSpec((TMX, D // 2), lambda j, te, tv, tn: (tn[7, j], 0)),
        scratch_shapes=[pltpu.VMEM((W_SLOTS, D, 2 * EXPERT_FF), F32),
                        pltpu.VMEM((W_SLOTS, EXPERT_FF, D), F32),
                        pltpu.VMEM((D, 2 * EXPERT_FF), BF16), pltpu.VMEM((EXPERT_FF, D), BF16),
                        pltpu.SemaphoreType.DMA((W_SLOTS, 2)), pltpu.SMEM((1,), jnp.int32)])
    return pl.pallas_call(
        functools.partial(_experts_kernel, l),
        grid_spec=grid_spec,
        out_shape=jax.ShapeDtypeStruct(xs.shape, U32),
        compiler_params=_cparams(("arbitrary",)),
        name="moe_experts",
    )(te, tv, tn, xs, w_gu, w_dn)


def _moe_post_kernel(x_ref, mod_ref, hp_ref, yg_ref, comb_ref, wsgu_ref, wsdn_ref, gpost_ref,
                     o_ref):
    hb = _unpack_rows(hp_ref[...]).astype(BF16)
    sgu = _dot(hb, wsgu_ref[...])
    sa = _silu(sgu[:, 0:SHARED_FF]) * sgu[:, SHARED_FF:2 * SHARED_FF]
    acc = _dot(sa.astype(BF16), wsdn_ref[...])
    comb = comb_ref[...]
    for k in range(TOP_K):
        acc = acc + comb[:, k:k + 1] * _unpack_rows(yg_ref[k])
    g2 = mod_ref[...][:, 5 * D:6 * D]
    o_ref[...] = x_ref[...] + g2 * _rms(acc, gpost_ref[...])


def _moe_post_call(l, x, mod3, mod_row, hp, yg, comb, lw):
    n_tok = x.shape[0]
    tm = TM_MOE_PRE
    return pl.pallas_call(
        _moe_post_kernel,
        grid=(n_tok // tm,),
        in_specs=[pl.BlockSpec((tm, D), lambda i: (i, 0)),
                  pl.BlockSpec((None, 1, 6 * D), lambda i: (mod_row(i * tm), 0, 0)),
                  pl.BlockSpec((tm, D // 2), lambda i: (i, 0)),
                  pl.BlockSpec((TOP_K, tm, D // 2), lambda i: (0, i, 0)),
                  pl.BlockSpec((tm, LANES), lambda i: (i, 0)),
                  _layer_spec(l, (D, 2 * SHARED_FF)), _layer_spec(l, (SHARED_FF, D)),
                  _layer_spec(l, (1, D))],
        out_specs=pl.BlockSpec((tm, D), lambda i: (i, 0)),
        out_shape=jax.ShapeDtypeStruct((n_tok, D), F32),
        compiler_params=_cparams(("arbitrary",)),
        name="moe_post",
    )(x, mod3, hp, yg, comb, lw["w_sh_gu"], lw["w_sh_down"], lw["g_post2"])


def _moe_call(l, x, mod3, mod_row, lw):
    n_tok = x.shape[0]
    n_slots = -(-(TOP_K * n_tok + N_EXPERTS * (TMX - 1)) // TMX) * TMX
    assert n_slots // TMX <= LANES
    hp, comb, dest, te, tv, tn = _moe_pre_call(l, x, mod3, mod_row, lw)
    dest = dest.reshape(TOP_K * n_tok)
    xs = _sc_scatter_rows(hp, dest, n_slots)
    ys = _experts_call(l, xs, te[0], tv[0], tn, lw["w_exp_gu"], lw["w_exp_down"])
    yg = _sc_gather_rows(ys, dest).reshape(TOP_K, n_tok, D // 2)
    return _moe_post_call(l, x, mod3, mod_row, hp, yg, comb, lw)


SC_CORES, SC_SUBCORES = 2, 16
SC_WORKERS = SC_CORES * SC_SUBCORES


def _sc_gather_rows(table, idx, chunk=64):
    n_out, width = idx.shape[0], table.shape[1]
    per_worker = n_out // SC_WORKERS
    n_chunks = per_worker // chunk
    assert per_worker * SC_WORKERS == n_out and n_chunks * chunk == per_worker
    mesh = plsc.VectorSubcoreMesh(core_axis_name="c", subcore_axis_name="s",
                                  num_cores=SC_CORES, num_subcores=SC_SUBCORES)

    @functools.partial(
        pl.kernel, mesh=mesh,
        out_type=jax.ShapeDtypeStruct((n_out, width), table.dtype),
        scratch_types=[pltpu.VMEM((chunk,), jnp.int32), pltpu.VMEM((chunk, width), table.dtype),
                       pltpu.SemaphoreType.DMA],
        name="sc_gather")
    def gather(table_hbm, idx_hbm, out_hbm, idx_v, rows_v, sem):
        base = (lax.axis_index("s") * SC_CORES + lax.axis_index("c")) * per_worker

        @pl.loop(0, n_chunks)
        def _(j):
            off = base + j * chunk
            pltpu.sync_copy(idx_hbm.at[pl.ds(off, chunk)], idx_v)
            pltpu.async_copy(table_hbm.at[idx_v], rows_v, sem).wait()
            pltpu.sync_copy(rows_v, out_hbm.at[pl.ds(off, chunk)])

    return gather(table, idx)


def _sc_scatter_rows(rows, dest, n_slots, chunk=64):
    n_tok, width = rows.shape
    per_worker = n_tok // SC_WORKERS
    n_chunks = per_worker // chunk
    assert per_worker * SC_WORKERS == n_tok and n_chunks * chunk == per_worker
    mesh = plsc.VectorSubcoreMesh(core_axis_name="c", subcore_axis_name="s",
                                  num_cores=SC_CORES, num_subcores=SC_SUBCORES)

    @functools.partial(
        pl.kernel, mesh=mesh,
        out_type=jax.ShapeDtypeStruct((n_slots, width), rows.dtype),
        scratch_types=[pltpu.VMEM((chunk,), jnp.int32), pltpu.VMEM((chunk, width), rows.dtype)],
        name="sc_scatter")
    def scatter(rows_hbm, dest_hbm, out_hbm, idx_v, rows_v):
        base = (lax.axis_index("s") * SC_CORES + lax.axis_index("c")) * per_worker

        @pl.loop(0, n_chunks)
        def _(j):
            off = base + j * chunk
            pltpu.sync_copy(rows_hbm.at[pl.ds(off, chunk)], rows_v)
            for k in range(TOP_K):
                pltpu.sync_copy(dest_hbm.at[pl.ds(k * n_tok + off, chunk)], idx_v)
                pltpu.sync_copy(rows_v, out_hbm.at[idx_v])

    return scatter(rows, dest)


def _rope_tables(t_len):
    pos = np.arange(t_len)
    row, col = pos // GRID_W, pos % GRID_W

    def tab(r):
        half = r // 2
        freq = ROPE_BASE ** (-np.arange(half, dtype=np.float64) / half)
        sign = np.concatenate([-np.ones(half), np.ones(half)])
        cs, sn = [], []
        for p in (row, col):
            ang = p[:, None].astype(np.float64) * freq[None, :]
            cs.append(np.concatenate([np.cos(ang), np.cos(ang)], axis=1))
            sn.append(np.concatenate([np.sin(ang), np.sin(ang)], axis=1) * sign[None, :])
        return np.concatenate(cs, axis=1), np.concatenate(sn, axis=1)

    c64, s64 = tab(GQA_HD // 2)
    cpe, spe = tab(MLA_ROPE // 2)
    out = (np.tile(c64, (1, 2)), np.tile(s64, (1, 2)), np.tile(cpe, (1, 4)), np.tile(spe, (1, 4)))
    return tuple(jnp.asarray(a, F32) for a in out)


def _prep_weights(p):
    n_l = p["w_in"].shape[0]

    def row(name):
        return p[name].reshape(n_l, 1, -1)

    w_uq = p["w_mla_uq"].reshape(n_l, MLA_Q_LORA, MLA_HEADS, MLA_NOPE + MLA_ROPE)
    w_uq = jnp.concatenate([w_uq[..., :MLA_NOPE].reshape(n_l, MLA_Q_LORA, -1),
                            w_uq[..., MLA_NOPE:].reshape(n_l, MLA_Q_LORA, -1)], axis=-1)
    w_ukv = p["w_mla_ukv"].reshape(n_l, MLA_KV_LORA, MLA_HEADS, MLA_NOPE + MLA_V)
    w_ukv = jnp.concatenate([w_ukv[..., :MLA_NOPE].reshape(n_l, MLA_KV_LORA, -1),
                             w_ukv[..., MLA_NOPE:].reshape(n_l, MLA_KV_LORA, -1)], axis=-1)
    w_br = p["w_br"]
    w_br_gqa = w_br[:, 1].reshape(n_l, GQA_HEADS, GQA_HD, D)[:, jnp.array(GQA_ORDER)]
    w_br = jnp.concatenate([w_br[:, 0:1], w_br_gqa.reshape(n_l, 1, BRANCH_W, D), w_br[:, 2:4]], axis=1)
    blk = np.arange(512) // GQA_HD
    return {
        "g_pre1": row("g_pre1"), "g_post1": row("g_post1"),
        "g_pre2": row("g_pre2"), "g_post2": row("g_post2"),
        "w_in": jnp.swapaxes(p["w_in"], 1, 2).astype(BF16),
        "g_mla_q": row("g_mla_q"), "w_uq": w_uq.astype(BF16),
        "g_mla_kv": row("g_mla_kv"), "w_ukv": w_ukv.astype(BF16),
        "g_gqa_q": jnp.tile(p["g_gqa_q"], (1, GQA_HEADS)).reshape(n_l, 1, -1),
        "g_gqa_k": jnp.tile(p["g_gqa_k"], (1, GQA_KV_HEADS)).reshape(n_l, 1, -1),
        "bd": jnp.asarray(blk[:, None] == blk[None, :], BF16),
        "ret_decay": p["ret_decay"],
        "g_ret": row("g_ret"),
        "diff_lambda": p["diff_lambda"], "g_diff": row("g_diff"),
        "w_br": w_br.astype(BF16), "w_out": p["w_out"].astype(BF16),
        "w_router_t": jnp.swapaxes(p["w_router"], 1, 2),
        "b_router": p["b_router"].reshape(n_l, N_EXPERTS, 1),
        "w_exp_gu": p["w_exp_gu"], "w_exp_down": p["w_exp_down"],
        "w_sh_gu": p["w_sh_gu"].astype(BF16), "w_sh_down": p["w_sh_down"].astype(BF16),
    }


def _mixers(latent, l, x, mod3, mod_row, lw, n_b, t_len, tabs=None, past=None, s0=None,
            prev_cache=None):
    lam_init = 0.8 - 0.6 * math.exp(-0.3 * l)
    outs = _inprep_call(latent, l, x, mod3, mod_row, lw, tabs, t_len,
                        None if prev_cache is None else prev_cache[:6])
    qm, kvm, gq, gkv, dq, dkv, ret, rg, gates = outs[:9]
    attn = _attn_call(l, lam_init, qm, kvm, gq, gkv, dq, dkv, lw["diff_lambda"], lw["g_diff"],
                      n_b, t_len, past)
    retn = _ret_call(latent, l, lw["ret_decay"], ret, rg, lw["g_ret"], s0, n_b, t_len,
                     None if prev_cache is None else prev_cache[6])
    mixed = _mixers_call(attn, retn, n_b, t_len // _query_block(t_len),
                         "mix_lat" if latent else "mix_ctx")
    br, r = mixed[0], mixed[1:]
    y = _merge_call(l, x, mod3, mod_row, br, r[0], gates, lw)
    cache = None if latent else tuple(outs[9:]) + (r[1],)
    return y, cache


def kernel(x_prompt, x_sample, cache_mla_ckv, cache_mla_kpe, cache_gqa_k, cache_gqa_v, cache_diff_k, cache_diff_v, state_ret, c, c_ctx, w_mod, b_mod, g_pre1, g_post1, g_pre2, g_post2, w_in, g_mla_q, w_mla_uq, g_mla_kv, w_mla_ukv, g_gqa_q, g_gqa_k, ret_decay, g_ret, diff_lambda, g_diff, w_br, w_out, w_router, b_router, w_exp_gu, w_exp_down, w_sh_gu, w_sh_down):
    params = dict(w_in=w_in, g_pre1=g_pre1, g_post1=g_post1, g_pre2=g_pre2,
                  g_post2=g_post2, g_mla_q=g_mla_q, w_mla_uq=w_mla_uq,
                  g_mla_kv=g_mla_kv, w_mla_ukv=w_mla_ukv, g_gqa_q=g_gqa_q, g_gqa_k=g_gqa_k,
                  ret_decay=ret_decay, g_ret=g_ret, diff_lambda=diff_lambda, g_diff=g_diff,
                  w_br=w_br, w_out=w_out, w_router=w_router, b_router=b_router,
                  w_exp_gu=w_exp_gu, w_exp_down=w_exp_down, w_sh_gu=w_sh_gu, w_sh_down=w_sh_down)
    n_bc, t_c, _ = x_prompt.shape
    n_bl, t_l, _ = x_sample.shape
    p_len = cache_mla_ckv.shape[2]
    tabs = _rope_tables(t_l)
    n_cond = 8
    cond = jnp.concatenate([c_ctx[None, :], c, jnp.zeros((n_cond - 1 - n_bl, D), F32)], axis=0)
    assert t_l % TM_MERGE == 0 and (t_c * n_bc) % TM_MERGE == 0
    assert t_l % TM_MOE_PRE == 0 and (t_c * n_bc) % TM_MOE_PRE == 0

    yp = x_prompt.reshape(n_bc * t_c, D)
    ys = x_sample.reshape(n_bl * t_l, D)
    cache = None
    lw = _prep_weights(params)
    for l in range(DEPTH):
        mod3 = _mod_call(l, cond, w_mod, b_mod).reshape(n_cond, 1, 6 * D)
        yp, cache = _mixers(False, l, yp, mod3, lambda i: 0, lw, n_bc, t_c, prev_cache=cache)
        yp = _moe_call(l, yp, mod3, lambda i: 0, lw)
        past_kvm = _pastkv_call(l, cache_mla_ckv[:, l].reshape(n_bl * p_len, -1),
                                jnp.tile(cache_mla_kpe[:, l].reshape(n_bl * p_len, -1), (1, 4)),
                                lw["w_ukv"])
        past_gkv = jnp.concatenate([cache_gqa_k[:, l].reshape(n_bl * p_len, -1),
                                    cache_gqa_v[:, l].reshape(n_bl * p_len, -1),
                                    jnp.ones((n_bl * p_len, LANES), F32)], axis=-1).astype(BF16)
        past_dv = jnp.concatenate([cache_diff_v[:, l], jnp.ones_like(cache_diff_v[:, l])], axis=-1)
        past_dkv = jnp.concatenate([cache_diff_k[:, l].reshape(n_bl * p_len, -1),
                                    past_dv.reshape(n_bl * p_len, -1)], axis=-1).astype(BF16)
        s0 = state_ret[:, l].reshape(n_bl, 2, RET_HEADS // 2, 2 * RET_DK, RET_DV)
        ys, _ = _mixers(True, l, ys, mod3, lambda t: 1 + t // t_l, lw, n_bl, t_l, tabs=tabs,
                        past=(past_kvm, past_gkv, past_dkv), s0=s0)
        ys = _moe_call(l, ys, mod3, lambda t: 1 + t // t_l, lw)

    ckv, kpe, gk_t, gv_t, dk_t, dv, ret_state = cache

    def time_minor(a, shape):
        a = a.reshape((n_bc, DEPTH) + shape + (t_c,))
        return jnp.transpose(a, (0, 1, a.ndim - 1) + tuple(range(2, a.ndim - 1)))

    return (yp.reshape(n_bc, t_c, D), ys.reshape(n_bl, t_l, D), ckv, kpe,
            time_minor(gk_t, (GQA_KV_HEADS, GQA_HD)), time_minor(gv_t, (GQA_KV_HEADS, GQA_HD)),
            time_minor(dk_t, (DIFF_HEADS, 2, DIFF_D)),
            dv.reshape(n_bc, DEPTH, t_c, DIFF_HEADS, DIFF_DV), ret_state)
```

```python
import functools
import math

import numpy as np
import jax
import jax.numpy as jnp
from jax import lax
from jax.experimental import pallas as pl
from jax.experimental.pallas import tpu as pltpu
from jax.experimental.pallas import tpu_sc as plsc

F32 = jnp.float32
BF16 = jnp.bfloat16

D = 1024
DEPTH = 2
GRID_W = 64
ROPE_BASE = 10000.0
EPS = 1e-6

MLA_HEADS, MLA_NOPE, MLA_ROPE, MLA_V = 8, 64, 32, 64
MLA_Q_LORA, MLA_KV_LORA = 384, 256
GQA_HEADS, GQA_KV_HEADS, GQA_HD = 8, 2, 64
RET_HEADS, RET_DK, RET_DV = 4, 64, 128
DIFF_HEADS, DIFF_D, DIFF_DV = 4, 64, 128
N_BRANCH, BRANCH_W = 4, 512
N_EXPERTS, TOP_K, N_GROUPS, TOPK_GROUPS = 32, 4, 4, 2
EXPERT_FF, SHARED_FF = 256, 256
ROUTE_SCALE = 2.5
GROUP_SIZE = N_EXPERTS // N_GROUPS

LANES = 128
HALF_LANES = 64
VMEM_LIMIT = 56 * 1024 * 1024

O_CQ, O_CKV, O_KPE, O_GQ, O_GK, O_GV = 0, 384, 640, 672, 1184, 1312
O_RQ, O_RK, O_RV, O_RG, O_DQ, O_DK, O_DV, O_GL, O_END = (
    1440, 1696, 1952, 2464, 2976, 3488, 4000, 4512, 8608)
GQA_ORDER = (0, 4, 1, 5, 2, 6, 3, 7)

KVM_W = 8 * 256
GKV_W = 3 * LANES
DKV_W = 512 + 4 * 256
LOG2E = 1.4426950408889634
TM = 256
TM_LAT = 512
TM_MERGE = 512
TQ = 512


def _query_block(t_len):
    return min(TQ, t_len)
TM_MOE_PRE = 1024
TMX = 256
W_SLOTS = 4


def _cparams(sem):
    return pltpu.CompilerParams(dimension_semantics=sem, vmem_limit_bytes=VMEM_LIMIT)


def _const_spec(shape):
    nd = len(shape)
    return pl.BlockSpec(shape, lambda *_: (0,) * nd)


def _layer_spec(l, shape):
    nd = len(shape)
    return pl.BlockSpec((None,) + tuple(shape), lambda *_: (l,) + (0,) * nd)


def _rms(x, g):
    return x * lax.rsqrt(jnp.mean(x * x, axis=-1, keepdims=True) + EPS) * g


def _dot(a, b):
    return jnp.dot(a, b, preferred_element_type=F32)


def _dot_nt(a, b):
    return lax.dot_general(a, b, (((1,), (1,)), ((), ())), preferred_element_type=F32)


def _silu(x):
    return x * jax.nn.sigmoid(x)


def _lane_iota(shape):
    return lax.broadcasted_iota(jnp.int32, shape, len(shape) - 1)


def _seg_meansq(x, bd_ref, width):
    sq = x * x
    hi = sq.astype(BF16)
    lo = (sq - hi.astype(F32)).astype(BF16)
    bd = bd_ref[0:width, 0:width]
    return (_dot(hi, bd) + _dot(lo, bd)) * (1.0 / GQA_HD)


def _rope(x, cos, sin_signed, half):
    width = x.shape[-1]
    first = (_lane_iota(x.shape) % (2 * half)) < half
    partner = jnp.where(first, pltpu.roll(x, width - half, 1), pltpu.roll(x, half, 1))
    return x * cos + partner * sin_signed


def _tile_lanes(t, reps):
    return t if reps == 1 else jnp.concatenate([t] * reps, axis=1)


def _put_layer(o_ref, val, whole_stack):
    if whole_stack:
        o_ref[0] = val
        for k in range(1, o_ref.shape[0]):
            o_ref[k] = jnp.zeros_like(val)
    else:
        o_ref[...] = val


def _store_kvm(kvm_ref, kv, kpe_b):
    ones = jnp.ones(kpe_b.shape, BF16)
    for p in range(4):
        kvm_ref[:, p * 256:p * 256 + LANES] = kv[:, p * LANES:(p + 1) * LANES].astype(BF16)
        kvm_ref[:, p * 256 + LANES:(p + 1) * 256] = kpe_b
        kvm_ref[:, 1024 + p * 256:1024 + p * 256 + LANES] = (
            kv[:, 512 + p * LANES:512 + (p + 1) * LANES].astype(BF16))
        kvm_ref[:, 1024 + p * 256 + LANES:1024 + (p + 1) * 256] = ones


def _mod_kernel(c_ref, w_ref, b_ref, o_ref):
    a = _silu(c_ref[...]).astype(BF16)
    o_ref[...] = _dot(a, w_ref[...].astype(BF16)) + b_ref[...]


def _mod_call(cond, w_mod, b_mod):
    n_l, _, n = w_mod.shape
    tn = 1536
    return pl.pallas_call(
        _mod_kernel,
        grid=(n_l, n // tn),
        in_specs=[_const_spec(cond.shape),
                  pl.BlockSpec((None, D, tn), lambda l, j: (l, 0, j)),
                  pl.BlockSpec((None, 1, tn), lambda l, j: (l, 0, j))],
        out_specs=pl.BlockSpec((None, cond.shape[0], tn), lambda l, j: (l, 0, j)),
        out_shape=jax.ShapeDtypeStruct((n_l, cond.shape[0], n), F32),
        compiler_params=_cparams(("arbitrary", "arbitrary")),
        name="mod",
    )(cond, w_mod, b_mod.reshape(n_l, 1, n))


def _inprep_kernel(latent, n_aliased, *refs):
    (x_ref, mod_ref, gpre_ref, win_ref, gmq_ref, wuq_ref, gmkv_ref, wukv_ref,
     ggq_ref, ggk_ref, bd_ref) = refs[:11]
    refs = refs[11:]
    if latent:
        cos64_ref, sin64_ref, cospe_ref, sinpe_ref = refs[:4]
        refs = refs[4:]
    refs = refs[n_aliased:]
    first_layer = n_aliased == 0
    (qm_ref, kvm_ref, gqo_ref, gkv_ref, dqo_ref, dkv_ref, ret_ref, rg_ref, gate_ref) = refs[:9]
    refs = refs[9:]
    if not latent:
        ckv_o, kpe_o, gk_o, gv_o, dk_o, dv_o = refs

    x = x_ref[...]
    mod = mod_ref[...]
    sh1 = mod[:, 0:D]
    sc1 = mod[:, D:2 * D]
    hb = (_rms(x, gpre_ref[...]) * (1.0 + sc1) + sh1).astype(BF16)

    def z(a, b):
        return _dot_nt(hb, win_ref[a:b, :])

    if latent:
        cos64, sin64 = cos64_ref[...], sin64_ref[...]
        cospe, sinpe = cospe_ref[...], sinpe_ref[...]

    cqn = _rms(z(O_CQ, O_CKV), gmq_ref[...]).astype(BF16)
    q = _dot(cqn, wuq_ref[...]) * ((MLA_NOPE + MLA_ROPE) ** -0.5 * LOG2E)
    q_nope, q_pe = q[:, 0:512], q[:, 512:768]
    if latent:
        q_pe = _rope(q_pe, _tile_lanes(cospe, 2), _tile_lanes(sinpe, 2), MLA_ROPE // 4)
    qm_ref[:, 0:512] = q_nope.astype(BF16)
    qm_ref[:, 512:768] = q_pe.astype(BF16)

    ckvn = _rms(z(O_CKV, O_KPE), gmkv_ref[...])
    kv = _dot(ckvn.astype(BF16), wukv_ref[...])
    kpe4 = _dot_nt(hb, jnp.concatenate([win_ref[O_KPE:O_GQ, :]] * 4, axis=0))
    if latent:
        kpe4 = _rope(kpe4, cospe, sinpe, MLA_ROPE // 4)
    else:
        _put_layer(ckv_o, ckvn, first_layer)
        _put_layer(kpe_o, kpe4[:, 0:MLA_ROPE], first_layer)
    _store_kvm(kvm_ref, kv, kpe4.astype(BF16))

    gq = _dot_nt(hb, jnp.concatenate(
        [win_ref[O_GQ + h * GQA_HD:O_GQ + (h + 1) * GQA_HD, :] for h in GQA_ORDER], axis=0))
    gq = gq * lax.rsqrt(_seg_meansq(gq, bd_ref, 512) + EPS) * ggq_ref[...]
    gk = z(O_GK, O_GV)
    gk = gk * lax.rsqrt(_seg_meansq(gk, bd_ref, LANES) + EPS) * ggk_ref[...]
    gv = z(O_GV, O_RQ)
    if latent:
        gq = _rope(gq, _tile_lanes(cos64, 4), _tile_lanes(sin64, 4), GQA_HD // 4)
        gk = _rope(gk, cos64, sin64, GQA_HD // 4)
    else:
        _put_layer(gk_o, jnp.transpose(gk), first_layer)
        _put_layer(gv_o, jnp.transpose(gv), first_layer)
    gqo_ref[...] = (gq * (GQA_HD ** -0.5 * LOG2E)).astype(BF16)
    gkv_ref[:, 0:LANES] = gk.astype(BF16)
    gkv_ref[:, LANES:2 * LANES] = gv.astype(BF16)
    gkv_ref[:, 2 * LANES:3 * LANES] = jnp.ones(gv.shape, BF16)

    dq = z(O_DQ, O_DK)
    dk = z(O_DK, O_DV)
    dv = z(O_DV, O_GL)
    if latent:
        dq = _rope(dq, _tile_lanes(cos64, 4), _tile_lanes(sin64, 4), DIFF_D // 4)
        dk = _rope(dk, _tile_lanes(cos64, 4), _tile_lanes(sin64, 4), DIFF_D // 4)
    else:
        _put_layer(dk_o, jnp.transpose(dk), first_layer)
        _put_layer(dv_o, dv, first_layer)
    dqo_ref[...] = (dq * (DIFF_D ** -0.5 * LOG2E)).astype(BF16)
    dkv_ref[:, 0:512] = dk.astype(BF16)
    for h in range(DIFF_HEADS):
        dkv_ref[:, 512 + h * 256:512 + h * 256 + LANES] = dv[:, h * LANES:(h + 1) * LANES].astype(BF16)
        dkv_ref[:, 512 + h * 256 + LANES:512 + (h + 1) * 256] = jnp.ones((dv.shape[0], LANES), BF16)

    ret_ref[:, 0:256] = z(O_RQ, O_RK).astype(BF16)
    ret_ref[:, 256:512] = (z(O_RK, O_RV) * (RET_DK ** -0.5)).astype(BF16)
    ret_ref[:, 512:1024] = z(O_RV, O_RG).astype(BF16)
    rg_ref[...] = z(O_RG, O_DQ).astype(BF16)

    for n in range(N_BRANCH):
        gate_ref[:, n * D:(n + 1) * D] = jax.nn.sigmoid(
            z(O_GL + n * D, O_GL + (n + 1) * D)).astype(BF16)


def _inprep_call(latent, l, x, mod3, mod_row, lw, tabs, t_len, prev_caches=None):
    n_tok = x.shape[0]
    tm = TM_LAT if latent else TM
    nblk = n_tok // tm
    blk_per_seq = t_len // tm

    def tok(w):
        return pl.BlockSpec((tm, w), lambda i: (i, 0))

    in_specs = [tok(D),
                pl.BlockSpec((None, 1, 6 * D), lambda i: (mod_row(i * tm), 0, 0)),
                _layer_spec(l, (1, D)),
                pl.BlockSpec((None, O_END, D), lambda i: (l, 0, 0), pipeline_mode=pl.Buffered(1)),
                _layer_spec(l, (1, MLA_Q_LORA)), _layer_spec(l, (MLA_Q_LORA, 768)),
                _layer_spec(l, (1, MLA_KV_LORA)), _layer_spec(l, (MLA_KV_LORA, 1024)),
                _layer_spec(l, (1, 512)), _layer_spec(l, (1, LANES)), _const_spec((512, 512))]
    args = [x, mod3, lw["g_pre1"], lw["w_in"], lw["g_mla_q"], lw["w_uq"], lw["g_mla_kv"],
            lw["w_ukv"], lw["g_gqa_q"], lw["g_gqa_k"], lw["bd"]]
    if latent:
        tab_spec = pl.BlockSpec((tm, LANES), lambda i: (i % blk_per_seq, 0))
        in_specs += [tab_spec] * 4
        args += list(tabs)
    widths = [768, KVM_W, 512, GKV_W, 512, DKV_W, 1024, 512, 4 * D]
    out_specs = [tok(w) for w in widths]
    out_shape = [jax.ShapeDtypeStruct((n_tok, w), BF16) for w in widths]
    aliases = {}
    if not latent:
        assert TM == t_len and (prev_caches is None) == (l == 0)
        n_seq = n_tok // t_len
        lead, at = ((DEPTH,), 0) if l == 0 else ((None,), l)

        def row_major(w):
            out_specs.append(pl.BlockSpec((None,) + lead + (t_len, w), lambda i: (i, at, 0, 0)))
            out_shape.append(jax.ShapeDtypeStruct((n_seq, DEPTH, t_len, w), F32))

        row_major(MLA_KV_LORA)
        row_major(MLA_ROPE)
        for w in (LANES, LANES, 512):
            out_specs.append(pl.BlockSpec((None,) + lead + (w, t_len), lambda i: (i, at, 0, 0)))
            out_shape.append(jax.ShapeDtypeStruct((n_seq, DEPTH, w, t_len), F32))
        row_major(512)
        if prev_caches is not None:
            n_in = len(args)
            in_specs += [pl.BlockSpec(memory_space=pl.ANY)] * len(prev_caches)
            args += list(prev_caches)
            aliases = {n_in + k: len(widths) + k for k in range(len(prev_caches))}
    return pl.pallas_call(
        functools.partial(_inprep_kernel, latent, len(aliases)),
        grid=(nblk,),
        in_specs=in_specs, out_specs=out_specs, out_shape=out_shape,
        input_output_aliases=aliases,
        compiler_params=_cparams(("arbitrary",)),
        name="inprep_lat" if latent else "inprep_ctx",
    )(*args)


def _pastkv_kernel(ckv_ref, kpe_ref, wukv_ref, o_ref):
    kv = _dot(ckv_ref[...].astype(BF16), wukv_ref[...])
    _store_kvm(o_ref, kv, kpe_ref[...].astype(BF16))


def _pastkv_call(l, ckv, kpe4, w_ukv):
    n = ckv.shape[0]
    return pl.pallas_call(
        _pastkv_kernel,
        grid=(n // TM,),
        in_specs=[pl.BlockSpec((TM, MLA_KV_LORA), lambda i: (i, 0)),
                  pl.BlockSpec((TM, LANES), lambda i: (i, 0)),
                  _layer_spec(l, (MLA_KV_LORA, 1024))],
        out_specs=pl.BlockSpec((TM, KVM_W), lambda i: (i, 0)),
        out_shape=jax.ShapeDtypeStruct((n, KVM_W), BF16),
        compiler_params=_cparams(("arbitrary",)),
        name="pastkv",
    )(ckv, kpe4, w_ukv)


def _softmax_pv(s, v_ones):
    m = jnp.max(s, axis=-1, keepdims=True)
    p = jnp.exp2(s - m).astype(BF16)
    o = _dot(p, v_ones)
    return o[:, 0:LANES] / o[:, LANES:2 * LANES]


def _attn_kernel(lam_init, n_past, qm_ref, kvm_ref, gq_ref, gkv_ref, dq_ref, dkv_ref, *refs):
    if n_past:
        past_refs, refs = refs[:3], refs[3:]
        lam_ref, gdiff_ref, o_ref = refs[:3]
        joined = refs[3:]

        @pl.when(pl.program_id(1) == 0)
        def _():
            for dst, past, new in zip(joined, past_refs, (kvm_ref, gkv_ref, dkv_ref)):
                dst[0:n_past, :] = past[...]
                dst[n_past:, :] = new[...]

        kvm_ref, gkv_ref, dkv_ref = joined
    else:
        lam_ref, gdiff_ref, o_ref = refs
    tq = qm_ref.shape[0]
    lane = _lane_iota((tq, LANES))
    low = lane < HALF_LANES
    zero = jnp.zeros((tq, LANES), BF16)

    for p in range(MLA_HEADS // 2):
        qn = qm_ref[:, p * LANES:(p + 1) * LANES]
        g = p // 2
        qpe = qm_ref[:, 512 + g * LANES:512 + (g + 1) * LANES]
        kk = kvm_ref[:, p * 256:(p + 1) * 256]
        vv = kvm_ref[:, 1024 + p * 256:1024 + (p + 1) * 256]
        outs = []
        for half in range(2):
            h = 2 * p + half
            slot = h % 4
            in_slot = (lane >= slot * MLA_ROPE) & (lane < (slot + 1) * MLA_ROPE)
            lhs = jnp.concatenate(
                [jnp.where(low if half == 0 else ~low, qn, zero),
                 jnp.where(in_slot, qpe, zero)], axis=1)
            outs.append(_softmax_pv(_dot_nt(lhs, kk), vv))
        o_ref[:, p * LANES:(p + 1) * LANES] = jnp.where(low, outs[0], outs[1]).astype(BF16)

    kk = gkv_ref[:, 0:LANES]
    vv = gkv_ref[:, LANES:3 * LANES]
    for g in range(GQA_HEADS // 2):
        qg = gq_ref[:, g * LANES:(g + 1) * LANES]
        o_lo = _softmax_pv(_dot_nt(jnp.where(low, qg, zero), kk), vv)
        o_hi = _softmax_pv(_dot_nt(jnp.where(low, zero, qg), kk), vv)
        o_ref[:, 512 + g * LANES:512 + (g + 1) * LANES] = jnp.where(low, o_lo, o_hi).astype(BF16)

    lp = lam_ref[...]
    lam = (jnp.exp(jnp.sum(lp[0:1] * lp[1:2], axis=-1, keepdims=True))
           - jnp.exp(jnp.sum(lp[2:3] * lp[3:4], axis=-1, keepdims=True)) + lam_init)
    for h in range(DIFF_HEADS):
        qh = dq_ref[:, h * LANES:(h + 1) * LANES]
        kk = dkv_ref[:, h * LANES:(h + 1) * LANES]
        vv = dkv_ref[:, 512 + h * 256:512 + (h + 1) * 256]
        a1 =_softmax_pv(_dot_nt(jnp.where(low, qh, zero), kk), vv)
        a2 = _softmax_pv(_dot_nt(jnp.where(low, zero, qh), kk), vv)
        od = _rms(a1 - lam * a2, gdiff_ref[...]) * (1.0 - lam_init)
        o_ref[:, 1024 + h * LANES:1024 + (h + 1) * LANES] = od.astype(BF16)


def _attn_call(l, lam_init, qm, kvm, gq, gkv, dq, dkv, lam_p, g_diff, n_b, t_len, past=None):
    tq = _query_block(t_len)
    nq = t_len // tq
    n_past = 0 if past is None else past[0].shape[0] // n_b

    def qspec(w):
        return pl.BlockSpec((tq, w), lambda b, i: (b * nq + i, 0))

    def kspec(w, rows=t_len):
        return pl.BlockSpec((rows, w), lambda b, i: (b, 0))

    in_specs = [qspec(768), kspec(KVM_W), qspec(512), kspec(GKV_W), qspec(512), kspec(DKV_W)]
    args = [qm, kvm, gq, gkv, dq, dkv]
    scratch = []
    if n_past:
        in_specs += [kspec(KVM_W, n_past), kspec(GKV_W, n_past), kspec(DKV_W, n_past)]
        args += list(past)
        scratch = [pltpu.VMEM((n_past + t_len, w), BF16) for w in (KVM_W, GKV_W, DKV_W)]
    in_specs += [_layer_spec(l, (4, DIFF_D)), _layer_spec(l, (1, DIFF_DV))]
    args += [lam_p, g_diff]
    return dict(kernel=functools.partial(_attn_kernel, lam_init, n_past), in_specs=in_specs, args=args,
                out_specs=[qspec(3 * BRANCH_W)],
                out_shape=[jax.ShapeDtypeStruct((n_b * t_len, 3 * BRANCH_W), BF16)],
                scratch=scratch, aliases={})


def _log_sigmoid(x):
    return jnp.minimum(x, 0.0) - jnp.log(1.0 + jnp.exp(-jnp.abs(x)))


def _log_gamma(dec_ref, l, d, h):
    return _log_sigmoid(jnp.full((1, 1), dec_ref[l, d, h], F32))


def _ret_kernel(latent, l, t_len, dec_ref, q_ref, k_ref, v_ref, rg_ref, gret_ref, *refs):
    if latent:
        s0_ref, o_ref = refs
    else:
        o_ref, st_ref = refs[-2:]
    tq = q_ref.shape[0]
    t0 = pl.program_id(1) * tq
    lane = _lane_iota((tq, LANES))
    low = lane < HALF_LANES
    zero = jnp.zeros((tq, LANES), BF16)
    t_idx = (t0 + lax.broadcasted_iota(jnp.int32, (tq, t_len), 0)).astype(F32)
    s_idx = lax.broadcasted_iota(jnp.int32, (tq, t_len), 1).astype(F32)
    dist = t_idx - s_idx
    past = dist >= 0
    diag = jnp.where(dist == 0, 1.0, 0.0)
    t_col = (t0 + lax.broadcasted_iota(jnp.int32, (tq, 1), 0)).astype(F32)

    def lg(d, h):
        return _log_gamma(dec_ref, l, d, h)

    for h in range(RET_HEADS):
        p, half = h // 2, h % 2
        qp = q_ref[:, p * LANES:(p + 1) * LANES]
        qm = jnp.where(low if half == 0 else ~low, qp, zero)
        kp = k_ref[:, p * LANES:(p + 1) * LANES]
        vh = v_ref[:, h * LANES:(h + 1) * LANES]
        lgf, lgb = lg(0, h), lg(1, h)
        dmask = jnp.exp(jnp.where(past, lgf, -lgb) * dist) + diag
        o = _dot((_dot_nt(qm, kp) * dmask).astype(BF16), vh)
        if latent:
            sf =s0_ref[0, p].astype(BF16)
            sb = s0_ref[1, p].astype(BF16)
            o = o + _dot(qm, sf) * jnp.exp(lgf * (t_col + 1.0))
            o = o + _dot(qm, sb) * jnp.exp(lgb * (float(t_len) - t_col))
        mu = jnp.mean(o, axis=-1, keepdims=True)
        oc = o - mu
        y = oc * lax.rsqrt(jnp.mean(oc * oc, axis=-1, keepdims=True) + EPS)
        y = y * gret_ref[:, h * LANES:(h + 1) * LANES]
        rg = rg_ref[:, h * LANES:(h + 1) * LANES].astype(F32)
        o_ref[:, h * LANES:(h + 1) * LANES] = (y * _silu(rg)).astype(BF16)

    if not latent:
        s_col = lax.broadcasted_iota(jnp.int32, (t_len, 1), 0).astype(F32)
        lane_t = _lane_iota((1, LANES)) < HALF_LANES
        for p in range(RET_HEADS // 2):
            kp = k_ref[:, p * LANES:(p + 1) * LANES].astype(F32)
            for d in range(2):
                lg_lane = jnp.where(lane_t, lg(d, 2 * p), lg(d, 2 * p + 1))
                expo = (float(t_len) - 1.0 - s_col) if d == 0 else s_col
                kdec_t = jnp.transpose(kp * jnp.exp(lg_lane * expo)).astype(BF16)
                for half in range(2):
                    h = 2 * p + half
                    st = _dot(kdec_t, v_ref[:, h * LANES:(h + 1) * LANES])
                    st = st[half * RET_DK:(half + 1) * RET_DK, :]
                    if l == 0:
                        st_ref[0, d, h] = st
                        for k in range(1, st_ref.shape[0]):
                            st_ref[k, d, h] = jnp.zeros_like(st)
                    else:
                        st_ref[d, h] = st


def _ret_call(latent, l, dec, ret, rg, g_ret, s0, n_b, t_len, prev_state=None):
    tq = _query_block(t_len)
    nq = t_len // tq
    aliases = {}
    assert latent or nq == 1
    in_specs = [pl.BlockSpec(memory_space=pltpu.SMEM),
                pl.BlockSpec((tq, 256), lambda b, i: (b * nq + i, 0)),
                pl.BlockSpec((t_len, 256), lambda b, i: (b, 1)),
                pl.BlockSpec((t_len, 512), lambda b, i: (b, 1)),
                pl.BlockSpec((tq, 512), lambda b, i: (b * nq + i, 0)),
                _layer_spec(l, (1, 512))]
    args = [dec, ret, ret, ret, rg, g_ret]
    out_specs = [pl.BlockSpec((tq, 512), lambda b, i: (b * nq + i, 0))]
    out_shape = [jax.ShapeDtypeStruct((n_b * t_len, 512), BF16)]
    if latent:
        in_specs.append(pl.BlockSpec((None, 2, 2, LANES, LANES), lambda b, i: (b, 0, 0, 0, 0)))
        args.append(s0)
    else:
        assert (prev_state is None) == (l == 0)
        lead, at = ((DEPTH,), 0) if l == 0 else ((None,), l)
        out_specs.append(pl.BlockSpec((None,) + lead + (2, RET_HEADS, RET_DK, RET_DV),
                                      lambda b, i: (b, at, 0, 0, 0, 0)))
        out_shape.append(jax.ShapeDtypeStruct((n_b, DEPTH, 2, RET_HEADS, RET_DK, RET_DV), F32))
        if prev_state is not None:
            aliases = {len(args): 1}
            in_specs.append(pl.BlockSpec(memory_space=pl.ANY))
            args.append(prev_state)
    return dict(kernel=functools.partial(_ret_kernel, latent, l, t_len), in_specs=in_specs, args=args,
                out_specs=out_specs, out_shape=out_shape, scratch=[], aliases=aliases)


def _mixers_call(attn, ret, n_b, nq, name):
    n_ai, n_ri = len(attn["args"]), len(ret["args"])
    n_ao, n_ro = len(attn["out_specs"]), len(ret["out_specs"])

    def kernel(*refs):
        a_in, r_in = refs[:n_ai], refs[n_ai:n_ai + n_ri]
        outs = refs[n_ai + n_ri:n_ai + n_ri + n_ao + n_ro]
        scratch = refs[n_ai + n_ri + n_ao + n_ro:]
        attn["kernel"](*a_in, *outs[:n_ao], *scratch)
        ret["kernel"](*r_in, *outs[n_ao:])

    aliases = {n_ai + k: n_ao + v for k, v in ret["aliases"].items()}
    return pl.pallas_call(
        kernel,
        grid=(n_b, nq),
        in_specs=attn["in_specs"] + ret["in_specs"],
        out_specs=attn["out_specs"] + ret["out_specs"],
        out_shape=attn["out_shape"] + ret["out_shape"],
        scratch_shapes=attn["scratch"],
        input_output_aliases=aliases,
        compiler_params=_cparams(("arbitrary", "arbitrary")),
        name=name,
    )(*attn["args"], *ret["args"])


def _merge_kernel(x_ref, mod_ref, br_ref, or_ref, gate_ref, wbr_ref, wout_ref, gpost_ref, o_ref):
    merged = None
    for n in range(N_BRANCH):
        if n < 2:
            b = br_ref[:, n * BRANCH_W:(n + 1) * BRANCH_W]
        elif n == 2:
            b = or_ref[...]
        else:
            b = br_ref[:, 2 * BRANCH_W:3 * BRANCH_W]
        t = gate_ref[:, n * D:(n + 1) * D].astype(F32) * _dot(b, wbr_ref[n])
        merged = t if merged is None else merged + t
    out = _dot(merged.astype(BF16), wout_ref[...])
    g1 = mod_ref[...][:, 2 * D:3 * D]
    o_ref[...] = x_ref[...] + g1 * _rms(out, gpost_ref[...])


def _merge_call(l, x, mod3, mod_row, br, o_r, gates, lw):
    n_tok = x.shape[0]

    def tok(w):
        return pl.BlockSpec((TM_MERGE, w), lambda i: (i, 0))

    return pl.pallas_call(
        _merge_kernel,
        grid=(n_tok // TM_MERGE,),
        in_specs=[tok(D), pl.BlockSpec((None, 1, 6 * D), lambda i: (mod_row(i * TM_MERGE), 0, 0)),
                  tok(3 * BRANCH_W), tok(BRANCH_W), tok(4 * D),
                  _layer_spec(l, (N_BRANCH, BRANCH_W, D)), _layer_spec(l, (D, D)),
                  _layer_spec(l, (1, D))],
        out_specs=tok(D),
        out_shape=jax.ShapeDtypeStruct((n_tok, D), F32),
        compiler_params=_cparams(("arbitrary",)),
        name="merge",
    )(x, mod3, br, o_r, gates, lw["w_br"], lw["w_out"], lw["g_post1"])


def _route(logits_t, bias):
    n = logits_t.shape[1]
    scores = jax.nn.sigmoid(logits_t)
    sel = scores + bias
    neg = -jnp.inf
    sub = lax.broadcasted_iota(jnp.int32, (GROUP_SIZE, n), 0)
    grp = []
    for g in range(N_GROUPS):
        blk = sel[g * GROUP_SIZE:(g + 1) * GROUP_SIZE]
        m1 = jnp.max(blk, axis=0, keepdims=True)
        i1 = jnp.min(jnp.where(blk == m1, sub, GROUP_SIZE), axis=0, keepdims=True)
        m2 = jnp.max(jnp.where(sub == i1, neg, blk), axis=0, keepdims=True)
        grp.append(m1 + m2)
    parts = []
    for g in range(N_GROUPS):
        beaten = jnp.zeros((1, n), jnp.int32)
        for o in range(N_GROUPS):
            if o == g:
                continue
            wins = (grp[o] > grp[g]) | (grp[o] == grp[g]) if o < g else (grp[o] > grp[g])
            beaten = beaten + wins.astype(jnp.int32)
        keep = beaten < TOPK_GROUPS
        parts.append(jnp.where(keep, sel[g * GROUP_SIZE:(g + 1) * GROUP_SIZE], neg))
    cur = jnp.concatenate(parts, axis=0)
    eidx = lax.broadcasted_iota(jnp.int32, (N_EXPERTS, n), 0)
    hits, ids, ws = [], [], []
    for _ in range(TOP_K):
        m = jnp.max(cur, axis=0, keepdims=True)
        i = jnp.min(jnp.where(cur == m, eidx, N_EXPERTS), axis=0, keepdims=True)
        hit = eidx == i
        hits.append(hit)
        ids.append(i)
        ws.append(jnp.sum(jnp.where(hit, scores, 0.0), axis=0, keepdims=True))
        cur = jnp.where(hit, neg, cur)
    wsum = ws[0] + ws[1] + ws[2] + ws[3]
    return hits, ids, [w / wsum * ROUTE_SCALE for w in ws]


U32 = jnp.uint32
HIGH16 = np.uint32(0xFFFF0000)


def _bf16_bits(v):
    return lax.bitcast_convert_type(v.astype(BF16).astype(F32), U32)


def _pack_rows(v):
    return (_bf16_bits(v[:, 0:D // 2]) >> 16) | _bf16_bits(v[:, D // 2:D])


def _unpack_rows(p):
    lo = lax.bitcast_convert_type(p << 16, F32)
    hi = lax.bitcast_convert_type(p & HIGH16, F32)
    return jnp.concatenate([lo, hi], axis=1)


def _moe_pre_kernel(x_ref, mod_ref, gpre_ref, wr_ref, br_ref, tri_ref,
                    hp_ref, comb_ref, dest_ref, te_ref, tv_ref, tn_ref, run_ref, eidx_ref, rank_ref):
    tm = x_ref.shape[0]
    step = pl.program_id(0)
    cols = pl.ds(pl.multiple_of(step * tm, tm), tm)

    @pl.when(pl.program_id(0) == 0)
    def _():
        run_ref[...] = jnp.zeros_like(run_ref)

    mod = mod_ref[...]
    sh2, sc2 = mod[:, 3 * D:4 * D], mod[:, 4 * D:5 * D]
    h = _rms(x_ref[...], gpre_ref[...]) * (1.0 + sc2) + sh2
    hp_ref[...] = _pack_rows(h)
    hb = h.astype(BF16)
    h_lo = (h - hb.astype(F32)).astype(BF16)
    wr = wr_ref[...]
    wr_hi = wr.astype(BF16)
    wr_lo = (wr - wr_hi.astype(F32)).astype(BF16)
    logits_t = _dot_nt(wr_hi, hb) + _dot_nt(wr_hi, h_lo) + _dot_nt(wr_lo, hb)
    hits, ids, ws = _route(logits_t, br_ref[...])

    picked = jnp.zeros((N_EXPERTS, tm), F32)
    for hit in hits:
        picked = jnp.where(hit, 1.0, picked)
    before = _dot(picked.astype(BF16), tri_ref[...]) + run_ref[:, 0:1]
    sub8 = lax.broadcasted_iota(jnp.int32, (8, tm), 0)
    comb8 = jnp.zeros((8, tm), F32)
    for k in range(TOP_K):
        rank = jnp.sum(jnp.where(hits[k], before, 0.0), axis=0, keepdims=True)
        eidx_ref[k:k + 1, cols] = ids[k]
        rank_ref[k:k + 1, cols] = rank.astype(jnp.int32)
        comb8 = jnp.where(sub8 == k, ws[k], comb8)
    comb_ref[...] = jnp.transpose(
        jnp.concatenate([comb8, jnp.zeros((LANES - 8, tm), F32)], axis=0))
    run_ref[...] = run_ref[...] + jnp.sum(picked, axis=1, keepdims=True)

    @pl.when(step == pl.num_programs(0) - 1)
    def _():
        _moe_plan(eidx_ref, rank_ref, run_ref, dest_ref, te_ref, tv_ref, tn_ref)


def _moe_pre_call(l, x, mod3, mod_row, lw):
    n_tok = x.shape[0]
    tm = TM_MOE_PRE
    tri = np.arange(tm)
    tri = jnp.asarray(tri[:, None] < tri[None, :], BF16)
    tiles = jax.ShapeDtypeStruct((1, LANES), jnp.int32)
    return pl.pallas_call(
        _moe_pre_kernel,
        grid=(n_tok // tm,),
        in_specs=[pl.BlockSpec((tm, D), lambda i: (i, 0)),
                  pl.BlockSpec((None, 1, 6 * D), lambda i: (mod_row(i * tm), 0, 0)),
                  _layer_spec(l, (1, D)), _layer_spec(l, (N_EXPERTS, D)),
                  _layer_spec(l, (N_EXPERTS, 1)), _const_spec((tm, tm))],
        out_specs=[pl.BlockSpec((tm, D // 2), lambda i: (i, 0)),
                   pl.BlockSpec((tm, LANES), lambda i: (i, 0)),
                   _const_spec((TOP_K, n_tok)), _const_spec((1, LANES)), _const_spec((1, LANES)),
                   _const_spec((8, LANES))],
        out_shape=[jax.ShapeDtypeStruct((n_tok, D // 2), U32),
                   jax.ShapeDtypeStruct((n_tok, LANES), F32),
                   jax.ShapeDtypeStruct((TOP_K, n_tok), jnp.int32), tiles, tiles,
                   jax.ShapeDtypeStruct((8, LANES), jnp.int32)],
        scratch_shapes=[pltpu.VMEM((N_EXPERTS, LANES), F32),
                        pltpu.VMEM((TOP_K, n_tok), jnp.int32), pltpu.VMEM((TOP_K, n_tok), jnp.int32)],
        compiler_params=_cparams(("arbitrary",)),
        name="moe_pre",
    )(x, mod3, lw["g_pre2"], lw["w_router_t"], lw["b_router"], tri)


def _moe_plan(eidx_ref, rank_ref, cnt_ref, dest_ref, te_ref, tv_ref, tn_ref):
    tm = eidx_ref.shape[1]
    cnt = cnt_ref[...]
    padded = jnp.ceil(cnt * (1.0 / TMX)) * TMX
    row = lax.broadcasted_iota(jnp.int32, cnt.shape, 0)
    incl = padded
    shift = 1
    while shift < N_EXPERTS:
        incl = incl + jnp.where(row >= shift, pltpu.roll(incl, shift, 0), 0.0)
        shift *= 2
    start = (incl - padded)[:, 0:1]
    end = incl[:, 0:1]
    erow = lax.broadcasted_iota(jnp.int32, (N_EXPERTS, tm), 0)
    for k in range(TOP_K):
        mine = erow == eidx_ref[k:k + 1, :]
        base = jnp.sum(jnp.where(mine, start, 0.0), axis=0, keepdims=True)
        dest_ref[k:k + 1, :] = rank_ref[k:k + 1, :] + base.astype(jnp.int32)

    tile0 = (_lane_iota((1, LANES)) * TMX).astype(F32)
    owner = jnp.sum(jnp.where(end <= tile0, 1.0, 0.0), axis=0, keepdims=True)
    owner = jnp.minimum(owner, N_EXPERTS - 1.0)
    erow_t = lax.broadcasted_iota(jnp.int32, (N_EXPERTS, LANES), 0).astype(F32)
    left = jnp.sum(jnp.where(erow_t == owner, cnt[:, 0:1] - (tile0 - start), 0.0),
                   axis=0, keepdims=True)
    te_ref[...] = owner.astype(jnp.int32)
    tv_ref[...] = jnp.clip(left, 0.0, float(TMX)).astype(jnp.int32)
    tn_ref[...] = jnp.full(tn_ref.shape, N_EXPERTS, jnp.int32)
    nxt = owner
    for k in range(W_SLOTS - 1):
        later = (erow_t > nxt) & (cnt[:, 0:1] > 0.0)
        nxt = jnp.min(jnp.where(later, erow_t, float(N_EXPERTS)), axis=0, keepdims=True)
        tn_ref[k:k + 1, :] = nxt.astype(jnp.int32)
    n_used = jnp.sum(jnp.where(left > 0.0, 1.0, 0.0), axis=1, keepdims=True)
    tn_ref[7:8, :] = jnp.minimum(tile0 * (1.0 / TMX), n_used - 1.0).astype(jnp.int32)


def _experts_kernel(l, te_ref, tv_ref, tn_ref, xs_ref, wgu_hbm, wdn_hbm, ys_ref,
                    wgu_f, wdn_f, wgu_b, wdn_b, sem, group_ref):
    j = pl.program_id(0)
    valid = tv_ref[j]
    expert = te_ref[j]

    def fetch(e, slot):
        return (pltpu.make_async_copy(wgu_hbm.at[l, e], wgu_f.at[slot], sem.at[slot, 0]),
                pltpu.make_async_copy(wdn_hbm.at[l, e], wdn_f.at[slot], sem.at[slot, 1]))

    def start_if_any(e, slot):
        @pl.when(e < N_EXPERTS)
        def _():
            for cp in fetch(e, slot):
                cp.start(priority=1)

    @pl.when(j == 0)
    def _():
        group_ref[0] = 0
        start_if_any(expert, 0)
        for k in range(W_SLOTS - 2):
            start_if_any(tn_ref[k, 0], k + 1)

    first_tile = (j == 0) | (expert != te_ref[jnp.maximum(j - 1, 0)])

    @pl.when(first_tile & (valid > 0))
    def _():
        group = group_ref[0]
        slot = lax.rem(group, W_SLOTS)
        for cp in fetch(expert, slot):
            cp.wait()
        wgu_b[...] = wgu_f[slot].astype(BF16)
        wdn_b[...] = wdn_f[slot].astype(BF16)
        start_if_any(tn_ref[W_SLOTS - 2, j], lax.rem(group + W_SLOTS - 1, W_SLOTS))
        group_ref[0] = group + 1

    @pl.when(valid > 0)
    def _():
        rows = lax.broadcasted_iota(jnp.int32, (TMX, D), 0)
        x = jnp.where(rows < valid, _unpack_rows(xs_ref[...]), 0.0).astype(BF16)
        gu = _dot(x, wgu_b[...])
        a = _silu(gu[:, 0:EXPERT_FF]) * gu[:, EXPERT_FF:2 * EXPERT_FF]
        ys_ref[...] = _pack_rows(_dot(a.astype(BF16), wdn_b[...]))


def _experts_call(l, xs, te, tv, tn, w_gu, w_dn):
    n_tiles = xs.shape[0] // TMX
    grid_spec = pltpu.PrefetchScalarGridSpec(
        num_scalar_prefetch=3,
        grid=(n_tiles,),
        in_specs=[pl.BlockSpec((TMX, D // 2), lambda j, te, tv, tn: (tn[7, j], 0)),
                  pl.BlockSpec(memory_space=pl.ANY), pl.BlockSpec(memory_space=pl.ANY)],
        out_specs=pl.BlockSpec((TMX, D // 2), lambda j, te, tv, tn: (tn[7, j], 0)),
        scratch_shapes=[pltpu.VMEM((W_SLOTS, D, 2 * EXPERT_FF), F32),
                        pltpu.VMEM((W_SLOTS, EXPERT_FF, D), F32),
                        pltpu.VMEM((D, 2 * EXPERT_FF), BF16), pltpu.VMEM((EXPERT_FF, D), BF16),
                        pltpu.SemaphoreType.DMA((W_SLOTS, 2)), pltpu.SMEM((1,), jnp.int32)])
    return pl.pallas_call(
        functools.partial(_experts_kernel, l),
        grid_spec=grid_spec,
        out_shape=jax.ShapeDtypeStruct(xs.shape, U32),
        compiler_params=_cparams(("arbitrary",)),
        name="moe_experts",
    )(te, tv, tn, xs, w_gu, w_dn)


def _moe_post_kernel(x_ref, mod_ref, hp_ref, yg_ref, comb_ref, wsgu_ref, wsdn_ref, gpost_ref,
                     o_ref):
    hb = _unpack_rows(hp_ref[...]).astype(BF16)
    sgu = _dot(hb, wsgu_ref[...])
    sa = _silu(sgu[:, 0:SHARED_FF]) * sgu[:, SHARED_FF:2 * SHARED_FF]
    acc = _dot(sa.astype(BF16), wsdn_ref[...])
    comb = comb_ref[...]
    for k in range(TOP_K):
        acc = acc + comb[:, k:k + 1] * _unpack_rows(yg_ref[k])
    g2 = mod_ref[...][:, 5 * D:6 * D]
    o_ref[...] = x_ref[...] + g2 * _rms(acc, gpost_ref[...])


def _moe_post_call(l, x, mod3, mod_row, hp, yg, comb, lw):
    n_tok = x.shape[0]
    tm = TM_MOE_PRE
    return pl.pallas_call(
        _moe_post_kernel,
        grid=(n_tok // tm,),
        in_specs=[pl.BlockSpec((tm, D), lambda i: (i, 0)),
                  pl.BlockSpec((None, 1, 6 * D), lambda i: (mod_row(i * tm), 0, 0)),
                  pl.BlockSpec((tm, D // 2), lambda i: (i, 0)),
                  pl.BlockSpec((TOP_K, tm, D // 2), lambda i: (0, i, 0)),
                  pl.BlockSpec((tm, LANES), lambda i: (i, 0)),
                  _layer_spec(l, (D, 2 * SHARED_FF)), _layer_spec(l, (SHARED_FF, D)),
                  _layer_spec(l, (1, D))],
        out_specs=pl.BlockSpec((tm, D), lambda i: (i, 0)),
        out_shape=jax.ShapeDtypeStruct((n_tok, D), F32),
        compiler_params=_cparams(("arbitrary",)),
        name="moe_post",
    )(x, mod3, hp, yg, comb, lw["w_sh_gu"], lw["w_sh_down"], lw["g_post2"])


def _moe_call(l, x, mod3, mod_row, lw):
    n_tok = x.shape[0]
    n_slots = -(-(TOP_K * n_tok + N_EXPERTS * (TMX - 1)) // TMX) * TMX
    assert n_slots // TMX <= LANES
    hp, comb, dest, te, tv, tn = _moe_pre_call(l, x, mod3, mod_row, lw)
    dest = dest.reshape(TOP_K * n_tok)
    xs = _sc_scatter_rows(hp, dest, n_slots)
    ys = _experts_call(l, xs, te[0], tv[0], tn, lw["w_exp_gu"], lw["w_exp_down"])
    yg = _sc_gather_rows(ys, dest).reshape(TOP_K, n_tok, D // 2)
    return _moe_post_call(l, x, mod3, mod_row, hp, yg, comb, lw)


SC_CORES, SC_SUBCORES = 2, 16
SC_WORKERS = SC_CORES * SC_SUBCORES


def _sc_gather_rows(table, idx, chunk=64):
    n_out, width = idx.shape[0], table.shape[1]
    per_worker = n_out // SC_WORKERS
    n_chunks = per_worker // chunk
    assert per_worker * SC_WORKERS == n_out and n_chunks * chunk == per_worker
    mesh = plsc.VectorSubcoreMesh(core_axis_name="c", subcore_axis_name="s",
                                  num_cores=SC_CORES, num_subcores=SC_SUBCORES)

    @functools.partial(
        pl.kernel, mesh=mesh,
        out_type=jax.ShapeDtypeStruct((n_out, width), table.dtype),
        scratch_types=[pltpu.VMEM((chunk,), jnp.int32), pltpu.VMEM((chunk, width), table.dtype),
                       pltpu.SemaphoreType.DMA],
        name="sc_gather")
    def gather(table_hbm, idx_hbm, out_hbm, idx_v, rows_v, sem):
        base = (lax.axis_index("s") * SC_CORES + lax.axis_index("c")) * per_worker

        @pl.loop(0, n_chunks)
        def _(j):
            off = base + j * chunk
            pltpu.sync_copy(idx_hbm.at[pl.ds(off, chunk)], idx_v)
            pltpu.async_copy(table_hbm.at[idx_v], rows_v, sem).wait()
            pltpu.sync_copy(rows_v, out_hbm.at[pl.ds(off, chunk)])

    return gather(table, idx)


def _sc_scatter_rows(rows, dest, n_slots, chunk=64):
    n_tok, width = rows.shape
    per_worker = n_tok // SC_WORKERS
    n_chunks = per_worker // chunk
    assert per_worker * SC_WORKERS == n_tok and n_chunks * chunk == per_worker
    mesh = plsc.VectorSubcoreMesh(core_axis_name="c", subcore_axis_name="s",
                                  num_cores=SC_CORES, num_subcores=SC_SUBCORES)

    @functools.partial(
        pl.kernel, mesh=mesh,
        out_type=jax.ShapeDtypeStruct((n_slots, width), rows.dtype),
        scratch_types=[pltpu.VMEM((chunk,), jnp.int32), pltpu.VMEM((chunk, width), rows.dtype)],
        name="sc_scatter")
    def scatter(rows_hbm, dest_hbm, out_hbm, idx_v, rows_v):
        base = (lax.axis_index("s") * SC_CORES + lax.axis_index("c")) * per_worker

        @pl.loop(0, n_chunks)
        def _(j):
            off = base + j * chunk
            pltpu.sync_copy(rows_hbm.at[pl.ds(off, chunk)], rows_v)
            for k in range(TOP_K):
                pltpu.sync_copy(dest_hbm.at[pl.ds(k * n_tok + off, chunk)], idx_v)
                pltpu.sync_copy(rows_v, out_hbm.at[idx_v])

    return scatter(rows, dest)


def _rope_tables(t_len):
    pos = np.arange(t_len)
    row, col = pos // GRID_W, pos % GRID_W

    def tab(r):
        half = r // 2
        freq = ROPE_BASE ** (-np.arange(half, dtype=np.float64) / half)
        sign = np.concatenate([-np.ones(half), np.ones(half)])
        cs, sn = [], []
        for p in (row, col):
            ang = p[:, None].astype(np.float64) * freq[None, :]
            cs.append(np.concatenate([np.cos(ang), np.cos(ang)], axis=1))
            sn.append(np.concatenate([np.sin(ang), np.sin(ang)], axis=1) * sign[None, :])
        return np.concatenate(cs, axis=1), np.concatenate(sn, axis=1)

    c64, s64 = tab(GQA_HD // 2)
    cpe, spe = tab(MLA_ROPE // 2)
    out = (np.tile(c64, (1, 2)), np.tile(s64, (1, 2)), np.tile(cpe, (1, 4)), np.tile(spe, (1, 4)))
    return tuple(jnp.asarray(a, F32) for a in out)


def _prep_weights(p):
    n_l = p["w_in"].shape[0]

    def row(name):
        return p[name].reshape(n_l, 1, -1)

    w_uq = p["w_mla_uq"].reshape(n_l, MLA_Q_LORA, MLA_HEADS, MLA_NOPE + MLA_ROPE)
    w_uq = jnp.concatenate([w_uq[..., :MLA_NOPE].reshape(n_l, MLA_Q_LORA, -1),
                            w_uq[..., MLA_NOPE:].reshape(n_l, MLA_Q_LORA, -1)], axis=-1)
    w_ukv = p["w_mla_ukv"].reshape(n_l, MLA_KV_LORA, MLA_HEADS, MLA_NOPE + MLA_V)
    w_ukv = jnp.concatenate([w_ukv[..., :MLA_NOPE].reshape(n_l, MLA_KV_LORA, -1),
                             w_ukv[..., MLA_NOPE:].reshape(n_l, MLA_KV_LORA, -1)], axis=-1)
    w_br = p["w_br"]
    w_br_gqa = w_br[:, 1].reshape(n_l, GQA_HEADS, GQA_HD, D)[:, jnp.array(GQA_ORDER)]
    w_br = jnp.concatenate([w_br[:, 0:1], w_br_gqa.reshape(n_l, 1, BRANCH_W, D), w_br[:, 2:4]], axis=1)
    blk = np.arange(512) // GQA_HD
    return {
        "g_pre1": row("g_pre1"), "g_post1": row("g_post1"),
        "g_pre2": row("g_pre2"), "g_post2": row("g_post2"),
        "w_in": jnp.swapaxes(p["w_in"], 1, 2).astype(BF16),
        "g_mla_q": row("g_mla_q"), "w_uq": w_uq.astype(BF16),
        "g_mla_kv": row("g_mla_kv"), "w_ukv": w_ukv.astype(BF16),
        "g_gqa_q": jnp.tile(p["g_gqa_q"], (1, GQA_HEADS)).reshape(n_l, 1, -1),
        "g_gqa_k": jnp.tile(p["g_gqa_k"], (1, GQA_KV_HEADS)).reshape(n_l, 1, -1),
        "bd": jnp.asarray(blk[:, None] == blk[None, :], BF16),
        "ret_decay": p["ret_decay"],
        "g_ret": row("g_ret"),
        "diff_lambda": p["diff_lambda"], "g_diff": row("g_diff"),
        "w_br": w_br.astype(BF16), "w_out": p["w_out"].astype(BF16),
        "w_router_t": jnp.swapaxes(p["w_router"], 1, 2),
        "b_router": p["b_router"].reshape(n_l, N_EXPERTS, 1),
        "w_exp_gu": p["w_exp_gu"], "w_exp_down": p["w_exp_down"],
        "w_sh_gu": p["w_sh_gu"].astype(BF16), "w_sh_down": p["w_sh_down"].astype(BF16),
    }


def _mixers(latent, l, x, mod3, mod_row, lw, n_b, t_len, tabs=None, past=None, s0=None,
            prev_cache=None):
    lam_init = 0.8 - 0.6 * math.exp(-0.3 * l)
    outs = _inprep_call(latent, l, x, mod3, mod_row, lw, tabs, t_len,
                        None if prev_cache is None else prev_cache[:6])
    qm, kvm, gq, gkv, dq, dkv, ret, rg, gates = outs[:9]
    attn = _attn_call(l, lam_init, qm, kvm, gq, gkv, dq, dkv, lw["diff_lambda"], lw["g_diff"],
                      n_b, t_len, past)
    retn = _ret_call(latent, l, lw["ret_decay"], ret, rg, lw["g_ret"], s0, n_b, t_len,
                     None if prev_cache is None else prev_cache[6])
    mixed = _mixers_call(attn, retn, n_b, t_len // _query_block(t_len),
                         "mix_lat" if latent else "mix_ctx")
    br, r = mixed[0], mixed[1:]
    y = _merge_call(l, x, mod3, mod_row, br, r[0], gates, lw)
    cache = None if latent else tuple(outs[9:]) + (r[1],)
    return y, cache


def kernel(x_prompt, x_sample, cache_mla_ckv, cache_mla_kpe, cache_gqa_k, cache_gqa_v, cache_diff_k, cache_diff_v, state_ret, c, c_ctx, w_mod, b_mod, g_pre1, g_post1, g_pre2, g_post2, w_in, g_mla_q, w_mla_uq, g_mla_kv, w_mla_ukv, g_gqa_q, g_gqa_k, ret_decay, g_ret, diff_lambda, g_diff, w_br, w_out, w_router, b_router, w_exp_gu, w_exp_down, w_sh_gu, w_sh_down):
    params = dict(w_in=w_in, g_pre1=g_pre1, g_post1=g_post1, g_pre2=g_pre2,
                  g_post2=g_post2, g_mla_q=g_mla_q, w_mla_uq=w_mla_uq,
                  g_mla_kv=g_mla_kv, w_mla_ukv=w_mla_ukv, g_gqa_q=g_gqa_q, g_gqa_k=g_gqa_k,
                  ret_decay=ret_decay, g_ret=g_ret, diff_lambda=diff_lambda, g_diff=g_diff,
                  w_br=w_br, w_out=w_out, w_router=w_router, b_router=b_router,
                  w_exp_gu=w_exp_gu, w_exp_down=w_exp_down, w_sh_gu=w_sh_gu, w_sh_down=w_sh_down)
    n_bc, t_c, _ = x_prompt.shape
    n_bl, t_l, _ = x_sample.shape
    p_len = cache_mla_ckv.shape[2]
    tabs = _rope_tables(t_l)
    n_cond = 8
    cond = jnp.concatenate([c_ctx[None, :], c, jnp.zeros((n_cond - 1 - n_bl, D), F32)], axis=0)
    assert t_l % TM_MERGE == 0 and (t_c * n_bc) % TM_MERGE == 0
    assert t_l % TM_MOE_PRE == 0 and (t_c * n_bc) % TM_MOE_PRE == 0

    yp = x_prompt.reshape(n_bc * t_c, D)
    ys = x_sample.reshape(n_bl * t_l, D)
    cache = None
    lw = _prep_weights(params)
    mod_all = _mod_call(cond, w_mod, b_mod)
    for l in range(DEPTH):
        mod3 = mod_all[l].reshape(n_cond, 1, 6 * D)
        yp, cache = _mixers(False, l, yp, mod3, lambda i: 0, lw, n_bc, t_c, prev_cache=cache)
        yp = _moe_call(l, yp, mod3, lambda i: 0, lw)
        past_kvm = _pastkv_call(l, cache_mla_ckv[:, l].reshape(n_bl * p_len, -1),
                                jnp.tile(cache_mla_kpe[:, l].reshape(n_bl * p_len, -1), (1, 4)),
                                lw["w_ukv"])
        past_gkv = jnp.concatenate([cache_gqa_k[:, l].reshape(n_bl * p_len, -1),
                                    cache_gqa_v[:, l].reshape(n_bl * p_len, -1),
                                    jnp.ones((n_bl * p_len, LANES), F32)], axis=-1).astype(BF16)
        past_dv = jnp.concatenate([cache_diff_v[:, l], jnp.ones_like(cache_diff_v[:, l])], axis=-1)
        past_dkv = jnp.concatenate([cache_diff_k[:, l].reshape(n_bl * p_len, -1),
                                    past_dv.reshape(n_bl * p_len, -1)], axis=-1).astype(BF16)
        s0 = state_ret[:, l].reshape(n_bl, 2, RET_HEADS // 2, 2 * RET_DK, RET_DV)
        ys, _ = _mixers(True, l, ys, mod3, lambda t: 1 + t // t_l, lw, n_bl, t_l, tabs=tabs,
                        past=(past_kvm, past_gkv, past_dkv), s0=s0)
        ys = _moe_call(l, ys, mod3, lambda t: 1 + t // t_l, lw)

    ckv, kpe, gk_t, gv_t, dk_t, dv, ret_state = cache

    def time_minor(a, shape):
        a = a.reshape((n_bc, DEPTH) + shape + (t_c,))
        return jnp.transpose(a, (0, 1, a.ndim - 1) + tuple(range(2, a.ndim - 1)))

    return (yp.reshape(n_bc, t_c, D), ys.reshape(n_bl, t_l, D), ckv, kpe,
            time_minor(gk_t, (GQA_KV_HEADS, GQA_HD)), time_minor(gv_t, (GQA_KV_HEADS, GQA_HD)),
            time_minor(dk_t, (DIFF_HEADS, 2, DIFF_D)),
            dv.reshape(n_bc, DEPTH, t_c, DIFF_HEADS, DIFF_DV), ret_state)
```
